```python
import jax, jax.numpy as jnp
from jax import lax
import numpy as np

D_MODEL = 1024
BATCH = 8
SEQ = 8192
DEPTH = 2

MIX_WIDTH = D_MODEL
N_MIXERS = 4
GROUP_WIDTH = MIX_WIDTH // N_MIXERS
HEADS_PER_GROUP = 4
HEAD_DIM = GROUP_WIDTH // HEADS_PER_GROUP
SGU_CHUNK = 128
SC_WIDTH = 3
DN_CONV_WIDTH = 4
DN_CHUNK = 64
GLA_CHUNK = 64
GLA_GATE_RANK = 16
GLA_GATE_TEMP = 16.0
D_FF = ((8 * D_MODEL // 3 + 255) // 256) * 256
EPS = 1e-6

_G = GROUP_WIDTH
_H = HEADS_PER_GROUP
IN_SPLITS = (_G, _G,
             _G, _G, _G,
             _G, _G, _G, _H, _H, _G,
             _G, _G, _G, GLA_GATE_RANK, _G)
IN_COLS = sum(IN_SPLITS)

kernel_name = "hybrid_sgu_shortconv_gdn_gla"


def rmsnorm(x, w):
    xf = x.astype(jnp.float32)
    y = xf * lax.rsqrt(jnp.mean(xf * xf, axis=-1, keepdims=True) + EPS)
    return (y * w.astype(jnp.float32)).astype(x.dtype)


def layernorm(x, w, b):
    xf = x.astype(jnp.float32)
    mu = jnp.mean(xf, axis=-1, keepdims=True)
    var = jnp.mean(jnp.square(xf - mu), axis=-1, keepdims=True)
    y = (xf - mu) * lax.rsqrt(var + EPS)
    return (y * w.astype(jnp.float32) + b.astype(jnp.float32)).astype(x.dtype)


def l2norm(t):
    return t * lax.rsqrt(jnp.sum(t * t, axis=-1, keepdims=True) + EPS)


def causal_dwconv(x, w):
    k_width, ch = w.shape
    return lax.conv_general_dilated(
        x, w[:, None, :].astype(x.dtype), window_strides=(1,),
        padding=[(k_width - 1, 0)], dimension_numbers=('NWC', 'WIO', 'NWC'),
        feature_group_count=ch)


def to_heads(t):
    b, s, _ = t.shape
    return t.reshape(b, s, HEADS_PER_GROUP, HEAD_DIM).astype(jnp.float32)


def to_chunks(t, c):
    b, s = t.shape[:2]
    t = t.reshape(b, s // c, c, *t.shape[2:])
    return jnp.moveaxis(t, 3, 1)


def from_chunks(t):
    b, h, n, c, d = t.shape
    return jnp.moveaxis(t, 1, 3).reshape(b, n * c, h, d)


def sgu_mixer(u, v, ln_w, ln_b, w_s, b_s):
    bsz, s, g = u.shape
    n = s // SGU_CHUNK
    u = jax.nn.gelu(u)
    v = layernorm(jax.nn.gelu(v), ln_w, ln_b)
    vc = v.reshape(bsz, n, SGU_CHUNK, HEADS_PER_GROUP, HEAD_DIM)
    mask = jnp.tril(jnp.ones((SGU_CHUNK, SGU_CHUNK), dtype=bool))
    ws = jnp.where(mask, w_s, 0.0).astype(v.dtype)
    mixed = jnp.einsum('hts,bnshd->bnthd', ws, vc) + b_s.T.astype(v.dtype)[None, None, :, :, None]
    return u * mixed.reshape(bsz, s, g)


def short_conv_mixer(gate_b, gate_c, h, w_conv):
    return gate_b * causal_dwconv(gate_c * h, w_conv)


def chunk_gated_delta_rule(q, k, v, g, beta):
    bsz, s, h, dk = q.shape
    dv = v.shape[-1]
    c = DN_CHUNK
    q, k, v = to_chunks(q, c), to_chunks(k, c), to_chunks(v, c)
    g, beta = to_chunks(g, c), to_chunks(beta, c)
    q = q * dk ** -0.5
    gc = jnp.cumsum(g, axis=-1)
    causal = jnp.tril(jnp.ones((c, c), dtype=bool))
    strict = jnp.tril(jnp.ones((c, c), dtype=bool), k=-1)
    decay = jnp.exp(jnp.where(causal, gc[..., :, None] - gc[..., None, :], -jnp.inf))
    kb = k * beta[..., None]
    low = jnp.where(strict, jnp.einsum('bhncd,bhnsd->bhncs', kb, k) * decay, 0.0)
    rhs = jnp.concatenate([v * beta[..., None], kb * jnp.exp(gc)[..., None]], axis=-1)
    sol = lax.linalg.triangular_solve(low, rhs, left_side=True, lower=True, unit_diagonal=True)
    u, w = sol[..., :dv], sol[..., dv:]
    attn = jnp.einsum('bhncd,bhnsd->bhncs', q, k) * decay
    qg = q * jnp.exp(gc)[..., None]
    k_dec = k * jnp.exp(gc[..., -1:] - gc)[..., None]
    chunk_dec = jnp.exp(gc[..., -1])

    def step(state, inp):
        qg_n, w_n, u_n, attn_n, kd_n, dec_n = inp
        v_new = u_n - jnp.einsum('bhcd,bhde->bhce', w_n, state)
        o = jnp.einsum('bhcd,bhde->bhce', qg_n, state) + jnp.einsum('bhcs,bhse->bhce', attn_n, v_new)
        state = state * dec_n[..., None, None] + jnp.einsum('bhcd,bhce->bhde', kd_n, v_new)
        return state, o

    s0 = jnp.zeros((bsz, h, dk, dv), jnp.float32)
    xs = tuple(jnp.moveaxis(t, 2, 0) for t in (qg, w, u, attn, k_dec, chunk_dec))
    _, o = lax.scan(step, s0, xs)
    return from_chunks(jnp.moveaxis(o, 0, 2))


def deltanet_mixer(q, k, v, a, b, z, conv_w, a_log, dt_bias, norm_w):
    bsz, s, _ = q.shape
    qkv = jax.nn.silu(causal_dwconv(jnp.concatenate([q, k, v], axis=-1), conv_w))
    q, k, v = jnp.split(qkv, 3, axis=-1)
    q, k, v = l2norm(to_heads(q)), l2norm(to_heads(k)), to_heads(v)
    g = -jnp.exp(a_log.astype(jnp.float32)) * jax.nn.softplus(a.astype(jnp.float32) + dt_bias.astype(jnp.float32))
    beta = jax.nn.sigmoid(b.astype(jnp.float32))
    o = chunk_gated_delta_rule(q, k, v, g, beta)
    o = rmsnorm(o, norm_w) * jax.nn.silu(to_heads(z))
    return o.reshape(bsz, s, GROUP_WIDTH).astype(z.dtype)


def chunk_gla(q, k, v, log_a):
    bsz, s, h, dk = q.shape
    dv = v.shape[-1]
    c = GLA_CHUNK
    q, k, v, log_a = (to_chunks(t, c) for t in (q, k, v, log_a))
    q = q * dk ** -0.5
    gcum = jnp.cumsum(log_a, axis=3)
    g_mid = gcum[:, :, :, c // 2:c // 2 + 1, :]
    qa = q * jnp.exp(gcum - g_mid)
    ka = k * jnp.exp(g_mid - gcum)
    causal = jnp.tril(jnp.ones((c, c), dtype=bool))
    attn = jnp.where(causal, jnp.einsum('bhncd,bhnsd->bhncs', qa, ka), 0.0)
    o_intra = jnp.einsum('bhncs,bhnse->bhnce', attn, v)
    qg = q * jnp.exp(gcum)
    k_last = k * jnp.exp(gcum[:, :, :, -1:, :] - gcum)
    chunk_dec = jnp.exp(gcum[:, :, :, -1, :])

    def step(state, inp):
        qg_n, kl_n, v_n, dec_n = inp
        o = jnp.einsum('bhcd,bhde->bhce', qg_n, state)
        state = state * dec_n[..., :, None] + jnp.einsum('bhcd,bhce->bhde', kl_n, v_n)
        return state, o

    s0 = jnp.zeros((bsz, h, dk, dv), jnp.float32)
    xs = tuple(jnp.moveaxis(t, 2, 0) for t in (qg, k_last, v, chunk_dec))
    _, o_inter = lax.scan(step, s0, xs)
    return from_chunks(o_intra + jnp.moveaxis(o_inter, 0, 2))


def gla_mixer(q, k, v, g_lr, z, w_gate2, gate_bias, norm_w):
    bsz, s, _ = q.shape
    pre = jnp.einsum('btr,rg->btg', g_lr, w_gate2.astype(g_lr.dtype)).astype(jnp.float32) + gate_bias.astype(jnp.float32)
    log_a = jax.nn.log_sigmoid(pre) / GLA_GATE_TEMP
    o = chunk_gla(to_heads(q), to_heads(k), to_heads(v), to_heads(log_a))
    o = rmsnorm(o, norm_w) * jax.nn.silu(to_heads(z))
    return o.reshape(bsz, s, GROUP_WIDTH).astype(z.dtype)


def swiglu(h, w_gate_up, w_down):
    gu = h @ w_gate_up
    gate, up = jnp.split(gu, 2, axis=-1)
    return (jax.nn.silu(gate) * up) @ w_down


def _fwd_setup_inputs(seed: int = 0) -> dict:
    key = jax.random.key(seed)
    ks = jax.random.split(key, 24)
    f32 = jnp.float32

    def nrm(k, shape, scale):
        return jax.random.normal(k, shape, f32) * scale

    def gain(k, shape):
        return 1.0 + 0.02 * jax.random.normal(k, shape, f32)

    dt = jnp.exp(jax.random.uniform(ks[11], (DEPTH, HEADS_PER_GROUP), f32, np.log(1e-3), np.log(1e-1)))
    return {
        "x": jax.random.normal(ks[0], (BATCH, SEQ, D_MODEL), f32),
        "norm1_w": gain(ks[1], (DEPTH, D_MODEL)),
        "w_in": nrm(ks[2], (DEPTH, D_MODEL, IN_COLS), D_MODEL ** -0.5),
        "sgu_ln_w": gain(ks[3], (DEPTH, GROUP_WIDTH)),
        "sgu_ln_b": nrm(ks[4], (DEPTH, GROUP_WIDTH), 0.02),
        "sgu_w_spatial": nrm(ks[5], (DEPTH, HEADS_PER_GROUP, SGU_CHUNK, SGU_CHUNK), SGU_CHUNK ** -0.5),
        "sgu_b_spatial": gain(ks[6], (DEPTH, HEADS_PER_GROUP, SGU_CHUNK)),
        "sc_conv_w": nrm(ks[7], (DEPTH, SC_WIDTH, GROUP_WIDTH), SC_WIDTH ** -0.5),
        "dn_conv_w": nrm(ks[8], (DEPTH, DN_CONV_WIDTH, 3 * GROUP_WIDTH), DN_CONV_WIDTH ** -0.5),
        "dn_a_log": jnp.log(jax.random.uniform(ks[9], (DEPTH, HEADS_PER_GROUP), f32, 1.0, 16.0)),
        "dn_dt_bias": dt + jnp.log(-jnp.expm1(-dt)),
        "dn_norm_w": gain(ks[10], (DEPTH, HEAD_DIM)),
        "gla_w_gate2": nrm(ks[12], (DEPTH, GLA_GATE_RANK, GROUP_WIDTH), GLA_GATE_RANK ** -0.5),
        "gla_gate_bias": nrm(ks[13], (DEPTH, GROUP_WIDTH), 0.1),
        "gla_norm_w": gain(ks[14], (DEPTH, HEAD_DIM)),
        "w_out": nrm(ks[15], (DEPTH, MIX_WIDTH, D_MODEL), MIX_WIDTH ** -0.5),
        "norm2_w": gain(ks[16], (DEPTH, D_MODEL)),
        "w_gate_up": nrm(ks[17], (DEPTH, D_MODEL, 2 * D_FF), D_MODEL ** -0.5),
        "w_down": nrm(ks[18], (DEPTH, D_FF, D_MODEL), D_FF ** -0.5),
        "final_norm_w": gain(ks[19], (D_MODEL,)),
    }


def _fwd_reference(x, norm1_w, w_in, sgu_ln_w, sgu_ln_b, sgu_w_spatial, sgu_b_spatial, sc_conv_w,
              dn_conv_w, dn_a_log, dn_dt_bias, dn_norm_w, gla_w_gate2, gla_gate_bias, gla_norm_w,
              w_out, norm2_w, w_gate_up, w_down, final_norm_w):
    split_idx = [int(i) for i in np.cumsum(IN_SPLITS)[:-1]]
    for l in range(DEPTH):
        h = rmsnorm(x, norm1_w[l])
        p = h @ w_in[l]
        (a_u, a_v, b_b, b_c, b_h, c_q, c_k, c_v, c_a, c_b, c_z,
         d_q, d_k, d_v, d_g, d_z) = jnp.split(p, split_idx, axis=-1)
        y_a = sgu_mixer(a_u, a_v, sgu_ln_w[l], sgu_ln_b[l], sgu_w_spatial[l], sgu_b_spatial[l])
        y_b = short_conv_mixer(b_b, b_c, b_h, sc_conv_w[l])
        y_c = deltanet_mixer(c_q, c_k, c_v, c_a, c_b, c_z, dn_conv_w[l], dn_a_log[l], dn_dt_bias[l], dn_norm_w[l])
        y_d = gla_mixer(d_q, d_k, d_v, d_g, d_z, gla_w_gate2[l], gla_gate_bias[l], gla_norm_w[l])
        mix = jnp.concatenate([y_a, y_b.astype(x.dtype), y_c, y_d], axis=-1)
        x = x + (mix @ w_out[l]).astype(x.dtype)
        x = x + swiglu(rmsnorm(x, norm2_w[l]), w_gate_up[l], w_down[l]).astype(x.dtype)
    return rmsnorm(x, final_norm_w)


import jax as _jax
import jax.numpy as _jnp

TWIN_FORMAT = 'train_step'
FWD_PARAMS = ['x', 'norm1_w', 'w_in', 'sgu_ln_w', 'sgu_ln_b', 'sgu_w_spatial', 'sgu_b_spatial', 'sc_conv_w', 'dn_conv_w', 'dn_a_log', 'dn_dt_bias', 'dn_norm_w', 'gla_w_gate2', 'gla_gate_bias', 'gla_norm_w', 'w_out', 'norm2_w', 'w_gate_up', 'w_down', 'final_norm_w']
TWIN_WEIGHTS = ['norm1_w', 'w_in', 'sgu_ln_w', 'sgu_ln_b', 'sgu_w_spatial', 'sgu_b_spatial', 'sc_conv_w', 'dn_conv_w', 'dn_a_log', 'dn_dt_bias', 'dn_norm_w', 'gla_w_gate2', 'gla_gate_bias', 'gla_norm_w', 'w_out', 'norm2_w', 'w_gate_up', 'w_down', 'final_norm_w']
TWIN_DIFF_INPUT = 'x'
TWIN_INPUTS = ['x', 'norm1_w', 'w_in', 'sgu_ln_w', 'sgu_ln_b', 'sgu_w_spatial', 'sgu_b_spatial', 'sc_conv_w', 'dn_conv_w', 'dn_a_log', 'dn_dt_bias', 'dn_norm_w', 'gla_w_gate2', 'gla_gate_bias', 'gla_norm_w', 'w_out', 'norm2_w', 'w_gate_up', 'w_down', 'final_norm_w', 'loss_target', 'm_norm1_w', 'm_w_in', 'm_sgu_ln_w', 'm_sgu_ln_b', 'm_sgu_w_spatial', 'm_sgu_b_spatial', 'm_sc_conv_w', 'm_dn_conv_w', 'm_dn_a_log', 'm_dn_dt_bias', 'm_dn_norm_w', 'm_gla_w_gate2', 'm_gla_gate_bias', 'm_gla_norm_w', 'm_w_out', 'm_norm2_w', 'm_w_gate_up', 'm_w_down', 'm_final_norm_w', 'v_norm1_w', 'v_w_in', 'v_sgu_ln_w', 'v_sgu_ln_b', 'v_sgu_w_spatial', 'v_sgu_b_spatial', 'v_sc_conv_w', 'v_dn_conv_w', 'v_dn_a_log', 'v_dn_dt_bias', 'v_dn_norm_w', 'v_gla_w_gate2', 'v_gla_gate_bias', 'v_gla_norm_w', 'v_w_out', 'v_norm2_w', 'v_w_gate_up', 'v_w_down', 'v_final_norm_w']
TWIN_OUTPUTS = ['loss', 'grad_x', 'grad_norm1_w', 'grad_w_in', 'grad_sgu_ln_w', 'grad_sgu_ln_b', 'grad_sgu_w_spatial', 'grad_sgu_b_spatial', 'grad_sc_conv_w', 'grad_dn_conv_w', 'grad_dn_a_log', 'grad_dn_dt_bias', 'grad_dn_norm_w', 'grad_gla_w_gate2', 'grad_gla_gate_bias', 'grad_gla_norm_w', 'grad_w_out', 'grad_norm2_w', 'grad_w_gate_up', 'grad_w_down', 'grad_final_norm_w', 'delta_norm1_w', 'delta_w_in', 'delta_sgu_ln_w', 'delta_sgu_ln_b', 'delta_sgu_w_spatial', 'delta_sgu_b_spatial', 'delta_sc_conv_w', 'delta_dn_conv_w', 'delta_dn_a_log', 'delta_dn_dt_bias', 'delta_dn_norm_w', 'delta_gla_w_gate2', 'delta_gla_gate_bias', 'delta_gla_norm_w', 'delta_w_out', 'delta_norm2_w', 'delta_w_gate_up', 'delta_w_down', 'delta_final_norm_w', 'new_m_norm1_w', 'new_m_w_in', 'new_m_sgu_ln_w', 'new_m_sgu_ln_b', 'new_m_sgu_w_spatial', 'new_m_sgu_b_spatial', 'new_m_sc_conv_w', 'new_m_dn_conv_w', 'new_m_dn_a_log', 'new_m_dn_dt_bias', 'new_m_dn_norm_w', 'new_m_gla_w_gate2', 'new_m_gla_gate_bias', 'new_m_gla_norm_w', 'new_m_w_out', 'new_m_norm2_w', 'new_m_w_gate_up', 'new_m_w_down', 'new_m_final_norm_w', 'new_v_norm1_w', 'new_v_w_in', 'new_v_sgu_ln_w', 'new_v_sgu_ln_b', 'new_v_sgu_w_spatial', 'new_v_sgu_b_spatial', 'new_v_sc_conv_w', 'new_v_dn_conv_w', 'new_v_dn_a_log', 'new_v_dn_dt_bias', 'new_v_dn_norm_w', 'new_v_gla_w_gate2', 'new_v_gla_gate_bias', 'new_v_gla_norm_w', 'new_v_w_out', 'new_v_norm2_w', 'new_v_w_gate_up', 'new_v_w_down', 'new_v_final_norm_w']
TWIN_LEAF_KINDS = {'loss': 'loss', 'grad_x': 'grad_x', 'grad_norm1_w': 'grad_w', 'grad_w_in': 'grad_w', 'grad_sgu_ln_w': 'grad_w', 'grad_sgu_ln_b': 'grad_w', 'grad_sgu_w_spatial': 'grad_w', 'grad_sgu_b_spatial': 'grad_w', 'grad_sc_conv_w': 'grad_w', 'grad_dn_conv_w': 'grad_w', 'grad_dn_a_log': 'grad_w', 'grad_dn_dt_bias': 'grad_w', 'grad_dn_norm_w': 'grad_w', 'grad_gla_w_gate2': 'grad_w', 'grad_gla_gate_bias': 'grad_w', 'grad_gla_norm_w': 'grad_w', 'grad_w_out': 'grad_w', 'grad_norm2_w': 'grad_w', 'grad_w_gate_up': 'grad_w', 'grad_w_down': 'grad_w', 'grad_final_norm_w': 'grad_w', 'delta_norm1_w': 'delta_w', 'delta_w_in': 'delta_w', 'delta_sgu_ln_w': 'delta_w', 'delta_sgu_ln_b': 'delta_w', 'delta_sgu_w_spatial': 'delta_w', 'delta_sgu_b_spatial': 'delta_w', 'delta_sc_conv_w': 'delta_w', 'delta_dn_conv_w': 'delta_w', 'delta_dn_a_log': 'delta_w', 'delta_dn_dt_bias': 'delta_w', 'delta_dn_norm_w': 'delta_w', 'delta_gla_w_gate2': 'delta_w', 'delta_gla_gate_bias': 'delta_w', 'delta_gla_norm_w': 'delta_w', 'delta_w_out': 'delta_w', 'delta_norm2_w': 'delta_w', 'delta_w_gate_up': 'delta_w', 'delta_w_down': 'delta_w', 'delta_final_norm_w': 'delta_w', 'new_m_norm1_w': 'new_m', 'new_m_w_in': 'new_m', 'new_m_sgu_ln_w': 'new_m', 'new_m_sgu_ln_b': 'new_m', 'new_m_sgu_w_spatial': 'new_m', 'new_m_sgu_b_spatial': 'new_m', 'new_m_sc_conv_w': 'new_m', 'new_m_dn_conv_w': 'new_m', 'new_m_dn_a_log': 'new_m', 'new_m_dn_dt_bias': 'new_m', 'new_m_dn_norm_w': 'new_m', 'new_m_gla_w_gate2': 'new_m', 'new_m_gla_gate_bias': 'new_m', 'new_m_gla_norm_w': 'new_m', 'new_m_w_out': 'new_m', 'new_m_norm2_w': 'new_m', 'new_m_w_gate_up': 'new_m', 'new_m_w_down': 'new_m', 'new_m_final_norm_w': 'new_m', 'new_v_norm1_w': 'new_v', 'new_v_w_in': 'new_v', 'new_v_sgu_ln_w': 'new_v', 'new_v_sgu_ln_b': 'new_v', 'new_v_sgu_w_spatial': 'new_v', 'new_v_sgu_b_spatial': 'new_v', 'new_v_sc_conv_w': 'new_v', 'new_v_dn_conv_w': 'new_v', 'new_v_dn_a_log': 'new_v', 'new_v_dn_dt_bias': 'new_v', 'new_v_dn_norm_w': 'new_v', 'new_v_gla_w_gate2': 'new_v', 'new_v_gla_gate_bias': 'new_v', 'new_v_gla_norm_w': 'new_v', 'new_v_w_out': 'new_v', 'new_v_norm2_w': 'new_v', 'new_v_w_gate_up': 'new_v', 'new_v_w_down': 'new_v', 'new_v_final_norm_w': 'new_v'}


def _forward(args):
    return _fwd_reference(*[args[k] for k in FWD_PARAMS])


def _output_shape():
    def fwd():
        inp = _fwd_setup_inputs(0)
        return _fwd_reference(*[inp[k] for k in FWD_PARAMS])
    out = _jax.eval_shape(fwd)
    return out.shape, out.dtype

N_MICROBATCH = 1
ADAM_LR = 0.001
ADAM_B1 = 0.9
ADAM_B2 = 0.999
ADAM_EPS = 1e-08
ADAM_WD = 0.01
ADAM_STEP = 10
PER_EXAMPLE_BATCH_AXIS = {'x': 0, 'loss_target': 0}
SHARED_INPUTS = []
_WEIGHT_DTYPES = {'norm1_w': _jnp.float32, 'w_in': _jnp.float32, 'sgu_ln_w': _jnp.float32, 'sgu_ln_b': _jnp.float32, 'sgu_w_spatial': _jnp.float32, 'sgu_b_spatial': _jnp.float32, 'sc_conv_w': _jnp.float32, 'dn_conv_w': _jnp.float32, 'dn_a_log': _jnp.float32, 'dn_dt_bias': _jnp.float32, 'dn_norm_w': _jnp.float32, 'gla_w_gate2': _jnp.float32, 'gla_gate_bias': _jnp.float32, 'gla_norm_w': _jnp.float32, 'w_out': _jnp.float32, 'norm2_w': _jnp.float32, 'w_gate_up': _jnp.float32, 'w_down': _jnp.float32, 'final_norm_w': _jnp.float32}
MOMENT_SCALE = {'norm1_w': 3.059214e-01, 'w_in': 1.656265e-01, 'sgu_ln_w': 1.186312e-01, 'sgu_ln_b': 1.024831e-01, 'sgu_w_spatial': 7.449679e-02, 'sgu_b_spatial': 1.121974e-01, 'sc_conv_w': 2.318313e-01, 'dn_conv_w': 1.304802e-01, 'dn_a_log': 6.684942e-01, 'dn_dt_bias': 6.574577e-01, 'dn_norm_w': 2.808634e-01, 'gla_w_gate2': 1.643680e-02, 'gla_gate_bias': 7.288316e-02, 'gla_norm_w': 2.904787e-01, 'w_out': 1.748841e-01, 'norm2_w': 1.677771e-01, 'w_gate_up': 7.116208e-02, 'w_down': 1.159333e-01, 'final_norm_w': 6.403054e+01}


def _to_microbatches(a, axis):
    t = _jnp.moveaxis(a, axis, 0)
    t = t.reshape((N_MICROBATCH, t.shape[0] // N_MICROBATCH) + t.shape[1:])
    return _jnp.moveaxis(t, 1, axis + 1)


def setup_inputs(seed: int = 0) -> dict:
    inp = _fwd_setup_inputs(seed)
    key = _jax.random.fold_in(_jax.random.key(seed), 7919)
    shape, _ = _output_shape()
    out = dict(inp)
    out["loss_target"] = _jax.random.normal(_jax.random.fold_in(key, 0), shape, _jnp.float32)
    for i, name in enumerate(TWIN_WEIGHTS):
        w = inp[name].astype(_jnp.float32)
        if MOMENT_SCALE is None:
            s = _jnp.sqrt(_jnp.mean(_jnp.square(w)) + 1e-30)
        else:
            s = MOMENT_SCALE[name]
        km, kv = _jax.random.split(_jax.random.fold_in(key, i + 1))
        out[name] = w
        out["m_" + name] = s * _jax.random.normal(km, w.shape, _jnp.float32)
        out["v_" + name] = (s * s) * _jax.random.uniform(kv, w.shape, _jnp.float32, 0.5, 1.5)
    if N_MICROBATCH > 1:
        for name, axis in PER_EXAMPLE_BATCH_AXIS.items():
            out[name] = _to_microbatches(out[name], axis)
    return {'x': out['x'], 'norm1_w': out['norm1_w'], 'w_in': out['w_in'], 'sgu_ln_w': out['sgu_ln_w'], 'sgu_ln_b': out['sgu_ln_b'], 'sgu_w_spatial': out['sgu_w_spatial'], 'sgu_b_spatial': out['sgu_b_spatial'], 'sc_conv_w': out['sc_conv_w'], 'dn_conv_w': out['dn_conv_w'], 'dn_a_log': out['dn_a_log'], 'dn_dt_bias': out['dn_dt_bias'], 'dn_norm_w': out['dn_norm_w'], 'gla_w_gate2': out['gla_w_gate2'], 'gla_gate_bias': out['gla_gate_bias'], 'gla_norm_w': out['gla_norm_w'], 'w_out': out['w_out'], 'norm2_w': out['norm2_w'], 'w_gate_up': out['w_gate_up'], 'w_down': out['w_down'], 'final_norm_w': out['final_norm_w'], 'loss_target': out['loss_target'], 'm_norm1_w': out['m_norm1_w'], 'm_w_in': out['m_w_in'], 'm_sgu_ln_w': out['m_sgu_ln_w'], 'm_sgu_ln_b': out['m_sgu_ln_b'], 'm_sgu_w_spatial': out['m_sgu_w_spatial'], 'm_sgu_b_spatial': out['m_sgu_b_spatial'], 'm_sc_conv_w': out['m_sc_conv_w'], 'm_dn_conv_w': out['m_dn_conv_w'], 'm_dn_a_log': out['m_dn_a_log'], 'm_dn_dt_bias': out['m_dn_dt_bias'], 'm_dn_norm_w': out['m_dn_norm_w'], 'm_gla_w_gate2': out['m_gla_w_gate2'], 'm_gla_gate_bias': out['m_gla_gate_bias'], 'm_gla_norm_w': out['m_gla_norm_w'], 'm_w_out': out['m_w_out'], 'm_norm2_w': out['m_norm2_w'], 'm_w_gate_up': out['m_w_gate_up'], 'm_w_down': out['m_w_down'], 'm_final_norm_w': out['m_final_norm_w'], 'v_norm1_w': out['v_norm1_w'], 'v_w_in': out['v_w_in'], 'v_sgu_ln_w': out['v_sgu_ln_w'], 'v_sgu_ln_b': out['v_sgu_ln_b'], 'v_sgu_w_spatial': out['v_sgu_w_spatial'], 'v_sgu_b_spatial': out['v_sgu_b_spatial'], 'v_sc_conv_w': out['v_sc_conv_w'], 'v_dn_conv_w': out['v_dn_conv_w'], 'v_dn_a_log': out['v_dn_a_log'], 'v_dn_dt_bias': out['v_dn_dt_bias'], 'v_dn_norm_w': out['v_dn_norm_w'], 'v_gla_w_gate2': out['v_gla_w_gate2'], 'v_gla_gate_bias': out['v_gla_gate_bias'], 'v_gla_norm_w': out['v_gla_norm_w'], 'v_w_out': out['v_w_out'], 'v_norm2_w': out['v_norm2_w'], 'v_w_gate_up': out['v_w_gate_up'], 'v_w_down': out['v_w_down'], 'v_final_norm_w': out['v_final_norm_w']}


def _loss(weights, diff, rest, loss_target):
    with _jax.named_scope("forward"):
        args = {**rest, TWIN_DIFF_INPUT: diff, **{k: w.astype(_WEIGHT_DTYPES[k]) for k, w in weights.items()}}
        y = _forward(args)
    with _jax.named_scope("loss_head"):
        err = _jnp.square(y.astype(_jnp.float32) - loss_target)
        return 0.5 * _jnp.sum(_jnp.mean(err, axis=-1)) if err.ndim else 0.5 * err


def _adamw(w, g, m, v):
    m = ADAM_B1 * m + (1.0 - ADAM_B1) * g
    v = ADAM_B2 * v + (1.0 - ADAM_B2) * _jnp.square(g)
    m_hat = m / (1.0 - ADAM_B1 ** ADAM_STEP)
    v_hat = v / (1.0 - ADAM_B2 ** ADAM_STEP)
    delta = -ADAM_LR * (m_hat / (_jnp.sqrt(v_hat) + ADAM_EPS) + ADAM_WD * w)
    return delta, m, v


def reference(x, norm1_w, w_in, sgu_ln_w, sgu_ln_b, sgu_w_spatial, sgu_b_spatial, sc_conv_w, dn_conv_w, dn_a_log, dn_dt_bias, dn_norm_w, gla_w_gate2, gla_gate_bias, gla_norm_w, w_out, norm2_w, w_gate_up, w_down, final_norm_w, loss_target, m_norm1_w, m_w_in, m_sgu_ln_w, m_sgu_ln_b, m_sgu_w_spatial, m_sgu_b_spatial, m_sc_conv_w, m_dn_conv_w, m_dn_a_log, m_dn_dt_bias, m_dn_norm_w, m_gla_w_gate2, m_gla_gate_bias, m_gla_norm_w, m_w_out, m_norm2_w, m_w_gate_up, m_w_down, m_final_norm_w, v_norm1_w, v_w_in, v_sgu_ln_w, v_sgu_ln_b, v_sgu_w_spatial, v_sgu_b_spatial, v_sc_conv_w, v_dn_conv_w, v_dn_a_log, v_dn_dt_bias, v_dn_norm_w, v_gla_w_gate2, v_gla_gate_bias, v_gla_norm_w, v_w_out, v_norm2_w, v_w_gate_up, v_w_down, v_final_norm_w):
    given = dict(x=x, norm1_w=norm1_w, w_in=w_in, sgu_ln_w=sgu_ln_w, sgu_ln_b=sgu_ln_b, sgu_w_spatial=sgu_w_spatial, sgu_b_spatial=sgu_b_spatial, sc_conv_w=sc_conv_w, dn_conv_w=dn_conv_w, dn_a_log=dn_a_log, dn_dt_bias=dn_dt_bias, dn_norm_w=dn_norm_w, gla_w_gate2=gla_w_gate2, gla_gate_bias=gla_gate_bias, gla_norm_w=gla_norm_w, w_out=w_out, norm2_w=norm2_w, w_gate_up=w_gate_up, w_down=w_down, final_norm_w=final_norm_w, loss_target=loss_target, m_norm1_w=m_norm1_w, m_w_in=m_w_in, m_sgu_ln_w=m_sgu_ln_w, m_sgu_ln_b=m_sgu_ln_b, m_sgu_w_spatial=m_sgu_w_spatial, m_sgu_b_spatial=m_sgu_b_spatial, m_sc_conv_w=m_sc_conv_w, m_dn_conv_w=m_dn_conv_w, m_dn_a_log=m_dn_a_log, m_dn_dt_bias=m_dn_dt_bias, m_dn_norm_w=m_dn_norm_w, m_gla_w_gate2=m_gla_w_gate2, m_gla_gate_bias=m_gla_gate_bias, m_gla_norm_w=m_gla_norm_w, m_w_out=m_w_out, m_norm2_w=m_norm2_w, m_w_gate_up=m_w_gate_up, m_w_down=m_w_down, m_final_norm_w=m_final_norm_w, v_norm1_w=v_norm1_w, v_w_in=v_w_in, v_sgu_ln_w=v_sgu_ln_w, v_sgu_ln_b=v_sgu_ln_b, v_sgu_w_spatial=v_sgu_w_spatial, v_sgu_b_spatial=v_sgu_b_spatial, v_sc_conv_w=v_sc_conv_w, v_dn_conv_w=v_dn_conv_w, v_dn_a_log=v_dn_a_log, v_dn_dt_bias=v_dn_dt_bias, v_dn_norm_w=v_dn_norm_w, v_gla_w_gate2=v_gla_w_gate2, v_gla_gate_bias=v_gla_gate_bias, v_gla_norm_w=v_gla_norm_w, v_w_out=v_w_out, v_norm2_w=v_norm2_w, v_w_gate_up=v_w_gate_up, v_w_down=v_w_down, v_final_norm_w=v_final_norm_w)
    weights = {n: given[n] for n in TWIN_WEIGHTS}
    shared = {n: given[n] for n in SHARED_INPUTS}
    per_example = {n: given[n] for n in ['x']}
    grad_fn = _jax.value_and_grad(_loss, argnums=(0, 1))

    def one_microbatch(ex, loss_target):
        ex = dict(ex)
        diff = ex.pop(TWIN_DIFF_INPUT)
        return grad_fn(weights, diff, {**shared, **ex}, loss_target)

    if N_MICROBATCH == 1:
        loss, (grad_w, grad_x) = one_microbatch(per_example, given["loss_target"])
    else:
        def body(carry, xs):
            loss_sum, grad_sum = carry
            l_k, (gw_k, gx_k) = one_microbatch(xs[0], xs[1])
            with _jax.named_scope("update"):
                return (loss_sum + l_k, _jax.tree.map(_jnp.add, grad_sum, gw_k)), gx_k

        init = (_jnp.zeros((), _jnp.float32), _jax.tree.map(_jnp.zeros_like, weights))
        (loss, grad_w), grad_x = _jax.lax.scan(body, init, (per_example, given["loss_target"]))
    with _jax.named_scope("update"):
        delta_w, new_m, new_v = {}, {}, {}
        for n in TWIN_WEIGHTS:
            delta_w[n], new_m[n], new_v[n] = _adamw(weights[n], grad_w[n], given["m_" + n], given["v_" + n])
    return (loss, grad_x, *[grad_w[n] for n in TWIN_WEIGHTS], *[delta_w[n] for n in TWIN_WEIGHTS],
            *[new_m[n] for n in TWIN_WEIGHTS], *[new_v[n] for n in TWIN_WEIGHTS])
```

```python
import functools

import jax
import jax.numpy as jnp
from jax import lax
from jax.experimental import pallas as pl
from jax.experimental.pallas import tpu as pltpu

F32 = jnp.float32
BF16 = jnp.bfloat16
MXU_DTYPE = jnp.bfloat16
HI = lax.Precision.HIGHEST
MESH = pl.DeviceIdType.MESH

D_MODEL = 1024
DEPTH = 2
GROUP = 256
HEADS = 4
HEAD_DIM = 64
SGU_CHUNK = 128
SCAN_CHUNK = 64
D_FF = 2816
EPS = 1e-6
IN_COLS = 3352
P_COLS = 3584
HALO = 8
N_CHIPS = 4
N_DEV = 8
VMEM_LIMIT = 48 * 1024 * 1024

ADAM_LR = 0.001
ADAM_B1 = 0.9
ADAM_B2 = 0.999
ADAM_EPS = 1e-08
ADAM_WD = 0.01
ADAM_STEP = 10

(COL_AU, COL_AV, COL_BB, COL_BC, COL_BH, COL_CQ, COL_CK, COL_CV, COL_CZ,
 COL_DQ, COL_DK, COL_DV, COL_DZ) = range(13)
COL128_SMALL_C = 26
COL128_SMALL_D = 27


def _cparams(sem=None):
    return pltpu.CompilerParams(dimension_semantics=sem, vmem_limit_bytes=VMEM_LIMIT)


def _iota(shape, dim):
    return lax.broadcasted_iota(jnp.int32, shape, dim)


def _dg(a, b, ca, cb, prec=None):
    return lax.dot_general(a, b, (((ca,), (cb,)), ((), ())), preferred_element_type=F32, precision=prec)


@functools.partial(jax.custom_vjp, nondiff_argnums=(2, 3))
def bdot(a, b, ca, cb):
    return _dg(a.astype(MXU_DTYPE), b.astype(MXU_DTYPE), ca, cb)


def _bdot_fwd(a, b, ca, cb):
    return bdot(a, b, ca, cb), (a, b)


def _bdot_bwd(ca, cb, res, g):
    a, b = res
    if ca == 1:
        da = bdot(g, b, 1, 1 if cb == 0 else 0)
    else:
        da = bdot(b, g, 1 if cb == 0 else 0, 1)
    if cb == 0:
        db = bdot(a, g, 0, 0) if ca == 1 else bdot(a, g, 1, 0)
    else:
        db = bdot(g, a, 0, 0) if ca == 1 else bdot(g, a, 0, 1)
    return da, db


bdot.defvjp(_bdot_fwd, _bdot_bwd)


def hdot(a, b, ca=1, cb=0):
    return _dg(a, b, ca, cb, HI)


def _head_mask(h):
    return ((_iota((1, GROUP), 1) >> 6) == h).astype(F32)


def _block_diag_mask():
    return ((_iota((GROUP, GROUP), 0) >> 6) == (_iota((GROUP, GROUP), 1) >> 6)).astype(F32)


def _expand_mat(offset):
    return ((_iota((128, GROUP), 0) - offset) == (_iota((128, GROUP), 1) >> 6)).astype(F32)


def _tril(n, strict=False):
    r, c = _iota((n, n), 0), _iota((n, n), 1)
    return (r > c) if strict else (r >= c)


def _row_pick(x, row):
    return jnp.sum(jnp.where(_iota(x.shape, 0) == row, x, 0.0), axis=0, keepdims=True)


def _shift_rows_impl(x, halo, j):
    n = x.shape[0]
    r = _iota(x.shape, 0)
    top = jnp.concatenate([pltpu.roll(halo, j, 0), jnp.zeros((n - HALO, x.shape[1]), x.dtype)], axis=0)
    return jnp.where(r >= j, pltpu.roll(x, j, 0), top)


def _mxu_round(a):
    return a.astype(MXU_DTYPE).astype(F32)


@functools.partial(jax.custom_vjp, nondiff_argnums=(3,))
def _causal_conv(x, halo, w, width):
    xb, hb, wb = _mxu_round(x), _mxu_round(halo), _mxu_round(w)
    out = xb * _row_pick(wb, width - 1)
    for j in range(1, width):
        out = out + _shift_rows_impl(xb, hb, j) * _row_pick(wb, width - 1 - j)
    return out


def _causal_conv_fwd(x, halo, w, width):
    return _causal_conv(x, halo, w, width), (x, halo, w)


def _causal_conv_bwd(width, res, g):
    x, halo, w = res
    xb, hb, wb, gb = _mxu_round(x), _mxu_round(halo), _mxu_round(w), _mxu_round(g)
    n = g.shape[0]
    rows, rows8 = _iota(g.shape, 0), _iota(halo.shape, 0)
    dx = gb * _row_pick(wb, width - 1)
    dh = jnp.zeros_like(halo)
    dw = jnp.where(rows8 == width - 1, jnp.sum(xb * gb, axis=0, keepdims=True), 0.0)
    for j in range(1, width):
        gj = gb * _row_pick(wb, width - 1 - j)
        dx = dx + jnp.where(rows < n - j, pltpu.roll(gj, n - j, 0), 0.0)
        dh = dh + jnp.where(rows8 >= HALO - j, pltpu.roll(gj[0:HALO], HALO - j, 0), 0.0)
        tap = jnp.sum(_shift_rows_impl(xb, hb, j) * gb, axis=0, keepdims=True)
        dw = dw + jnp.where(rows8 == width - 1 - j, tap, 0.0)
    return dx, dh, dw


_causal_conv.defvjp(_causal_conv_fwd, _causal_conv_bwd)


def _head_sum(x, bd):
    return hdot(x, bd)


def _softplus(x):
    return jnp.maximum(x, 0.0) + jnp.log1p(jnp.exp(-jnp.abs(x)))


def _log_sigmoid(x):
    return -_softplus(-x)


def _silu(x):
    return x * jax.nn.sigmoid(x)


def _head_rmsnorm_gate(o, nw, z, bd):
    ms = _head_sum(o * o, bd) * (1.0 / HEAD_DIM)
    return o * lax.rsqrt(ms + EPS) * nw * _silu(z)


def _sgu_chunk(pu, pv, ln_w, ln_b, ws0, ws1, ws2, ws3, bs_t):
    u = jax.nn.gelu(pu)
    g = jax.nn.gelu(pv)
    mu = jnp.mean(g, axis=-1, keepdims=True)
    var = jnp.mean(jnp.square(g - mu), axis=-1, keepdims=True)
    v = (g - mu) * lax.rsqrt(var + EPS) * ln_w + ln_b
    keep = _tril(SGU_CHUNK)
    mixed = hdot(bs_t, _expand_mat(0))
    for h, ws in enumerate((ws0, ws1, ws2, ws3)):
        mixed = mixed + _head_mask(h) * bdot(jnp.where(keep, ws, 0.0), v, 1, 0)
    return u * mixed


def _sc_chunk(pb, pc, ph, halo_c, halo_h, cw):
    return pb * _causal_conv(pc * ph, halo_c * halo_h, cw, 3)


def _neumann_inverse(low):
    n = low.shape[0]
    eye = (_iota((n, n), 0) == _iota((n, n), 1)).astype(F32)
    a = -low
    t = eye + a
    for _ in range(5):
        a = hdot(a, a)
        t = t + hdot(t, a)
    return t


def _dn_chunk(pq, pk, pv, hq, hk, hv, small, pz, cwq, cwk, cwv, a_log, dt_bias, nw, state):
    c = SCAN_CHUNK
    bd = _block_diag_mask()
    q = _silu(_causal_conv(pq, hq, cwq, 4))
    k = _silu(_causal_conv(pk, hk, cwk, 4))
    v = _silu(_causal_conv(pv, hv, cwv, 4))
    q = q * lax.rsqrt(_head_sum(q * q, bd) + EPS) * (HEAD_DIM ** -0.5)
    k = k * lax.rsqrt(_head_sum(k * k, bd) + EPS)
    lane = _iota((1, 128), 1)
    g = jnp.where(lane < HEADS, -jnp.exp(a_log) * _softplus(small + dt_bias), 0.0)
    beta_b = hdot(jax.nn.sigmoid(small), _expand_mat(HEADS))
    gc = hdot(_tril(c).astype(F32), g)
    gc_b = hdot(gc, _expand_mat(0))
    gc_last_b = _row_pick(gc_b, c - 1)
    causal, strict = _tril(c), _tril(c, strict=True)
    kb = k * beta_b
    vb = v * beta_b
    kbe = kb * jnp.exp(gc_b)
    u = jnp.zeros_like(v)
    w = jnp.zeros_like(v)
    decays = []
    for h in range(HEADS):
        onehot = (_iota((c, 128), 1) == h).astype(F32)
        col = hdot(onehot, gc, 1, 1)
        row = jnp.sum(gc * onehot, axis=1, keepdims=True)
        decay = jnp.exp(jnp.where(causal, row - col, -jnp.inf))
        decays.append(decay)
        mh = _head_mask(h)
        low = jnp.where(strict, bdot(kb * mh, k, 1, 1) * decay, 0.0)
        inv = _neumann_inverse(low)
        u = u + mh * hdot(inv, vb)
        w = w + mh * hdot(inv, kbe)
    v_new = u - bdot(w, state, 1, 0)
    o = bdot(q * jnp.exp(gc_b), state, 1, 0)
    for h in range(HEADS):
        mh = _head_mask(h)
        attn = bdot(q * mh, k, 1, 1) * decays[h]
        o = o + mh * bdot(attn, v_new, 1, 0)
    k_dec = k * jnp.exp(gc_last_b - gc_b)
    new_state = state * jnp.exp(gc_last_b) + bd * bdot(k_dec, v_new, 0, 0)
    return _head_rmsnorm_gate(o, nw, pz, bd), new_state


def _gla_chunk(pq, pk, pv, small, pz, w2, gbias, nw, state_t):
    c = SCAN_CHUNK
    bd = _block_diag_mask()
    log_a = _log_sigmoid(bdot(small, w2, 1, 0) + gbias) * (1.0 / 16.0)
    gcum = hdot(_tril(c).astype(F32), log_a)
    g_mid = _row_pick(gcum, c // 2)
    g_last = _row_pick(gcum, c - 1)
    q = pq * (HEAD_DIM ** -0.5)
    qa = q * jnp.exp(gcum - g_mid)
    ka = pk * jnp.exp(g_mid - gcum)
    causal = _tril(c)
    o = bdot(q * jnp.exp(gcum), state_t, 1, 1)
    for h in range(HEADS):
        mh = _head_mask(h)
        attn = jnp.where(causal, bdot(qa * mh, ka, 1, 1), 0.0)
        o = o + mh * bdot(attn, pv, 1, 0)
    k_last = pk * jnp.exp(g_last - gcum)
    new_state_t = state_t * jnp.exp(g_last) + bd * bdot(pv, k_last, 0, 0)
    return _head_rmsnorm_gate(o, nw, pz, bd), new_state_t


def _col_spec(rows, group, rev_n=None):
    if rev_n is None:
        return pl.BlockSpec((rows, GROUP), lambda i: (i, group))
    return pl.BlockSpec((rows, GROUP), lambda i: (rev_n - 1 - i, group))


def _small_spec(rows, group128, rev_n=None):
    if rev_n is None:
        return pl.BlockSpec((rows, 128), lambda i: (i, group128))
    return pl.BlockSpec((rows, 128), lambda i: (rev_n - 1 - i, group128))


def _halo_spec(rows, group, rev_n=None):
    per = rows // HALO
    if rev_n is None:
        return pl.BlockSpec((HALO, GROUP), lambda i: (jnp.maximum(i * per - 1, 0), group))
    return pl.BlockSpec((HALO, GROUP), lambda i: (jnp.maximum((rev_n - 1 - i) * per - 1, 0), group))


def _full_spec(shape):
    nd = len(shape)
    return pl.BlockSpec(shape, lambda i: (0,) * nd)


def _out_rows_spec(rows, lanes, rev_n=None):
    if rev_n is None:
        return pl.BlockSpec((rows, lanes), lambda i: (i, 0))
    return pl.BlockSpec((rows, lanes), lambda i: (rev_n - 1 - i, 0))


def _sgu_fwd(p, ln_w, ln_b, ws, bs_t):
    t = p.shape[0]
    n = t // SGU_CHUNK

    def body(pu_ref, pv_ref, lw_ref, lb_ref, ws_ref, bs_ref, y_ref):
        y = _sgu_chunk(pu_ref[...], pv_ref[...], lw_ref[...], lb_ref[...],
                       ws_ref[0], ws_ref[1], ws_ref[2], ws_ref[3], bs_ref[...])
        y_ref[...] = y.astype(y_ref.dtype)

    return pl.pallas_call(
        body, name="sgu_fwd", grid=(n,),
        in_specs=[_col_spec(SGU_CHUNK, COL_AU), _col_spec(SGU_CHUNK, COL_AV), _full_spec((1, GROUP)),
                  _full_spec((1, GROUP)), _full_spec((HEADS, SGU_CHUNK, SGU_CHUNK)), _full_spec((SGU_CHUNK, 128))],
        out_specs=_out_rows_spec(SGU_CHUNK, GROUP),
        out_shape=jax.ShapeDtypeStruct((t, GROUP), BF16),
        compiler_params=_cparams(("arbitrary",)),
    )(p, p, ln_w, ln_b, ws, bs_t)


def _sgu_bwd(p, dmix, ln_w, ln_b, ws, bs_t):
    t = p.shape[0]
    n = t // SGU_CHUNK

    def body(pu_ref, pv_ref, dy_ref, lw_ref, lb_ref, ws_ref, bs_ref,
             dpu_ref, dpv_ref, dlw_ref, dlb_ref, dws_ref, dbs_ref):
        args = (pu_ref[...], pv_ref[...], lw_ref[...], lb_ref[...],
                ws_ref[0], ws_ref[1], ws_ref[2], ws_ref[3], bs_ref[...])
        _, vjp = jax.vjp(_sgu_chunk, *args)
        dpu, dpv, dlw, dlb, d0, d1, d2, d3, dbs = vjp(dy_ref[...])
        dpu_ref[...] = dpu.astype(dpu_ref.dtype)
        dpv_ref[...] = dpv.astype(dpv_ref.dtype)

        @pl.when(pl.program_id(0) == 0)
        def _():
            dlw_ref[...] = jnp.zeros_like(dlw_ref)
            dlb_ref[...] = jnp.zeros_like(dlb_ref)
            dws_ref[...] = jnp.zeros_like(dws_ref)
            dbs_ref[...] = jnp.zeros_like(dbs_ref)

        dlw_ref[...] += dlw
        dlb_ref[...] += dlb
        for h, d in enumerate((d0, d1, d2, d3)):
            dws_ref[h] += d
        dbs_ref[...] += dbs

    return pl.pallas_call(
        body, name="sgu_bwd", grid=(n,),
        in_specs=[_col_spec(SGU_CHUNK, COL_AU), _col_spec(SGU_CHUNK, COL_AV),
                  pl.BlockSpec((SGU_CHUNK, GROUP), lambda i: (i, 0)),
                  _full_spec((1, GROUP)), _full_spec((1, GROUP)), _full_spec((HEADS, SGU_CHUNK, SGU_CHUNK)),
                  _full_spec((SGU_CHUNK, 128))],
        out_specs=[_out_rows_spec(SGU_CHUNK, GROUP), _out_rows_spec(SGU_CHUNK, GROUP), _full_spec((1, GROUP)),
                   _full_spec((1, GROUP)), _full_spec((HEADS, SGU_CHUNK, SGU_CHUNK)), _full_spec((SGU_CHUNK, 128))],
        out_shape=[jax.ShapeDtypeStruct((t, GROUP), BF16), jax.ShapeDtypeStruct((t, GROUP), BF16),
                   jax.ShapeDtypeStruct((1, GROUP), F32), jax.ShapeDtypeStruct((1, GROUP), F32),
                   jax.ShapeDtypeStruct((HEADS, SGU_CHUNK, SGU_CHUNK), F32),
                   jax.ShapeDtypeStruct((SGU_CHUNK, 128), F32)],
        compiler_params=_cparams(("arbitrary",)),
    )(p, p, dmix, ln_w, ln_b, ws, bs_t)


SC_ROWS = 256


def _first_block_zero(halo, first):
    return jnp.where(first, 0.0, halo)


def _sc_fwd(p, cw):
    t = p.shape[0]
    n = t // SC_ROWS

    def body(pb_ref, pc_ref, ph_ref, hc_ref, hh_ref, cw_ref, y_ref):
        first = pl.program_id(0) == 0
        y = _sc_chunk(pb_ref[...], pc_ref[...], ph_ref[...], _first_block_zero(hc_ref[...], first),
                      _first_block_zero(hh_ref[...], first), cw_ref[...])
        y_ref[...] = y.astype(y_ref.dtype)

    return pl.pallas_call(
        body, name="sc_fwd", grid=(n,),
        in_specs=[_col_spec(SC_ROWS, COL_BB), _col_spec(SC_ROWS, COL_BC), _col_spec(SC_ROWS, COL_BH),
                  _halo_spec(SC_ROWS, COL_BC), _halo_spec(SC_ROWS, COL_BH), _full_spec((HALO, GROUP))],
        out_specs=_out_rows_spec(SC_ROWS, GROUP),
        out_shape=jax.ShapeDtypeStruct((t, GROUP), BF16),
        compiler_params=_cparams(("arbitrary",)),
    )(p, p, p, p, p, cw)


def _add_halo_grad(d, carry):
    return d + jnp.concatenate([jnp.zeros((d.shape[0] - HALO, d.shape[1]), d.dtype), carry], axis=0)


def _sc_bwd(p, dmix, cw):
    t = p.shape[0]
    n = t // SC_ROWS

    def body(pb_ref, pc_ref, ph_ref, hc_ref, hh_ref, dy_ref, cw_ref,
             dpb_ref, dpc_ref, dph_ref, dcw_ref, carry_c, carry_h):
        i = pl.program_id(0)
        first = i == n - 1

        @pl.when(i == 0)
        def _():
            carry_c[...] = jnp.zeros_like(carry_c)
            carry_h[...] = jnp.zeros_like(carry_h)
            dcw_ref[...] = jnp.zeros_like(dcw_ref)

        args = (pb_ref[...], pc_ref[...], ph_ref[...], _first_block_zero(hc_ref[...], first),
                _first_block_zero(hh_ref[...], first), cw_ref[...])
        _, vjp = jax.vjp(_sc_chunk, *args)
        dpb, dpc, dph, dhc, dhh, dcw = vjp(dy_ref[...])
        dpb_ref[...] = dpb.astype(dpb_ref.dtype)
        dpc_ref[...] = _add_halo_grad(dpc, carry_c[...]).astype(dpc_ref.dtype)
        dph_ref[...] = _add_halo_grad(dph, carry_h[...]).astype(dph_ref.dtype)
        carry_c[...] = dhc
        carry_h[...] = dhh
        dcw_ref[...] += dcw

    return pl.pallas_call(
        body, name="sc_bwd", grid=(n,),
        in_specs=[_col_spec(SC_ROWS, COL_BB, n), _col_spec(SC_ROWS, COL_BC, n), _col_spec(SC_ROWS, COL_BH, n),
                  _halo_spec(SC_ROWS, COL_BC, n), _halo_spec(SC_ROWS, COL_BH, n),
                  pl.BlockSpec((SC_ROWS, GROUP), lambda i: (n - 1 - i, 1)), _full_spec((HALO, GROUP))],
        out_specs=[_out_rows_spec(SC_ROWS, GROUP, n)] * 3 + [_full_spec((HALO, GROUP))],
        out_shape=[jax.ShapeDtypeStruct((t, GROUP), BF16)] * 3 + [jax.ShapeDtypeStruct((HALO, GROUP), F32)],
        scratch_shapes=[pltpu.VMEM((HALO, GROUP), F32), pltpu.VMEM((HALO, GROUP), F32)],
        compiler_params=_cparams(("arbitrary",)),
    )(p, p, p, p, p, dmix, cw)


def _dn_fwd(p, cw3, a_log, dt_bias, nw):
    t = p.shape[0]
    c = SCAN_CHUNK
    n = t // c

    def body(pq_ref, pk_ref, pv_ref, hq_ref, hk_ref, hv_ref, sm_ref, pz_ref, cw_ref, al_ref, dt_ref, nw_ref,
             y_ref, ck_ref, state):
        first = pl.program_id(0) == 0

        @pl.when(first)
        def _():
            state[...] = jnp.zeros_like(state)

        s_in = state[...]
        ck_ref[0] = s_in
        y, s_out = _dn_chunk(pq_ref[...], pk_ref[...], pv_ref[...], _first_block_zero(hq_ref[...], first),
                             _first_block_zero(hk_ref[...], first), _first_block_zero(hv_ref[...], first),
                             sm_ref[...], pz_ref[...], cw_ref[0], cw_ref[1], cw_ref[2],
                             al_ref[...], dt_ref[...], nw_ref[...], s_in)
        y_ref[...] = y.astype(y_ref.dtype)
        state[...] = s_out

    return pl.pallas_call(
        body, name="dn_fwd", grid=(n,),
        in_specs=[_col_spec(c, COL_CQ), _col_spec(c, COL_CK), _col_spec(c, COL_CV),
                  _halo_spec(c, COL_CQ), _halo_spec(c, COL_CK), _halo_spec(c, COL_CV),
                  _small_spec(c, COL128_SMALL_C), _col_spec(c, COL_CZ), _full_spec((3, HALO, GROUP)),
                  _full_spec((1, 128)), _full_spec((1, 128)), _full_spec((1, GROUP))],
        out_specs=[_out_rows_spec(c, GROUP), pl.BlockSpec((1, GROUP, GROUP), lambda i: (i, 0, 0))],
        out_shape=[jax.ShapeDtypeStruct((t, GROUP), BF16), jax.ShapeDtypeStruct((n, GROUP, GROUP), F32)],
        scratch_shapes=[pltpu.VMEM((GROUP, GROUP), F32)],
        compiler_params=_cparams(("arbitrary",)),
    )(p, p, p, p, p, p, p, p, cw3, a_log, dt_bias, nw)


def _dn_bwd(p, dmix, states, cw3, a_log, dt_bias, nw):
    t = p.shape[0]
    c = SCAN_CHUNK
    n = t // c

    def body(pq_ref, pk_ref, pv_ref, hq_ref, hk_ref, hv_ref, sm_ref, pz_ref, dy_ref, ck_ref,
             cw_ref, al_ref, dt_ref, nw_ref,
             dpq_ref, dpk_ref, dpv_ref, dsm_ref, dpz_ref, dcw_ref, dal_ref, ddt_ref, dnw_ref,
             dstate, carry):
        i = pl.program_id(0)
        first = i == n - 1

        @pl.when(i == 0)
        def _():
            dstate[...] = jnp.zeros_like(dstate)
            carry[...] = jnp.zeros_like(carry)
            dcw_ref[...] = jnp.zeros_like(dcw_ref)
            dal_ref[...] = jnp.zeros_like(dal_ref)
            ddt_ref[...] = jnp.zeros_like(ddt_ref)
            dnw_ref[...] = jnp.zeros_like(dnw_ref)

        args = (pq_ref[...], pk_ref[...], pv_ref[...], _first_block_zero(hq_ref[...], first),
                _first_block_zero(hk_ref[...], first), _first_block_zero(hv_ref[...], first),
                sm_ref[...], pz_ref[...], cw_ref[0], cw_ref[1], cw_ref[2],
                al_ref[...], dt_ref[...], nw_ref[...], ck_ref[0])
        _, vjp = jax.vjp(_dn_chunk, *args)
        (dpq, dpk, dpv, dhq, dhk, dhv, dsm, dpz, dcq, dck, dcv, dal, ddt, dnw, dst) = vjp(
            (dy_ref[...], dstate[...]))
        dpq_ref[...] = _add_halo_grad(dpq, carry[0]).astype(dpq_ref.dtype)
        dpk_ref[...] = _add_halo_grad(dpk, carry[1]).astype(dpk_ref.dtype)
        dpv_ref[...] = _add_halo_grad(dpv, carry[2]).astype(dpv_ref.dtype)
        dsm_ref[...] = dsm.astype(dsm_ref.dtype)
        dpz_ref[...] = dpz.astype(dpz_ref.dtype)
        carry[0] = dhq
        carry[1] = dhk
        carry[2] = dhv
        dstate[...] = dst
        dcw_ref[0] += dcq
        dcw_ref[1] += dck
        dcw_ref[2] += dcv
        dal_ref[...] += dal
        ddt_ref[...] += ddt
        dnw_ref[...] += dnw

    return pl.pallas_call(
        body, name="dn_bwd", grid=(n,),
        in_specs=[_col_spec(c, COL_CQ, n), _col_spec(c, COL_CK, n), _col_spec(c, COL_CV, n),
                  _halo_spec(c, COL_CQ, n), _halo_spec(c, COL_CK, n), _halo_spec(c, COL_CV, n),
                  _small_spec(c, COL128_SMALL_C, n), _col_spec(c, COL_CZ, n),
                  pl.BlockSpec((c, GROUP), lambda i: (n - 1 - i, 2)),
                  pl.BlockSpec((1, GROUP, GROUP), lambda i: (n - 1 - i, 0, 0)),
                  _full_spec((3, HALO, GROUP)), _full_spec((1, 128)), _full_spec((1, 128)), _full_spec((1, GROUP))],
        out_specs=[_out_rows_spec(c, GROUP, n)] * 3 + [_out_rows_spec(c, 128, n), _out_rows_spec(c, GROUP, n),
                   _full_spec((3, HALO, GROUP)), _full_spec((1, 128)), _full_spec((1, 128)), _full_spec((1, GROUP))],
        out_shape=[jax.ShapeDtypeStruct((t, GROUP), BF16)] * 3 + [
            jax.ShapeDtypeStruct((t, 128), BF16), jax.ShapeDtypeStruct((t, GROUP), BF16),
            jax.ShapeDtypeStruct((3, HALO, GROUP), F32), jax.ShapeDtypeStruct((1, 128), F32),
            jax.ShapeDtypeStruct((1, 128), F32), jax.ShapeDtypeStruct((1, GROUP), F32)],
        scratch_shapes=[pltpu.VMEM((GROUP, GROUP), F32), pltpu.VMEM((3, HALO, GROUP), F32)],
        compiler_params=_cparams(("arbitrary",)),
    )(p, p, p, p, p, p, p, p, dmix, states, cw3, a_log, dt_bias, nw)


def _gla_fwd(p, w2, gbias, nw):
    t = p.shape[0]
    c = SCAN_CHUNK
    n = t // c

    def body(pq_ref, pk_ref, pv_ref, sm_ref, pz_ref, w2_ref, gb_ref, nw_ref, y_ref, ck_ref, state):
        @pl.when(pl.program_id(0) == 0)
        def _():
            state[...] = jnp.zeros_like(state)

        s_in = state[...]
        ck_ref[0] = s_in
        y, s_out = _gla_chunk(pq_ref[...], pk_ref[...], pv_ref[...], sm_ref[...], pz_ref[...],
                              w2_ref[...], gb_ref[...], nw_ref[...], s_in)
        y_ref[...] = y.astype(y_ref.dtype)
        state[...] = s_out

    return pl.pallas_call(
        body, name="gla_fwd", grid=(n,),
        in_specs=[_col_spec(c, COL_DQ), _col_spec(c, COL_DK), _col_spec(c, COL_DV),
                  _small_spec(c, COL128_SMALL_D), _col_spec(c, COL_DZ),
                  _full_spec((128, GROUP)), _full_spec((1, GROUP)), _full_spec((1, GROUP))],
        out_specs=[_out_rows_spec(c, GROUP), pl.BlockSpec((1, GROUP, GROUP), lambda i: (i, 0, 0))],
        out_shape=[jax.ShapeDtypeStruct((t, GROUP), BF16), jax.ShapeDtypeStruct((n, GROUP, GROUP), F32)],
        scratch_shapes=[pltpu.VMEM((GROUP, GROUP), F32)],
        compiler_params=_cparams(("arbitrary",)),
    )(p, p, p, p, p, w2, gbias, nw)


def _gla_bwd(p, dmix, states, w2, gbias, nw):
    t = p.shape[0]
    c = SCAN_CHUNK
    n = t // c

    def body(pq_ref, pk_ref, pv_ref, sm_ref, pz_ref, dy_ref, ck_ref, w2_ref, gb_ref, nw_ref,
             dpq_ref, dpk_ref, dpv_ref, dsm_ref, dpz_ref, dw2_ref, dgb_ref, dnw_ref, dstate):
        @pl.when(pl.program_id(0) == 0)
        def _():
            dstate[...] = jnp.zeros_like(dstate)
            dw2_ref[...] = jnp.zeros_like(dw2_ref)
            dgb_ref[...] = jnp.zeros_like(dgb_ref)
            dnw_ref[...] = jnp.zeros_like(dnw_ref)

        args = (pq_ref[...], pk_ref[...], pv_ref[...], sm_ref[...], pz_ref[...],
                w2_ref[...], gb_ref[...], nw_ref[...], ck_ref[0])
        _, vjp = jax.vjp(_gla_chunk, *args)
        dpq, dpk, dpv, dsm, dpz, dw2, dgb, dnw, dst = vjp((dy_ref[...], dstate[...]))
        dpq_ref[...] = dpq.astype(dpq_ref.dtype)
        dpk_ref[...] = dpk.astype(dpk_ref.dtype)
        dpv_ref[...] = dpv.astype(dpv_ref.dtype)
        dsm_ref[...] = dsm.astype(dsm_ref.dtype)
        dpz_ref[...] = dpz.astype(dpz_ref.dtype)
        dstate[...] = dst
        dw2_ref[...] += dw2
        dgb_ref[...] += dgb
        dnw_ref[...] += dnw

    return pl.pallas_call(
        body, name="gla_bwd", grid=(n,),
        in_specs=[_col_spec(c, COL_DQ, n), _col_spec(c, COL_DK, n), _col_spec(c, COL_DV, n),
                  _small_spec(c, COL128_SMALL_D, n), _col_spec(c, COL_DZ, n),
                  pl.BlockSpec((c, GROUP), lambda i: (n - 1 - i, 3)),
                  pl.BlockSpec((1, GROUP, GROUP), lambda i: (n - 1 - i, 0, 0)),
                  _full_spec((128, GROUP)), _full_spec((1, GROUP)), _full_spec((1, GROUP))],
        out_specs=[_out_rows_spec(c, GROUP, n)] * 3 + [_out_rows_spec(c, 128, n), _out_rows_spec(c, GROUP, n),
                   _full_spec((128, GROUP)), _full_spec((1, GROUP)), _full_spec((1, GROUP))],
        out_shape=[jax.ShapeDtypeStruct((t, GROUP), BF16)] * 3 + [
            jax.ShapeDtypeStruct((t, 128), BF16), jax.ShapeDtypeStruct((t, GROUP), BF16),
            jax.ShapeDtypeStruct((128, GROUP), F32), jax.ShapeDtypeStruct((1, GROUP), F32),
            jax.ShapeDtypeStruct((1, GROUP), F32)],
        scratch_shapes=[pltpu.VMEM((GROUP, GROUP), F32)],
        compiler_params=_cparams(("arbitrary",)),
    )(p, p, p, p, p, dmix, states, w2, gbias, nw)


def _pick_tile(n, pref):
    for cand in pref:
        if n % cand == 0:
            return cand
    return n


def _matmul(a, b, mode, out_dtype, name, res=None):
    if mode == "nn":
        (m, k), n = a.shape, b.shape[1]
    elif mode == "nt":
        (m, k), n = a.shape, b.shape[0]
    else:
        (k, m), n = a.shape, b.shape[1]
    tm = _pick_tile(m, (512, 256, 128))
    tn = _pick_tile(n, (512, 1408, 256, 128))
    tk = _pick_tile(k, (1024, 1408, 512, 256, 128))
    nk = k // tk
    if mode == "nn":
        a_spec = pl.BlockSpec((tm, tk), lambda i, j, kk: (i, kk))
        b_spec = pl.BlockSpec((tk, tn), lambda i, j, kk: (kk, j))
        dims = (1, 0)
    elif mode == "nt":
        a_spec = pl.BlockSpec((tm, tk), lambda i, j, kk: (i, kk))
        b_spec = pl.BlockSpec((tn, tk), lambda i, j, kk: (j, kk))
        dims = (1, 1)
    else:
        a_spec = pl.BlockSpec((tk, tm), lambda i, j, kk: (kk, i))
        b_spec = pl.BlockSpec((tk, tn), lambda i, j, kk: (kk, j))
        dims = (0, 0)
    o_spec = pl.BlockSpec((tm, tn), lambda i, j, kk: (i, j))
    has_res = res is not None

    def body(*refs):
        if has_res:
            a_ref, b_ref, r_ref, o_ref, acc = refs
        else:
            a_ref, b_ref, o_ref, acc = refs
        kk = pl.program_id(2)

        @pl.when(kk == 0)
        def _():
            acc[...] = jnp.zeros_like(acc)

        acc[...] += _dg(a_ref[...].astype(MXU_DTYPE), b_ref[...].astype(MXU_DTYPE), *dims)

        @pl.when(kk == nk - 1)
        def _():
            out = acc[...]
            if has_res:
                out = out + r_ref[...]
            o_ref[...] = out.astype(o_ref.dtype)

    in_specs = [a_spec, b_spec] + ([o_spec] if has_res else [])
    args = (a, b) + ((res,) if has_res else ())
    return pl.pallas_call(
        body, name=name, grid=(m // tm, n // tn, nk), in_specs=in_specs, out_specs=o_spec,
        out_shape=jax.ShapeDtypeStruct((m, n), out_dtype),
        scratch_shapes=[pltpu.VMEM((tm, tn), F32)],
        compiler_params=_cparams(("parallel", "parallel", "arbitrary")),
    )(*args)


NORM_ROWS = 512


def _rmsnorm_fwd(x, w, name):
    t, d = x.shape

    def body(x_ref, w_ref, o_ref):
        xv = x_ref[...]
        r = lax.rsqrt(jnp.mean(xv * xv, axis=-1, keepdims=True) + EPS)
        o_ref[...] = (xv * r * w_ref[...]).astype(o_ref.dtype)

    return pl.pallas_call(
        body, name=name, grid=(t // NORM_ROWS,),
        in_specs=[pl.BlockSpec((NORM_ROWS, d), lambda i: (i, 0)), _full_spec((1, d))],
        out_specs=pl.BlockSpec((NORM_ROWS, d), lambda i: (i, 0)),
        out_shape=jax.ShapeDtypeStruct((t, d), BF16),
        compiler_params=_cparams(("parallel",)),
    )(x, w)


def _rmsnorm_bwd(x, w, dh, dres, name):
    t, d = x.shape

    def body(x_ref, w_ref, dh_ref, dr_ref, dx_ref, dw_ref):
        @pl.when(pl.program_id(0) == 0)
        def _():
            dw_ref[...] = jnp.zeros_like(dw_ref)

        xv = x_ref[...]
        g = dh_ref[...].astype(F32)
        r = lax.rsqrt(jnp.mean(xv * xv, axis=-1, keepdims=True) + EPS)
        xhat = xv * r
        dw_ref[...] += jnp.sum(g * xhat, axis=0, keepdims=True)
        gx = g * w_ref[...]
        dx = r * (gx - xhat * jnp.mean(gx * xhat, axis=-1, keepdims=True))
        dx_ref[...] = dr_ref[...] + dx

    return pl.pallas_call(
        body, name=name, grid=(t // NORM_ROWS,),
        in_specs=[pl.BlockSpec((NORM_ROWS, d), lambda i: (i, 0)), _full_spec((1, d)),
                  pl.BlockSpec((NORM_ROWS, d), lambda i: (i, 0)), pl.BlockSpec((NORM_ROWS, d), lambda i: (i, 0))],
        out_specs=[pl.BlockSpec((NORM_ROWS, d), lambda i: (i, 0)), _full_spec((1, d))],
        out_shape=[jax.ShapeDtypeStruct((t, d), F32), jax.ShapeDtypeStruct((1, d), F32)],
        compiler_params=_cparams(("arbitrary",)),
    )(x, w, dh, dres)


SWIGLU_ROWS = 128


def _swiglu_fwd(gu):
    t = gu.shape[0]

    def body(gu_ref, o_ref):
        gate = gu_ref[:, :D_FF]
        up = gu_ref[:, D_FF:]
        o_ref[...] = (_silu(gate) * up).astype(o_ref.dtype)

    return pl.pallas_call(
        body, name="swiglu_fwd", grid=(t // SWIGLU_ROWS,),
        in_specs=[pl.BlockSpec((SWIGLU_ROWS, 2 * D_FF), lambda i: (i, 0))],
        out_specs=pl.BlockSpec((SWIGLU_ROWS, D_FF), lambda i: (i, 0)),
        out_shape=jax.ShapeDtypeStruct((t, D_FF), BF16),
        compiler_params=_cparams(("parallel",)),
    )(gu)


def _swiglu_bwd(gu, dact):
    t = gu.shape[0]

    def body(gu_ref, da_ref, o_ref):
        gate = gu_ref[:, :D_FF]
        up = gu_ref[:, D_FF:]
        da = da_ref[...]
        sg = jax.nn.sigmoid(gate)
        o_ref[:, :D_FF] = (da * up * (sg * (1.0 + gate * (1.0 - sg)))).astype(o_ref.dtype)
        o_ref[:, D_FF:] = (da * gate * sg).astype(o_ref.dtype)

    return pl.pallas_call(
        body, name="swiglu_bwd", grid=(t // SWIGLU_ROWS,),
        in_specs=[pl.BlockSpec((SWIGLU_ROWS, 2 * D_FF), lambda i: (i, 0)),
                  pl.BlockSpec((SWIGLU_ROWS, D_FF), lambda i: (i, 0))],
        out_specs=pl.BlockSpec((SWIGLU_ROWS, 2 * D_FF), lambda i: (i, 0)),
        out_shape=jax.ShapeDtypeStruct((t, 2 * D_FF), BF16),
        compiler_params=_cparams(("parallel",)),
    )(gu, dact)


def _loss_head(x, w, target):
    t, d = x.shape

    def fwd(xv, wv, tv):
        r = lax.rsqrt(jnp.mean(xv * xv, axis=-1, keepdims=True) + EPS)
        err = xv * r * wv - tv
        return 0.5 * jnp.sum(jnp.mean(err * err, axis=-1, keepdims=True), axis=0, keepdims=True)

    def body(x_ref, w_ref, t_ref, dx_ref, dw_ref, loss_ref):
        @pl.when(pl.program_id(0) == 0)
        def _():
            dw_ref[...] = jnp.zeros_like(dw_ref)
            loss_ref[...] = jnp.zeros_like(loss_ref)

        loss, vjp = jax.vjp(fwd, x_ref[...], w_ref[...], t_ref[...])
        dx, dw, _ = vjp(jnp.ones((1, 1), F32))
        dx_ref[...] = dx
        dw_ref[...] += dw
        loss_ref[...] += jnp.broadcast_to(loss, loss_ref.shape)

    return pl.pallas_call(
        body, name="loss_head", grid=(t // NORM_ROWS,),
        in_specs=[pl.BlockSpec((NORM_ROWS, d), lambda i: (i, 0)), _full_spec((1, d)),
                  pl.BlockSpec((NORM_ROWS, d), lambda i: (i, 0))],
        out_specs=[pl.BlockSpec((NORM_ROWS, d), lambda i: (i, 0)), _full_spec((1, d)), _full_spec((8, 128))],
        out_shape=[jax.ShapeDtypeStruct((t, d), F32), jax.ShapeDtypeStruct((1, d), F32),
                   jax.ShapeDtypeStruct((8, 128), F32)],
        compiler_params=_cparams(("arbitrary",)),
    )(x, w, target)


def _pad_w_in(w):
    z = lambda n: jnp.zeros((w.shape[0], n), w.dtype)
    return jnp.concatenate([w[:, 0:2048], w[:, 2056:2312], w[:, 2312:3080], w[:, 3096:3352],
                            w[:, 2048:2056], z(120), w[:, 3080:3096], z(112)], axis=1)


def _unpad_w_in(wp):
    return jnp.concatenate([wp[:, 0:2048], wp[:, 3328:3336], wp[:, 2048:2304], wp[:, 2304:3072],
                            wp[:, 3456:3472], wp[:, 3072:3328]], axis=1)


def _pad_rows(a, rows):
    return jnp.concatenate([a, jnp.zeros((rows - a.shape[0],) + a.shape[1:], a.dtype)], axis=0)


def _pad_lanes(a, lanes):
    return jnp.concatenate([a, jnp.zeros(a.shape[:-1] + (lanes - a.shape[-1],), a.dtype)], axis=-1)


def _layer_params(l, small):
    dn_cw = small["dn_conv_w"][l]
    return dict(
        ln_w=small["sgu_ln_w"][l][None], ln_b=small["sgu_ln_b"][l][None],
        ws=small["sgu_w_spatial"][l], bs_t=_pad_lanes(small["sgu_b_spatial"][l].T, 128),
        sc_cw=_pad_rows(small["sc_conv_w"][l], HALO),
        dn_cw=jnp.stack([_pad_rows(dn_cw[:, j * GROUP:(j + 1) * GROUP], HALO) for j in range(3)]),
        dn_al=_pad_lanes(small["dn_a_log"][l][None], 128), dn_dt=_pad_lanes(small["dn_dt_bias"][l][None], 128),
        dn_nw=jnp.tile(small["dn_norm_w"][l][None], (1, HEADS)),
        gla_w2=_pad_rows(small["gla_w_gate2"][l], 128), gla_gb=small["gla_gate_bias"][l][None],
        gla_nw=jnp.tile(small["gla_norm_w"][l][None], (1, HEADS)),
    )


def _local_step(x, target, big, small):
    saved = []
    h = x
    for l in range(DEPTH):
        lp = _layer_params(l, small)
        h1 = _rmsnorm_fwd(h, small["norm1_w"][l][None], "norm1_fwd")
        p = _matmul(h1, big["w_in"][l], "nn", F32, "proj_in")
        y_a = _sgu_fwd(p, lp["ln_w"], lp["ln_b"], lp["ws"], lp["bs_t"])
        y_b = _sc_fwd(p, lp["sc_cw"])
        y_c, st_c = _dn_fwd(p, lp["dn_cw"], lp["dn_al"], lp["dn_dt"], lp["dn_nw"])
        y_d, st_d = _gla_fwd(p, lp["gla_w2"], lp["gla_gb"], lp["gla_nw"])
        mix = jnp.concatenate([y_a, y_b, y_c, y_d], axis=1)
        x1 = _matmul(mix, big["w_out"][l], "nn", F32, "proj_out", res=h)
        h2 = _rmsnorm_fwd(x1, small["norm2_w"][l][None], "norm2_fwd")
        gu = _matmul(h2, big["w_gate_up"][l], "nn", F32, "ffn_up")
        act = _swiglu_fwd(gu)
        x2 = _matmul(act, big["w_down"][l], "nn", F32, "ffn_down", res=x1)
        saved.append(dict(x0=h, h1=h1, p=p, st_c=st_c, st_d=st_d, mix=mix, x1=x1, h2=h2, gu=gu, act=act, lp=lp))
        h = x2

    dx, d_final, loss = _loss_head(h, small["final_norm_w"][None], target)
    gbig = {k: [None] * DEPTH for k in ("w_in", "w_out", "w_gate_up", "w_down")}
    gs = {k: [None] * DEPTH for k in ("norm1_w", "sgu_ln_w", "sgu_ln_b", "sgu_w_spatial", "sgu_b_spatial", "sc_conv_w",
                                     "dn_conv_w", "dn_a_log", "dn_dt_bias", "dn_norm_w", "gla_w_gate2",
                                     "gla_gate_bias", "gla_norm_w", "norm2_w")}
    for l in reversed(range(DEPTH)):
        s = saved[l]
        lp = s["lp"]
        gbig["w_down"][l] = _matmul(s["act"], dx, "tn", F32, "ffn_down_dw")
        dact = _matmul(dx, big["w_down"][l], "nt", F32, "ffn_down_dx")
        dgu = _swiglu_bwd(s["gu"], dact)
        gbig["w_gate_up"][l] = _matmul(s["h2"], dgu, "tn", F32, "ffn_up_dw")
        dh2 = _matmul(dgu, big["w_gate_up"][l], "nt", F32, "ffn_up_dx")
        dx1, gs["norm2_w"][l] = _rmsnorm_bwd(s["x1"], small["norm2_w"][l][None], dh2, dx, "norm2_bwd")
        gbig["w_out"][l] = _matmul(s["mix"], dx1, "tn", F32, "proj_out_dw")
        dmix = _matmul(dx1, big["w_out"][l], "nt", F32, "proj_out_dx")
        p = s["p"]
        dpu, dpv, g_lw, g_lb, g_ws, g_bs = _sgu_bwd(p, dmix, lp["ln_w"], lp["ln_b"], lp["ws"], lp["bs_t"])
        dpb, dpc, dph, g_sc = _sc_bwd(p, dmix, lp["sc_cw"])
        dcq, dck, dcv, dcs, dcz, g_dcw, g_al, g_dt, g_dnw = _dn_bwd(
            p, dmix, s["st_c"], lp["dn_cw"], lp["dn_al"], lp["dn_dt"], lp["dn_nw"])
        ddq, ddk, ddv, dds, ddz, g_w2, g_gb, g_gnw = _gla_bwd(p, dmix, s["st_d"], lp["gla_w2"], lp["gla_gb"],
                                                             lp["gla_nw"])
        dp = jnp.concatenate([dpu, dpv, dpb, dpc, dph, dcq, dck, dcv, dcz, ddq, ddk, ddv, ddz, dcs, dds], axis=1)
        gbig["w_in"][l] = _matmul(s["h1"], dp, "tn", F32, "proj_in_dw")
        dh1 = _matmul(dp, big["w_in"][l], "nt", F32, "proj_in_dx")
        dx, gs["norm1_w"][l] = _rmsnorm_bwd(s["x0"], small["norm1_w"][l][None], dh1, dx1, "norm1_bwd")
        gs["sgu_ln_w"][l], gs["sgu_ln_b"][l] = g_lw[0], g_lb[0]
        gs["sgu_w_spatial"][l] = g_ws
        gs["sgu_b_spatial"][l] = g_bs[:, :HEADS].T
        gs["sc_conv_w"][l] = g_sc[:3]
        gs["dn_conv_w"][l] = jnp.concatenate([g_dcw[0, :4], g_dcw[1, :4], g_dcw[2, :4]], axis=1)
        gs["dn_a_log"][l], gs["dn_dt_bias"][l] = g_al[0, :HEADS], g_dt[0, :HEADS]
        gs["dn_norm_w"][l] = jnp.sum(g_dnw.reshape(HEADS, HEAD_DIM), axis=0)
        gs["gla_w_gate2"][l] = g_w2[:16]
        gs["gla_gate_bias"][l] = g_gb[0]
        gs["gla_norm_w"][l] = jnp.sum(g_gnw.reshape(HEADS, HEAD_DIM), axis=0)
        gs["norm1_w"][l] = gs["norm1_w"][l][0]
        gs["norm2_w"][l] = gs["norm2_w"][l][0]
    gsmall = {k: jnp.stack(v) for k, v in gs.items()}
    gsmall["final_norm_w"] = d_final[0]
    return loss, dx, gbig, gsmall


def _peer_chips(x, y):
    return [(1 - x, y, 2 * (1 - x) + y), (x, 1 - y, 2 * x + 1 - y), (1 - x, 1 - y, 2 * (1 - x) + 1 - y)]


def _gather_weights(shards):
    na = len(shards)

    def body(*refs):
        ins, outs = refs[:na], refs[na:2 * na]
        send_sems, recv_sems, local_sems = refs[2 * na:]
        x, y, c = lax.axis_index("x"), lax.axis_index("y"), lax.axis_index("c")
        me = 2 * x + y
        copies = []
        for a in range(na):
            own = pltpu.make_async_copy(ins[a], outs[a].at[me], local_sems.at[a])
            own.start()
            copies.append(own)
            for k, (px, py, pidx) in enumerate(_peer_chips(x, y)):
                cp = pltpu.make_async_remote_copy(
                    src_ref=ins[a], dst_ref=outs[a].at[me], send_sem=send_sems.at[a, k], recv_sem=recv_sems.at[a, k],
                    device_id=(px, py, c), device_id_type=MESH)
                cp.start()
        for own in copies:
            own.wait()
        for a in range(na):
            for k, (px, py, pidx) in enumerate(_peer_chips(x, y)):
                cp = pltpu.make_async_remote_copy(
                    src_ref=ins[a], dst_ref=outs[a].at[pidx], send_sem=send_sems.at[a, k],
                    recv_sem=recv_sems.at[a, k], device_id=(px, py, c), device_id_type=MESH)
                cp.wait_send()
                cp.wait_recv()

    any_spec = pl.BlockSpec(memory_space=pl.ANY)
    return pl.pallas_call(
        body, name="gather_weights",
        in_specs=[any_spec] * na, out_specs=[any_spec] * na,
        out_shape=[jax.ShapeDtypeStruct((N_CHIPS,) + s.shape, s.dtype) for s in shards],
        scratch_shapes=[pltpu.SemaphoreType.DMA((na, 3)), pltpu.SemaphoreType.DMA((na, 3)),
                        pltpu.SemaphoreType.DMA((na,))],
    )(*shards)


def _exchange_grads(pieces, small_vec):
    na = len(pieces)

    def body(*refs):
        ins, sm_in = refs[:na], refs[na]
        outs, sm_out = refs[na + 1:2 * na + 1], refs[2 * na + 1]
        send1, recv1, send2, recv2, local_sems = refs[2 * na + 2:]
        x, y, c = lax.axis_index("x"), lax.axis_index("y"), lax.axis_index("c")
        me = 2 * x + y
        srcs = [ins[a] for a in range(na)]
        dsts = list(outs) + [sm_out]
        nb = na + 1

        def src_for(b, chip):
            return srcs[b].at[chip] if b < na else sm_in

        locals_ = []
        for b in range(nb):
            own = pltpu.make_async_copy(src_for(b, me), dsts[b].at[4 * c + me], local_sems.at[b])
            own.start()
            locals_.append(own)
            for k, (px, py, pidx) in enumerate(_peer_chips(x, y)):
                pltpu.make_async_remote_copy(
                    src_ref=src_for(b, pidx), dst_ref=dsts[b].at[4 * c + me], send_sem=send1.at[b, k],
                    recv_sem=recv1.at[b, k], device_id=(px, py, c), device_id_type=MESH).start()
        for own in locals_:
            own.wait()
        for b in range(nb):
            for k, (px, py, pidx) in enumerate(_peer_chips(x, y)):
                cp = pltpu.make_async_remote_copy(
                    src_ref=src_for(b, pidx), dst_ref=dsts[b].at[4 * c + pidx], send_sem=send1.at[b, k],
                    recv_sem=recv1.at[b, k], device_id=(px, py, c), device_id_type=MESH)
                cp.wait_send()
                cp.wait_recv()
        for b in range(nb):
            mine = dsts[b].at[pl.ds(4 * c, 4)]
            pltpu.make_async_remote_copy(
                src_ref=mine, dst_ref=mine, send_sem=send2.at[b], recv_sem=recv2.at[b],
                device_id=(x, y, 1 - c), device_id_type=MESH).start()
        for b in range(nb):
            mine = dsts[b].at[pl.ds(4 * c, 4)]
            theirs = dsts[b].at[pl.ds(4 * (1 - c), 4)]
            cp = pltpu.make_async_remote_copy(
                src_ref=mine, dst_ref=theirs, send_sem=send2.at[b], recv_sem=recv2.at[b],
                device_id=(x, y, 1 - c), device_id_type=MESH)
            cp.wait_send()
            cp.wait_recv()

    any_spec = pl.BlockSpec(memory_space=pl.ANY)
    out_shape = [jax.ShapeDtypeStruct((N_DEV,) + p.shape[1:], p.dtype) for p in pieces]
    out_shape.append(jax.ShapeDtypeStruct((N_DEV,) + small_vec.shape, small_vec.dtype))
    nb = na + 1
    return pl.pallas_call(
        body, name="exchange_grads",
        in_specs=[any_spec] * nb, out_specs=[any_spec] * nb, out_shape=out_shape,
        scratch_shapes=[pltpu.SemaphoreType.DMA((nb, 3)), pltpu.SemaphoreType.DMA((nb, 3)),
                        pltpu.SemaphoreType.DMA((nb,)), pltpu.SemaphoreType.DMA((nb,)),
                        pltpu.SemaphoreType.DMA((nb,))],
    )(*pieces, small_vec)


def _adamw_math(g, w, m, v):
    m2 = ADAM_B1 * m + (1.0 - ADAM_B1) * g
    v2 = ADAM_B2 * v + (1.0 - ADAM_B2) * (g * g)
    m_hat = m2 / (1.0 - ADAM_B1 ** ADAM_STEP)
    v_hat = v2 / (1.0 - ADAM_B2 ** ADAM_STEP)
    delta = -ADAM_LR * (m_hat / (jnp.sqrt(v_hat) + ADAM_EPS) + ADAM_WD * w)
    return delta, m2, v2


def _ordered_sum(ref):
    total = ref[0].astype(F32)
    for j in range(1, N_DEV):
        total = total + ref[j].astype(F32)
    return total


def _adamw_big(contrib, w, m, v, name):
    l, r, c = w.shape
    tr = _pick_tile(r, (256, 176, 128, 64, 8))
    blk = pl.BlockSpec((1, tr, c), lambda i, j: (i, j, 0))

    def body(g_ref, w_ref, m_ref, v_ref, go_ref, d_ref, mo_ref, vo_ref):
        g = _ordered_sum(g_ref)
        delta, m2, v2 = _adamw_math(g, w_ref[...], m_ref[...], v_ref[...])
        go_ref[...] = g
        d_ref[...] = delta
        mo_ref[...] = m2
        vo_ref[...] = v2

    return pl.pallas_call(
        body, name=name, grid=(l, r // tr),
        in_specs=[pl.BlockSpec((N_DEV, 1, tr, c), lambda i, j: (0, i, j, 0)), blk, blk, blk],
        out_specs=[blk] * 4, out_shape=[jax.ShapeDtypeStruct(w.shape, F32)] * 4,
        compiler_params=_cparams(("parallel", "parallel")),
    )(contrib, w, m, v)


def _sum_small(contrib):
    _, rows, _ = contrib.shape

    def body(g_ref, o_ref):
        o_ref[...] = _ordered_sum(g_ref)

    return pl.pallas_call(
        body, name="sum_small", out_shape=jax.ShapeDtypeStruct((rows, 128), F32),
        compiler_params=_cparams(),
    )(contrib)


def _adamw_small(g, w, m, v):
    def body(g_ref, w_ref, m_ref, v_ref, d_ref, mo_ref, vo_ref):
        delta, m2, v2 = _adamw_math(g_ref[...], w_ref[...], m_ref[...], v_ref[...])
        d_ref[...] = delta
        mo_ref[...] = m2
        vo_ref[...] = v2

    return pl.pallas_call(
        body, name="adamw_small", out_shape=[jax.ShapeDtypeStruct(g.shape, F32)] * 3,
        compiler_params=_cparams(),
    )(g, w, m, v)


def _pack(arrays):
    flat = jnp.concatenate([a.reshape(-1) for a in arrays])
    pad = (-flat.shape[0]) % 1024
    return jnp.concatenate([flat, jnp.zeros((pad,), F32)]).reshape(-1, 128)


def _unpack(packed, shapes):
    flat = packed.reshape(-1)
    out, off = [], 0
    for s in shapes:
        n = 1
        for d in s:
            n *= d
        out.append(flat[off:off + n].reshape(s))
        off += n
    return out


SMALL_NAMES = ("norm1_w", "sgu_ln_w", "sgu_ln_b", "sgu_w_spatial", "sgu_b_spatial", "sc_conv_w", "dn_conv_w",
               "dn_a_log", "dn_dt_bias", "dn_norm_w", "gla_w_gate2", "gla_gate_bias", "gla_norm_w", "norm2_w",
               "final_norm_w")
SHARDED_SMALL = ("sc_conv_w", "dn_conv_w", "gla_w_gate2")
BIG_NAMES = ("w_in", "w_out", "w_gate_up", "w_down")
WEIGHT_ORDER = ("norm1_w", "w_in", "sgu_ln_w", "sgu_ln_b", "sgu_w_spatial", "sgu_b_spatial", "sc_conv_w", "dn_conv_w",
                "dn_a_log", "dn_dt_bias", "dn_norm_w", "gla_w_gate2", "gla_gate_bias", "gla_norm_w", "w_out",
                "norm2_w", "w_gate_up", "w_down", "final_norm_w")


def _cols_from_shards(g):
    n, l, r, c = g.shape
    return jnp.transpose(g, (1, 2, 0, 3)).reshape(l, r, n * c)


def _rows_from_shards(g):
    n, l, r, c = g.shape
    return jnp.transpose(g, (1, 0, 2, 3)).reshape(l, n * r, c)


def _cols_to_shards(full):
    l, r, c4 = full.shape
    return jnp.transpose(full.reshape(l, r, N_CHIPS, c4 // N_CHIPS), (2, 0, 1, 3))


def _rows_to_shards(full):
    l, r4, c = full.shape
    return jnp.transpose(full.reshape(l, N_CHIPS, r4 // N_CHIPS, c), (1, 0, 2, 3))


def kernel(x, norm1_w, w_in, sgu_ln_w, sgu_ln_b, sgu_w_spatial, sgu_b_spatial, sc_conv_w, dn_conv_w, dn_a_log, dn_dt_bias, dn_norm_w, gla_w_gate2, gla_gate_bias, gla_norm_w, w_out, norm2_w, w_gate_up, w_down, final_norm_w, loss_target, m_norm1_w, m_w_in, m_sgu_ln_w, m_sgu_ln_b, m_sgu_w_spatial, m_sgu_b_spatial, m_sc_conv_w, m_dn_conv_w, m_dn_a_log, m_dn_dt_bias, m_dn_norm_w, m_gla_w_gate2, m_gla_gate_bias, m_gla_norm_w, m_w_out, m_norm2_w, m_w_gate_up, m_w_down, m_final_norm_w, v_norm1_w, v_w_in, v_sgu_ln_w, v_sgu_ln_b, v_sgu_w_spatial, v_sgu_b_spatial, v_sc_conv_w, v_dn_conv_w, v_dn_a_log, v_dn_dt_bias, v_dn_norm_w, v_gla_w_gate2, v_gla_gate_bias, v_gla_norm_w, v_w_out, v_norm2_w, v_w_gate_up, v_w_down, v_final_norm_w):
    w = dict(norm1_w=norm1_w, w_in=w_in, sgu_ln_w=sgu_ln_w, sgu_ln_b=sgu_ln_b, sgu_w_spatial=sgu_w_spatial,
             sgu_b_spatial=sgu_b_spatial, sc_conv_w=sc_conv_w, dn_conv_w=dn_conv_w, dn_a_log=dn_a_log,
             dn_dt_bias=dn_dt_bias, dn_norm_w=dn_norm_w, gla_w_gate2=gla_w_gate2, gla_gate_bias=gla_gate_bias,
             gla_norm_w=gla_norm_w, w_out=w_out, norm2_w=norm2_w, w_gate_up=w_gate_up, w_down=w_down,
             final_norm_w=final_norm_w)
    m = dict(norm1_w=m_norm1_w, w_in=m_w_in, sgu_ln_w=m_sgu_ln_w, sgu_ln_b=m_sgu_ln_b, sgu_w_spatial=m_sgu_w_spatial,
             sgu_b_spatial=m_sgu_b_spatial, sc_conv_w=m_sc_conv_w, dn_conv_w=m_dn_conv_w, dn_a_log=m_dn_a_log,
             dn_dt_bias=m_dn_dt_bias, dn_norm_w=m_dn_norm_w, gla_w_gate2=m_gla_w_gate2,
             gla_gate_bias=m_gla_gate_bias, gla_norm_w=m_gla_norm_w, w_out=m_w_out, norm2_w=m_norm2_w,
             w_gate_up=m_w_gate_up, w_down=m_w_down, final_norm_w=m_final_norm_w)
    v = dict(norm1_w=v_norm1_w, w_in=v_w_in, sgu_ln_w=v_sgu_ln_w, sgu_ln_b=v_sgu_ln_b, sgu_w_spatial=v_sgu_w_spatial,
             sgu_b_spatial=v_sgu_b_spatial, sc_conv_w=v_sc_conv_w, dn_conv_w=v_dn_conv_w, dn_a_log=v_dn_a_log,
             dn_dt_bias=v_dn_dt_bias, dn_norm_w=v_dn_norm_w, gla_w_gate2=v_gla_w_gate2,
             gla_gate_bias=v_gla_gate_bias, gla_norm_w=v_gla_norm_w, w_out=v_w_out, norm2_w=v_norm2_w,
             w_gate_up=v_w_gate_up, w_down=v_w_down, final_norm_w=v_final_norm_w)
    chip = 2 * lax.axis_index("x") + lax.axis_index("y")

    shards = [w[n].astype(MXU_DTYPE) for n in BIG_NAMES] + [w[n] for n in SHARDED_SMALL]
    gathered = _gather_weights(shards)
    full_in = _cols_from_shards(gathered[0])
    big = dict(
        w_in=[_pad_w_in(full_in[l]) for l in range(DEPTH)],
        w_out=_rows_from_shards(gathered[1]),
        w_gate_up=_cols_from_shards(gathered[2]),
        w_down=_rows_from_shards(gathered[3]),
    )
    small = {n: w[n] for n in SMALL_NAMES if n not in SHARDED_SMALL}
    for j, n in enumerate(SHARDED_SMALL):
        small[n] = _cols_from_shards(gathered[4 + j])

    loss_tile, grad_x, gbig, gsmall = _local_step(x[0], loss_target[0], big, small)

    pieces = [
        _cols_to_shards(jnp.stack([_unpad_w_in(g) for g in gbig["w_in"]])),
        _rows_to_shards(jnp.stack(gbig["w_out"])),
        _cols_to_shards(jnp.stack(gbig["w_gate_up"])),
        _rows_to_shards(jnp.stack(gbig["w_down"])),
    ]
    small_list = [gsmall[n] for n in SMALL_NAMES] + [loss_tile[0:1, 0]]
    small_shapes = [a.shape for a in small_list]
    contrib = _exchange_grads(pieces, _pack(small_list))

    out_g, out_d, out_m, out_v = {}, {}, {}, {}
    for j, n in enumerate(BIG_NAMES):
        out_g[n], out_d[n], out_m[n], out_v[n] = _adamw_big(contrib[j], w[n], m[n], v[n], "adamw_" + n)
    summed = _unpack(_sum_small(contrib[len(BIG_NAMES)]), small_shapes)
    loss = summed[-1][0]
    for n, g in zip(SMALL_NAMES, summed[:-1]):
        if n in SHARDED_SMALL:
            cols = g.shape[-1] // N_CHIPS
            g = lax.dynamic_slice_in_dim(g, chip * cols, cols, axis=g.ndim - 1)
        out_g[n] = g
    shapes = [out_g[n].shape for n in SMALL_NAMES]
    d_p, m_p, v_p = _adamw_small(_pack([out_g[n] for n in SMALL_NAMES]), _pack([w[n] for n in SMALL_NAMES]),
                                 _pack([m[n] for n in SMALL_NAMES]), _pack([v[n] for n in SMALL_NAMES]))
    for n, d_, m_, v_ in zip(SMALL_NAMES, _unpack(d_p, shapes), _unpack(m_p, shapes), _unpack(v_p, shapes)):
        out_d[n], out_m[n], out_v[n] = d_, m_, v_

    return (loss, grad_x[None], *[out_g[n] for n in WEIGHT_ORDER], *[out_d[n] for n in WEIGHT_ORDER],
            *[out_m[n] for n in WEIGHT_ORDER], *[out_v[n] for n in WEIGHT_ORDER])
```

```python
import functools

import jax
import jax.numpy as jnp
from jax import lax
from jax.experimental import pallas as pl
from jax.experimental.pallas import tpu as pltpu

F32 = jnp.float32
BF16 = jnp.bfloat16
MXU_DTYPE = jnp.bfloat16
GRAD_WIRE_DTYPE = jnp.bfloat16
HI = lax.Precision.HIGHEST
MESH = pl.DeviceIdType.MESH

D_MODEL = 1024
DEPTH = 2
GROUP = 256
HEADS = 4
HEAD_DIM = 64
SGU_CHUNK = 128
SCAN_CHUNK = 64
D_FF = 2816
EPS = 1e-6
IN_COLS = 3352
P_COLS = 3584
HALO = 8
N_CHIPS = 4
N_DEV = 8
VMEM_LIMIT = 48 * 1024 * 1024

ADAM_LR = 0.001
ADAM_B1 = 0.9
ADAM_B2 = 0.999
ADAM_EPS = 1e-08
ADAM_WD = 0.01
ADAM_STEP = 10

(COL_AU, COL_AV, COL_BB, COL_BC, COL_BH, COL_CQ, COL_CK, COL_CV, COL_CZ,
 COL_DQ, COL_DK, COL_DV, COL_DZ) = range(13)
COL128_SMALL_C = 26
COL128_SMALL_D = 27


def _cparams(sem=None):
    return pltpu.CompilerParams(dimension_semantics=sem, vmem_limit_bytes=VMEM_LIMIT)


def _iota(shape, dim):
    return lax.broadcasted_iota(jnp.int32, shape, dim)


def _dg(a, b, ca, cb, prec=None):
    return lax.dot_general(a, b, (((ca,), (cb,)), ((), ())), preferred_element_type=F32, precision=prec)


@functools.partial(jax.custom_vjp, nondiff_argnums=(2, 3))
def bdot(a, b, ca, cb):
    return _dg(a.astype(MXU_DTYPE), b.astype(MXU_DTYPE), ca, cb)


def _bdot_fwd(a, b, ca, cb):
    return bdot(a, b, ca, cb), (a, b)


def _bdot_bwd(ca, cb, res, g):
    a, b = res
    if ca == 1:
        da = bdot(g, b, 1, 1 if cb == 0 else 0)
    else:
        da = bdot(b, g, 1 if cb == 0 else 0, 1)
    if cb == 0:
        db = bdot(a, g, 0, 0) if ca == 1 else bdot(a, g, 1, 0)
    else:
        db = bdot(g, a, 0, 0) if ca == 1 else bdot(g, a, 0, 1)
    return da, db


bdot.defvjp(_bdot_fwd, _bdot_bwd)


def _pieces(a, n):
    out, r = [], a
    for i in range(n):
        p = r.astype(MXU_DTYPE)
        out.append(p)
        if i + 1 < n:
            r = r - p.astype(F32)
    return out


def _mdot_impl(a, b, ca, cb, sa, sb):
    total = None
    for i, x in enumerate(_pieces(a, sa)):
        for j, y in enumerate(_pieces(b, sb)):
            if i + j < max(sa, sb):
                t = _dg(x, y, ca, cb)
                total = t if total is None else total + t
    return total


@functools.partial(jax.custom_vjp, nondiff_argnums=(2, 3, 4, 5))
def mdot(a, b, ca, cb, sa, sb):
    return _mdot_impl(a, b, ca, cb, sa, sb)


def _mdot_fwd(a, b, ca, cb, sa, sb):
    return _mdot_impl(a, b, ca, cb, sa, sb), (a, b)


def _mdot_bwd(ca, cb, sa, sb, res, g):
    a, b = res
    ga, gb = (3 if sb == 1 else 2), (3 if sa == 1 else 2)
    if sa == 1:
        da = jnp.zeros_like(a)
    elif ca == 1:
        da = mdot(g, b, 1, 1 if cb == 0 else 0, ga, sb)
    else:
        da = mdot(b, g, 1 if cb == 0 else 0, 1, sb, ga)
    if sb == 1:
        db = jnp.zeros_like(b)
    elif cb == 0:
        db = mdot(a, g, 0, 0, sa, gb) if ca == 1 else mdot(a, g, 1, 0, sa, gb)
    else:
        db = mdot(g, a, 0, 0, gb, sa) if ca == 1 else mdot(g, a, 0, 1, gb, sa)
    return da, db


mdot.defvjp(_mdot_fwd, _mdot_bwd)


def mask_r(a, m, ca=1, cb=0):
    return mdot(a, m, ca, cb, 3, 1)


def mask_l(m, b, ca=1, cb=0):
    return mdot(m, b, ca, cb, 1, 3)


def ddot(a, b, ca=1, cb=0):
    return mdot(a, b, ca, cb, 2, 2)


def _head_mask(h):
    return ((_iota((1, GROUP), 1) >> 6) == h).astype(F32)


def _block_diag_mask():
    return ((_iota((GROUP, GROUP), 0) >> 6) == (_iota((GROUP, GROUP), 1) >> 6)).astype(F32)


def _expand_mat(offset):
    return ((_iota((128, GROUP), 0) - offset) == (_iota((128, GROUP), 1) >> 6)).astype(F32)


def _tril(n, strict=False):
    r, c = _iota((n, n), 0), _iota((n, n), 1)
    return (r > c) if strict else (r >= c)


def _row_pick(x, row):
    return jnp.sum(jnp.where(_iota(x.shape, 0) == row, x, 0.0), axis=0, keepdims=True)


def _shift_rows_impl(x, halo, j):
    n = x.shape[0]
    r = _iota(x.shape, 0)
    top = jnp.concatenate([pltpu.roll(halo, j, 0), jnp.zeros((n - HALO, x.shape[1]), x.dtype)], axis=0)
    return jnp.where(r >= j, pltpu.roll(x, j, 0), top)


def _mxu_round(a):
    return a.astype(MXU_DTYPE).astype(F32)


@functools.partial(jax.custom_vjp, nondiff_argnums=(3,))
def _causal_conv(x, halo, w, width):
    xb, hb, wb = _mxu_round(x), _mxu_round(halo), _mxu_round(w)
    out = xb * _row_pick(wb, width - 1)
    for j in range(1, width):
        out = out + _shift_rows_impl(xb, hb, j) * _row_pick(wb, width - 1 - j)
    return out


def _causal_conv_fwd(x, halo, w, width):
    return _causal_conv(x, halo, w, width), (x, halo, w)


def _causal_conv_bwd(width, res, g):
    x, halo, w = res
    xb, hb, wb, gb = _mxu_round(x), _mxu_round(halo), _mxu_round(w), _mxu_round(g)
    n = g.shape[0]
    rows, rows8 = _iota(g.shape, 0), _iota(halo.shape, 0)
    dx = gb * _row_pick(wb, width - 1)
    dh = jnp.zeros_like(halo)
    dw = jnp.where(rows8 == width - 1, jnp.sum(xb * gb, axis=0, keepdims=True), 0.0)
    for j in range(1, width):
        gj = gb * _row_pick(wb, width - 1 - j)
        dx = dx + jnp.where(rows < n - j, pltpu.roll(gj, n - j, 0), 0.0)
        dh = dh + jnp.where(rows8 >= HALO - j, pltpu.roll(gj[0:HALO], HALO - j, 0), 0.0)
        tap = jnp.sum(_shift_rows_impl(xb, hb, j) * gb, axis=0, keepdims=True)
        dw = dw + jnp.where(rows8 == width - 1 - j, tap, 0.0)
    return dx, dh, dw


_causal_conv.defvjp(_causal_conv_fwd, _causal_conv_bwd)


def _head_sum(x, bd):
    return mask_r(x, bd)


def _softplus(x):
    return jnp.maximum(x, 0.0) + jnp.log1p(jnp.exp(-jnp.abs(x)))


def _log_sigmoid(x):
    return -_softplus(-x)


def _silu(x):
    return x * jax.nn.sigmoid(x)


def _head_rmsnorm_gate(o, nw, z, bd):
    ms = _head_sum(o * o, bd) * (1.0 / HEAD_DIM)
    return o * lax.rsqrt(ms + EPS) * nw * _silu(z)


def _sgu_chunk(pu, pv, ln_w, ln_b, ws0, ws1, ws2, ws3, bs_t):
    u = jax.nn.gelu(pu)
    g = jax.nn.gelu(pv)
    mu = jnp.mean(g, axis=-1, keepdims=True)
    var = jnp.mean(jnp.square(g - mu), axis=-1, keepdims=True)
    v = (g - mu) * lax.rsqrt(var + EPS) * ln_w + ln_b
    keep = _tril(SGU_CHUNK)
    mixed = mask_r(bs_t, _expand_mat(0))
    for h, ws in enumerate((ws0, ws1, ws2, ws3)):
        mixed = mixed + _head_mask(h) * bdot(jnp.where(keep, ws, 0.0), v, 1, 0)
    return u * mixed


def _sc_chunk(pb, pc, ph, halo_c, halo_h, cw):
    return pb * _causal_conv(pc * ph, halo_c * halo_h, cw, 3)


def _neumann_inverses(lows):
    n = lows[0].shape[0]
    eye = (_iota((n, n), 0) == _iota((n, n), 1)).astype(F32)
    a = [-low for low in lows]
    t = [eye + x for x in a]
    for _ in range(5):
        a = [ddot(x, x) for x in a]
        t = [ti + ddot(ti, ai) for ti, ai in zip(t, a)]
    return t


@jax.custom_vjp
def _saved_inverse(low, inv):
    return inv


def _saved_inverse_fwd(low, inv):
    return inv, inv


def _saved_inverse_bwd(inv, g):
    return -ddot(ddot(inv, g, 0, 0), inv, 1, 1), jnp.zeros_like(inv)


_saved_inverse.defvjp(_saved_inverse_fwd, _saved_inverse_bwd)


def _chunk_tril(rows):
    r, c = _iota((rows, rows), 0), _iota((rows, rows), 1)
    return ((r >> 6) == (c >> 6)) & (r >= c)


def _dn_block(pq, pk, pv, hq, hk, hv, small, pz, cwq, cwk, cwv, a_log, dt_bias, nw, state, saved_inv=None):
    c = SCAN_CHUNK
    rows = pq.shape[0]
    bd = _block_diag_mask()
    q = _silu(_causal_conv(pq, hq, cwq, 4))
    k = _silu(_causal_conv(pk, hk, cwk, 4))
    v = _silu(_causal_conv(pv, hv, cwv, 4))
    q = q * lax.rsqrt(_head_sum(q * q, bd) + EPS) * (HEAD_DIM ** -0.5)
    k = k * lax.rsqrt(_head_sum(k * k, bd) + EPS)
    lane = _iota((1, 128), 1)
    g = jnp.where(lane < HEADS, -jnp.exp(a_log) * _softplus(small + dt_bias), 0.0)
    beta_b = mask_r(jax.nn.sigmoid(small), _expand_mat(HEADS))
    gc_all = mask_l(_chunk_tril(rows).astype(F32), g)
    gcb_all = mask_r(gc_all, _expand_mat(0))
    kb_all = k * beta_b
    vb_all = v * beta_b
    kbe_all = kb_all * jnp.exp(gcb_all)
    qg_all = q * jnp.exp(gcb_all)
    causal, strict = _tril(c), _tril(c, strict=True)
    nc = rows // c
    pairs = [(ci, h) for ci in range(nc) for h in range(HEADS)]
    sls = [slice(ci * c, (ci + 1) * c) for ci in range(nc)]
    decays, lows, attns = [], [], []
    for ci, h in pairs:
        gc = gc_all[sls[ci]]
        onehot = (_iota((c, 128), 1) == h).astype(F32)
        col = mask_l(onehot, gc, 1, 1)
        row = jnp.sum(gc * onehot, axis=1, keepdims=True)
        decays.append(jnp.exp(jnp.where(causal, row - col, -jnp.inf)))
    for j, (ci, h) in enumerate(pairs):
        mh = _head_mask(h)
        k_c = k[sls[ci]]
        lows.append(jnp.where(strict, bdot(kb_all[sls[ci]] * mh, k_c, 1, 1) * decays[j], 0.0))
        attns.append(bdot(q[sls[ci]] * mh, k_c, 1, 1) * decays[j])
    if saved_inv is None:
        invs = _neumann_inverses(lows)
    else:
        invs = [_saved_inverse(low, s) for low, s in zip(lows, saved_inv)]
    us, ws = [], []
    for ci in range(nc):
        u = jnp.zeros((c, GROUP), F32)
        w = jnp.zeros((c, GROUP), F32)
        for h in range(HEADS):
            mh = _head_mask(h)
            u = u + mh * ddot(invs[ci * HEADS + h], vb_all[sls[ci]])
            w = w + mh * ddot(invs[ci * HEADS + h], kbe_all[sls[ci]])
        us.append(u)
        ws.append(w)
    outs = []
    for ci in range(nc):
        gc_b = gcb_all[sls[ci]]
        gc_last_b = _row_pick(gc_b, c - 1)
        v_new = us[ci] - bdot(ws[ci], state, 1, 0)
        o = bdot(qg_all[sls[ci]], state, 1, 0)
        for h in range(HEADS):
            o = o + _head_mask(h) * bdot(attns[ci * HEADS + h], v_new, 1, 0)
        k_dec = k[sls[ci]] * jnp.exp(gc_last_b - gc_b)
        state = state * jnp.exp(gc_last_b) + bd * bdot(k_dec, v_new, 0, 0)
        outs.append(o)
    o = jnp.concatenate(outs, axis=0)
    return _head_rmsnorm_gate(o, nw, pz, bd), state, invs


def _gla_chunk(pq, pk, pv, small, pz, w2, gbias, nw, state_t):
    c = SCAN_CHUNK
    rows = pq.shape[0]
    nc = rows // c
    sls = [slice(ci * c, (ci + 1) * c) for ci in range(nc)]
    bd = _block_diag_mask()
    log_a = _log_sigmoid(bdot(small, w2, 1, 0) + gbias) * (1.0 / 16.0)
    gcum = mask_l(_chunk_tril(rows).astype(F32), log_a)
    r, s = _iota((rows, rows), 0), _iota((rows, rows), 1)
    base = (r >> 6) << 6
    g_mid = mask_l((s == base + c // 2).astype(F32), gcum)
    g_last = mask_l((s == base + c - 1).astype(F32), gcum)
    q = pq * (HEAD_DIM ** -0.5)
    qa = q * jnp.exp(gcum - g_mid)
    ka = pk * jnp.exp(g_mid - gcum)
    qg = q * jnp.exp(gcum)
    k_last = pk * jnp.exp(g_last - gcum)
    causal = _tril(c)
    attns = [jnp.where(causal, bdot(qa[sls[ci]] * _head_mask(h), ka[sls[ci]], 1, 1), 0.0)
             for ci in range(nc) for h in range(HEADS)]
    intra = []
    for ci in range(nc):
        o = jnp.zeros((c, GROUP), F32)
        for h in range(HEADS):
            o = o + _head_mask(h) * bdot(attns[ci * HEADS + h], pv[sls[ci]], 1, 0)
        intra.append(o)
    outs = []
    for ci in range(nc):
        outs.append(intra[ci] + bdot(qg[sls[ci]], state_t, 1, 1))
        dec = _row_pick(g_last[sls[ci]], 0)
        state_t = state_t * jnp.exp(dec) + bd * bdot(pv[sls[ci]], k_last[sls[ci]], 0, 0)
    o = jnp.concatenate(outs, axis=0)
    return _head_rmsnorm_gate(o, nw, pz, bd), state_t


def _col_spec(rows, group, rev_n=None):
    if rev_n is None:
        return pl.BlockSpec((rows, GROUP), lambda i: (i, group))
    return pl.BlockSpec((rows, GROUP), lambda i: (rev_n - 1 - i, group))


def _small_spec(rows, group128, rev_n=None):
    if rev_n is None:
        return pl.BlockSpec((rows, 128), lambda i: (i, group128))
    return pl.BlockSpec((rows, 128), lambda i: (rev_n - 1 - i, group128))


def _halo_spec(rows, group, rev_n=None):
    per = rows // HALO
    if rev_n is None:
        return pl.BlockSpec((HALO, GROUP), lambda i: (jnp.maximum(i * per - 1, 0), group))
    return pl.BlockSpec((HALO, GROUP), lambda i: (jnp.maximum((rev_n - 1 - i) * per - 1, 0), group))


def _full_spec(shape):
    nd = len(shape)
    return pl.BlockSpec(shape, lambda i: (0,) * nd)


def _out_rows_spec(rows, lanes, rev_n=None):
    if rev_n is None:
        return pl.BlockSpec((rows, lanes), lambda i: (i, 0))
    return pl.BlockSpec((rows, lanes), lambda i: (rev_n - 1 - i, 0))


def _sgu_fwd(p, ln_w, ln_b, ws, bs_t):
    t = p.shape[0]
    n = t // SGU_CHUNK

    def body(pu_ref, pv_ref, lw_ref, lb_ref, ws_ref, bs_ref, y_ref):
        y = _sgu_chunk(pu_ref[...], pv_ref[...], lw_ref[...], lb_ref[...],
                       ws_ref[0], ws_ref[1], ws_ref[2], ws_ref[3], bs_ref[...])
        y_ref[...] = y.astype(y_ref.dtype)

    return pl.pallas_call(
        body, name="sgu_fwd", grid=(n,),
        in_specs=[_col_spec(SGU_CHUNK, COL_AU), _col_spec(SGU_CHUNK, COL_AV), _full_spec((1, GROUP)),
                  _full_spec((1, GROUP)), _full_spec((HEADS, SGU_CHUNK, SGU_CHUNK)), _full_spec((SGU_CHUNK, 128))],
        out_specs=_out_rows_spec(SGU_CHUNK, GROUP),
        out_shape=jax.ShapeDtypeStruct((t, GROUP), BF16),
        compiler_params=_cparams(("arbitrary",)),
    )(p, p, ln_w, ln_b, ws, bs_t)


def _sgu_bwd(p, dmix, ln_w, ln_b, ws, bs_t):
    t = p.shape[0]
    n = t // SGU_CHUNK

    def body(pu_ref, pv_ref, dy_ref, lw_ref, lb_ref, ws_ref, bs_ref,
             dpu_ref, dpv_ref, dlw_ref, dlb_ref, dws_ref, dbs_ref):
        args = (pu_ref[...], pv_ref[...], lw_ref[...], lb_ref[...],
                ws_ref[0], ws_ref[1], ws_ref[2], ws_ref[3], bs_ref[...])
        _, vjp = jax.vjp(_sgu_chunk, *args)
        dpu, dpv, dlw, dlb, d0, d1, d2, d3, dbs = vjp(dy_ref[...])
        dpu_ref[...] = dpu.astype(dpu_ref.dtype)
        dpv_ref[...] = dpv.astype(dpv_ref.dtype)

        @pl.when(pl.program_id(0) == 0)
        def _():
            dlw_ref[...] = jnp.zeros_like(dlw_ref)
            dlb_ref[...] = jnp.zeros_like(dlb_ref)
            dws_ref[...] = jnp.zeros_like(dws_ref)
            dbs_ref[...] = jnp.zeros_like(dbs_ref)

        dlw_ref[...] += dlw
        dlb_ref[...] += dlb
        for h, d in enumerate((d0, d1, d2, d3)):
            dws_ref[h] += d
        dbs_ref[...] += dbs

    return pl.pallas_call(
        body, name="sgu_bwd", grid=(n,),
        in_specs=[_col_spec(SGU_CHUNK, COL_AU), _col_spec(SGU_CHUNK, COL_AV),
                  pl.BlockSpec((SGU_CHUNK, GROUP), lambda i: (i, 0)),
                  _full_spec((1, GROUP)), _full_spec((1, GROUP)), _full_spec((HEADS, SGU_CHUNK, SGU_CHUNK)),
                  _full_spec((SGU_CHUNK, 128))],
        out_specs=[_out_rows_spec(SGU_CHUNK, GROUP), _out_rows_spec(SGU_CHUNK, GROUP), _full_spec((1, GROUP)),
                   _full_spec((1, GROUP)), _full_spec((HEADS, SGU_CHUNK, SGU_CHUNK)), _full_spec((SGU_CHUNK, 128))],
        out_shape=[jax.ShapeDtypeStruct((t, GROUP), BF16), jax.ShapeDtypeStruct((t, GROUP), BF16),
                   jax.ShapeDtypeStruct((1, GROUP), F32), jax.ShapeDtypeStruct((1, GROUP), F32),
                   jax.ShapeDtypeStruct((HEADS, SGU_CHUNK, SGU_CHUNK), F32),
                   jax.ShapeDtypeStruct((SGU_CHUNK, 128), F32)],
        compiler_params=_cparams(("arbitrary",)),
    )(p, p, dmix, ln_w, ln_b, ws, bs_t)


SC_ROWS = 256


def _first_block_zero(halo, first):
    return jnp.where(first, 0.0, halo)


def _sc_fwd(p, cw):
    t = p.shape[0]
    n = t // SC_ROWS

    def body(pb_ref, pc_ref, ph_ref, hc_ref, hh_ref, cw_ref, y_ref):
        first = pl.program_id(0) == 0
        y = _sc_chunk(pb_ref[...], pc_ref[...], ph_ref[...], _first_block_zero(hc_ref[...], first),
                      _first_block_zero(hh_ref[...], first), cw_ref[...])
        y_ref[...] = y.astype(y_ref.dtype)

    return pl.pallas_call(
        body, name="sc_fwd", grid=(n,),
        in_specs=[_col_spec(SC_ROWS, COL_BB), _col_spec(SC_ROWS, COL_BC), _col_spec(SC_ROWS, COL_BH),
                  _halo_spec(SC_ROWS, COL_BC), _halo_spec(SC_ROWS, COL_BH), _full_spec((HALO, GROUP))],
        out_specs=_out_rows_spec(SC_ROWS, GROUP),
        out_shape=jax.ShapeDtypeStruct((t, GROUP), BF16),
        compiler_params=_cparams(("arbitrary",)),
    )(p, p, p, p, p, cw)


def _add_halo_grad(d, carry):
    return d + jnp.concatenate([jnp.zeros((d.shape[0] - HALO, d.shape[1]), d.dtype), carry], axis=0)


def _sc_bwd(p, dmix, cw):
    t = p.shape[0]
    n = t // SC_ROWS

    def body(pb_ref, pc_ref, ph_ref, hc_ref, hh_ref, dy_ref, cw_ref,
             dpb_ref, dpc_ref, dph_ref, dcw_ref, carry_c, carry_h):
        i = pl.program_id(0)
        first = i == n - 1

        @pl.when(i == 0)
        def _():
            carry_c[...] = jnp.zeros_like(carry_c)
            carry_h[...] = jnp.zeros_like(carry_h)
            dcw_ref[...] = jnp.zeros_like(dcw_ref)

        args = (pb_ref[...], pc_ref[...], ph_ref[...], _first_block_zero(hc_ref[...], first),
                _first_block_zero(hh_ref[...], first), cw_ref[...])
        _, vjp = jax.vjp(_sc_chunk, *args)
        dpb, dpc, dph, dhc, dhh, dcw = vjp(dy_ref[...])
        dpb_ref[...] = dpb.astype(dpb_ref.dtype)
        dpc_ref[...] = _add_halo_grad(dpc, carry_c[...]).astype(dpc_ref.dtype)
        dph_ref[...] = _add_halo_grad(dph, carry_h[...]).astype(dph_ref.dtype)
        carry_c[...] = dhc
        carry_h[...] = dhh
        dcw_ref[...] += dcw

    return pl.pallas_call(
        body, name="sc_bwd", grid=(n,),
        in_specs=[_col_spec(SC_ROWS, COL_BB, n), _col_spec(SC_ROWS, COL_BC, n), _col_spec(SC_ROWS, COL_BH, n),
                  _halo_spec(SC_ROWS, COL_BC, n), _halo_spec(SC_ROWS, COL_BH, n),
                  pl.BlockSpec((SC_ROWS, GROUP), lambda i: (n - 1 - i, 1)), _full_spec((HALO, GROUP))],
        out_specs=[_out_rows_spec(SC_ROWS, GROUP, n)] * 3 + [_full_spec((HALO, GROUP))],
        out_shape=[jax.ShapeDtypeStruct((t, GROUP), BF16)] * 3 + [jax.ShapeDtypeStruct((HALO, GROUP), F32)],
        scratch_shapes=[pltpu.VMEM((HALO, GROUP), F32), pltpu.VMEM((HALO, GROUP), F32)],
        compiler_params=_cparams(("arbitrary",)),
    )(p, p, p, p, p, dmix, cw)


SCAN_STEP_CHUNKS = 4
SCAN_ROWS = SCAN_STEP_CHUNKS * SCAN_CHUNK


def _dn_fwd(p, cw3, a_log, dt_bias, nw):
    t = p.shape[0]
    r = SCAN_ROWS
    n = t // r

    def body(pq_ref, pk_ref, pv_ref, hq_ref, hk_ref, hv_ref, sm_ref, pz_ref, cw_ref, al_ref, dt_ref, nw_ref,
             y_ref, ck_ref, inv_ref, state):
        first = pl.program_id(0) == 0

        @pl.when(first)
        def _():
            state[...] = jnp.zeros_like(state)

        s_in = state[...]
        ck_ref[0] = s_in
        y, s_out, invs = _dn_block(pq_ref[...], pk_ref[...], pv_ref[...], _first_block_zero(hq_ref[...], first),
                                   _first_block_zero(hk_ref[...], first), _first_block_zero(hv_ref[...], first),
                                   sm_ref[...], pz_ref[...], cw_ref[0], cw_ref[1], cw_ref[2],
                                   al_ref[...], dt_ref[...], nw_ref[...], s_in)
        y_ref[...] = y.astype(y_ref.dtype)
        state[...] = s_out
        for j, inv in enumerate(invs):
            inv_ref[j] = inv

    nh = SCAN_STEP_CHUNKS * HEADS
    return pl.pallas_call(
        body, name="dn_fwd", grid=(n,),
        in_specs=[_col_spec(r, COL_CQ), _col_spec(r, COL_CK), _col_spec(r, COL_CV),
                  _halo_spec(r, COL_CQ), _halo_spec(r, COL_CK), _halo_spec(r, COL_CV),
                  _small_spec(r, COL128_SMALL_C), _col_spec(r, COL_CZ), _full_spec((3, HALO, GROUP)),
                  _full_spec((1, 128)), _full_spec((1, 128)), _full_spec((1, GROUP))],
        out_specs=[_out_rows_spec(r, GROUP), pl.BlockSpec((1, GROUP, GROUP), lambda i: (i, 0, 0)),
                   pl.BlockSpec((nh, SCAN_CHUNK, SCAN_CHUNK), lambda i: (i, 0, 0))],
        out_shape=[jax.ShapeDtypeStruct((t, GROUP), BF16), jax.ShapeDtypeStruct((n, GROUP, GROUP), F32),
                   jax.ShapeDtypeStruct((n * nh, SCAN_CHUNK, SCAN_CHUNK), F32)],
        scratch_shapes=[pltpu.VMEM((GROUP, GROUP), F32)],
        compiler_params=_cparams(("arbitrary",)),
    )(p, p, p, p, p, p, p, p, cw3, a_log, dt_bias, nw)


def _dn_bwd(p, dmix, states, invs, cw3, a_log, dt_bias, nw):
    t = p.shape[0]
    c = SCAN_ROWS
    n = t // c
    nh = SCAN_STEP_CHUNKS * HEADS

    def body(pq_ref, pk_ref, pv_ref, hq_ref, hk_ref, hv_ref, sm_ref, pz_ref, dy_ref, ck_ref, inv_ref,
             cw_ref, al_ref, dt_ref, nw_ref,
             dpq_ref, dpk_ref, dpv_ref, dsm_ref, dpz_ref, dcw_ref, dal_ref, ddt_ref, dnw_ref,
             dstate, carry):
        i = pl.program_id(0)
        first = i == n - 1

        @pl.when(i == 0)
        def _():
            dstate[...] = jnp.zeros_like(dstate)
            carry[...] = jnp.zeros_like(carry)
            dcw_ref[...] = jnp.zeros_like(dcw_ref)
            dal_ref[...] = jnp.zeros_like(dal_ref)
            ddt_ref[...] = jnp.zeros_like(ddt_ref)
            dnw_ref[...] = jnp.zeros_like(dnw_ref)

        args = (pq_ref[...], pk_ref[...], pv_ref[...], _first_block_zero(hq_ref[...], first),
                _first_block_zero(hk_ref[...], first), _first_block_zero(hv_ref[...], first),
                sm_ref[...], pz_ref[...], cw_ref[0], cw_ref[1], cw_ref[2],
                al_ref[...], dt_ref[...], nw_ref[...], ck_ref[0])
        saved = [inv_ref[j] for j in range(nh)]
        _, vjp = jax.vjp(lambda *a: _dn_block(*a, saved_inv=saved)[:2], *args)
        (dpq, dpk, dpv, dhq, dhk, dhv, dsm, dpz, dcq, dck, dcv, dal, ddt, dnw, dst) = vjp(
            (dy_ref[...], dstate[...]))
        dpq_ref[...] = _add_halo_grad(dpq, carry[0]).astype(dpq_ref.dtype)
        dpk_ref[...] = _add_halo_grad(dpk, carry[1]).astype(dpk_ref.dtype)
        dpv_ref[...] = _add_halo_grad(dpv, carry[2]).astype(dpv_ref.dtype)
        dsm_ref[...] = dsm.astype(dsm_ref.dtype)
        dpz_ref[...] = dpz.astype(dpz_ref.dtype)
        carry[0] = dhq
        carry[1] = dhk
        carry[2] = dhv
        dstate[...] = dst
        dcw_ref[0] += dcq
        dcw_ref[1] += dck
        dcw_ref[2] += dcv
        dal_ref[...] += dal
        ddt_ref[...] += ddt
        dnw_ref[...] += dnw

    return pl.pallas_call(
        body, name="dn_bwd", grid=(n,),
        in_specs=[_col_spec(c, COL_CQ, n), _col_spec(c, COL_CK, n), _col_spec(c, COL_CV, n),
                  _halo_spec(c, COL_CQ, n), _halo_spec(c, COL_CK, n), _halo_spec(c, COL_CV, n),
                  _small_spec(c, COL128_SMALL_C, n), _col_spec(c, COL_CZ, n),
                  pl.BlockSpec((c, GROUP), lambda i: (n - 1 - i, 2)),
                  pl.BlockSpec((1, GROUP, GROUP), lambda i: (n - 1 - i, 0, 0)),
                  pl.BlockSpec((nh, SCAN_CHUNK, SCAN_CHUNK), lambda i: (n - 1 - i, 0, 0)),
                  _full_spec((3, HALO, GROUP)), _full_spec((1, 128)), _full_spec((1, 128)), _full_spec((1, GROUP))],
        out_specs=[_out_rows_spec(c, GROUP, n)] * 3 + [_out_rows_spec(c, 128, n), _out_rows_spec(c, GROUP, n),
                   _full_spec((3, HALO, GROUP)), _full_spec((1, 128)), _full_spec((1, 128)), _full_spec((1, GROUP))],
        out_shape=[jax.ShapeDtypeStruct((t, GROUP), BF16)] * 3 + [
            jax.ShapeDtypeStruct((t, 128), BF16), jax.ShapeDtypeStruct((t, GROUP), BF16),
            jax.ShapeDtypeStruct((3, HALO, GROUP), F32), jax.ShapeDtypeStruct((1, 128), F32),
            jax.ShapeDtypeStruct((1, 128), F32), jax.ShapeDtypeStruct((1, GROUP), F32)],
        scratch_shapes=[pltpu.VMEM((GROUP, GROUP), F32), pltpu.VMEM((3, HALO, GROUP), F32)],
        compiler_params=_cparams(("arbitrary",)),
    )(p, p, p, p, p, p, p, p, dmix, states, invs, cw3, a_log, dt_bias, nw)


def _gla_fwd(p, w2, gbias, nw):
    t = p.shape[0]
    c = SCAN_ROWS
    n = t // c

    def body(pq_ref, pk_ref, pv_ref, sm_ref, pz_ref, w2_ref, gb_ref, nw_ref, y_ref, ck_ref, state):
        @pl.when(pl.program_id(0) == 0)
        def _():
            state[...] = jnp.zeros_like(state)

        s_in = state[...]
        ck_ref[0] = s_in
        y, s_out = _gla_chunk(pq_ref[...], pk_ref[...], pv_ref[...], sm_ref[...], pz_ref[...],
                              w2_ref[...], gb_ref[...], nw_ref[...], s_in)
        y_ref[...] = y.astype(y_ref.dtype)
        state[...] = s_out

    return pl.pallas_call(
        body, name="gla_fwd", grid=(n,),
        in_specs=[_col_spec(c, COL_DQ), _col_spec(c, COL_DK), _col_spec(c, COL_DV),
                  _small_spec(c, COL128_SMALL_D), _col_spec(c, COL_DZ),
                  _full_spec((128, GROUP)), _full_spec((1, GROUP)), _full_spec((1, GROUP))],
        out_specs=[_out_rows_spec(c, GROUP), pl.BlockSpec((1, GROUP, GROUP), lambda i: (i, 0, 0))],
        out_shape=[jax.ShapeDtypeStruct((t, GROUP), BF16), jax.ShapeDtypeStruct((n, GROUP, GROUP), F32)],
        scratch_shapes=[pltpu.VMEM((GROUP, GROUP), F32)],
        compiler_params=_cparams(("arbitrary",)),
    )(p, p, p, p, p, w2, gbias, nw)


def _gla_bwd(p, dmix, states, w2, gbias, nw):
    t = p.shape[0]
    c = SCAN_ROWS
    n = t // c

    def body(pq_ref, pk_ref, pv_ref, sm_ref, pz_ref, dy_ref, ck_ref, w2_ref, gb_ref, nw_ref,
             dpq_ref, dpk_ref, dpv_ref, dsm_ref, dpz_ref, dw2_ref, dgb_ref, dnw_ref, dstate):
        @pl.when(pl.program_id(0) == 0)
        def _():
            dstate[...] = jnp.zeros_like(dstate)
            dw2_ref[...] = jnp.zeros_like(dw2_ref)
            dgb_ref[...] = jnp.zeros_like(dgb_ref)
            dnw_ref[...] = jnp.zeros_like(dnw_ref)

        args = (pq_ref[...], pk_ref[...], pv_ref[...], sm_ref[...], pz_ref[...],
                w2_ref[...], gb_ref[...], nw_ref[...], ck_ref[0])
        _, vjp = jax.vjp(_gla_chunk, *args)
        dpq, dpk, dpv, dsm, dpz, dw2, dgb, dnw, dst = vjp((dy_ref[...], dstate[...]))
        dpq_ref[...] = dpq.astype(dpq_ref.dtype)
        dpk_ref[...] = dpk.astype(dpk_ref.dtype)
        dpv_ref[...] = dpv.astype(dpv_ref.dtype)
        dsm_ref[...] = dsm.astype(dsm_ref.dtype)
        dpz_ref[...] = dpz.astype(dpz_ref.dtype)
        dstate[...] = dst
        dw2_ref[...] += dw2
        dgb_ref[...] += dgb
        dnw_ref[...] += dnw

    return pl.pallas_call(
        body, name="gla_bwd", grid=(n,),
        in_specs=[_col_spec(c, COL_DQ, n), _col_spec(c, COL_DK, n), _col_spec(c, COL_DV, n),
                  _small_spec(c, COL128_SMALL_D, n), _col_spec(c, COL_DZ, n),
                  pl.BlockSpec((c, GROUP), lambda i: (n - 1 - i, 3)),
                  pl.BlockSpec((1, GROUP, GROUP), lambda i: (n - 1 - i, 0, 0)),
                  _full_spec((128, GROUP)), _full_spec((1, GROUP)), _full_spec((1, GROUP))],
        out_specs=[_out_rows_spec(c, GROUP, n)] * 3 + [_out_rows_spec(c, 128, n), _out_rows_spec(c, GROUP, n),
                   _full_spec((128, GROUP)), _full_spec((1, GROUP)), _full_spec((1, GROUP))],
        out_shape=[jax.ShapeDtypeStruct((t, GROUP), BF16)] * 3 + [
            jax.ShapeDtypeStruct((t, 128), BF16), jax.ShapeDtypeStruct((t, GROUP), BF16),
            jax.ShapeDtypeStruct((128, GROUP), F32), jax.ShapeDtypeStruct((1, GROUP), F32),
            jax.ShapeDtypeStruct((1, GROUP), F32)],
        scratch_shapes=[pltpu.VMEM((GROUP, GROUP), F32)],
        compiler_params=_cparams(("arbitrary",)),
    )(p, p, p, p, p, dmix, states, w2, gbias, nw)


def _pick_tile(n, pref):
    for cand in pref:
        if n % cand == 0:
            return cand
    return n


def _matmul(a, b, mode, out_dtype, name, res=None):
    if mode == "nn":
        (m, k), n = a.shape, b.shape[1]
    elif mode == "nt":
        (m, k), n = a.shape, b.shape[0]
    else:
        (k, m), n = a.shape, b.shape[1]
    tm = _pick_tile(m, (512, 256, 128))
    tn = _pick_tile(n, (512, 1408, 256, 128))
    tk = _pick_tile(k, (1024, 1408, 512, 256, 128))
    nk = k // tk
    if mode == "nn":
        a_spec = pl.BlockSpec((tm, tk), lambda i, j, kk: (i, kk))
        b_spec = pl.BlockSpec((tk, tn), lambda i, j, kk: (kk, j))
        dims = (1, 0)
    elif mode == "nt":
        a_spec = pl.BlockSpec((tm, tk), lambda i, j, kk: (i, kk))
        b_spec = pl.BlockSpec((tn, tk), lambda i, j, kk: (j, kk))
        dims = (1, 1)
    else:
        a_spec = pl.BlockSpec((tk, tm), lambda i, j, kk: (kk, i))
        b_spec = pl.BlockSpec((tk, tn), lambda i, j, kk: (kk, j))
        dims = (0, 0)
    o_spec = pl.BlockSpec((tm, tn), lambda i, j, kk: (i, j))
    has_res = res is not None

    def body(*refs):
        if has_res:
            a_ref, b_ref, r_ref, o_ref, acc = refs
        else:
            a_ref, b_ref, o_ref, acc = refs
        kk = pl.program_id(2)

        @pl.when(kk == 0)
        def _():
            acc[...] = jnp.zeros_like(acc)

        acc[...] += _dg(a_ref[...].astype(MXU_DTYPE), b_ref[...].astype(MXU_DTYPE), *dims)

        @pl.when(kk == nk - 1)
        def _():
            out = acc[...]
            if has_res:
                out = out + r_ref[...]
            o_ref[...] = out.astype(o_ref.dtype)

    in_specs = [a_spec, b_spec] + ([o_spec] if has_res else [])
    args = (a, b) + ((res,) if has_res else ())
    return pl.pallas_call(
        body, name=name, grid=(m // tm, n // tn, nk), in_specs=in_specs, out_specs=o_spec,
        out_shape=jax.ShapeDtypeStruct((m, n), out_dtype),
        scratch_shapes=[pltpu.VMEM((tm, tn), F32)],
        compiler_params=_cparams(("parallel", "parallel", "arbitrary")),
    )(*args)


NORM_ROWS = 512


def _rmsnorm_fwd(x, w, name):
    t, d = x.shape

    def body(x_ref, w_ref, o_ref):
        xv = x_ref[...]
        r = lax.rsqrt(jnp.mean(xv * xv, axis=-1, keepdims=True) + EPS)
        o_ref[...] = (xv * r * w_ref[...]).astype(o_ref.dtype)

    return pl.pallas_call(
        body, name=name, grid=(t // NORM_ROWS,),
        in_specs=[pl.BlockSpec((NORM_ROWS, d), lambda i: (i, 0)), _full_spec((1, d))],
        out_specs=pl.BlockSpec((NORM_ROWS, d), lambda i: (i, 0)),
        out_shape=jax.ShapeDtypeStruct((t, d), BF16),
        compiler_params=_cparams(("parallel",)),
    )(x, w)


def _rmsnorm_bwd(x, w, dh, dres, name):
    t, d = x.shape

    def body(x_ref, w_ref, dh_ref, dr_ref, dx_ref, dw_ref):
        @pl.when(pl.program_id(0) == 0)
        def _():
            dw_ref[...] = jnp.zeros_like(dw_ref)

        xv = x_ref[...]
        g = dh_ref[...].astype(F32)
        r = lax.rsqrt(jnp.mean(xv * xv, axis=-1, keepdims=True) + EPS)
        xhat = xv * r
        dw_ref[...] += jnp.sum(g * xhat, axis=0, keepdims=True)
        gx = g * w_ref[...]
        dx = r * (gx - xhat * jnp.mean(gx * xhat, axis=-1, keepdims=True))
        dx_ref[...] = dr_ref[...] + dx

    return pl.pallas_call(
        body, name=name, grid=(t // NORM_ROWS,),
        in_specs=[pl.BlockSpec((NORM_ROWS, d), lambda i: (i, 0)), _full_spec((1, d)),
                  pl.BlockSpec((NORM_ROWS, d), lambda i: (i, 0)), pl.BlockSpec((NORM_ROWS, d), lambda i: (i, 0))],
        out_specs=[pl.BlockSpec((NORM_ROWS, d), lambda i: (i, 0)), _full_spec((1, d))],
        out_shape=[jax.ShapeDtypeStruct((t, d), F32), jax.ShapeDtypeStruct((1, d), F32)],
        compiler_params=_cparams(("arbitrary",)),
    )(x, w, dh, dres)


SWIGLU_ROWS = 128


def _swiglu_fwd(gu):
    t = gu.shape[0]

    def body(gu_ref, o_ref):
        gate = gu_ref[:, :D_FF]
        up = gu_ref[:, D_FF:]
        o_ref[...] = (_silu(gate) * up).astype(o_ref.dtype)

    return pl.pallas_call(
        body, name="swiglu_fwd", grid=(t // SWIGLU_ROWS,),
        in_specs=[pl.BlockSpec((SWIGLU_ROWS, 2 * D_FF), lambda i: (i, 0))],
        out_specs=pl.BlockSpec((SWIGLU_ROWS, D_FF), lambda i: (i, 0)),
        out_shape=jax.ShapeDtypeStruct((t, D_FF), BF16),
        compiler_params=_cparams(("parallel",)),
    )(gu)


def _swiglu_bwd(gu, dact):
    t = gu.shape[0]

    def body(gu_ref, da_ref, o_ref):
        gate = gu_ref[:, :D_FF]
        up = gu_ref[:, D_FF:]
        da = da_ref[...]
        sg = jax.nn.sigmoid(gate)
        o_ref[:, :D_FF] = (da * up * (sg * (1.0 + gate * (1.0 - sg)))).astype(o_ref.dtype)
        o_ref[:, D_FF:] = (da * gate * sg).astype(o_ref.dtype)

    return pl.pallas_call(
        body, name="swiglu_bwd", grid=(t // SWIGLU_ROWS,),
        in_specs=[pl.BlockSpec((SWIGLU_ROWS, 2 * D_FF), lambda i: (i, 0)),
                  pl.BlockSpec((SWIGLU_ROWS, D_FF), lambda i: (i, 0))],
        out_specs=pl.BlockSpec((SWIGLU_ROWS, 2 * D_FF), lambda i: (i, 0)),
        out_shape=jax.ShapeDtypeStruct((t, 2 * D_FF), BF16),
        compiler_params=_cparams(("parallel",)),
    )(gu, dact)


def _loss_head(x, w, target):
    t, d = x.shape

    def fwd(xv, wv, tv):
        r = lax.rsqrt(jnp.mean(xv * xv, axis=-1, keepdims=True) + EPS)
        err = xv * r * wv - tv
        return 0.5 * jnp.sum(jnp.mean(err * err, axis=-1, keepdims=True), axis=0, keepdims=True)

    def body(x_ref, w_ref, t_ref, dx_ref, dw_ref, loss_ref):
        @pl.when(pl.program_id(0) == 0)
        def _():
            dw_ref[...] = jnp.zeros_like(dw_ref)
            loss_ref[...] = jnp.zeros_like(loss_ref)

        loss, vjp = jax.vjp(fwd, x_ref[...], w_ref[...], t_ref[...])
        dx, dw, _ = vjp(jnp.ones((1, 1), F32))
        dx_ref[...] = dx
        dw_ref[...] += dw
        loss_ref[...] += jnp.broadcast_to(loss, loss_ref.shape)

    return pl.pallas_call(
        body, name="loss_head", grid=(t // NORM_ROWS,),
        in_specs=[pl.BlockSpec((NORM_ROWS, d), lambda i: (i, 0)), _full_spec((1, d)),
                  pl.BlockSpec((NORM_ROWS, d), lambda i: (i, 0))],
        out_specs=[pl.BlockSpec((NORM_ROWS, d), lambda i: (i, 0)), _full_spec((1, d)), _full_spec((8, 128))],
        out_shape=[jax.ShapeDtypeStruct((t, d), F32), jax.ShapeDtypeStruct((1, d), F32),
                   jax.ShapeDtypeStruct((8, 128), F32)],
        compiler_params=_cparams(("arbitrary",)),
    )(x, w, target)


def _pad_w_in(w):
    z = lambda n: jnp.zeros((w.shape[0], n), w.dtype)
    return jnp.concatenate([w[:, 0:2048], w[:, 2056:2312], w[:, 2312:3080], w[:, 3096:3352],
                            w[:, 2048:2056], z(120), w[:, 3080:3096], z(112)], axis=1)


def _unpad_w_in(wp):
    return jnp.concatenate([wp[:, 0:2048], wp[:, 3328:3336], wp[:, 2048:2304], wp[:, 2304:3072],
                            wp[:, 3456:3472], wp[:, 3072:3328]], axis=1)


def _pad_rows(a, rows):
    return jnp.concatenate([a, jnp.zeros((rows - a.shape[0],) + a.shape[1:], a.dtype)], axis=0)


def _pad_lanes(a, lanes):
    return jnp.concatenate([a, jnp.zeros(a.shape[:-1] + (lanes - a.shape[-1],), a.dtype)], axis=-1)


def _layer_params(l, small):
    dn_cw = small["dn_conv_w"][l]
    return dict(
        ln_w=small["sgu_ln_w"][l][None], ln_b=small["sgu_ln_b"][l][None],
        ws=small["sgu_w_spatial"][l], bs_t=_pad_lanes(small["sgu_b_spatial"][l].T, 128),
        sc_cw=_pad_rows(small["sc_conv_w"][l], HALO),
        dn_cw=jnp.stack([_pad_rows(dn_cw[:, j * GROUP:(j + 1) * GROUP], HALO) for j in range(3)]),
        dn_al=_pad_lanes(small["dn_a_log"][l][None], 128), dn_dt=_pad_lanes(small["dn_dt_bias"][l][None], 128),
        dn_nw=jnp.tile(small["dn_norm_w"][l][None], (1, HEADS)),
        gla_w2=_pad_rows(small["gla_w_gate2"][l], 128), gla_gb=small["gla_gate_bias"][l][None],
        gla_nw=jnp.tile(small["gla_norm_w"][l][None], (1, HEADS)),
    )


def _local_step(x, target, big, small):
    saved = []
    h = x
    for l in range(DEPTH):
        lp = _layer_params(l, small)
        h1 = _rmsnorm_fwd(h, small["norm1_w"][l][None], "norm1_fwd")
        p = _matmul(h1, big["w_in"][l], "nn", F32, "proj_in")
        y_a = _sgu_fwd(p, lp["ln_w"], lp["ln_b"], lp["ws"], lp["bs_t"])
        y_b = _sc_fwd(p, lp["sc_cw"])
        y_c, st_c, inv_c = _dn_fwd(p, lp["dn_cw"], lp["dn_al"], lp["dn_dt"], lp["dn_nw"])
        y_d, st_d = _gla_fwd(p, lp["gla_w2"], lp["gla_gb"], lp["gla_nw"])
        mix = jnp.concatenate([y_a, y_b, y_c, y_d], axis=1)
        x1 = _matmul(mix, big["w_out"][l], "nn", F32, "proj_out", res=h)
        h2 = _rmsnorm_fwd(x1, small["norm2_w"][l][None], "norm2_fwd")
        gu = _matmul(h2, big["w_gate_up"][l], "nn", F32, "ffn_up")
        act = _swiglu_fwd(gu)
        x2 = _matmul(act, big["w_down"][l], "nn", F32, "ffn_down", res=x1)
        saved.append(dict(x0=h, h1=h1, p=p, st_c=st_c, inv_c=inv_c, st_d=st_d, mix=mix, x1=x1, h2=h2, gu=gu, act=act, lp=lp))
        h = x2

    dx, d_final, loss = _loss_head(h, small["final_norm_w"][None], target)
    gbig = {k: [None] * DEPTH for k in ("w_in", "w_out", "w_gate_up", "w_down")}
    gs = {k: [None] * DEPTH for k in ("norm1_w", "sgu_ln_w", "sgu_ln_b", "sgu_w_spatial", "sgu_b_spatial", "sc_conv_w",
                                     "dn_conv_w", "dn_a_log", "dn_dt_bias", "dn_norm_w", "gla_w_gate2",
                                     "gla_gate_bias", "gla_norm_w", "norm2_w")}
    for l in reversed(range(DEPTH)):
        s = saved[l]
        lp = s["lp"]
        gbig["w_down"][l] = _matmul(s["act"], dx, "tn", GRAD_WIRE_DTYPE, "ffn_down_dw")
        dact = _matmul(dx, big["w_down"][l], "nt", F32, "ffn_down_dx")
        dgu = _swiglu_bwd(s["gu"], dact)
        gbig["w_gate_up"][l] = _matmul(s["h2"], dgu, "tn", GRAD_WIRE_DTYPE, "ffn_up_dw")
        dh2 = _matmul(dgu, big["w_gate_up"][l], "nt", F32, "ffn_up_dx")
        dx1, gs["norm2_w"][l] = _rmsnorm_bwd(s["x1"], small["norm2_w"][l][None], dh2, dx, "norm2_bwd")
        gbig["w_out"][l] = _matmul(s["mix"], dx1, "tn", GRAD_WIRE_DTYPE, "proj_out_dw")
        dmix = _matmul(dx1, big["w_out"][l], "nt", F32, "proj_out_dx")
        p = s["p"]
        dpu, dpv, g_lw, g_lb, g_ws, g_bs = _sgu_bwd(p, dmix, lp["ln_w"], lp["ln_b"], lp["ws"], lp["bs_t"])
        dpb, dpc, dph, g_sc = _sc_bwd(p, dmix, lp["sc_cw"])
        dcq, dck, dcv, dcs, dcz, g_dcw, g_al, g_dt, g_dnw = _dn_bwd(
            p, dmix, s["st_c"], s["inv_c"], lp["dn_cw"], lp["dn_al"], lp["dn_dt"], lp["dn_nw"])
        ddq, ddk, ddv, dds, ddz, g_w2, g_gb, g_gnw = _gla_bwd(p, dmix, s["st_d"], lp["gla_w2"], lp["gla_gb"],
                                                             lp["gla_nw"])
        dp = jnp.concatenate([dpu, dpv, dpb, dpc, dph, dcq, dck, dcv, dcz, ddq, ddk, ddv, ddz, dcs, dds], axis=1)
        gbig["w_in"][l] = _matmul(s["h1"], dp, "tn", GRAD_WIRE_DTYPE, "proj_in_dw")
        dh1 = _matmul(dp, big["w_in"][l], "nt", F32, "proj_in_dx")
        dx, gs["norm1_w"][l] = _rmsnorm_bwd(s["x0"], small["norm1_w"][l][None], dh1, dx1, "norm1_bwd")
        gs["sgu_ln_w"][l], gs["sgu_ln_b"][l] = g_lw[0], g_lb[0]
        gs["sgu_w_spatial"][l] = g_ws
        gs["sgu_b_spatial"][l] = g_bs[:, :HEADS].T
        gs["sc_conv_w"][l] = g_sc[:3]
        gs["dn_conv_w"][l] = jnp.concatenate([g_dcw[0, :4], g_dcw[1, :4], g_dcw[2, :4]], axis=1)
        gs["dn_a_log"][l], gs["dn_dt_bias"][l] = g_al[0, :HEADS], g_dt[0, :HEADS]
        gs["dn_norm_w"][l] = jnp.sum(g_dnw.reshape(HEADS, HEAD_DIM), axis=0)
        gs["gla_w_gate2"][l] = g_w2[:16]
        gs["gla_gate_bias"][l] = g_gb[0]
        gs["gla_norm_w"][l] = jnp.sum(g_gnw.reshape(HEADS, HEAD_DIM), axis=0)
        gs["norm1_w"][l] = gs["norm1_w"][l][0]
        gs["norm2_w"][l] = gs["norm2_w"][l][0]
    gsmall = {k: jnp.stack(v) for k, v in gs.items()}
    gsmall["final_norm_w"] = d_final[0]
    return loss, dx, gbig, gsmall


def _peer_chips(x, y):
    return [(1 - x, y, 2 * (1 - x) + y), (x, 1 - y, 2 * x + 1 - y), (1 - x, 1 - y, 2 * (1 - x) + 1 - y)]


def _gather_weights(shards):
    na = len(shards)

    def body(*refs):
        ins, outs = refs[:na], refs[na:2 * na]
        send_sems, recv_sems, local_sems = refs[2 * na:]
        x, y, c = lax.axis_index("x"), lax.axis_index("y"), lax.axis_index("c")
        me = 2 * x + y
        copies = []
        for a in range(na):
            own = pltpu.make_async_copy(ins[a], outs[a].at[me], local_sems.at[a])
            own.start()
            copies.append(own)
            for k, (px, py, pidx) in enumerate(_peer_chips(x, y)):
                cp = pltpu.make_async_remote_copy(
                    src_ref=ins[a], dst_ref=outs[a].at[me], send_sem=send_sems.at[a, k], recv_sem=recv_sems.at[a, k],
                    device_id=(px, py, c), device_id_type=MESH)
                cp.start()
        for own in copies:
            own.wait()
        for a in range(na):
            for k, (px, py, pidx) in enumerate(_peer_chips(x, y)):
                cp = pltpu.make_async_remote_copy(
                    src_ref=ins[a], dst_ref=outs[a].at[pidx], send_sem=send_sems.at[a, k],
                    recv_sem=recv_sems.at[a, k], device_id=(px, py, c), device_id_type=MESH)
                cp.wait_send()
                cp.wait_recv()

    any_spec = pl.BlockSpec(memory_space=pl.ANY)
    return pl.pallas_call(
        body, name="gather_weights",
        in_specs=[any_spec] * na, out_specs=[any_spec] * na,
        out_shape=[jax.ShapeDtypeStruct((N_CHIPS,) + s.shape, s.dtype) for s in shards],
        scratch_shapes=[pltpu.SemaphoreType.DMA((na, 3)), pltpu.SemaphoreType.DMA((na, 3)),
                        pltpu.SemaphoreType.DMA((na,))],
    )(*shards)


def _exchange_grads(pieces, small_vec):
    na = len(pieces)

    def body(*refs):
        ins, sm_in = refs[:na], refs[na]
        outs, sm_out = refs[na + 1:2 * na + 1], refs[2 * na + 1]
        send1, recv1, send2, recv2, local_sems = refs[2 * na + 2:]
        x, y, c = lax.axis_index("x"), lax.axis_index("y"), lax.axis_index("c")
        me = 2 * x + y
        srcs = [ins[a] for a in range(na)]
        dsts = list(outs) + [sm_out]
        nb = na + 1

        def src_for(b, chip):
            return srcs[b].at[chip] if b < na else sm_in

        locals_ = []
        for b in range(nb):
            own = pltpu.make_async_copy(src_for(b, me), dsts[b].at[4 * c + me], local_sems.at[b])
            own.start()
            locals_.append(own)
            for k, (px, py, pidx) in enumerate(_peer_chips(x, y)):
                pltpu.make_async_remote_copy(
                    src_ref=src_for(b, pidx), dst_ref=dsts[b].at[4 * c + me], send_sem=send1.at[b, k],
                    recv_sem=recv1.at[b, k], device_id=(px, py, c), device_id_type=MESH).start()
        for own in locals_:
            own.wait()
        for b in range(nb):
            for k, (px, py, pidx) in enumerate(_peer_chips(x, y)):
                cp = pltpu.make_async_remote_copy(
                    src_ref=src_for(b, pidx), dst_ref=dsts[b].at[4 * c + pidx], send_sem=send1.at[b, k],
                    recv_sem=recv1.at[b, k], device_id=(px, py, c), device_id_type=MESH)
                cp.wait_send()
                cp.wait_recv()
        for b in range(nb):
            mine = dsts[b].at[pl.ds(4 * c, 4)]
            pltpu.make_async_remote_copy(
                src_ref=mine, dst_ref=mine, send_sem=send2.at[b], recv_sem=recv2.at[b],
                device_id=(x, y, 1 - c), device_id_type=MESH).start()
        for b in range(nb):
            mine = dsts[b].at[pl.ds(4 * c, 4)]
            theirs = dsts[b].at[pl.ds(4 * (1 - c), 4)]
            cp = pltpu.make_async_remote_copy(
                src_ref=mine, dst_ref=theirs, send_sem=send2.at[b], recv_sem=recv2.at[b],
                device_id=(x, y, 1 - c), device_id_type=MESH)
            cp.wait_send()
            cp.wait_recv()

    any_spec = pl.BlockSpec(memory_space=pl.ANY)
    out_shape = [jax.ShapeDtypeStruct((N_DEV,) + p.shape[1:], p.dtype) for p in pieces]
    out_shape.append(jax.ShapeDtypeStruct((N_DEV,) + small_vec.shape, small_vec.dtype))
    nb = na + 1
    return pl.pallas_call(
        body, name="exchange_grads",
        in_specs=[any_spec] * nb, out_specs=[any_spec] * nb, out_shape=out_shape,
        scratch_shapes=[pltpu.SemaphoreType.DMA((nb, 3)), pltpu.SemaphoreType.DMA((nb, 3)),
                        pltpu.SemaphoreType.DMA((nb,)), pltpu.SemaphoreType.DMA((nb,)),
                        pltpu.SemaphoreType.DMA((nb,))],
    )(*pieces, small_vec)


def _adamw_math(g, w, m, v):
    m2 = ADAM_B1 * m + (1.0 - ADAM_B1) * g
    v2 = ADAM_B2 * v + (1.0 - ADAM_B2) * (g * g)
    m_hat = m2 / (1.0 - ADAM_B1 ** ADAM_STEP)
    v_hat = v2 / (1.0 - ADAM_B2 ** ADAM_STEP)
    delta = -ADAM_LR * (m_hat / (jnp.sqrt(v_hat) + ADAM_EPS) + ADAM_WD * w)
    return delta, m2, v2


def _ordered_sum(ref):
    total = ref[0].astype(F32)
    for j in range(1, N_DEV):
        total = total + ref[j].astype(F32)
    return total


def _adamw_big(contrib, w, m, v, name):
    l, r, c = w.shape
    tr = _pick_tile(r, (256, 176, 128, 64, 8))
    blk = pl.BlockSpec((1, tr, c), lambda i, j: (i, j, 0))

    def body(g_ref, w_ref, m_ref, v_ref, go_ref, d_ref, mo_ref, vo_ref):
        g = _ordered_sum(g_ref)
        delta, m2, v2 = _adamw_math(g, w_ref[...], m_ref[...], v_ref[...])
        go_ref[...] = g
        d_ref[...] = delta
        mo_ref[...] = m2
        vo_ref[...] = v2

    return pl.pallas_call(
        body, name=name, grid=(l, r // tr),
        in_specs=[pl.BlockSpec((N_DEV, 1, tr, c), lambda i, j: (0, i, j, 0)), blk, blk, blk],
        out_specs=[blk] * 4, out_shape=[jax.ShapeDtypeStruct(w.shape, F32)] * 4,
        compiler_params=_cparams(("parallel", "parallel")),
    )(contrib, w, m, v)


def _sum_small(contrib):
    _, rows, _ = contrib.shape

    def body(g_ref, o_ref):
        o_ref[...] = _ordered_sum(g_ref)

    return pl.pallas_call(
        body, name="sum_small", out_shape=jax.ShapeDtypeStruct((rows, 128), F32),
        compiler_params=_cparams(),
    )(contrib)


def _adamw_small(g, w, m, v):
    def body(g_ref, w_ref, m_ref, v_ref, d_ref, mo_ref, vo_ref):
        delta, m2, v2 = _adamw_math(g_ref[...], w_ref[...], m_ref[...], v_ref[...])
        d_ref[...] = delta
        mo_ref[...] = m2
        vo_ref[...] = v2

    return pl.pallas_call(
        body, name="adamw_small", out_shape=[jax.ShapeDtypeStruct(g.shape, F32)] * 3,
        compiler_params=_cparams(),
    )(g, w, m, v)


def _pack(arrays):
    flat = jnp.concatenate([a.reshape(-1) for a in arrays])
    pad = (-flat.shape[0]) % 1024
    return jnp.concatenate([flat, jnp.zeros((pad,), F32)]).reshape(-1, 128)


def _unpack(packed, shapes):
    flat = packed.reshape(-1)
    out, off = [], 0
    for s in shapes:
        n = 1
        for d in s:
            n *= d
        out.append(flat[off:off + n].reshape(s))
        off += n
    return out


SMALL_NAMES = ("norm1_w", "sgu_ln_w", "sgu_ln_b", "sgu_w_spatial", "sgu_b_spatial", "sc_conv_w", "dn_conv_w",
               "dn_a_log", "dn_dt_bias", "dn_norm_w", "gla_w_gate2", "gla_gate_bias", "gla_norm_w", "norm2_w",
               "final_norm_w")
SHARDED_SMALL = ("sc_conv_w", "dn_conv_w", "gla_w_gate2")
BIG_NAMES = ("w_in", "w_out", "w_gate_up", "w_down")
WEIGHT_ORDER = ("norm1_w", "w_in", "sgu_ln_w", "sgu_ln_b", "sgu_w_spatial", "sgu_b_spatial", "sc_conv_w", "dn_conv_w",
                "dn_a_log", "dn_dt_bias", "dn_norm_w", "gla_w_gate2", "gla_gate_bias", "gla_norm_w", "w_out",
                "norm2_w", "w_gate_up", "w_down", "final_norm_w")


def _cols_from_shards(g):
    n, l, r, c = g.shape
    return jnp.transpose(g, (1, 2, 0, 3)).reshape(l, r, n * c)


def _rows_from_shards(g):
    n, l, r, c = g.shape
    return jnp.transpose(g, (1, 0, 2, 3)).reshape(l, n * r, c)


def _cols_to_shards(full):
    l, r, c4 = full.shape
    return jnp.transpose(full.reshape(l, r, N_CHIPS, c4 // N_CHIPS), (2, 0, 1, 3))


def _rows_to_shards(full):
    l, r4, c = full.shape
    return jnp.transpose(full.reshape(l, N_CHIPS, r4 // N_CHIPS, c), (1, 0, 2, 3))


def kernel(x, norm1_w, w_in, sgu_ln_w, sgu_ln_b, sgu_w_spatial, sgu_b_spatial, sc_conv_w, dn_conv_w, dn_a_log, dn_dt_bias, dn_norm_w, gla_w_gate2, gla_gate_bias, gla_norm_w, w_out, norm2_w, w_gate_up, w_down, final_norm_w, loss_target, m_norm1_w, m_w_in, m_sgu_ln_w, m_sgu_ln_b, m_sgu_w_spatial, m_sgu_b_spatial, m_sc_conv_w, m_dn_conv_w, m_dn_a_log, m_dn_dt_bias, m_dn_norm_w, m_gla_w_gate2, m_gla_gate_bias, m_gla_norm_w, m_w_out, m_norm2_w, m_w_gate_up, m_w_down, m_final_norm_w, v_norm1_w, v_w_in, v_sgu_ln_w, v_sgu_ln_b, v_sgu_w_spatial, v_sgu_b_spatial, v_sc_conv_w, v_dn_conv_w, v_dn_a_log, v_dn_dt_bias, v_dn_norm_w, v_gla_w_gate2, v_gla_gate_bias, v_gla_norm_w, v_w_out, v_norm2_w, v_w_gate_up, v_w_down, v_final_norm_w):
    w = dict(norm1_w=norm1_w, w_in=w_in, sgu_ln_w=sgu_ln_w, sgu_ln_b=sgu_ln_b, sgu_w_spatial=sgu_w_spatial,
             sgu_b_spatial=sgu_b_spatial, sc_conv_w=sc_conv_w, dn_conv_w=dn_conv_w, dn_a_log=dn_a_log,
             dn_dt_bias=dn_dt_bias, dn_norm_w=dn_norm_w, gla_w_gate2=gla_w_gate2, gla_gate_bias=gla_gate_bias,
             gla_norm_w=gla_norm_w, w_out=w_out, norm2_w=norm2_w, w_gate_up=w_gate_up, w_down=w_down,
             final_norm_w=final_norm_w)
    m = dict(norm1_w=m_norm1_w, w_in=m_w_in, sgu_ln_w=m_sgu_ln_w, sgu_ln_b=m_sgu_ln_b, sgu_w_spatial=m_sgu_w_spatial,
             sgu_b_spatial=m_sgu_b_spatial, sc_conv_w=m_sc_conv_w, dn_conv_w=m_dn_conv_w, dn_a_log=m_dn_a_log,
             dn_dt_bias=m_dn_dt_bias, dn_norm_w=m_dn_norm_w, gla_w_gate2=m_gla_w_gate2,
             gla_gate_bias=m_gla_gate_bias, gla_norm_w=m_gla_norm_w, w_out=m_w_out, norm2_w=m_norm2_w,
             w_gate_up=m_w_gate_up, w_down=m_w_down, final_norm_w=m_final_norm_w)
    v = dict(norm1_w=v_norm1_w, w_in=v_w_in, sgu_ln_w=v_sgu_ln_w, sgu_ln_b=v_sgu_ln_b, sgu_w_spatial=v_sgu_w_spatial,
             sgu_b_spatial=v_sgu_b_spatial, sc_conv_w=v_sc_conv_w, dn_conv_w=v_dn_conv_w, dn_a_log=v_dn_a_log,
             dn_dt_bias=v_dn_dt_bias, dn_norm_w=v_dn_norm_w, gla_w_gate2=v_gla_w_gate2,
             gla_gate_bias=v_gla_gate_bias, gla_norm_w=v_gla_norm_w, w_out=v_w_out, norm2_w=v_norm2_w,
             w_gate_up=v_w_gate_up, w_down=v_w_down, final_norm_w=v_final_norm_w)
    chip = 2 * lax.axis_index("x") + lax.axis_index("y")

    shards = [w[n].astype(MXU_DTYPE) for n in BIG_NAMES] + [w[n] for n in SHARDED_SMALL]
    gathered = _gather_weights(shards)
    full_in = _cols_from_shards(gathered[0])
    big = dict(
        w_in=[_pad_w_in(full_in[l]) for l in range(DEPTH)],
        w_out=_rows_from_shards(gathered[1]),
        w_gate_up=_cols_from_shards(gathered[2]),
        w_down=_rows_from_shards(gathered[3]),
    )
    small = {n: w[n] for n in SMALL_NAMES if n not in SHARDED_SMALL}
    for j, n in enumerate(SHARDED_SMALL):
        small[n] = _cols_from_shards(gathered[4 + j])

    loss_tile, grad_x, gbig, gsmall = _local_step(x[0], loss_target[0], big, small)

    pieces = [
        _cols_to_shards(jnp.stack([_unpad_w_in(g) for g in gbig["w_in"]])),
        _rows_to_shards(jnp.stack(gbig["w_out"])),
        _cols_to_shards(jnp.stack(gbig["w_gate_up"])),
        _rows_to_shards(jnp.stack(gbig["w_down"])),
    ]
    small_list = [gsmall[n] for n in SMALL_NAMES] + [loss_tile[0:1, 0]]
    small_shapes = [a.shape for a in small_list]
    contrib = _exchange_grads(pieces, _pack(small_list))

    out_g, out_d, out_m, out_v = {}, {}, {}, {}
    for j, n in enumerate(BIG_NAMES):
        out_g[n], out_d[n], out_m[n], out_v[n] = _adamw_big(contrib[j], w[n], m[n], v[n], "adamw_" + n)
    summed = _unpack(_sum_small(contrib[len(BIG_NAMES)]), small_shapes)
    loss = summed[-1][0]
    for n, g in zip(SMALL_NAMES, summed[:-1]):
        if n in SHARDED_SMALL:
            cols = g.shape[-1] // N_CHIPS
            g = lax.dynamic_slice_in_dim(g, chip * cols, cols, axis=g.ndim - 1)
        out_g[n] = g
    shapes = [out_g[n].shape for n in SMALL_NAMES]
    d_p, m_p, v_p = _adamw_small(_pack([out_g[n] for n in SMALL_NAMES]), _pack([w[n] for n in SMALL_NAMES]),
                                 _pack([m[n] for n in SMALL_NAMES]), _pack([v[n] for n in SMALL_NAMES]))
    for n, d_, m_, v_ in zip(SMALL_NAMES, _unpack(d_p, shapes), _unpack(m_p, shapes), _unpack(v_p, shapes)):
        out_d[n], out_m[n], out_v[n] = d_, m_, v_

    return (loss, grad_x[None], *[out_g[n] for n in WEIGHT_ORDER], *[out_d[n] for n in WEIGHT_ORDER],
            *[out_m[n] for n in WEIGHT_ORDER], *[out_v[n] for n in WEIGHT_ORDER])
```

```python
import functools

import jax
import jax.numpy as jnp
from jax import lax
from jax.experimental import pallas as pl
from jax.experimental.pallas import tpu as pltpu

F32 = jnp.float32
BF16 = jnp.bfloat16
MXU_DTYPE = jnp.bfloat16
GRAD_WIRE_DTYPE = jnp.bfloat16
HI = lax.Precision.HIGHEST
MESH = pl.DeviceIdType.MESH

D_MODEL = 1024
DEPTH = 2
GROUP = 256
HEADS = 4
HEAD_DIM = 64
SGU_CHUNK = 128
SCAN_CHUNK = 64
D_FF = 2816
EPS = 1e-6
IN_COLS = 3352
P_COLS = 3584
HALO = 8
N_CHIPS = 4
N_DEV = 8
VMEM_LIMIT = 48 * 1024 * 1024

ADAM_LR = 0.001
ADAM_B1 = 0.9
ADAM_B2 = 0.999
ADAM_EPS = 1e-08
ADAM_WD = 0.01
ADAM_STEP = 10

(COL_AU, COL_AV, COL_BB, COL_BC, COL_BH, COL_CQ, COL_CK, COL_CV, COL_CZ,
 COL_DQ, COL_DK, COL_DV, COL_DZ) = range(13)
COL128_SMALL_C = 26
COL128_SMALL_D = 27


def _cparams(sem=None):
    return pltpu.CompilerParams(dimension_semantics=sem, vmem_limit_bytes=VMEM_LIMIT)


def _iota(shape, dim):
    return lax.broadcasted_iota(jnp.int32, shape, dim)


def _dg(a, b, ca, cb, prec=None):
    return lax.dot_general(a, b, (((ca,), (cb,)), ((), ())), preferred_element_type=F32, precision=prec)


@functools.partial(jax.custom_vjp, nondiff_argnums=(2, 3))
def bdot(a, b, ca, cb):
    return _dg(a.astype(MXU_DTYPE), b.astype(MXU_DTYPE), ca, cb)


def _bdot_fwd(a, b, ca, cb):
    return bdot(a, b, ca, cb), (a, b)


def _bdot_bwd(ca, cb, res, g):
    a, b = res
    if ca == 1:
        da = bdot(g, b, 1, 1 if cb == 0 else 0)
    else:
        da = bdot(b, g, 1 if cb == 0 else 0, 1)
    if cb == 0:
        db = bdot(a, g, 0, 0) if ca == 1 else bdot(a, g, 1, 0)
    else:
        db = bdot(g, a, 0, 0) if ca == 1 else bdot(g, a, 0, 1)
    return da, db


bdot.defvjp(_bdot_fwd, _bdot_bwd)


def _pieces(a, n):
    out, r = [], a
    for i in range(n):
        p = r.astype(MXU_DTYPE)
        out.append(p)
        if i + 1 < n:
            r = r - p.astype(F32)
    return out


def _mdot_impl(a, b, ca, cb, sa, sb):
    total = None
    for i, x in enumerate(_pieces(a, sa)):
        for j, y in enumerate(_pieces(b, sb)):
            if i + j < max(sa, sb):
                t = _dg(x, y, ca, cb)
                total = t if total is None else total + t
    return total


@functools.partial(jax.custom_vjp, nondiff_argnums=(2, 3, 4, 5))
def mdot(a, b, ca, cb, sa, sb):
    return _mdot_impl(a, b, ca, cb, sa, sb)


def _mdot_fwd(a, b, ca, cb, sa, sb):
    return _mdot_impl(a, b, ca, cb, sa, sb), (a, b)


def _mdot_bwd(ca, cb, sa, sb, res, g):
    a, b = res
    ga, gb = (3 if sb == 1 else 2), (3 if sa == 1 else 2)
    if sa == 1:
        da = jnp.zeros_like(a)
    elif ca == 1:
        da = mdot(g, b, 1, 1 if cb == 0 else 0, ga, sb)
    else:
        da = mdot(b, g, 1 if cb == 0 else 0, 1, sb, ga)
    if sb == 1:
        db = jnp.zeros_like(b)
    elif cb == 0:
        db = mdot(a, g, 0, 0, sa, gb) if ca == 1 else mdot(a, g, 1, 0, sa, gb)
    else:
        db = mdot(g, a, 0, 0, gb, sa) if ca == 1 else mdot(g, a, 0, 1, gb, sa)
    return da, db


mdot.defvjp(_mdot_fwd, _mdot_bwd)


def mask_r(a, m, ca=1, cb=0):
    return mdot(a, m, ca, cb, 3, 1)


def mask_l(m, b, ca=1, cb=0):
    return mdot(m, b, ca, cb, 1, 3)


def ddot(a, b, ca=1, cb=0):
    return mdot(a, b, ca, cb, 2, 2)


def _head_mask(h):
    return ((_iota((1, GROUP), 1) >> 6) == h).astype(F32)


def _block_diag_mask():
    return ((_iota((GROUP, GROUP), 0) >> 6) == (_iota((GROUP, GROUP), 1) >> 6)).astype(F32)


def _expand_mat(offset):
    return ((_iota((128, GROUP), 0) - offset) == (_iota((128, GROUP), 1) >> 6)).astype(F32)


def _tril(n, strict=False):
    r, c = _iota((n, n), 0), _iota((n, n), 1)
    return (r > c) if strict else (r >= c)


def _row_pick(x, row):
    return jnp.sum(jnp.where(_iota(x.shape, 0) == row, x, 0.0), axis=0, keepdims=True)


def _shift_rows_impl(x, halo, j):
    n = x.shape[0]
    r = _iota(x.shape, 0)
    top = jnp.concatenate([pltpu.roll(halo, j, 0), jnp.zeros((n - HALO, x.shape[1]), x.dtype)], axis=0)
    return jnp.where(r >= j, pltpu.roll(x, j, 0), top)


def _mxu_round(a):
    return a.astype(MXU_DTYPE).astype(F32)


@functools.partial(jax.custom_vjp, nondiff_argnums=(3,))
def _causal_conv(x, halo, w, width):
    xb, hb, wb = _mxu_round(x), _mxu_round(halo), _mxu_round(w)
    out = xb * _row_pick(wb, width - 1)
    for j in range(1, width):
        out = out + _shift_rows_impl(xb, hb, j) * _row_pick(wb, width - 1 - j)
    return out


def _causal_conv_fwd(x, halo, w, width):
    return _causal_conv(x, halo, w, width), (x, halo, w)


def _causal_conv_bwd(width, res, g):
    x, halo, w = res
    xb, hb, wb, gb = _mxu_round(x), _mxu_round(halo), _mxu_round(w), _mxu_round(g)
    n = g.shape[0]
    rows, rows8 = _iota(g.shape, 0), _iota(halo.shape, 0)
    dx = gb * _row_pick(wb, width - 1)
    dh = jnp.zeros_like(halo)
    dw = jnp.where(rows8 == width - 1, jnp.sum(xb * gb, axis=0, keepdims=True), 0.0)
    for j in range(1, width):
        gj = gb * _row_pick(wb, width - 1 - j)
        dx = dx + jnp.where(rows < n - j, pltpu.roll(gj, n - j, 0), 0.0)
        dh = dh + jnp.where(rows8 >= HALO - j, pltpu.roll(gj[0:HALO], HALO - j, 0), 0.0)
        tap = jnp.sum(_shift_rows_impl(xb, hb, j) * gb, axis=0, keepdims=True)
        dw = dw + jnp.where(rows8 == width - 1 - j, tap, 0.0)
    return dx, dh, dw


_causal_conv.defvjp(_causal_conv_fwd, _causal_conv_bwd)


def _head_sum(x, bd):
    return mask_r(x, bd)


def _softplus(x):
    return jnp.maximum(x, 0.0) + jnp.log1p(jnp.exp(-jnp.abs(x)))


def _log_sigmoid(x):
    return -_softplus(-x)


def _silu(x):
    return x * jax.nn.sigmoid(x)


def _head_rmsnorm_gate(o, nw, z, bd):
    ms = _head_sum(o * o, bd) * (1.0 / HEAD_DIM)
    return o * lax.rsqrt(ms + EPS) * nw * _silu(z)


def _sgu_chunk(pu, pv, ln_w, ln_b, ws0, ws1, ws2, ws3, bs_t):
    u = jax.nn.gelu(pu)
    g = jax.nn.gelu(pv)
    mu = jnp.mean(g, axis=-1, keepdims=True)
    var = jnp.mean(jnp.square(g - mu), axis=-1, keepdims=True)
    v = (g - mu) * lax.rsqrt(var + EPS) * ln_w + ln_b
    keep = _tril(SGU_CHUNK)
    mixed = mask_r(bs_t, _expand_mat(0))
    for h, ws in enumerate((ws0, ws1, ws2, ws3)):
        mixed = mixed + _head_mask(h) * bdot(jnp.where(keep, ws, 0.0), v, 1, 0)
    return u * mixed


def _sc_chunk(pb, pc, ph, halo_c, halo_h, cw):
    return pb * _causal_conv(pc * ph, halo_c * halo_h, cw, 3)


def _neumann_inverses(lows):
    n = lows[0].shape[0]
    eye = (_iota((n, n), 0) == _iota((n, n), 1)).astype(F32)
    a = [-low for low in lows]
    t = [eye + x for x in a]
    for _ in range(5):
        a = [ddot(x, x) for x in a]
        t = [ti + ddot(ti, ai) for ti, ai in zip(t, a)]
    return t


@jax.custom_vjp
def _saved_inverse(low, inv):
    return inv


def _saved_inverse_fwd(low, inv):
    return inv, inv


def _saved_inverse_bwd(inv, g):
    return -ddot(ddot(inv, g, 0, 0), inv, 1, 1), jnp.zeros_like(inv)


_saved_inverse.defvjp(_saved_inverse_fwd, _saved_inverse_bwd)


def _chunk_tril(rows):
    r, c = _iota((rows, rows), 0), _iota((rows, rows), 1)
    return ((r >> 6) == (c >> 6)) & (r >= c)


def _dn_block(pq, pk, pv, hq, hk, hv, small, pz, cwq, cwk, cwv, a_log, dt_bias, nw, state, saved_inv=None):
    c = SCAN_CHUNK
    rows = pq.shape[0]
    bd = _block_diag_mask()
    q = _silu(_causal_conv(pq, hq, cwq, 4))
    k = _silu(_causal_conv(pk, hk, cwk, 4))
    v = _silu(_causal_conv(pv, hv, cwv, 4))
    q = q * lax.rsqrt(_head_sum(q * q, bd) + EPS) * (HEAD_DIM ** -0.5)
    k = k * lax.rsqrt(_head_sum(k * k, bd) + EPS)
    lane = _iota((1, 128), 1)
    g = jnp.where(lane < HEADS, -jnp.exp(a_log) * _softplus(small + dt_bias), 0.0)
    beta_b = mask_r(jax.nn.sigmoid(small), _expand_mat(HEADS))
    gc_all = mask_l(_chunk_tril(rows).astype(F32), g)
    gcb_all = mask_r(gc_all, _expand_mat(0))
    kb_all = k * beta_b
    vb_all = v * beta_b
    kbe_all = kb_all * jnp.exp(gcb_all)
    qg_all = q * jnp.exp(gcb_all)
    causal, strict = _tril(c), _tril(c, strict=True)
    nc = rows // c
    pairs = [(ci, h) for ci in range(nc) for h in range(HEADS)]
    sls = [slice(ci * c, (ci + 1) * c) for ci in range(nc)]
    decays, lows, attns = [], [], []
    for ci, h in pairs:
        gc = gc_all[sls[ci]]
        onehot = (_iota((c, 128), 1) == h).astype(F32)
        col = mask_l(onehot, gc, 1, 1)
        row = jnp.sum(gc * onehot, axis=1, keepdims=True)
        decays.append(jnp.exp(jnp.where(causal, row - col, -jnp.inf)))
    for j, (ci, h) in enumerate(pairs):
        mh = _head_mask(h)
        k_c = k[sls[ci]]
        lows.append(jnp.where(strict, bdot(kb_all[sls[ci]] * mh, k_c, 1, 1) * decays[j], 0.0))
        attns.append(bdot(q[sls[ci]] * mh, k_c, 1, 1) * decays[j])
    if saved_inv is None:
        invs = _neumann_inverses(lows)
    else:
        invs = [_saved_inverse(low, s) for low, s in zip(lows, saved_inv)]
    us, ws = [], []
    for ci in range(nc):
        u = jnp.zeros((c, GROUP), F32)
        w = jnp.zeros((c, GROUP), F32)
        for h in range(HEADS):
            mh = _head_mask(h)
            u = u + mh * ddot(invs[ci * HEADS + h], vb_all[sls[ci]])
            w = w + mh * ddot(invs[ci * HEADS + h], kbe_all[sls[ci]])
        us.append(u)
        ws.append(w)
    outs = []
    for ci in range(nc):
        gc_b = gcb_all[sls[ci]]
        gc_last_b = _row_pick(gc_b, c - 1)
        v_new = us[ci] - bdot(ws[ci], state, 1, 0)
        o = bdot(qg_all[sls[ci]], state, 1, 0)
        for h in range(HEADS):
            o = o + _head_mask(h) * bdot(attns[ci * HEADS + h], v_new, 1, 0)
        k_dec = k[sls[ci]] * jnp.exp(gc_last_b - gc_b)
        state = state * jnp.exp(gc_last_b) + bd * bdot(k_dec, v_new, 0, 0)
        outs.append(o)
    o = jnp.concatenate(outs, axis=0)
    return _head_rmsnorm_gate(o, nw, pz, bd), state, invs


def _gla_chunk(pq, pk, pv, small, pz, w2, gbias, nw, state_t):
    c = SCAN_CHUNK
    rows = pq.shape[0]
    nc = rows // c
    sls = [slice(ci * c, (ci + 1) * c) for ci in range(nc)]
    bd = _block_diag_mask()
    log_a = _log_sigmoid(bdot(small, w2, 1, 0) + gbias) * (1.0 / 16.0)
    gcum = mask_l(_chunk_tril(rows).astype(F32), log_a)
    r, s = _iota((rows, rows), 0), _iota((rows, rows), 1)
    base = (r >> 6) << 6
    g_mid = mask_l((s == base + c // 2).astype(F32), gcum)
    g_last = mask_l((s == base + c - 1).astype(F32), gcum)
    q = pq * (HEAD_DIM ** -0.5)
    qa = q * jnp.exp(gcum - g_mid)
    ka = pk * jnp.exp(g_mid - gcum)
    qg = q * jnp.exp(gcum)
    k_last = pk * jnp.exp(g_last - gcum)
    causal = _tril(c)
    attns = [jnp.where(causal, bdot(qa[sls[ci]] * _head_mask(h), ka[sls[ci]], 1, 1), 0.0)
             for ci in range(nc) for h in range(HEADS)]
    intra = []
    for ci in range(nc):
        o = jnp.zeros((c, GROUP), F32)
        for h in range(HEADS):
            o = o + _head_mask(h) * bdot(attns[ci * HEADS + h], pv[sls[ci]], 1, 0)
        intra.append(o)
    outs = []
    for ci in range(nc):
        outs.append(intra[ci] + bdot(qg[sls[ci]], state_t, 1, 1))
        dec = _row_pick(g_last[sls[ci]], 0)
        state_t = state_t * jnp.exp(dec) + bd * bdot(pv[sls[ci]], k_last[sls[ci]], 0, 0)
    o = jnp.concatenate(outs, axis=0)
    return _head_rmsnorm_gate(o, nw, pz, bd), state_t


def _col_spec(rows, group, rev_n=None):
    if rev_n is None:
        return pl.BlockSpec((rows, GROUP), lambda i: (i, group))
    return pl.BlockSpec((rows, GROUP), lambda i: (rev_n - 1 - i, group))


def _small_spec(rows, group128, rev_n=None):
    if rev_n is None:
        return pl.BlockSpec((rows, 128), lambda i: (i, group128))
    return pl.BlockSpec((rows, 128), lambda i: (rev_n - 1 - i, group128))


def _halo_spec(rows, group, rev_n=None):
    per = rows // HALO
    if rev_n is None:
        return pl.BlockSpec((HALO, GROUP), lambda i: (jnp.maximum(i * per - 1, 0), group))
    return pl.BlockSpec((HALO, GROUP), lambda i: (jnp.maximum((rev_n - 1 - i) * per - 1, 0), group))


def _full_spec(shape):
    nd = len(shape)
    return pl.BlockSpec(shape, lambda i: (0,) * nd)


def _out_rows_spec(rows, lanes, rev_n=None):
    if rev_n is None:
        return pl.BlockSpec((rows, lanes), lambda i: (i, 0))
    return pl.BlockSpec((rows, lanes), lambda i: (rev_n - 1 - i, 0))


def _sgu_fwd(p, ln_w, ln_b, ws, bs_t):
    t = p.shape[0]
    n = t // SGU_CHUNK

    def body(pu_ref, pv_ref, lw_ref, lb_ref, ws_ref, bs_ref, y_ref):
        y = _sgu_chunk(pu_ref[...], pv_ref[...], lw_ref[...], lb_ref[...],
                       ws_ref[0], ws_ref[1], ws_ref[2], ws_ref[3], bs_ref[...])
        y_ref[...] = y.astype(y_ref.dtype)

    return pl.pallas_call(
        body, name="sgu_fwd", grid=(n,),
        in_specs=[_col_spec(SGU_CHUNK, COL_AU), _col_spec(SGU_CHUNK, COL_AV), _full_spec((1, GROUP)),
                  _full_spec((1, GROUP)), _full_spec((HEADS, SGU_CHUNK, SGU_CHUNK)), _full_spec((SGU_CHUNK, 128))],
        out_specs=_out_rows_spec(SGU_CHUNK, GROUP),
        out_shape=jax.ShapeDtypeStruct((t, GROUP), BF16),
        compiler_params=_cparams(("arbitrary",)),
    )(p, p, ln_w, ln_b, ws, bs_t)


def _sgu_bwd(p, dmix, ln_w, ln_b, ws, bs_t):
    t = p.shape[0]
    n = t // SGU_CHUNK

    def body(pu_ref, pv_ref, dy_ref, lw_ref, lb_ref, ws_ref, bs_ref,
             dpu_ref, dpv_ref, dlw_ref, dlb_ref, dws_ref, dbs_ref):
        args = (pu_ref[...], pv_ref[...], lw_ref[...], lb_ref[...],
                ws_ref[0], ws_ref[1], ws_ref[2], ws_ref[3], bs_ref[...])
        _, vjp = jax.vjp(_sgu_chunk, *args)
        dpu, dpv, dlw, dlb, d0, d1, d2, d3, dbs = vjp(dy_ref[...])
        dpu_ref[...] = dpu.astype(dpu_ref.dtype)
        dpv_ref[...] = dpv.astype(dpv_ref.dtype)

        @pl.when(pl.program_id(0) == 0)
        def _():
            dlw_ref[...] = jnp.zeros_like(dlw_ref)
            dlb_ref[...] = jnp.zeros_like(dlb_ref)
            dws_ref[...] = jnp.zeros_like(dws_ref)
            dbs_ref[...] = jnp.zeros_like(dbs_ref)

        dlw_ref[...] += dlw
        dlb_ref[...] += dlb
        for h, d in enumerate((d0, d1, d2, d3)):
            dws_ref[h] += d
        dbs_ref[...] += dbs

    return pl.pallas_call(
        body, name="sgu_bwd", grid=(n,),
        in_specs=[_col_spec(SGU_CHUNK, COL_AU), _col_spec(SGU_CHUNK, COL_AV),
                  pl.BlockSpec((SGU_CHUNK, GROUP), lambda i: (i, 0)),
                  _full_spec((1, GROUP)), _full_spec((1, GROUP)), _full_spec((HEADS, SGU_CHUNK, SGU_CHUNK)),
                  _full_spec((SGU_CHUNK, 128))],
        out_specs=[_out_rows_spec(SGU_CHUNK, GROUP), _out_rows_spec(SGU_CHUNK, GROUP), _full_spec((1, GROUP)),
                   _full_spec((1, GROUP)), _full_spec((HEADS, SGU_CHUNK, SGU_CHUNK)), _full_spec((SGU_CHUNK, 128))],
        out_shape=[jax.ShapeDtypeStruct((t, GROUP), BF16), jax.ShapeDtypeStruct((t, GROUP), BF16),
                   jax.ShapeDtypeStruct((1, GROUP), F32), jax.ShapeDtypeStruct((1, GROUP), F32),
                   jax.ShapeDtypeStruct((HEADS, SGU_CHUNK, SGU_CHUNK), F32),
                   jax.ShapeDtypeStruct((SGU_CHUNK, 128), F32)],
        compiler_params=_cparams(("arbitrary",)),
    )(p, p, dmix, ln_w, ln_b, ws, bs_t)


SC_ROWS = 256


def _first_block_zero(halo, first):
    return jnp.where(first, 0.0, halo)


def _sc_fwd(p, cw):
    t = p.shape[0]
    n = t // SC_ROWS

    def body(pb_ref, pc_ref, ph_ref, hc_ref, hh_ref, cw_ref, y_ref):
        first = pl.program_id(0) == 0
        y = _sc_chunk(pb_ref[...], pc_ref[...], ph_ref[...], _first_block_zero(hc_ref[...], first),
                      _first_block_zero(hh_ref[...], first), cw_ref[...])
        y_ref[...] = y.astype(y_ref.dtype)

    return pl.pallas_call(
        body, name="sc_fwd", grid=(n,),
        in_specs=[_col_spec(SC_ROWS, COL_BB), _col_spec(SC_ROWS, COL_BC), _col_spec(SC_ROWS, COL_BH),
                  _halo_spec(SC_ROWS, COL_BC), _halo_spec(SC_ROWS, COL_BH), _full_spec((HALO, GROUP))],
        out_specs=_out_rows_spec(SC_ROWS, GROUP),
        out_shape=jax.ShapeDtypeStruct((t, GROUP), BF16),
        compiler_params=_cparams(("arbitrary",)),
    )(p, p, p, p, p, cw)


def _add_halo_grad(d, carry):
    return d + jnp.concatenate([jnp.zeros((d.shape[0] - HALO, d.shape[1]), d.dtype), carry], axis=0)


def _sc_bwd(p, dmix, cw):
    t = p.shape[0]
    n = t // SC_ROWS

    def body(pb_ref, pc_ref, ph_ref, hc_ref, hh_ref, dy_ref, cw_ref,
             dpb_ref, dpc_ref, dph_ref, dcw_ref, carry_c, carry_h):
        i = pl.program_id(0)
        first = i == n - 1

        @pl.when(i == 0)
        def _():
            carry_c[...] = jnp.zeros_like(carry_c)
            carry_h[...] = jnp.zeros_like(carry_h)
            dcw_ref[...] = jnp.zeros_like(dcw_ref)

        args = (pb_ref[...], pc_ref[...], ph_ref[...], _first_block_zero(hc_ref[...], first),
                _first_block_zero(hh_ref[...], first), cw_ref[...])
        _, vjp = jax.vjp(_sc_chunk, *args)
        dpb, dpc, dph, dhc, dhh, dcw = vjp(dy_ref[...])
        dpb_ref[...] = dpb.astype(dpb_ref.dtype)
        dpc_ref[...] = _add_halo_grad(dpc, carry_c[...]).astype(dpc_ref.dtype)
        dph_ref[...] = _add_halo_grad(dph, carry_h[...]).astype(dph_ref.dtype)
        carry_c[...] = dhc
        carry_h[...] = dhh
        dcw_ref[...] += dcw

    return pl.pallas_call(
        body, name="sc_bwd", grid=(n,),
        in_specs=[_col_spec(SC_ROWS, COL_BB, n), _col_spec(SC_ROWS, COL_BC, n), _col_spec(SC_ROWS, COL_BH, n),
                  _halo_spec(SC_ROWS, COL_BC, n), _halo_spec(SC_ROWS, COL_BH, n),
                  pl.BlockSpec((SC_ROWS, GROUP), lambda i: (n - 1 - i, 1)), _full_spec((HALO, GROUP))],
        out_specs=[_out_rows_spec(SC_ROWS, GROUP, n)] * 3 + [_full_spec((HALO, GROUP))],
        out_shape=[jax.ShapeDtypeStruct((t, GROUP), BF16)] * 3 + [jax.ShapeDtypeStruct((HALO, GROUP), F32)],
        scratch_shapes=[pltpu.VMEM((HALO, GROUP), F32), pltpu.VMEM((HALO, GROUP), F32)],
        compiler_params=_cparams(("arbitrary",)),
    )(p, p, p, p, p, dmix, cw)


SCAN_STEP_CHUNKS = 4
SCAN_ROWS = SCAN_STEP_CHUNKS * SCAN_CHUNK


def _dn_fwd(p, cw3, a_log, dt_bias, nw):
    t = p.shape[0]
    r = SCAN_ROWS
    n = t // r

    def body(pq_ref, pk_ref, pv_ref, hq_ref, hk_ref, hv_ref, sm_ref, pz_ref, cw_ref, al_ref, dt_ref, nw_ref,
             y_ref, ck_ref, inv_ref, state):
        first = pl.program_id(0) == 0

        @pl.when(first)
        def _():
            state[...] = jnp.zeros_like(state)

        s_in = state[...]
        ck_ref[0] = s_in
        y, s_out, invs = _dn_block(pq_ref[...], pk_ref[...], pv_ref[...], _first_block_zero(hq_ref[...], first),
                                   _first_block_zero(hk_ref[...], first), _first_block_zero(hv_ref[...], first),
                                   sm_ref[...], pz_ref[...], cw_ref[0], cw_ref[1], cw_ref[2],
                                   al_ref[...], dt_ref[...], nw_ref[...], s_in)
        y_ref[...] = y.astype(y_ref.dtype)
        state[...] = s_out
        for j, inv in enumerate(invs):
            inv_ref[j] = inv

    nh = SCAN_STEP_CHUNKS * HEADS
    return pl.pallas_call(
        body, name="dn_fwd", grid=(n,),
        in_specs=[_col_spec(r, COL_CQ), _col_spec(r, COL_CK), _col_spec(r, COL_CV),
                  _halo_spec(r, COL_CQ), _halo_spec(r, COL_CK), _halo_spec(r, COL_CV),
                  _small_spec(r, COL128_SMALL_C), _col_spec(r, COL_CZ), _full_spec((3, HALO, GROUP)),
                  _full_spec((1, 128)), _full_spec((1, 128)), _full_spec((1, GROUP))],
        out_specs=[_out_rows_spec(r, GROUP), pl.BlockSpec((1, GROUP, GROUP), lambda i: (i, 0, 0)),
                   pl.BlockSpec((nh, SCAN_CHUNK, SCAN_CHUNK), lambda i: (i, 0, 0))],
        out_shape=[jax.ShapeDtypeStruct((t, GROUP), BF16), jax.ShapeDtypeStruct((n, GROUP, GROUP), F32),
                   jax.ShapeDtypeStruct((n * nh, SCAN_CHUNK, SCAN_CHUNK), F32)],
        scratch_shapes=[pltpu.VMEM((GROUP, GROUP), F32)],
        compiler_params=_cparams(("arbitrary",)),
    )(p, p, p, p, p, p, p, p, cw3, a_log, dt_bias, nw)


def _dn_bwd(p, dmix, states, invs, cw3, a_log, dt_bias, nw):
    t = p.shape[0]
    c = SCAN_ROWS
    n = t // c
    nh = SCAN_STEP_CHUNKS * HEADS

    def body(pq_ref, pk_ref, pv_ref, hq_ref, hk_ref, hv_ref, sm_ref, pz_ref, dy_ref, ck_ref, inv_ref,
             cw_ref, al_ref, dt_ref, nw_ref,
             dpq_ref, dpk_ref, dpv_ref, dsm_ref, dpz_ref, dcw_ref, dal_ref, ddt_ref, dnw_ref,
             dstate, carry):
        i = pl.program_id(0)
        first = i == n - 1

        @pl.when(i == 0)
        def _():
            dstate[...] = jnp.zeros_like(dstate)
            carry[...] = jnp.zeros_like(carry)
            dcw_ref[...] = jnp.zeros_like(dcw_ref)
            dal_ref[...] = jnp.zeros_like(dal_ref)
            ddt_ref[...] = jnp.zeros_like(ddt_ref)
            dnw_ref[...] = jnp.zeros_like(dnw_ref)

        args = (pq_ref[...], pk_ref[...], pv_ref[...], _first_block_zero(hq_ref[...], first),
                _first_block_zero(hk_ref[...], first), _first_block_zero(hv_ref[...], first),
                sm_ref[...], pz_ref[...], cw_ref[0], cw_ref[1], cw_ref[2],
                al_ref[...], dt_ref[...], nw_ref[...], ck_ref[0])
        saved = [inv_ref[j] for j in range(nh)]
        _, vjp = jax.vjp(lambda *a: _dn_block(*a, saved_inv=saved)[:2], *args)
        (dpq, dpk, dpv, dhq, dhk, dhv, dsm, dpz, dcq, dck, dcv, dal, ddt, dnw, dst) = vjp(
            (dy_ref[...], dstate[...]))
        dpq_ref[...] = _add_halo_grad(dpq, carry[0]).astype(dpq_ref.dtype)
        dpk_ref[...] = _add_halo_grad(dpk, carry[1]).astype(dpk_ref.dtype)
        dpv_ref[...] = _add_halo_grad(dpv, carry[2]).astype(dpv_ref.dtype)
        dsm_ref[...] = dsm.astype(dsm_ref.dtype)
        dpz_ref[...] = dpz.astype(dpz_ref.dtype)
        carry[0] = dhq
        carry[1] = dhk
        carry[2] = dhv
        dstate[...] = dst
        dcw_ref[0] += dcq
        dcw_ref[1] += dck
        dcw_ref[2] += dcv
        dal_ref[...] += dal
        ddt_ref[...] += ddt
        dnw_ref[...] += dnw

    return pl.pallas_call(
        body, name="dn_bwd", grid=(n,),
        in_specs=[_col_spec(c, COL_CQ, n), _col_spec(c, COL_CK, n), _col_spec(c, COL_CV, n),
                  _halo_spec(c, COL_CQ, n), _halo_spec(c, COL_CK, n), _halo_spec(c, COL_CV, n),
                  _small_spec(c, COL128_SMALL_C, n), _col_spec(c, COL_CZ, n),
                  pl.BlockSpec((c, GROUP), lambda i: (n - 1 - i, 2)),
                  pl.BlockSpec((1, GROUP, GROUP), lambda i: (n - 1 - i, 0, 0)),
                  pl.BlockSpec((nh, SCAN_CHUNK, SCAN_CHUNK), lambda i: (n - 1 - i, 0, 0)),
                  _full_spec((3, HALO, GROUP)), _full_spec((1, 128)), _full_spec((1, 128)), _full_spec((1, GROUP))],
        out_specs=[_out_rows_spec(c, GROUP, n)] * 3 + [_out_rows_spec(c, 128, n), _out_rows_spec(c, GROUP, n),
                   _full_spec((3, HALO, GROUP)), _full_spec((1, 128)), _full_spec((1, 128)), _full_spec((1, GROUP))],
        out_shape=[jax.ShapeDtypeStruct((t, GROUP), BF16)] * 3 + [
            jax.ShapeDtypeStruct((t, 128), BF16), jax.ShapeDtypeStruct((t, GROUP), BF16),
            jax.ShapeDtypeStruct((3, HALO, GROUP), F32), jax.ShapeDtypeStruct((1, 128), F32),
            jax.ShapeDtypeStruct((1, 128), F32), jax.ShapeDtypeStruct((1, GROUP), F32)],
        scratch_shapes=[pltpu.VMEM((GROUP, GROUP), F32), pltpu.VMEM((3, HALO, GROUP), F32)],
        compiler_params=_cparams(("arbitrary",)),
    )(p, p, p, p, p, p, p, p, dmix, states, invs, cw3, a_log, dt_bias, nw)


def _gla_fwd(p, w2, gbias, nw):
    t = p.shape[0]
    c = SCAN_ROWS
    n = t // c

    def body(pq_ref, pk_ref, pv_ref, sm_ref, pz_ref, w2_ref, gb_ref, nw_ref, y_ref, ck_ref, state):
        @pl.when(pl.program_id(0) == 0)
        def _():
            state[...] = jnp.zeros_like(state)

        s_in = state[...]
        ck_ref[0] = s_in
        y, s_out = _gla_chunk(pq_ref[...], pk_ref[...], pv_ref[...], sm_ref[...], pz_ref[...],
                              w2_ref[...], gb_ref[...], nw_ref[...], s_in)
        y_ref[...] = y.astype(y_ref.dtype)
        state[...] = s_out

    return pl.pallas_call(
        body, name="gla_fwd", grid=(n,),
        in_specs=[_col_spec(c, COL_DQ), _col_spec(c, COL_DK), _col_spec(c, COL_DV),
                  _small_spec(c, COL128_SMALL_D), _col_spec(c, COL_DZ),
                  _full_spec((128, GROUP)), _full_spec((1, GROUP)), _full_spec((1, GROUP))],
        out_specs=[_out_rows_spec(c, GROUP), pl.BlockSpec((1, GROUP, GROUP), lambda i: (i, 0, 0))],
        out_shape=[jax.ShapeDtypeStruct((t, GROUP), BF16), jax.ShapeDtypeStruct((n, GROUP, GROUP), F32)],
        scratch_shapes=[pltpu.VMEM((GROUP, GROUP), F32)],
        compiler_params=_cparams(("arbitrary",)),
    )(p, p, p, p, p, w2, gbias, nw)


def _gla_bwd(p, dmix, states, w2, gbias, nw):
    t = p.shape[0]
    c = SCAN_ROWS
    n = t // c

    def body(pq_ref, pk_ref, pv_ref, sm_ref, pz_ref, dy_ref, ck_ref, w2_ref, gb_ref, nw_ref,
             dpq_ref, dpk_ref, dpv_ref, dsm_ref, dpz_ref, dw2_ref, dgb_ref, dnw_ref, dstate):
        @pl.when(pl.program_id(0) == 0)
        def _():
            dstate[...] = jnp.zeros_like(dstate)
            dw2_ref[...] = jnp.zeros_like(dw2_ref)
            dgb_ref[...] = jnp.zeros_like(dgb_ref)
            dnw_ref[...] = jnp.zeros_like(dnw_ref)

        args = (pq_ref[...], pk_ref[...], pv_ref[...], sm_ref[...], pz_ref[...],
                w2_ref[...], gb_ref[...], nw_ref[...], ck_ref[0])
        _, vjp = jax.vjp(_gla_chunk, *args)
        dpq, dpk, dpv, dsm, dpz, dw2, dgb, dnw, dst = vjp((dy_ref[...], dstate[...]))
        dpq_ref[...] = dpq.astype(dpq_ref.dtype)
        dpk_ref[...] = dpk.astype(dpk_ref.dtype)
        dpv_ref[...] = dpv.astype(dpv_ref.dtype)
        dsm_ref[...] = dsm.astype(dsm_ref.dtype)
        dpz_ref[...] = dpz.astype(dpz_ref.dtype)
        dstate[...] = dst
        dw2_ref[...] += dw2
        dgb_ref[...] += dgb
        dnw_ref[...] += dnw

    return pl.pallas_call(
        body, name="gla_bwd", grid=(n,),
        in_specs=[_col_spec(c, COL_DQ, n), _col_spec(c, COL_DK, n), _col_spec(c, COL_DV, n),
                  _small_spec(c, COL128_SMALL_D, n), _col_spec(c, COL_DZ, n),
                  pl.BlockSpec((c, GROUP), lambda i: (n - 1 - i, 3)),
                  pl.BlockSpec((1, GROUP, GROUP), lambda i: (n - 1 - i, 0, 0)),
                  _full_spec((128, GROUP)), _full_spec((1, GROUP)), _full_spec((1, GROUP))],
        out_specs=[_out_rows_spec(c, GROUP, n)] * 3 + [_out_rows_spec(c, 128, n), _out_rows_spec(c, GROUP, n),
                   _full_spec((128, GROUP)), _full_spec((1, GROUP)), _full_spec((1, GROUP))],
        out_shape=[jax.ShapeDtypeStruct((t, GROUP), BF16)] * 3 + [
            jax.ShapeDtypeStruct((t, 128), BF16), jax.ShapeDtypeStruct((t, GROUP), BF16),
            jax.ShapeDtypeStruct((128, GROUP), F32), jax.ShapeDtypeStruct((1, GROUP), F32),
            jax.ShapeDtypeStruct((1, GROUP), F32)],
        scratch_shapes=[pltpu.VMEM((GROUP, GROUP), F32)],
        compiler_params=_cparams(("arbitrary",)),
    )(p, p, p, p, p, dmix, states, w2, gbias, nw)


def _pick_tile(n, pref):
    for cand in pref:
        if n % cand == 0:
            return cand
    return n


MM_TILE_CAP = 1408


def _largest_tile(n, cap):
    best = None
    for mult in range(1, cap // 128 + 1):
        if n % (128 * mult) == 0:
            best = 128 * mult
    return best if best is not None else n


def _matmul(a, b, mode, out_dtype, name, res=None):
    if mode == "nn":
        (m, k), n = a.shape, b.shape[1]
    elif mode == "nt":
        (m, k), n = a.shape, b.shape[0]
    else:
        (k, m), n = a.shape, b.shape[1]
    tm = _largest_tile(m, MM_TILE_CAP)
    tn = _largest_tile(n, MM_TILE_CAP)
    tk = _largest_tile(k, MM_TILE_CAP)
    nk = k // tk
    if mode == "nn":
        a_spec = pl.BlockSpec((tm, tk), lambda i, j, kk: (i, kk))
        b_spec = pl.BlockSpec((tk, tn), lambda i, j, kk: (kk, j))
        dims = (1, 0)
    elif mode == "nt":
        a_spec = pl.BlockSpec((tm, tk), lambda i, j, kk: (i, kk))
        b_spec = pl.BlockSpec((tn, tk), lambda i, j, kk: (j, kk))
        dims = (1, 1)
    else:
        a_spec = pl.BlockSpec((tk, tm), lambda i, j, kk: (kk, i))
        b_spec = pl.BlockSpec((tk, tn), lambda i, j, kk: (kk, j))
        dims = (0, 0)
    o_spec = pl.BlockSpec((tm, tn), lambda i, j, kk: (i, j))
    has_res = res is not None

    def body(*refs):
        a_ref, b_ref = refs[:2]
        r_ref = refs[2] if has_res else None
        o_ref = refs[3] if has_res else refs[2]
        part = _dg(a_ref[...].astype(MXU_DTYPE), b_ref[...].astype(MXU_DTYPE), *dims)

        def finish(out):
            if has_res:
                out = out + r_ref[...]
            o_ref[...] = out.astype(o_ref.dtype)

        if nk == 1:
            finish(part)
            return
        acc = refs[-1]
        kk = pl.program_id(2)

        @pl.when(kk == 0)
        def _():
            acc[...] = part

        @pl.when(kk > 0)
        def _():
            acc[...] += part

        @pl.when(kk == nk - 1)
        def _():
            finish(acc[...])

    in_specs = [a_spec, b_spec] + ([o_spec] if has_res else [])
    args = (a, b) + ((res,) if has_res else ())
    return pl.pallas_call(
        body, name=name, grid=(m // tm, n // tn, nk), in_specs=in_specs, out_specs=o_spec,
        out_shape=jax.ShapeDtypeStruct((m, n), out_dtype),
        scratch_shapes=[pltpu.VMEM((tm, tn), F32)] if nk > 1 else [],
        compiler_params=_cparams(("parallel", "parallel", "arbitrary")),
    )(*args)


NORM_ROWS = 512


def _rmsnorm_fwd(x, w, name):
    t, d = x.shape

    def body(x_ref, w_ref, o_ref):
        xv = x_ref[...]
        r = lax.rsqrt(jnp.mean(xv * xv, axis=-1, keepdims=True) + EPS)
        o_ref[...] = (xv * r * w_ref[...]).astype(o_ref.dtype)

    return pl.pallas_call(
        body, name=name, grid=(t // NORM_ROWS,),
        in_specs=[pl.BlockSpec((NORM_ROWS, d), lambda i: (i, 0)), _full_spec((1, d))],
        out_specs=pl.BlockSpec((NORM_ROWS, d), lambda i: (i, 0)),
        out_shape=jax.ShapeDtypeStruct((t, d), BF16),
        compiler_params=_cparams(("parallel",)),
    )(x, w)


def _rmsnorm_bwd(x, w, dh, dres, name):
    t, d = x.shape

    def body(x_ref, w_ref, dh_ref, dr_ref, dx_ref, dw_ref):
        @pl.when(pl.program_id(0) == 0)
        def _():
            dw_ref[...] = jnp.zeros_like(dw_ref)

        xv = x_ref[...]
        g = dh_ref[...].astype(F32)
        r = lax.rsqrt(jnp.mean(xv * xv, axis=-1, keepdims=True) + EPS)
        xhat = xv * r
        dw_ref[...] += jnp.sum(g * xhat, axis=0, keepdims=True)
        gx = g * w_ref[...]
        dx = r * (gx - xhat * jnp.mean(gx * xhat, axis=-1, keepdims=True))
        dx_ref[...] = dr_ref[...] + dx

    return pl.pallas_call(
        body, name=name, grid=(t // NORM_ROWS,),
        in_specs=[pl.BlockSpec((NORM_ROWS, d), lambda i: (i, 0)), _full_spec((1, d)),
                  pl.BlockSpec((NORM_ROWS, d), lambda i: (i, 0)), pl.BlockSpec((NORM_ROWS, d), lambda i: (i, 0))],
        out_specs=[pl.BlockSpec((NORM_ROWS, d), lambda i: (i, 0)), _full_spec((1, d))],
        out_shape=[jax.ShapeDtypeStruct((t, d), F32), jax.ShapeDtypeStruct((1, d), F32)],
        compiler_params=_cparams(("arbitrary",)),
    )(x, w, dh, dres)


SWIGLU_ROWS = 128


def _swiglu_fwd(gu):
    t = gu.shape[0]

    def body(gu_ref, o_ref):
        gate = gu_ref[:, :D_FF]
        up = gu_ref[:, D_FF:]
        o_ref[...] = (_silu(gate) * up).astype(o_ref.dtype)

    return pl.pallas_call(
        body, name="swiglu_fwd", grid=(t // SWIGLU_ROWS,),
        in_specs=[pl.BlockSpec((SWIGLU_ROWS, 2 * D_FF), lambda i: (i, 0))],
        out_specs=pl.BlockSpec((SWIGLU_ROWS, D_FF), lambda i: (i, 0)),
        out_shape=jax.ShapeDtypeStruct((t, D_FF), BF16),
        compiler_params=_cparams(("parallel",)),
    )(gu)


def _swiglu_bwd(gu, dact):
    t = gu.shape[0]

    def body(gu_ref, da_ref, o_ref):
        gate = gu_ref[:, :D_FF]
        up = gu_ref[:, D_FF:]
        da = da_ref[...]
        sg = jax.nn.sigmoid(gate)
        o_ref[:, :D_FF] = (da * up * (sg * (1.0 + gate * (1.0 - sg)))).astype(o_ref.dtype)
        o_ref[:, D_FF:] = (da * gate * sg).astype(o_ref.dtype)

    return pl.pallas_call(
        body, name="swiglu_bwd", grid=(t // SWIGLU_ROWS,),
        in_specs=[pl.BlockSpec((SWIGLU_ROWS, 2 * D_FF), lambda i: (i, 0)),
                  pl.BlockSpec((SWIGLU_ROWS, D_FF), lambda i: (i, 0))],
        out_specs=pl.BlockSpec((SWIGLU_ROWS, 2 * D_FF), lambda i: (i, 0)),
        out_shape=jax.ShapeDtypeStruct((t, 2 * D_FF), BF16),
        compiler_params=_cparams(("parallel",)),
    )(gu, dact)


def _loss_head(x, w, target):
    t, d = x.shape

    def fwd(xv, wv, tv):
        r = lax.rsqrt(jnp.mean(xv * xv, axis=-1, keepdims=True) + EPS)
        err = xv * r * wv - tv
        return 0.5 * jnp.sum(jnp.mean(err * err, axis=-1, keepdims=True), axis=0, keepdims=True)

    def body(x_ref, w_ref, t_ref, dx_ref, dw_ref, loss_ref):
        @pl.when(pl.program_id(0) == 0)
        def _():
            dw_ref[...] = jnp.zeros_like(dw_ref)
            loss_ref[...] = jnp.zeros_like(loss_ref)

        loss, vjp = jax.vjp(fwd, x_ref[...], w_ref[...], t_ref[...])
        dx, dw, _ = vjp(jnp.ones((1, 1), F32))
        dx_ref[...] = dx
        dw_ref[...] += dw
        loss_ref[...] += jnp.broadcast_to(loss, loss_ref.shape)

    return pl.pallas_call(
        body, name="loss_head", grid=(t // NORM_ROWS,),
        in_specs=[pl.BlockSpec((NORM_ROWS, d), lambda i: (i, 0)), _full_spec((1, d)),
                  pl.BlockSpec((NORM_ROWS, d), lambda i: (i, 0))],
        out_specs=[pl.BlockSpec((NORM_ROWS, d), lambda i: (i, 0)), _full_spec((1, d)), _full_spec((8, 128))],
        out_shape=[jax.ShapeDtypeStruct((t, d), F32), jax.ShapeDtypeStruct((1, d), F32),
                   jax.ShapeDtypeStruct((8, 128), F32)],
        compiler_params=_cparams(("arbitrary",)),
    )(x, w, target)


def _pad_w_in(w):
    z = lambda n: jnp.zeros((w.shape[0], n), w.dtype)
    return jnp.concatenate([w[:, 0:2048], w[:, 2056:2312], w[:, 2312:3080], w[:, 3096:3352],
                            w[:, 2048:2056], z(120), w[:, 3080:3096], z(112)], axis=1)


def _unpad_w_in(wp):
    return jnp.concatenate([wp[:, 0:2048], wp[:, 3328:3336], wp[:, 2048:2304], wp[:, 2304:3072],
                            wp[:, 3456:3472], wp[:, 3072:3328]], axis=1)


def _pad_rows(a, rows):
    return jnp.concatenate([a, jnp.zeros((rows - a.shape[0],) + a.shape[1:], a.dtype)], axis=0)


def _pad_lanes(a, lanes):
    return jnp.concatenate([a, jnp.zeros(a.shape[:-1] + (lanes - a.shape[-1],), a.dtype)], axis=-1)


def _layer_params(l, small):
    dn_cw = small["dn_conv_w"][l]
    return dict(
        ln_w=small["sgu_ln_w"][l][None], ln_b=small["sgu_ln_b"][l][None],
        ws=small["sgu_w_spatial"][l], bs_t=_pad_lanes(small["sgu_b_spatial"][l].T, 128),
        sc_cw=_pad_rows(small["sc_conv_w"][l], HALO),
        dn_cw=jnp.stack([_pad_rows(dn_cw[:, j * GROUP:(j + 1) * GROUP], HALO) for j in range(3)]),
        dn_al=_pad_lanes(small["dn_a_log"][l][None], 128), dn_dt=_pad_lanes(small["dn_dt_bias"][l][None], 128),
        dn_nw=jnp.tile(small["dn_norm_w"][l][None], (1, HEADS)),
        gla_w2=_pad_rows(small["gla_w_gate2"][l], 128), gla_gb=small["gla_gate_bias"][l][None],
        gla_nw=jnp.tile(small["gla_norm_w"][l][None], (1, HEADS)),
    )


def _local_step(x, target, big, small):
    saved = []
    h = x
    for l in range(DEPTH):
        lp = _layer_params(l, small)
        h1 = _rmsnorm_fwd(h, small["norm1_w"][l][None], "norm1_fwd")
        p = _matmul(h1, big["w_in"][l], "nn", F32, "proj_in")
        y_a = _sgu_fwd(p, lp["ln_w"], lp["ln_b"], lp["ws"], lp["bs_t"])
        y_b = _sc_fwd(p, lp["sc_cw"])
        y_c, st_c, inv_c = _dn_fwd(p, lp["dn_cw"], lp["dn_al"], lp["dn_dt"], lp["dn_nw"])
        y_d, st_d = _gla_fwd(p, lp["gla_w2"], lp["gla_gb"], lp["gla_nw"])
        mix = jnp.concatenate([y_a, y_b, y_c, y_d], axis=1)
        x1 = _matmul(mix, big["w_out"][l], "nn", F32, "proj_out", res=h)
        h2 = _rmsnorm_fwd(x1, small["norm2_w"][l][None], "norm2_fwd")
        gu = _matmul(h2, big["w_gate_up"][l], "nn", F32, "ffn_up")
        act = _swiglu_fwd(gu)
        x2 = _matmul(act, big["w_down"][l], "nn", F32, "ffn_down", res=x1)
        saved.append(dict(x0=h, h1=h1, p=p, st_c=st_c, inv_c=inv_c, st_d=st_d, mix=mix, x1=x1, h2=h2, gu=gu, act=act, lp=lp))
        h = x2

    dx, d_final, loss = _loss_head(h, small["final_norm_w"][None], target)
    gbig = {k: [None] * DEPTH for k in ("w_in", "w_out", "w_gate_up", "w_down")}
    gs = {k: [None] * DEPTH for k in ("norm1_w", "sgu_ln_w", "sgu_ln_b", "sgu_w_spatial", "sgu_b_spatial", "sc_conv_w",
                                     "dn_conv_w", "dn_a_log", "dn_dt_bias", "dn_norm_w", "gla_w_gate2",
                                     "gla_gate_bias", "gla_norm_w", "norm2_w")}
    for l in reversed(range(DEPTH)):
        s = saved[l]
        lp = s["lp"]
        gbig["w_down"][l] = _matmul(s["act"], dx, "tn", GRAD_WIRE_DTYPE, "ffn_down_dw")
        dact = _matmul(dx, big["w_down"][l], "nt", F32, "ffn_down_dx")
        dgu = _swiglu_bwd(s["gu"], dact)
        gbig["w_gate_up"][l] = _matmul(s["h2"], dgu, "tn", GRAD_WIRE_DTYPE, "ffn_up_dw")
        dh2 = _matmul(dgu, big["w_gate_up"][l], "nt", F32, "ffn_up_dx")
        dx1, gs["norm2_w"][l] = _rmsnorm_bwd(s["x1"], small["norm2_w"][l][None], dh2, dx, "norm2_bwd")
        gbig["w_out"][l] = _matmul(s["mix"], dx1, "tn", GRAD_WIRE_DTYPE, "proj_out_dw")
        dmix = _matmul(dx1, big["w_out"][l], "nt", F32, "proj_out_dx")
        p = s["p"]
        dpu, dpv, g_lw, g_lb, g_ws, g_bs = _sgu_bwd(p, dmix, lp["ln_w"], lp["ln_b"], lp["ws"], lp["bs_t"])
        dpb, dpc, dph, g_sc = _sc_bwd(p, dmix, lp["sc_cw"])
        dcq, dck, dcv, dcs, dcz, g_dcw, g_al, g_dt, g_dnw = _dn_bwd(
            p, dmix, s["st_c"], s["inv_c"], lp["dn_cw"], lp["dn_al"], lp["dn_dt"], lp["dn_nw"])
        ddq, ddk, ddv, dds, ddz, g_w2, g_gb, g_gnw = _gla_bwd(p, dmix, s["st_d"], lp["gla_w2"], lp["gla_gb"],
                                                             lp["gla_nw"])
        dp = jnp.concatenate([dpu, dpv, dpb, dpc, dph, dcq, dck, dcv, dcz, ddq, ddk, ddv, ddz, dcs, dds], axis=1)
        gbig["w_in"][l] = _matmul(s["h1"], dp, "tn", GRAD_WIRE_DTYPE, "proj_in_dw")
        dh1 = _matmul(dp, big["w_in"][l], "nt", F32, "proj_in_dx")
        dx, gs["norm1_w"][l] = _rmsnorm_bwd(s["x0"], small["norm1_w"][l][None], dh1, dx1, "norm1_bwd")
        gs["sgu_ln_w"][l], gs["sgu_ln_b"][l] = g_lw[0], g_lb[0]
        gs["sgu_w_spatial"][l] = g_ws
        gs["sgu_b_spatial"][l] = g_bs[:, :HEADS].T
        gs["sc_conv_w"][l] = g_sc[:3]
        gs["dn_conv_w"][l] = jnp.concatenate([g_dcw[0, :4], g_dcw[1, :4], g_dcw[2, :4]], axis=1)
        gs["dn_a_log"][l], gs["dn_dt_bias"][l] = g_al[0, :HEADS], g_dt[0, :HEADS]
        gs["dn_norm_w"][l] = jnp.sum(g_dnw.reshape(HEADS, HEAD_DIM), axis=0)
        gs["gla_w_gate2"][l] = g_w2[:16]
        gs["gla_gate_bias"][l] = g_gb[0]
        gs["gla_norm_w"][l] = jnp.sum(g_gnw.reshape(HEADS, HEAD_DIM), axis=0)
        gs["norm1_w"][l] = gs["norm1_w"][l][0]
        gs["norm2_w"][l] = gs["norm2_w"][l][0]
    gsmall = {k: jnp.stack(v) for k, v in gs.items()}
    gsmall["final_norm_w"] = d_final[0]
    return loss, dx, gbig, gsmall


def _peer_chips(x, y):
    return [(1 - x, y, 2 * (1 - x) + y), (x, 1 - y, 2 * x + 1 - y), (1 - x, 1 - y, 2 * (1 - x) + 1 - y)]


def _chip_exchange(arrays, modes, name):
    na = len(arrays)

    def body(*refs):
        ins, outs = refs[:na], refs[na:2 * na]
        send1, recv1, send2, recv2, local_sems = refs[2 * na:]
        x, y, c = lax.axis_index("x"), lax.axis_index("y"), lax.axis_index("c")
        me = 2 * x + y
        sibling = (x, y, 1 - c)

        def src(a, chip):
            if modes[a] == "layer":
                return ins[a].at[c]
            return ins[a].at[chip] if modes[a] == "piece" else ins[a]

        copies = []
        for a in range(na):
            own = pltpu.make_async_copy(src(a, me), outs[a].at[c, me], local_sems.at[a])
            own.start()
            copies.append(own)
            for k, (px, py, pidx) in enumerate(_peer_chips(x, y)):
                pltpu.make_async_remote_copy(
                    src_ref=src(a, pidx), dst_ref=outs[a].at[c, me], send_sem=send1.at[a, k], recv_sem=recv1.at[a, k],
                    device_id=(px, py, c), device_id_type=MESH).start()
        for own in copies:
            own.wait()
        for a in range(na):
            for k, (px, py, pidx) in enumerate(_peer_chips(x, y)):
                cp = pltpu.make_async_remote_copy(
                    src_ref=src(a, pidx), dst_ref=outs[a].at[c, pidx], send_sem=send1.at[a, k],
                    recv_sem=recv1.at[a, k], device_id=(px, py, c), device_id_type=MESH)
                cp.wait_send()
                cp.wait_recv()
        for a in range(na):
            pltpu.make_async_remote_copy(
                src_ref=outs[a].at[c], dst_ref=outs[a].at[c], send_sem=send2.at[a], recv_sem=recv2.at[a],
                device_id=sibling, device_id_type=MESH).start()
        for a in range(na):
            cp = pltpu.make_async_remote_copy(
                src_ref=outs[a].at[c], dst_ref=outs[a].at[1 - c], send_sem=send2.at[a], recv_sem=recv2.at[a],
                device_id=sibling, device_id_type=MESH)
            cp.wait_send()
            cp.wait_recv()

    any_spec = pl.BlockSpec(memory_space=pl.ANY)
    units = [s.shape if md == "whole" else s.shape[1:] for s, md in zip(arrays, modes)]
    return pl.pallas_call(
        body, name=name,
        in_specs=[any_spec] * na, out_specs=[any_spec] * na,
        out_shape=[jax.ShapeDtypeStruct((2, N_CHIPS) + u, s.dtype) for u, s in zip(units, arrays)],
        scratch_shapes=[pltpu.SemaphoreType.DMA((na, 3)), pltpu.SemaphoreType.DMA((na, 3)),
                        pltpu.SemaphoreType.DMA((na,)), pltpu.SemaphoreType.DMA((na,)),
                        pltpu.SemaphoreType.DMA((na,))],
    )(*arrays)


def _sibling_swap(arrays):
    na = len(arrays)

    def body(*refs):
        ins, mine, theirs = refs[:na], refs[na:2 * na], refs[2 * na:3 * na]
        send_sems, recv_sems, local_sems = refs[3 * na:]
        x, y, c = lax.axis_index("x"), lax.axis_index("y"), lax.axis_index("c")
        sibling = (x, y, 1 - c)
        copies = []
        for a in range(na):
            own = pltpu.make_async_copy(ins[a].at[c], mine[a], local_sems.at[a])
            own.start()
            copies.append(own)
            pltpu.make_async_remote_copy(
                src_ref=ins[a].at[1 - c], dst_ref=theirs[a], send_sem=send_sems.at[a], recv_sem=recv_sems.at[a],
                device_id=sibling, device_id_type=MESH).start()
        for own in copies:
            own.wait()
        for a in range(na):
            cp = pltpu.make_async_remote_copy(
                src_ref=ins[a].at[1 - c], dst_ref=theirs[a], send_sem=send_sems.at[a], recv_sem=recv_sems.at[a],
                device_id=sibling, device_id_type=MESH)
            cp.wait_send()
            cp.wait_recv()

    any_spec = pl.BlockSpec(memory_space=pl.ANY)
    half = [jax.ShapeDtypeStruct(s.shape[1:], s.dtype) for s in arrays]
    out = pl.pallas_call(
        body, name="sibling_swap",
        in_specs=[any_spec] * na, out_specs=[any_spec] * (2 * na), out_shape=half + half,
        scratch_shapes=[pltpu.SemaphoreType.DMA((na,)), pltpu.SemaphoreType.DMA((na,)),
                        pltpu.SemaphoreType.DMA((na,))],
    )(*arrays)
    return out[:na], out[na:]


def _pair_add(a, b, name):
    n, r, c = a.shape
    tr = _pick_tile(r, (256, 176, 128, 64, 8))
    blk = pl.BlockSpec((1, tr, c), lambda i, j: (i, j, 0))

    def body(a_ref, b_ref, o_ref):
        o_ref[...] = (a_ref[...].astype(F32) + b_ref[...].astype(F32)).astype(o_ref.dtype)

    return pl.pallas_call(
        body, name=name, grid=(n, r // tr), in_specs=[blk, blk], out_specs=blk,
        out_shape=jax.ShapeDtypeStruct(a.shape, a.dtype),
        compiler_params=_cparams(("parallel", "parallel")),
    )(a, b)


def _adamw_math(g, w, m, v):
    m2 = ADAM_B1 * m + (1.0 - ADAM_B1) * g
    v2 = ADAM_B2 * v + (1.0 - ADAM_B2) * (g * g)
    m_hat = m2 / (1.0 - ADAM_B1 ** ADAM_STEP)
    v_hat = v2 / (1.0 - ADAM_B2 ** ADAM_STEP)
    delta = -ADAM_LR * (m_hat / (jnp.sqrt(v_hat) + ADAM_EPS) + ADAM_WD * w)
    return delta, m2, v2


def _adamw_big(contrib, w, m, v, name):
    l, r, c = w.shape
    tr = _pick_tile(r, (256, 176, 128, 64, 8))
    blk = pl.BlockSpec((1, tr, c), lambda i, j: (i, j, 0))

    def body(g_ref, w_ref, m_ref, v_ref, go_ref, d_ref, mo_ref, vo_ref):
        g = g_ref[0, 0].astype(F32)
        for s in range(1, N_CHIPS):
            g = g + g_ref[0, s].astype(F32)
        delta, m2, v2 = _adamw_math(g, w_ref[0], m_ref[0], v_ref[0])
        go_ref[0] = g
        d_ref[0] = delta
        mo_ref[0] = m2
        vo_ref[0] = v2

    return pl.pallas_call(
        body, name=name, grid=(l, r // tr),
        in_specs=[pl.BlockSpec((1, N_CHIPS, tr, c), lambda i, j: (i, 0, j, 0)), blk, blk, blk],
        out_specs=[blk] * 4, out_shape=[jax.ShapeDtypeStruct(w.shape, F32)] * 4,
        compiler_params=_cparams(("parallel", "parallel")),
    )(contrib, w, m, v)


def _sum_small(contrib):
    rows = contrib.shape[2]

    def body(g_ref, o_ref):
        total = g_ref[0, 0]
        for j in range(1, N_DEV):
            total = total + g_ref[j // N_CHIPS, j % N_CHIPS]
        o_ref[...] = total

    return pl.pallas_call(
        body, name="sum_small", out_shape=jax.ShapeDtypeStruct((rows, 128), F32),
        compiler_params=_cparams(),
    )(contrib)


def _adamw_small(g, w, m, v):
    def body(g_ref, w_ref, m_ref, v_ref, d_ref, mo_ref, vo_ref):
        delta, m2, v2 = _adamw_math(g_ref[...], w_ref[...], m_ref[...], v_ref[...])
        d_ref[...] = delta
        mo_ref[...] = m2
        vo_ref[...] = v2

    return pl.pallas_call(
        body, name="adamw_small", out_shape=[jax.ShapeDtypeStruct(g.shape, F32)] * 3,
        compiler_params=_cparams(),
    )(g, w, m, v)


def _pack(arrays):
    flat = jnp.concatenate([a.reshape(-1) for a in arrays])
    pad = (-flat.shape[0]) % 1024
    return jnp.concatenate([flat, jnp.zeros((pad,), F32)]).reshape(-1, 128)


def _unpack(packed, shapes):
    flat = packed.reshape(-1)
    out, off = [], 0
    for s in shapes:
        n = 1
        for d in s:
            n *= d
        out.append(flat[off:off + n].reshape(s))
        off += n
    return out


SMALL_NAMES = ("norm1_w", "sgu_ln_w", "sgu_ln_b", "sgu_w_spatial", "sgu_b_spatial", "sc_conv_w", "dn_conv_w",
               "dn_a_log", "dn_dt_bias", "dn_norm_w", "gla_w_gate2", "gla_gate_bias", "gla_norm_w", "norm2_w",
               "final_norm_w")
SHARDED_SMALL = ("sc_conv_w", "dn_conv_w", "gla_w_gate2")
BIG_NAMES = ("w_in", "w_out", "w_gate_up", "w_down")
WEIGHT_ORDER = ("norm1_w", "w_in", "sgu_ln_w", "sgu_ln_b", "sgu_w_spatial", "sgu_b_spatial", "sc_conv_w", "dn_conv_w",
                "dn_a_log", "dn_dt_bias", "dn_norm_w", "gla_w_gate2", "gla_gate_bias", "gla_norm_w", "w_out",
                "norm2_w", "w_gate_up", "w_down", "final_norm_w")


def _cols_from_shards(g):
    l, n, r, c = g.shape
    return jnp.transpose(g, (0, 2, 1, 3)).reshape(l, r, n * c)


def _rows_from_shards(g):
    l, n, r, c = g.shape
    return g.reshape(l, n * r, c)


def _cols_to_shards(full):
    l, r, c4 = full.shape
    return jnp.transpose(full.reshape(l, r, N_CHIPS, c4 // N_CHIPS), (0, 2, 1, 3))


def _rows_to_shards(full):
    l, r4, c = full.shape
    return full.reshape(l, N_CHIPS, r4 // N_CHIPS, c)


def kernel(x, norm1_w, w_in, sgu_ln_w, sgu_ln_b, sgu_w_spatial, sgu_b_spatial, sc_conv_w, dn_conv_w, dn_a_log, dn_dt_bias, dn_norm_w, gla_w_gate2, gla_gate_bias, gla_norm_w, w_out, norm2_w, w_gate_up, w_down, final_norm_w, loss_target, m_norm1_w, m_w_in, m_sgu_ln_w, m_sgu_ln_b, m_sgu_w_spatial, m_sgu_b_spatial, m_sc_conv_w, m_dn_conv_w, m_dn_a_log, m_dn_dt_bias, m_dn_norm_w, m_gla_w_gate2, m_gla_gate_bias, m_gla_norm_w, m_w_out, m_norm2_w, m_w_gate_up, m_w_down, m_final_norm_w, v_norm1_w, v_w_in, v_sgu_ln_w, v_sgu_ln_b, v_sgu_w_spatial, v_sgu_b_spatial, v_sc_conv_w, v_dn_conv_w, v_dn_a_log, v_dn_dt_bias, v_dn_norm_w, v_gla_w_gate2, v_gla_gate_bias, v_gla_norm_w, v_w_out, v_norm2_w, v_w_gate_up, v_w_down, v_final_norm_w):
    w = dict(norm1_w=norm1_w, w_in=w_in, sgu_ln_w=sgu_ln_w, sgu_ln_b=sgu_ln_b, sgu_w_spatial=sgu_w_spatial,
             sgu_b_spatial=sgu_b_spatial, sc_conv_w=sc_conv_w, dn_conv_w=dn_conv_w, dn_a_log=dn_a_log,
             dn_dt_bias=dn_dt_bias, dn_norm_w=dn_norm_w, gla_w_gate2=gla_w_gate2, gla_gate_bias=gla_gate_bias,
             gla_norm_w=gla_norm_w, w_out=w_out, norm2_w=norm2_w, w_gate_up=w_gate_up, w_down=w_down,
             final_norm_w=final_norm_w)
    m = dict(norm1_w=m_norm1_w, w_in=m_w_in, sgu_ln_w=m_sgu_ln_w, sgu_ln_b=m_sgu_ln_b, sgu_w_spatial=m_sgu_w_spatial,
             sgu_b_spatial=m_sgu_b_spatial, sc_conv_w=m_sc_conv_w, dn_conv_w=m_dn_conv_w, dn_a_log=m_dn_a_log,
             dn_dt_bias=m_dn_dt_bias, dn_norm_w=m_dn_norm_w, gla_w_gate2=m_gla_w_gate2,
             gla_gate_bias=m_gla_gate_bias, gla_norm_w=m_gla_norm_w, w_out=m_w_out, norm2_w=m_norm2_w,
             w_gate_up=m_w_gate_up, w_down=m_w_down, final_norm_w=m_final_norm_w)
    v = dict(norm1_w=v_norm1_w, w_in=v_w_in, sgu_ln_w=v_sgu_ln_w, sgu_ln_b=v_sgu_ln_b, sgu_w_spatial=v_sgu_w_spatial,
             sgu_b_spatial=v_sgu_b_spatial, sc_conv_w=v_sc_conv_w, dn_conv_w=v_dn_conv_w, dn_a_log=v_dn_a_log,
             dn_dt_bias=v_dn_dt_bias, dn_norm_w=v_dn_norm_w, gla_w_gate2=v_gla_w_gate2,
             gla_gate_bias=v_gla_gate_bias, gla_norm_w=v_gla_norm_w, w_out=v_w_out, norm2_w=v_norm2_w,
             w_gate_up=v_w_gate_up, w_down=v_w_down, final_norm_w=v_final_norm_w)
    chip = 2 * lax.axis_index("x") + lax.axis_index("y")

    shards = [w[n].astype(MXU_DTYPE) for n in BIG_NAMES] + [w[n] for n in SHARDED_SMALL]
    gathered = _chip_exchange(shards, ["layer"] * len(shards), "gather_weights")
    full_in = _cols_from_shards(gathered[0])
    big = dict(
        w_in=[_pad_w_in(full_in[l]) for l in range(DEPTH)],
        w_out=_rows_from_shards(gathered[1]),
        w_gate_up=_cols_from_shards(gathered[2]),
        w_down=_rows_from_shards(gathered[3]),
    )
    small = {n: w[n] for n in SMALL_NAMES if n not in SHARDED_SMALL}
    for j, n in enumerate(SHARDED_SMALL):
        small[n] = _cols_from_shards(gathered[4 + j])

    loss_tile, grad_x, gbig, gsmall = _local_step(x[0], loss_target[0], big, small)

    pieces = [
        _cols_to_shards(jnp.stack([_unpad_w_in(g) for g in gbig["w_in"]])),
        _rows_to_shards(jnp.stack(gbig["w_out"])),
        _cols_to_shards(jnp.stack(gbig["w_gate_up"])),
        _rows_to_shards(jnp.stack(gbig["w_down"])),
    ]
    small_list = [gsmall[n] for n in SMALL_NAMES] + [loss_tile[0:1, 0]]
    small_shapes = [a.shape for a in small_list]
    mine, theirs = _sibling_swap(pieces)
    parts = [_pair_add(a, b, "pair_add_" + n) for a, b, n in zip(mine, theirs, BIG_NAMES)]
    contrib = _chip_exchange(parts + [_pack(small_list)], ["piece"] * len(parts) + ["whole"], "exchange_grads")

    out_g, out_d, out_m, out_v = {}, {}, {}, {}
    for j, n in enumerate(BIG_NAMES):
        out_g[n], out_d[n], out_m[n], out_v[n] = _adamw_big(contrib[j], w[n], m[n], v[n], "adamw_" + n)
    summed = _unpack(_sum_small(contrib[len(BIG_NAMES)]), small_shapes)
    loss = summed[-1][0]
    for n, g in zip(SMALL_NAMES, summed[:-1]):
        if n in SHARDED_SMALL:
            cols = g.shape[-1] // N_CHIPS
            g = lax.dynamic_slice_in_dim(g, chip * cols, cols, axis=g.ndim - 1)
        out_g[n] = g
    shapes = [out_g[n].shape for n in SMALL_NAMES]
    d_p, m_p, v_p = _adamw_small(_pack([out_g[n] for n in SMALL_NAMES]), _pack([w[n] for n in SMALL_NAMES]),
                                 _pack([m[n] for n in SMALL_NAMES]), _pack([v[n] for n in SMALL_NAMES]))
    for n, d_, m_, v_ in zip(SMALL_NAMES, _unpack(d_p, shapes), _unpack(m_p, shapes), _unpack(v_p, shapes)):
        out_d[n], out_m[n], out_v[n] = d_, m_, v_

    return (loss, grad_x[None], *[out_g[n] for n in WEIGHT_ORDER], *[out_d[n] for n in WEIGHT_ORDER],
            *[out_m[n] for n in WEIGHT_ORDER], *[out_v[n] for n in WEIGHT_ORDER])
```

```python
import functools

import jax
import jax.numpy as jnp
from jax import lax
from jax.experimental import pallas as pl
from jax.experimental.pallas import tpu as pltpu

F32 = jnp.float32
BF16 = jnp.bfloat16
MXU_DTYPE = jnp.bfloat16
GRAD_WIRE_DTYPE = jnp.bfloat16
HI = lax.Precision.HIGHEST
MESH = pl.DeviceIdType.MESH

D_MODEL = 1024
DEPTH = 2
GROUP = 256
HEADS = 4
HEAD_DIM = 64
SGU_CHUNK = 128
SCAN_CHUNK = 64
D_FF = 2816
EPS = 1e-6
IN_COLS = 3352
P_COLS = 3584
HALO = 8
N_CHIPS = 4
N_DEV = 8
VMEM_LIMIT = 48 * 1024 * 1024

ADAM_LR = 0.001
ADAM_B1 = 0.9
ADAM_B2 = 0.999
ADAM_EPS = 1e-08
ADAM_WD = 0.01
ADAM_STEP = 10

(COL_AU, COL_AV, COL_BB, COL_BC, COL_BH, COL_CQ, COL_CK, COL_CV, COL_CZ,
 COL_DQ, COL_DK, COL_DV, COL_DZ) = range(13)
COL128_SMALL_C = 26
COL128_SMALL_D = 27


def _cparams(sem=None):
    return pltpu.CompilerParams(dimension_semantics=sem, vmem_limit_bytes=VMEM_LIMIT)


def _iota(shape, dim):
    return lax.broadcasted_iota(jnp.int32, shape, dim)


def _dg(a, b, ca, cb, prec=None):
    return lax.dot_general(a, b, (((ca,), (cb,)), ((), ())), preferred_element_type=F32, precision=prec)


@functools.partial(jax.custom_vjp, nondiff_argnums=(2, 3))
def bdot(a, b, ca, cb):
    return _dg(a.astype(MXU_DTYPE), b.astype(MXU_DTYPE), ca, cb)


def _bdot_fwd(a, b, ca, cb):
    return bdot(a, b, ca, cb), (a, b)


def _bdot_bwd(ca, cb, res, g):
    a, b = res
    if ca == 1:
        da = bdot(g, b, 1, 1 if cb == 0 else 0)
    else:
        da = bdot(b, g, 1 if cb == 0 else 0, 1)
    if cb == 0:
        db = bdot(a, g, 0, 0) if ca == 1 else bdot(a, g, 1, 0)
    else:
        db = bdot(g, a, 0, 0) if ca == 1 else bdot(g, a, 0, 1)
    return da, db


bdot.defvjp(_bdot_fwd, _bdot_bwd)


def _pieces(a, n):
    out, r = [], a
    for i in range(n):
        p = r.astype(MXU_DTYPE)
        out.append(p)
        if i + 1 < n:
            r = r - p.astype(F32)
    return out


def _mdot_impl(a, b, ca, cb, sa, sb):
    total = None
    for i, x in enumerate(_pieces(a, sa)):
        for j, y in enumerate(_pieces(b, sb)):
            if i + j < max(sa, sb):
                t = _dg(x, y, ca, cb)
                total = t if total is None else total + t
    return total


@functools.partial(jax.custom_vjp, nondiff_argnums=(2, 3, 4, 5))
def mdot(a, b, ca, cb, sa, sb):
    return _mdot_impl(a, b, ca, cb, sa, sb)


def _mdot_fwd(a, b, ca, cb, sa, sb):
    return _mdot_impl(a, b, ca, cb, sa, sb), (a, b)


def _mdot_bwd(ca, cb, sa, sb, res, g):
    a, b = res
    ga, gb = (3 if sb == 1 else 2), (3 if sa == 1 else 2)
    if sa == 1:
        da = jnp.zeros_like(a)
    elif ca == 1:
        da = mdot(g, b, 1, 1 if cb == 0 else 0, ga, sb)
    else:
        da = mdot(b, g, 1 if cb == 0 else 0, 1, sb, ga)
    if sb == 1:
        db = jnp.zeros_like(b)
    elif cb == 0:
        db = mdot(a, g, 0, 0, sa, gb) if ca == 1 else mdot(a, g, 1, 0, sa, gb)
    else:
        db = mdot(g, a, 0, 0, gb, sa) if ca == 1 else mdot(g, a, 0, 1, gb, sa)
    return da, db


mdot.defvjp(_mdot_fwd, _mdot_bwd)


def mask_r(a, m, ca=1, cb=0):
    return mdot(a, m, ca, cb, 3, 1)


def mask_l(m, b, ca=1, cb=0):
    return mdot(m, b, ca, cb, 1, 3)


def ddot(a, b, ca=1, cb=0):
    return mdot(a, b, ca, cb, 2, 2)


def _head_mask(h):
    return ((_iota((1, GROUP), 1) >> 6) == h).astype(F32)


def _block_diag_mask():
    return ((_iota((GROUP, GROUP), 0) >> 6) == (_iota((GROUP, GROUP), 1) >> 6)).astype(F32)


def _expand_mat(offset):
    return ((_iota((128, GROUP), 0) - offset) == (_iota((128, GROUP), 1) >> 6)).astype(F32)


def _tril(n, strict=False):
    r, c = _iota((n, n), 0), _iota((n, n), 1)
    return (r > c) if strict else (r >= c)


def _row_pick(x, row):
    return jnp.sum(jnp.where(_iota(x.shape, 0) == row, x, 0.0), axis=0, keepdims=True)


def _shift_rows_impl(x, halo, j):
    n = x.shape[0]
    r = _iota(x.shape, 0)
    top = jnp.concatenate([pltpu.roll(halo, j, 0), jnp.zeros((n - HALO, x.shape[1]), x.dtype)], axis=0)
    return jnp.where(r >= j, pltpu.roll(x, j, 0), top)


def _mxu_round(a):
    return a.astype(MXU_DTYPE).astype(F32)


@functools.partial(jax.custom_vjp, nondiff_argnums=(3,))
def _causal_conv(x, halo, w, width):
    xb, hb, wb = _mxu_round(x), _mxu_round(halo), _mxu_round(w)
    out = xb * _row_pick(wb, width - 1)
    for j in range(1, width):
        out = out + _shift_rows_impl(xb, hb, j) * _row_pick(wb, width - 1 - j)
    return out


def _causal_conv_fwd(x, halo, w, width):
    return _causal_conv(x, halo, w, width), (x, halo, w)


def _causal_conv_bwd(width, res, g):
    x, halo, w = res
    xb, hb, wb, gb = _mxu_round(x), _mxu_round(halo), _mxu_round(w), _mxu_round(g)
    n = g.shape[0]
    rows, rows8 = _iota(g.shape, 0), _iota(halo.shape, 0)
    dx = gb * _row_pick(wb, width - 1)
    dh = jnp.zeros_like(halo)
    dw = jnp.where(rows8 == width - 1, jnp.sum(xb * gb, axis=0, keepdims=True), 0.0)
    for j in range(1, width):
        gj = gb * _row_pick(wb, width - 1 - j)
        dx = dx + jnp.where(rows < n - j, pltpu.roll(gj, n - j, 0), 0.0)
        dh = dh + jnp.where(rows8 >= HALO - j, pltpu.roll(gj[0:HALO], HALO - j, 0), 0.0)
        tap = jnp.sum(_shift_rows_impl(xb, hb, j) * gb, axis=0, keepdims=True)
        dw = dw + jnp.where(rows8 == width - 1 - j, tap, 0.0)
    return dx, dh, dw


_causal_conv.defvjp(_causal_conv_fwd, _causal_conv_bwd)


def _head_sum(x, bd):
    return mask_r(x, bd)


def _softplus(x):
    return jnp.maximum(x, 0.0) + jnp.log1p(jnp.exp(-jnp.abs(x)))


def _log_sigmoid(x):
    return -_softplus(-x)


def _silu(x):
    return x * jax.nn.sigmoid(x)


def _head_rmsnorm_gate(o, nw, z, bd):
    ms = _head_sum(o * o, bd) * (1.0 / HEAD_DIM)
    return o * lax.rsqrt(ms + EPS) * nw * _silu(z)


def _sgu_chunk(pu, pv, ln_w, ln_b, ws0, ws1, ws2, ws3, bs_t):
    u = jax.nn.gelu(pu)
    g = jax.nn.gelu(pv)
    mu = jnp.mean(g, axis=-1, keepdims=True)
    var = jnp.mean(jnp.square(g - mu), axis=-1, keepdims=True)
    v = (g - mu) * lax.rsqrt(var + EPS) * ln_w + ln_b
    keep = _tril(SGU_CHUNK)
    mixed = mask_r(bs_t, _expand_mat(0))
    for h, ws in enumerate((ws0, ws1, ws2, ws3)):
        mixed = mixed + _head_mask(h) * bdot(jnp.where(keep, ws, 0.0), v, 1, 0)
    return u * mixed


def _sc_chunk(pb, pc, ph, halo_c, halo_h, cw):
    return pb * _causal_conv(pc * ph, halo_c * halo_h, cw, 3)


def _neumann_inverses(lows):
    n = lows[0].shape[0]
    eye = (_iota((n, n), 0) == _iota((n, n), 1)).astype(F32)
    a = [-low for low in lows]
    t = [eye + x for x in a]
    for _ in range(5):
        a = [ddot(x, x) for x in a]
        t = [ti + ddot(ti, ai) for ti, ai in zip(t, a)]
    return t


@jax.custom_vjp
def _saved_inverse(low, inv):
    return inv


def _saved_inverse_fwd(low, inv):
    return inv, inv


def _saved_inverse_bwd(inv, g):
    return -ddot(ddot(inv, g, 0, 0), inv, 1, 1), jnp.zeros_like(inv)


_saved_inverse.defvjp(_saved_inverse_fwd, _saved_inverse_bwd)


def _chunk_tril(rows):
    r, c = _iota((rows, rows), 0), _iota((rows, rows), 1)
    return ((r >> 6) == (c >> 6)) & (r >= c)


def _dn_block(pq, pk, pv, hq, hk, hv, small, pz, cwq, cwk, cwv, a_log, dt_bias, nw, state, saved_inv=None):
    c = SCAN_CHUNK
    rows = pq.shape[0]
    bd = _block_diag_mask()
    q = _silu(_causal_conv(pq, hq, cwq, 4))
    k = _silu(_causal_conv(pk, hk, cwk, 4))
    v = _silu(_causal_conv(pv, hv, cwv, 4))
    q = q * lax.rsqrt(_head_sum(q * q, bd) + EPS) * (HEAD_DIM ** -0.5)
    k = k * lax.rsqrt(_head_sum(k * k, bd) + EPS)
    lane = _iota((1, 128), 1)
    g = jnp.where(lane < HEADS, -jnp.exp(a_log) * _softplus(small + dt_bias), 0.0)
    beta_b = mask_r(jax.nn.sigmoid(small), _expand_mat(HEADS))
    gc_all = mask_l(_chunk_tril(rows).astype(F32), g)
    gcb_all = mask_r(gc_all, _expand_mat(0))
    kb_all = k * beta_b
    vb_all = v * beta_b
    kbe_all = kb_all * jnp.exp(gcb_all)
    qg_all = q * jnp.exp(gcb_all)
    causal, strict = _tril(c), _tril(c, strict=True)
    nc = rows // c
    pairs = [(ci, h) for ci in range(nc) for h in range(HEADS)]
    sls = [slice(ci * c, (ci + 1) * c) for ci in range(nc)]
    decays, lows, attns = [], [], []
    for ci, h in pairs:
        gc = gc_all[sls[ci]]
        onehot = (_iota((c, 128), 1) == h).astype(F32)
        col = mask_l(onehot, gc, 1, 1)
        row = jnp.sum(gc * onehot, axis=1, keepdims=True)
        decays.append(jnp.exp(jnp.where(causal, row - col, -jnp.inf)))
    for j, (ci, h) in enumerate(pairs):
        mh = _head_mask(h)
        k_c = k[sls[ci]]
        lows.append(jnp.where(strict, bdot(kb_all[sls[ci]] * mh, k_c, 1, 1) * decays[j], 0.0))
        attns.append(bdot(q[sls[ci]] * mh, k_c, 1, 1) * decays[j])
    if saved_inv is None:
        invs = _neumann_inverses(lows)
    else:
        invs = [_saved_inverse(low, s) for low, s in zip(lows, saved_inv)]
    us, ws = [], []
    for ci in range(nc):
        u = jnp.zeros((c, GROUP), F32)
        w = jnp.zeros((c, GROUP), F32)
        for h in range(HEADS):
            mh = _head_mask(h)
            u = u + mh * ddot(invs[ci * HEADS + h], vb_all[sls[ci]])
            w = w + mh * ddot(invs[ci * HEADS + h], kbe_all[sls[ci]])
        us.append(u)
        ws.append(w)
    outs = []
    for ci in range(nc):
        gc_b = gcb_all[sls[ci]]
        gc_last_b = _row_pick(gc_b, c - 1)
        v_new = us[ci] - bdot(ws[ci], state, 1, 0)
        o = bdot(qg_all[sls[ci]], state, 1, 0)
        for h in range(HEADS):
            o = o + _head_mask(h) * bdot(attns[ci * HEADS + h], v_new, 1, 0)
        k_dec = k[sls[ci]] * jnp.exp(gc_last_b - gc_b)
        state = state * jnp.exp(gc_last_b) + bd * bdot(k_dec, v_new, 0, 0)
        outs.append(o)
    o = jnp.concatenate(outs, axis=0)
    return _head_rmsnorm_gate(o, nw, pz, bd), state, invs


def _gla_chunk(pq, pk, pv, small, pz, w2, gbias, nw, state_t):
    c = SCAN_CHUNK
    rows = pq.shape[0]
    nc = rows // c
    sls = [slice(ci * c, (ci + 1) * c) for ci in range(nc)]
    bd = _block_diag_mask()
    log_a = _log_sigmoid(bdot(small, w2, 1, 0) + gbias) * (1.0 / 16.0)
    gcum = mask_l(_chunk_tril(rows).astype(F32), log_a)
    r, s = _iota((rows, rows), 0), _iota((rows, rows), 1)
    base = (r >> 6) << 6
    g_mid = mask_l((s == base + c // 2).astype(F32), gcum)
    g_last = mask_l((s == base + c - 1).astype(F32), gcum)
    q = pq * (HEAD_DIM ** -0.5)
    qa = q * jnp.exp(gcum - g_mid)
    ka = pk * jnp.exp(g_mid - gcum)
    qg = q * jnp.exp(gcum)
    k_last = pk * jnp.exp(g_last - gcum)
    causal = _tril(c)
    attns = [jnp.where(causal, bdot(qa[sls[ci]] * _head_mask(h), ka[sls[ci]], 1, 1), 0.0)
             for ci in range(nc) for h in range(HEADS)]
    intra = []
    for ci in range(nc):
        o = jnp.zeros((c, GROUP), F32)
        for h in range(HEADS):
            o = o + _head_mask(h) * bdot(attns[ci * HEADS + h], pv[sls[ci]], 1, 0)
        intra.append(o)
    outs = []
    for ci in range(nc):
        outs.append(intra[ci] + bdot(qg[sls[ci]], state_t, 1, 1))
        dec = _row_pick(g_last[sls[ci]], 0)
        state_t = state_t * jnp.exp(dec) + bd * bdot(pv[sls[ci]], k_last[sls[ci]], 0, 0)
    o = jnp.concatenate(outs, axis=0)
    return _head_rmsnorm_gate(o, nw, pz, bd), state_t


def _col_spec(rows, group, rev_n=None):
    if rev_n is None:
        return pl.BlockSpec((rows, GROUP), lambda i: (i, group))
    return pl.BlockSpec((rows, GROUP), lambda i: (rev_n - 1 - i, group))


def _small_spec(rows, group128, rev_n=None):
    if rev_n is None:
        return pl.BlockSpec((rows, 128), lambda i: (i, group128))
    return pl.BlockSpec((rows, 128), lambda i: (rev_n - 1 - i, group128))


def _halo_spec(rows, group, rev_n=None):
    per = rows // HALO
    if rev_n is None:
        return pl.BlockSpec((HALO, GROUP), lambda i: (jnp.maximum(i * per - 1, 0), group))
    return pl.BlockSpec((HALO, GROUP), lambda i: (jnp.maximum((rev_n - 1 - i) * per - 1, 0), group))


def _full_spec(shape):
    nd = len(shape)
    return pl.BlockSpec(shape, lambda i: (0,) * nd)


def _out_rows_spec(rows, lanes, rev_n=None):
    if rev_n is None:
        return pl.BlockSpec((rows, lanes), lambda i: (i, 0))
    return pl.BlockSpec((rows, lanes), lambda i: (rev_n - 1 - i, 0))


def _sgu_fwd(p, ln_w, ln_b, ws, bs_t):
    t = p.shape[0]
    n = t // SGU_CHUNK

    def body(pu_ref, pv_ref, lw_ref, lb_ref, ws_ref, bs_ref, y_ref):
        y = _sgu_chunk(pu_ref[...], pv_ref[...], lw_ref[...], lb_ref[...],
                       ws_ref[0], ws_ref[1], ws_ref[2], ws_ref[3], bs_ref[...])
        y_ref[...] = y.astype(y_ref.dtype)

    return pl.pallas_call(
        body, name="sgu_fwd", grid=(n,),
        in_specs=[_col_spec(SGU_CHUNK, COL_AU), _col_spec(SGU_CHUNK, COL_AV), _full_spec((1, GROUP)),
                  _full_spec((1, GROUP)), _full_spec((HEADS, SGU_CHUNK, SGU_CHUNK)), _full_spec((SGU_CHUNK, 128))],
        out_specs=_out_rows_spec(SGU_CHUNK, GROUP),
        out_shape=jax.ShapeDtypeStruct((t, GROUP), BF16),
        compiler_params=_cparams(("arbitrary",)),
    )(p, p, ln_w, ln_b, ws, bs_t)


def _sgu_bwd(p, dmix, ln_w, ln_b, ws, bs_t):
    t = p.shape[0]
    n = t // SGU_CHUNK

    def body(pu_ref, pv_ref, dy_ref, lw_ref, lb_ref, ws_ref, bs_ref,
             dpu_ref, dpv_ref, dlw_ref, dlb_ref, dws_ref, dbs_ref):
        args = (pu_ref[...], pv_ref[...], lw_ref[...], lb_ref[...],
                ws_ref[0], ws_ref[1], ws_ref[2], ws_ref[3], bs_ref[...])
        _, vjp = jax.vjp(_sgu_chunk, *args)
        dpu, dpv, dlw, dlb, d0, d1, d2, d3, dbs = vjp(dy_ref[...])
        dpu_ref[...] = dpu.astype(dpu_ref.dtype)
        dpv_ref[...] = dpv.astype(dpv_ref.dtype)

        @pl.when(pl.program_id(0) == 0)
        def _():
            dlw_ref[...] = jnp.zeros_like(dlw_ref)
            dlb_ref[...] = jnp.zeros_like(dlb_ref)
            dws_ref[...] = jnp.zeros_like(dws_ref)
            dbs_ref[...] = jnp.zeros_like(dbs_ref)

        dlw_ref[...] += dlw
        dlb_ref[...] += dlb
        for h, d in enumerate((d0, d1, d2, d3)):
            dws_ref[h] += d
        dbs_ref[...] += dbs

    return pl.pallas_call(
        body, name="sgu_bwd", grid=(n,),
        in_specs=[_col_spec(SGU_CHUNK, COL_AU), _col_spec(SGU_CHUNK, COL_AV),
                  pl.BlockSpec((SGU_CHUNK, GROUP), lambda i: (i, 0)),
                  _full_spec((1, GROUP)), _full_spec((1, GROUP)), _full_spec((HEADS, SGU_CHUNK, SGU_CHUNK)),
                  _full_spec((SGU_CHUNK, 128))],
        out_specs=[_out_rows_spec(SGU_CHUNK, GROUP), _out_rows_spec(SGU_CHUNK, GROUP), _full_spec((1, GROUP)),
                   _full_spec((1, GROUP)), _full_spec((HEADS, SGU_CHUNK, SGU_CHUNK)), _full_spec((SGU_CHUNK, 128))],
        out_shape=[jax.ShapeDtypeStruct((t, GROUP), BF16), jax.ShapeDtypeStruct((t, GROUP), BF16),
                   jax.ShapeDtypeStruct((1, GROUP), F32), jax.ShapeDtypeStruct((1, GROUP), F32),
                   jax.ShapeDtypeStruct((HEADS, SGU_CHUNK, SGU_CHUNK), F32),
                   jax.ShapeDtypeStruct((SGU_CHUNK, 128), F32)],
        compiler_params=_cparams(("arbitrary",)),
    )(p, p, dmix, ln_w, ln_b, ws, bs_t)


SC_ROWS = 256


def _first_block_zero(halo, first):
    return jnp.where(first, 0.0, halo)


def _sc_fwd(p, cw):
    t = p.shape[0]
    n = t // SC_ROWS

    def body(pb_ref, pc_ref, ph_ref, hc_ref, hh_ref, cw_ref, y_ref):
        first = pl.program_id(0) == 0
        y = _sc_chunk(pb_ref[...], pc_ref[...], ph_ref[...], _first_block_zero(hc_ref[...], first),
                      _first_block_zero(hh_ref[...], first), cw_ref[...])
        y_ref[...] = y.astype(y_ref.dtype)

    return pl.pallas_call(
        body, name="sc_fwd", grid=(n,),
        in_specs=[_col_spec(SC_ROWS, COL_BB), _col_spec(SC_ROWS, COL_BC), _col_spec(SC_ROWS, COL_BH),
                  _halo_spec(SC_ROWS, COL_BC), _halo_spec(SC_ROWS, COL_BH), _full_spec((HALO, GROUP))],
        out_specs=_out_rows_spec(SC_ROWS, GROUP),
        out_shape=jax.ShapeDtypeStruct((t, GROUP), BF16),
        compiler_params=_cparams(("arbitrary",)),
    )(p, p, p, p, p, cw)


def _add_halo_grad(d, carry):
    return d + jnp.concatenate([jnp.zeros((d.shape[0] - HALO, d.shape[1]), d.dtype), carry], axis=0)


def _sc_bwd(p, dmix, cw):
    t = p.shape[0]
    n = t // SC_ROWS

    def body(pb_ref, pc_ref, ph_ref, hc_ref, hh_ref, dy_ref, cw_ref,
             dpb_ref, dpc_ref, dph_ref, dcw_ref, carry_c, carry_h):
        i = pl.program_id(0)
        first = i == n - 1

        @pl.when(i == 0)
        def _():
            carry_c[...] = jnp.zeros_like(carry_c)
            carry_h[...] = jnp.zeros_like(carry_h)
            dcw_ref[...] = jnp.zeros_like(dcw_ref)

        args = (pb_ref[...], pc_ref[...], ph_ref[...], _first_block_zero(hc_ref[...], first),
                _first_block_zero(hh_ref[...], first), cw_ref[...])
        _, vjp = jax.vjp(_sc_chunk, *args)
        dpb, dpc, dph, dhc, dhh, dcw = vjp(dy_ref[...])
        dpb_ref[...] = dpb.astype(dpb_ref.dtype)
        dpc_ref[...] = _add_halo_grad(dpc, carry_c[...]).astype(dpc_ref.dtype)
        dph_ref[...] = _add_halo_grad(dph, carry_h[...]).astype(dph_ref.dtype)
        carry_c[...] = dhc
        carry_h[...] = dhh
        dcw_ref[...] += dcw

    return pl.pallas_call(
        body, name="sc_bwd", grid=(n,),
        in_specs=[_col_spec(SC_ROWS, COL_BB, n), _col_spec(SC_ROWS, COL_BC, n), _col_spec(SC_ROWS, COL_BH, n),
                  _halo_spec(SC_ROWS, COL_BC, n), _halo_spec(SC_ROWS, COL_BH, n),
                  pl.BlockSpec((SC_ROWS, GROUP), lambda i: (n - 1 - i, 1)), _full_spec((HALO, GROUP))],
        out_specs=[_out_rows_spec(SC_ROWS, GROUP, n)] * 3 + [_full_spec((HALO, GROUP))],
        out_shape=[jax.ShapeDtypeStruct((t, GROUP), BF16)] * 3 + [jax.ShapeDtypeStruct((HALO, GROUP), F32)],
        scratch_shapes=[pltpu.VMEM((HALO, GROUP), F32), pltpu.VMEM((HALO, GROUP), F32)],
        compiler_params=_cparams(("arbitrary",)),
    )(p, p, p, p, p, dmix, cw)


SCAN_STEP_CHUNKS = 4
SCAN_ROWS = SCAN_STEP_CHUNKS * SCAN_CHUNK


def _dn_fwd(p, cw3, a_log, dt_bias, nw):
    t = p.shape[0]
    r = SCAN_ROWS
    n = t // r

    def body(pq_ref, pk_ref, pv_ref, hq_ref, hk_ref, hv_ref, sm_ref, pz_ref, cw_ref, al_ref, dt_ref, nw_ref,
             y_ref, ck_ref, inv_ref, state):
        first = pl.program_id(0) == 0

        @pl.when(first)
        def _():
            state[...] = jnp.zeros_like(state)

        s_in = state[...]
        ck_ref[0] = s_in
        y, s_out, invs = _dn_block(pq_ref[...], pk_ref[...], pv_ref[...], _first_block_zero(hq_ref[...], first),
                                   _first_block_zero(hk_ref[...], first), _first_block_zero(hv_ref[...], first),
                                   sm_ref[...], pz_ref[...], cw_ref[0], cw_ref[1], cw_ref[2],
                                   al_ref[...], dt_ref[...], nw_ref[...], s_in)
        y_ref[...] = y.astype(y_ref.dtype)
        state[...] = s_out
        for j, inv in enumerate(invs):
            inv_ref[j] = inv

    nh = SCAN_STEP_CHUNKS * HEADS
    return pl.pallas_call(
        body, name="dn_fwd", grid=(n,),
        in_specs=[_col_spec(r, COL_CQ), _col_spec(r, COL_CK), _col_spec(r, COL_CV),
                  _halo_spec(r, COL_CQ), _halo_spec(r, COL_CK), _halo_spec(r, COL_CV),
                  _small_spec(r, COL128_SMALL_C), _col_spec(r, COL_CZ), _full_spec((3, HALO, GROUP)),
                  _full_spec((1, 128)), _full_spec((1, 128)), _full_spec((1, GROUP))],
        out_specs=[_out_rows_spec(r, GROUP), pl.BlockSpec((1, GROUP, GROUP), lambda i: (i, 0, 0)),
                   pl.BlockSpec((nh, SCAN_CHUNK, SCAN_CHUNK), lambda i: (i, 0, 0))],
        out_shape=[jax.ShapeDtypeStruct((t, GROUP), BF16), jax.ShapeDtypeStruct((n, GROUP, GROUP), F32),
                   jax.ShapeDtypeStruct((n * nh, SCAN_CHUNK, SCAN_CHUNK), F32)],
        scratch_shapes=[pltpu.VMEM((GROUP, GROUP), F32)],
        compiler_params=_cparams(("arbitrary",)),
    )(p, p, p, p, p, p, p, p, cw3, a_log, dt_bias, nw)


def _dn_bwd(p, dmix, states, invs, cw3, a_log, dt_bias, nw):
    t = p.shape[0]
    c = SCAN_ROWS
    n = t // c
    nh = SCAN_STEP_CHUNKS * HEADS

    def body(pq_ref, pk_ref, pv_ref, hq_ref, hk_ref, hv_ref, sm_ref, pz_ref, dy_ref, ck_ref, inv_ref,
             cw_ref, al_ref, dt_ref, nw_ref,
             dpq_ref, dpk_ref, dpv_ref, dsm_ref, dpz_ref, dcw_ref, dal_ref, ddt_ref, dnw_ref,
             dstate, carry):
        i = pl.program_id(0)
        first = i == n - 1

        @pl.when(i == 0)
        def _():
            dstate[...] = jnp.zeros_like(dstate)
            carry[...] = jnp.zeros_like(carry)
            dcw_ref[...] = jnp.zeros_like(dcw_ref)
            dal_ref[...] = jnp.zeros_like(dal_ref)
            ddt_ref[...] = jnp.zeros_like(ddt_ref)
            dnw_ref[...] = jnp.zeros_like(dnw_ref)

        args = (pq_ref[...], pk_ref[...], pv_ref[...], _first_block_zero(hq_ref[...], first),
                _first_block_zero(hk_ref[...], first), _first_block_zero(hv_ref[...], first),
                sm_ref[...], pz_ref[...], cw_ref[0], cw_ref[1], cw_ref[2],
                al_ref[...], dt_ref[...], nw_ref[...], ck_ref[0])
        saved = [inv_ref[j] for j in range(nh)]
        _, vjp = jax.vjp(lambda *a: _dn_block(*a, saved_inv=saved)[:2], *args)
        (dpq, dpk, dpv, dhq, dhk, dhv, dsm, dpz, dcq, dck, dcv, dal, ddt, dnw, dst) = vjp(
            (dy_ref[...], dstate[...]))
        dpq_ref[...] = _add_halo_grad(dpq, carry[0]).astype(dpq_ref.dtype)
        dpk_ref[...] = _add_halo_grad(dpk, carry[1]).astype(dpk_ref.dtype)
        dpv_ref[...] = _add_halo_grad(dpv, carry[2]).astype(dpv_ref.dtype)
        dsm_ref[...] = dsm.astype(dsm_ref.dtype)
        dpz_ref[...] = dpz.astype(dpz_ref.dtype)
        carry[0] = dhq
        carry[1] = dhk
        carry[2] = dhv
        dstate[...] = dst
        dcw_ref[0] += dcq
        dcw_ref[1] += dck
        dcw_ref[2] += dcv
        dal_ref[...] += dal
        ddt_ref[...] += ddt
        dnw_ref[...] += dnw

    return pl.pallas_call(
        body, name="dn_bwd", grid=(n,),
        in_specs=[_col_spec(c, COL_CQ, n), _col_spec(c, COL_CK, n), _col_spec(c, COL_CV, n),
                  _halo_spec(c, COL_CQ, n), _halo_spec(c, COL_CK, n), _halo_spec(c, COL_CV, n),
                  _small_spec(c, COL128_SMALL_C, n), _col_spec(c, COL_CZ, n),
                  pl.BlockSpec((c, GROUP), lambda i: (n - 1 - i, 2)),
                  pl.BlockSpec((1, GROUP, GROUP), lambda i: (n - 1 - i, 0, 0)),
                  pl.BlockSpec((nh, SCAN_CHUNK, SCAN_CHUNK), lambda i: (n - 1 - i, 0, 0)),
                  _full_spec((3, HALO, GROUP)), _full_spec((1, 128)), _full_spec((1, 128)), _full_spec((1, GROUP))],
        out_specs=[_out_rows_spec(c, GROUP, n)] * 3 + [_out_rows_spec(c, 128, n), _out_rows_spec(c, GROUP, n),
                   _full_spec((3, HALO, GROUP)), _full_spec((1, 128)), _full_spec((1, 128)), _full_spec((1, GROUP))],
        out_shape=[jax.ShapeDtypeStruct((t, GROUP), BF16)] * 3 + [
            jax.ShapeDtypeStruct((t, 128), BF16), jax.ShapeDtypeStruct((t, GROUP), BF16),
            jax.ShapeDtypeStruct((3, HALO, GROUP), F32), jax.ShapeDtypeStruct((1, 128), F32),
            jax.ShapeDtypeStruct((1, 128), F32), jax.ShapeDtypeStruct((1, GROUP), F32)],
        scratch_shapes=[pltpu.VMEM((GROUP, GROUP), F32), pltpu.VMEM((3, HALO, GROUP), F32)],
        compiler_params=_cparams(("arbitrary",)),
    )(p, p, p, p, p, p, p, p, dmix, states, invs, cw3, a_log, dt_bias, nw)


def _gla_fwd(p, w2, gbias, nw):
    t = p.shape[0]
    c = SCAN_ROWS
    n = t // c

    def body(pq_ref, pk_ref, pv_ref, sm_ref, pz_ref, w2_ref, gb_ref, nw_ref, y_ref, ck_ref, state):
        @pl.when(pl.program_id(0) == 0)
        def _():
            state[...] = jnp.zeros_like(state)

        s_in = state[...]
        ck_ref[0] = s_in
        y, s_out = _gla_chunk(pq_ref[...], pk_ref[...], pv_ref[...], sm_ref[...], pz_ref[...],
                              w2_ref[...], gb_ref[...], nw_ref[...], s_in)
        y_ref[...] = y.astype(y_ref.dtype)
        state[...] = s_out

    return pl.pallas_call(
        body, name="gla_fwd", grid=(n,),
        in_specs=[_col_spec(c, COL_DQ), _col_spec(c, COL_DK), _col_spec(c, COL_DV),
                  _small_spec(c, COL128_SMALL_D), _col_spec(c, COL_DZ),
                  _full_spec((128, GROUP)), _full_spec((1, GROUP)), _full_spec((1, GROUP))],
        out_specs=[_out_rows_spec(c, GROUP), pl.BlockSpec((1, GROUP, GROUP), lambda i: (i, 0, 0))],
        out_shape=[jax.ShapeDtypeStruct((t, GROUP), BF16), jax.ShapeDtypeStruct((n, GROUP, GROUP), F32)],
        scratch_shapes=[pltpu.VMEM((GROUP, GROUP), F32)],
        compiler_params=_cparams(("arbitrary",)),
    )(p, p, p, p, p, w2, gbias, nw)


def _gla_bwd(p, dmix, states, w2, gbias, nw):
    t = p.shape[0]
    c = SCAN_ROWS
    n = t // c

    def body(pq_ref, pk_ref, pv_ref, sm_ref, pz_ref, dy_ref, ck_ref, w2_ref, gb_ref, nw_ref,
             dpq_ref, dpk_ref, dpv_ref, dsm_ref, dpz_ref, dw2_ref, dgb_ref, dnw_ref, dstate):
        @pl.when(pl.program_id(0) == 0)
        def _():
            dstate[...] = jnp.zeros_like(dstate)
            dw2_ref[...] = jnp.zeros_like(dw2_ref)
            dgb_ref[...] = jnp.zeros_like(dgb_ref)
            dnw_ref[...] = jnp.zeros_like(dnw_ref)

        args = (pq_ref[...], pk_ref[...], pv_ref[...], sm_ref[...], pz_ref[...],
                w2_ref[...], gb_ref[...], nw_ref[...], ck_ref[0])
        _, vjp = jax.vjp(_gla_chunk, *args)
        dpq, dpk, dpv, dsm, dpz, dw2, dgb, dnw, dst = vjp((dy_ref[...], dstate[...]))
        dpq_ref[...] = dpq.astype(dpq_ref.dtype)
        dpk_ref[...] = dpk.astype(dpk_ref.dtype)
        dpv_ref[...] = dpv.astype(dpv_ref.dtype)
        dsm_ref[...] = dsm.astype(dsm_ref.dtype)
        dpz_ref[...] = dpz.astype(dpz_ref.dtype)
        dstate[...] = dst
        dw2_ref[...] += dw2
        dgb_ref[...] += dgb
        dnw_ref[...] += dnw

    return pl.pallas_call(
        body, name="gla_bwd", grid=(n,),
        in_specs=[_col_spec(c, COL_DQ, n), _col_spec(c, COL_DK, n), _col_spec(c, COL_DV, n),
                  _small_spec(c, COL128_SMALL_D, n), _col_spec(c, COL_DZ, n),
                  pl.BlockSpec((c, GROUP), lambda i: (n - 1 - i, 3)),
                  pl.BlockSpec((1, GROUP, GROUP), lambda i: (n - 1 - i, 0, 0)),
                  _full_spec((128, GROUP)), _full_spec((1, GROUP)), _full_spec((1, GROUP))],
        out_specs=[_out_rows_spec(c, GROUP, n)] * 3 + [_out_rows_spec(c, 128, n), _out_rows_spec(c, GROUP, n),
                   _full_spec((128, GROUP)), _full_spec((1, GROUP)), _full_spec((1, GROUP))],
        out_shape=[jax.ShapeDtypeStruct((t, GROUP), BF16)] * 3 + [
            jax.ShapeDtypeStruct((t, 128), BF16), jax.ShapeDtypeStruct((t, GROUP), BF16),
            jax.ShapeDtypeStruct((128, GROUP), F32), jax.ShapeDtypeStruct((1, GROUP), F32),
            jax.ShapeDtypeStruct((1, GROUP), F32)],
        scratch_shapes=[pltpu.VMEM((GROUP, GROUP), F32)],
        compiler_params=_cparams(("arbitrary",)),
    )(p, p, p, p, p, dmix, states, w2, gbias, nw)


def _pick_tile(n, pref):
    for cand in pref:
        if n % cand == 0:
            return cand
    return n


MM_TILE_CAP = 1408


def _largest_tile(n, cap):
    best = None
    for mult in range(1, cap // 128 + 1):
        if n % (128 * mult) == 0:
            best = 128 * mult
    return best if best is not None else n


def _matmul(a, b, mode, out_dtype, name, res=None):
    if mode == "nn":
        (m, k), n = a.shape, b.shape[1]
    elif mode == "nt":
        (m, k), n = a.shape, b.shape[0]
    else:
        (k, m), n = a.shape, b.shape[1]
    tm = _largest_tile(m, MM_TILE_CAP)
    tn = _largest_tile(n, MM_TILE_CAP)
    tk = _largest_tile(k, MM_TILE_CAP)
    nk = k // tk
    if mode == "nn":
        a_spec = pl.BlockSpec((tm, tk), lambda i, j, kk: (i, kk))
        b_spec = pl.BlockSpec((tk, tn), lambda i, j, kk: (kk, j))
        dims = (1, 0)
    elif mode == "nt":
        a_spec = pl.BlockSpec((tm, tk), lambda i, j, kk: (i, kk))
        b_spec = pl.BlockSpec((tn, tk), lambda i, j, kk: (j, kk))
        dims = (1, 1)
    else:
        a_spec = pl.BlockSpec((tk, tm), lambda i, j, kk: (kk, i))
        b_spec = pl.BlockSpec((tk, tn), lambda i, j, kk: (kk, j))
        dims = (0, 0)
    o_spec = pl.BlockSpec((tm, tn), lambda i, j, kk: (i, j))
    has_res = res is not None

    def body(*refs):
        a_ref, b_ref = refs[:2]
        r_ref = refs[2] if has_res else None
        o_ref = refs[3] if has_res else refs[2]
        part = _dg(a_ref[...].astype(MXU_DTYPE), b_ref[...].astype(MXU_DTYPE), *dims)

        def finish(out):
            if has_res:
                out = out + r_ref[...]
            o_ref[...] = out.astype(o_ref.dtype)

        if nk == 1:
            finish(part)
            return
        acc = refs[-1]
        kk = pl.program_id(2)

        @pl.when(kk == 0)
        def _():
            acc[...] = part

        @pl.when(kk > 0)
        def _():
            acc[...] += part

        @pl.when(kk == nk - 1)
        def _():
            finish(acc[...])

    in_specs = [a_spec, b_spec] + ([o_spec] if has_res else [])
    args = (a, b) + ((res,) if has_res else ())
    return pl.pallas_call(
        body, name=name, grid=(m // tm, n // tn, nk), in_specs=in_specs, out_specs=o_spec,
        out_shape=jax.ShapeDtypeStruct((m, n), out_dtype),
        scratch_shapes=[pltpu.VMEM((tm, tn), F32)] if nk > 1 else [],
        compiler_params=_cparams(("parallel", "parallel", "arbitrary")),
    )(*args)


NORM_ROWS = 512


def _rmsnorm_fwd(x, w, name):
    t, d = x.shape

    def body(x_ref, w_ref, o_ref):
        xv = x_ref[...]
        r = lax.rsqrt(jnp.mean(xv * xv, axis=-1, keepdims=True) + EPS)
        o_ref[...] = (xv * r * w_ref[...]).astype(o_ref.dtype)

    return pl.pallas_call(
        body, name=name, grid=(t // NORM_ROWS,),
        in_specs=[pl.BlockSpec((NORM_ROWS, d), lambda i: (i, 0)), _full_spec((1, d))],
        out_specs=pl.BlockSpec((NORM_ROWS, d), lambda i: (i, 0)),
        out_shape=jax.ShapeDtypeStruct((t, d), BF16),
        compiler_params=_cparams(("parallel",)),
    )(x, w)


def _rmsnorm_bwd(x, w, dh, dres, name):
    t, d = x.shape

    def body(x_ref, w_ref, dh_ref, dr_ref, dx_ref, dw_ref):
        @pl.when(pl.program_id(0) == 0)
        def _():
            dw_ref[...] = jnp.zeros_like(dw_ref)

        xv = x_ref[...]
        g = dh_ref[...].astype(F32)
        r = lax.rsqrt(jnp.mean(xv * xv, axis=-1, keepdims=True) + EPS)
        xhat = xv * r
        dw_ref[...] += jnp.sum(g * xhat, axis=0, keepdims=True)
        gx = g * w_ref[...]
        dx = r * (gx - xhat * jnp.mean(gx * xhat, axis=-1, keepdims=True))
        dx_ref[...] = dr_ref[...] + dx

    return pl.pallas_call(
        body, name=name, grid=(t // NORM_ROWS,),
        in_specs=[pl.BlockSpec((NORM_ROWS, d), lambda i: (i, 0)), _full_spec((1, d)),
                  pl.BlockSpec((NORM_ROWS, d), lambda i: (i, 0)), pl.BlockSpec((NORM_ROWS, d), lambda i: (i, 0))],
        out_specs=[pl.BlockSpec((NORM_ROWS, d), lambda i: (i, 0)), _full_spec((1, d))],
        out_shape=[jax.ShapeDtypeStruct((t, d), F32), jax.ShapeDtypeStruct((1, d), F32)],
        compiler_params=_cparams(("arbitrary",)),
    )(x, w, dh, dres)


SWIGLU_ROWS = 128


def _swiglu_fwd(gu):
    t = gu.shape[0]

    def body(gu_ref, o_ref):
        gate = gu_ref[:, :D_FF]
        up = gu_ref[:, D_FF:]
        o_ref[...] = (_silu(gate) * up).astype(o_ref.dtype)

    return pl.pallas_call(
        body, name="swiglu_fwd", grid=(t // SWIGLU_ROWS,),
        in_specs=[pl.BlockSpec((SWIGLU_ROWS, 2 * D_FF), lambda i: (i, 0))],
        out_specs=pl.BlockSpec((SWIGLU_ROWS, D_FF), lambda i: (i, 0)),
        out_shape=jax.ShapeDtypeStruct((t, D_FF), BF16),
        compiler_params=_cparams(("parallel",)),
    )(gu)


def _swiglu_bwd(gu, dact):
    t = gu.shape[0]

    def body(gu_ref, da_ref, o_ref):
        gate = gu_ref[:, :D_FF]
        up = gu_ref[:, D_FF:]
        da = da_ref[...]
        sg = jax.nn.sigmoid(gate)
        o_ref[:, :D_FF] = (da * up * (sg * (1.0 + gate * (1.0 - sg)))).astype(o_ref.dtype)
        o_ref[:, D_FF:] = (da * gate * sg).astype(o_ref.dtype)

    return pl.pallas_call(
        body, name="swiglu_bwd", grid=(t // SWIGLU_ROWS,),
        in_specs=[pl.BlockSpec((SWIGLU_ROWS, 2 * D_FF), lambda i: (i, 0)),
                  pl.BlockSpec((SWIGLU_ROWS, D_FF), lambda i: (i, 0))],
        out_specs=pl.BlockSpec((SWIGLU_ROWS, 2 * D_FF), lambda i: (i, 0)),
        out_shape=jax.ShapeDtypeStruct((t, 2 * D_FF), BF16),
        compiler_params=_cparams(("parallel",)),
    )(gu, dact)


def _loss_head(x, w, target):
    t, d = x.shape

    def fwd(xv, wv, tv):
        r = lax.rsqrt(jnp.mean(xv * xv, axis=-1, keepdims=True) + EPS)
        err = xv * r * wv - tv
        return 0.5 * jnp.sum(jnp.mean(err * err, axis=-1, keepdims=True), axis=0, keepdims=True)

    def body(x_ref, w_ref, t_ref, dx_ref, dw_ref, loss_ref):
        @pl.when(pl.program_id(0) == 0)
        def _():
            dw_ref[...] = jnp.zeros_like(dw_ref)
            loss_ref[...] = jnp.zeros_like(loss_ref)

        loss, vjp = jax.vjp(fwd, x_ref[...], w_ref[...], t_ref[...])
        dx, dw, _ = vjp(jnp.ones((1, 1), F32))
        dx_ref[...] = dx
        dw_ref[...] += dw
        loss_ref[...] += jnp.broadcast_to(loss, loss_ref.shape)

    return pl.pallas_call(
        body, name="loss_head", grid=(t // NORM_ROWS,),
        in_specs=[pl.BlockSpec((NORM_ROWS, d), lambda i: (i, 0)), _full_spec((1, d)),
                  pl.BlockSpec((NORM_ROWS, d), lambda i: (i, 0))],
        out_specs=[pl.BlockSpec((NORM_ROWS, d), lambda i: (i, 0)), _full_spec((1, d)), _full_spec((8, 128))],
        out_shape=[jax.ShapeDtypeStruct((t, d), F32), jax.ShapeDtypeStruct((1, d), F32),
                   jax.ShapeDtypeStruct((8, 128), F32)],
        compiler_params=_cparams(("arbitrary",)),
    )(x, w, target)


def _pad_w_in(w):
    z = lambda n: jnp.zeros((w.shape[0], n), w.dtype)
    return jnp.concatenate([w[:, 0:2048], w[:, 2056:2312], w[:, 2312:3080], w[:, 3096:3352],
                            w[:, 2048:2056], z(120), w[:, 3080:3096], z(112)], axis=1)


def _unpad_w_in(wp):
    return jnp.concatenate([wp[:, 0:2048], wp[:, 3328:3336], wp[:, 2048:2304], wp[:, 2304:3072],
                            wp[:, 3456:3472], wp[:, 3072:3328]], axis=1)


def _pad_rows(a, rows):
    return jnp.concatenate([a, jnp.zeros((rows - a.shape[0],) + a.shape[1:], a.dtype)], axis=0)


def _pad_lanes(a, lanes):
    return jnp.concatenate([a, jnp.zeros(a.shape[:-1] + (lanes - a.shape[-1],), a.dtype)], axis=-1)


def _layer_params(l, small):
    dn_cw = small["dn_conv_w"][l]
    return dict(
        ln_w=small["sgu_ln_w"][l][None], ln_b=small["sgu_ln_b"][l][None],
        ws=small["sgu_w_spatial"][l], bs_t=_pad_lanes(small["sgu_b_spatial"][l].T, 128),
        sc_cw=_pad_rows(small["sc_conv_w"][l], HALO),
        dn_cw=jnp.stack([_pad_rows(dn_cw[:, j * GROUP:(j + 1) * GROUP], HALO) for j in range(3)]),
        dn_al=_pad_lanes(small["dn_a_log"][l][None], 128), dn_dt=_pad_lanes(small["dn_dt_bias"][l][None], 128),
        dn_nw=jnp.tile(small["dn_norm_w"][l][None], (1, HEADS)),
        gla_w2=_pad_rows(small["gla_w_gate2"][l], 128), gla_gb=small["gla_gate_bias"][l][None],
        gla_nw=jnp.tile(small["gla_norm_w"][l][None], (1, HEADS)),
    )


def _local_step(x, target, big, small):
    saved = []
    h = x
    for l in range(DEPTH):
        lp = _layer_params(l, small)
        h1 = _rmsnorm_fwd(h, small["norm1_w"][l][None], "norm1_fwd")
        p = _matmul(h1, big["w_in"][l], "nn", F32, "proj_in")
        y_a = _sgu_fwd(p, lp["ln_w"], lp["ln_b"], lp["ws"], lp["bs_t"])
        y_b = _sc_fwd(p, lp["sc_cw"])
        y_c, st_c, inv_c = _dn_fwd(p, lp["dn_cw"], lp["dn_al"], lp["dn_dt"], lp["dn_nw"])
        y_d, st_d = _gla_fwd(p, lp["gla_w2"], lp["gla_gb"], lp["gla_nw"])
        mix = jnp.concatenate([y_a, y_b, y_c, y_d], axis=1)
        x1 = _matmul(mix, big["w_out"][l], "nn", F32, "proj_out", res=h)
        h2 = _rmsnorm_fwd(x1, small["norm2_w"][l][None], "norm2_fwd")
        gu = _matmul(h2, big["w_gate_up"][l], "nn", F32, "ffn_up")
        act = _swiglu_fwd(gu)
        x2 = _matmul(act, big["w_down"][l], "nn", F32, "ffn_down", res=x1)
        saved.append(dict(x0=h, h1=h1, p=p, st_c=st_c, inv_c=inv_c, st_d=st_d, mix=mix, x1=x1, h2=h2, gu=gu, act=act, lp=lp))
        h = x2

    dx, d_final, loss = _loss_head(h, small["final_norm_w"][None], target)
    gbig = {k: [None] * DEPTH for k in ("w_in", "w_out", "w_gate_up", "w_down")}
    gs = {k: [None] * DEPTH for k in ("norm1_w", "sgu_ln_w", "sgu_ln_b", "sgu_w_spatial", "sgu_b_spatial", "sc_conv_w",
                                     "dn_conv_w", "dn_a_log", "dn_dt_bias", "dn_norm_w", "gla_w_gate2",
                                     "gla_gate_bias", "gla_norm_w", "norm2_w")}
    for l in reversed(range(DEPTH)):
        s = saved[l]
        lp = s["lp"]
        gbig["w_down"][l] = _matmul(s["act"], dx, "tn", GRAD_WIRE_DTYPE, "ffn_down_dw")
        dact = _matmul(dx, big["w_down"][l], "nt", F32, "ffn_down_dx")
        dgu = _swiglu_bwd(s["gu"], dact)
        gbig["w_gate_up"][l] = _matmul(s["h2"], dgu, "tn", GRAD_WIRE_DTYPE, "ffn_up_dw")
        dh2 = _matmul(dgu, big["w_gate_up"][l], "nt", F32, "ffn_up_dx")
        dx1, gs["norm2_w"][l] = _rmsnorm_bwd(s["x1"], small["norm2_w"][l][None], dh2, dx, "norm2_bwd")
        gbig["w_out"][l] = _matmul(s["mix"], dx1, "tn", GRAD_WIRE_DTYPE, "proj_out_dw")
        dmix = _matmul(dx1, big["w_out"][l], "nt", F32, "proj_out_dx")
        p = s["p"]
        dpu, dpv, g_lw, g_lb, g_ws, g_bs = _sgu_bwd(p, dmix, lp["ln_w"], lp["ln_b"], lp["ws"], lp["bs_t"])
        dpb, dpc, dph, g_sc = _sc_bwd(p, dmix, lp["sc_cw"])
        dcq, dck, dcv, dcs, dcz, g_dcw, g_al, g_dt, g_dnw = _dn_bwd(
            p, dmix, s["st_c"], s["inv_c"], lp["dn_cw"], lp["dn_al"], lp["dn_dt"], lp["dn_nw"])
        ddq, ddk, ddv, dds, ddz, g_w2, g_gb, g_gnw = _gla_bwd(p, dmix, s["st_d"], lp["gla_w2"], lp["gla_gb"],
                                                             lp["gla_nw"])
        dp = jnp.concatenate([dpu, dpv, dpb, dpc, dph, dcq, dck, dcv, dcz, ddq, ddk, ddv, ddz, dcs, dds], axis=1)
        gbig["w_in"][l] = _matmul(s["h1"], dp, "tn", GRAD_WIRE_DTYPE, "proj_in_dw")
        dh1 = _matmul(dp, big["w_in"][l], "nt", F32, "proj_in_dx")
        dx, gs["norm1_w"][l] = _rmsnorm_bwd(s["x0"], small["norm1_w"][l][None], dh1, dx1, "norm1_bwd")
        gs["sgu_ln_w"][l], gs["sgu_ln_b"][l] = g_lw[0], g_lb[0]
        gs["sgu_w_spatial"][l] = g_ws
        gs["sgu_b_spatial"][l] = g_bs[:, :HEADS].T
        gs["sc_conv_w"][l] = g_sc[:3]
        gs["dn_conv_w"][l] = jnp.concatenate([g_dcw[0, :4], g_dcw[1, :4], g_dcw[2, :4]], axis=1)
        gs["dn_a_log"][l], gs["dn_dt_bias"][l] = g_al[0, :HEADS], g_dt[0, :HEADS]
        gs["dn_norm_w"][l] = jnp.sum(g_dnw.reshape(HEADS, HEAD_DIM), axis=0)
        gs["gla_w_gate2"][l] = g_w2[:16]
        gs["gla_gate_bias"][l] = g_gb[0]
        gs["gla_norm_w"][l] = jnp.sum(g_gnw.reshape(HEADS, HEAD_DIM), axis=0)
        gs["norm1_w"][l] = gs["norm1_w"][l][0]
        gs["norm2_w"][l] = gs["norm2_w"][l][0]
    gsmall = {k: jnp.stack(v) for k, v in gs.items()}
    gsmall["final_norm_w"] = d_final[0]
    return loss, dx, gbig, gsmall


def _peer_chips(x, y):
    return [(1 - x, y, 2 * (1 - x) + y), (x, 1 - y, 2 * x + 1 - y), (1 - x, 1 - y, 2 * (1 - x) + 1 - y)]


def _chip_exchange(arrays, modes, name):
    na = len(arrays)
    c_idx = lax.axis_index("c")
    chip = 2 * lax.axis_index("x") + lax.axis_index("y")
    bufs = []
    for arr, md in zip(arrays, modes):
        if md == "layer":
            unit = lax.dynamic_index_in_dim(arr, c_idx, 0, keepdims=False)
        elif md == "piece":
            unit = lax.dynamic_index_in_dim(arr, chip, 0, keepdims=False)
        else:
            unit = arr
        buf = lax.empty((2, N_CHIPS) + unit.shape, unit.dtype)
        bufs.append(lax.dynamic_update_slice(buf, unit[None, None], (c_idx, chip) + (0,) * unit.ndim))

    def body(*refs):
        ins, outs = refs[:na], refs[2 * na:3 * na]
        send1, recv1, send2, recv2 = refs[3 * na:]
        x, y, c = lax.axis_index("x"), lax.axis_index("y"), lax.axis_index("c")
        me = 2 * x + y
        sibling = (x, y, 1 - c)

        def src(a, chip):
            if modes[a] == "layer":
                return ins[a].at[c]
            return ins[a].at[chip] if modes[a] == "piece" else ins[a]

        for a in range(na):
            for k, (px, py, pidx) in enumerate(_peer_chips(x, y)):
                pltpu.make_async_remote_copy(
                    src_ref=src(a, pidx), dst_ref=outs[a].at[c, me], send_sem=send1.at[a, k], recv_sem=recv1.at[a, k],
                    device_id=(px, py, c), device_id_type=MESH).start()
        for a in range(na):
            for k, (px, py, pidx) in enumerate(_peer_chips(x, y)):
                cp = pltpu.make_async_remote_copy(
                    src_ref=src(a, pidx), dst_ref=outs[a].at[c, pidx], send_sem=send1.at[a, k],
                    recv_sem=recv1.at[a, k], device_id=(px, py, c), device_id_type=MESH)
                cp.wait_send()
                cp.wait_recv()
        for a in range(na):
            pltpu.make_async_remote_copy(
                src_ref=outs[a].at[c], dst_ref=outs[a].at[c], send_sem=send2.at[a], recv_sem=recv2.at[a],
                device_id=sibling, device_id_type=MESH).start()
        for a in range(na):
            cp = pltpu.make_async_remote_copy(
                src_ref=outs[a].at[c], dst_ref=outs[a].at[1 - c], send_sem=send2.at[a], recv_sem=recv2.at[a],
                device_id=sibling, device_id_type=MESH)
            cp.wait_send()
            cp.wait_recv()

    any_spec = pl.BlockSpec(memory_space=pl.ANY)
    units = [s.shape if md == "whole" else s.shape[1:] for s, md in zip(arrays, modes)]
    return pl.pallas_call(
        body, name=name,
        in_specs=[any_spec] * (2 * na), out_specs=[any_spec] * na,
        out_shape=[jax.ShapeDtypeStruct((2, N_CHIPS) + u, s.dtype) for u, s in zip(units, arrays)],
        input_output_aliases={na + a: a for a in range(na)},
        scratch_shapes=[pltpu.SemaphoreType.DMA((na, 3)), pltpu.SemaphoreType.DMA((na, 3)),
                        pltpu.SemaphoreType.DMA((na,)), pltpu.SemaphoreType.DMA((na,))],
    )(*arrays, *bufs)


def _sibling_swap(arrays):
    na = len(arrays)

    def body(*refs):
        ins, theirs = refs[:na], refs[na:2 * na]
        send_sems, recv_sems = refs[2 * na:]
        x, y, c = lax.axis_index("x"), lax.axis_index("y"), lax.axis_index("c")
        sibling = (x, y, 1 - c)
        for a in range(na):
            pltpu.make_async_remote_copy(
                src_ref=ins[a].at[1 - c], dst_ref=theirs[a], send_sem=send_sems.at[a], recv_sem=recv_sems.at[a],
                device_id=sibling, device_id_type=MESH).start()
        for a in range(na):
            cp = pltpu.make_async_remote_copy(
                src_ref=ins[a].at[1 - c], dst_ref=theirs[a], send_sem=send_sems.at[a], recv_sem=recv_sems.at[a],
                device_id=sibling, device_id_type=MESH)
            cp.wait_send()
            cp.wait_recv()

    any_spec = pl.BlockSpec(memory_space=pl.ANY)
    return pl.pallas_call(
        body, name="sibling_swap",
        in_specs=[any_spec] * na, out_specs=[any_spec] * na,
        out_shape=[jax.ShapeDtypeStruct(s.shape[1:], s.dtype) for s in arrays],
        scratch_shapes=[pltpu.SemaphoreType.DMA((na,)), pltpu.SemaphoreType.DMA((na,))],
    )(*arrays)


def _pair_add(both, theirs, name):
    n, r, c = theirs.shape
    tr = _pick_tile(r, (256, 176, 128, 64, 8))
    core = lax.axis_index("c").astype(jnp.int32).reshape(1)

    def body(core_ref, a_ref, b_ref, o_ref):
        o_ref[...] = (a_ref[...].astype(F32) + b_ref[...].astype(F32)).astype(o_ref.dtype)

    return pl.pallas_call(
        body, name=name,
        grid_spec=pltpu.PrefetchScalarGridSpec(
            num_scalar_prefetch=1, grid=(n, r // tr),
            in_specs=[pl.BlockSpec((None, 1, tr, c), lambda i, j, core_ref: (core_ref[0], i, j, 0)),
                      pl.BlockSpec((1, tr, c), lambda i, j, core_ref: (i, j, 0))],
            out_specs=pl.BlockSpec((1, tr, c), lambda i, j, core_ref: (i, j, 0))),
        out_shape=jax.ShapeDtypeStruct(theirs.shape, theirs.dtype),
        compiler_params=_cparams(("parallel", "parallel")),
    )(core, both, theirs)


def _adamw_math(g, w, m, v):
    m2 = ADAM_B1 * m + (1.0 - ADAM_B1) * g
    v2 = ADAM_B2 * v + (1.0 - ADAM_B2) * (g * g)
    m_hat = m2 / (1.0 - ADAM_B1 ** ADAM_STEP)
    v_hat = v2 / (1.0 - ADAM_B2 ** ADAM_STEP)
    delta = -ADAM_LR * (m_hat / (jnp.sqrt(v_hat) + ADAM_EPS) + ADAM_WD * w)
    return delta, m2, v2


def _adamw_big(contrib, w, m, v, name):
    l, r, c = w.shape
    tr = _pick_tile(r, (256, 176, 128, 64, 8))
    blk = pl.BlockSpec((1, tr, c), lambda i, j: (i, j, 0))

    def body(g_ref, w_ref, m_ref, v_ref, go_ref, d_ref, mo_ref, vo_ref):
        g = g_ref[0, 0].astype(F32)
        for s in range(1, N_CHIPS):
            g = g + g_ref[0, s].astype(F32)
        delta, m2, v2 = _adamw_math(g, w_ref[0], m_ref[0], v_ref[0])
        go_ref[0] = g
        d_ref[0] = delta
        mo_ref[0] = m2
        vo_ref[0] = v2

    return pl.pallas_call(
        body, name=name, grid=(l, r // tr),
        in_specs=[pl.BlockSpec((1, N_CHIPS, tr, c), lambda i, j: (i, 0, j, 0)), blk, blk, blk],
        out_specs=[blk] * 4, out_shape=[jax.ShapeDtypeStruct(w.shape, F32)] * 4,
        compiler_params=_cparams(("parallel", "parallel")),
    )(contrib, w, m, v)


def _sum_small(contrib):
    rows = contrib.shape[2]

    def body(g_ref, o_ref):
        total = g_ref[0, 0]
        for j in range(1, N_DEV):
            total = total + g_ref[j // N_CHIPS, j % N_CHIPS]
        o_ref[...] = total

    return pl.pallas_call(
        body, name="sum_small", out_shape=jax.ShapeDtypeStruct((rows, 128), F32),
        compiler_params=_cparams(),
    )(contrib)


def _adamw_small(g, w, m, v):
    def body(g_ref, w_ref, m_ref, v_ref, d_ref, mo_ref, vo_ref):
        delta, m2, v2 = _adamw_math(g_ref[...], w_ref[...], m_ref[...], v_ref[...])
        d_ref[...] = delta
        mo_ref[...] = m2
        vo_ref[...] = v2

    return pl.pallas_call(
        body, name="adamw_small", out_shape=[jax.ShapeDtypeStruct(g.shape, F32)] * 3,
        compiler_params=_cparams(),
    )(g, w, m, v)


def _pack(arrays):
    flat = jnp.concatenate([a.reshape(-1) for a in arrays])
    pad = (-flat.shape[0]) % 1024
    return jnp.concatenate([flat, jnp.zeros((pad,), F32)]).reshape(-1, 128)


def _unpack(packed, shapes):
    flat = packed.reshape(-1)
    out, off = [], 0
    for s in shapes:
        n = 1
        for d in s:
            n *= d
        out.append(flat[off:off + n].reshape(s))
        off += n
    return out


SMALL_NAMES = ("norm1_w", "sgu_ln_w", "sgu_ln_b", "sgu_w_spatial", "sgu_b_spatial", "sc_conv_w", "dn_conv_w",
               "dn_a_log", "dn_dt_bias", "dn_norm_w", "gla_w_gate2", "gla_gate_bias", "gla_norm_w", "norm2_w",
               "final_norm_w")
SHARDED_SMALL = ("sc_conv_w", "dn_conv_w", "gla_w_gate2")
BIG_NAMES = ("w_in", "w_out", "w_gate_up", "w_down")
WEIGHT_ORDER = ("norm1_w", "w_in", "sgu_ln_w", "sgu_ln_b", "sgu_w_spatial", "sgu_b_spatial", "sc_conv_w", "dn_conv_w",
                "dn_a_log", "dn_dt_bias", "dn_norm_w", "gla_w_gate2", "gla_gate_bias", "gla_norm_w", "w_out",
                "norm2_w", "w_gate_up", "w_down", "final_norm_w")


def _cols_from_shards(g):
    l, n, r, c = g.shape
    return jnp.transpose(g, (0, 2, 1, 3)).reshape(l, r, n * c)


def _rows_from_shards(g):
    l, n, r, c = g.shape
    return g.reshape(l, n * r, c)


def _cols_to_shards(full):
    l, r, c4 = full.shape
    return jnp.transpose(full.reshape(l, r, N_CHIPS, c4 // N_CHIPS), (0, 2, 1, 3))


def _rows_to_shards(full):
    l, r4, c = full.shape
    return full.reshape(l, N_CHIPS, r4 // N_CHIPS, c)


def kernel(x, norm1_w, w_in, sgu_ln_w, sgu_ln_b, sgu_w_spatial, sgu_b_spatial, sc_conv_w, dn_conv_w, dn_a_log, dn_dt_bias, dn_norm_w, gla_w_gate2, gla_gate_bias, gla_norm_w, w_out, norm2_w, w_gate_up, w_down, final_norm_w, loss_target, m_norm1_w, m_w_in, m_sgu_ln_w, m_sgu_ln_b, m_sgu_w_spatial, m_sgu_b_spatial, m_sc_conv_w, m_dn_conv_w, m_dn_a_log, m_dn_dt_bias, m_dn_norm_w, m_gla_w_gate2, m_gla_gate_bias, m_gla_norm_w, m_w_out, m_norm2_w, m_w_gate_up, m_w_down, m_final_norm_w, v_norm1_w, v_w_in, v_sgu_ln_w, v_sgu_ln_b, v_sgu_w_spatial, v_sgu_b_spatial, v_sc_conv_w, v_dn_conv_w, v_dn_a_log, v_dn_dt_bias, v_dn_norm_w, v_gla_w_gate2, v_gla_gate_bias, v_gla_norm_w, v_w_out, v_norm2_w, v_w_gate_up, v_w_down, v_final_norm_w):
    w = dict(norm1_w=norm1_w, w_in=w_in, sgu_ln_w=sgu_ln_w, sgu_ln_b=sgu_ln_b, sgu_w_spatial=sgu_w_spatial,
             sgu_b_spatial=sgu_b_spatial, sc_conv_w=sc_conv_w, dn_conv_w=dn_conv_w, dn_a_log=dn_a_log,
             dn_dt_bias=dn_dt_bias, dn_norm_w=dn_norm_w, gla_w_gate2=gla_w_gate2, gla_gate_bias=gla_gate_bias,
             gla_norm_w=gla_norm_w, w_out=w_out, norm2_w=norm2_w, w_gate_up=w_gate_up, w_down=w_down,
             final_norm_w=final_norm_w)
    m = dict(norm1_w=m_norm1_w, w_in=m_w_in, sgu_ln_w=m_sgu_ln_w, sgu_ln_b=m_sgu_ln_b, sgu_w_spatial=m_sgu_w_spatial,
             sgu_b_spatial=m_sgu_b_spatial, sc_conv_w=m_sc_conv_w, dn_conv_w=m_dn_conv_w, dn_a_log=m_dn_a_log,
             dn_dt_bias=m_dn_dt_bias, dn_norm_w=m_dn_norm_w, gla_w_gate2=m_gla_w_gate2,
             gla_gate_bias=m_gla_gate_bias, gla_norm_w=m_gla_norm_w, w_out=m_w_out, norm2_w=m_norm2_w,
             w_gate_up=m_w_gate_up, w_down=m_w_down, final_norm_w=m_final_norm_w)
    v = dict(norm1_w=v_norm1_w, w_in=v_w_in, sgu_ln_w=v_sgu_ln_w, sgu_ln_b=v_sgu_ln_b, sgu_w_spatial=v_sgu_w_spatial,
             sgu_b_spatial=v_sgu_b_spatial, sc_conv_w=v_sc_conv_w, dn_conv_w=v_dn_conv_w, dn_a_log=v_dn_a_log,
             dn_dt_bias=v_dn_dt_bias, dn_norm_w=v_dn_norm_w, gla_w_gate2=v_gla_w_gate2,
             gla_gate_bias=v_gla_gate_bias, gla_norm_w=v_gla_norm_w, w_out=v_w_out, norm2_w=v_norm2_w,
             w_gate_up=v_w_gate_up, w_down=v_w_down, final_norm_w=v_final_norm_w)
    chip = 2 * lax.axis_index("x") + lax.axis_index("y")

    shards = [w[n].astype(MXU_DTYPE) for n in BIG_NAMES] + [w[n] for n in SHARDED_SMALL]
    gathered = _chip_exchange(shards, ["layer"] * len(shards), "gather_weights")
    full_in = _cols_from_shards(gathered[0])
    big = dict(
        w_in=[_pad_w_in(full_in[l]) for l in range(DEPTH)],
        w_out=_rows_from_shards(gathered[1]),
        w_gate_up=_cols_from_shards(gathered[2]),
        w_down=_rows_from_shards(gathered[3]),
    )
    small = {n: w[n] for n in SMALL_NAMES if n not in SHARDED_SMALL}
    for j, n in enumerate(SHARDED_SMALL):
        small[n] = _cols_from_shards(gathered[4 + j])

    loss_tile, grad_x, gbig, gsmall = _local_step(x[0], loss_target[0], big, small)

    pieces = [
        _cols_to_shards(jnp.stack([_unpad_w_in(g) for g in gbig["w_in"]])),
        _rows_to_shards(jnp.stack(gbig["w_out"])),
        _cols_to_shards(jnp.stack(gbig["w_gate_up"])),
        _rows_to_shards(jnp.stack(gbig["w_down"])),
    ]
    small_list = [gsmall[n] for n in SMALL_NAMES] + [loss_tile[0:1, 0]]
    small_shapes = [a.shape for a in small_list]
    theirs = _sibling_swap(pieces)
    parts = [_pair_add(a, b, "pair_add_" + n) for a, b, n in zip(pieces, theirs, BIG_NAMES)]
    contrib = _chip_exchange(parts + [_pack(small_list)], ["piece"] * len(parts) + ["whole"], "exchange_grads")

    out_g, out_d, out_m, out_v = {}, {}, {}, {}
    for j, n in enumerate(BIG_NAMES):
        out_g[n], out_d[n], out_m[n], out_v[n] = _adamw_big(contrib[j], w[n], m[n], v[n], "adamw_" + n)
    summed = _unpack(_sum_small(contrib[len(BIG_NAMES)]), small_shapes)
    loss = summed[-1][0]
    for n, g in zip(SMALL_NAMES, summed[:-1]):
        if n in SHARDED_SMALL:
            cols = g.shape[-1] // N_CHIPS
            g = lax.dynamic_slice_in_dim(g, chip * cols, cols, axis=g.ndim - 1)
        out_g[n] = g
    shapes = [out_g[n].shape for n in SMALL_NAMES]
    d_p, m_p, v_p = _adamw_small(_pack([out_g[n] for n in SMALL_NAMES]), _pack([w[n] for n in SMALL_NAMES]),
                                 _pack([m[n] for n in SMALL_NAMES]), _pack([v[n] for n in SMALL_NAMES]))
    for n, d_, m_, v_ in zip(SMALL_NAMES, _unpack(d_p, shapes), _unpack(m_p, shapes), _unpack(v_p, shapes)):
        out_d[n], out_m[n], out_v[n] = d_, m_, v_

    return (loss, grad_x[None], *[out_g[n] for n in WEIGHT_ORDER], *[out_d[n] for n in WEIGHT_ORDER],
            *[out_m[n] for n in WEIGHT_ORDER], *[out_v[n] for n in WEIGHT_ORDER])
```

```python
import functools

import jax
import jax.numpy as jnp
from jax import lax
from jax.experimental import pallas as pl
from jax.experimental.pallas import tpu as pltpu

F32 = jnp.float32
BF16 = jnp.bfloat16
MXU_DTYPE = jnp.bfloat16
GRAD_WIRE_DTYPE = jnp.bfloat16
HI = lax.Precision.HIGHEST
MESH = pl.DeviceIdType.MESH

D_MODEL = 1024
DEPTH = 2
GROUP = 256
HEADS = 4
HEAD_DIM = 64
SGU_CHUNK = 128
SCAN_CHUNK = 64
D_FF = 2816
EPS = 1e-6
IN_COLS = 3352
P_COLS = 3584
HALO = 8
N_CHIPS = 4
N_DEV = 8
VMEM_LIMIT = 48 * 1024 * 1024

ADAM_LR = 0.001
ADAM_B1 = 0.9
ADAM_B2 = 0.999
ADAM_EPS = 1e-08
ADAM_WD = 0.01
ADAM_STEP = 10

(COL_AU, COL_AV, COL_BB, COL_BC, COL_BH, COL_CQ, COL_CK, COL_CV, COL_CZ,
 COL_DQ, COL_DK, COL_DV, COL_DZ) = range(13)
COL128_SMALL_C = 26
COL128_SMALL_D = 27


def _cparams(sem=None):
    return pltpu.CompilerParams(dimension_semantics=sem, vmem_limit_bytes=VMEM_LIMIT)


def _iota(shape, dim):
    return lax.broadcasted_iota(jnp.int32, shape, dim)


def _dg(a, b, ca, cb, prec=None):
    return lax.dot_general(a, b, (((ca,), (cb,)), ((), ())), preferred_element_type=F32, precision=prec)


@functools.partial(jax.custom_vjp, nondiff_argnums=(2, 3))
def bdot(a, b, ca, cb):
    return _dg(a.astype(MXU_DTYPE), b.astype(MXU_DTYPE), ca, cb)


def _bdot_fwd(a, b, ca, cb):
    return bdot(a, b, ca, cb), (a, b)


def _bdot_bwd(ca, cb, res, g):
    a, b = res
    if ca == 1:
        da = bdot(g, b, 1, 1 if cb == 0 else 0)
    else:
        da = bdot(b, g, 1 if cb == 0 else 0, 1)
    if cb == 0:
        db = bdot(a, g, 0, 0) if ca == 1 else bdot(a, g, 1, 0)
    else:
        db = bdot(g, a, 0, 0) if ca == 1 else bdot(g, a, 0, 1)
    return da, db


bdot.defvjp(_bdot_fwd, _bdot_bwd)


def _pieces(a, n):
    out, r = [], a
    for i in range(n):
        p = r.astype(MXU_DTYPE)
        out.append(p)
        if i + 1 < n:
            r = r - p.astype(F32)
    return out


def _mdot_impl(a, b, ca, cb, sa, sb):
    total = None
    for i, x in enumerate(_pieces(a, sa)):
        for j, y in enumerate(_pieces(b, sb)):
            if i + j < max(sa, sb):
                t = _dg(x, y, ca, cb)
                total = t if total is None else total + t
    return total


@functools.partial(jax.custom_vjp, nondiff_argnums=(2, 3, 4, 5))
def mdot(a, b, ca, cb, sa, sb):
    return _mdot_impl(a, b, ca, cb, sa, sb)


def _mdot_fwd(a, b, ca, cb, sa, sb):
    return _mdot_impl(a, b, ca, cb, sa, sb), (a, b)


def _mdot_bwd(ca, cb, sa, sb, res, g):
    a, b = res
    ga, gb = (3 if sb == 1 else 2), (3 if sa == 1 else 2)
    if sa == 1:
        da = jnp.zeros_like(a)
    elif ca == 1:
        da = mdot(g, b, 1, 1 if cb == 0 else 0, ga, sb)
    else:
        da = mdot(b, g, 1 if cb == 0 else 0, 1, sb, ga)
    if sb == 1:
        db = jnp.zeros_like(b)
    elif cb == 0:
        db = mdot(a, g, 0, 0, sa, gb) if ca == 1 else mdot(a, g, 1, 0, sa, gb)
    else:
        db = mdot(g, a, 0, 0, gb, sa) if ca == 1 else mdot(g, a, 0, 1, gb, sa)
    return da, db


mdot.defvjp(_mdot_fwd, _mdot_bwd)


def mask_r(a, m, ca=1, cb=0):
    return mdot(a, m, ca, cb, 3, 1)


def mask_l(m, b, ca=1, cb=0):
    return mdot(m, b, ca, cb, 1, 3)


def ddot(a, b, ca=1, cb=0):
    return mdot(a, b, ca, cb, 2, 2)


def _head_mask(h):
    return ((_iota((1, GROUP), 1) >> 6) == h).astype(F32)


def _block_diag_mask():
    return ((_iota((GROUP, GROUP), 0) >> 6) == (_iota((GROUP, GROUP), 1) >> 6)).astype(F32)


def _expand_mat(offset):
    return ((_iota((128, GROUP), 0) - offset) == (_iota((128, GROUP), 1) >> 6)).astype(F32)


def _tril(n, strict=False):
    r, c = _iota((n, n), 0), _iota((n, n), 1)
    return (r > c) if strict else (r >= c)


def _row_pick(x, row):
    return jnp.sum(jnp.where(_iota(x.shape, 0) == row, x, 0.0), axis=0, keepdims=True)


def _shift_rows_impl(x, halo, j):
    n = x.shape[0]
    r = _iota(x.shape, 0)
    top = jnp.concatenate([pltpu.roll(halo, j, 0), jnp.zeros((n - HALO, x.shape[1]), x.dtype)], axis=0)
    return jnp.where(r >= j, pltpu.roll(x, j, 0), top)


def _mxu_round(a):
    return a.astype(MXU_DTYPE).astype(F32)


@functools.partial(jax.custom_vjp, nondiff_argnums=(3,))
def _causal_conv(x, halo, w, width):
    xb, hb, wb = _mxu_round(x), _mxu_round(halo), _mxu_round(w)
    out = xb * _row_pick(wb, width - 1)
    for j in range(1, width):
        out = out + _shift_rows_impl(xb, hb, j) * _row_pick(wb, width - 1 - j)
    return out


def _causal_conv_fwd(x, halo, w, width):
    return _causal_conv(x, halo, w, width), (x, halo, w)


def _causal_conv_bwd(width, res, g):
    x, halo, w = res
    xb, hb, wb, gb = _mxu_round(x), _mxu_round(halo), _mxu_round(w), _mxu_round(g)
    n = g.shape[0]
    rows, rows8 = _iota(g.shape, 0), _iota(halo.shape, 0)
    dx = gb * _row_pick(wb, width - 1)
    dh = jnp.zeros_like(halo)
    dw = jnp.where(rows8 == width - 1, jnp.sum(xb * gb, axis=0, keepdims=True), 0.0)
    for j in range(1, width):
        gj = gb * _row_pick(wb, width - 1 - j)
        dx = dx + jnp.where(rows < n - j, pltpu.roll(gj, n - j, 0), 0.0)
        dh = dh + jnp.where(rows8 >= HALO - j, pltpu.roll(gj[0:HALO], HALO - j, 0), 0.0)
        tap = jnp.sum(_shift_rows_impl(xb, hb, j) * gb, axis=0, keepdims=True)
        dw = dw + jnp.where(rows8 == width - 1 - j, tap, 0.0)
    return dx, dh, dw


_causal_conv.defvjp(_causal_conv_fwd, _causal_conv_bwd)


def _head_sum(x, bd):
    return mask_r(x, bd)


def _softplus(x):
    return jnp.maximum(x, 0.0) + jnp.log1p(jnp.exp(-jnp.abs(x)))


def _log_sigmoid(x):
    return -_softplus(-x)


def _silu(x):
    return x * jax.nn.sigmoid(x)


def _head_rmsnorm_gate(o, nw, z, bd):
    ms = _head_sum(o * o, bd) * (1.0 / HEAD_DIM)
    return o * lax.rsqrt(ms + EPS) * nw * _silu(z)


def _sgu_chunk(pu, pv, ln_w, ln_b, ws0, ws1, ws2, ws3, bs_t):
    u = jax.nn.gelu(pu)
    g = jax.nn.gelu(pv)
    mu = jnp.mean(g, axis=-1, keepdims=True)
    var = jnp.mean(jnp.square(g - mu), axis=-1, keepdims=True)
    v = (g - mu) * lax.rsqrt(var + EPS) * ln_w + ln_b
    keep = _tril(SGU_CHUNK)
    mixed = mask_r(bs_t, _expand_mat(0))
    for h, ws in enumerate((ws0, ws1, ws2, ws3)):
        mixed = mixed + _head_mask(h) * bdot(jnp.where(keep, ws, 0.0), v, 1, 0)
    return u * mixed


def _sc_chunk(pb, pc, ph, halo_c, halo_h, cw):
    return pb * _causal_conv(pc * ph, halo_c * halo_h, cw, 3)


def _neumann_inverses(lows):
    n = lows[0].shape[0]
    eye = (_iota((n, n), 0) == _iota((n, n), 1)).astype(F32)
    a = [-low for low in lows]
    t = [eye + x for x in a]
    for _ in range(5):
        a = [ddot(x, x) for x in a]
        t = [ti + ddot(ti, ai) for ti, ai in zip(t, a)]
    return t


@jax.custom_vjp
def _saved_inverse(low, inv):
    return inv


def _saved_inverse_fwd(low, inv):
    return inv, inv


def _saved_inverse_bwd(inv, g):
    return -ddot(ddot(inv, g, 0, 0), inv, 1, 1), jnp.zeros_like(inv)


_saved_inverse.defvjp(_saved_inverse_fwd, _saved_inverse_bwd)


def _chunk_tril(rows):
    r, c = _iota((rows, rows), 0), _iota((rows, rows), 1)
    return ((r >> 6) == (c >> 6)) & (r >= c)


def _dn_block(pq, pk, pv, hq, hk, hv, small, pz, cwq, cwk, cwv, a_log, dt_bias, nw, state, saved_inv=None):
    c = SCAN_CHUNK
    rows = pq.shape[0]
    bd = _block_diag_mask()
    q = _silu(_causal_conv(pq, hq, cwq, 4))
    k = _silu(_causal_conv(pk, hk, cwk, 4))
    v = _silu(_causal_conv(pv, hv, cwv, 4))
    q = q * lax.rsqrt(_head_sum(q * q, bd) + EPS) * (HEAD_DIM ** -0.5)
    k = k * lax.rsqrt(_head_sum(k * k, bd) + EPS)
    lane = _iota((1, 128), 1)
    g = jnp.where(lane < HEADS, -jnp.exp(a_log) * _softplus(small + dt_bias), 0.0)
    beta_b = mask_r(jax.nn.sigmoid(small), _expand_mat(HEADS))
    gc_all = mask_l(_chunk_tril(rows).astype(F32), g)
    gcb_all = mask_r(gc_all, _expand_mat(0))
    kb_all = k * beta_b
    vb_all = v * beta_b
    kbe_all = kb_all * jnp.exp(gcb_all)
    qg_all = q * jnp.exp(gcb_all)
    causal, strict = _tril(c), _tril(c, strict=True)
    nc = rows // c
    pairs = [(ci, h) for ci in range(nc) for h in range(HEADS)]
    sls = [slice(ci * c, (ci + 1) * c) for ci in range(nc)]
    decays, lows, attns = [], [], []
    for ci, h in pairs:
        gc = gc_all[sls[ci]]
        onehot = (_iota((c, 128), 1) == h).astype(F32)
        col = mask_l(onehot, gc, 1, 1)
        row = jnp.sum(gc * onehot, axis=1, keepdims=True)
        decays.append(jnp.exp(jnp.where(causal, row - col, -jnp.inf)))
    for j, (ci, h) in enumerate(pairs):
        mh = _head_mask(h)
        k_c = k[sls[ci]]
        lows.append(jnp.where(strict, bdot(kb_all[sls[ci]] * mh, k_c, 1, 1) * decays[j], 0.0))
        attns.append(bdot(q[sls[ci]] * mh, k_c, 1, 1) * decays[j])
    if saved_inv is None:
        invs = _neumann_inverses(lows)
    else:
        invs = [_saved_inverse(low, s) for low, s in zip(lows, saved_inv)]
    us, ws = [], []
    for ci in range(nc):
        u = jnp.zeros((c, GROUP), F32)
        w = jnp.zeros((c, GROUP), F32)
        for h in range(HEADS):
            mh = _head_mask(h)
            u = u + mh * ddot(invs[ci * HEADS + h], vb_all[sls[ci]])
            w = w + mh * ddot(invs[ci * HEADS + h], kbe_all[sls[ci]])
        us.append(u)
        ws.append(w)
    outs = []
    for ci in range(nc):
        gc_b = gcb_all[sls[ci]]
        gc_last_b = _row_pick(gc_b, c - 1)
        v_new = us[ci] - bdot(ws[ci], state, 1, 0)
        o = bdot(qg_all[sls[ci]], state, 1, 0)
        for h in range(HEADS):
            o = o + _head_mask(h) * bdot(attns[ci * HEADS + h], v_new, 1, 0)
        k_dec = k[sls[ci]] * jnp.exp(gc_last_b - gc_b)
        state = state * jnp.exp(gc_last_b) + bd * bdot(k_dec, v_new, 0, 0)
        outs.append(o)
    o = jnp.concatenate(outs, axis=0)
    return _head_rmsnorm_gate(o, nw, pz, bd), state, invs


def _gla_chunk(pq, pk, pv, small, pz, w2, gbias, nw, state_t):
    c = SCAN_CHUNK
    rows = pq.shape[0]
    nc = rows // c
    sls = [slice(ci * c, (ci + 1) * c) for ci in range(nc)]
    bd = _block_diag_mask()
    log_a = _log_sigmoid(bdot(small, w2, 1, 0) + gbias) * (1.0 / 16.0)
    gcum = mask_l(_chunk_tril(rows).astype(F32), log_a)
    r, s = _iota((rows, rows), 0), _iota((rows, rows), 1)
    base = (r >> 6) << 6
    g_mid = mask_l((s == base + c // 2).astype(F32), gcum)
    g_last = mask_l((s == base + c - 1).astype(F32), gcum)
    q = pq * (HEAD_DIM ** -0.5)
    qa = q * jnp.exp(gcum - g_mid)
    ka = pk * jnp.exp(g_mid - gcum)
    qg = q * jnp.exp(gcum)
    k_last = pk * jnp.exp(g_last - gcum)
    causal = _tril(c)
    attns = [jnp.where(causal, bdot(qa[sls[ci]] * _head_mask(h), ka[sls[ci]], 1, 1), 0.0)
             for ci in range(nc) for h in range(HEADS)]
    intra = []
    for ci in range(nc):
        o = jnp.zeros((c, GROUP), F32)
        for h in range(HEADS):
            o = o + _head_mask(h) * bdot(attns[ci * HEADS + h], pv[sls[ci]], 1, 0)
        intra.append(o)
    outs = []
    for ci in range(nc):
        outs.append(intra[ci] + bdot(qg[sls[ci]], state_t, 1, 1))
        dec = _row_pick(g_last[sls[ci]], 0)
        state_t = state_t * jnp.exp(dec) + bd * bdot(pv[sls[ci]], k_last[sls[ci]], 0, 0)
    o = jnp.concatenate(outs, axis=0)
    return _head_rmsnorm_gate(o, nw, pz, bd), state_t


def _col_spec(rows, group, rev_n=None):
    if rev_n is None:
        return pl.BlockSpec((rows, GROUP), lambda i: (i, group))
    return pl.BlockSpec((rows, GROUP), lambda i: (rev_n - 1 - i, group))


def _small_spec(rows, group128, rev_n=None):
    if rev_n is None:
        return pl.BlockSpec((rows, 128), lambda i: (i, group128))
    return pl.BlockSpec((rows, 128), lambda i: (rev_n - 1 - i, group128))


def _halo_spec(rows, group, rev_n=None):
    per = rows // HALO
    if rev_n is None:
        return pl.BlockSpec((HALO, GROUP), lambda i: (jnp.maximum(i * per - 1, 0), group))
    return pl.BlockSpec((HALO, GROUP), lambda i: (jnp.maximum((rev_n - 1 - i) * per - 1, 0), group))


def _full_spec(shape):
    nd = len(shape)
    return pl.BlockSpec(shape, lambda i: (0,) * nd)


def _out_rows_spec(rows, lanes, rev_n=None):
    if rev_n is None:
        return pl.BlockSpec((rows, lanes), lambda i: (i, 0))
    return pl.BlockSpec((rows, lanes), lambda i: (rev_n - 1 - i, 0))


def _sgu_fwd(p, ln_w, ln_b, ws, bs_t):
    t = p.shape[0]
    n = t // SGU_CHUNK

    def body(pu_ref, pv_ref, lw_ref, lb_ref, ws_ref, bs_ref, y_ref):
        y = _sgu_chunk(pu_ref[...], pv_ref[...], lw_ref[...], lb_ref[...],
                       ws_ref[0], ws_ref[1], ws_ref[2], ws_ref[3], bs_ref[...])
        y_ref[...] = y.astype(y_ref.dtype)

    return pl.pallas_call(
        body, name="sgu_fwd", grid=(n,),
        in_specs=[_col_spec(SGU_CHUNK, COL_AU), _col_spec(SGU_CHUNK, COL_AV), _full_spec((1, GROUP)),
                  _full_spec((1, GROUP)), _full_spec((HEADS, SGU_CHUNK, SGU_CHUNK)), _full_spec((SGU_CHUNK, 128))],
        out_specs=_out_rows_spec(SGU_CHUNK, GROUP),
        out_shape=jax.ShapeDtypeStruct((t, GROUP), BF16),
        compiler_params=_cparams(("arbitrary",)),
    )(p, p, ln_w, ln_b, ws, bs_t)


def _sgu_bwd(p, dmix, ln_w, ln_b, ws, bs_t):
    t = p.shape[0]
    n = t // SGU_CHUNK

    def body(pu_ref, pv_ref, dy_ref, lw_ref, lb_ref, ws_ref, bs_ref,
             dpu_ref, dpv_ref, dlw_ref, dlb_ref, dws_ref, dbs_ref):
        args = (pu_ref[...], pv_ref[...], lw_ref[...], lb_ref[...],
                ws_ref[0], ws_ref[1], ws_ref[2], ws_ref[3], bs_ref[...])
        _, vjp = jax.vjp(_sgu_chunk, *args)
        dpu, dpv, dlw, dlb, d0, d1, d2, d3, dbs = vjp(dy_ref[...])
        dpu_ref[...] = dpu.astype(dpu_ref.dtype)
        dpv_ref[...] = dpv.astype(dpv_ref.dtype)

        @pl.when(pl.program_id(0) == 0)
        def _():
            dlw_ref[...] = jnp.zeros_like(dlw_ref)
            dlb_ref[...] = jnp.zeros_like(dlb_ref)
            dws_ref[...] = jnp.zeros_like(dws_ref)
            dbs_ref[...] = jnp.zeros_like(dbs_ref)

        dlw_ref[...] += dlw
        dlb_ref[...] += dlb
        for h, d in enumerate((d0, d1, d2, d3)):
            dws_ref[h] += d
        dbs_ref[...] += dbs

    return pl.pallas_call(
        body, name="sgu_bwd", grid=(n,),
        in_specs=[_col_spec(SGU_CHUNK, COL_AU), _col_spec(SGU_CHUNK, COL_AV),
                  pl.BlockSpec((SGU_CHUNK, GROUP), lambda i: (i, 0)),
                  _full_spec((1, GROUP)), _full_spec((1, GROUP)), _full_spec((HEADS, SGU_CHUNK, SGU_CHUNK)),
                  _full_spec((SGU_CHUNK, 128))],
        out_specs=[_out_rows_spec(SGU_CHUNK, GROUP), _out_rows_spec(SGU_CHUNK, GROUP), _full_spec((1, GROUP)),
                   _full_spec((1, GROUP)), _full_spec((HEADS, SGU_CHUNK, SGU_CHUNK)), _full_spec((SGU_CHUNK, 128))],
        out_shape=[jax.ShapeDtypeStruct((t, GROUP), BF16), jax.ShapeDtypeStruct((t, GROUP), BF16),
                   jax.ShapeDtypeStruct((1, GROUP), F32), jax.ShapeDtypeStruct((1, GROUP), F32),
                   jax.ShapeDtypeStruct((HEADS, SGU_CHUNK, SGU_CHUNK), F32),
                   jax.ShapeDtypeStruct((SGU_CHUNK, 128), F32)],
        compiler_params=_cparams(("arbitrary",)),
    )(p, p, dmix, ln_w, ln_b, ws, bs_t)


SC_ROWS = 256


def _first_block_zero(halo, first):
    return jnp.where(first, 0.0, halo)


def _sc_fwd(p, cw):
    t = p.shape[0]
    n = t // SC_ROWS

    def body(pb_ref, pc_ref, ph_ref, hc_ref, hh_ref, cw_ref, y_ref):
        first = pl.program_id(0) == 0
        y = _sc_chunk(pb_ref[...], pc_ref[...], ph_ref[...], _first_block_zero(hc_ref[...], first),
                      _first_block_zero(hh_ref[...], first), cw_ref[...])
        y_ref[...] = y.astype(y_ref.dtype)

    return pl.pallas_call(
        body, name="sc_fwd", grid=(n,),
        in_specs=[_col_spec(SC_ROWS, COL_BB), _col_spec(SC_ROWS, COL_BC), _col_spec(SC_ROWS, COL_BH),
                  _halo_spec(SC_ROWS, COL_BC), _halo_spec(SC_ROWS, COL_BH), _full_spec((HALO, GROUP))],
        out_specs=_out_rows_spec(SC_ROWS, GROUP),
        out_shape=jax.ShapeDtypeStruct((t, GROUP), BF16),
        compiler_params=_cparams(("arbitrary",)),
    )(p, p, p, p, p, cw)


def _add_halo_grad(d, carry):
    return d + jnp.concatenate([jnp.zeros((d.shape[0] - HALO, d.shape[1]), d.dtype), carry], axis=0)


def _sc_bwd(p, dmix, cw):
    t = p.shape[0]
    n = t // SC_ROWS

    def body(pb_ref, pc_ref, ph_ref, hc_ref, hh_ref, dy_ref, cw_ref,
             dpb_ref, dpc_ref, dph_ref, dcw_ref, carry_c, carry_h):
        i = pl.program_id(0)
        first = i == n - 1

        @pl.when(i == 0)
        def _():
            carry_c[...] = jnp.zeros_like(carry_c)
            carry_h[...] = jnp.zeros_like(carry_h)
            dcw_ref[...] = jnp.zeros_like(dcw_ref)

        args = (pb_ref[...], pc_ref[...], ph_ref[...], _first_block_zero(hc_ref[...], first),
                _first_block_zero(hh_ref[...], first), cw_ref[...])
        _, vjp = jax.vjp(_sc_chunk, *args)
        dpb, dpc, dph, dhc, dhh, dcw = vjp(dy_ref[...])
        dpb_ref[...] = dpb.astype(dpb_ref.dtype)
        dpc_ref[...] = _add_halo_grad(dpc, carry_c[...]).astype(dpc_ref.dtype)
        dph_ref[...] = _add_halo_grad(dph, carry_h[...]).astype(dph_ref.dtype)
        carry_c[...] = dhc
        carry_h[...] = dhh
        dcw_ref[...] += dcw

    return pl.pallas_call(
        body, name="sc_bwd", grid=(n,),
        in_specs=[_col_spec(SC_ROWS, COL_BB, n), _col_spec(SC_ROWS, COL_BC, n), _col_spec(SC_ROWS, COL_BH, n),
                  _halo_spec(SC_ROWS, COL_BC, n), _halo_spec(SC_ROWS, COL_BH, n),
                  pl.BlockSpec((SC_ROWS, GROUP), lambda i: (n - 1 - i, 1)), _full_spec((HALO, GROUP))],
        out_specs=[_out_rows_spec(SC_ROWS, GROUP, n)] * 3 + [_full_spec((HALO, GROUP))],
        out_shape=[jax.ShapeDtypeStruct((t, GROUP), BF16)] * 3 + [jax.ShapeDtypeStruct((HALO, GROUP), F32)],
        scratch_shapes=[pltpu.VMEM((HALO, GROUP), F32), pltpu.VMEM((HALO, GROUP), F32)],
        compiler_params=_cparams(("arbitrary",)),
    )(p, p, p, p, p, dmix, cw)


SCAN_STEP_CHUNKS = 4
SCAN_ROWS = SCAN_STEP_CHUNKS * SCAN_CHUNK


def _dn_fwd(p, cw3, a_log, dt_bias, nw):
    t = p.shape[0]
    r = SCAN_ROWS
    n = t // r

    def body(pq_ref, pk_ref, pv_ref, hq_ref, hk_ref, hv_ref, sm_ref, pz_ref, cw_ref, al_ref, dt_ref, nw_ref,
             y_ref, ck_ref, inv_ref, state):
        first = pl.program_id(0) == 0

        @pl.when(first)
        def _():
            state[...] = jnp.zeros_like(state)

        s_in = state[...]
        ck_ref[0] = s_in
        y, s_out, invs = _dn_block(pq_ref[...], pk_ref[...], pv_ref[...], _first_block_zero(hq_ref[...], first),
                                   _first_block_zero(hk_ref[...], first), _first_block_zero(hv_ref[...], first),
                                   sm_ref[...], pz_ref[...], cw_ref[0], cw_ref[1], cw_ref[2],
                                   al_ref[...], dt_ref[...], nw_ref[...], s_in)
        y_ref[...] = y.astype(y_ref.dtype)
        state[...] = s_out
        for j, inv in enumerate(invs):
            inv_ref[j] = inv

    nh = SCAN_STEP_CHUNKS * HEADS
    return pl.pallas_call(
        body, name="dn_fwd", grid=(n,),
        in_specs=[_col_spec(r, COL_CQ), _col_spec(r, COL_CK), _col_spec(r, COL_CV),
                  _halo_spec(r, COL_CQ), _halo_spec(r, COL_CK), _halo_spec(r, COL_CV),
                  _small_spec(r, COL128_SMALL_C), _col_spec(r, COL_CZ), _full_spec((3, HALO, GROUP)),
                  _full_spec((1, 128)), _full_spec((1, 128)), _full_spec((1, GROUP))],
        out_specs=[_out_rows_spec(r, GROUP), pl.BlockSpec((1, GROUP, GROUP), lambda i: (i, 0, 0)),
                   pl.BlockSpec((nh, SCAN_CHUNK, SCAN_CHUNK), lambda i: (i, 0, 0))],
        out_shape=[jax.ShapeDtypeStruct((t, GROUP), BF16), jax.ShapeDtypeStruct((n, GROUP, GROUP), F32),
                   jax.ShapeDtypeStruct((n * nh, SCAN_CHUNK, SCAN_CHUNK), F32)],
        scratch_shapes=[pltpu.VMEM((GROUP, GROUP), F32)],
        compiler_params=_cparams(("arbitrary",)),
    )(p, p, p, p, p, p, p, p, cw3, a_log, dt_bias, nw)


def _dn_bwd(p, dmix, states, invs, cw3, a_log, dt_bias, nw):
    t = p.shape[0]
    c = SCAN_ROWS
    n = t // c
    nh = SCAN_STEP_CHUNKS * HEADS

    def body(pq_ref, pk_ref, pv_ref, hq_ref, hk_ref, hv_ref, sm_ref, pz_ref, dy_ref, ck_ref, inv_ref,
             cw_ref, al_ref, dt_ref, nw_ref,
             dpq_ref, dpk_ref, dpv_ref, dsm_ref, dpz_ref, dcw_ref, dal_ref, ddt_ref, dnw_ref,
             dstate, carry):
        i = pl.program_id(0)
        first = i == n - 1

        @pl.when(i == 0)
        def _():
            dstate[...] = jnp.zeros_like(dstate)
            carry[...] = jnp.zeros_like(carry)
            dcw_ref[...] = jnp.zeros_like(dcw_ref)
            dal_ref[...] = jnp.zeros_like(dal_ref)
            ddt_ref[...] = jnp.zeros_like(ddt_ref)
            dnw_ref[...] = jnp.zeros_like(dnw_ref)

        args = (pq_ref[...], pk_ref[...], pv_ref[...], _first_block_zero(hq_ref[...], first),
                _first_block_zero(hk_ref[...], first), _first_block_zero(hv_ref[...], first),
                sm_ref[...], pz_ref[...], cw_ref[0], cw_ref[1], cw_ref[2],
                al_ref[...], dt_ref[...], nw_ref[...], ck_ref[0])
        saved = [inv_ref[j] for j in range(nh)]
        _, vjp = jax.vjp(lambda *a: _dn_block(*a, saved_inv=saved)[:2], *args)
        (dpq, dpk, dpv, dhq, dhk, dhv, dsm, dpz, dcq, dck, dcv, dal, ddt, dnw, dst) = vjp(
            (dy_ref[...], dstate[...]))
        dpq_ref[...] = _add_halo_grad(dpq, carry[0]).astype(dpq_ref.dtype)
        dpk_ref[...] = _add_halo_grad(dpk, carry[1]).astype(dpk_ref.dtype)
        dpv_ref[...] = _add_halo_grad(dpv, carry[2]).astype(dpv_ref.dtype)
        dsm_ref[...] = dsm.astype(dsm_ref.dtype)
        dpz_ref[...] = dpz.astype(dpz_ref.dtype)
        carry[0] = dhq
        carry[1] = dhk
        carry[2] = dhv
        dstate[...] = dst
        dcw_ref[0] += dcq
        dcw_ref[1] += dck
        dcw_ref[2] += dcv
        dal_ref[...] += dal
        ddt_ref[...] += ddt
        dnw_ref[...] += dnw

    return pl.pallas_call(
        body, name="dn_bwd", grid=(n,),
        in_specs=[_col_spec(c, COL_CQ, n), _col_spec(c, COL_CK, n), _col_spec(c, COL_CV, n),
                  _halo_spec(c, COL_CQ, n), _halo_spec(c, COL_CK, n), _halo_spec(c, COL_CV, n),
                  _small_spec(c, COL128_SMALL_C, n), _col_spec(c, COL_CZ, n),
                  pl.BlockSpec((c, GROUP), lambda i: (n - 1 - i, 2)),
                  pl.BlockSpec((1, GROUP, GROUP), lambda i: (n - 1 - i, 0, 0)),
                  pl.BlockSpec((nh, SCAN_CHUNK, SCAN_CHUNK), lambda i: (n - 1 - i, 0, 0)),
                  _full_spec((3, HALO, GROUP)), _full_spec((1, 128)), _full_spec((1, 128)), _full_spec((1, GROUP))],
        out_specs=[_out_rows_spec(c, GROUP, n)] * 3 + [_out_rows_spec(c, 128, n), _out_rows_spec(c, GROUP, n),
                   _full_spec((3, HALO, GROUP)), _full_spec((1, 128)), _full_spec((1, 128)), _full_spec((1, GROUP))],
        out_shape=[jax.ShapeDtypeStruct((t, GROUP), BF16)] * 3 + [
            jax.ShapeDtypeStruct((t, 128), BF16), jax.ShapeDtypeStruct((t, GROUP), BF16),
            jax.ShapeDtypeStruct((3, HALO, GROUP), F32), jax.ShapeDtypeStruct((1, 128), F32),
            jax.ShapeDtypeStruct((1, 128), F32), jax.ShapeDtypeStruct((1, GROUP), F32)],
        scratch_shapes=[pltpu.VMEM((GROUP, GROUP), F32), pltpu.VMEM((3, HALO, GROUP), F32)],
        compiler_params=_cparams(("arbitrary",)),
    )(p, p, p, p, p, p, p, p, dmix, states, invs, cw3, a_log, dt_bias, nw)


def _gla_fwd(p, w2, gbias, nw):
    t = p.shape[0]
    c = SCAN_ROWS
    n = t // c

    def body(pq_ref, pk_ref, pv_ref, sm_ref, pz_ref, w2_ref, gb_ref, nw_ref, y_ref, ck_ref, state):
        @pl.when(pl.program_id(0) == 0)
        def _():
            state[...] = jnp.zeros_like(state)

        s_in = state[...]
        ck_ref[0] = s_in
        y, s_out = _gla_chunk(pq_ref[...], pk_ref[...], pv_ref[...], sm_ref[...], pz_ref[...],
                              w2_ref[...], gb_ref[...], nw_ref[...], s_in)
        y_ref[...] = y.astype(y_ref.dtype)
        state[...] = s_out

    return pl.pallas_call(
        body, name="gla_fwd", grid=(n,),
        in_specs=[_col_spec(c, COL_DQ), _col_spec(c, COL_DK), _col_spec(c, COL_DV),
                  _small_spec(c, COL128_SMALL_D), _col_spec(c, COL_DZ),
                  _full_spec((128, GROUP)), _full_spec((1, GROUP)), _full_spec((1, GROUP))],
        out_specs=[_out_rows_spec(c, GROUP), pl.BlockSpec((1, GROUP, GROUP), lambda i: (i, 0, 0))],
        out_shape=[jax.ShapeDtypeStruct((t, GROUP), BF16), jax.ShapeDtypeStruct((n, GROUP, GROUP), F32)],
        scratch_shapes=[pltpu.VMEM((GROUP, GROUP), F32)],
        compiler_params=_cparams(("arbitrary",)),
    )(p, p, p, p, p, w2, gbias, nw)


def _gla_bwd(p, dmix, states, w2, gbias, nw):
    t = p.shape[0]
    c = SCAN_ROWS
    n = t // c

    def body(pq_ref, pk_ref, pv_ref, sm_ref, pz_ref, dy_ref, ck_ref, w2_ref, gb_ref, nw_ref,
             dpq_ref, dpk_ref, dpv_ref, dsm_ref, dpz_ref, dw2_ref, dgb_ref, dnw_ref, dstate):
        @pl.when(pl.program_id(0) == 0)
        def _():
            dstate[...] = jnp.zeros_like(dstate)
            dw2_ref[...] = jnp.zeros_like(dw2_ref)
            dgb_ref[...] = jnp.zeros_like(dgb_ref)
            dnw_ref[...] = jnp.zeros_like(dnw_ref)

        args = (pq_ref[...], pk_ref[...], pv_ref[...], sm_ref[...], pz_ref[...],
                w2_ref[...], gb_ref[...], nw_ref[...], ck_ref[0])
        _, vjp = jax.vjp(_gla_chunk, *args)
        dpq, dpk, dpv, dsm, dpz, dw2, dgb, dnw, dst = vjp((dy_ref[...], dstate[...]))
        dpq_ref[...] = dpq.astype(dpq_ref.dtype)
        dpk_ref[...] = dpk.astype(dpk_ref.dtype)
        dpv_ref[...] = dpv.astype(dpv_ref.dtype)
        dsm_ref[...] = dsm.astype(dsm_ref.dtype)
        dpz_ref[...] = dpz.astype(dpz_ref.dtype)
        dstate[...] = dst
        dw2_ref[...] += dw2
        dgb_ref[...] += dgb
        dnw_ref[...] += dnw

    return pl.pallas_call(
        body, name="gla_bwd", grid=(n,),
        in_specs=[_col_spec(c, COL_DQ, n), _col_spec(c, COL_DK, n), _col_spec(c, COL_DV, n),
                  _small_spec(c, COL128_SMALL_D, n), _col_spec(c, COL_DZ, n),
                  pl.BlockSpec((c, GROUP), lambda i: (n - 1 - i, 3)),
                  pl.BlockSpec((1, GROUP, GROUP), lambda i: (n - 1 - i, 0, 0)),
                  _full_spec((128, GROUP)), _full_spec((1, GROUP)), _full_spec((1, GROUP))],
        out_specs=[_out_rows_spec(c, GROUP, n)] * 3 + [_out_rows_spec(c, 128, n), _out_rows_spec(c, GROUP, n),
                   _full_spec((128, GROUP)), _full_spec((1, GROUP)), _full_spec((1, GROUP))],
        out_shape=[jax.ShapeDtypeStruct((t, GROUP), BF16)] * 3 + [
            jax.ShapeDtypeStruct((t, 128), BF16), jax.ShapeDtypeStruct((t, GROUP), BF16),
            jax.ShapeDtypeStruct((128, GROUP), F32), jax.ShapeDtypeStruct((1, GROUP), F32),
            jax.ShapeDtypeStruct((1, GROUP), F32)],
        scratch_shapes=[pltpu.VMEM((GROUP, GROUP), F32)],
        compiler_params=_cparams(("arbitrary",)),
    )(p, p, p, p, p, dmix, states, w2, gbias, nw)


def _pick_tile(n, pref):
    for cand in pref:
        if n % cand == 0:
            return cand
    return n


MM_TILE_CAP = 1408


def _largest_tile(n, cap):
    best = None
    for mult in range(1, cap // 128 + 1):
        if n % (128 * mult) == 0:
            best = 128 * mult
    return best if best is not None else n


def _matmul(a, b, mode, out_dtype, name, res=None):
    if mode == "nn":
        (m, k), n = a.shape, b.shape[1]
    elif mode == "nt":
        (m, k), n = a.shape, b.shape[0]
    else:
        (k, m), n = a.shape, b.shape[1]
    tm = _largest_tile(m, MM_TILE_CAP)
    tn = _largest_tile(n, MM_TILE_CAP)
    tk = _largest_tile(k, MM_TILE_CAP)
    nk = k // tk
    if mode == "nn":
        a_spec = pl.BlockSpec((tm, tk), lambda i, j, kk: (i, kk))
        b_spec = pl.BlockSpec((tk, tn), lambda i, j, kk: (kk, j))
        dims = (1, 0)
    elif mode == "nt":
        a_spec = pl.BlockSpec((tm, tk), lambda i, j, kk: (i, kk))
        b_spec = pl.BlockSpec((tn, tk), lambda i, j, kk: (j, kk))
        dims = (1, 1)
    else:
        a_spec = pl.BlockSpec((tk, tm), lambda i, j, kk: (kk, i))
        b_spec = pl.BlockSpec((tk, tn), lambda i, j, kk: (kk, j))
        dims = (0, 0)
    o_spec = pl.BlockSpec((tm, tn), lambda i, j, kk: (i, j))
    has_res = res is not None

    def body(*refs):
        a_ref, b_ref = refs[:2]
        r_ref = refs[2] if has_res else None
        o_ref = refs[3] if has_res else refs[2]
        part = _dg(a_ref[...].astype(MXU_DTYPE), b_ref[...].astype(MXU_DTYPE), *dims)

        def finish(out):
            if has_res:
                out = out + r_ref[...]
            o_ref[...] = out.astype(o_ref.dtype)

        if nk == 1:
            finish(part)
            return
        acc = refs[-1]
        kk = pl.program_id(2)

        @pl.when(kk == 0)
        def _():
            acc[...] = part

        @pl.when(kk > 0)
        def _():
            acc[...] += part

        @pl.when(kk == nk - 1)
        def _():
            finish(acc[...])

    in_specs = [a_spec, b_spec] + ([o_spec] if has_res else [])
    args = (a, b) + ((res,) if has_res else ())
    return pl.pallas_call(
        body, name=name, grid=(m // tm, n // tn, nk), in_specs=in_specs, out_specs=o_spec,
        out_shape=jax.ShapeDtypeStruct((m, n), out_dtype),
        scratch_shapes=[pltpu.VMEM((tm, tn), F32)] if nk > 1 else [],
        compiler_params=_cparams(("parallel", "parallel", "arbitrary")),
    )(*args)


def _matmul_nt_norm_bwd(a, b, x, w, dres, name):
    (m, k), n = a.shape, b.shape[0]
    tm = _largest_tile(m, 512)
    tk = _largest_tile(k, MM_TILE_CAP)
    nk = k // tk

    def body(a_ref, b_ref, x_ref, w_ref, r_ref, dx_ref, dw_ref, acc):
        i, kk = pl.program_id(0), pl.program_id(1)
        part = _dg(a_ref[...].astype(MXU_DTYPE), b_ref[...].astype(MXU_DTYPE), 1, 1)

        @pl.when(kk == 0)
        def _():
            acc[...] = part

        @pl.when(kk > 0)
        def _():
            acc[...] += part

        @pl.when((i == 0) & (kk == 0))
        def _():
            dw_ref[...] = jnp.zeros_like(dw_ref)

        @pl.when(kk == nk - 1)
        def _():
            g = acc[...]
            xv = x_ref[...]
            r = lax.rsqrt(jnp.mean(xv * xv, axis=-1, keepdims=True) + EPS)
            xhat = xv * r
            dw_ref[...] += jnp.sum(g * xhat, axis=0, keepdims=True)
            gx = g * w_ref[...]
            dx_ref[...] = r_ref[...] + r * (gx - xhat * jnp.mean(gx * xhat, axis=-1, keepdims=True))

    row_spec = pl.BlockSpec((tm, n), lambda i, kk: (i, 0))
    return pl.pallas_call(
        body, name=name, grid=(m // tm, nk),
        in_specs=[pl.BlockSpec((tm, tk), lambda i, kk: (i, kk)), pl.BlockSpec((n, tk), lambda i, kk: (0, kk)),
                  row_spec, pl.BlockSpec((1, n), lambda i, kk: (0, 0)), row_spec],
        out_specs=[row_spec, pl.BlockSpec((1, n), lambda i, kk: (0, 0))],
        out_shape=[jax.ShapeDtypeStruct((m, n), F32), jax.ShapeDtypeStruct((1, n), F32)],
        scratch_shapes=[pltpu.VMEM((tm, n), F32)],
        compiler_params=_cparams(("arbitrary", "arbitrary")),
    )(a, b, x, w, dres)


NORM_ROWS = 512


def _rmsnorm_fwd(x, w, name):
    t, d = x.shape

    def body(x_ref, w_ref, o_ref):
        xv = x_ref[...]
        r = lax.rsqrt(jnp.mean(xv * xv, axis=-1, keepdims=True) + EPS)
        o_ref[...] = (xv * r * w_ref[...]).astype(o_ref.dtype)

    return pl.pallas_call(
        body, name=name, grid=(t // NORM_ROWS,),
        in_specs=[pl.BlockSpec((NORM_ROWS, d), lambda i: (i, 0)), _full_spec((1, d))],
        out_specs=pl.BlockSpec((NORM_ROWS, d), lambda i: (i, 0)),
        out_shape=jax.ShapeDtypeStruct((t, d), BF16),
        compiler_params=_cparams(("parallel",)),
    )(x, w)


SWIGLU_ROWS = 128


def _ffn_up_swiglu(h, w_gate_up):
    m, k = h.shape
    tm = _largest_tile(m, 512)
    tn = _largest_tile(D_FF, MM_TILE_CAP)
    nj = D_FF // tn

    def body(a_ref, bg_ref, bu_ref, g_ref, u_ref, act_ref):
        a = a_ref[...].astype(MXU_DTYPE)
        gate = _dg(a, bg_ref[...].astype(MXU_DTYPE), 1, 0)
        up = _dg(a, bu_ref[...].astype(MXU_DTYPE), 1, 0)
        g_ref[...] = gate
        u_ref[...] = up
        act_ref[...] = (_silu(gate) * up).astype(act_ref.dtype)

    o_spec = pl.BlockSpec((tm, tn), lambda i, j: (i, j))
    return pl.pallas_call(
        body, name="ffn_up", grid=(m // tm, nj),
        in_specs=[pl.BlockSpec((tm, k), lambda i, j: (i, 0)), pl.BlockSpec((k, tn), lambda i, j: (0, j)),
                  pl.BlockSpec((k, tn), lambda i, j: (0, j + nj))],
        out_specs=[o_spec, o_spec, o_spec],
        out_shape=[jax.ShapeDtypeStruct((m, D_FF), F32), jax.ShapeDtypeStruct((m, D_FF), F32),
                   jax.ShapeDtypeStruct((m, D_FF), BF16)],
        compiler_params=_cparams(("parallel", "parallel")),
    )(h, w_gate_up, w_gate_up)


def _swiglu_bwd(gate_all, up_all, dact):
    t = dact.shape[0]

    def body(g_ref, u_ref, da_ref, o_ref):
        gate = g_ref[...]
        up = u_ref[...]
        da = da_ref[...]
        sg = jax.nn.sigmoid(gate)
        o_ref[:, :D_FF] = (da * up * (sg * (1.0 + gate * (1.0 - sg)))).astype(o_ref.dtype)
        o_ref[:, D_FF:] = (da * gate * sg).astype(o_ref.dtype)

    half = pl.BlockSpec((SWIGLU_ROWS, D_FF), lambda i: (i, 0))
    return pl.pallas_call(
        body, name="swiglu_bwd", grid=(t // SWIGLU_ROWS,),
        in_specs=[half, half, half],
        out_specs=pl.BlockSpec((SWIGLU_ROWS, 2 * D_FF), lambda i: (i, 0)),
        out_shape=jax.ShapeDtypeStruct((t, 2 * D_FF), BF16),
        compiler_params=_cparams(("parallel",)),
    )(gate_all, up_all, dact)


def _loss_head(x, w, target):
    t, d = x.shape

    def fwd(xv, wv, tv):
        r = lax.rsqrt(jnp.mean(xv * xv, axis=-1, keepdims=True) + EPS)
        err = xv * r * wv - tv
        return 0.5 * jnp.sum(jnp.mean(err * err, axis=-1, keepdims=True), axis=0, keepdims=True)

    def body(x_ref, w_ref, t_ref, dx_ref, dw_ref, loss_ref):
        @pl.when(pl.program_id(0) == 0)
        def _():
            dw_ref[...] = jnp.zeros_like(dw_ref)
            loss_ref[...] = jnp.zeros_like(loss_ref)

        loss, vjp = jax.vjp(fwd, x_ref[...], w_ref[...], t_ref[...])
        dx, dw, _ = vjp(jnp.ones((1, 1), F32))
        dx_ref[...] = dx
        dw_ref[...] += dw
        loss_ref[...] += jnp.broadcast_to(loss, loss_ref.shape)

    return pl.pallas_call(
        body, name="loss_head", grid=(t // NORM_ROWS,),
        in_specs=[pl.BlockSpec((NORM_ROWS, d), lambda i: (i, 0)), _full_spec((1, d)),
                  pl.BlockSpec((NORM_ROWS, d), lambda i: (i, 0))],
        out_specs=[pl.BlockSpec((NORM_ROWS, d), lambda i: (i, 0)), _full_spec((1, d)), _full_spec((8, 128))],
        out_shape=[jax.ShapeDtypeStruct((t, d), F32), jax.ShapeDtypeStruct((1, d), F32),
                   jax.ShapeDtypeStruct((8, 128), F32)],
        compiler_params=_cparams(("arbitrary",)),
    )(x, w, target)


def _pad_w_in(w):
    z = lambda n: jnp.zeros((w.shape[0], n), w.dtype)
    return jnp.concatenate([w[:, 0:2048], w[:, 2056:2312], w[:, 2312:3080], w[:, 3096:3352],
                            w[:, 2048:2056], z(120), w[:, 3080:3096], z(112)], axis=1)


def _unpad_w_in(wp):
    return jnp.concatenate([wp[:, 0:2048], wp[:, 3328:3336], wp[:, 2048:2304], wp[:, 2304:3072],
                            wp[:, 3456:3472], wp[:, 3072:3328]], axis=1)


def _pad_rows(a, rows):
    return jnp.concatenate([a, jnp.zeros((rows - a.shape[0],) + a.shape[1:], a.dtype)], axis=0)


def _pad_lanes(a, lanes):
    return jnp.concatenate([a, jnp.zeros(a.shape[:-1] + (lanes - a.shape[-1],), a.dtype)], axis=-1)


def _layer_params(l, small):
    dn_cw = small["dn_conv_w"][l]
    return dict(
        ln_w=small["sgu_ln_w"][l][None], ln_b=small["sgu_ln_b"][l][None],
        ws=small["sgu_w_spatial"][l], bs_t=_pad_lanes(small["sgu_b_spatial"][l].T, 128),
        sc_cw=_pad_rows(small["sc_conv_w"][l], HALO),
        dn_cw=jnp.stack([_pad_rows(dn_cw[:, j * GROUP:(j + 1) * GROUP], HALO) for j in range(3)]),
        dn_al=_pad_lanes(small["dn_a_log"][l][None], 128), dn_dt=_pad_lanes(small["dn_dt_bias"][l][None], 128),
        dn_nw=jnp.tile(small["dn_norm_w"][l][None], (1, HEADS)),
        gla_w2=_pad_rows(small["gla_w_gate2"][l], 128), gla_gb=small["gla_gate_bias"][l][None],
        gla_nw=jnp.tile(small["gla_norm_w"][l][None], (1, HEADS)),
    )


def _local_step(x, target, big, small):
    saved = []
    h = x
    for l in range(DEPTH):
        lp = _layer_params(l, small)
        h1 = _rmsnorm_fwd(h, small["norm1_w"][l][None], "norm1_fwd")
        p = _matmul(h1, big["w_in"][l], "nn", F32, "proj_in")
        y_a = _sgu_fwd(p, lp["ln_w"], lp["ln_b"], lp["ws"], lp["bs_t"])
        y_b = _sc_fwd(p, lp["sc_cw"])
        y_c, st_c, inv_c = _dn_fwd(p, lp["dn_cw"], lp["dn_al"], lp["dn_dt"], lp["dn_nw"])
        y_d, st_d = _gla_fwd(p, lp["gla_w2"], lp["gla_gb"], lp["gla_nw"])
        mix = jnp.concatenate([y_a, y_b, y_c, y_d], axis=1)
        x1 = _matmul(mix, big["w_out"][l], "nn", F32, "proj_out", res=h)
        h2 = _rmsnorm_fwd(x1, small["norm2_w"][l][None], "norm2_fwd")
        gate, up, act = _ffn_up_swiglu(h2, big["w_gate_up"][l])
        x2 = _matmul(act, big["w_down"][l], "nn", F32, "ffn_down", res=x1)
        saved.append(dict(x0=h, h1=h1, p=p, st_c=st_c, inv_c=inv_c, st_d=st_d, mix=mix, x1=x1, h2=h2, gate=gate, up=up, act=act, lp=lp))
        h = x2

    dx, d_final, loss = _loss_head(h, small["final_norm_w"][None], target)
    gbig = {k: [None] * DEPTH for k in ("w_in", "w_out", "w_gate_up", "w_down")}
    gs = {k: [None] * DEPTH for k in ("norm1_w", "sgu_ln_w", "sgu_ln_b", "sgu_w_spatial", "sgu_b_spatial", "sc_conv_w",
                                     "dn_conv_w", "dn_a_log", "dn_dt_bias", "dn_norm_w", "gla_w_gate2",
                                     "gla_gate_bias", "gla_norm_w", "norm2_w")}
    for l in reversed(range(DEPTH)):
        s = saved[l]
        lp = s["lp"]
        gbig["w_down"][l] = _matmul(s["act"], dx, "tn", GRAD_WIRE_DTYPE, "ffn_down_dw")
        dact = _matmul(dx, big["w_down"][l], "nt", F32, "ffn_down_dx")
        dgu = _swiglu_bwd(s["gate"], s["up"], dact)
        gbig["w_gate_up"][l] = _matmul(s["h2"], dgu, "tn", GRAD_WIRE_DTYPE, "ffn_up_dw")
        dx1, gs["norm2_w"][l] = _matmul_nt_norm_bwd(dgu, big["w_gate_up"][l], s["x1"], small["norm2_w"][l][None], dx,
                                                    "ffn_up_dx")
        gbig["w_out"][l] = _matmul(s["mix"], dx1, "tn", GRAD_WIRE_DTYPE, "proj_out_dw")
        dmix = _matmul(dx1, big["w_out"][l], "nt", F32, "proj_out_dx")
        p = s["p"]
        dpu, dpv, g_lw, g_lb, g_ws, g_bs = _sgu_bwd(p, dmix, lp["ln_w"], lp["ln_b"], lp["ws"], lp["bs_t"])
        dpb, dpc, dph, g_sc = _sc_bwd(p, dmix, lp["sc_cw"])
        dcq, dck, dcv, dcs, dcz, g_dcw, g_al, g_dt, g_dnw = _dn_bwd(
            p, dmix, s["st_c"], s["inv_c"], lp["dn_cw"], lp["dn_al"], lp["dn_dt"], lp["dn_nw"])
        ddq, ddk, ddv, dds, ddz, g_w2, g_gb, g_gnw = _gla_bwd(p, dmix, s["st_d"], lp["gla_w2"], lp["gla_gb"],
                                                             lp["gla_nw"])
        dp = jnp.concatenate([dpu, dpv, dpb, dpc, dph, dcq, dck, dcv, dcz, ddq, ddk, ddv, ddz, dcs, dds], axis=1)
        gbig["w_in"][l] = _matmul(s["h1"], dp, "tn", GRAD_WIRE_DTYPE, "proj_in_dw")
        dx, gs["norm1_w"][l] = _matmul_nt_norm_bwd(dp, big["w_in"][l], s["x0"], small["norm1_w"][l][None], dx1,
                                                   "proj_in_dx")
        gs["sgu_ln_w"][l], gs["sgu_ln_b"][l] = g_lw[0], g_lb[0]
        gs["sgu_w_spatial"][l] = g_ws
        gs["sgu_b_spatial"][l] = g_bs[:, :HEADS].T
        gs["sc_conv_w"][l] = g_sc[:3]
        gs["dn_conv_w"][l] = jnp.concatenate([g_dcw[0, :4], g_dcw[1, :4], g_dcw[2, :4]], axis=1)
        gs["dn_a_log"][l], gs["dn_dt_bias"][l] = g_al[0, :HEADS], g_dt[0, :HEADS]
        gs["dn_norm_w"][l] = jnp.sum(g_dnw.reshape(HEADS, HEAD_DIM), axis=0)
        gs["gla_w_gate2"][l] = g_w2[:16]
        gs["gla_gate_bias"][l] = g_gb[0]
        gs["gla_norm_w"][l] = jnp.sum(g_gnw.reshape(HEADS, HEAD_DIM), axis=0)
        gs["norm1_w"][l] = gs["norm1_w"][l][0]
        gs["norm2_w"][l] = gs["norm2_w"][l][0]
    gsmall = {k: jnp.stack(v) for k, v in gs.items()}
    gsmall["final_norm_w"] = d_final[0]
    return loss, dx, gbig, gsmall


def _peer_chips(x, y):
    return [(1 - x, y, 2 * (1 - x) + y), (x, 1 - y, 2 * x + 1 - y), (1 - x, 1 - y, 2 * (1 - x) + 1 - y)]


def _chip_exchange(arrays, modes, name):
    na = len(arrays)
    c_idx = lax.axis_index("c")
    chip = 2 * lax.axis_index("x") + lax.axis_index("y")
    bufs = []
    for arr, md in zip(arrays, modes):
        if md == "layer":
            unit = lax.dynamic_index_in_dim(arr, c_idx, 0, keepdims=False)
        elif md == "piece":
            unit = lax.dynamic_index_in_dim(arr, chip, 0, keepdims=False)
        else:
            unit = arr
        buf = lax.empty((2, N_CHIPS) + unit.shape, unit.dtype)
        bufs.append(lax.dynamic_update_slice(buf, unit[None, None], (c_idx, chip) + (0,) * unit.ndim))

    def body(*refs):
        ins, outs = refs[:na], refs[2 * na:3 * na]
        send1, recv1, send2, recv2 = refs[3 * na:]
        x, y, c = lax.axis_index("x"), lax.axis_index("y"), lax.axis_index("c")
        me = 2 * x + y
        sibling = (x, y, 1 - c)

        def src(a, chip):
            if modes[a] == "layer":
                return ins[a].at[c]
            return ins[a].at[chip] if modes[a] == "piece" else ins[a]

        for a in range(na):
            for k, (px, py, pidx) in enumerate(_peer_chips(x, y)):
                pltpu.make_async_remote_copy(
                    src_ref=src(a, pidx), dst_ref=outs[a].at[c, me], send_sem=send1.at[a, k], recv_sem=recv1.at[a, k],
                    device_id=(px, py, c), device_id_type=MESH).start()
        for a in range(na):
            for k, (px, py, pidx) in enumerate(_peer_chips(x, y)):
                cp = pltpu.make_async_remote_copy(
                    src_ref=src(a, pidx), dst_ref=outs[a].at[c, pidx], send_sem=send1.at[a, k],
                    recv_sem=recv1.at[a, k], device_id=(px, py, c), device_id_type=MESH)
                cp.wait_send()
                cp.wait_recv()
        for a in range(na):
            pltpu.make_async_remote_copy(
                src_ref=outs[a].at[c], dst_ref=outs[a].at[c], send_sem=send2.at[a], recv_sem=recv2.at[a],
                device_id=sibling, device_id_type=MESH).start()
        for a in range(na):
            cp = pltpu.make_async_remote_copy(
                src_ref=outs[a].at[c], dst_ref=outs[a].at[1 - c], send_sem=send2.at[a], recv_sem=recv2.at[a],
                device_id=sibling, device_id_type=MESH)
            cp.wait_send()
            cp.wait_recv()

    any_spec = pl.BlockSpec(memory_space=pl.ANY)
    units = [s.shape if md == "whole" else s.shape[1:] for s, md in zip(arrays, modes)]
    return pl.pallas_call(
        body, name=name,
        in_specs=[any_spec] * (2 * na), out_specs=[any_spec] * na,
        out_shape=[jax.ShapeDtypeStruct((2, N_CHIPS) + u, s.dtype) for u, s in zip(units, arrays)],
        input_output_aliases={na + a: a for a in range(na)},
        scratch_shapes=[pltpu.SemaphoreType.DMA((na, 3)), pltpu.SemaphoreType.DMA((na, 3)),
                        pltpu.SemaphoreType.DMA((na,)), pltpu.SemaphoreType.DMA((na,))],
    )(*arrays, *bufs)


def _sibling_swap(arrays):
    na = len(arrays)

    def body(*refs):
        ins, theirs = refs[:na], refs[na:2 * na]
        send_sems, recv_sems = refs[2 * na:]
        x, y, c = lax.axis_index("x"), lax.axis_index("y"), lax.axis_index("c")
        sibling = (x, y, 1 - c)
        for a in range(na):
            pltpu.make_async_remote_copy(
                src_ref=ins[a].at[1 - c], dst_ref=theirs[a], send_sem=send_sems.at[a], recv_sem=recv_sems.at[a],
                device_id=sibling, device_id_type=MESH).start()
        for a in range(na):
            cp = pltpu.make_async_remote_copy(
                src_ref=ins[a].at[1 - c], dst_ref=theirs[a], send_sem=send_sems.at[a], recv_sem=recv_sems.at[a],
                device_id=sibling, device_id_type=MESH)
            cp.wait_send()
            cp.wait_recv()

    any_spec = pl.BlockSpec(memory_space=pl.ANY)
    return pl.pallas_call(
        body, name="sibling_swap",
        in_specs=[any_spec] * na, out_specs=[any_spec] * na,
        out_shape=[jax.ShapeDtypeStruct(s.shape[1:], s.dtype) for s in arrays],
        scratch_shapes=[pltpu.SemaphoreType.DMA((na,)), pltpu.SemaphoreType.DMA((na,))],
    )(*arrays)


def _pair_add(both, theirs, name):
    n, r, c = theirs.shape
    tr = _pick_tile(r, (256, 176, 128, 64, 8))
    core = lax.axis_index("c").astype(jnp.int32).reshape(1)

    def body(core_ref, a_ref, b_ref, o_ref):
        o_ref[...] = (a_ref[...].astype(F32) + b_ref[...].astype(F32)).astype(o_ref.dtype)

    return pl.pallas_call(
        body, name=name,
        grid_spec=pltpu.PrefetchScalarGridSpec(
            num_scalar_prefetch=1, grid=(n, r // tr),
            in_specs=[pl.BlockSpec((None, 1, tr, c), lambda i, j, core_ref: (core_ref[0], i, j, 0)),
                      pl.BlockSpec((1, tr, c), lambda i, j, core_ref: (i, j, 0))],
            out_specs=pl.BlockSpec((1, tr, c), lambda i, j, core_ref: (i, j, 0))),
        out_shape=jax.ShapeDtypeStruct(theirs.shape, theirs.dtype),
        compiler_params=_cparams(("parallel", "parallel")),
    )(core, both, theirs)


def _adamw_math(g, w, m, v):
    m2 = ADAM_B1 * m + (1.0 - ADAM_B1) * g
    v2 = ADAM_B2 * v + (1.0 - ADAM_B2) * (g * g)
    m_hat = m2 / (1.0 - ADAM_B1 ** ADAM_STEP)
    v_hat = v2 / (1.0 - ADAM_B2 ** ADAM_STEP)
    delta = -ADAM_LR * (m_hat / (jnp.sqrt(v_hat) + ADAM_EPS) + ADAM_WD * w)
    return delta, m2, v2


def _adamw_big(contrib, w, m, v, name):
    l, r, c = w.shape
    tr = _pick_tile(r, (256, 176, 128, 64, 8))
    blk = pl.BlockSpec((1, tr, c), lambda i, j: (i, j, 0))

    def body(g_ref, w_ref, m_ref, v_ref, go_ref, d_ref, mo_ref, vo_ref):
        g = g_ref[0, 0].astype(F32)
        for s in range(1, N_CHIPS):
            g = g + g_ref[0, s].astype(F32)
        delta, m2, v2 = _adamw_math(g, w_ref[0], m_ref[0], v_ref[0])
        go_ref[0] = g
        d_ref[0] = delta
        mo_ref[0] = m2
        vo_ref[0] = v2

    return pl.pallas_call(
        body, name=name, grid=(l, r // tr),
        in_specs=[pl.BlockSpec((1, N_CHIPS, tr, c), lambda i, j: (i, 0, j, 0)), blk, blk, blk],
        out_specs=[blk] * 4, out_shape=[jax.ShapeDtypeStruct(w.shape, F32)] * 4,
        compiler_params=_cparams(("parallel", "parallel")),
    )(contrib, w, m, v)


def _sum_small(contrib):
    rows = contrib.shape[2]

    def body(g_ref, o_ref):
        total = g_ref[0, 0]
        for j in range(1, N_DEV):
            total = total + g_ref[j // N_CHIPS, j % N_CHIPS]
        o_ref[...] = total

    return pl.pallas_call(
        body, name="sum_small", out_shape=jax.ShapeDtypeStruct((rows, 128), F32),
        compiler_params=_cparams(),
    )(contrib)


def _adamw_small(gs, ws, ms, vs):
    n = len(gs)
    as2d = lambda a: a.reshape(1, -1) if a.ndim == 1 else a

    def body(*refs):
        g_refs, w_refs, m_refs, v_refs = refs[:n], refs[n:2 * n], refs[2 * n:3 * n], refs[3 * n:4 * n]
        d_refs, mo_refs, vo_refs = refs[4 * n:5 * n], refs[5 * n:6 * n], refs[6 * n:]
        for j in range(n):
            delta, m2, v2 = _adamw_math(g_refs[j][...], w_refs[j][...], m_refs[j][...], v_refs[j][...])
            d_refs[j][...] = delta
            mo_refs[j][...] = m2
            vo_refs[j][...] = v2

    ins = [as2d(a) for a in (*gs, *ws, *ms, *vs)]
    outs = pl.pallas_call(
        body, name="adamw_small", out_shape=[jax.ShapeDtypeStruct(a.shape, F32) for a in ins[:n]] * 3,
        compiler_params=_cparams(),
    )(*ins)
    back = lambda group: [o.reshape(g.shape) for o, g in zip(group, gs)]
    return back(outs[:n]), back(outs[n:2 * n]), back(outs[2 * n:])


PACK_ALIGN = 8 * 128


def _packed_rows(shape):
    n = 1
    for d in shape:
        n *= d
    return (n + PACK_ALIGN - 1) // PACK_ALIGN * 8


def _pack(arrays):
    parts = []
    for a in arrays:
        flat = a.reshape(-1)
        pad = _packed_rows(a.shape) * 128 - flat.shape[0]
        if pad:
            flat = jnp.concatenate([flat, jnp.zeros((pad,), F32)])
        parts.append(flat.reshape(-1, 128))
    return jnp.concatenate(parts, axis=0)


def _unpack(packed, shapes):
    out, row = [], 0
    for s in shapes:
        rows = _packed_rows(s)
        n = 1
        for d in s:
            n *= d
        out.append(packed[row:row + rows].reshape(-1)[:n].reshape(s))
        row += rows
    return out


SMALL_NAMES = ("norm1_w", "sgu_ln_w", "sgu_ln_b", "sgu_w_spatial", "sgu_b_spatial", "sc_conv_w", "dn_conv_w",
               "dn_a_log", "dn_dt_bias", "dn_norm_w", "gla_w_gate2", "gla_gate_bias", "gla_norm_w", "norm2_w",
               "final_norm_w")
SHARDED_SMALL = ("sc_conv_w", "dn_conv_w", "gla_w_gate2")
BIG_NAMES = ("w_in", "w_out", "w_gate_up", "w_down")
WEIGHT_ORDER = ("norm1_w", "w_in", "sgu_ln_w", "sgu_ln_b", "sgu_w_spatial", "sgu_b_spatial", "sc_conv_w", "dn_conv_w",
                "dn_a_log", "dn_dt_bias", "dn_norm_w", "gla_w_gate2", "gla_gate_bias", "gla_norm_w", "w_out",
                "norm2_w", "w_gate_up", "w_down", "final_norm_w")


def _cols_from_shards(g):
    l, n, r, c = g.shape
    return jnp.transpose(g, (0, 2, 1, 3)).reshape(l, r, n * c)


def _rows_from_shards(g):
    l, n, r, c = g.shape
    return g.reshape(l, n * r, c)


def _cols_to_shards(full):
    l, r, c4 = full.shape
    return jnp.transpose(full.reshape(l, r, N_CHIPS, c4 // N_CHIPS), (0, 2, 1, 3))


def _rows_to_shards(full):
    l, r4, c = full.shape
    return full.reshape(l, N_CHIPS, r4 // N_CHIPS, c)


def kernel(x, norm1_w, w_in, sgu_ln_w, sgu_ln_b, sgu_w_spatial, sgu_b_spatial, sc_conv_w, dn_conv_w, dn_a_log, dn_dt_bias, dn_norm_w, gla_w_gate2, gla_gate_bias, gla_norm_w, w_out, norm2_w, w_gate_up, w_down, final_norm_w, loss_target, m_norm1_w, m_w_in, m_sgu_ln_w, m_sgu_ln_b, m_sgu_w_spatial, m_sgu_b_spatial, m_sc_conv_w, m_dn_conv_w, m_dn_a_log, m_dn_dt_bias, m_dn_norm_w, m_gla_w_gate2, m_gla_gate_bias, m_gla_norm_w, m_w_out, m_norm2_w, m_w_gate_up, m_w_down, m_final_norm_w, v_norm1_w, v_w_in, v_sgu_ln_w, v_sgu_ln_b, v_sgu_w_spatial, v_sgu_b_spatial, v_sc_conv_w, v_dn_conv_w, v_dn_a_log, v_dn_dt_bias, v_dn_norm_w, v_gla_w_gate2, v_gla_gate_bias, v_gla_norm_w, v_w_out, v_norm2_w, v_w_gate_up, v_w_down, v_final_norm_w):
    w = dict(norm1_w=norm1_w, w_in=w_in, sgu_ln_w=sgu_ln_w, sgu_ln_b=sgu_ln_b, sgu_w_spatial=sgu_w_spatial,
             sgu_b_spatial=sgu_b_spatial, sc_conv_w=sc_conv_w, dn_conv_w=dn_conv_w, dn_a_log=dn_a_log,
             dn_dt_bias=dn_dt_bias, dn_norm_w=dn_norm_w, gla_w_gate2=gla_w_gate2, gla_gate_bias=gla_gate_bias,
             gla_norm_w=gla_norm_w, w_out=w_out, norm2_w=norm2_w, w_gate_up=w_gate_up, w_down=w_down,
             final_norm_w=final_norm_w)
    m = dict(norm1_w=m_norm1_w, w_in=m_w_in, sgu_ln_w=m_sgu_ln_w, sgu_ln_b=m_sgu_ln_b, sgu_w_spatial=m_sgu_w_spatial,
             sgu_b_spatial=m_sgu_b_spatial, sc_conv_w=m_sc_conv_w, dn_conv_w=m_dn_conv_w, dn_a_log=m_dn_a_log,
             dn_dt_bias=m_dn_dt_bias, dn_norm_w=m_dn_norm_w, gla_w_gate2=m_gla_w_gate2,
             gla_gate_bias=m_gla_gate_bias, gla_norm_w=m_gla_norm_w, w_out=m_w_out, norm2_w=m_norm2_w,
             w_gate_up=m_w_gate_up, w_down=m_w_down, final_norm_w=m_final_norm_w)
    v = dict(norm1_w=v_norm1_w, w_in=v_w_in, sgu_ln_w=v_sgu_ln_w, sgu_ln_b=v_sgu_ln_b, sgu_w_spatial=v_sgu_w_spatial,
             sgu_b_spatial=v_sgu_b_spatial, sc_conv_w=v_sc_conv_w, dn_conv_w=v_dn_conv_w, dn_a_log=v_dn_a_log,
             dn_dt_bias=v_dn_dt_bias, dn_norm_w=v_dn_norm_w, gla_w_gate2=v_gla_w_gate2,
             gla_gate_bias=v_gla_gate_bias, gla_norm_w=v_gla_norm_w, w_out=v_w_out, norm2_w=v_norm2_w,
             w_gate_up=v_w_gate_up, w_down=v_w_down, final_norm_w=v_final_norm_w)
    chip = 2 * lax.axis_index("x") + lax.axis_index("y")

    shards = [w[n].astype(MXU_DTYPE) for n in BIG_NAMES] + [w[n] for n in SHARDED_SMALL]
    gathered = _chip_exchange(shards, ["layer"] * len(shards), "gather_weights")
    full_in = _cols_from_shards(gathered[0])
    big = dict(
        w_in=[_pad_w_in(full_in[l]) for l in range(DEPTH)],
        w_out=_rows_from_shards(gathered[1]),
        w_gate_up=_cols_from_shards(gathered[2]),
        w_down=_rows_from_shards(gathered[3]),
    )
    small = {n: w[n] for n in SMALL_NAMES if n not in SHARDED_SMALL}
    for j, n in enumerate(SHARDED_SMALL):
        small[n] = _cols_from_shards(gathered[4 + j])

    loss_tile, grad_x, gbig, gsmall = _local_step(x[0], loss_target[0], big, small)

    pieces = [
        _cols_to_shards(jnp.stack([_unpad_w_in(g) for g in gbig["w_in"]])),
        _rows_to_shards(jnp.stack(gbig["w_out"])),
        _cols_to_shards(jnp.stack(gbig["w_gate_up"])),
        _rows_to_shards(jnp.stack(gbig["w_down"])),
    ]
    small_list = [gsmall[n] for n in SMALL_NAMES] + [loss_tile[0:1, 0]]
    small_shapes = [a.shape for a in small_list]
    theirs = _sibling_swap(pieces)
    parts = [_pair_add(a, b, "pair_add_" + n) for a, b, n in zip(pieces, theirs, BIG_NAMES)]
    contrib = _chip_exchange(parts + [_pack(small_list)], ["piece"] * len(parts) + ["whole"], "exchange_grads")

    out_g, out_d, out_m, out_v = {}, {}, {}, {}
    for j, n in enumerate(BIG_NAMES):
        out_g[n], out_d[n], out_m[n], out_v[n] = _adamw_big(contrib[j], w[n], m[n], v[n], "adamw_" + n)
    summed = _unpack(_sum_small(contrib[len(BIG_NAMES)]), small_shapes)
    loss = summed[-1][0]
    for n, g in zip(SMALL_NAMES, summed[:-1]):
        if n in SHARDED_SMALL:
            cols = g.shape[-1] // N_CHIPS
            g = lax.dynamic_slice_in_dim(g, chip * cols, cols, axis=g.ndim - 1)
        out_g[n] = g
    d_s, m_s, v_s = _adamw_small([out_g[n] for n in SMALL_NAMES], [w[n] for n in SMALL_NAMES],
                                 [m[n] for n in SMALL_NAMES], [v[n] for n in SMALL_NAMES])
    for n, d_, m_, v_ in zip(SMALL_NAMES, d_s, m_s, v_s):
        out_d[n], out_m[n], out_v[n] = d_, m_, v_

    return (loss, grad_x[None], *[out_g[n] for n in WEIGHT_ORDER], *[out_d[n] for n in WEIGHT_ORDER],
            *[out_m[n] for n in WEIGHT_ORDER], *[out_v[n] for n in WEIGHT_ORDER])
```

```python
import functools

import jax
import jax.numpy as jnp
from jax import lax
from jax.experimental import pallas as pl
from jax.experimental.pallas import tpu as pltpu

F32 = jnp.float32
BF16 = jnp.bfloat16
MXU_DTYPE = jnp.bfloat16
GRAD_WIRE_DTYPE = jnp.bfloat16
HI = lax.Precision.HIGHEST
MESH = pl.DeviceIdType.MESH

D_MODEL = 1024
DEPTH = 2
GROUP = 256
HEADS = 4
HEAD_DIM = 64
SGU_CHUNK = 128
SCAN_CHUNK = 64
D_FF = 2816
EPS = 1e-6
IN_COLS = 3352
P_COLS = 3584
HALO = 8
N_CHIPS = 4
N_DEV = 8
VMEM_LIMIT = 56 * 1024 * 1024

ADAM_LR = 0.001
ADAM_B1 = 0.9
ADAM_B2 = 0.999
ADAM_EPS = 1e-08
ADAM_WD = 0.01
ADAM_STEP = 10

(COL_AU, COL_AV, COL_BB, COL_BC, COL_BH, COL_CQ, COL_CK, COL_CV, COL_CZ,
 COL_DQ, COL_DK, COL_DV, COL_DZ) = range(13)
COL128_SMALL_C = 26
COL128_SMALL_D = 27


def _cparams(sem=None):
    return pltpu.CompilerParams(dimension_semantics=sem, vmem_limit_bytes=VMEM_LIMIT)


def _iota(shape, dim):
    return lax.broadcasted_iota(jnp.int32, shape, dim)


def _dg(a, b, ca, cb, prec=None):
    return lax.dot_general(a, b, (((ca,), (cb,)), ((), ())), preferred_element_type=F32, precision=prec)


@functools.partial(jax.custom_vjp, nondiff_argnums=(2, 3))
def bdot(a, b, ca, cb):
    return _dg(a.astype(MXU_DTYPE), b.astype(MXU_DTYPE), ca, cb)


def _bdot_fwd(a, b, ca, cb):
    return bdot(a, b, ca, cb), (a, b)


def _bdot_bwd(ca, cb, res, g):
    a, b = res
    if ca == 1:
        da = bdot(g, b, 1, 1 if cb == 0 else 0)
    else:
        da = bdot(b, g, 1 if cb == 0 else 0, 1)
    if cb == 0:
        db = bdot(a, g, 0, 0) if ca == 1 else bdot(a, g, 1, 0)
    else:
        db = bdot(g, a, 0, 0) if ca == 1 else bdot(g, a, 0, 1)
    return da, db


bdot.defvjp(_bdot_fwd, _bdot_bwd)


def _pieces(a, n):
    out, r = [], a
    for i in range(n):
        p = r.astype(MXU_DTYPE)
        out.append(p)
        if i + 1 < n:
            r = r - p.astype(F32)
    return out


def _mdot_impl(a, b, ca, cb, sa, sb):
    total = None
    for i, x in enumerate(_pieces(a, sa)):
        for j, y in enumerate(_pieces(b, sb)):
            if i + j < max(sa, sb):
                t = _dg(x, y, ca, cb)
                total = t if total is None else total + t
    return total


@functools.partial(jax.custom_vjp, nondiff_argnums=(2, 3, 4, 5))
def mdot(a, b, ca, cb, sa, sb):
    return _mdot_impl(a, b, ca, cb, sa, sb)


def _mdot_fwd(a, b, ca, cb, sa, sb):
    return _mdot_impl(a, b, ca, cb, sa, sb), (a, b)


def _mdot_bwd(ca, cb, sa, sb, res, g):
    a, b = res
    ga, gb = (3 if sb == 1 else 2), (3 if sa == 1 else 2)
    if sa == 1:
        da = jnp.zeros_like(a)
    elif ca == 1:
        da = mdot(g, b, 1, 1 if cb == 0 else 0, ga, sb)
    else:
        da = mdot(b, g, 1 if cb == 0 else 0, 1, sb, ga)
    if sb == 1:
        db = jnp.zeros_like(b)
    elif cb == 0:
        db = mdot(a, g, 0, 0, sa, gb) if ca == 1 else mdot(a, g, 1, 0, sa, gb)
    else:
        db = mdot(g, a, 0, 0, gb, sa) if ca == 1 else mdot(g, a, 0, 1, gb, sa)
    return da, db


mdot.defvjp(_mdot_fwd, _mdot_bwd)


def mask_r(a, m, ca=1, cb=0):
    return mdot(a, m, ca, cb, 3, 1)


def mask_l(m, b, ca=1, cb=0):
    return mdot(m, b, ca, cb, 1, 3)


def ddot(a, b, ca=1, cb=0):
    return mdot(a, b, ca, cb, 2, 2)


def _head_mask(h):
    return ((_iota((1, GROUP), 1) >> 6) == h).astype(F32)


def _block_diag_mask():
    return ((_iota((GROUP, GROUP), 0) >> 6) == (_iota((GROUP, GROUP), 1) >> 6)).astype(F32)


def _expand_mat(offset):
    return ((_iota((128, GROUP), 0) - offset) == (_iota((128, GROUP), 1) >> 6)).astype(F32)


def _tril(n, strict=False):
    r, c = _iota((n, n), 0), _iota((n, n), 1)
    return (r > c) if strict else (r >= c)


def _row_pick(x, row):
    return jnp.sum(jnp.where(_iota(x.shape, 0) == row, x, 0.0), axis=0, keepdims=True)


def _shift_rows_impl(x, halo, j):
    n = x.shape[0]
    r = _iota(x.shape, 0)
    top = jnp.concatenate([pltpu.roll(halo, j, 0), jnp.zeros((n - HALO, x.shape[1]), x.dtype)], axis=0)
    return jnp.where(r >= j, pltpu.roll(x, j, 0), top)


def _mxu_round(a):
    return a.astype(MXU_DTYPE).astype(F32)


@functools.partial(jax.custom_vjp, nondiff_argnums=(3,))
def _causal_conv(x, halo, w, width):
    xb, hb, wb = _mxu_round(x), _mxu_round(halo), _mxu_round(w)
    out = xb * _row_pick(wb, width - 1)
    for j in range(1, width):
        out = out + _shift_rows_impl(xb, hb, j) * _row_pick(wb, width - 1 - j)
    return out


def _causal_conv_fwd(x, halo, w, width):
    return _causal_conv(x, halo, w, width), (x, halo, w)


def _causal_conv_bwd(width, res, g):
    x, halo, w = res
    xb, hb, wb, gb = _mxu_round(x), _mxu_round(halo), _mxu_round(w), _mxu_round(g)
    n = g.shape[0]
    rows, rows8 = _iota(g.shape, 0), _iota(halo.shape, 0)
    dx = gb * _row_pick(wb, width - 1)
    dh = jnp.zeros_like(halo)
    dw = jnp.where(rows8 == width - 1, jnp.sum(xb * gb, axis=0, keepdims=True), 0.0)
    for j in range(1, width):
        gj = gb * _row_pick(wb, width - 1 - j)
        dx = dx + jnp.where(rows < n - j, pltpu.roll(gj, n - j, 0), 0.0)
        dh = dh + jnp.where(rows8 >= HALO - j, pltpu.roll(gj[0:HALO], HALO - j, 0), 0.0)
        tap = jnp.sum(_shift_rows_impl(xb, hb, j) * gb, axis=0, keepdims=True)
        dw = dw + jnp.where(rows8 == width - 1 - j, tap, 0.0)
    return dx, dh, dw


_causal_conv.defvjp(_causal_conv_fwd, _causal_conv_bwd)


def _head_sum(x, bd):
    return mask_r(x, bd)


def _softplus(x):
    return jnp.maximum(x, 0.0) + jnp.log1p(jnp.exp(-jnp.abs(x)))


def _log_sigmoid(x):
    return -_softplus(-x)


def _silu(x):
    return x * jax.nn.sigmoid(x)


def _head_rmsnorm_gate(o, nw, z, bd):
    ms = _head_sum(o * o, bd) * (1.0 / HEAD_DIM)
    return o * lax.rsqrt(ms + EPS) * nw * _silu(z)


def _sgu_chunk(pu, pv, ln_w, ln_b, ws0, ws1, ws2, ws3, bs_t):
    u = jax.nn.gelu(pu)
    g = jax.nn.gelu(pv)
    mu = jnp.mean(g, axis=-1, keepdims=True)
    var = jnp.mean(jnp.square(g - mu), axis=-1, keepdims=True)
    v = (g - mu) * lax.rsqrt(var + EPS) * ln_w + ln_b
    keep = _tril(SGU_CHUNK)
    mixed = mask_r(bs_t, _expand_mat(0))
    for h, ws in enumerate((ws0, ws1, ws2, ws3)):
        mixed = mixed + _head_mask(h) * bdot(jnp.where(keep, ws, 0.0), v, 1, 0)
    return u * mixed


def _sc_chunk(pb, pc, ph, halo_c, halo_h, cw):
    return pb * _causal_conv(pc * ph, halo_c * halo_h, cw, 3)


def _neumann_inverses(lows):
    n = lows[0].shape[0]
    eye = (_iota((n, n), 0) == _iota((n, n), 1)).astype(F32)
    a = [-low for low in lows]
    t = [eye + x for x in a]
    for _ in range(5):
        a = [ddot(x, x) for x in a]
        t = [ti + ddot(ti, ai) for ti, ai in zip(t, a)]
    return t


@jax.custom_vjp
def _saved_inverse(low, inv):
    return inv


def _saved_inverse_fwd(low, inv):
    return inv, inv


def _saved_inverse_bwd(inv, g):
    return -ddot(ddot(inv, g, 0, 0), inv, 1, 1), jnp.zeros_like(inv)


_saved_inverse.defvjp(_saved_inverse_fwd, _saved_inverse_bwd)


def _chunk_tril(rows):
    r, c = _iota((rows, rows), 0), _iota((rows, rows), 1)
    return ((r >> 6) == (c >> 6)) & (r >= c)


def _dn_block(pq, pk, pv, hq, hk, hv, small, pz, cwq, cwk, cwv, a_log, dt_bias, nw, state, saved_inv=None):
    c = SCAN_CHUNK
    rows = pq.shape[0]
    bd = _block_diag_mask()
    q = _silu(_causal_conv(pq, hq, cwq, 4))
    k = _silu(_causal_conv(pk, hk, cwk, 4))
    v = _silu(_causal_conv(pv, hv, cwv, 4))
    q = q * lax.rsqrt(_head_sum(q * q, bd) + EPS) * (HEAD_DIM ** -0.5)
    k = k * lax.rsqrt(_head_sum(k * k, bd) + EPS)
    lane = _iota((1, 128), 1)
    g = jnp.where(lane < HEADS, -jnp.exp(a_log) * _softplus(small + dt_bias), 0.0)
    beta_b = mask_r(jax.nn.sigmoid(small), _expand_mat(HEADS))
    gc_all = mask_l(_chunk_tril(rows).astype(F32), g)
    gcb_all = mask_r(gc_all, _expand_mat(0))
    kb_all = k * beta_b
    vb_all = v * beta_b
    kbe_all = kb_all * jnp.exp(gcb_all)
    qg_all = q * jnp.exp(gcb_all)
    causal, strict = _tril(c), _tril(c, strict=True)
    nc = rows // c
    pairs = [(ci, h) for ci in range(nc) for h in range(HEADS)]
    sls = [slice(ci * c, (ci + 1) * c) for ci in range(nc)]
    decays, lows, attns = [], [], []
    for ci, h in pairs:
        gc = gc_all[sls[ci]]
        onehot = (_iota((c, 128), 1) == h).astype(F32)
        col = mask_l(onehot, gc, 1, 1)
        row = jnp.sum(gc * onehot, axis=1, keepdims=True)
        decays.append(jnp.exp(jnp.where(causal, row - col, -jnp.inf)))
    for j, (ci, h) in enumerate(pairs):
        mh = _head_mask(h)
        k_c = k[sls[ci]]
        lows.append(jnp.where(strict, bdot(kb_all[sls[ci]] * mh, k_c, 1, 1) * decays[j], 0.0))
        attns.append(bdot(q[sls[ci]] * mh, k_c, 1, 1) * decays[j])
    if saved_inv is None:
        invs = _neumann_inverses(lows)
    else:
        invs = [_saved_inverse(low, s) for low, s in zip(lows, saved_inv)]
    us, ws = [], []
    for ci in range(nc):
        u = jnp.zeros((c, GROUP), F32)
        w = jnp.zeros((c, GROUP), F32)
        for h in range(HEADS):
            mh = _head_mask(h)
            u = u + mh * ddot(invs[ci * HEADS + h], vb_all[sls[ci]])
            w = w + mh * ddot(invs[ci * HEADS + h], kbe_all[sls[ci]])
        us.append(u)
        ws.append(w)
    outs = []
    for ci in range(nc):
        gc_b = gcb_all[sls[ci]]
        gc_last_b = _row_pick(gc_b, c - 1)
        v_new = us[ci] - bdot(ws[ci], state, 1, 0)
        o = bdot(qg_all[sls[ci]], state, 1, 0)
        for h in range(HEADS):
            o = o + _head_mask(h) * bdot(attns[ci * HEADS + h], v_new, 1, 0)
        k_dec = k[sls[ci]] * jnp.exp(gc_last_b - gc_b)
        state = state * jnp.exp(gc_last_b) + bd * bdot(k_dec, v_new, 0, 0)
        outs.append(o)
    o = jnp.concatenate(outs, axis=0)
    return _head_rmsnorm_gate(o, nw, pz, bd), state, invs


def _gla_chunk(pq, pk, pv, small, pz, w2, gbias, nw, state_t):
    c = SCAN_CHUNK
    rows = pq.shape[0]
    nc = rows // c
    sls = [slice(ci * c, (ci + 1) * c) for ci in range(nc)]
    bd = _block_diag_mask()
    log_a = _log_sigmoid(bdot(small, w2, 1, 0) + gbias) * (1.0 / 16.0)
    gcum = mask_l(_chunk_tril(rows).astype(F32), log_a)
    r, s = _iota((rows, rows), 0), _iota((rows, rows), 1)
    base = (r >> 6) << 6
    g_mid = mask_l((s == base + c // 2).astype(F32), gcum)
    g_last = mask_l((s == base + c - 1).astype(F32), gcum)
    q = pq * (HEAD_DIM ** -0.5)
    qa = q * jnp.exp(gcum - g_mid)
    ka = pk * jnp.exp(g_mid - gcum)
    qg = q * jnp.exp(gcum)
    k_last = pk * jnp.exp(g_last - gcum)
    causal = _tril(c)
    attns = [jnp.where(causal, bdot(qa[sls[ci]] * _head_mask(h), ka[sls[ci]], 1, 1), 0.0)
             for ci in range(nc) for h in range(HEADS)]
    intra = []
    for ci in range(nc):
        o = jnp.zeros((c, GROUP), F32)
        for h in range(HEADS):
            o = o + _head_mask(h) * bdot(attns[ci * HEADS + h], pv[sls[ci]], 1, 0)
        intra.append(o)
    kvs = [bd * bdot(pv[sls[ci]], k_last[sls[ci]], 0, 0) for ci in range(nc)]
    states = []
    for ci in range(nc):
        states.append(state_t)
        state_t = state_t * jnp.exp(_row_pick(g_last[sls[ci]], 0)) + kvs[ci]
    outs = [intra[ci] + bdot(qg[sls[ci]], states[ci], 1, 1) for ci in range(nc)]
    o = jnp.concatenate(outs, axis=0)
    return _head_rmsnorm_gate(o, nw, pz, bd), state_t


def _col_spec(rows, group, rev_n=None):
    if rev_n is None:
        return pl.BlockSpec((rows, GROUP), lambda i: (i, group))
    return pl.BlockSpec((rows, GROUP), lambda i: (rev_n - 1 - i, group))


def _small_spec(rows, group128, rev_n=None):
    if rev_n is None:
        return pl.BlockSpec((rows, 128), lambda i: (i, group128))
    return pl.BlockSpec((rows, 128), lambda i: (rev_n - 1 - i, group128))


def _halo_spec(rows, group, rev_n=None):
    per = rows // HALO
    if rev_n is None:
        return pl.BlockSpec((HALO, GROUP), lambda i: (jnp.maximum(i * per - 1, 0), group))
    return pl.BlockSpec((HALO, GROUP), lambda i: (jnp.maximum((rev_n - 1 - i) * per - 1, 0), group))


def _full_spec(shape):
    nd = len(shape)
    return pl.BlockSpec(shape, lambda i: (0,) * nd)


def _out_rows_spec(rows, lanes, rev_n=None):
    if rev_n is None:
        return pl.BlockSpec((rows, lanes), lambda i: (i, 0))
    return pl.BlockSpec((rows, lanes), lambda i: (rev_n - 1 - i, 0))


def _sgu_fwd(p, ln_w, ln_b, ws, bs_t):
    t = p.shape[0]
    n = t // SGU_CHUNK

    def body(pu_ref, pv_ref, lw_ref, lb_ref, ws_ref, bs_ref, y_ref):
        y = _sgu_chunk(pu_ref[...], pv_ref[...], lw_ref[...], lb_ref[...],
                       ws_ref[0], ws_ref[1], ws_ref[2], ws_ref[3], bs_ref[...])
        y_ref[...] = y.astype(y_ref.dtype)

    return pl.pallas_call(
        body, name="sgu_fwd", grid=(n,),
        in_specs=[_col_spec(SGU_CHUNK, COL_AU), _col_spec(SGU_CHUNK, COL_AV), _full_spec((1, GROUP)),
                  _full_spec((1, GROUP)), _full_spec((HEADS, SGU_CHUNK, SGU_CHUNK)), _full_spec((SGU_CHUNK, 128))],
        out_specs=_out_rows_spec(SGU_CHUNK, GROUP),
        out_shape=jax.ShapeDtypeStruct((t, GROUP), BF16),
        compiler_params=_cparams(("arbitrary",)),
    )(p, p, ln_w, ln_b, ws, bs_t)


def _sgu_bwd(p, dmix, ln_w, ln_b, ws, bs_t):
    t = p.shape[0]
    n = t // SGU_CHUNK

    def body(pu_ref, pv_ref, dy_ref, lw_ref, lb_ref, ws_ref, bs_ref,
             dpu_ref, dpv_ref, dlw_ref, dlb_ref, dws_ref, dbs_ref):
        args = (pu_ref[...], pv_ref[...], lw_ref[...], lb_ref[...],
                ws_ref[0], ws_ref[1], ws_ref[2], ws_ref[3], bs_ref[...])
        _, vjp = jax.vjp(_sgu_chunk, *args)
        dpu, dpv, dlw, dlb, d0, d1, d2, d3, dbs = vjp(dy_ref[...])
        dpu_ref[...] = dpu.astype(dpu_ref.dtype)
        dpv_ref[...] = dpv.astype(dpv_ref.dtype)

        @pl.when(pl.program_id(0) == 0)
        def _():
            dlw_ref[...] = jnp.zeros_like(dlw_ref)
            dlb_ref[...] = jnp.zeros_like(dlb_ref)
            dws_ref[...] = jnp.zeros_like(dws_ref)
            dbs_ref[...] = jnp.zeros_like(dbs_ref)

        dlw_ref[...] += dlw
        dlb_ref[...] += dlb
        for h, d in enumerate((d0, d1, d2, d3)):
            dws_ref[h] += d
        dbs_ref[...] += dbs

    return pl.pallas_call(
        body, name="sgu_bwd", grid=(n,),
        in_specs=[_col_spec(SGU_CHUNK, COL_AU), _col_spec(SGU_CHUNK, COL_AV),
                  pl.BlockSpec((SGU_CHUNK, GROUP), lambda i: (i, 0)),
                  _full_spec((1, GROUP)), _full_spec((1, GROUP)), _full_spec((HEADS, SGU_CHUNK, SGU_CHUNK)),
                  _full_spec((SGU_CHUNK, 128))],
        out_specs=[_out_rows_spec(SGU_CHUNK, GROUP), _out_rows_spec(SGU_CHUNK, GROUP), _full_spec((1, GROUP)),
                   _full_spec((1, GROUP)), _full_spec((HEADS, SGU_CHUNK, SGU_CHUNK)), _full_spec((SGU_CHUNK, 128))],
        out_shape=[jax.ShapeDtypeStruct((t, GROUP), BF16), jax.ShapeDtypeStruct((t, GROUP), BF16),
                   jax.ShapeDtypeStruct((1, GROUP), F32), jax.ShapeDtypeStruct((1, GROUP), F32),
                   jax.ShapeDtypeStruct((HEADS, SGU_CHUNK, SGU_CHUNK), F32),
                   jax.ShapeDtypeStruct((SGU_CHUNK, 128), F32)],
        compiler_params=_cparams(("arbitrary",)),
    )(p, p, dmix, ln_w, ln_b, ws, bs_t)


SC_ROWS = 256


def _first_block_zero(halo, first):
    return jnp.where(first, 0.0, halo)


def _sc_fwd(p, cw):
    t = p.shape[0]
    n = t // SC_ROWS

    def body(pb_ref, pc_ref, ph_ref, hc_ref, hh_ref, cw_ref, y_ref):
        first = pl.program_id(0) == 0
        y = _sc_chunk(pb_ref[...], pc_ref[...], ph_ref[...], _first_block_zero(hc_ref[...], first),
                      _first_block_zero(hh_ref[...], first), cw_ref[...])
        y_ref[...] = y.astype(y_ref.dtype)

    return pl.pallas_call(
        body, name="sc_fwd", grid=(n,),
        in_specs=[_col_spec(SC_ROWS, COL_BB), _col_spec(SC_ROWS, COL_BC), _col_spec(SC_ROWS, COL_BH),
                  _halo_spec(SC_ROWS, COL_BC), _halo_spec(SC_ROWS, COL_BH), _full_spec((HALO, GROUP))],
        out_specs=_out_rows_spec(SC_ROWS, GROUP),
        out_shape=jax.ShapeDtypeStruct((t, GROUP), BF16),
        compiler_params=_cparams(("arbitrary",)),
    )(p, p, p, p, p, cw)


def _add_halo_grad(d, carry):
    return d + jnp.concatenate([jnp.zeros((d.shape[0] - HALO, d.shape[1]), d.dtype), carry], axis=0)


def _sc_bwd(p, dmix, cw):
    t = p.shape[0]
    n = t // SC_ROWS

    def body(pb_ref, pc_ref, ph_ref, hc_ref, hh_ref, dy_ref, cw_ref,
             dpb_ref, dpc_ref, dph_ref, dcw_ref, carry_c, carry_h):
        i = pl.program_id(0)
        first = i == n - 1

        @pl.when(i == 0)
        def _():
            carry_c[...] = jnp.zeros_like(carry_c)
            carry_h[...] = jnp.zeros_like(carry_h)
            dcw_ref[...] = jnp.zeros_like(dcw_ref)

        args = (pb_ref[...], pc_ref[...], ph_ref[...], _first_block_zero(hc_ref[...], first),
                _first_block_zero(hh_ref[...], first), cw_ref[...])
        _, vjp = jax.vjp(_sc_chunk, *args)
        dpb, dpc, dph, dhc, dhh, dcw = vjp(dy_ref[...])
        dpb_ref[...] = dpb.astype(dpb_ref.dtype)
        dpc_ref[...] = _add_halo_grad(dpc, carry_c[...]).astype(dpc_ref.dtype)
        dph_ref[...] = _add_halo_grad(dph, carry_h[...]).astype(dph_ref.dtype)
        carry_c[...] = dhc
        carry_h[...] = dhh
        dcw_ref[...] += dcw

    return pl.pallas_call(
        body, name="sc_bwd", grid=(n,),
        in_specs=[_col_spec(SC_ROWS, COL_BB, n), _col_spec(SC_ROWS, COL_BC, n), _col_spec(SC_ROWS, COL_BH, n),
                  _halo_spec(SC_ROWS, COL_BC, n), _halo_spec(SC_ROWS, COL_BH, n),
                  pl.BlockSpec((SC_ROWS, GROUP), lambda i: (n - 1 - i, 1)), _full_spec((HALO, GROUP))],
        out_specs=[_out_rows_spec(SC_ROWS, GROUP, n)] * 3 + [_full_spec((HALO, GROUP))],
        out_shape=[jax.ShapeDtypeStruct((t, GROUP), BF16)] * 3 + [jax.ShapeDtypeStruct((HALO, GROUP), F32)],
        scratch_shapes=[pltpu.VMEM((HALO, GROUP), F32), pltpu.VMEM((HALO, GROUP), F32)],
        compiler_params=_cparams(("arbitrary",)),
    )(p, p, p, p, p, dmix, cw)


SCAN_STEP_CHUNKS = 4
SCAN_ROWS = SCAN_STEP_CHUNKS * SCAN_CHUNK


def _dn_fwd(p, cw3, a_log, dt_bias, nw):
    t = p.shape[0]
    r = SCAN_ROWS
    n = t // r

    def body(pq_ref, pk_ref, pv_ref, hq_ref, hk_ref, hv_ref, sm_ref, pz_ref, cw_ref, al_ref, dt_ref, nw_ref,
             y_ref, ck_ref, inv_ref, state):
        first = pl.program_id(0) == 0

        @pl.when(first)
        def _():
            state[...] = jnp.zeros_like(state)

        s_in = state[...]
        ck_ref[0] = s_in
        y, s_out, invs = _dn_block(pq_ref[...], pk_ref[...], pv_ref[...], _first_block_zero(hq_ref[...], first),
                                   _first_block_zero(hk_ref[...], first), _first_block_zero(hv_ref[...], first),
                                   sm_ref[...], pz_ref[...], cw_ref[0], cw_ref[1], cw_ref[2],
                                   al_ref[...], dt_ref[...], nw_ref[...], s_in)
        y_ref[...] = y.astype(y_ref.dtype)
        state[...] = s_out
        for j, inv in enumerate(invs):
            inv_ref[j] = inv

    nh = SCAN_STEP_CHUNKS * HEADS
    return pl.pallas_call(
        body, name="dn_fwd", grid=(n,),
        in_specs=[_col_spec(r, COL_CQ), _col_spec(r, COL_CK), _col_spec(r, COL_CV),
                  _halo_spec(r, COL_CQ), _halo_spec(r, COL_CK), _halo_spec(r, COL_CV),
                  _small_spec(r, COL128_SMALL_C), _col_spec(r, COL_CZ), _full_spec((3, HALO, GROUP)),
                  _full_spec((1, 128)), _full_spec((1, 128)), _full_spec((1, GROUP))],
        out_specs=[_out_rows_spec(r, GROUP), pl.BlockSpec((1, GROUP, GROUP), lambda i: (i, 0, 0)),
                   pl.BlockSpec((nh, SCAN_CHUNK, SCAN_CHUNK), lambda i: (i, 0, 0))],
        out_shape=[jax.ShapeDtypeStruct((t, GROUP), BF16), jax.ShapeDtypeStruct((n, GROUP, GROUP), F32),
                   jax.ShapeDtypeStruct((n * nh, SCAN_CHUNK, SCAN_CHUNK), F32)],
        scratch_shapes=[pltpu.VMEM((GROUP, GROUP), F32)],
        compiler_params=_cparams(("arbitrary",)),
    )(p, p, p, p, p, p, p, p, cw3, a_log, dt_bias, nw)


def _dn_bwd(p, dmix, states, invs, cw3, a_log, dt_bias, nw):
    t = p.shape[0]
    c = SCAN_ROWS
    n = t // c
    nh = SCAN_STEP_CHUNKS * HEADS

    def body(pq_ref, pk_ref, pv_ref, hq_ref, hk_ref, hv_ref, sm_ref, pz_ref, dy_ref, ck_ref, inv_ref,
             cw_ref, al_ref, dt_ref, nw_ref,
             dpq_ref, dpk_ref, dpv_ref, dsm_ref, dpz_ref, dcw_ref, dal_ref, ddt_ref, dnw_ref,
             dstate, carry):
        i = pl.program_id(0)
        first = i == n - 1

        @pl.when(i == 0)
        def _():
            dstate[...] = jnp.zeros_like(dstate)
            carry[...] = jnp.zeros_like(carry)
            dcw_ref[...] = jnp.zeros_like(dcw_ref)
            dal_ref[...] = jnp.zeros_like(dal_ref)
            ddt_ref[...] = jnp.zeros_like(ddt_ref)
            dnw_ref[...] = jnp.zeros_like(dnw_ref)

        args = (pq_ref[...], pk_ref[...], pv_ref[...], _first_block_zero(hq_ref[...], first),
                _first_block_zero(hk_ref[...], first), _first_block_zero(hv_ref[...], first),
                sm_ref[...], pz_ref[...], cw_ref[0], cw_ref[1], cw_ref[2],
                al_ref[...], dt_ref[...], nw_ref[...], ck_ref[0])
        saved = [inv_ref[j] for j in range(nh)]
        _, vjp = jax.vjp(lambda *a: _dn_block(*a, saved_inv=saved)[:2], *args)
        (dpq, dpk, dpv, dhq, dhk, dhv, dsm, dpz, dcq, dck, dcv, dal, ddt, dnw, dst) = vjp(
            (dy_ref[...], dstate[...]))
        dpq_ref[...] = _add_halo_grad(dpq, carry[0]).astype(dpq_ref.dtype)
        dpk_ref[...] = _add_halo_grad(dpk, carry[1]).astype(dpk_ref.dtype)
        dpv_ref[...] = _add_halo_grad(dpv, carry[2]).astype(dpv_ref.dtype)
        dsm_ref[...] = dsm.astype(dsm_ref.dtype)
        dpz_ref[...] = dpz.astype(dpz_ref.dtype)
        carry[0] = dhq
        carry[1] = dhk
        carry[2] = dhv
        dstate[...] = dst
        dcw_ref[0] += dcq
        dcw_ref[1] += dck
        dcw_ref[2] += dcv
        dal_ref[...] += dal
        ddt_ref[...] += ddt
        dnw_ref[...] += dnw

    return pl.pallas_call(
        body, name="dn_bwd", grid=(n,),
        in_specs=[_col_spec(c, COL_CQ, n), _col_spec(c, COL_CK, n), _col_spec(c, COL_CV, n),
                  _halo_spec(c, COL_CQ, n), _halo_spec(c, COL_CK, n), _halo_spec(c, COL_CV, n),
                  _small_spec(c, COL128_SMALL_C, n), _col_spec(c, COL_CZ, n),
                  pl.BlockSpec((c, GROUP), lambda i: (n - 1 - i, 2)),
                  pl.BlockSpec((1, GROUP, GROUP), lambda i: (n - 1 - i, 0, 0)),
                  pl.BlockSpec((nh, SCAN_CHUNK, SCAN_CHUNK), lambda i: (n - 1 - i, 0, 0)),
                  _full_spec((3, HALO, GROUP)), _full_spec((1, 128)), _full_spec((1, 128)), _full_spec((1, GROUP))],
        out_specs=[_out_rows_spec(c, GROUP, n)] * 3 + [_out_rows_spec(c, 128, n), _out_rows_spec(c, GROUP, n),
                   _full_spec((3, HALO, GROUP)), _full_spec((1, 128)), _full_spec((1, 128)), _full_spec((1, GROUP))],
        out_shape=[jax.ShapeDtypeStruct((t, GROUP), BF16)] * 3 + [
            jax.ShapeDtypeStruct((t, 128), BF16), jax.ShapeDtypeStruct((t, GROUP), BF16),
            jax.ShapeDtypeStruct((3, HALO, GROUP), F32), jax.ShapeDtypeStruct((1, 128), F32),
            jax.ShapeDtypeStruct((1, 128), F32), jax.ShapeDtypeStruct((1, GROUP), F32)],
        scratch_shapes=[pltpu.VMEM((GROUP, GROUP), F32), pltpu.VMEM((3, HALO, GROUP), F32)],
        compiler_params=_cparams(("arbitrary",)),
    )(p, p, p, p, p, p, p, p, dmix, states, invs, cw3, a_log, dt_bias, nw)


def _gla_fwd(p, w2, gbias, nw):
    t = p.shape[0]
    c = SCAN_ROWS
    n = t // c

    def body(pq_ref, pk_ref, pv_ref, sm_ref, pz_ref, w2_ref, gb_ref, nw_ref, y_ref, ck_ref, state):
        @pl.when(pl.program_id(0) == 0)
        def _():
            state[...] = jnp.zeros_like(state)

        s_in = state[...]
        ck_ref[0] = s_in
        y, s_out = _gla_chunk(pq_ref[...], pk_ref[...], pv_ref[...], sm_ref[...], pz_ref[...],
                              w2_ref[...], gb_ref[...], nw_ref[...], s_in)
        y_ref[...] = y.astype(y_ref.dtype)
        state[...] = s_out

    return pl.pallas_call(
        body, name="gla_fwd", grid=(n,),
        in_specs=[_col_spec(c, COL_DQ), _col_spec(c, COL_DK), _col_spec(c, COL_DV),
                  _small_spec(c, COL128_SMALL_D), _col_spec(c, COL_DZ),
                  _full_spec((128, GROUP)), _full_spec((1, GROUP)), _full_spec((1, GROUP))],
        out_specs=[_out_rows_spec(c, GROUP), pl.BlockSpec((1, GROUP, GROUP), lambda i: (i, 0, 0))],
        out_shape=[jax.ShapeDtypeStruct((t, GROUP), BF16), jax.ShapeDtypeStruct((n, GROUP, GROUP), F32)],
        scratch_shapes=[pltpu.VMEM((GROUP, GROUP), F32)],
        compiler_params=_cparams(("arbitrary",)),
    )(p, p, p, p, p, w2, gbias, nw)


def _gla_bwd(p, dmix, states, w2, gbias, nw):
    t = p.shape[0]
    c = SCAN_ROWS
    n = t // c

    def body(pq_ref, pk_ref, pv_ref, sm_ref, pz_ref, dy_ref, ck_ref, w2_ref, gb_ref, nw_ref,
             dpq_ref, dpk_ref, dpv_ref, dsm_ref, dpz_ref, dw2_ref, dgb_ref, dnw_ref, dstate):
        @pl.when(pl.program_id(0) == 0)
        def _():
            dstate[...] = jnp.zeros_like(dstate)
            dw2_ref[...] = jnp.zeros_like(dw2_ref)
            dgb_ref[...] = jnp.zeros_like(dgb_ref)
            dnw_ref[...] = jnp.zeros_like(dnw_ref)

        args = (pq_ref[...], pk_ref[...], pv_ref[...], sm_ref[...], pz_ref[...],
                w2_ref[...], gb_ref[...], nw_ref[...], ck_ref[0])
        _, vjp = jax.vjp(_gla_chunk, *args)
        dpq, dpk, dpv, dsm, dpz, dw2, dgb, dnw, dst = vjp((dy_ref[...], dstate[...]))
        dpq_ref[...] = dpq.astype(dpq_ref.dtype)
        dpk_ref[...] = dpk.astype(dpk_ref.dtype)
        dpv_ref[...] = dpv.astype(dpv_ref.dtype)
        dsm_ref[...] = dsm.astype(dsm_ref.dtype)
        dpz_ref[...] = dpz.astype(dpz_ref.dtype)
        dstate[...] = dst
        dw2_ref[...] += dw2
        dgb_ref[...] += dgb
        dnw_ref[...] += dnw

    return pl.pallas_call(
        body, name="gla_bwd", grid=(n,),
        in_specs=[_col_spec(c, COL_DQ, n), _col_spec(c, COL_DK, n), _col_spec(c, COL_DV, n),
                  _small_spec(c, COL128_SMALL_D, n), _col_spec(c, COL_DZ, n),
                  pl.BlockSpec((c, GROUP), lambda i: (n - 1 - i, 3)),
                  pl.BlockSpec((1, GROUP, GROUP), lambda i: (n - 1 - i, 0, 0)),
                  _full_spec((128, GROUP)), _full_spec((1, GROUP)), _full_spec((1, GROUP))],
        out_specs=[_out_rows_spec(c, GROUP, n)] * 3 + [_out_rows_spec(c, 128, n), _out_rows_spec(c, GROUP, n),
                   _full_spec((128, GROUP)), _full_spec((1, GROUP)), _full_spec((1, GROUP))],
        out_shape=[jax.ShapeDtypeStruct((t, GROUP), BF16)] * 3 + [
            jax.ShapeDtypeStruct((t, 128), BF16), jax.ShapeDtypeStruct((t, GROUP), BF16),
            jax.ShapeDtypeStruct((128, GROUP), F32), jax.ShapeDtypeStruct((1, GROUP), F32),
            jax.ShapeDtypeStruct((1, GROUP), F32)],
        scratch_shapes=[pltpu.VMEM((GROUP, GROUP), F32)],
        compiler_params=_cparams(("arbitrary",)),
    )(p, p, p, p, p, dmix, states, w2, gbias, nw)


def _pick_tile(n, pref):
    for cand in pref:
        if n % cand == 0:
            return cand
    return n


MM_TILE_CAP = 1408


def _largest_tile(n, cap):
    best = None
    for mult in range(1, cap // 128 + 1):
        if n % (128 * mult) == 0:
            best = 128 * mult
    return best if best is not None else n


def _half_index(t, per_half, middle):
    half = jnp.where(t >= per_half, 1, 0)
    return half, middle, t - half * per_half


def _matmul(a, b, mode, out_dtype, name, res=None):
    if mode == "nn":
        (m, k), n = a.shape, b.shape[1]
    elif mode == "nt":
        (m, k), n = a.shape, b.shape[0]
    else:
        (k, m), n = a.shape, b.shape[-1] * (2 if b.ndim == 3 else 1)
    tm = _largest_tile(m, MM_TILE_CAP)
    tn = _largest_tile(b.shape[-1] if b.ndim == 3 else n, MM_TILE_CAP)
    tk = _largest_tile(k, MM_TILE_CAP)
    nk = k // tk
    if mode == "nn":
        a_spec = pl.BlockSpec((tm, tk), lambda i, j, kk: (i, kk))
        b_spec = pl.BlockSpec((tk, tn), lambda i, j, kk: (kk, j))
        dims = (1, 0)
    elif mode == "nt":
        a_spec = pl.BlockSpec((tm, tk), lambda i, j, kk: (i, kk))
        b_spec = pl.BlockSpec((tn, tk), lambda i, j, kk: (j, kk))
        dims = (1, 1)
    else:
        a_spec = pl.BlockSpec((tk, tm), lambda i, j, kk: (kk, i))
        if b.ndim == 3:
            njh = b.shape[-1] // tn
            b_spec = pl.BlockSpec((None, tk, tn), lambda i, j, kk: _half_index(j, njh, kk))
        else:
            b_spec = pl.BlockSpec((tk, tn), lambda i, j, kk: (kk, j))
        dims = (0, 0)
    o_spec = pl.BlockSpec((tm, tn), lambda i, j, kk: (i, j))
    has_res = res is not None

    def body(*refs):
        a_ref, b_ref = refs[:2]
        r_ref = refs[2] if has_res else None
        o_ref = refs[3] if has_res else refs[2]
        part = _dg(a_ref[...].astype(MXU_DTYPE), b_ref[...].astype(MXU_DTYPE), *dims)

        def finish(out):
            if has_res:
                out = out + r_ref[...]
            o_ref[...] = out.astype(o_ref.dtype)

        if nk == 1:
            finish(part)
            return
        acc = refs[-1]
        kk = pl.program_id(2)

        @pl.when(kk == 0)
        def _():
            acc[...] = part

        @pl.when(kk > 0)
        def _():
            acc[...] += part

        @pl.when(kk == nk - 1)
        def _():
            finish(acc[...])

    in_specs = [a_spec, b_spec] + ([o_spec] if has_res else [])
    args = (a, b) + ((res,) if has_res else ())
    return pl.pallas_call(
        body, name=name, grid=(m // tm, n // tn, nk), in_specs=in_specs, out_specs=o_spec,
        out_shape=jax.ShapeDtypeStruct((m, n), out_dtype),
        scratch_shapes=[pltpu.VMEM((tm, tn), F32)] if nk > 1 else [],
        compiler_params=_cparams(("parallel", "parallel", "arbitrary")),
    )(*args)


def _matmul_nt_norm_bwd(a, b, x, w, dres, name):
    n = b.shape[0]
    m = a.shape[-2]
    tm = _largest_tile(m, 1024)
    if a.ndim == 3:
        kh = a.shape[2]
        k = 2 * kh
        tk = _largest_tile(kh, MM_TILE_CAP)
        nkh = kh // tk
        a_spec = pl.BlockSpec((None, tm, tk), lambda i, kk: _half_index(kk, nkh, i))
    else:
        k = a.shape[1]
        tk = _largest_tile(k, MM_TILE_CAP)
        a_spec = pl.BlockSpec((tm, tk), lambda i, kk: (i, kk))
    nk = k // tk

    def body(a_ref, b_ref, x_ref, w_ref, r_ref, dx_ref, dw_ref, acc):
        i, kk = pl.program_id(0), pl.program_id(1)
        part = _dg(a_ref[...].astype(MXU_DTYPE), b_ref[...].astype(MXU_DTYPE), 1, 1)

        @pl.when(kk == 0)
        def _():
            acc[...] = part

        @pl.when(kk > 0)
        def _():
            acc[...] += part

        @pl.when((i == 0) & (kk == 0))
        def _():
            dw_ref[...] = jnp.zeros_like(dw_ref)

        @pl.when(kk == nk - 1)
        def _():
            g = acc[...]
            xv = x_ref[...]
            r = lax.rsqrt(jnp.mean(xv * xv, axis=-1, keepdims=True) + EPS)
            xhat = xv * r
            dw_ref[...] += jnp.sum(g * xhat, axis=0, keepdims=True)
            gx = g * w_ref[...]
            dx_ref[...] = r_ref[...] + r * (gx - xhat * jnp.mean(gx * xhat, axis=-1, keepdims=True))

    row_spec = pl.BlockSpec((tm, n), lambda i, kk: (i, 0))
    return pl.pallas_call(
        body, name=name, grid=(m // tm, nk),
        in_specs=[a_spec, pl.BlockSpec((n, tk), lambda i, kk: (0, kk)),
                  row_spec, pl.BlockSpec((1, n), lambda i, kk: (0, 0)), row_spec],
        out_specs=[row_spec, pl.BlockSpec((1, n), lambda i, kk: (0, 0))],
        out_shape=[jax.ShapeDtypeStruct((m, n), F32), jax.ShapeDtypeStruct((1, n), F32)],
        scratch_shapes=[pltpu.VMEM((tm, n), F32)],
        compiler_params=_cparams(("arbitrary", "arbitrary")),
    )(a, b, x, w, dres)


NORM_ROWS = 512


def _rmsnorm_fwd(x, w, name):
    t, d = x.shape

    def body(x_ref, w_ref, o_ref):
        xv = x_ref[...]
        r = lax.rsqrt(jnp.mean(xv * xv, axis=-1, keepdims=True) + EPS)
        o_ref[...] = (xv * r * w_ref[...]).astype(o_ref.dtype)

    return pl.pallas_call(
        body, name=name, grid=(t // NORM_ROWS,),
        in_specs=[pl.BlockSpec((NORM_ROWS, d), lambda i: (i, 0)), _full_spec((1, d))],
        out_specs=pl.BlockSpec((NORM_ROWS, d), lambda i: (i, 0)),
        out_shape=jax.ShapeDtypeStruct((t, d), BF16),
        compiler_params=_cparams(("parallel",)),
    )(x, w)


SWIGLU_ROWS = 128


def _ffn_up_swiglu(h, w_gate_up):
    m, k = h.shape
    tm = _largest_tile(m, 512)
    tn = _largest_tile(D_FF, MM_TILE_CAP)
    nj = D_FF // tn

    def body(a_ref, bg_ref, bu_ref, g_ref, u_ref, act_ref):
        a = a_ref[...].astype(MXU_DTYPE)
        gate = _dg(a, bg_ref[...].astype(MXU_DTYPE), 1, 0)
        up = _dg(a, bu_ref[...].astype(MXU_DTYPE), 1, 0)
        g_ref[...] = gate
        u_ref[...] = up
        act_ref[...] = (_silu(gate) * up).astype(act_ref.dtype)

    o_spec = pl.BlockSpec((tm, tn), lambda i, j: (i, j))
    return pl.pallas_call(
        body, name="ffn_up", grid=(m // tm, nj),
        in_specs=[pl.BlockSpec((tm, k), lambda i, j: (i, 0)), pl.BlockSpec((k, tn), lambda i, j: (0, j)),
                  pl.BlockSpec((k, tn), lambda i, j: (0, j + nj))],
        out_specs=[o_spec, o_spec, o_spec],
        out_shape=[jax.ShapeDtypeStruct((m, D_FF), F32), jax.ShapeDtypeStruct((m, D_FF), F32),
                   jax.ShapeDtypeStruct((m, D_FF), BF16)],
        compiler_params=_cparams(("parallel", "parallel")),
    )(h, w_gate_up, w_gate_up)


def _ffn_down_dx_swiglu(dx, w_down, gate, up):
    m, k = dx.shape
    tm = _largest_tile(m, 512)
    tn = _largest_tile(D_FF, MM_TILE_CAP)

    def body(a_ref, b_ref, g_ref, u_ref, o_ref):
        da = _dg(a_ref[...].astype(MXU_DTYPE), b_ref[...].astype(MXU_DTYPE), 1, 1)
        gate = g_ref[...]
        sg = jax.nn.sigmoid(gate)
        o_ref[0] = (da * u_ref[...] * (sg * (1.0 + gate * (1.0 - sg)))).astype(o_ref.dtype)
        o_ref[1] = (da * gate * sg).astype(o_ref.dtype)

    tile = pl.BlockSpec((tm, tn), lambda i, j: (i, j))
    return pl.pallas_call(
        body, name="ffn_down_dx", grid=(m // tm, D_FF // tn),
        in_specs=[pl.BlockSpec((tm, k), lambda i, j: (i, 0)), pl.BlockSpec((tn, k), lambda i, j: (j, 0)), tile, tile],
        out_specs=pl.BlockSpec((2, tm, tn), lambda i, j: (0, i, j)),
        out_shape=jax.ShapeDtypeStruct((2, m, D_FF), MXU_DTYPE),
        compiler_params=_cparams(("parallel", "parallel")),
    )(dx, w_down, gate, up)


def _loss_head(x, w, target):
    t, d = x.shape

    def fwd(xv, wv, tv):
        r = lax.rsqrt(jnp.mean(xv * xv, axis=-1, keepdims=True) + EPS)
        err = xv * r * wv - tv
        return 0.5 * jnp.sum(jnp.mean(err * err, axis=-1, keepdims=True), axis=0, keepdims=True)

    def body(x_ref, w_ref, t_ref, dx_ref, dw_ref, loss_ref):
        @pl.when(pl.program_id(0) == 0)
        def _():
            dw_ref[...] = jnp.zeros_like(dw_ref)
            loss_ref[...] = jnp.zeros_like(loss_ref)

        loss, vjp = jax.vjp(fwd, x_ref[...], w_ref[...], t_ref[...])
        dx, dw, _ = vjp(jnp.ones((1, 1), F32))
        dx_ref[...] = dx
        dw_ref[...] += dw
        loss_ref[...] += jnp.broadcast_to(loss, loss_ref.shape)

    return pl.pallas_call(
        body, name="loss_head", grid=(t // NORM_ROWS,),
        in_specs=[pl.BlockSpec((NORM_ROWS, d), lambda i: (i, 0)), _full_spec((1, d)),
                  pl.BlockSpec((NORM_ROWS, d), lambda i: (i, 0))],
        out_specs=[pl.BlockSpec((NORM_ROWS, d), lambda i: (i, 0)), _full_spec((1, d)), _full_spec((8, 128))],
        out_shape=[jax.ShapeDtypeStruct((t, d), F32), jax.ShapeDtypeStruct((1, d), F32),
                   jax.ShapeDtypeStruct((8, 128), F32)],
        compiler_params=_cparams(("arbitrary",)),
    )(x, w, target)


def _pad_w_in(w):
    z = lambda n: jnp.zeros((w.shape[0], n), w.dtype)
    return jnp.concatenate([w[:, 0:2048], w[:, 2056:2312], w[:, 2312:3080], w[:, 3096:3352],
                            w[:, 2048:2056], z(120), w[:, 3080:3096], z(112)], axis=1)


def _unpad_w_in(wp):
    return jnp.concatenate([wp[:, 0:2048], wp[:, 3328:3336], wp[:, 2048:2304], wp[:, 2304:3072],
                            wp[:, 3456:3472], wp[:, 3072:3328]], axis=1)


def _pad_rows(a, rows):
    return jnp.concatenate([a, jnp.zeros((rows - a.shape[0],) + a.shape[1:], a.dtype)], axis=0)


def _pad_lanes(a, lanes):
    return jnp.concatenate([a, jnp.zeros(a.shape[:-1] + (lanes - a.shape[-1],), a.dtype)], axis=-1)


def _layer_params(l, small):
    dn_cw = small["dn_conv_w"][l]
    return dict(
        ln_w=small["sgu_ln_w"][l][None], ln_b=small["sgu_ln_b"][l][None],
        ws=small["sgu_w_spatial"][l], bs_t=_pad_lanes(small["sgu_b_spatial"][l].T, 128),
        sc_cw=_pad_rows(small["sc_conv_w"][l], HALO),
        dn_cw=jnp.stack([_pad_rows(dn_cw[:, j * GROUP:(j + 1) * GROUP], HALO) for j in range(3)]),
        dn_al=_pad_lanes(small["dn_a_log"][l][None], 128), dn_dt=_pad_lanes(small["dn_dt_bias"][l][None], 128),
        dn_nw=jnp.tile(small["dn_norm_w"][l][None], (1, HEADS)),
        gla_w2=_pad_rows(small["gla_w_gate2"][l], 128), gla_gb=small["gla_gate_bias"][l][None],
        gla_nw=jnp.tile(small["gla_norm_w"][l][None], (1, HEADS)),
    )


def _local_step(x, target, big, small):
    saved = []
    h = x
    for l in range(DEPTH):
        lp = _layer_params(l, small)
        h1 = _rmsnorm_fwd(h, small["norm1_w"][l][None], "norm1_fwd")
        p = _matmul(h1, big["w_in"][l], "nn", F32, "proj_in")
        y_a = _sgu_fwd(p, lp["ln_w"], lp["ln_b"], lp["ws"], lp["bs_t"])
        y_b = _sc_fwd(p, lp["sc_cw"])
        y_c, st_c, inv_c = _dn_fwd(p, lp["dn_cw"], lp["dn_al"], lp["dn_dt"], lp["dn_nw"])
        y_d, st_d = _gla_fwd(p, lp["gla_w2"], lp["gla_gb"], lp["gla_nw"])
        mix = jnp.concatenate([y_a, y_b, y_c, y_d], axis=1)
        x1 = _matmul(mix, big["w_out"][l], "nn", F32, "proj_out", res=h)
        h2 = _rmsnorm_fwd(x1, small["norm2_w"][l][None], "norm2_fwd")
        gate, up, act = _ffn_up_swiglu(h2, big["w_gate_up"][l])
        x2 = _matmul(act, big["w_down"][l], "nn", F32, "ffn_down", res=x1)
        saved.append(dict(x0=h, h1=h1, p=p, st_c=st_c, inv_c=inv_c, st_d=st_d, mix=mix, x1=x1, h2=h2, gate=gate, up=up, act=act, lp=lp))
        h = x2

    dx, d_final, loss = _loss_head(h, small["final_norm_w"][None], target)
    gbig = {k: [None] * DEPTH for k in ("w_in", "w_out", "w_gate_up", "w_down")}
    gs = {k: [None] * DEPTH for k in ("norm1_w", "sgu_ln_w", "sgu_ln_b", "sgu_w_spatial", "sgu_b_spatial", "sc_conv_w",
                                     "dn_conv_w", "dn_a_log", "dn_dt_bias", "dn_norm_w", "gla_w_gate2",
                                     "gla_gate_bias", "gla_norm_w", "norm2_w")}
    for l in reversed(range(DEPTH)):
        s = saved[l]
        lp = s["lp"]
        gbig["w_down"][l] = _matmul(s["act"], dx, "tn", GRAD_WIRE_DTYPE, "ffn_down_dw")
        dgu = _ffn_down_dx_swiglu(dx, big["w_down"][l], s["gate"], s["up"])
        gbig["w_gate_up"][l] = _matmul(s["h2"], dgu, "tn", GRAD_WIRE_DTYPE, "ffn_up_dw")
        dx1, gs["norm2_w"][l] = _matmul_nt_norm_bwd(dgu, big["w_gate_up"][l], s["x1"], small["norm2_w"][l][None], dx,
                                                    "ffn_up_dx")
        gbig["w_out"][l] = _matmul(s["mix"], dx1, "tn", GRAD_WIRE_DTYPE, "proj_out_dw")
        dmix = _matmul(dx1, big["w_out"][l], "nt", F32, "proj_out_dx")
        p = s["p"]
        dpu, dpv, g_lw, g_lb, g_ws, g_bs = _sgu_bwd(p, dmix, lp["ln_w"], lp["ln_b"], lp["ws"], lp["bs_t"])
        dpb, dpc, dph, g_sc = _sc_bwd(p, dmix, lp["sc_cw"])
        dcq, dck, dcv, dcs, dcz, g_dcw, g_al, g_dt, g_dnw = _dn_bwd(
            p, dmix, s["st_c"], s["inv_c"], lp["dn_cw"], lp["dn_al"], lp["dn_dt"], lp["dn_nw"])
        ddq, ddk, ddv, dds, ddz, g_w2, g_gb, g_gnw = _gla_bwd(p, dmix, s["st_d"], lp["gla_w2"], lp["gla_gb"],
                                                             lp["gla_nw"])
        dp = jnp.concatenate([dpu, dpv, dpb, dpc, dph, dcq, dck, dcv, dcz, ddq, ddk, ddv, ddz, dcs, dds], axis=1)
        gbig["w_in"][l] = _matmul(s["h1"], dp, "tn", GRAD_WIRE_DTYPE, "proj_in_dw")
        dx, gs["norm1_w"][l] = _matmul_nt_norm_bwd(dp, big["w_in"][l], s["x0"], small["norm1_w"][l][None], dx1,
                                                   "proj_in_dx")
        gs["sgu_ln_w"][l], gs["sgu_ln_b"][l] = g_lw[0], g_lb[0]
        gs["sgu_w_spatial"][l] = g_ws
        gs["sgu_b_spatial"][l] = g_bs[:, :HEADS].T
        gs["sc_conv_w"][l] = g_sc[:3]
        gs["dn_conv_w"][l] = jnp.concatenate([g_dcw[0, :4], g_dcw[1, :4], g_dcw[2, :4]], axis=1)
        gs["dn_a_log"][l], gs["dn_dt_bias"][l] = g_al[0, :HEADS], g_dt[0, :HEADS]
        gs["dn_norm_w"][l] = jnp.sum(g_dnw.reshape(HEADS, HEAD_DIM), axis=0)
        gs["gla_w_gate2"][l] = g_w2[:16]
        gs["gla_gate_bias"][l] = g_gb[0]
        gs["gla_norm_w"][l] = jnp.sum(g_gnw.reshape(HEADS, HEAD_DIM), axis=0)
        gs["norm1_w"][l] = gs["norm1_w"][l][0]
        gs["norm2_w"][l] = gs["norm2_w"][l][0]
    gsmall = {k: jnp.stack(v) for k, v in gs.items()}
    gsmall["final_norm_w"] = d_final[0]
    return loss, dx, gbig, gsmall


def _peer_chips(x, y):
    return [(1 - x, y, 2 * (1 - x) + y), (x, 1 - y, 2 * x + 1 - y), (1 - x, 1 - y, 2 * (1 - x) + 1 - y)]


def _chip_exchange(arrays, modes, name):
    na = len(arrays)
    c_idx = lax.axis_index("c")
    chip = 2 * lax.axis_index("x") + lax.axis_index("y")
    bufs = []
    for arr, md in zip(arrays, modes):
        if md == "layer":
            unit = lax.dynamic_index_in_dim(arr, c_idx, 0, keepdims=False)
        elif md == "piece":
            unit = lax.dynamic_index_in_dim(arr, chip, 0, keepdims=False)
        else:
            unit = arr
        buf = lax.empty((2, N_CHIPS) + unit.shape, unit.dtype)
        bufs.append(lax.dynamic_update_slice(buf, unit[None, None], (c_idx, chip) + (0,) * unit.ndim))

    def body(*refs):
        ins, outs = refs[:na], refs[2 * na:3 * na]
        send1, recv1, send2, recv2 = refs[3 * na:]
        x, y, c = lax.axis_index("x"), lax.axis_index("y"), lax.axis_index("c")
        me = 2 * x + y
        sibling = (x, y, 1 - c)

        def src(a, chip):
            if modes[a] == "layer":
                return ins[a].at[c]
            return ins[a].at[chip] if modes[a] == "piece" else ins[a]

        for a in range(na):
            for k, (px, py, pidx) in enumerate(_peer_chips(x, y)):
                pltpu.make_async_remote_copy(
                    src_ref=src(a, pidx), dst_ref=outs[a].at[c, me], send_sem=send1.at[a, k], recv_sem=recv1.at[a, k],
                    device_id=(px, py, c), device_id_type=MESH).start()
        for a in range(na):
            for k, (px, py, pidx) in enumerate(_peer_chips(x, y)):
                cp = pltpu.make_async_remote_copy(
                    src_ref=src(a, pidx), dst_ref=outs[a].at[c, pidx], send_sem=send1.at[a, k],
                    recv_sem=recv1.at[a, k], device_id=(px, py, c), device_id_type=MESH)
                cp.wait_send()
                cp.wait_recv()
        for a in range(na):
            pltpu.make_async_remote_copy(
                src_ref=outs[a].at[c], dst_ref=outs[a].at[c], send_sem=send2.at[a], recv_sem=recv2.at[a],
                device_id=sibling, device_id_type=MESH).start()
        for a in range(na):
            cp = pltpu.make_async_remote_copy(
                src_ref=outs[a].at[c], dst_ref=outs[a].at[1 - c], send_sem=send2.at[a], recv_sem=recv2.at[a],
                device_id=sibling, device_id_type=MESH)
            cp.wait_send()
            cp.wait_recv()

    any_spec = pl.BlockSpec(memory_space=pl.ANY)
    units = [s.shape if md == "whole" else s.shape[1:] for s, md in zip(arrays, modes)]
    return pl.pallas_call(
        body, name=name,
        in_specs=[any_spec] * (2 * na), out_specs=[any_spec] * na,
        out_shape=[jax.ShapeDtypeStruct((2, N_CHIPS) + u, s.dtype) for u, s in zip(units, arrays)],
        input_output_aliases={na + a: a for a in range(na)},
        scratch_shapes=[pltpu.SemaphoreType.DMA((na, 3)), pltpu.SemaphoreType.DMA((na, 3)),
                        pltpu.SemaphoreType.DMA((na,)), pltpu.SemaphoreType.DMA((na,))],
    )(*arrays, *bufs)


def _sibling_swap(arrays):
    na = len(arrays)

    def body(*refs):
        ins, theirs = refs[:na], refs[na:2 * na]
        send_sems, recv_sems = refs[2 * na:]
        x, y, c = lax.axis_index("x"), lax.axis_index("y"), lax.axis_index("c")
        sibling = (x, y, 1 - c)
        for a in range(na):
            pltpu.make_async_remote_copy(
                src_ref=ins[a].at[1 - c], dst_ref=theirs[a], send_sem=send_sems.at[a], recv_sem=recv_sems.at[a],
                device_id=sibling, device_id_type=MESH).start()
        for a in range(na):
            cp = pltpu.make_async_remote_copy(
                src_ref=ins[a].at[1 - c], dst_ref=theirs[a], send_sem=send_sems.at[a], recv_sem=recv_sems.at[a],
                device_id=sibling, device_id_type=MESH)
            cp.wait_send()
            cp.wait_recv()

    any_spec = pl.BlockSpec(memory_space=pl.ANY)
    return pl.pallas_call(
        body, name="sibling_swap",
        in_specs=[any_spec] * na, out_specs=[any_spec] * na,
        out_shape=[jax.ShapeDtypeStruct(s.shape[1:], s.dtype) for s in arrays],
        scratch_shapes=[pltpu.SemaphoreType.DMA((na,)), pltpu.SemaphoreType.DMA((na,))],
    )(*arrays)


def _pair_add(both, theirs, name):
    n, r, c = theirs.shape
    tr = _pick_tile(r, (256, 176, 128, 64, 8))
    core = lax.axis_index("c").astype(jnp.int32).reshape(1)

    def body(core_ref, a_ref, b_ref, o_ref):
        o_ref[...] = (a_ref[...].astype(F32) + b_ref[...].astype(F32)).astype(o_ref.dtype)

    return pl.pallas_call(
        body, name=name,
        grid_spec=pltpu.PrefetchScalarGridSpec(
            num_scalar_prefetch=1, grid=(n, r // tr),
            in_specs=[pl.BlockSpec((None, 1, tr, c), lambda i, j, core_ref: (core_ref[0], i, j, 0)),
                      pl.BlockSpec((1, tr, c), lambda i, j, core_ref: (i, j, 0))],
            out_specs=pl.BlockSpec((1, tr, c), lambda i, j, core_ref: (i, j, 0))),
        out_shape=jax.ShapeDtypeStruct(theirs.shape, theirs.dtype),
        compiler_params=_cparams(("parallel", "parallel")),
    )(core, both, theirs)


def _adamw_math(g, w, m, v):
    m2 = ADAM_B1 * m + (1.0 - ADAM_B1) * g
    v2 = ADAM_B2 * v + (1.0 - ADAM_B2) * (g * g)
    m_hat = m2 / (1.0 - ADAM_B1 ** ADAM_STEP)
    v_hat = v2 / (1.0 - ADAM_B2 ** ADAM_STEP)
    delta = -ADAM_LR * (m_hat / (jnp.sqrt(v_hat) + ADAM_EPS) + ADAM_WD * w)
    return delta, m2, v2


def _adamw_big(contrib, w, m, v, name):
    l, r, c = w.shape
    tr = _pick_tile(r, (256, 176, 128, 64, 8))
    blk = pl.BlockSpec((1, tr, c), lambda i, j: (i, j, 0))

    def body(g_ref, w_ref, m_ref, v_ref, go_ref, d_ref, mo_ref, vo_ref):
        g = g_ref[0, 0].astype(F32)
        for s in range(1, N_CHIPS):
            g = g + g_ref[0, s].astype(F32)
        delta, m2, v2 = _adamw_math(g, w_ref[0], m_ref[0], v_ref[0])
        go_ref[0] = g
        d_ref[0] = delta
        mo_ref[0] = m2
        vo_ref[0] = v2

    return pl.pallas_call(
        body, name=name, grid=(l, r // tr),
        in_specs=[pl.BlockSpec((1, N_CHIPS, tr, c), lambda i, j: (i, 0, j, 0)), blk, blk, blk],
        out_specs=[blk] * 4, out_shape=[jax.ShapeDtypeStruct(w.shape, F32)] * 4,
        compiler_params=_cparams(("parallel", "parallel")),
    )(contrib, w, m, v)


def _sum_small(contrib):
    rows = contrib.shape[2]

    def body(g_ref, o_ref):
        total = g_ref[0, 0]
        for j in range(1, N_DEV):
            total = total + g_ref[j // N_CHIPS, j % N_CHIPS]
        o_ref[...] = total

    return pl.pallas_call(
        body, name="sum_small", out_shape=jax.ShapeDtypeStruct((rows, 128), F32),
        compiler_params=_cparams(),
    )(contrib)


def _adamw_small(gs, ws, ms, vs):
    n = len(gs)
    as2d = lambda a: a.reshape(1, -1) if a.ndim == 1 else a

    def body(*refs):
        g_refs, w_refs, m_refs, v_refs = refs[:n], refs[n:2 * n], refs[2 * n:3 * n], refs[3 * n:4 * n]
        d_refs, mo_refs, vo_refs = refs[4 * n:5 * n], refs[5 * n:6 * n], refs[6 * n:]
        for j in range(n):
            delta, m2, v2 = _adamw_math(g_refs[j][...], w_refs[j][...], m_refs[j][...], v_refs[j][...])
            d_refs[j][...] = delta
            mo_refs[j][...] = m2
            vo_refs[j][...] = v2

    ins = [as2d(a) for a in (*gs, *ws, *ms, *vs)]
    outs = pl.pallas_call(
        body, name="adamw_small", out_shape=[jax.ShapeDtypeStruct(a.shape, F32) for a in ins[:n]] * 3,
        compiler_params=_cparams(),
    )(*ins)
    back = lambda group: [o.reshape(g.shape) for o, g in zip(group, gs)]
    return back(outs[:n]), back(outs[n:2 * n]), back(outs[2 * n:])


PACK_ALIGN = 8 * 128


def _packed_rows(shape):
    n = 1
    for d in shape:
        n *= d
    return (n + PACK_ALIGN - 1) // PACK_ALIGN * 8


def _pack(arrays):
    parts = []
    for a in arrays:
        flat = a.reshape(-1)
        pad = _packed_rows(a.shape) * 128 - flat.shape[0]
        if pad:
            flat = jnp.concatenate([flat, jnp.zeros((pad,), F32)])
        parts.append(flat.reshape(-1, 128))
    return jnp.concatenate(parts, axis=0)


def _unpack(packed, shapes):
    out, row = [], 0
    for s in shapes:
        rows = _packed_rows(s)
        n = 1
        for d in s:
            n *= d
        out.append(packed[row:row + rows].reshape(-1)[:n].reshape(s))
        row += rows
    return out


SMALL_NAMES = ("norm1_w", "sgu_ln_w", "sgu_ln_b", "sgu_w_spatial", "sgu_b_spatial", "sc_conv_w", "dn_conv_w",
               "dn_a_log", "dn_dt_bias", "dn_norm_w", "gla_w_gate2", "gla_gate_bias", "gla_norm_w", "norm2_w",
               "final_norm_w")
SHARDED_SMALL = ("sc_conv_w", "dn_conv_w", "gla_w_gate2")
BIG_NAMES = ("w_in", "w_out", "w_gate_up", "w_down")
WEIGHT_ORDER = ("norm1_w", "w_in", "sgu_ln_w", "sgu_ln_b", "sgu_w_spatial", "sgu_b_spatial", "sc_conv_w", "dn_conv_w",
                "dn_a_log", "dn_dt_bias", "dn_norm_w", "gla_w_gate2", "gla_gate_bias", "gla_norm_w", "w_out",
                "norm2_w", "w_gate_up", "w_down", "final_norm_w")


def _cols_from_shards(g):
    l, n, r, c = g.shape
    return jnp.transpose(g, (0, 2, 1, 3)).reshape(l, r, n * c)


def _rows_from_shards(g):
    l, n, r, c = g.shape
    return g.reshape(l, n * r, c)


def _cols_to_shards(full):
    l, r, c4 = full.shape
    return jnp.transpose(full.reshape(l, r, N_CHIPS, c4 // N_CHIPS), (0, 2, 1, 3))


def _rows_to_shards(full):
    l, r4, c = full.shape
    return full.reshape(l, N_CHIPS, r4 // N_CHIPS, c)


def kernel(x, norm1_w, w_in, sgu_ln_w, sgu_ln_b, sgu_w_spatial, sgu_b_spatial, sc_conv_w, dn_conv_w, dn_a_log, dn_dt_bias, dn_norm_w, gla_w_gate2, gla_gate_bias, gla_norm_w, w_out, norm2_w, w_gate_up, w_down, final_norm_w, loss_target, m_norm1_w, m_w_in, m_sgu_ln_w, m_sgu_ln_b, m_sgu_w_spatial, m_sgu_b_spatial, m_sc_conv_w, m_dn_conv_w, m_dn_a_log, m_dn_dt_bias, m_dn_norm_w, m_gla_w_gate2, m_gla_gate_bias, m_gla_norm_w, m_w_out, m_norm2_w, m_w_gate_up, m_w_down, m_final_norm_w, v_norm1_w, v_w_in, v_sgu_ln_w, v_sgu_ln_b, v_sgu_w_spatial, v_sgu_b_spatial, v_sc_conv_w, v_dn_conv_w, v_dn_a_log, v_dn_dt_bias, v_dn_norm_w, v_gla_w_gate2, v_gla_gate_bias, v_gla_norm_w, v_w_out, v_norm2_w, v_w_gate_up, v_w_down, v_final_norm_w):
    w = dict(norm1_w=norm1_w, w_in=w_in, sgu_ln_w=sgu_ln_w, sgu_ln_b=sgu_ln_b, sgu_w_spatial=sgu_w_spatial,
             sgu_b_spatial=sgu_b_spatial, sc_conv_w=sc_conv_w, dn_conv_w=dn_conv_w, dn_a_log=dn_a_log,
             dn_dt_bias=dn_dt_bias, dn_norm_w=dn_norm_w, gla_w_gate2=gla_w_gate2, gla_gate_bias=gla_gate_bias,
             gla_norm_w=gla_norm_w, w_out=w_out, norm2_w=norm2_w, w_gate_up=w_gate_up, w_down=w_down,
             final_norm_w=final_norm_w)
    m = dict(norm1_w=m_norm1_w, w_in=m_w_in, sgu_ln_w=m_sgu_ln_w, sgu_ln_b=m_sgu_ln_b, sgu_w_spatial=m_sgu_w_spatial,
             sgu_b_spatial=m_sgu_b_spatial, sc_conv_w=m_sc_conv_w, dn_conv_w=m_dn_conv_w, dn_a_log=m_dn_a_log,
             dn_dt_bias=m_dn_dt_bias, dn_norm_w=m_dn_norm_w, gla_w_gate2=m_gla_w_gate2,
             gla_gate_bias=m_gla_gate_bias, gla_norm_w=m_gla_norm_w, w_out=m_w_out, norm2_w=m_norm2_w,
             w_gate_up=m_w_gate_up, w_down=m_w_down, final_norm_w=m_final_norm_w)
    v = dict(norm1_w=v_norm1_w, w_in=v_w_in, sgu_ln_w=v_sgu_ln_w, sgu_ln_b=v_sgu_ln_b, sgu_w_spatial=v_sgu_w_spatial,
             sgu_b_spatial=v_sgu_b_spatial, sc_conv_w=v_sc_conv_w, dn_conv_w=v_dn_conv_w, dn_a_log=v_dn_a_log,
             dn_dt_bias=v_dn_dt_bias, dn_norm_w=v_dn_norm_w, gla_w_gate2=v_gla_w_gate2,
             gla_gate_bias=v_gla_gate_bias, gla_norm_w=v_gla_norm_w, w_out=v_w_out, norm2_w=v_norm2_w,
             w_gate_up=v_w_gate_up, w_down=v_w_down, final_norm_w=v_final_norm_w)
    chip = 2 * lax.axis_index("x") + lax.axis_index("y")

    shards = [w[n].astype(MXU_DTYPE) for n in BIG_NAMES] + [w[n] for n in SHARDED_SMALL]
    gathered = _chip_exchange(shards, ["layer"] * len(shards), "gather_weights")
    full_in = _cols_from_shards(gathered[0])
    big = dict(
        w_in=[_pad_w_in(full_in[l]) for l in range(DEPTH)],
        w_out=_rows_from_shards(gathered[1]),
        w_gate_up=_cols_from_shards(gathered[2]),
        w_down=_rows_from_shards(gathered[3]),
    )
    small = {n: w[n] for n in SMALL_NAMES if n not in SHARDED_SMALL}
    for j, n in enumerate(SHARDED_SMALL):
        small[n] = _cols_from_shards(gathered[4 + j])

    loss_tile, grad_x, gbig, gsmall = _local_step(x[0], loss_target[0], big, small)

    pieces = [
        _cols_to_shards(jnp.stack([_unpad_w_in(g) for g in gbig["w_in"]])),
        _rows_to_shards(jnp.stack(gbig["w_out"])),
        _cols_to_shards(jnp.stack(gbig["w_gate_up"])),
        _rows_to_shards(jnp.stack(gbig["w_down"])),
    ]
    small_list = [gsmall[n] for n in SMALL_NAMES] + [loss_tile[0:1, 0]]
    small_shapes = [a.shape for a in small_list]
    theirs = _sibling_swap(pieces)
    parts = [_pair_add(a, b, "pair_add_" + n) for a, b, n in zip(pieces, theirs, BIG_NAMES)]
    contrib = _chip_exchange(parts + [_pack(small_list)], ["piece"] * len(parts) + ["whole"], "exchange_grads")

    out_g, out_d, out_m, out_v = {}, {}, {}, {}
    for j, n in enumerate(BIG_NAMES):
        out_g[n], out_d[n], out_m[n], out_v[n] = _adamw_big(contrib[j], w[n], m[n], v[n], "adamw_" + n)
    summed = _unpack(_sum_small(contrib[len(BIG_NAMES)]), small_shapes)
    loss = summed[-1][0]
    for n, g in zip(SMALL_NAMES, summed[:-1]):
        if n in SHARDED_SMALL:
            cols = g.shape[-1] // N_CHIPS
            g = lax.dynamic_slice_in_dim(g, chip * cols, cols, axis=g.ndim - 1)
        out_g[n] = g
    d_s, m_s, v_s = _adamw_small([out_g[n] for n in SMALL_NAMES], [w[n] for n in SMALL_NAMES],
                                 [m[n] for n in SMALL_NAMES], [v[n] for n in SMALL_NAMES])
    for n, d_, m_, v_ in zip(SMALL_NAMES, d_s, m_s, v_s):
        out_d[n], out_m[n], out_v[n] = d_, m_, v_

    return (loss, grad_x[None], *[out_g[n] for n in WEIGHT_ORDER], *[out_d[n] for n in WEIGHT_ORDER],
            *[out_m[n] for n in WEIGHT_ORDER], *[out_v[n] for n in WEIGHT_ORDER])
```

```python
import functools

import jax
import jax.numpy as jnp
from jax import lax
from jax.experimental import pallas as pl
from jax.experimental.pallas import tpu as pltpu

F32 = jnp.float32
BF16 = jnp.bfloat16
MXU_DTYPE = jnp.bfloat16
GRAD_WIRE_DTYPE = jnp.bfloat16
HI = lax.Precision.HIGHEST
MESH = pl.DeviceIdType.MESH

D_MODEL = 1024
DEPTH = 2
GROUP = 256
HEADS = 4
HEAD_DIM = 64
SGU_CHUNK = 128
SCAN_CHUNK = 64
D_FF = 2816
EPS = 1e-6
IN_COLS = 3352
P_COLS = 3584
HALO = 8
N_CHIPS = 4
N_DEV = 8
VMEM_LIMIT = 56 * 1024 * 1024

ADAM_LR = 0.001
ADAM_B1 = 0.9
ADAM_B2 = 0.999
ADAM_EPS = 1e-08
ADAM_WD = 0.01
ADAM_STEP = 10

(COL_AU, COL_AV, COL_BB, COL_BC, COL_BH, COL_CQ, COL_CK, COL_CV, COL_CZ,
 COL_DQ, COL_DK, COL_DV, COL_DZ) = range(13)
COL128_SMALL_C = 26
COL128_SMALL_D = 27


def _cparams(sem=None):
    return pltpu.CompilerParams(dimension_semantics=sem, vmem_limit_bytes=VMEM_LIMIT)


def _iota(shape, dim):
    return lax.broadcasted_iota(jnp.int32, shape, dim)


def _dg(a, b, ca, cb, prec=None):
    return lax.dot_general(a, b, (((ca,), (cb,)), ((), ())), preferred_element_type=F32, precision=prec)


@functools.partial(jax.custom_vjp, nondiff_argnums=(2, 3))
def bdot(a, b, ca, cb):
    return _dg(a.astype(MXU_DTYPE), b.astype(MXU_DTYPE), ca, cb)


def _bdot_fwd(a, b, ca, cb):
    return bdot(a, b, ca, cb), (a, b)


def _bdot_bwd(ca, cb, res, g):
    a, b = res
    if ca == 1:
        da = bdot(g, b, 1, 1 if cb == 0 else 0)
    else:
        da = bdot(b, g, 1 if cb == 0 else 0, 1)
    if cb == 0:
        db = bdot(a, g, 0, 0) if ca == 1 else bdot(a, g, 1, 0)
    else:
        db = bdot(g, a, 0, 0) if ca == 1 else bdot(g, a, 0, 1)
    return da, db


bdot.defvjp(_bdot_fwd, _bdot_bwd)


def _pieces(a, n):
    out, r = [], a
    for i in range(n):
        p = r.astype(MXU_DTYPE)
        out.append(p)
        if i + 1 < n:
            r = r - p.astype(F32)
    return out


def _mdot_impl(a, b, ca, cb, sa, sb):
    total = None
    for i, x in enumerate(_pieces(a, sa)):
        for j, y in enumerate(_pieces(b, sb)):
            if i + j < max(sa, sb):
                t = _dg(x, y, ca, cb)
                total = t if total is None else total + t
    return total


@functools.partial(jax.custom_vjp, nondiff_argnums=(2, 3, 4, 5))
def mdot(a, b, ca, cb, sa, sb):
    return _mdot_impl(a, b, ca, cb, sa, sb)


def _mdot_fwd(a, b, ca, cb, sa, sb):
    return _mdot_impl(a, b, ca, cb, sa, sb), (a, b)


def _mdot_bwd(ca, cb, sa, sb, res, g):
    a, b = res
    ga, gb = (3 if sb == 1 else 2), (3 if sa == 1 else 2)
    if sa == 1:
        da = jnp.zeros_like(a)
    elif ca == 1:
        da = mdot(g, b, 1, 1 if cb == 0 else 0, ga, sb)
    else:
        da = mdot(b, g, 1 if cb == 0 else 0, 1, sb, ga)
    if sb == 1:
        db = jnp.zeros_like(b)
    elif cb == 0:
        db = mdot(a, g, 0, 0, sa, gb) if ca == 1 else mdot(a, g, 1, 0, sa, gb)
    else:
        db = mdot(g, a, 0, 0, gb, sa) if ca == 1 else mdot(g, a, 0, 1, gb, sa)
    return da, db


mdot.defvjp(_mdot_fwd, _mdot_bwd)


def mask_r(a, m, ca=1, cb=0):
    return mdot(a, m, ca, cb, 3, 1)


def mask_l(m, b, ca=1, cb=0):
    return mdot(m, b, ca, cb, 1, 3)


def ddot(a, b, ca=1, cb=0):
    return mdot(a, b, ca, cb, 2, 2)


def _head_mask(h):
    return ((_iota((1, GROUP), 1) >> 6) == h).astype(F32)


def _block_diag_mask():
    return ((_iota((GROUP, GROUP), 0) >> 6) == (_iota((GROUP, GROUP), 1) >> 6)).astype(F32)


def _expand_mat(offset):
    return ((_iota((128, GROUP), 0) - offset) == (_iota((128, GROUP), 1) >> 6)).astype(F32)


def _tril(n, strict=False):
    r, c = _iota((n, n), 0), _iota((n, n), 1)
    return (r > c) if strict else (r >= c)


def _row_pick(x, row):
    return jnp.sum(jnp.where(_iota(x.shape, 0) == row, x, 0.0), axis=0, keepdims=True)


def _shift_rows_impl(x, halo, j):
    n = x.shape[0]
    r = _iota(x.shape, 0)
    top = jnp.concatenate([pltpu.roll(halo, j, 0), jnp.zeros((n - HALO, x.shape[1]), x.dtype)], axis=0)
    return jnp.where(r >= j, pltpu.roll(x, j, 0), top)


def _mxu_round(a):
    return a.astype(MXU_DTYPE).astype(F32)


@functools.partial(jax.custom_vjp, nondiff_argnums=(3,))
def _causal_conv(x, halo, w, width):
    xb, hb, wb = _mxu_round(x), _mxu_round(halo), _mxu_round(w)
    out = xb * _row_pick(wb, width - 1)
    for j in range(1, width):
        out = out + _shift_rows_impl(xb, hb, j) * _row_pick(wb, width - 1 - j)
    return out


def _causal_conv_fwd(x, halo, w, width):
    return _causal_conv(x, halo, w, width), (x, halo, w)


def _causal_conv_bwd(width, res, g):
    x, halo, w = res
    xb, hb, wb, gb = _mxu_round(x), _mxu_round(halo), _mxu_round(w), _mxu_round(g)
    n = g.shape[0]
    rows, rows8 = _iota(g.shape, 0), _iota(halo.shape, 0)
    dx = gb * _row_pick(wb, width - 1)
    dh = jnp.zeros_like(halo)
    dw = jnp.where(rows8 == width - 1, jnp.sum(xb * gb, axis=0, keepdims=True), 0.0)
    for j in range(1, width):
        gj = gb * _row_pick(wb, width - 1 - j)
        dx = dx + jnp.where(rows < n - j, pltpu.roll(gj, n - j, 0), 0.0)
        dh = dh + jnp.where(rows8 >= HALO - j, pltpu.roll(gj[0:HALO], HALO - j, 0), 0.0)
        tap = jnp.sum(_shift_rows_impl(xb, hb, j) * gb, axis=0, keepdims=True)
        dw = dw + jnp.where(rows8 == width - 1 - j, tap, 0.0)
    return dx, dh, dw


_causal_conv.defvjp(_causal_conv_fwd, _causal_conv_bwd)


def _head_sum(x, bd):
    return mask_r(x, bd)


def _softplus(x):
    return jnp.maximum(x, 0.0) + jnp.log1p(jnp.exp(-jnp.abs(x)))


def _log_sigmoid(x):
    return -_softplus(-x)


def _silu(x):
    return x * jax.nn.sigmoid(x)


def _head_rmsnorm_gate(o, nw, z, bd):
    ms = _head_sum(o * o, bd) * (1.0 / HEAD_DIM)
    return o * lax.rsqrt(ms + EPS) * nw * _silu(z)


def _sgu_chunk(pu, pv, ln_w, ln_b, ws0, ws1, ws2, ws3, bs_t):
    u = jax.nn.gelu(pu)
    g = jax.nn.gelu(pv)
    mu = jnp.mean(g, axis=-1, keepdims=True)
    var = jnp.mean(jnp.square(g - mu), axis=-1, keepdims=True)
    v = (g - mu) * lax.rsqrt(var + EPS) * ln_w + ln_b
    keep = _tril(SGU_CHUNK)
    mixed = mask_r(bs_t, _expand_mat(0))
    for h, ws in enumerate((ws0, ws1, ws2, ws3)):
        mixed = mixed + _head_mask(h) * bdot(jnp.where(keep, ws, 0.0), v, 1, 0)
    return u * mixed


def _sc_chunk(pb, pc, ph, halo_c, halo_h, cw):
    return pb * _causal_conv(pc * ph, halo_c * halo_h, cw, 3)


def _neumann_inverses(lows):
    n = lows[0].shape[0]
    eye = (_iota((n, n), 0) == _iota((n, n), 1)).astype(F32)
    a = [-low for low in lows]
    t = [eye + x for x in a]
    for _ in range(5):
        a = [ddot(x, x) for x in a]
        t = [ti + ddot(ti, ai) for ti, ai in zip(t, a)]
    return t


@jax.custom_vjp
def _saved_inverse(low, inv):
    return inv


def _saved_inverse_fwd(low, inv):
    return inv, inv


def _saved_inverse_bwd(inv, g):
    return -ddot(ddot(inv, g, 0, 0), inv, 1, 1), jnp.zeros_like(inv)


_saved_inverse.defvjp(_saved_inverse_fwd, _saved_inverse_bwd)


def _chunk_tril(rows):
    r, c = _iota((rows, rows), 0), _iota((rows, rows), 1)
    return ((r >> 6) == (c >> 6)) & (r >= c)


def _dn_block(pq, pk, pv, hq, hk, hv, small, pz, cwq, cwk, cwv, a_log, dt_bias, nw, state, saved_inv=None):
    c = SCAN_CHUNK
    rows = pq.shape[0]
    bd = _block_diag_mask()
    q = _silu(_causal_conv(pq, hq, cwq, 4))
    k = _silu(_causal_conv(pk, hk, cwk, 4))
    v = _silu(_causal_conv(pv, hv, cwv, 4))
    q = q * lax.rsqrt(_head_sum(q * q, bd) + EPS) * (HEAD_DIM ** -0.5)
    k = k * lax.rsqrt(_head_sum(k * k, bd) + EPS)
    lane = _iota((1, 128), 1)
    g = jnp.where(lane < HEADS, -jnp.exp(a_log) * _softplus(small + dt_bias), 0.0)
    beta_b = mask_r(jax.nn.sigmoid(small), _expand_mat(HEADS))
    gc_all = mask_l(_chunk_tril(rows).astype(F32), g)
    gcb_all = mask_r(gc_all, _expand_mat(0))
    kb_all = k * beta_b
    vb_all = v * beta_b
    kbe_all = kb_all * jnp.exp(gcb_all)
    qg_all = q * jnp.exp(gcb_all)
    causal, strict = _tril(c), _tril(c, strict=True)
    nc = rows // c
    pairs = [(ci, h) for ci in range(nc) for h in range(HEADS)]
    sls = [slice(ci * c, (ci + 1) * c) for ci in range(nc)]
    decays, lows, attns = [], [], []
    for ci, h in pairs:
        gc = gc_all[sls[ci]]
        onehot = (_iota((c, 128), 1) == h).astype(F32)
        col = mask_l(onehot, gc, 1, 1)
        row = jnp.sum(gc * onehot, axis=1, keepdims=True)
        decays.append(jnp.exp(jnp.where(causal, row - col, -jnp.inf)))
    for j, (ci, h) in enumerate(pairs):
        mh = _head_mask(h)
        k_c = k[sls[ci]]
        lows.append(jnp.where(strict, bdot(kb_all[sls[ci]] * mh, k_c, 1, 1) * decays[j], 0.0))
        attns.append(bdot(q[sls[ci]] * mh, k_c, 1, 1) * decays[j])
    if saved_inv is None:
        invs = _neumann_inverses(lows)
    else:
        invs = [_saved_inverse(low, s) for low, s in zip(lows, saved_inv)]
    us, ws = [], []
    for ci in range(nc):
        u = jnp.zeros((c, GROUP), F32)
        w = jnp.zeros((c, GROUP), F32)
        for h in range(HEADS):
            mh = _head_mask(h)
            u = u + mh * ddot(invs[ci * HEADS + h], vb_all[sls[ci]])
            w = w + mh * ddot(invs[ci * HEADS + h], kbe_all[sls[ci]])
        us.append(u)
        ws.append(w)
    outs = []
    for ci in range(nc):
        gc_b = gcb_all[sls[ci]]
        gc_last_b = _row_pick(gc_b, c - 1)
        v_new = us[ci] - bdot(ws[ci], state, 1, 0)
        o = bdot(qg_all[sls[ci]], state, 1, 0)
        for h in range(HEADS):
            o = o + _head_mask(h) * bdot(attns[ci * HEADS + h], v_new, 1, 0)
        k_dec = k[sls[ci]] * jnp.exp(gc_last_b - gc_b)
        state = state * jnp.exp(gc_last_b) + bd * bdot(k_dec, v_new, 0, 0)
        outs.append(o)
    o = jnp.concatenate(outs, axis=0)
    return _head_rmsnorm_gate(o, nw, pz, bd), state, invs


def _gla_chunk(pq, pk, pv, small, pz, w2, gbias, nw, state_t):
    c = SCAN_CHUNK
    rows = pq.shape[0]
    nc = rows // c
    sls = [slice(ci * c, (ci + 1) * c) for ci in range(nc)]
    bd = _block_diag_mask()
    log_a = _log_sigmoid(bdot(small, w2, 1, 0) + gbias) * (1.0 / 16.0)
    gcum = mask_l(_chunk_tril(rows).astype(F32), log_a)
    r, s = _iota((rows, rows), 0), _iota((rows, rows), 1)
    base = (r >> 6) << 6
    g_mid = mask_l((s == base + c // 2).astype(F32), gcum)
    g_last = mask_l((s == base + c - 1).astype(F32), gcum)
    q = pq * (HEAD_DIM ** -0.5)
    qa = q * jnp.exp(gcum - g_mid)
    ka = pk * jnp.exp(g_mid - gcum)
    qg = q * jnp.exp(gcum)
    k_last = pk * jnp.exp(g_last - gcum)
    causal = _tril(c)
    attns = [jnp.where(causal, bdot(qa[sls[ci]] * _head_mask(h), ka[sls[ci]], 1, 1), 0.0)
             for ci in range(nc) for h in range(HEADS)]
    intra = []
    for ci in range(nc):
        o = jnp.zeros((c, GROUP), F32)
        for h in range(HEADS):
            o = o + _head_mask(h) * bdot(attns[ci * HEADS + h], pv[sls[ci]], 1, 0)
        intra.append(o)
    kvs = [bd * bdot(pv[sls[ci]], k_last[sls[ci]], 0, 0) for ci in range(nc)]
    states = []
    for ci in range(nc):
        states.append(state_t)
        state_t = state_t * jnp.exp(_row_pick(g_last[sls[ci]], 0)) + kvs[ci]
    outs = [intra[ci] + bdot(qg[sls[ci]], states[ci], 1, 1) for ci in range(nc)]
    o = jnp.concatenate(outs, axis=0)
    return _head_rmsnorm_gate(o, nw, pz, bd), state_t


def _col_spec(rows, group, rev_n=None):
    if rev_n is None:
        return pl.BlockSpec((rows, GROUP), lambda i: (i, group))
    return pl.BlockSpec((rows, GROUP), lambda i: (rev_n - 1 - i, group))


def _small_spec(rows, group128, rev_n=None):
    if rev_n is None:
        return pl.BlockSpec((rows, 128), lambda i: (i, group128))
    return pl.BlockSpec((rows, 128), lambda i: (rev_n - 1 - i, group128))


def _halo_spec(rows, group, rev_n=None):
    per = rows // HALO
    if rev_n is None:
        return pl.BlockSpec((HALO, GROUP), lambda i: (jnp.maximum(i * per - 1, 0), group))
    return pl.BlockSpec((HALO, GROUP), lambda i: (jnp.maximum((rev_n - 1 - i) * per - 1, 0), group))


def _full_spec(shape):
    nd = len(shape)
    return pl.BlockSpec(shape, lambda i: (0,) * nd)


def _out_rows_spec(rows, lanes, rev_n=None):
    if rev_n is None:
        return pl.BlockSpec((rows, lanes), lambda i: (i, 0))
    return pl.BlockSpec((rows, lanes), lambda i: (rev_n - 1 - i, 0))


def _sgu_fwd(p, ln_w, ln_b, ws, bs_t):
    t = p.shape[0]
    n = t // SGU_CHUNK

    def body(pu_ref, pv_ref, lw_ref, lb_ref, ws_ref, bs_ref, y_ref):
        y = _sgu_chunk(pu_ref[...], pv_ref[...], lw_ref[...], lb_ref[...],
                       ws_ref[0], ws_ref[1], ws_ref[2], ws_ref[3], bs_ref[...])
        y_ref[...] = y.astype(y_ref.dtype)

    return pl.pallas_call(
        body, name="sgu_fwd", grid=(n,),
        in_specs=[_col_spec(SGU_CHUNK, COL_AU), _col_spec(SGU_CHUNK, COL_AV), _full_spec((1, GROUP)),
                  _full_spec((1, GROUP)), _full_spec((HEADS, SGU_CHUNK, SGU_CHUNK)), _full_spec((SGU_CHUNK, 128))],
        out_specs=_out_rows_spec(SGU_CHUNK, GROUP),
        out_shape=jax.ShapeDtypeStruct((t, GROUP), BF16),
        compiler_params=_cparams(("arbitrary",)),
    )(p, p, ln_w, ln_b, ws, bs_t)


def _sgu_bwd(p, dmix, ln_w, ln_b, ws, bs_t):
    t = p.shape[0]
    n = t // SGU_CHUNK

    def body(pu_ref, pv_ref, dy_ref, lw_ref, lb_ref, ws_ref, bs_ref,
             dpu_ref, dpv_ref, dlw_ref, dlb_ref, dws_ref, dbs_ref):
        args = (pu_ref[...], pv_ref[...], lw_ref[...], lb_ref[...],
                ws_ref[0], ws_ref[1], ws_ref[2], ws_ref[3], bs_ref[...])
        _, vjp = jax.vjp(_sgu_chunk, *args)
        dpu, dpv, dlw, dlb, d0, d1, d2, d3, dbs = vjp(dy_ref[...])
        dpu_ref[...] = dpu.astype(dpu_ref.dtype)
        dpv_ref[...] = dpv.astype(dpv_ref.dtype)

        @pl.when(pl.program_id(0) == 0)
        def _():
            dlw_ref[...] = jnp.zeros_like(dlw_ref)
            dlb_ref[...] = jnp.zeros_like(dlb_ref)
            dws_ref[...] = jnp.zeros_like(dws_ref)
            dbs_ref[...] = jnp.zeros_like(dbs_ref)

        dlw_ref[...] += dlw
        dlb_ref[...] += dlb
        for h, d in enumerate((d0, d1, d2, d3)):
            dws_ref[h] += d
        dbs_ref[...] += dbs

    return pl.pallas_call(
        body, name="sgu_bwd", grid=(n,),
        in_specs=[_col_spec(SGU_CHUNK, COL_AU), _col_spec(SGU_CHUNK, COL_AV),
                  pl.BlockSpec((SGU_CHUNK, GROUP), lambda i: (i, 0)),
                  _full_spec((1, GROUP)), _full_spec((1, GROUP)), _full_spec((HEADS, SGU_CHUNK, SGU_CHUNK)),
                  _full_spec((SGU_CHUNK, 128))],
        out_specs=[_out_rows_spec(SGU_CHUNK, GROUP), _out_rows_spec(SGU_CHUNK, GROUP), _full_spec((1, GROUP)),
                   _full_spec((1, GROUP)), _full_spec((HEADS, SGU_CHUNK, SGU_CHUNK)), _full_spec((SGU_CHUNK, 128))],
        out_shape=[jax.ShapeDtypeStruct((t, GROUP), BF16), jax.ShapeDtypeStruct((t, GROUP), BF16),
                   jax.ShapeDtypeStruct((1, GROUP), F32), jax.ShapeDtypeStruct((1, GROUP), F32),
                   jax.ShapeDtypeStruct((HEADS, SGU_CHUNK, SGU_CHUNK), F32),
                   jax.ShapeDtypeStruct((SGU_CHUNK, 128), F32)],
        compiler_params=_cparams(("arbitrary",)),
    )(p, p, dmix, ln_w, ln_b, ws, bs_t)


SC_ROWS = 256


def _first_block_zero(halo, first):
    return jnp.where(first, 0.0, halo)


def _sc_fwd(p, cw):
    t = p.shape[0]
    n = t // SC_ROWS

    def body(pb_ref, pc_ref, ph_ref, hc_ref, hh_ref, cw_ref, y_ref):
        first = pl.program_id(0) == 0
        y = _sc_chunk(pb_ref[...], pc_ref[...], ph_ref[...], _first_block_zero(hc_ref[...], first),
                      _first_block_zero(hh_ref[...], first), cw_ref[...])
        y_ref[...] = y.astype(y_ref.dtype)

    return pl.pallas_call(
        body, name="sc_fwd", grid=(n,),
        in_specs=[_col_spec(SC_ROWS, COL_BB), _col_spec(SC_ROWS, COL_BC), _col_spec(SC_ROWS, COL_BH),
                  _halo_spec(SC_ROWS, COL_BC), _halo_spec(SC_ROWS, COL_BH), _full_spec((HALO, GROUP))],
        out_specs=_out_rows_spec(SC_ROWS, GROUP),
        out_shape=jax.ShapeDtypeStruct((t, GROUP), BF16),
        compiler_params=_cparams(("arbitrary",)),
    )(p, p, p, p, p, cw)


def _add_halo_grad(d, carry):
    return d + jnp.concatenate([jnp.zeros((d.shape[0] - HALO, d.shape[1]), d.dtype), carry], axis=0)


def _sc_bwd(p, dmix, cw):
    t = p.shape[0]
    n = t // SC_ROWS

    def body(pb_ref, pc_ref, ph_ref, hc_ref, hh_ref, dy_ref, cw_ref,
             dpb_ref, dpc_ref, dph_ref, dcw_ref, carry_c, carry_h):
        i = pl.program_id(0)
        first = i == n - 1

        @pl.when(i == 0)
        def _():
            carry_c[...] = jnp.zeros_like(carry_c)
            carry_h[...] = jnp.zeros_like(carry_h)
            dcw_ref[...] = jnp.zeros_like(dcw_ref)

        args = (pb_ref[...], pc_ref[...], ph_ref[...], _first_block_zero(hc_ref[...], first),
                _first_block_zero(hh_ref[...], first), cw_ref[...])
        _, vjp = jax.vjp(_sc_chunk, *args)
        dpb, dpc, dph, dhc, dhh, dcw = vjp(dy_ref[...])
        dpb_ref[...] = dpb.astype(dpb_ref.dtype)
        dpc_ref[...] = _add_halo_grad(dpc, carry_c[...]).astype(dpc_ref.dtype)
        dph_ref[...] = _add_halo_grad(dph, carry_h[...]).astype(dph_ref.dtype)
        carry_c[...] = dhc
        carry_h[...] = dhh
        dcw_ref[...] += dcw

    return pl.pallas_call(
        body, name="sc_bwd", grid=(n,),
        in_specs=[_col_spec(SC_ROWS, COL_BB, n), _col_spec(SC_ROWS, COL_BC, n), _col_spec(SC_ROWS, COL_BH, n),
                  _halo_spec(SC_ROWS, COL_BC, n), _halo_spec(SC_ROWS, COL_BH, n),
                  pl.BlockSpec((SC_ROWS, GROUP), lambda i: (n - 1 - i, 1)), _full_spec((HALO, GROUP))],
        out_specs=[_out_rows_spec(SC_ROWS, GROUP, n)] * 3 + [_full_spec((HALO, GROUP))],
        out_shape=[jax.ShapeDtypeStruct((t, GROUP), BF16)] * 3 + [jax.ShapeDtypeStruct((HALO, GROUP), F32)],
        scratch_shapes=[pltpu.VMEM((HALO, GROUP), F32), pltpu.VMEM((HALO, GROUP), F32)],
        compiler_params=_cparams(("arbitrary",)),
    )(p, p, p, p, p, dmix, cw)


SCAN_STEP_CHUNKS = 4
SCAN_ROWS = SCAN_STEP_CHUNKS * SCAN_CHUNK


def _host_call(body, hosted, *, name, grid, in_specs, out_specs, out_shape, scratch_shapes, args):
    params = _cparams(("arbitrary",))
    if hosted is None:
        outs = pl.pallas_call(body, name=name, grid=grid, in_specs=in_specs, out_specs=out_specs,
                              out_shape=out_shape, scratch_shapes=scratch_shapes, compiler_params=params)(*args)
        return outs, None
    stage, arrays, modes, bufs = hosted
    arrays = list(arrays) if stage == 1 else []
    n_in, n_out, n_scr, n_src, na = len(in_specs), len(out_specs), len(scratch_shapes), len(arrays), len(bufs)
    last = grid[0] - 1

    def new_body(*refs):
        srcs = refs[n_in:n_in + n_src]
        o0 = n_in + n_src + na
        ex = refs[o0 + n_out:o0 + n_out + na]
        s0 = o0 + n_out + na
        sems = refs[s0 + n_scr:]
        i = pl.program_id(0)

        def run(what):
            if stage == 1:
                _exchange_stage1(srcs, ex, modes, sems[0], sems[1], what)
            else:
                _exchange_stage2(ex, sems[0], sems[1], what)

        @pl.when(i == 0)
        def _():
            run("start")

        body(*refs[:n_in], *refs[o0:o0 + n_out], *refs[s0:s0 + n_scr])

        @pl.when(i == last)
        def _():
            run("wait")

    any_spec = pl.BlockSpec(memory_space=pl.ANY)
    outs = pl.pallas_call(
        new_body, name=name, grid=grid,
        in_specs=list(in_specs) + [any_spec] * (n_src + na), out_specs=list(out_specs) + [any_spec] * na,
        out_shape=list(out_shape) + [jax.ShapeDtypeStruct(b.shape, b.dtype) for b in bufs],
        input_output_aliases={n_in + n_src + a: n_out + a for a in range(na)},
        scratch_shapes=list(scratch_shapes) + (_stage1_sems(na) if stage == 1 else _stage2_sems(na)),
        compiler_params=params,
    )(*args, *arrays, *bufs)
    return outs[:n_out], outs[n_out:]


def _dn_fwd(p, cw3, a_log, dt_bias, nw, hosted=None):
    t = p.shape[0]
    r = SCAN_ROWS
    n = t // r

    def body(pq_ref, pk_ref, pv_ref, hq_ref, hk_ref, hv_ref, sm_ref, pz_ref, cw_ref, al_ref, dt_ref, nw_ref,
             y_ref, ck_ref, inv_ref, state):
        first = pl.program_id(0) == 0

        @pl.when(first)
        def _():
            state[...] = jnp.zeros_like(state)

        s_in = state[...]
        ck_ref[0] = s_in
        y, s_out, invs = _dn_block(pq_ref[...], pk_ref[...], pv_ref[...], _first_block_zero(hq_ref[...], first),
                                   _first_block_zero(hk_ref[...], first), _first_block_zero(hv_ref[...], first),
                                   sm_ref[...], pz_ref[...], cw_ref[0], cw_ref[1], cw_ref[2],
                                   al_ref[...], dt_ref[...], nw_ref[...], s_in)
        y_ref[...] = y.astype(y_ref.dtype)
        state[...] = s_out
        for j, inv in enumerate(invs):
            inv_ref[j] = inv

    nh = SCAN_STEP_CHUNKS * HEADS
    return _host_call(
        body, hosted, name="dn_fwd", grid=(n,),
        in_specs=[_col_spec(r, COL_CQ), _col_spec(r, COL_CK), _col_spec(r, COL_CV),
                  _halo_spec(r, COL_CQ), _halo_spec(r, COL_CK), _halo_spec(r, COL_CV),
                  _small_spec(r, COL128_SMALL_C), _col_spec(r, COL_CZ), _full_spec((3, HALO, GROUP)),
                  _full_spec((1, 128)), _full_spec((1, 128)), _full_spec((1, GROUP))],
        out_specs=[_out_rows_spec(r, GROUP), pl.BlockSpec((1, GROUP, GROUP), lambda i: (i, 0, 0)),
                   pl.BlockSpec((nh, SCAN_CHUNK, SCAN_CHUNK), lambda i: (i, 0, 0))],
        out_shape=[jax.ShapeDtypeStruct((t, GROUP), BF16), jax.ShapeDtypeStruct((n, GROUP, GROUP), F32),
                   jax.ShapeDtypeStruct((n * nh, SCAN_CHUNK, SCAN_CHUNK), F32)],
        scratch_shapes=[pltpu.VMEM((GROUP, GROUP), F32)],
        args=(p, p, p, p, p, p, p, p, cw3, a_log, dt_bias, nw))


def _dn_bwd(p, dmix, states, invs, cw3, a_log, dt_bias, nw, hosted=None):
    t = p.shape[0]
    c = SCAN_ROWS
    n = t // c
    nh = SCAN_STEP_CHUNKS * HEADS

    def body(pq_ref, pk_ref, pv_ref, hq_ref, hk_ref, hv_ref, sm_ref, pz_ref, dy_ref, ck_ref, inv_ref,
             cw_ref, al_ref, dt_ref, nw_ref,
             dpq_ref, dpk_ref, dpv_ref, dsm_ref, dpz_ref, dcw_ref, dal_ref, ddt_ref, dnw_ref,
             dstate, carry):
        i = pl.program_id(0)
        first = i == n - 1

        @pl.when(i == 0)
        def _():
            dstate[...] = jnp.zeros_like(dstate)
            carry[...] = jnp.zeros_like(carry)
            dcw_ref[...] = jnp.zeros_like(dcw_ref)
            dal_ref[...] = jnp.zeros_like(dal_ref)
            ddt_ref[...] = jnp.zeros_like(ddt_ref)
            dnw_ref[...] = jnp.zeros_like(dnw_ref)

        args = (pq_ref[...], pk_ref[...], pv_ref[...], _first_block_zero(hq_ref[...], first),
                _first_block_zero(hk_ref[...], first), _first_block_zero(hv_ref[...], first),
                sm_ref[...], pz_ref[...], cw_ref[0], cw_ref[1], cw_ref[2],
                al_ref[...], dt_ref[...], nw_ref[...], ck_ref[0])
        saved = [inv_ref[j] for j in range(nh)]
        _, vjp = jax.vjp(lambda *a: _dn_block(*a, saved_inv=saved)[:2], *args)
        (dpq, dpk, dpv, dhq, dhk, dhv, dsm, dpz, dcq, dck, dcv, dal, ddt, dnw, dst) = vjp(
            (dy_ref[...], dstate[...]))
        dpq_ref[...] = _add_halo_grad(dpq, carry[0]).astype(dpq_ref.dtype)
        dpk_ref[...] = _add_halo_grad(dpk, carry[1]).astype(dpk_ref.dtype)
        dpv_ref[...] = _add_halo_grad(dpv, carry[2]).astype(dpv_ref.dtype)
        dsm_ref[...] = dsm.astype(dsm_ref.dtype)
        dpz_ref[...] = dpz.astype(dpz_ref.dtype)
        carry[0] = dhq
        carry[1] = dhk
        carry[2] = dhv
        dstate[...] = dst
        dcw_ref[0] += dcq
        dcw_ref[1] += dck
        dcw_ref[2] += dcv
        dal_ref[...] += dal
        ddt_ref[...] += ddt
        dnw_ref[...] += dnw

    return _host_call(
        body, hosted, name="dn_bwd", grid=(n,),
        in_specs=[_col_spec(c, COL_CQ, n), _col_spec(c, COL_CK, n), _col_spec(c, COL_CV, n),
                  _halo_spec(c, COL_CQ, n), _halo_spec(c, COL_CK, n), _halo_spec(c, COL_CV, n),
                  _small_spec(c, COL128_SMALL_C, n), _col_spec(c, COL_CZ, n),
                  pl.BlockSpec((c, GROUP), lambda i: (n - 1 - i, 2)),
                  pl.BlockSpec((1, GROUP, GROUP), lambda i: (n - 1 - i, 0, 0)),
                  pl.BlockSpec((nh, SCAN_CHUNK, SCAN_CHUNK), lambda i: (n - 1 - i, 0, 0)),
                  _full_spec((3, HALO, GROUP)), _full_spec((1, 128)), _full_spec((1, 128)), _full_spec((1, GROUP))],
        out_specs=[_out_rows_spec(c, GROUP, n)] * 3 + [_out_rows_spec(c, 128, n), _out_rows_spec(c, GROUP, n),
                   _full_spec((3, HALO, GROUP)), _full_spec((1, 128)), _full_spec((1, 128)), _full_spec((1, GROUP))],
        out_shape=[jax.ShapeDtypeStruct((t, GROUP), BF16)] * 3 + [
            jax.ShapeDtypeStruct((t, 128), BF16), jax.ShapeDtypeStruct((t, GROUP), BF16),
            jax.ShapeDtypeStruct((3, HALO, GROUP), F32), jax.ShapeDtypeStruct((1, 128), F32),
            jax.ShapeDtypeStruct((1, 128), F32), jax.ShapeDtypeStruct((1, GROUP), F32)],
        scratch_shapes=[pltpu.VMEM((GROUP, GROUP), F32), pltpu.VMEM((3, HALO, GROUP), F32)],
        args=(p, p, p, p, p, p, p, p, dmix, states, invs, cw3, a_log, dt_bias, nw))


def _gla_fwd(p, w2, gbias, nw, hosted=None):
    t = p.shape[0]
    c = SCAN_ROWS
    n = t // c

    def body(pq_ref, pk_ref, pv_ref, sm_ref, pz_ref, w2_ref, gb_ref, nw_ref, y_ref, ck_ref, state):
        @pl.when(pl.program_id(0) == 0)
        def _():
            state[...] = jnp.zeros_like(state)

        s_in = state[...]
        ck_ref[0] = s_in
        y, s_out = _gla_chunk(pq_ref[...], pk_ref[...], pv_ref[...], sm_ref[...], pz_ref[...],
                              w2_ref[...], gb_ref[...], nw_ref[...], s_in)
        y_ref[...] = y.astype(y_ref.dtype)
        state[...] = s_out

    return _host_call(
        body, hosted, name="gla_fwd", grid=(n,),
        in_specs=[_col_spec(c, COL_DQ), _col_spec(c, COL_DK), _col_spec(c, COL_DV),
                  _small_spec(c, COL128_SMALL_D), _col_spec(c, COL_DZ),
                  _full_spec((128, GROUP)), _full_spec((1, GROUP)), _full_spec((1, GROUP))],
        out_specs=[_out_rows_spec(c, GROUP), pl.BlockSpec((1, GROUP, GROUP), lambda i: (i, 0, 0))],
        out_shape=[jax.ShapeDtypeStruct((t, GROUP), BF16), jax.ShapeDtypeStruct((n, GROUP, GROUP), F32)],
        scratch_shapes=[pltpu.VMEM((GROUP, GROUP), F32)],
        args=(p, p, p, p, p, w2, gbias, nw))


def _gla_bwd(p, dmix, states, w2, gbias, nw, hosted=None):
    t = p.shape[0]
    c = SCAN_ROWS
    n = t // c

    def body(pq_ref, pk_ref, pv_ref, sm_ref, pz_ref, dy_ref, ck_ref, w2_ref, gb_ref, nw_ref,
             dpq_ref, dpk_ref, dpv_ref, dsm_ref, dpz_ref, dw2_ref, dgb_ref, dnw_ref, dstate):
        @pl.when(pl.program_id(0) == 0)
        def _():
            dstate[...] = jnp.zeros_like(dstate)
            dw2_ref[...] = jnp.zeros_like(dw2_ref)
            dgb_ref[...] = jnp.zeros_like(dgb_ref)
            dnw_ref[...] = jnp.zeros_like(dnw_ref)

        args = (pq_ref[...], pk_ref[...], pv_ref[...], sm_ref[...], pz_ref[...],
                w2_ref[...], gb_ref[...], nw_ref[...], ck_ref[0])
        _, vjp = jax.vjp(_gla_chunk, *args)
        dpq, dpk, dpv, dsm, dpz, dw2, dgb, dnw, dst = vjp((dy_ref[...], dstate[...]))
        dpq_ref[...] = dpq.astype(dpq_ref.dtype)
        dpk_ref[...] = dpk.astype(dpk_ref.dtype)
        dpv_ref[...] = dpv.astype(dpv_ref.dtype)
        dsm_ref[...] = dsm.astype(dsm_ref.dtype)
        dpz_ref[...] = dpz.astype(dpz_ref.dtype)
        dstate[...] = dst
        dw2_ref[...] += dw2
        dgb_ref[...] += dgb
        dnw_ref[...] += dnw

    return _host_call(
        body, hosted, name="gla_bwd", grid=(n,),
        in_specs=[_col_spec(c, COL_DQ, n), _col_spec(c, COL_DK, n), _col_spec(c, COL_DV, n),
                  _small_spec(c, COL128_SMALL_D, n), _col_spec(c, COL_DZ, n),
                  pl.BlockSpec((c, GROUP), lambda i: (n - 1 - i, 3)),
                  pl.BlockSpec((1, GROUP, GROUP), lambda i: (n - 1 - i, 0, 0)),
                  _full_spec((128, GROUP)), _full_spec((1, GROUP)), _full_spec((1, GROUP))],
        out_specs=[_out_rows_spec(c, GROUP, n)] * 3 + [_out_rows_spec(c, 128, n), _out_rows_spec(c, GROUP, n),
                   _full_spec((128, GROUP)), _full_spec((1, GROUP)), _full_spec((1, GROUP))],
        out_shape=[jax.ShapeDtypeStruct((t, GROUP), BF16)] * 3 + [
            jax.ShapeDtypeStruct((t, 128), BF16), jax.ShapeDtypeStruct((t, GROUP), BF16),
            jax.ShapeDtypeStruct((128, GROUP), F32), jax.ShapeDtypeStruct((1, GROUP), F32),
            jax.ShapeDtypeStruct((1, GROUP), F32)],
        scratch_shapes=[pltpu.VMEM((GROUP, GROUP), F32)],
        args=(p, p, p, p, p, dmix, states, w2, gbias, nw))


def _pick_tile(n, pref):
    for cand in pref:
        if n % cand == 0:
            return cand
    return n


MM_TILE_CAP = 1408


def _largest_tile(n, cap):
    best = None
    for mult in range(1, cap // 128 + 1):
        if n % (128 * mult) == 0:
            best = 128 * mult
    return best if best is not None else n


def _half_index(t, per_half, middle):
    half = jnp.where(t >= per_half, 1, 0)
    return half, middle, t - half * per_half


def _matmul(a, b, mode, out_dtype, name, res=None):
    if mode == "nn":
        (m, k), n = a.shape, b.shape[1]
    elif mode == "nt":
        (m, k), n = a.shape, b.shape[0]
    else:
        (k, m), n = a.shape, b.shape[-1] * (2 if b.ndim == 3 else 1)
    tm = _largest_tile(m, MM_TILE_CAP)
    tn = _largest_tile(b.shape[-1] if b.ndim == 3 else n, MM_TILE_CAP)
    tk = _largest_tile(k, MM_TILE_CAP)
    nk = k // tk
    if mode == "nn":
        a_spec = pl.BlockSpec((tm, tk), lambda i, j, kk: (i, kk))
        b_spec = pl.BlockSpec((tk, tn), lambda i, j, kk: (kk, j))
        dims = (1, 0)
    elif mode == "nt":
        a_spec = pl.BlockSpec((tm, tk), lambda i, j, kk: (i, kk))
        b_spec = pl.BlockSpec((tn, tk), lambda i, j, kk: (j, kk))
        dims = (1, 1)
    else:
        a_spec = pl.BlockSpec((tk, tm), lambda i, j, kk: (kk, i))
        if b.ndim == 3:
            njh = b.shape[-1] // tn
            b_spec = pl.BlockSpec((None, tk, tn), lambda i, j, kk: _half_index(j, njh, kk))
        else:
            b_spec = pl.BlockSpec((tk, tn), lambda i, j, kk: (kk, j))
        dims = (0, 0)
    o_spec = pl.BlockSpec((tm, tn), lambda i, j, kk: (i, j))
    has_res = res is not None

    def body(*refs):
        a_ref, b_ref = refs[:2]
        r_ref = refs[2] if has_res else None
        o_ref = refs[3] if has_res else refs[2]
        part = _dg(a_ref[...].astype(MXU_DTYPE), b_ref[...].astype(MXU_DTYPE), *dims)

        def finish(out):
            if has_res:
                out = out + r_ref[...]
            o_ref[...] = out.astype(o_ref.dtype)

        if nk == 1:
            finish(part)
            return
        acc = refs[-1]
        kk = pl.program_id(2)

        @pl.when(kk == 0)
        def _():
            acc[...] = part

        @pl.when(kk > 0)
        def _():
            acc[...] += part

        @pl.when(kk == nk - 1)
        def _():
            finish(acc[...])

    in_specs = [a_spec, b_spec] + ([o_spec] if has_res else [])
    args = (a, b) + ((res,) if has_res else ())
    return pl.pallas_call(
        body, name=name, grid=(m // tm, n // tn, nk), in_specs=in_specs, out_specs=o_spec,
        out_shape=jax.ShapeDtypeStruct((m, n), out_dtype),
        scratch_shapes=[pltpu.VMEM((tm, tn), F32)] if nk > 1 else [],
        compiler_params=_cparams(("parallel", "parallel", "arbitrary")),
    )(*args)


def _matmul_nt_norm_bwd(a, b, x, w, dres, name):
    n = b.shape[0]
    m = a.shape[-2]
    tm = _largest_tile(m, 1024)
    if a.ndim == 3:
        kh = a.shape[2]
        k = 2 * kh
        tk = _largest_tile(kh, MM_TILE_CAP)
        nkh = kh // tk
        a_spec = pl.BlockSpec((None, tm, tk), lambda i, kk: _half_index(kk, nkh, i))
    else:
        k = a.shape[1]
        tk = _largest_tile(k, MM_TILE_CAP)
        a_spec = pl.BlockSpec((tm, tk), lambda i, kk: (i, kk))
    nk = k // tk

    def body(a_ref, b_ref, x_ref, w_ref, r_ref, dx_ref, dw_ref, acc):
        i, kk = pl.program_id(0), pl.program_id(1)
        part = _dg(a_ref[...].astype(MXU_DTYPE), b_ref[...].astype(MXU_DTYPE), 1, 1)

        @pl.when(kk == 0)
        def _():
            acc[...] = part

        @pl.when(kk > 0)
        def _():
            acc[...] += part

        @pl.when((i == 0) & (kk == 0))
        def _():
            dw_ref[...] = jnp.zeros_like(dw_ref)

        @pl.when(kk == nk - 1)
        def _():
            g = acc[...]
            xv = x_ref[...]
            r = lax.rsqrt(jnp.mean(xv * xv, axis=-1, keepdims=True) + EPS)
            xhat = xv * r
            dw_ref[...] += jnp.sum(g * xhat, axis=0, keepdims=True)
            gx = g * w_ref[...]
            dx_ref[...] = r_ref[...] + r * (gx - xhat * jnp.mean(gx * xhat, axis=-1, keepdims=True))

    row_spec = pl.BlockSpec((tm, n), lambda i, kk: (i, 0))
    return pl.pallas_call(
        body, name=name, grid=(m // tm, nk),
        in_specs=[a_spec, pl.BlockSpec((n, tk), lambda i, kk: (0, kk)),
                  row_spec, pl.BlockSpec((1, n), lambda i, kk: (0, 0)), row_spec],
        out_specs=[row_spec, pl.BlockSpec((1, n), lambda i, kk: (0, 0))],
        out_shape=[jax.ShapeDtypeStruct((m, n), F32), jax.ShapeDtypeStruct((1, n), F32)],
        scratch_shapes=[pltpu.VMEM((tm, n), F32)],
        compiler_params=_cparams(("arbitrary", "arbitrary")),
    )(a, b, x, w, dres)


NORM_ROWS = 512


def _rmsnorm_fwd(x, w, name):
    t, d = x.shape

    def body(x_ref, w_ref, o_ref):
        xv = x_ref[...]
        r = lax.rsqrt(jnp.mean(xv * xv, axis=-1, keepdims=True) + EPS)
        o_ref[...] = (xv * r * w_ref[...]).astype(o_ref.dtype)

    return pl.pallas_call(
        body, name=name, grid=(t // NORM_ROWS,),
        in_specs=[pl.BlockSpec((NORM_ROWS, d), lambda i: (i, 0)), _full_spec((1, d))],
        out_specs=pl.BlockSpec((NORM_ROWS, d), lambda i: (i, 0)),
        out_shape=jax.ShapeDtypeStruct((t, d), BF16),
        compiler_params=_cparams(("parallel",)),
    )(x, w)


SWIGLU_ROWS = 128


def _ffn_up_swiglu(h, w_gate_up):
    m, k = h.shape
    tm = _largest_tile(m, 512)
    tn = _largest_tile(D_FF, MM_TILE_CAP)
    nj = D_FF // tn

    def body(a_ref, bg_ref, bu_ref, g_ref, u_ref, act_ref):
        a = a_ref[...].astype(MXU_DTYPE)
        gate = _dg(a, bg_ref[...].astype(MXU_DTYPE), 1, 0)
        up = _dg(a, bu_ref[...].astype(MXU_DTYPE), 1, 0)
        g_ref[...] = gate
        u_ref[...] = up
        act_ref[...] = (_silu(gate) * up).astype(act_ref.dtype)

    o_spec = pl.BlockSpec((tm, tn), lambda i, j: (i, j))
    return pl.pallas_call(
        body, name="ffn_up", grid=(m // tm, nj),
        in_specs=[pl.BlockSpec((tm, k), lambda i, j: (i, 0)), pl.BlockSpec((k, tn), lambda i, j: (0, j)),
                  pl.BlockSpec((k, tn), lambda i, j: (0, j + nj))],
        out_specs=[o_spec, o_spec, o_spec],
        out_shape=[jax.ShapeDtypeStruct((m, D_FF), F32), jax.ShapeDtypeStruct((m, D_FF), F32),
                   jax.ShapeDtypeStruct((m, D_FF), BF16)],
        compiler_params=_cparams(("parallel", "parallel")),
    )(h, w_gate_up, w_gate_up)


def _ffn_down_dx_swiglu(dx, w_down, gate, up):
    m, k = dx.shape
    tm = _largest_tile(m, 512)
    tn = _largest_tile(D_FF, MM_TILE_CAP)

    def body(a_ref, b_ref, g_ref, u_ref, o_ref):
        da = _dg(a_ref[...].astype(MXU_DTYPE), b_ref[...].astype(MXU_DTYPE), 1, 1)
        gate = g_ref[...]
        sg = jax.nn.sigmoid(gate)
        o_ref[0] = (da * u_ref[...] * (sg * (1.0 + gate * (1.0 - sg)))).astype(o_ref.dtype)
        o_ref[1] = (da * gate * sg).astype(o_ref.dtype)

    tile = pl.BlockSpec((tm, tn), lambda i, j: (i, j))
    return pl.pallas_call(
        body, name="ffn_down_dx", grid=(m // tm, D_FF // tn),
        in_specs=[pl.BlockSpec((tm, k), lambda i, j: (i, 0)), pl.BlockSpec((tn, k), lambda i, j: (j, 0)), tile, tile],
        out_specs=pl.BlockSpec((2, tm, tn), lambda i, j: (0, i, j)),
        out_shape=jax.ShapeDtypeStruct((2, m, D_FF), MXU_DTYPE),
        compiler_params=_cparams(("parallel", "parallel")),
    )(dx, w_down, gate, up)


def _loss_head(x, w, target):
    t, d = x.shape

    def fwd(xv, wv, tv):
        r = lax.rsqrt(jnp.mean(xv * xv, axis=-1, keepdims=True) + EPS)
        err = xv * r * wv - tv
        return 0.5 * jnp.sum(jnp.mean(err * err, axis=-1, keepdims=True), axis=0, keepdims=True)

    def body(x_ref, w_ref, t_ref, dx_ref, dw_ref, loss_ref):
        @pl.when(pl.program_id(0) == 0)
        def _():
            dw_ref[...] = jnp.zeros_like(dw_ref)
            loss_ref[...] = jnp.zeros_like(loss_ref)

        loss, vjp = jax.vjp(fwd, x_ref[...], w_ref[...], t_ref[...])
        dx, dw, _ = vjp(jnp.ones((1, 1), F32))
        dx_ref[...] = dx
        dw_ref[...] += dw
        loss_ref[...] += jnp.broadcast_to(loss, loss_ref.shape)

    return pl.pallas_call(
        body, name="loss_head", grid=(t // NORM_ROWS,),
        in_specs=[pl.BlockSpec((NORM_ROWS, d), lambda i: (i, 0)), _full_spec((1, d)),
                  pl.BlockSpec((NORM_ROWS, d), lambda i: (i, 0))],
        out_specs=[pl.BlockSpec((NORM_ROWS, d), lambda i: (i, 0)), _full_spec((1, d)), _full_spec((8, 128))],
        out_shape=[jax.ShapeDtypeStruct((t, d), F32), jax.ShapeDtypeStruct((1, d), F32),
                   jax.ShapeDtypeStruct((8, 128), F32)],
        compiler_params=_cparams(("arbitrary",)),
    )(x, w, target)


def _pad_w_in(w):
    z = lambda n: jnp.zeros((w.shape[0], n), w.dtype)
    return jnp.concatenate([w[:, 0:2048], w[:, 2056:2312], w[:, 2312:3080], w[:, 3096:3352],
                            w[:, 2048:2056], z(120), w[:, 3080:3096], z(112)], axis=1)


def _unpad_w_in(wp):
    return jnp.concatenate([wp[:, 0:2048], wp[:, 3328:3336], wp[:, 2048:2304], wp[:, 2304:3072],
                            wp[:, 3456:3472], wp[:, 3072:3328]], axis=1)


def _pad_rows(a, rows):
    return jnp.concatenate([a, jnp.zeros((rows - a.shape[0],) + a.shape[1:], a.dtype)], axis=0)


def _pad_lanes(a, lanes):
    return jnp.concatenate([a, jnp.zeros(a.shape[:-1] + (lanes - a.shape[-1],), a.dtype)], axis=-1)


def _layer_params(l, small):
    dn_cw = small["dn_conv_w"][l]
    return dict(
        ln_w=small["sgu_ln_w"][l][None], ln_b=small["sgu_ln_b"][l][None],
        ws=small["sgu_w_spatial"][l], bs_t=_pad_lanes(small["sgu_b_spatial"][l].T, 128),
        sc_cw=_pad_rows(small["sc_conv_w"][l], HALO),
        dn_cw=jnp.stack([_pad_rows(dn_cw[:, j * GROUP:(j + 1) * GROUP], HALO) for j in range(3)]),
        dn_al=_pad_lanes(small["dn_a_log"][l][None], 128), dn_dt=_pad_lanes(small["dn_dt_bias"][l][None], 128),
        dn_nw=jnp.tile(small["dn_norm_w"][l][None], (1, HEADS)),
        gla_w2=_pad_rows(small["gla_w_gate2"][l], 128), gla_gb=small["gla_gate_bias"][l][None],
        gla_nw=jnp.tile(small["gla_norm_w"][l][None], (1, HEADS)),
    )


def _layer_pieces(gbig, l):
    def cols(full):
        r, c4 = full.shape
        return jnp.transpose(full.reshape(2, r // 2, N_CHIPS, c4 // N_CHIPS), (0, 2, 1, 3))

    def rows(full):
        r4, c = full.shape
        return jnp.transpose(full.reshape(N_CHIPS, 2, r4 // (2 * N_CHIPS), c), (1, 0, 2, 3))

    return [cols(_unpad_w_in(gbig["w_in"][l])), rows(gbig["w_out"][l]), cols(gbig["w_gate_up"][l]),
            rows(gbig["w_down"][l])]


def _local_step(x, target, big, small, late_weights=None, exchange=False, small_extra=None):
    saved = []
    h = x
    for l in range(DEPTH):
        lp = _layer_params(l, small)
        h1 = _rmsnorm_fwd(h, small["norm1_w"][l][None], "norm1_fwd")
        p = _matmul(h1, big["w_in"][l], "nn", F32, "proj_in")
        y_a = _sgu_fwd(p, lp["ln_w"], lp["ln_b"], lp["ws"], lp["bs_t"])
        y_b = _sc_fwd(p, lp["sc_cw"])
        host1 = host2 = None
        if l == 0 and late_weights is not None:
            shards, finish = late_weights
            modes = ["layer"] * len(shards)
            host1 = (1, shards, modes, _exchange_buffers(shards, modes))
        (y_c, st_c, inv_c), ex = _dn_fwd(p, lp["dn_cw"], lp["dn_al"], lp["dn_dt"], lp["dn_nw"], hosted=host1)
        if host1 is not None:
            host2 = (2, None, None, ex)
        (y_d, st_d), ex = _gla_fwd(p, lp["gla_w2"], lp["gla_gb"], lp["gla_nw"], hosted=host2)
        if host2 is not None:
            big = {**big, **finish(ex)}
        mix = jnp.concatenate([y_a, y_b, y_c, y_d], axis=1)
        x1 = _matmul(mix, big["w_out"][l], "nn", F32, "proj_out", res=h)
        h2 = _rmsnorm_fwd(x1, small["norm2_w"][l][None], "norm2_fwd")
        gate, up, act = _ffn_up_swiglu(h2, big["w_gate_up"][l])
        x2 = _matmul(act, big["w_down"][l], "nn", F32, "ffn_down", res=x1)
        saved.append(dict(x0=h, h1=h1, p=p, st_c=st_c, inv_c=inv_c, st_d=st_d, mix=mix, x1=x1, h2=h2, gate=gate, up=up, act=act, lp=lp))
        h = x2

    dx, d_final, loss = _loss_head(h, small["final_norm_w"][None], target)
    gbig = {k: [None] * DEPTH for k in ("w_in", "w_out", "w_gate_up", "w_down")}
    gs = {k: [None] * DEPTH for k in ("norm1_w", "sgu_ln_w", "sgu_ln_b", "sgu_w_spatial", "sgu_b_spatial", "sc_conv_w",
                                     "dn_conv_w", "dn_a_log", "dn_dt_bias", "dn_norm_w", "gla_w_gate2",
                                     "gla_gate_bias", "gla_norm_w", "norm2_w")}
    pending = None
    contribs = [None] * DEPTH
    for l in reversed(range(DEPTH)):
        s = saved[l]
        lp = s["lp"]
        gbig["w_down"][l] = _matmul(s["act"], dx, "tn", GRAD_WIRE_DTYPE, "ffn_down_dw")
        dgu = _ffn_down_dx_swiglu(dx, big["w_down"][l], s["gate"], s["up"])
        gbig["w_gate_up"][l] = _matmul(s["h2"], dgu, "tn", GRAD_WIRE_DTYPE, "ffn_up_dw")
        dx1, gs["norm2_w"][l] = _matmul_nt_norm_bwd(dgu, big["w_gate_up"][l], s["x1"], small["norm2_w"][l][None], dx,
                                                    "ffn_up_dx")
        gbig["w_out"][l] = _matmul(s["mix"], dx1, "tn", GRAD_WIRE_DTYPE, "proj_out_dw")
        dmix = _matmul(dx1, big["w_out"][l], "nt", F32, "proj_out_dx")
        p = s["p"]
        dpu, dpv, g_lw, g_lb, g_ws, g_bs = _sgu_bwd(p, dmix, lp["ln_w"], lp["ln_b"], lp["ws"], lp["bs_t"])
        dpb, dpc, dph, g_sc = _sc_bwd(p, dmix, lp["sc_cw"])
        host1 = host2 = None
        if pending is not None:
            modes = ["piece"] * len(pending)
            host1 = (1, pending, modes, _exchange_buffers(pending, modes))
        (dcq, dck, dcv, dcs, dcz, g_dcw, g_al, g_dt, g_dnw), ex = _dn_bwd(
            p, dmix, s["st_c"], s["inv_c"], lp["dn_cw"], lp["dn_al"], lp["dn_dt"], lp["dn_nw"], hosted=host1)
        if host1 is not None:
            host2 = (2, None, None, ex)
        (ddq, ddk, ddv, dds, ddz, g_w2, g_gb, g_gnw), ex = _gla_bwd(p, dmix, s["st_d"], lp["gla_w2"], lp["gla_gb"],
                                                                   lp["gla_nw"], hosted=host2)
        if host2 is not None:
            contribs[l + 1] = ex
            pending = None
        dp = jnp.concatenate([dpu, dpv, dpb, dpc, dph, dcq, dck, dcv, dcz, ddq, ddk, ddv, ddz, dcs, dds], axis=1)
        gbig["w_in"][l] = _matmul(s["h1"], dp, "tn", GRAD_WIRE_DTYPE, "proj_in_dw")
        dx, gs["norm1_w"][l] = _matmul_nt_norm_bwd(dp, big["w_in"][l], s["x0"], small["norm1_w"][l][None], dx1,
                                                   "proj_in_dx")
        gs["sgu_ln_w"][l], gs["sgu_ln_b"][l] = g_lw[0], g_lb[0]
        gs["sgu_w_spatial"][l] = g_ws
        gs["sgu_b_spatial"][l] = g_bs[:, :HEADS].T
        gs["sc_conv_w"][l] = g_sc[:3]
        gs["dn_conv_w"][l] = jnp.concatenate([g_dcw[0, :4], g_dcw[1, :4], g_dcw[2, :4]], axis=1)
        gs["dn_a_log"][l], gs["dn_dt_bias"][l] = g_al[0, :HEADS], g_dt[0, :HEADS]
        gs["dn_norm_w"][l] = jnp.sum(g_dnw.reshape(HEADS, HEAD_DIM), axis=0)
        gs["gla_w_gate2"][l] = g_w2[:16]
        gs["gla_gate_bias"][l] = g_gb[0]
        gs["gla_norm_w"][l] = jnp.sum(g_gnw.reshape(HEADS, HEAD_DIM), axis=0)
        gs["norm1_w"][l] = gs["norm1_w"][l][0]
        gs["norm2_w"][l] = gs["norm2_w"][l][0]
        if exchange:
            pieces = _layer_pieces(gbig, l)
            theirs = _sibling_swap(pieces)
            pending = [_pair_add(a, b, "pair_add_" + n) for a, b, n in zip(pieces, theirs, BIG_NAMES)]
    gsmall = {k: jnp.stack(v) for k, v in gs.items()}
    gsmall["final_norm_w"] = d_final[0]
    if not exchange:
        return loss, dx, gbig, gsmall
    vec = small_extra(gsmall, loss)
    last = _chip_exchange(pending + [vec], ["piece"] * len(pending) + ["whole"], "exchange_grads")
    contribs[0] = last[:-1]
    return loss, dx, contribs, last[-1]


def _peer_chips(x, y):
    return [(1 - x, y, 2 * (1 - x) + y), (x, 1 - y, 2 * x + 1 - y), (1 - x, 1 - y, 2 * (1 - x) + 1 - y)]


def _chip_exchange(arrays, modes, name):
    na = len(arrays)
    bufs = _exchange_buffers(arrays, modes)

    def body(*refs):
        ins, outs = refs[:na], refs[2 * na:3 * na]
        send1, recv1, send2, recv2 = refs[3 * na:]
        _exchange_stage1(ins, outs, modes, send1, recv1, "start")
        _exchange_stage1(ins, outs, modes, send1, recv1, "wait")
        _exchange_stage2(outs, send2, recv2, "start")
        _exchange_stage2(outs, send2, recv2, "wait")

    any_spec = pl.BlockSpec(memory_space=pl.ANY)
    return pl.pallas_call(
        body, name=name,
        in_specs=[any_spec] * (2 * na), out_specs=[any_spec] * na,
        out_shape=[jax.ShapeDtypeStruct(b.shape, b.dtype) for b in bufs],
        input_output_aliases={na + a: a for a in range(na)},
        scratch_shapes=_stage1_sems(na) + _stage2_sems(na),
    )(*arrays, *bufs)


def _exchange_buffers(arrays, modes):
    c_idx = lax.axis_index("c")
    chip = 2 * lax.axis_index("x") + lax.axis_index("y")
    bufs = []
    for arr, md in zip(arrays, modes):
        if md == "layer":
            unit = lax.dynamic_index_in_dim(arr, c_idx, 0, keepdims=False)
        elif md == "piece":
            unit = lax.dynamic_index_in_dim(arr, chip, 0, keepdims=False)
        else:
            unit = arr
        buf = lax.empty((2, N_CHIPS) + unit.shape, unit.dtype)
        bufs.append(lax.dynamic_update_slice(buf, unit[None, None], (c_idx, chip) + (0,) * unit.ndim))
    return bufs


def _stage1_sems(na):
    return [pltpu.SemaphoreType.DMA((na, 3)), pltpu.SemaphoreType.DMA((na, 3))]


def _stage2_sems(na):
    return [pltpu.SemaphoreType.DMA((na,)), pltpu.SemaphoreType.DMA((na,))]


def _exchange_stage1(ins, outs, modes, send1, recv1, what):
    x, y, c = lax.axis_index("x"), lax.axis_index("y"), lax.axis_index("c")
    me = 2 * x + y
    for a in range(len(ins)):
        for k, (px, py, pidx) in enumerate(_peer_chips(x, y)):
            if modes[a] == "layer":
                src = ins[a].at[c]
            else:
                src = ins[a].at[pidx] if modes[a] == "piece" else ins[a]
            if what == "start":
                pltpu.make_async_remote_copy(
                    src_ref=src, dst_ref=outs[a].at[c, me], send_sem=send1.at[a, k], recv_sem=recv1.at[a, k],
                    device_id=(px, py, c), device_id_type=MESH).start()
            else:
                cp = pltpu.make_async_remote_copy(
                    src_ref=src, dst_ref=outs[a].at[c, pidx], send_sem=send1.at[a, k], recv_sem=recv1.at[a, k],
                    device_id=(px, py, c), device_id_type=MESH)
                cp.wait_send()
                cp.wait_recv()


def _exchange_stage2(outs, send2, recv2, what):
    x, y, c = lax.axis_index("x"), lax.axis_index("y"), lax.axis_index("c")
    sibling = (x, y, 1 - c)
    for a in range(len(outs)):
        if what == "start":
            pltpu.make_async_remote_copy(
                src_ref=outs[a].at[c], dst_ref=outs[a].at[c], send_sem=send2.at[a], recv_sem=recv2.at[a],
                device_id=sibling, device_id_type=MESH).start()
        else:
            cp = pltpu.make_async_remote_copy(
                src_ref=outs[a].at[c], dst_ref=outs[a].at[1 - c], send_sem=send2.at[a], recv_sem=recv2.at[a],
                device_id=sibling, device_id_type=MESH)
            cp.wait_send()
            cp.wait_recv()


def _sibling_swap(arrays):
    na = len(arrays)

    def body(*refs):
        ins, theirs = refs[:na], refs[na:2 * na]
        send_sems, recv_sems = refs[2 * na:]
        x, y, c = lax.axis_index("x"), lax.axis_index("y"), lax.axis_index("c")
        sibling = (x, y, 1 - c)
        for a in range(na):
            pltpu.make_async_remote_copy(
                src_ref=ins[a].at[1 - c], dst_ref=theirs[a], send_sem=send_sems.at[a], recv_sem=recv_sems.at[a],
                device_id=sibling, device_id_type=MESH).start()
        for a in range(na):
            cp = pltpu.make_async_remote_copy(
                src_ref=ins[a].at[1 - c], dst_ref=theirs[a], send_sem=send_sems.at[a], recv_sem=recv_sems.at[a],
                device_id=sibling, device_id_type=MESH)
            cp.wait_send()
            cp.wait_recv()

    any_spec = pl.BlockSpec(memory_space=pl.ANY)
    return pl.pallas_call(
        body, name="sibling_swap",
        in_specs=[any_spec] * na, out_specs=[any_spec] * na,
        out_shape=[jax.ShapeDtypeStruct(s.shape[1:], s.dtype) for s in arrays],
        scratch_shapes=[pltpu.SemaphoreType.DMA((na,)), pltpu.SemaphoreType.DMA((na,))],
    )(*arrays)


def _pair_add(both, theirs, name):
    n, r, c = theirs.shape
    tr = _pick_tile(r, (256, 176, 128, 64, 8))
    core = lax.axis_index("c").astype(jnp.int32).reshape(1)

    def body(core_ref, a_ref, b_ref, o_ref):
        o_ref[...] = (a_ref[...].astype(F32) + b_ref[...].astype(F32)).astype(o_ref.dtype)

    return pl.pallas_call(
        body, name=name,
        grid_spec=pltpu.PrefetchScalarGridSpec(
            num_scalar_prefetch=1, grid=(n, r // tr),
            in_specs=[pl.BlockSpec((None, 1, tr, c), lambda i, j, core_ref: (core_ref[0], i, j, 0)),
                      pl.BlockSpec((1, tr, c), lambda i, j, core_ref: (i, j, 0))],
            out_specs=pl.BlockSpec((1, tr, c), lambda i, j, core_ref: (i, j, 0))),
        out_shape=jax.ShapeDtypeStruct(theirs.shape, theirs.dtype),
        compiler_params=_cparams(("parallel", "parallel")),
    )(core, both, theirs)


def _adamw_math(g, w, m, v):
    m2 = ADAM_B1 * m + (1.0 - ADAM_B1) * g
    v2 = ADAM_B2 * v + (1.0 - ADAM_B2) * (g * g)
    m_hat = m2 / (1.0 - ADAM_B1 ** ADAM_STEP)
    v_hat = v2 / (1.0 - ADAM_B2 ** ADAM_STEP)
    delta = -ADAM_LR * (m_hat / (jnp.sqrt(v_hat) + ADAM_EPS) + ADAM_WD * w)
    return delta, m2, v2


def _adamw_big(contrib, w, m, v, layer, name, prev=None):
    _, r, c = w.shape
    rh = r // 2
    tr = _pick_tile(rh, (256, 176, 128, 64, 8))
    nj = rh // tr
    blk = pl.BlockSpec((1, tr, c), lambda h, j: (layer, h * nj + j, 0))
    n_prev = 0 if prev is None else 4

    def body(*refs):
        g_ref, w_ref, m_ref, v_ref = refs[:4]
        go_ref, d_ref, mo_ref, vo_ref = refs[4 + n_prev:]
        g = g_ref[0, 0].astype(F32)
        for s in range(1, N_CHIPS):
            g = g + g_ref[0, s].astype(F32)
        delta, m2, v2 = _adamw_math(g, w_ref[0], m_ref[0], v_ref[0])
        go_ref[0] = g
        d_ref[0] = delta
        mo_ref[0] = m2
        vo_ref[0] = v2

    any_spec = pl.BlockSpec(memory_space=pl.ANY)
    return pl.pallas_call(
        body, name=name, grid=(2, nj),
        in_specs=[pl.BlockSpec((1, N_CHIPS, tr, c), lambda h, j: (h, 0, j, 0)), blk, blk, blk] + [any_spec] * n_prev,
        out_specs=[blk] * 4, out_shape=[jax.ShapeDtypeStruct(w.shape, F32)] * 4,
        input_output_aliases={4 + a: a for a in range(n_prev)},
        compiler_params=_cparams(("parallel", "parallel")),
    )(contrib, w, m, v, *([] if prev is None else prev))


def _sum_small(contrib):
    rows = contrib.shape[2]

    def body(g_ref, o_ref):
        total = g_ref[0, 0]
        for j in range(1, N_DEV):
            total = total + g_ref[j // N_CHIPS, j % N_CHIPS]
        o_ref[...] = total

    return pl.pallas_call(
        body, name="sum_small", out_shape=jax.ShapeDtypeStruct((rows, 128), F32),
        compiler_params=_cparams(),
    )(contrib)


def _adamw_small(gs, ws, ms, vs):
    n = len(gs)
    as2d = lambda a: a.reshape(1, -1) if a.ndim == 1 else a

    def body(*refs):
        g_refs, w_refs, m_refs, v_refs = refs[:n], refs[n:2 * n], refs[2 * n:3 * n], refs[3 * n:4 * n]
        d_refs, mo_refs, vo_refs = refs[4 * n:5 * n], refs[5 * n:6 * n], refs[6 * n:]
        for j in range(n):
            delta, m2, v2 = _adamw_math(g_refs[j][...], w_refs[j][...], m_refs[j][...], v_refs[j][...])
            d_refs[j][...] = delta
            mo_refs[j][...] = m2
            vo_refs[j][...] = v2

    ins = [as2d(a) for a in (*gs, *ws, *ms, *vs)]
    outs = pl.pallas_call(
        body, name="adamw_small", out_shape=[jax.ShapeDtypeStruct(a.shape, F32) for a in ins[:n]] * 3,
        compiler_params=_cparams(),
    )(*ins)
    back = lambda group: [o.reshape(g.shape) for o, g in zip(group, gs)]
    return back(outs[:n]), back(outs[n:2 * n]), back(outs[2 * n:])


PACK_ALIGN = 8 * 128


def _packed_rows(shape):
    n = 1
    for d in shape:
        n *= d
    return (n + PACK_ALIGN - 1) // PACK_ALIGN * 8


def _pack(arrays):
    parts = []
    for a in arrays:
        flat = a.reshape(-1)
        pad = _packed_rows(a.shape) * 128 - flat.shape[0]
        if pad:
            flat = jnp.concatenate([flat, jnp.zeros((pad,), F32)])
        parts.append(flat.reshape(-1, 128))
    return jnp.concatenate(parts, axis=0)


def _unpack(packed, shapes):
    out, row = [], 0
    for s in shapes:
        rows = _packed_rows(s)
        n = 1
        for d in s:
            n *= d
        out.append(packed[row:row + rows].reshape(-1)[:n].reshape(s))
        row += rows
    return out


SMALL_NAMES = ("norm1_w", "sgu_ln_w", "sgu_ln_b", "sgu_w_spatial", "sgu_b_spatial", "sc_conv_w", "dn_conv_w",
               "dn_a_log", "dn_dt_bias", "dn_norm_w", "gla_w_gate2", "gla_gate_bias", "gla_norm_w", "norm2_w",
               "final_norm_w")
SHARDED_SMALL = ("sc_conv_w", "dn_conv_w", "gla_w_gate2")
BIG_NAMES = ("w_in", "w_out", "w_gate_up", "w_down")
WEIGHT_ORDER = ("norm1_w", "w_in", "sgu_ln_w", "sgu_ln_b", "sgu_w_spatial", "sgu_b_spatial", "sc_conv_w", "dn_conv_w",
                "dn_a_log", "dn_dt_bias", "dn_norm_w", "gla_w_gate2", "gla_gate_bias", "gla_norm_w", "w_out",
                "norm2_w", "w_gate_up", "w_down", "final_norm_w")


def _cols_from_shards(g):
    l, n, r, c = g.shape
    return jnp.transpose(g, (0, 2, 1, 3)).reshape(l, r, n * c)


def _rows_from_shards(g):
    l, n, r, c = g.shape
    return g.reshape(l, n * r, c)


def kernel(x, norm1_w, w_in, sgu_ln_w, sgu_ln_b, sgu_w_spatial, sgu_b_spatial, sc_conv_w, dn_conv_w, dn_a_log, dn_dt_bias, dn_norm_w, gla_w_gate2, gla_gate_bias, gla_norm_w, w_out, norm2_w, w_gate_up, w_down, final_norm_w, loss_target, m_norm1_w, m_w_in, m_sgu_ln_w, m_sgu_ln_b, m_sgu_w_spatial, m_sgu_b_spatial, m_sc_conv_w, m_dn_conv_w, m_dn_a_log, m_dn_dt_bias, m_dn_norm_w, m_gla_w_gate2, m_gla_gate_bias, m_gla_norm_w, m_w_out, m_norm2_w, m_w_gate_up, m_w_down, m_final_norm_w, v_norm1_w, v_w_in, v_sgu_ln_w, v_sgu_ln_b, v_sgu_w_spatial, v_sgu_b_spatial, v_sc_conv_w, v_dn_conv_w, v_dn_a_log, v_dn_dt_bias, v_dn_norm_w, v_gla_w_gate2, v_gla_gate_bias, v_gla_norm_w, v_w_out, v_norm2_w, v_w_gate_up, v_w_down, v_final_norm_w):
    w = dict(norm1_w=norm1_w, w_in=w_in, sgu_ln_w=sgu_ln_w, sgu_ln_b=sgu_ln_b, sgu_w_spatial=sgu_w_spatial,
             sgu_b_spatial=sgu_b_spatial, sc_conv_w=sc_conv_w, dn_conv_w=dn_conv_w, dn_a_log=dn_a_log,
             dn_dt_bias=dn_dt_bias, dn_norm_w=dn_norm_w, gla_w_gate2=gla_w_gate2, gla_gate_bias=gla_gate_bias,
             gla_norm_w=gla_norm_w, w_out=w_out, norm2_w=norm2_w, w_gate_up=w_gate_up, w_down=w_down,
             final_norm_w=final_norm_w)
    m = dict(norm1_w=m_norm1_w, w_in=m_w_in, sgu_ln_w=m_sgu_ln_w, sgu_ln_b=m_sgu_ln_b, sgu_w_spatial=m_sgu_w_spatial,
             sgu_b_spatial=m_sgu_b_spatial, sc_conv_w=m_sc_conv_w, dn_conv_w=m_dn_conv_w, dn_a_log=m_dn_a_log,
             dn_dt_bias=m_dn_dt_bias, dn_norm_w=m_dn_norm_w, gla_w_gate2=m_gla_w_gate2,
             gla_gate_bias=m_gla_gate_bias, gla_norm_w=m_gla_norm_w, w_out=m_w_out, norm2_w=m_norm2_w,
             w_gate_up=m_w_gate_up, w_down=m_w_down, final_norm_w=m_final_norm_w)
    v = dict(norm1_w=v_norm1_w, w_in=v_w_in, sgu_ln_w=v_sgu_ln_w, sgu_ln_b=v_sgu_ln_b, sgu_w_spatial=v_sgu_w_spatial,
             sgu_b_spatial=v_sgu_b_spatial, sc_conv_w=v_sc_conv_w, dn_conv_w=v_dn_conv_w, dn_a_log=v_dn_a_log,
             dn_dt_bias=v_dn_dt_bias, dn_norm_w=v_dn_norm_w, gla_w_gate2=v_gla_w_gate2,
             gla_gate_bias=v_gla_gate_bias, gla_norm_w=v_gla_norm_w, w_out=v_w_out, norm2_w=v_norm2_w,
             w_gate_up=v_w_gate_up, w_down=v_w_down, final_norm_w=v_final_norm_w)
    chip = 2 * lax.axis_index("x") + lax.axis_index("y")

    first = [w["w_in"].astype(MXU_DTYPE)] + [w[n] for n in SHARDED_SMALL]
    gathered = _chip_exchange(first, ["layer"] * len(first), "gather_first")
    full_in = _cols_from_shards(gathered[0])
    big = dict(w_in=[_pad_w_in(full_in[l]) for l in range(DEPTH)])
    small = {n: w[n] for n in SMALL_NAMES if n not in SHARDED_SMALL}
    for j, n in enumerate(SHARDED_SMALL):
        small[n] = _cols_from_shards(gathered[1 + j])
    late = [w[n].astype(MXU_DTYPE) for n in ("w_out", "w_gate_up", "w_down")]

    def finish_late(g):
        return dict(w_out=_rows_from_shards(g[0]), w_gate_up=_cols_from_shards(g[1]), w_down=_rows_from_shards(g[2]))

    small_shapes = [(DEPTH,) + w[n].shape[1:-1] + (w[n].shape[-1] * (N_CHIPS if n in SHARDED_SMALL else 1),)
                    if n != "final_norm_w" else w[n].shape for n in SMALL_NAMES] + [(1,)]

    def pack_small(gsmall, loss_tile):
        return _pack([gsmall[n] for n in SMALL_NAMES] + [loss_tile[0:1, 0]])

    _, grad_x, contribs, small_contrib = _local_step(
        x[0], loss_target[0], big, small, late_weights=(late, finish_late), exchange=True, small_extra=pack_small)

    out_g, out_d, out_m, out_v = {}, {}, {}, {}
    for j, n in enumerate(BIG_NAMES):
        outs = _adamw_big(contribs[1][j], w[n], m[n], v[n], 1, "adamw_" + n + "_1")
        out_g[n], out_d[n], out_m[n], out_v[n] = _adamw_big(contribs[0][j], w[n], m[n], v[n], 0, "adamw_" + n + "_0",
                                                            prev=outs)
    summed = _unpack(_sum_small(small_contrib), small_shapes)
    loss = summed[-1][0]
    for n, g in zip(SMALL_NAMES, summed[:-1]):
        if n in SHARDED_SMALL:
            cols = g.shape[-1] // N_CHIPS
            g = lax.dynamic_slice_in_dim(g, chip * cols, cols, axis=g.ndim - 1)
        out_g[n] = g
    d_s, m_s, v_s = _adamw_small([out_g[n] for n in SMALL_NAMES], [w[n] for n in SMALL_NAMES],
                                 [m[n] for n in SMALL_NAMES], [v[n] for n in SMALL_NAMES])
    for n, d_, m_, v_ in zip(SMALL_NAMES, d_s, m_s, v_s):
        out_d[n], out_m[n], out_v[n] = d_, m_, v_

    return (loss, grad_x[None], *[out_g[n] for n in WEIGHT_ORDER], *[out_d[n] for n in WEIGHT_ORDER],
            *[out_m[n] for n in WEIGHT_ORDER], *[out_v[n] for n in WEIGHT_ORDER])
```

```python
import functools

import jax
import jax.numpy as jnp
from jax import lax
from jax.experimental import pallas as pl
from jax.experimental.pallas import tpu as pltpu

F32 = jnp.float32
BF16 = jnp.bfloat16
MXU_DTYPE = jnp.bfloat16
GRAD_WIRE_DTYPE = jnp.bfloat16
HI = lax.Precision.HIGHEST
MESH = pl.DeviceIdType.MESH

D_MODEL = 1024
DEPTH = 2
GROUP = 256
HEADS = 4
HEAD_DIM = 64
SGU_CHUNK = 128
SCAN_CHUNK = 64
D_FF = 2816
EPS = 1e-6
IN_COLS = 3352
P_COLS = 3584
HALO = 8
N_CHIPS = 4
N_DEV = 8
VMEM_LIMIT = 56 * 1024 * 1024

ADAM_LR = 0.001
ADAM_B1 = 0.9
ADAM_B2 = 0.999
ADAM_EPS = 1e-08
ADAM_WD = 0.01
ADAM_STEP = 10

(COL_AU, COL_AV, COL_BB, COL_BC, COL_BH, COL_CQ, COL_CK, COL_CV, COL_CZ,
 COL_DQ, COL_DK, COL_DV, COL_DZ) = range(13)
COL128_SMALL_C = 26
COL128_SMALL_D = 27


def _cparams(sem=None):
    return pltpu.CompilerParams(dimension_semantics=sem, vmem_limit_bytes=VMEM_LIMIT)


def _iota(shape, dim):
    return lax.broadcasted_iota(jnp.int32, shape, dim)


def _dg(a, b, ca, cb, prec=None):
    return lax.dot_general(a, b, (((ca,), (cb,)), ((), ())), preferred_element_type=F32, precision=prec)


@functools.partial(jax.custom_vjp, nondiff_argnums=(2, 3))
def bdot(a, b, ca, cb):
    return _dg(a.astype(MXU_DTYPE), b.astype(MXU_DTYPE), ca, cb)


def _bdot_fwd(a, b, ca, cb):
    return bdot(a, b, ca, cb), (a, b)


def _bdot_bwd(ca, cb, res, g):
    a, b = res
    if ca == 1:
        da = bdot(g, b, 1, 1 if cb == 0 else 0)
    else:
        da = bdot(b, g, 1 if cb == 0 else 0, 1)
    if cb == 0:
        db = bdot(a, g, 0, 0) if ca == 1 else bdot(a, g, 1, 0)
    else:
        db = bdot(g, a, 0, 0) if ca == 1 else bdot(g, a, 0, 1)
    return da, db


bdot.defvjp(_bdot_fwd, _bdot_bwd)


def _pieces(a, n):
    out, r = [], a
    for i in range(n):
        p = r.astype(MXU_DTYPE)
        out.append(p)
        if i + 1 < n:
            r = r - p.astype(F32)
    return out


def _mdot_impl(a, b, ca, cb, sa, sb):
    total = None
    for i, x in enumerate(_pieces(a, sa)):
        for j, y in enumerate(_pieces(b, sb)):
            if i + j < max(sa, sb):
                t = _dg(x, y, ca, cb)
                total = t if total is None else total + t
    return total


@functools.partial(jax.custom_vjp, nondiff_argnums=(2, 3, 4, 5))
def mdot(a, b, ca, cb, sa, sb):
    return _mdot_impl(a, b, ca, cb, sa, sb)


def _mdot_fwd(a, b, ca, cb, sa, sb):
    return _mdot_impl(a, b, ca, cb, sa, sb), (a, b)


def _mdot_bwd(ca, cb, sa, sb, res, g):
    a, b = res
    ga, gb = (3 if sb == 1 else 2), (3 if sa == 1 else 2)
    if sa == 1:
        da = jnp.zeros_like(a)
    elif ca == 1:
        da = mdot(g, b, 1, 1 if cb == 0 else 0, ga, sb)
    else:
        da = mdot(b, g, 1 if cb == 0 else 0, 1, sb, ga)
    if sb == 1:
        db = jnp.zeros_like(b)
    elif cb == 0:
        db = mdot(a, g, 0, 0, sa, gb) if ca == 1 else mdot(a, g, 1, 0, sa, gb)
    else:
        db = mdot(g, a, 0, 0, gb, sa) if ca == 1 else mdot(g, a, 0, 1, gb, sa)
    return da, db


mdot.defvjp(_mdot_fwd, _mdot_bwd)


def mask_r(a, m, ca=1, cb=0):
    return mdot(a, m, ca, cb, 3, 1)


def mask_l(m, b, ca=1, cb=0):
    return mdot(m, b, ca, cb, 1, 3)


def ddot(a, b, ca=1, cb=0):
    return mdot(a, b, ca, cb, 2, 2)


def _head_mask(h):
    return ((_iota((1, GROUP), 1) >> 6) == h).astype(F32)


def _block_diag_mask():
    return ((_iota((GROUP, GROUP), 0) >> 6) == (_iota((GROUP, GROUP), 1) >> 6)).astype(F32)


def _expand_mat(offset):
    return ((_iota((128, GROUP), 0) - offset) == (_iota((128, GROUP), 1) >> 6)).astype(F32)


def _tril(n, strict=False):
    r, c = _iota((n, n), 0), _iota((n, n), 1)
    return (r > c) if strict else (r >= c)


def _row_pick(x, row):
    return jnp.sum(jnp.where(_iota(x.shape, 0) == row, x, 0.0), axis=0, keepdims=True)


def _shift_rows_impl(x, halo, j):
    n = x.shape[0]
    r = _iota(x.shape, 0)
    top = jnp.concatenate([pltpu.roll(halo, j, 0), jnp.zeros((n - HALO, x.shape[1]), x.dtype)], axis=0)
    return jnp.where(r >= j, pltpu.roll(x, j, 0), top)


def _mxu_round(a):
    return a.astype(MXU_DTYPE).astype(F32)


@functools.partial(jax.custom_vjp, nondiff_argnums=(3,))
def _causal_conv(x, halo, w, width):
    xb, hb, wb = _mxu_round(x), _mxu_round(halo), _mxu_round(w)
    out = xb * _row_pick(wb, width - 1)
    for j in range(1, width):
        out = out + _shift_rows_impl(xb, hb, j) * _row_pick(wb, width - 1 - j)
    return out


def _causal_conv_fwd(x, halo, w, width):
    return _causal_conv(x, halo, w, width), (x, halo, w)


def _causal_conv_bwd(width, res, g):
    x, halo, w = res
    xb, hb, wb, gb = _mxu_round(x), _mxu_round(halo), _mxu_round(w), _mxu_round(g)
    n = g.shape[0]
    rows, rows8 = _iota(g.shape, 0), _iota(halo.shape, 0)
    dx = gb * _row_pick(wb, width - 1)
    dh = jnp.zeros_like(halo)
    dw = jnp.where(rows8 == width - 1, jnp.sum(xb * gb, axis=0, keepdims=True), 0.0)
    for j in range(1, width):
        gj = gb * _row_pick(wb, width - 1 - j)
        dx = dx + jnp.where(rows < n - j, pltpu.roll(gj, n - j, 0), 0.0)
        dh = dh + jnp.where(rows8 >= HALO - j, pltpu.roll(gj[0:HALO], HALO - j, 0), 0.0)
        tap = jnp.sum(_shift_rows_impl(xb, hb, j) * gb, axis=0, keepdims=True)
        dw = dw + jnp.where(rows8 == width - 1 - j, tap, 0.0)
    return dx, dh, dw


_causal_conv.defvjp(_causal_conv_fwd, _causal_conv_bwd)


def _head_sum(x, bd):
    return mask_r(x, bd)


def _softplus(x):
    return jnp.maximum(x, 0.0) + jnp.log1p(jnp.exp(-jnp.abs(x)))


def _log_sigmoid(x):
    return -_softplus(-x)


def _silu(x):
    return x * jax.nn.sigmoid(x)


def _head_rmsnorm_gate(o, nw, z, bd):
    ms = _head_sum(o * o, bd) * (1.0 / HEAD_DIM)
    return o * lax.rsqrt(ms + EPS) * nw * _silu(z)


def _sgu_chunk(pu, pv, ln_w, ln_b, ws0, ws1, ws2, ws3, bs_t):
    u = jax.nn.gelu(pu)
    g = jax.nn.gelu(pv)
    mu = jnp.mean(g, axis=-1, keepdims=True)
    var = jnp.mean(jnp.square(g - mu), axis=-1, keepdims=True)
    v = (g - mu) * lax.rsqrt(var + EPS) * ln_w + ln_b
    keep = _tril(SGU_CHUNK)
    mixed = mask_r(bs_t, _expand_mat(0))
    for h, ws in enumerate((ws0, ws1, ws2, ws3)):
        mixed = mixed + _head_mask(h) * bdot(jnp.where(keep, ws, 0.0), v, 1, 0)
    return u * mixed


def _sc_chunk(pb, pc, ph, halo_c, halo_h, cw):
    return pb * _causal_conv(pc * ph, halo_c * halo_h, cw, 3)


def _neumann_inverses(lows):
    n = lows[0].shape[0]
    eye = (_iota((n, n), 0) == _iota((n, n), 1)).astype(F32)
    a = [-low for low in lows]
    t = [eye + x for x in a]
    for _ in range(5):
        a = [ddot(x, x) for x in a]
        t = [ti + ddot(ti, ai) for ti, ai in zip(t, a)]
    return t


@jax.custom_vjp
def _saved_inverse(low, inv):
    return inv


def _saved_inverse_fwd(low, inv):
    return inv, inv


def _saved_inverse_bwd(inv, g):
    return -ddot(ddot(inv, g, 0, 0), inv, 1, 1), jnp.zeros_like(inv)


_saved_inverse.defvjp(_saved_inverse_fwd, _saved_inverse_bwd)


def _chunk_tril(rows):
    r, c = _iota((rows, rows), 0), _iota((rows, rows), 1)
    return ((r >> 6) == (c >> 6)) & (r >= c)


def _dn_block(pq, pk, pv, hq, hk, hv, small, pz, cwq, cwk, cwv, a_log, dt_bias, nw, state, saved_inv=None):
    c = SCAN_CHUNK
    rows = pq.shape[0]
    bd = _block_diag_mask()
    q = _silu(_causal_conv(pq, hq, cwq, 4))
    k = _silu(_causal_conv(pk, hk, cwk, 4))
    v = _silu(_causal_conv(pv, hv, cwv, 4))
    q = q * lax.rsqrt(_head_sum(q * q, bd) + EPS) * (HEAD_DIM ** -0.5)
    k = k * lax.rsqrt(_head_sum(k * k, bd) + EPS)
    lane = _iota((1, 128), 1)
    g = jnp.where(lane < HEADS, -jnp.exp(a_log) * _softplus(small + dt_bias), 0.0)
    beta_b = mask_r(jax.nn.sigmoid(small), _expand_mat(HEADS))
    gc_all = mask_l(_chunk_tril(rows).astype(F32), g)
    gcb_all = mask_r(gc_all, _expand_mat(0))
    kb_all = k * beta_b
    vb_all = v * beta_b
    kbe_all = kb_all * jnp.exp(gcb_all)
    qg_all = q * jnp.exp(gcb_all)
    causal, strict = _tril(c), _tril(c, strict=True)
    nc = rows // c
    pairs = [(ci, h) for ci in range(nc) for h in range(HEADS)]
    sls = [slice(ci * c, (ci + 1) * c) for ci in range(nc)]
    decays, lows, attns = [], [], []
    for ci, h in pairs:
        gc = gc_all[sls[ci]]
        onehot = (_iota((c, 128), 1) == h).astype(F32)
        col = mask_l(onehot, gc, 1, 1)
        row = jnp.sum(gc * onehot, axis=1, keepdims=True)
        decays.append(jnp.exp(jnp.where(causal, row - col, -jnp.inf)))
    for j, (ci, h) in enumerate(pairs):
        mh = _head_mask(h)
        k_c = k[sls[ci]]
        lows.append(jnp.where(strict, bdot(kb_all[sls[ci]] * mh, k_c, 1, 1) * decays[j], 0.0))
        attns.append(bdot(q[sls[ci]] * mh, k_c, 1, 1) * decays[j])
    if saved_inv is None:
        invs = _neumann_inverses(lows)
    else:
        invs = [_saved_inverse(low, s) for low, s in zip(lows, saved_inv)]
    us, ws = [], []
    for ci in range(nc):
        u = jnp.zeros((c, GROUP), F32)
        w = jnp.zeros((c, GROUP), F32)
        for h in range(HEADS):
            mh = _head_mask(h)
            u = u + mh * ddot(invs[ci * HEADS + h], vb_all[sls[ci]])
            w = w + mh * ddot(invs[ci * HEADS + h], kbe_all[sls[ci]])
        us.append(u)
        ws.append(w)
    outs = []
    for ci in range(nc):
        gc_b = gcb_all[sls[ci]]
        gc_last_b = _row_pick(gc_b, c - 1)
        v_new = us[ci] - bdot(ws[ci], state, 1, 0)
        o = bdot(qg_all[sls[ci]], state, 1, 0)
        for h in range(HEADS):
            o = o + _head_mask(h) * bdot(attns[ci * HEADS + h], v_new, 1, 0)
        k_dec = k[sls[ci]] * jnp.exp(gc_last_b - gc_b)
        state = state * jnp.exp(gc_last_b) + bd * bdot(k_dec, v_new, 0, 0)
        outs.append(o)
    o = jnp.concatenate(outs, axis=0)
    return _head_rmsnorm_gate(o, nw, pz, bd), state, invs


def _gla_chunk(pq, pk, pv, small, pz, w2, gbias, nw, state_t):
    c = SCAN_CHUNK
    rows = pq.shape[0]
    nc = rows // c
    sls = [slice(ci * c, (ci + 1) * c) for ci in range(nc)]
    bd = _block_diag_mask()
    log_a = _log_sigmoid(bdot(small, w2, 1, 0) + gbias) * (1.0 / 16.0)
    gcum = mask_l(_chunk_tril(rows).astype(F32), log_a)
    r, s = _iota((rows, rows), 0), _iota((rows, rows), 1)
    base = (r >> 6) << 6
    g_mid = mask_l((s == base + c // 2).astype(F32), gcum)
    g_last = mask_l((s == base + c - 1).astype(F32), gcum)
    q = pq * (HEAD_DIM ** -0.5)
    qa = q * jnp.exp(gcum - g_mid)
    ka = pk * jnp.exp(g_mid - gcum)
    qg = q * jnp.exp(gcum)
    k_last = pk * jnp.exp(g_last - gcum)
    causal = _tril(c)
    attns = [jnp.where(causal, bdot(qa[sls[ci]] * _head_mask(h), ka[sls[ci]], 1, 1), 0.0)
             for ci in range(nc) for h in range(HEADS)]
    intra = []
    for ci in range(nc):
        o = jnp.zeros((c, GROUP), F32)
        for h in range(HEADS):
            o = o + _head_mask(h) * bdot(attns[ci * HEADS + h], pv[sls[ci]], 1, 0)
        intra.append(o)
    kvs = [bd * bdot(pv[sls[ci]], k_last[sls[ci]], 0, 0) for ci in range(nc)]
    states = []
    for ci in range(nc):
        states.append(state_t)
        state_t = state_t * jnp.exp(_row_pick(g_last[sls[ci]], 0)) + kvs[ci]
    outs = [intra[ci] + bdot(qg[sls[ci]], states[ci], 1, 1) for ci in range(nc)]
    o = jnp.concatenate(outs, axis=0)
    return _head_rmsnorm_gate(o, nw, pz, bd), state_t


def _col_spec(rows, group, rev_n=None):
    if rev_n is None:
        return pl.BlockSpec((rows, GROUP), lambda i: (i, group))
    return pl.BlockSpec((rows, GROUP), lambda i: (rev_n - 1 - i, group))


def _small_spec(rows, group128, rev_n=None):
    if rev_n is None:
        return pl.BlockSpec((rows, 128), lambda i: (i, group128))
    return pl.BlockSpec((rows, 128), lambda i: (rev_n - 1 - i, group128))


def _halo_spec(rows, group, rev_n=None):
    per = rows // HALO
    if rev_n is None:
        return pl.BlockSpec((HALO, GROUP), lambda i: (jnp.maximum(i * per - 1, 0), group))
    return pl.BlockSpec((HALO, GROUP), lambda i: (jnp.maximum((rev_n - 1 - i) * per - 1, 0), group))


def _full_spec(shape):
    nd = len(shape)
    return pl.BlockSpec(shape, lambda i: (0,) * nd)


def _out_rows_spec(rows, lanes, rev_n=None):
    if rev_n is None:
        return pl.BlockSpec((rows, lanes), lambda i: (i, 0))
    return pl.BlockSpec((rows, lanes), lambda i: (rev_n - 1 - i, 0))


def _sgu_fwd(p, ln_w, ln_b, ws, bs_t):
    t = p.shape[0]
    n = t // SGU_CHUNK

    def body(pu_ref, pv_ref, lw_ref, lb_ref, ws_ref, bs_ref, y_ref):
        y = _sgu_chunk(pu_ref[...], pv_ref[...], lw_ref[...], lb_ref[...],
                       ws_ref[0], ws_ref[1], ws_ref[2], ws_ref[3], bs_ref[...])
        y_ref[...] = y.astype(y_ref.dtype)

    return pl.pallas_call(
        body, name="sgu_fwd", grid=(n,),
        in_specs=[_col_spec(SGU_CHUNK, COL_AU), _col_spec(SGU_CHUNK, COL_AV), _full_spec((1, GROUP)),
                  _full_spec((1, GROUP)), _full_spec((HEADS, SGU_CHUNK, SGU_CHUNK)), _full_spec((SGU_CHUNK, 128))],
        out_specs=_out_rows_spec(SGU_CHUNK, GROUP),
        out_shape=jax.ShapeDtypeStruct((t, GROUP), BF16),
        compiler_params=_cparams(("arbitrary",)),
    )(p, p, ln_w, ln_b, ws, bs_t)


def _sgu_bwd(p, dmix, ln_w, ln_b, ws, bs_t):
    t = p.shape[0]
    n = t // SGU_CHUNK

    def body(pu_ref, pv_ref, dy_ref, lw_ref, lb_ref, ws_ref, bs_ref,
             dpu_ref, dpv_ref, dlw_ref, dlb_ref, dws_ref, dbs_ref):
        args = (pu_ref[...], pv_ref[...], lw_ref[...], lb_ref[...],
                ws_ref[0], ws_ref[1], ws_ref[2], ws_ref[3], bs_ref[...])
        _, vjp = jax.vjp(_sgu_chunk, *args)
        dpu, dpv, dlw, dlb, d0, d1, d2, d3, dbs = vjp(dy_ref[...])
        dpu_ref[...] = dpu.astype(dpu_ref.dtype)
        dpv_ref[...] = dpv.astype(dpv_ref.dtype)

        @pl.when(pl.program_id(0) == 0)
        def _():
            dlw_ref[...] = jnp.zeros_like(dlw_ref)
            dlb_ref[...] = jnp.zeros_like(dlb_ref)
            dws_ref[...] = jnp.zeros_like(dws_ref)
            dbs_ref[...] = jnp.zeros_like(dbs_ref)

        dlw_ref[...] += dlw
        dlb_ref[...] += dlb
        for h, d in enumerate((d0, d1, d2, d3)):
            dws_ref[h] += d
        dbs_ref[...] += dbs

    return pl.pallas_call(
        body, name="sgu_bwd", grid=(n,),
        in_specs=[_col_spec(SGU_CHUNK, COL_AU), _col_spec(SGU_CHUNK, COL_AV),
                  pl.BlockSpec((SGU_CHUNK, GROUP), lambda i: (i, 0)),
                  _full_spec((1, GROUP)), _full_spec((1, GROUP)), _full_spec((HEADS, SGU_CHUNK, SGU_CHUNK)),
                  _full_spec((SGU_CHUNK, 128))],
        out_specs=[_out_rows_spec(SGU_CHUNK, GROUP), _out_rows_spec(SGU_CHUNK, GROUP), _full_spec((1, GROUP)),
                   _full_spec((1, GROUP)), _full_spec((HEADS, SGU_CHUNK, SGU_CHUNK)), _full_spec((SGU_CHUNK, 128))],
        out_shape=[jax.ShapeDtypeStruct((t, GROUP), BF16), jax.ShapeDtypeStruct((t, GROUP), BF16),
                   jax.ShapeDtypeStruct((1, GROUP), F32), jax.ShapeDtypeStruct((1, GROUP), F32),
                   jax.ShapeDtypeStruct((HEADS, SGU_CHUNK, SGU_CHUNK), F32),
                   jax.ShapeDtypeStruct((SGU_CHUNK, 128), F32)],
        compiler_params=_cparams(("arbitrary",)),
    )(p, p, dmix, ln_w, ln_b, ws, bs_t)


SC_ROWS = 256


def _first_block_zero(halo, first):
    return jnp.where(first, 0.0, halo)


def _sc_fwd(p, cw):
    t = p.shape[0]
    n = t // SC_ROWS

    def body(pb_ref, pc_ref, ph_ref, hc_ref, hh_ref, cw_ref, y_ref):
        first = pl.program_id(0) == 0
        y = _sc_chunk(pb_ref[...], pc_ref[...], ph_ref[...], _first_block_zero(hc_ref[...], first),
                      _first_block_zero(hh_ref[...], first), cw_ref[...])
        y_ref[...] = y.astype(y_ref.dtype)

    return pl.pallas_call(
        body, name="sc_fwd", grid=(n,),
        in_specs=[_col_spec(SC_ROWS, COL_BB), _col_spec(SC_ROWS, COL_BC), _col_spec(SC_ROWS, COL_BH),
                  _halo_spec(SC_ROWS, COL_BC), _halo_spec(SC_ROWS, COL_BH), _full_spec((HALO, GROUP))],
        out_specs=_out_rows_spec(SC_ROWS, GROUP),
        out_shape=jax.ShapeDtypeStruct((t, GROUP), BF16),
        compiler_params=_cparams(("arbitrary",)),
    )(p, p, p, p, p, cw)


def _add_halo_grad(d, carry):
    return d + jnp.concatenate([jnp.zeros((d.shape[0] - HALO, d.shape[1]), d.dtype), carry], axis=0)


def _sc_bwd(p, dmix, cw):
    t = p.shape[0]
    n = t // SC_ROWS

    def body(pb_ref, pc_ref, ph_ref, hc_ref, hh_ref, dy_ref, cw_ref,
             dpb_ref, dpc_ref, dph_ref, dcw_ref, carry_c, carry_h):
        i = pl.program_id(0)
        first = i == n - 1

        @pl.when(i == 0)
        def _():
            carry_c[...] = jnp.zeros_like(carry_c)
            carry_h[...] = jnp.zeros_like(carry_h)
            dcw_ref[...] = jnp.zeros_like(dcw_ref)

        args = (pb_ref[...], pc_ref[...], ph_ref[...], _first_block_zero(hc_ref[...], first),
                _first_block_zero(hh_ref[...], first), cw_ref[...])
        _, vjp = jax.vjp(_sc_chunk, *args)
        dpb, dpc, dph, dhc, dhh, dcw = vjp(dy_ref[...])
        dpb_ref[...] = dpb.astype(dpb_ref.dtype)
        dpc_ref[...] = _add_halo_grad(dpc, carry_c[...]).astype(dpc_ref.dtype)
        dph_ref[...] = _add_halo_grad(dph, carry_h[...]).astype(dph_ref.dtype)
        carry_c[...] = dhc
        carry_h[...] = dhh
        dcw_ref[...] += dcw

    return pl.pallas_call(
        body, name="sc_bwd", grid=(n,),
        in_specs=[_col_spec(SC_ROWS, COL_BB, n), _col_spec(SC_ROWS, COL_BC, n), _col_spec(SC_ROWS, COL_BH, n),
                  _halo_spec(SC_ROWS, COL_BC, n), _halo_spec(SC_ROWS, COL_BH, n),
                  pl.BlockSpec((SC_ROWS, GROUP), lambda i: (n - 1 - i, 1)), _full_spec((HALO, GROUP))],
        out_specs=[_out_rows_spec(SC_ROWS, GROUP, n)] * 3 + [_full_spec((HALO, GROUP))],
        out_shape=[jax.ShapeDtypeStruct((t, GROUP), BF16)] * 3 + [jax.ShapeDtypeStruct((HALO, GROUP), F32)],
        scratch_shapes=[pltpu.VMEM((HALO, GROUP), F32), pltpu.VMEM((HALO, GROUP), F32)],
        compiler_params=_cparams(("arbitrary",)),
    )(p, p, p, p, p, dmix, cw)


SCAN_STEP_CHUNKS = 4
SCAN_ROWS = SCAN_STEP_CHUNKS * SCAN_CHUNK


def _host_call(body, hosted, *, name, grid, in_specs, out_specs, out_shape, scratch_shapes, args):
    params = _cparams(("arbitrary",))
    if hosted is None:
        outs = pl.pallas_call(body, name=name, grid=grid, in_specs=in_specs, out_specs=out_specs,
                              out_shape=out_shape, scratch_shapes=scratch_shapes, compiler_params=params)(*args)
        return outs, None
    stage, arrays, modes, bufs = hosted
    arrays = list(arrays) if stage == 1 else []
    n_in, n_out, n_scr, n_src, na = len(in_specs), len(out_specs), len(scratch_shapes), len(arrays), len(bufs)
    last = grid[0] - 1

    def new_body(*refs):
        srcs = refs[n_in:n_in + n_src]
        o0 = n_in + n_src + na
        ex = refs[o0 + n_out:o0 + n_out + na]
        s0 = o0 + n_out + na
        sems = refs[s0 + n_scr:]
        i = pl.program_id(0)

        def run(what):
            if stage == 1:
                _exchange_stage1(srcs, ex, modes, sems[0], sems[1], what)
            else:
                _exchange_stage2(ex, sems[0], sems[1], what)

        @pl.when(i == 0)
        def _():
            run("start")

        body(*refs[:n_in], *refs[o0:o0 + n_out], *refs[s0:s0 + n_scr])

        @pl.when(i == last)
        def _():
            run("wait")

    any_spec = pl.BlockSpec(memory_space=pl.ANY)
    outs = pl.pallas_call(
        new_body, name=name, grid=grid,
        in_specs=list(in_specs) + [any_spec] * (n_src + na), out_specs=list(out_specs) + [any_spec] * na,
        out_shape=list(out_shape) + [jax.ShapeDtypeStruct(b.shape, b.dtype) for b in bufs],
        input_output_aliases={n_in + n_src + a: n_out + a for a in range(na)},
        scratch_shapes=list(scratch_shapes) + (_stage1_sems(na) if stage == 1 else _stage2_sems(na)),
        compiler_params=params,
    )(*args, *arrays, *bufs)
    return outs[:n_out], outs[n_out:]


def _dn_fwd(p, cw3, a_log, dt_bias, nw, hosted=None):
    t = p.shape[0]
    r = SCAN_ROWS
    n = t // r

    def body(pq_ref, pk_ref, pv_ref, hq_ref, hk_ref, hv_ref, sm_ref, pz_ref, cw_ref, al_ref, dt_ref, nw_ref,
             y_ref, ck_ref, inv_ref, state):
        first = pl.program_id(0) == 0

        @pl.when(first)
        def _():
            state[...] = jnp.zeros_like(state)

        s_in = state[...]
        ck_ref[0] = s_in
        y, s_out, invs = _dn_block(pq_ref[...], pk_ref[...], pv_ref[...], _first_block_zero(hq_ref[...], first),
                                   _first_block_zero(hk_ref[...], first), _first_block_zero(hv_ref[...], first),
                                   sm_ref[...], pz_ref[...], cw_ref[0], cw_ref[1], cw_ref[2],
                                   al_ref[...], dt_ref[...], nw_ref[...], s_in)
        y_ref[...] = y.astype(y_ref.dtype)
        state[...] = s_out
        for j, inv in enumerate(invs):
            inv_ref[j] = inv

    nh = SCAN_STEP_CHUNKS * HEADS
    return _host_call(
        body, hosted, name="dn_fwd", grid=(n,),
        in_specs=[_col_spec(r, COL_CQ), _col_spec(r, COL_CK), _col_spec(r, COL_CV),
                  _halo_spec(r, COL_CQ), _halo_spec(r, COL_CK), _halo_spec(r, COL_CV),
                  _small_spec(r, COL128_SMALL_C), _col_spec(r, COL_CZ), _full_spec((3, HALO, GROUP)),
                  _full_spec((1, 128)), _full_spec((1, 128)), _full_spec((1, GROUP))],
        out_specs=[_out_rows_spec(r, GROUP), pl.BlockSpec((1, GROUP, GROUP), lambda i: (i, 0, 0)),
                   pl.BlockSpec((nh, SCAN_CHUNK, SCAN_CHUNK), lambda i: (i, 0, 0))],
        out_shape=[jax.ShapeDtypeStruct((t, GROUP), BF16), jax.ShapeDtypeStruct((n, GROUP, GROUP), F32),
                   jax.ShapeDtypeStruct((n * nh, SCAN_CHUNK, SCAN_CHUNK), F32)],
        scratch_shapes=[pltpu.VMEM((GROUP, GROUP), F32)],
        args=(p, p, p, p, p, p, p, p, cw3, a_log, dt_bias, nw))


def _dn_bwd(p, dmix, states, invs, cw3, a_log, dt_bias, nw, hosted=None):
    t = p.shape[0]
    c = SCAN_ROWS
    n = t // c
    nh = SCAN_STEP_CHUNKS * HEADS

    def body(pq_ref, pk_ref, pv_ref, hq_ref, hk_ref, hv_ref, sm_ref, pz_ref, dy_ref, ck_ref, inv_ref,
             cw_ref, al_ref, dt_ref, nw_ref,
             dpq_ref, dpk_ref, dpv_ref, dsm_ref, dpz_ref, dcw_ref, dal_ref, ddt_ref, dnw_ref,
             dstate, carry):
        i = pl.program_id(0)
        first = i == n - 1

        @pl.when(i == 0)
        def _():
            dstate[...] = jnp.zeros_like(dstate)
            carry[...] = jnp.zeros_like(carry)
            dcw_ref[...] = jnp.zeros_like(dcw_ref)
            dal_ref[...] = jnp.zeros_like(dal_ref)
            ddt_ref[...] = jnp.zeros_like(ddt_ref)
            dnw_ref[...] = jnp.zeros_like(dnw_ref)

        args = (pq_ref[...], pk_ref[...], pv_ref[...], _first_block_zero(hq_ref[...], first),
                _first_block_zero(hk_ref[...], first), _first_block_zero(hv_ref[...], first),
                sm_ref[...], pz_ref[...], cw_ref[0], cw_ref[1], cw_ref[2],
                al_ref[...], dt_ref[...], nw_ref[...], ck_ref[0])
        saved = [inv_ref[j] for j in range(nh)]
        _, vjp = jax.vjp(lambda *a: _dn_block(*a, saved_inv=saved)[:2], *args)
        (dpq, dpk, dpv, dhq, dhk, dhv, dsm, dpz, dcq, dck, dcv, dal, ddt, dnw, dst) = vjp(
            (dy_ref[...], dstate[...]))
        dpq_ref[...] = _add_halo_grad(dpq, carry[0]).astype(dpq_ref.dtype)
        dpk_ref[...] = _add_halo_grad(dpk, carry[1]).astype(dpk_ref.dtype)
        dpv_ref[...] = _add_halo_grad(dpv, carry[2]).astype(dpv_ref.dtype)
        dsm_ref[...] = dsm.astype(dsm_ref.dtype)
        dpz_ref[...] = dpz.astype(dpz_ref.dtype)
        carry[0] = dhq
        carry[1] = dhk
        carry[2] = dhv
        dstate[...] = dst
        dcw_ref[0] += dcq
        dcw_ref[1] += dck
        dcw_ref[2] += dcv
        dal_ref[...] += dal
        ddt_ref[...] += ddt
        dnw_ref[...] += dnw

    return _host_call(
        body, hosted, name="dn_bwd", grid=(n,),
        in_specs=[_col_spec(c, COL_CQ, n), _col_spec(c, COL_CK, n), _col_spec(c, COL_CV, n),
                  _halo_spec(c, COL_CQ, n), _halo_spec(c, COL_CK, n), _halo_spec(c, COL_CV, n),
                  _small_spec(c, COL128_SMALL_C, n), _col_spec(c, COL_CZ, n),
                  pl.BlockSpec((c, GROUP), lambda i: (n - 1 - i, 2)),
                  pl.BlockSpec((1, GROUP, GROUP), lambda i: (n - 1 - i, 0, 0)),
                  pl.BlockSpec((nh, SCAN_CHUNK, SCAN_CHUNK), lambda i: (n - 1 - i, 0, 0)),
                  _full_spec((3, HALO, GROUP)), _full_spec((1, 128)), _full_spec((1, 128)), _full_spec((1, GROUP))],
        out_specs=[_out_rows_spec(c, GROUP, n)] * 3 + [_out_rows_spec(c, 128, n), _out_rows_spec(c, GROUP, n),
                   _full_spec((3, HALO, GROUP)), _full_spec((1, 128)), _full_spec((1, 128)), _full_spec((1, GROUP))],
        out_shape=[jax.ShapeDtypeStruct((t, GROUP), BF16)] * 3 + [
            jax.ShapeDtypeStruct((t, 128), BF16), jax.ShapeDtypeStruct((t, GROUP), BF16),
            jax.ShapeDtypeStruct((3, HALO, GROUP), F32), jax.ShapeDtypeStruct((1, 128), F32),
            jax.ShapeDtypeStruct((1, 128), F32), jax.ShapeDtypeStruct((1, GROUP), F32)],
        scratch_shapes=[pltpu.VMEM((GROUP, GROUP), F32), pltpu.VMEM((3, HALO, GROUP), F32)],
        args=(p, p, p, p, p, p, p, p, dmix, states, invs, cw3, a_log, dt_bias, nw))


def _gla_fwd(p, w2, gbias, nw, hosted=None):
    t = p.shape[0]
    c = SCAN_ROWS
    n = t // c

    def body(pq_ref, pk_ref, pv_ref, sm_ref, pz_ref, w2_ref, gb_ref, nw_ref, y_ref, ck_ref, state):
        @pl.when(pl.program_id(0) == 0)
        def _():
            state[...] = jnp.zeros_like(state)

        s_in = state[...]
        ck_ref[0] = s_in
        y, s_out = _gla_chunk(pq_ref[...], pk_ref[...], pv_ref[...], sm_ref[...], pz_ref[...],
                              w2_ref[...], gb_ref[...], nw_ref[...], s_in)
        y_ref[...] = y.astype(y_ref.dtype)
        state[...] = s_out

    return _host_call(
        body, hosted, name="gla_fwd", grid=(n,),
        in_specs=[_col_spec(c, COL_DQ), _col_spec(c, COL_DK), _col_spec(c, COL_DV),
                  _small_spec(c, COL128_SMALL_D), _col_spec(c, COL_DZ),
                  _full_spec((128, GROUP)), _full_spec((1, GROUP)), _full_spec((1, GROUP))],
        out_specs=[_out_rows_spec(c, GROUP), pl.BlockSpec((1, GROUP, GROUP), lambda i: (i, 0, 0))],
        out_shape=[jax.ShapeDtypeStruct((t, GROUP), BF16), jax.ShapeDtypeStruct((n, GROUP, GROUP), F32)],
        scratch_shapes=[pltpu.VMEM((GROUP, GROUP), F32)],
        args=(p, p, p, p, p, w2, gbias, nw))


def _gla_bwd(p, dmix, states, w2, gbias, nw, hosted=None):
    t = p.shape[0]
    c = SCAN_ROWS
    n = t // c

    def body(pq_ref, pk_ref, pv_ref, sm_ref, pz_ref, dy_ref, ck_ref, w2_ref, gb_ref, nw_ref,
             dpq_ref, dpk_ref, dpv_ref, dsm_ref, dpz_ref, dw2_ref, dgb_ref, dnw_ref, dstate):
        @pl.when(pl.program_id(0) == 0)
        def _():
            dstate[...] = jnp.zeros_like(dstate)
            dw2_ref[...] = jnp.zeros_like(dw2_ref)
            dgb_ref[...] = jnp.zeros_like(dgb_ref)
            dnw_ref[...] = jnp.zeros_like(dnw_ref)

        args = (pq_ref[...], pk_ref[...], pv_ref[...], sm_ref[...], pz_ref[...],
                w2_ref[...], gb_ref[...], nw_ref[...], ck_ref[0])
        _, vjp = jax.vjp(_gla_chunk, *args)
        dpq, dpk, dpv, dsm, dpz, dw2, dgb, dnw, dst = vjp((dy_ref[...], dstate[...]))
        dpq_ref[...] = dpq.astype(dpq_ref.dtype)
        dpk_ref[...] = dpk.astype(dpk_ref.dtype)
        dpv_ref[...] = dpv.astype(dpv_ref.dtype)
        dsm_ref[...] = dsm.astype(dsm_ref.dtype)
        dpz_ref[...] = dpz.astype(dpz_ref.dtype)
        dstate[...] = dst
        dw2_ref[...] += dw2
        dgb_ref[...] += dgb
        dnw_ref[...] += dnw

    return _host_call(
        body, hosted, name="gla_bwd", grid=(n,),
        in_specs=[_col_spec(c, COL_DQ, n), _col_spec(c, COL_DK, n), _col_spec(c, COL_DV, n),
                  _small_spec(c, COL128_SMALL_D, n), _col_spec(c, COL_DZ, n),
                  pl.BlockSpec((c, GROUP), lambda i: (n - 1 - i, 3)),
                  pl.BlockSpec((1, GROUP, GROUP), lambda i: (n - 1 - i, 0, 0)),
                  _full_spec((128, GROUP)), _full_spec((1, GROUP)), _full_spec((1, GROUP))],
        out_specs=[_out_rows_spec(c, GROUP, n)] * 3 + [_out_rows_spec(c, 128, n), _out_rows_spec(c, GROUP, n),
                   _full_spec((128, GROUP)), _full_spec((1, GROUP)), _full_spec((1, GROUP))],
        out_shape=[jax.ShapeDtypeStruct((t, GROUP), BF16)] * 3 + [
            jax.ShapeDtypeStruct((t, 128), BF16), jax.ShapeDtypeStruct((t, GROUP), BF16),
            jax.ShapeDtypeStruct((128, GROUP), F32), jax.ShapeDtypeStruct((1, GROUP), F32),
            jax.ShapeDtypeStruct((1, GROUP), F32)],
        scratch_shapes=[pltpu.VMEM((GROUP, GROUP), F32)],
        args=(p, p, p, p, p, dmix, states, w2, gbias, nw))


def _pick_tile(n, pref):
    for cand in pref:
        if n % cand == 0:
            return cand
    return n


MM_TILE_CAP = 1408


def _largest_tile(n, cap):
    best = None
    for mult in range(1, cap // 128 + 1):
        if n % (128 * mult) == 0:
            best = 128 * mult
    return best if best is not None else n


def _half_index(t, per_half, middle):
    half = jnp.where(t >= per_half, 1, 0)
    return half, middle, t - half * per_half


def _matmul(a, b, mode, out_dtype, name, res=None):
    if mode == "nn":
        (m, k), n = a.shape, b.shape[1]
    elif mode == "nt":
        (m, k), n = a.shape, b.shape[0]
    else:
        (k, m), n = a.shape, b.shape[-1] * (2 if b.ndim == 3 else 1)
    tm = _largest_tile(m, MM_TILE_CAP)
    tn = _largest_tile(b.shape[-1] if b.ndim == 3 else n, MM_TILE_CAP)
    tk = _largest_tile(k, MM_TILE_CAP)
    nk = k // tk
    if mode == "nn":
        a_spec = pl.BlockSpec((tm, tk), lambda i, j, kk: (i, kk))
        b_spec = pl.BlockSpec((tk, tn), lambda i, j, kk: (kk, j))
        dims = (1, 0)
    elif mode == "nt":
        a_spec = pl.BlockSpec((tm, tk), lambda i, j, kk: (i, kk))
        b_spec = pl.BlockSpec((tn, tk), lambda i, j, kk: (j, kk))
        dims = (1, 1)
    else:
        a_spec = pl.BlockSpec((tk, tm), lambda i, j, kk: (kk, i))
        if b.ndim == 3:
            njh = b.shape[-1] // tn
            b_spec = pl.BlockSpec((None, tk, tn), lambda i, j, kk: _half_index(j, njh, kk))
        else:
            b_spec = pl.BlockSpec((tk, tn), lambda i, j, kk: (kk, j))
        dims = (0, 0)
    o_spec = pl.BlockSpec((tm, tn), lambda i, j, kk: (i, j))
    has_res = res is not None

    def body(*refs):
        a_ref, b_ref = refs[:2]
        r_ref = refs[2] if has_res else None
        o_ref = refs[3] if has_res else refs[2]
        part = _dg(a_ref[...].astype(MXU_DTYPE), b_ref[...].astype(MXU_DTYPE), *dims)

        def finish(out):
            if has_res:
                out = out + r_ref[...]
            o_ref[...] = out.astype(o_ref.dtype)

        if nk == 1:
            finish(part)
            return
        acc = refs[-1]
        kk = pl.program_id(2)

        @pl.when(kk == 0)
        def _():
            acc[...] = part

        @pl.when(kk > 0)
        def _():
            acc[...] += part

        @pl.when(kk == nk - 1)
        def _():
            finish(acc[...])

    in_specs = [a_spec, b_spec] + ([o_spec] if has_res else [])
    args = (a, b) + ((res,) if has_res else ())
    return pl.pallas_call(
        body, name=name, grid=(m // tm, n // tn, nk), in_specs=in_specs, out_specs=o_spec,
        out_shape=jax.ShapeDtypeStruct((m, n), out_dtype),
        scratch_shapes=[pltpu.VMEM((tm, tn), F32)] if nk > 1 else [],
        compiler_params=_cparams(("parallel", "parallel", "arbitrary")),
    )(*args)


def _matmul_nt_norm_bwd(a, b, x, w, dres, name):
    n = b.shape[0]
    m = a.shape[-2]
    tm = _largest_tile(m, 1024)
    if a.ndim == 3:
        kh = a.shape[2]
        k = 2 * kh
        tk = _largest_tile(kh, MM_TILE_CAP)
        nkh = kh // tk
        a_spec = pl.BlockSpec((None, tm, tk), lambda i, kk: _half_index(kk, nkh, i))
    else:
        k = a.shape[1]
        tk = _largest_tile(k, MM_TILE_CAP)
        a_spec = pl.BlockSpec((tm, tk), lambda i, kk: (i, kk))
    nk = k // tk

    def body(a_ref, b_ref, x_ref, w_ref, r_ref, dx_ref, dw_ref, acc):
        i, kk = pl.program_id(0), pl.program_id(1)
        part = _dg(a_ref[...].astype(MXU_DTYPE), b_ref[...].astype(MXU_DTYPE), 1, 1)

        @pl.when(kk == 0)
        def _():
            acc[...] = part

        @pl.when(kk > 0)
        def _():
            acc[...] += part

        @pl.when((i == 0) & (kk == 0))
        def _():
            dw_ref[...] = jnp.zeros_like(dw_ref)

        @pl.when(kk == nk - 1)
        def _():
            g = acc[...]
            xv = x_ref[...]
            r = lax.rsqrt(jnp.mean(xv * xv, axis=-1, keepdims=True) + EPS)
            xhat = xv * r
            dw_ref[...] += jnp.sum(g * xhat, axis=0, keepdims=True)
            gx = g * w_ref[...]
            dx_ref[...] = r_ref[...] + r * (gx - xhat * jnp.mean(gx * xhat, axis=-1, keepdims=True))

    row_spec = pl.BlockSpec((tm, n), lambda i, kk: (i, 0))
    return pl.pallas_call(
        body, name=name, grid=(m // tm, nk),
        in_specs=[a_spec, pl.BlockSpec((n, tk), lambda i, kk: (0, kk)),
                  row_spec, pl.BlockSpec((1, n), lambda i, kk: (0, 0)), row_spec],
        out_specs=[row_spec, pl.BlockSpec((1, n), lambda i, kk: (0, 0))],
        out_shape=[jax.ShapeDtypeStruct((m, n), F32), jax.ShapeDtypeStruct((1, n), F32)],
        scratch_shapes=[pltpu.VMEM((tm, n), F32)],
        compiler_params=_cparams(("arbitrary", "arbitrary")),
    )(a, b, x, w, dres)


NORM_ROWS = 512


def _rmsnorm_fwd(x, w, name):
    t, d = x.shape

    def body(x_ref, w_ref, o_ref):
        xv = x_ref[...]
        r = lax.rsqrt(jnp.mean(xv * xv, axis=-1, keepdims=True) + EPS)
        o_ref[...] = (xv * r * w_ref[...]).astype(o_ref.dtype)

    return pl.pallas_call(
        body, name=name, grid=(t // NORM_ROWS,),
        in_specs=[pl.BlockSpec((NORM_ROWS, d), lambda i: (i, 0)), _full_spec((1, d))],
        out_specs=pl.BlockSpec((NORM_ROWS, d), lambda i: (i, 0)),
        out_shape=jax.ShapeDtypeStruct((t, d), BF16),
        compiler_params=_cparams(("parallel",)),
    )(x, w)


SWIGLU_ROWS = 128


def _ffn_up_swiglu(h, w_gate_up):
    m, k = h.shape
    tm = _largest_tile(m, 512)
    tn = _largest_tile(D_FF, MM_TILE_CAP)
    nj = D_FF // tn

    def body(a_ref, bg_ref, bu_ref, g_ref, u_ref, act_ref):
        a = a_ref[...].astype(MXU_DTYPE)
        gate = _dg(a, bg_ref[...].astype(MXU_DTYPE), 1, 0)
        up = _dg(a, bu_ref[...].astype(MXU_DTYPE), 1, 0)
        g_ref[...] = gate
        u_ref[...] = up
        act_ref[...] = (_silu(gate) * up).astype(act_ref.dtype)

    o_spec = pl.BlockSpec((tm, tn), lambda i, j: (i, j))
    return pl.pallas_call(
        body, name="ffn_up", grid=(m // tm, nj),
        in_specs=[pl.BlockSpec((tm, k), lambda i, j: (i, 0)), pl.BlockSpec((k, tn), lambda i, j: (0, j)),
                  pl.BlockSpec((k, tn), lambda i, j: (0, j + nj))],
        out_specs=[o_spec, o_spec, o_spec],
        out_shape=[jax.ShapeDtypeStruct((m, D_FF), F32), jax.ShapeDtypeStruct((m, D_FF), F32),
                   jax.ShapeDtypeStruct((m, D_FF), BF16)],
        compiler_params=_cparams(("parallel", "parallel")),
    )(h, w_gate_up, w_gate_up)


def _ffn_down_dx_swiglu(dx, w_down, gate, up):
    m, k = dx.shape
    tm = _largest_tile(m, 512)
    tn = _largest_tile(D_FF, MM_TILE_CAP)

    def body(a_ref, b_ref, g_ref, u_ref, o_ref):
        da = _dg(a_ref[...].astype(MXU_DTYPE), b_ref[...].astype(MXU_DTYPE), 1, 1)
        gate = g_ref[...]
        sg = jax.nn.sigmoid(gate)
        o_ref[0] = (da * u_ref[...] * (sg * (1.0 + gate * (1.0 - sg)))).astype(o_ref.dtype)
        o_ref[1] = (da * gate * sg).astype(o_ref.dtype)

    tile = pl.BlockSpec((tm, tn), lambda i, j: (i, j))
    return pl.pallas_call(
        body, name="ffn_down_dx", grid=(m // tm, D_FF // tn),
        in_specs=[pl.BlockSpec((tm, k), lambda i, j: (i, 0)), pl.BlockSpec((tn, k), lambda i, j: (j, 0)), tile, tile],
        out_specs=pl.BlockSpec((2, tm, tn), lambda i, j: (0, i, j)),
        out_shape=jax.ShapeDtypeStruct((2, m, D_FF), MXU_DTYPE),
        compiler_params=_cparams(("parallel", "parallel")),
    )(dx, w_down, gate, up)


def _loss_head(x, w, target):
    t, d = x.shape

    def fwd(xv, wv, tv):
        r = lax.rsqrt(jnp.mean(xv * xv, axis=-1, keepdims=True) + EPS)
        err = xv * r * wv - tv
        return 0.5 * jnp.sum(jnp.mean(err * err, axis=-1, keepdims=True), axis=0, keepdims=True)

    def body(x_ref, w_ref, t_ref, dx_ref, dw_ref, loss_ref):
        @pl.when(pl.program_id(0) == 0)
        def _():
            dw_ref[...] = jnp.zeros_like(dw_ref)
            loss_ref[...] = jnp.zeros_like(loss_ref)

        loss, vjp = jax.vjp(fwd, x_ref[...], w_ref[...], t_ref[...])
        dx, dw, _ = vjp(jnp.ones((1, 1), F32))
        dx_ref[...] = dx
        dw_ref[...] += dw
        loss_ref[...] += jnp.broadcast_to(loss, loss_ref.shape)

    return pl.pallas_call(
        body, name="loss_head", grid=(t // NORM_ROWS,),
        in_specs=[pl.BlockSpec((NORM_ROWS, d), lambda i: (i, 0)), _full_spec((1, d)),
                  pl.BlockSpec((NORM_ROWS, d), lambda i: (i, 0))],
        out_specs=[pl.BlockSpec((NORM_ROWS, d), lambda i: (i, 0)), _full_spec((1, d)), _full_spec((8, 128))],
        out_shape=[jax.ShapeDtypeStruct((t, d), F32), jax.ShapeDtypeStruct((1, d), F32),
                   jax.ShapeDtypeStruct((8, 128), F32)],
        compiler_params=_cparams(("arbitrary",)),
    )(x, w, target)


def _pad_w_in(w):
    z = lambda n: jnp.zeros((w.shape[0], n), w.dtype)
    return jnp.concatenate([w[:, 0:2048], w[:, 2056:2312], w[:, 2312:3080], w[:, 3096:3352],
                            w[:, 2048:2056], z(120), w[:, 3080:3096], z(112)], axis=1)


def _unpad_w_in(wp):
    return jnp.concatenate([wp[:, 0:2048], wp[:, 3328:3336], wp[:, 2048:2304], wp[:, 2304:3072],
                            wp[:, 3456:3472], wp[:, 3072:3328]], axis=1)


def _pad_rows(a, rows):
    return jnp.concatenate([a, jnp.zeros((rows - a.shape[0],) + a.shape[1:], a.dtype)], axis=0)


def _pad_lanes(a, lanes):
    return jnp.concatenate([a, jnp.zeros(a.shape[:-1] + (lanes - a.shape[-1],), a.dtype)], axis=-1)


def _layer_params(l, small):
    dn_cw = small["dn_conv_w"][l]
    return dict(
        ln_w=small["sgu_ln_w"][l][None], ln_b=small["sgu_ln_b"][l][None],
        ws=small["sgu_w_spatial"][l], bs_t=_pad_lanes(small["sgu_b_spatial"][l].T, 128),
        sc_cw=_pad_rows(small["sc_conv_w"][l], HALO),
        dn_cw=jnp.stack([_pad_rows(dn_cw[:, j * GROUP:(j + 1) * GROUP], HALO) for j in range(3)]),
        dn_al=_pad_lanes(small["dn_a_log"][l][None], 128), dn_dt=_pad_lanes(small["dn_dt_bias"][l][None], 128),
        dn_nw=jnp.tile(small["dn_norm_w"][l][None], (1, HEADS)),
        gla_w2=_pad_rows(small["gla_w_gate2"][l], 128), gla_gb=small["gla_gate_bias"][l][None],
        gla_nw=jnp.tile(small["gla_norm_w"][l][None], (1, HEADS)),
    )


def _exchange_piece(name, grad):
    if name == "w_in":
        grad = _unpad_w_in(grad)
    if name in ("w_in", "w_gate_up"):
        r, c4 = grad.shape
        return jnp.transpose(grad.reshape(2, r // 2, N_CHIPS, c4 // N_CHIPS), (0, 2, 1, 3))
    r4, c = grad.shape
    return jnp.transpose(grad.reshape(N_CHIPS, 2, r4 // (2 * N_CHIPS), c), (1, 0, 2, 3))


def _reduce_on_chip(pieces):
    return _pair_add(pieces, _sibling_swap(pieces))


def _local_step(x, target, big, small, late_weights=None, exchange=False, small_extra=None):
    saved = []
    h = x
    for l in range(DEPTH):
        lp = _layer_params(l, small)
        h1 = _rmsnorm_fwd(h, small["norm1_w"][l][None], "norm1_fwd")
        p = _matmul(h1, big["w_in"][l], "nn", F32, "proj_in")
        y_a = _sgu_fwd(p, lp["ln_w"], lp["ln_b"], lp["ws"], lp["bs_t"])
        y_b = _sc_fwd(p, lp["sc_cw"])
        host1 = host2 = None
        if l == 0 and late_weights is not None:
            shards, finish = late_weights
            modes = ["layer"] * len(shards)
            host1 = (1, shards, modes, _exchange_buffers(shards, modes))
        (y_c, st_c, inv_c), ex = _dn_fwd(p, lp["dn_cw"], lp["dn_al"], lp["dn_dt"], lp["dn_nw"], hosted=host1)
        if host1 is not None:
            host2 = (2, None, None, ex)
        (y_d, st_d), ex = _gla_fwd(p, lp["gla_w2"], lp["gla_gb"], lp["gla_nw"], hosted=host2)
        if host2 is not None:
            big = {**big, **finish(ex)}
        mix = jnp.concatenate([y_a, y_b, y_c, y_d], axis=1)
        x1 = _matmul(mix, big["w_out"][l], "nn", F32, "proj_out", res=h)
        h2 = _rmsnorm_fwd(x1, small["norm2_w"][l][None], "norm2_fwd")
        gate, up, act = _ffn_up_swiglu(h2, big["w_gate_up"][l])
        x2 = _matmul(act, big["w_down"][l], "nn", F32, "ffn_down", res=x1)
        saved.append(dict(x0=h, h1=h1, p=p, st_c=st_c, inv_c=inv_c, st_d=st_d, mix=mix, x1=x1, h2=h2, gate=gate, up=up, act=act, lp=lp))
        h = x2

    dx, d_final, loss = _loss_head(h, small["final_norm_w"][None], target)
    gbig = {k: [None] * DEPTH for k in ("w_in", "w_out", "w_gate_up", "w_down")}
    gs = {k: [None] * DEPTH for k in ("norm1_w", "sgu_ln_w", "sgu_ln_b", "sgu_w_spatial", "sgu_b_spatial", "sc_conv_w",
                                     "dn_conv_w", "dn_a_log", "dn_dt_bias", "dn_norm_w", "gla_w_gate2",
                                     "gla_gate_bias", "gla_norm_w", "norm2_w")}
    carry = []
    contribs = {}
    for l in reversed(range(DEPTH)):
        s = saved[l]
        lp = s["lp"]
        gbig["w_down"][l] = _matmul(s["act"], dx, "tn", GRAD_WIRE_DTYPE, "ffn_down_dw")
        dgu = _ffn_down_dx_swiglu(dx, big["w_down"][l], s["gate"], s["up"])
        gbig["w_gate_up"][l] = _matmul(s["h2"], dgu, "tn", GRAD_WIRE_DTYPE, "ffn_up_dw")
        dx1, gs["norm2_w"][l] = _matmul_nt_norm_bwd(dgu, big["w_gate_up"][l], s["x1"], small["norm2_w"][l][None], dx,
                                                    "ffn_up_dx")
        gbig["w_out"][l] = _matmul(s["mix"], dx1, "tn", GRAD_WIRE_DTYPE, "proj_out_dw")
        dmix = _matmul(dx1, big["w_out"][l], "nt", F32, "proj_out_dx")
        p = s["p"]
        dpu, dpv, g_lw, g_lb, g_ws, g_bs = _sgu_bwd(p, dmix, lp["ln_w"], lp["ln_b"], lp["ws"], lp["bs_t"])
        dpb, dpc, dph, g_sc = _sc_bwd(p, dmix, lp["sc_cw"])
        host1 = host2 = None
        if exchange:
            unit = carry + [(n, l, _exchange_piece(n, gbig[n][l])) for n in ("w_out", "w_gate_up", "w_down")]
            carry = []
            summed = _reduce_on_chip([piece for _, _, piece in unit])
            modes = ["piece"] * len(summed)
            host1 = (1, summed, modes, _exchange_buffers(summed, modes))
        (dcq, dck, dcv, dcs, dcz, g_dcw, g_al, g_dt, g_dnw), ex = _dn_bwd(
            p, dmix, s["st_c"], s["inv_c"], lp["dn_cw"], lp["dn_al"], lp["dn_dt"], lp["dn_nw"], hosted=host1)
        if host1 is not None:
            host2 = (2, None, None, ex)
        (ddq, ddk, ddv, dds, ddz, g_w2, g_gb, g_gnw), ex = _gla_bwd(p, dmix, s["st_d"], lp["gla_w2"], lp["gla_gb"],
                                                                   lp["gla_nw"], hosted=host2)
        if host2 is not None:
            for (n, lay, _), got in zip(unit, ex):
                contribs[(n, lay)] = got
        dp = jnp.concatenate([dpu, dpv, dpb, dpc, dph, dcq, dck, dcv, dcz, ddq, ddk, ddv, ddz, dcs, dds], axis=1)
        gbig["w_in"][l] = _matmul(s["h1"], dp, "tn", GRAD_WIRE_DTYPE, "proj_in_dw")
        dx, gs["norm1_w"][l] = _matmul_nt_norm_bwd(dp, big["w_in"][l], s["x0"], small["norm1_w"][l][None], dx1,
                                                   "proj_in_dx")
        gs["sgu_ln_w"][l], gs["sgu_ln_b"][l] = g_lw[0], g_lb[0]
        gs["sgu_w_spatial"][l] = g_ws
        gs["sgu_b_spatial"][l] = g_bs[:, :HEADS].T
        gs["sc_conv_w"][l] = g_sc[:3]
        gs["dn_conv_w"][l] = jnp.concatenate([g_dcw[0, :4], g_dcw[1, :4], g_dcw[2, :4]], axis=1)
        gs["dn_a_log"][l], gs["dn_dt_bias"][l] = g_al[0, :HEADS], g_dt[0, :HEADS]
        gs["dn_norm_w"][l] = jnp.sum(g_dnw.reshape(HEADS, HEAD_DIM), axis=0)
        gs["gla_w_gate2"][l] = g_w2[:16]
        gs["gla_gate_bias"][l] = g_gb[0]
        gs["gla_norm_w"][l] = jnp.sum(g_gnw.reshape(HEADS, HEAD_DIM), axis=0)
        gs["norm1_w"][l] = gs["norm1_w"][l][0]
        gs["norm2_w"][l] = gs["norm2_w"][l][0]
        if exchange:
            carry = [("w_in", l, _exchange_piece("w_in", gbig["w_in"][l]))]
    gsmall = {k: jnp.stack(v) for k, v in gs.items()}
    gsmall["final_norm_w"] = d_final[0]
    if not exchange:
        return loss, dx, gbig, gsmall
    summed = _reduce_on_chip([piece for _, _, piece in carry])
    last = _chip_exchange(summed + [small_extra(gsmall, loss)], ["piece"] * len(summed) + ["whole"], "exchange_grads")
    for (n, lay, _), got in zip(carry, last):
        contribs[(n, lay)] = got
    return loss, dx, contribs, last[-1]


def _peer_chips(x, y):
    return [(1 - x, y, 2 * (1 - x) + y), (x, 1 - y, 2 * x + 1 - y), (1 - x, 1 - y, 2 * (1 - x) + 1 - y)]


def _chip_exchange(arrays, modes, name):
    na = len(arrays)
    bufs = _exchange_buffers(arrays, modes)

    def body(*refs):
        ins, outs = refs[:na], refs[2 * na:3 * na]
        send1, recv1, send2, recv2 = refs[3 * na:]
        _exchange_stage1(ins, outs, modes, send1, recv1, "start")
        _exchange_stage1(ins, outs, modes, send1, recv1, "wait")
        _exchange_stage2(outs, send2, recv2, "start")
        _exchange_stage2(outs, send2, recv2, "wait")

    any_spec = pl.BlockSpec(memory_space=pl.ANY)
    return pl.pallas_call(
        body, name=name,
        in_specs=[any_spec] * (2 * na), out_specs=[any_spec] * na,
        out_shape=[jax.ShapeDtypeStruct(b.shape, b.dtype) for b in bufs],
        input_output_aliases={na + a: a for a in range(na)},
        scratch_shapes=_stage1_sems(na) + _stage2_sems(na),
    )(*arrays, *bufs)


def _exchange_buffers(arrays, modes):
    c_idx = lax.axis_index("c")
    chip = 2 * lax.axis_index("x") + lax.axis_index("y")
    bufs = []
    for arr, md in zip(arrays, modes):
        if md == "layer":
            unit = lax.dynamic_index_in_dim(arr, c_idx, 0, keepdims=False)
        elif md == "piece":
            unit = lax.dynamic_index_in_dim(arr, chip, 0, keepdims=False)
        else:
            unit = arr
        buf = lax.empty((2, N_CHIPS) + unit.shape, unit.dtype)
        bufs.append(lax.dynamic_update_slice(buf, unit[None, None], (c_idx, chip) + (0,) * unit.ndim))
    return bufs


def _stage1_sems(na):
    return [pltpu.SemaphoreType.DMA((na, 3)), pltpu.SemaphoreType.DMA((na, 3))]


def _stage2_sems(na):
    return [pltpu.SemaphoreType.DMA((na,)), pltpu.SemaphoreType.DMA((na,))]


def _exchange_stage1(ins, outs, modes, send1, recv1, what):
    x, y, c = lax.axis_index("x"), lax.axis_index("y"), lax.axis_index("c")
    me = 2 * x + y
    for a in range(len(ins)):
        for k, (px, py, pidx) in enumerate(_peer_chips(x, y)):
            if modes[a] == "layer":
                src = ins[a].at[c]
            else:
                src = ins[a].at[pidx] if modes[a] == "piece" else ins[a]
            if what == "start":
                pltpu.make_async_remote_copy(
                    src_ref=src, dst_ref=outs[a].at[c, me], send_sem=send1.at[a, k], recv_sem=recv1.at[a, k],
                    device_id=(px, py, c), device_id_type=MESH).start()
            else:
                cp = pltpu.make_async_remote_copy(
                    src_ref=src, dst_ref=outs[a].at[c, pidx], send_sem=send1.at[a, k], recv_sem=recv1.at[a, k],
                    device_id=(px, py, c), device_id_type=MESH)
                cp.wait_send()
                cp.wait_recv()


def _exchange_stage2(outs, send2, recv2, what):
    x, y, c = lax.axis_index("x"), lax.axis_index("y"), lax.axis_index("c")
    sibling = (x, y, 1 - c)
    for a in range(len(outs)):
        if what == "start":
            pltpu.make_async_remote_copy(
                src_ref=outs[a].at[c], dst_ref=outs[a].at[c], send_sem=send2.at[a], recv_sem=recv2.at[a],
                device_id=sibling, device_id_type=MESH).start()
        else:
            cp = pltpu.make_async_remote_copy(
                src_ref=outs[a].at[c], dst_ref=outs[a].at[1 - c], send_sem=send2.at[a], recv_sem=recv2.at[a],
                device_id=sibling, device_id_type=MESH)
            cp.wait_send()
            cp.wait_recv()


def _sibling_swap(arrays):
    na = len(arrays)

    def body(*refs):
        ins, theirs = refs[:na], refs[na:2 * na]
        send_sems, recv_sems = refs[2 * na:]
        x, y, c = lax.axis_index("x"), lax.axis_index("y"), lax.axis_index("c")
        sibling = (x, y, 1 - c)
        for a in range(na):
            pltpu.make_async_remote_copy(
                src_ref=ins[a].at[1 - c], dst_ref=theirs[a], send_sem=send_sems.at[a], recv_sem=recv_sems.at[a],
                device_id=sibling, device_id_type=MESH).start()
        for a in range(na):
            cp = pltpu.make_async_remote_copy(
                src_ref=ins[a].at[1 - c], dst_ref=theirs[a], send_sem=send_sems.at[a], recv_sem=recv_sems.at[a],
                device_id=sibling, device_id_type=MESH)
            cp.wait_send()
            cp.wait_recv()

    any_spec = pl.BlockSpec(memory_space=pl.ANY)
    return pl.pallas_call(
        body, name="sibling_swap",
        in_specs=[any_spec] * na, out_specs=[any_spec] * na,
        out_shape=[jax.ShapeDtypeStruct(s.shape[1:], s.dtype) for s in arrays],
        scratch_shapes=[pltpu.SemaphoreType.DMA((na,)), pltpu.SemaphoreType.DMA((na,))],
    )(*arrays)


PAIR_ADD_STEPS = 8


def _pair_add(boths, theirs):
    na = len(boths)
    core = lax.axis_index("c").astype(jnp.int32).reshape(1)
    flat_b = [b.reshape(2, b.shape[1] * b.shape[2], b.shape[3]) for b in boths]
    flat_t = [t.reshape(t.shape[0] * t.shape[1], t.shape[2]) for t in theirs]
    rows = [t.shape[0] // PAIR_ADD_STEPS for t in flat_t]

    def body(core_ref, *refs):
        for a in range(na):
            refs[2 * na + a][...] = (refs[a][...].astype(F32) + refs[na + a][...].astype(F32)).astype(
                refs[2 * na + a].dtype)

    own = [pl.BlockSpec((None, r, t.shape[1]), lambda i, core_ref: (core_ref[0], i, 0)) for r, t in zip(rows, flat_t)]
    plain = [pl.BlockSpec((r, t.shape[1]), lambda i, core_ref: (i, 0)) for r, t in zip(rows, flat_t)]
    outs = pl.pallas_call(
        body, name="pair_add",
        grid_spec=pltpu.PrefetchScalarGridSpec(
            num_scalar_prefetch=1, grid=(PAIR_ADD_STEPS,), in_specs=own + plain, out_specs=plain),
        out_shape=[jax.ShapeDtypeStruct(t.shape, t.dtype) for t in flat_t],
        compiler_params=_cparams(("parallel",)),
    )(core, *flat_b, *flat_t)
    return [o.reshape(t.shape) for o, t in zip(outs, theirs)]


def _adamw_math(g, w, m, v):
    m2 = ADAM_B1 * m + (1.0 - ADAM_B1) * g
    v2 = ADAM_B2 * v + (1.0 - ADAM_B2) * (g * g)
    m_hat = m2 / (1.0 - ADAM_B1 ** ADAM_STEP)
    v_hat = v2 / (1.0 - ADAM_B2 ** ADAM_STEP)
    delta = -ADAM_LR * (m_hat / (jnp.sqrt(v_hat) + ADAM_EPS) + ADAM_WD * w)
    return delta, m2, v2


def _adamw_big(contrib, w, m, v, layer, name, prev=None):
    _, r, c = w.shape
    rh = r // 2
    tr = _pick_tile(rh, (256, 176, 128, 64, 8))
    nj = rh // tr
    blk = pl.BlockSpec((1, tr, c), lambda h, j: (layer, h * nj + j, 0))
    n_prev = 0 if prev is None else 4

    def body(*refs):
        g_ref, w_ref, m_ref, v_ref = refs[:4]
        go_ref, d_ref, mo_ref, vo_ref = refs[4 + n_prev:]
        g = g_ref[0, 0].astype(F32)
        for s in range(1, N_CHIPS):
            g = g + g_ref[0, s].astype(F32)
        delta, m2, v2 = _adamw_math(g, w_ref[0], m_ref[0], v_ref[0])
        go_ref[0] = g
        d_ref[0] = delta
        mo_ref[0] = m2
        vo_ref[0] = v2

    any_spec = pl.BlockSpec(memory_space=pl.ANY)
    return pl.pallas_call(
        body, name=name, grid=(2, nj),
        in_specs=[pl.BlockSpec((1, N_CHIPS, tr, c), lambda h, j: (h, 0, j, 0)), blk, blk, blk] + [any_spec] * n_prev,
        out_specs=[blk] * 4, out_shape=[jax.ShapeDtypeStruct(w.shape, F32)] * 4,
        input_output_aliases={4 + a: a for a in range(n_prev)},
        compiler_params=_cparams(("parallel", "parallel")),
    )(contrib, w, m, v, *([] if prev is None else prev))


def _sum_small(contrib):
    rows = contrib.shape[2]

    def body(g_ref, o_ref):
        total = g_ref[0, 0]
        for j in range(1, N_DEV):
            total = total + g_ref[j // N_CHIPS, j % N_CHIPS]
        o_ref[...] = total

    return pl.pallas_call(
        body, name="sum_small", out_shape=jax.ShapeDtypeStruct((rows, 128), F32),
        compiler_params=_cparams(),
    )(contrib)


def _adamw_small(gs, ws, ms, vs):
    n = len(gs)
    as2d = lambda a: a.reshape(1, -1) if a.ndim == 1 else a

    def body(*refs):
        g_refs, w_refs, m_refs, v_refs = refs[:n], refs[n:2 * n], refs[2 * n:3 * n], refs[3 * n:4 * n]
        d_refs, mo_refs, vo_refs = refs[4 * n:5 * n], refs[5 * n:6 * n], refs[6 * n:]
        for j in range(n):
            delta, m2, v2 = _adamw_math(g_refs[j][...], w_refs[j][...], m_refs[j][...], v_refs[j][...])
            d_refs[j][...] = delta
            mo_refs[j][...] = m2
            vo_refs[j][...] = v2

    ins = [as2d(a) for a in (*gs, *ws, *ms, *vs)]
    outs = pl.pallas_call(
        body, name="adamw_small", out_shape=[jax.ShapeDtypeStruct(a.shape, F32) for a in ins[:n]] * 3,
        compiler_params=_cparams(),
    )(*ins)
    back = lambda group: [o.reshape(g.shape) for o, g in zip(group, gs)]
    return back(outs[:n]), back(outs[n:2 * n]), back(outs[2 * n:])


PACK_ALIGN = 8 * 128


def _packed_rows(shape):
    n = 1
    for d in shape:
        n *= d
    return (n + PACK_ALIGN - 1) // PACK_ALIGN * 8


def _pack(arrays):
    parts = []
    for a in arrays:
        flat = a.reshape(-1)
        pad = _packed_rows(a.shape) * 128 - flat.shape[0]
        if pad:
            flat = jnp.concatenate([flat, jnp.zeros((pad,), F32)])
        parts.append(flat.reshape(-1, 128))
    return jnp.concatenate(parts, axis=0)


def _unpack(packed, shapes):
    out, row = [], 0
    for s in shapes:
        rows = _packed_rows(s)
        n = 1
        for d in s:
            n *= d
        out.append(packed[row:row + rows].reshape(-1)[:n].reshape(s))
        row += rows
    return out


SMALL_NAMES = ("norm1_w", "sgu_ln_w", "sgu_ln_b", "sgu_w_spatial", "sgu_b_spatial", "sc_conv_w", "dn_conv_w",
               "dn_a_log", "dn_dt_bias", "dn_norm_w", "gla_w_gate2", "gla_gate_bias", "gla_norm_w", "norm2_w",
               "final_norm_w")
SHARDED_SMALL = ("sc_conv_w", "dn_conv_w", "gla_w_gate2")
BIG_NAMES = ("w_in", "w_out", "w_gate_up", "w_down")
WEIGHT_ORDER = ("norm1_w", "w_in", "sgu_ln_w", "sgu_ln_b", "sgu_w_spatial", "sgu_b_spatial", "sc_conv_w", "dn_conv_w",
                "dn_a_log", "dn_dt_bias", "dn_norm_w", "gla_w_gate2", "gla_gate_bias", "gla_norm_w", "w_out",
                "norm2_w", "w_gate_up", "w_down", "final_norm_w")


def _cols_from_shards(g):
    l, n, r, c = g.shape
    return jnp.transpose(g, (0, 2, 1, 3)).reshape(l, r, n * c)


def _rows_from_shards(g):
    l, n, r, c = g.shape
    return g.reshape(l, n * r, c)


def kernel(x, norm1_w, w_in, sgu_ln_w, sgu_ln_b, sgu_w_spatial, sgu_b_spatial, sc_conv_w, dn_conv_w, dn_a_log, dn_dt_bias, dn_norm_w, gla_w_gate2, gla_gate_bias, gla_norm_w, w_out, norm2_w, w_gate_up, w_down, final_norm_w, loss_target, m_norm1_w, m_w_in, m_sgu_ln_w, m_sgu_ln_b, m_sgu_w_spatial, m_sgu_b_spatial, m_sc_conv_w, m_dn_conv_w, m_dn_a_log, m_dn_dt_bias, m_dn_norm_w, m_gla_w_gate2, m_gla_gate_bias, m_gla_norm_w, m_w_out, m_norm2_w, m_w_gate_up, m_w_down, m_final_norm_w, v_norm1_w, v_w_in, v_sgu_ln_w, v_sgu_ln_b, v_sgu_w_spatial, v_sgu_b_spatial, v_sc_conv_w, v_dn_conv_w, v_dn_a_log, v_dn_dt_bias, v_dn_norm_w, v_gla_w_gate2, v_gla_gate_bias, v_gla_norm_w, v_w_out, v_norm2_w, v_w_gate_up, v_w_down, v_final_norm_w):
    w = dict(norm1_w=norm1_w, w_in=w_in, sgu_ln_w=sgu_ln_w, sgu_ln_b=sgu_ln_b, sgu_w_spatial=sgu_w_spatial,
             sgu_b_spatial=sgu_b_spatial, sc_conv_w=sc_conv_w, dn_conv_w=dn_conv_w, dn_a_log=dn_a_log,
             dn_dt_bias=dn_dt_bias, dn_norm_w=dn_norm_w, gla_w_gate2=gla_w_gate2, gla_gate_bias=gla_gate_bias,
             gla_norm_w=gla_norm_w, w_out=w_out, norm2_w=norm2_w, w_gate_up=w_gate_up, w_down=w_down,
             final_norm_w=final_norm_w)
    m = dict(norm1_w=m_norm1_w, w_in=m_w_in, sgu_ln_w=m_sgu_ln_w, sgu_ln_b=m_sgu_ln_b, sgu_w_spatial=m_sgu_w_spatial,
             sgu_b_spatial=m_sgu_b_spatial, sc_conv_w=m_sc_conv_w, dn_conv_w=m_dn_conv_w, dn_a_log=m_dn_a_log,
             dn_dt_bias=m_dn_dt_bias, dn_norm_w=m_dn_norm_w, gla_w_gate2=m_gla_w_gate2,
             gla_gate_bias=m_gla_gate_bias, gla_norm_w=m_gla_norm_w, w_out=m_w_out, norm2_w=m_norm2_w,
             w_gate_up=m_w_gate_up, w_down=m_w_down, final_norm_w=m_final_norm_w)
    v = dict(norm1_w=v_norm1_w, w_in=v_w_in, sgu_ln_w=v_sgu_ln_w, sgu_ln_b=v_sgu_ln_b, sgu_w_spatial=v_sgu_w_spatial,
             sgu_b_spatial=v_sgu_b_spatial, sc_conv_w=v_sc_conv_w, dn_conv_w=v_dn_conv_w, dn_a_log=v_dn_a_log,
             dn_dt_bias=v_dn_dt_bias, dn_norm_w=v_dn_norm_w, gla_w_gate2=v_gla_w_gate2,
             gla_gate_bias=v_gla_gate_bias, gla_norm_w=v_gla_norm_w, w_out=v_w_out, norm2_w=v_norm2_w,
             w_gate_up=v_w_gate_up, w_down=v_w_down, final_norm_w=v_final_norm_w)
    chip = 2 * lax.axis_index("x") + lax.axis_index("y")

    w_in_wire = w["w_in"].astype(MXU_DTYPE)
    row_halves = lambda a: a.reshape(2, a.shape[0] // 2, a.shape[1])

    def full_w_in(g):
        return _pad_w_in(jnp.transpose(g, (0, 2, 1, 3)).reshape(D_MODEL, IN_COLS))

    first = [row_halves(w_in_wire[0])] + [w[n] for n in SHARDED_SMALL]
    gathered = _chip_exchange(first, ["layer"] * len(first), "gather_first")
    w_in_0 = full_w_in(gathered[0])
    big = dict(w_in=[w_in_0, None])
    small = {n: w[n] for n in SMALL_NAMES if n not in SHARDED_SMALL}
    for j, n in enumerate(SHARDED_SMALL):
        small[n] = _cols_from_shards(gathered[1 + j])
    late = [row_halves(w_in_wire[1])] + [w[n].astype(MXU_DTYPE) for n in ("w_out", "w_gate_up", "w_down")]

    def finish_late(g):
        return dict(w_in=[w_in_0, full_w_in(g[0])], w_out=_rows_from_shards(g[1]),
                    w_gate_up=_cols_from_shards(g[2]), w_down=_rows_from_shards(g[3]))

    small_shapes = [(DEPTH,) + w[n].shape[1:-1] + (w[n].shape[-1] * (N_CHIPS if n in SHARDED_SMALL else 1),)
                    if n != "final_norm_w" else w[n].shape for n in SMALL_NAMES] + [(1,)]

    def pack_small(gsmall, loss_tile):
        return _pack([gsmall[n] for n in SMALL_NAMES] + [loss_tile[0:1, 0]])

    _, grad_x, contribs, small_contrib = _local_step(
        x[0], loss_target[0], big, small, late_weights=(late, finish_late), exchange=True, small_extra=pack_small)

    out_g, out_d, out_m, out_v = {}, {}, {}, {}
    for j, n in enumerate(BIG_NAMES):
        outs = _adamw_big(contribs[(n, 1)], w[n], m[n], v[n], 1, "adamw_" + n + "_1")
        out_g[n], out_d[n], out_m[n], out_v[n] = _adamw_big(contribs[(n, 0)], w[n], m[n], v[n], 0,
                                                            "adamw_" + n + "_0", prev=outs)
    summed = _unpack(_sum_small(small_contrib), small_shapes)
    loss = summed[-1][0]
    for n, g in zip(SMALL_NAMES, summed[:-1]):
        if n in SHARDED_SMALL:
            cols = g.shape[-1] // N_CHIPS
            g = lax.dynamic_slice_in_dim(g, chip * cols, cols, axis=g.ndim - 1)
        out_g[n] = g
    d_s, m_s, v_s = _adamw_small([out_g[n] for n in SMALL_NAMES], [w[n] for n in SMALL_NAMES],
                                 [m[n] for n in SMALL_NAMES], [v[n] for n in SMALL_NAMES])
    for n, d_, m_, v_ in zip(SMALL_NAMES, d_s, m_s, v_s):
        out_d[n], out_m[n], out_v[n] = d_, m_, v_

    return (loss, grad_x[None], *[out_g[n] for n in WEIGHT_ORDER], *[out_d[n] for n in WEIGHT_ORDER],
            *[out_m[n] for n in WEIGHT_ORDER], *[out_v[n] for n in WEIGHT_ORDER])
```

```python
import functools

import jax
import jax.numpy as jnp
from jax import lax
from jax.experimental import pallas as pl
from jax.experimental.pallas import tpu as pltpu

F32 = jnp.float32
BF16 = jnp.bfloat16
MXU_DTYPE = jnp.bfloat16
GRAD_WIRE_DTYPE = jnp.bfloat16
HI = lax.Precision.HIGHEST
MESH = pl.DeviceIdType.MESH

D_MODEL = 1024
DEPTH = 2
GROUP = 256
HEADS = 4
HEAD_DIM = 64
SGU_CHUNK = 128
SCAN_CHUNK = 64
D_FF = 2816
EPS = 1e-6
IN_COLS = 3352
P_COLS = 3584
HALO = 8
N_CHIPS = 4
N_DEV = 8
VMEM_LIMIT = 56 * 1024 * 1024

ADAM_LR = 0.001
ADAM_B1 = 0.9
ADAM_B2 = 0.999
ADAM_EPS = 1e-08
ADAM_WD = 0.01
ADAM_STEP = 10

(COL_AU, COL_AV, COL_BB, COL_BC, COL_BH, COL_CQ, COL_CK, COL_CV, COL_CZ,
 COL_DQ, COL_DK, COL_DV, COL_DZ) = range(13)
COL128_SMALL_C = 26
COL128_SMALL_D = 27


def _cparams(sem=None):
    return pltpu.CompilerParams(dimension_semantics=sem, vmem_limit_bytes=VMEM_LIMIT)


def _iota(shape, dim):
    return lax.broadcasted_iota(jnp.int32, shape, dim)


def _dg(a, b, ca, cb, prec=None):
    return lax.dot_general(a, b, (((ca,), (cb,)), ((), ())), preferred_element_type=F32, precision=prec)


@functools.partial(jax.custom_vjp, nondiff_argnums=(2, 3))
def bdot(a, b, ca, cb):
    return _dg(a.astype(MXU_DTYPE), b.astype(MXU_DTYPE), ca, cb)


def _bdot_fwd(a, b, ca, cb):
    return bdot(a, b, ca, cb), (a, b)


def _bdot_bwd(ca, cb, res, g):
    a, b = res
    if ca == 1:
        da = bdot(g, b, 1, 1 if cb == 0 else 0)
    else:
        da = bdot(b, g, 1 if cb == 0 else 0, 1)
    if cb == 0:
        db = bdot(a, g, 0, 0) if ca == 1 else bdot(a, g, 1, 0)
    else:
        db = bdot(g, a, 0, 0) if ca == 1 else bdot(g, a, 0, 1)
    return da, db


bdot.defvjp(_bdot_fwd, _bdot_bwd)


def _pieces(a, n):
    out, r = [], a
    for i in range(n):
        p = r.astype(MXU_DTYPE)
        out.append(p)
        if i + 1 < n:
            r = r - p.astype(F32)
    return out


def _mdot_impl(a, b, ca, cb, sa, sb):
    total = None
    for i, x in enumerate(_pieces(a, sa)):
        for j, y in enumerate(_pieces(b, sb)):
            if i + j < max(sa, sb):
                t = _dg(x, y, ca, cb)
                total = t if total is None else total + t
    return total


@functools.partial(jax.custom_vjp, nondiff_argnums=(2, 3, 4, 5))
def mdot(a, b, ca, cb, sa, sb):
    return _mdot_impl(a, b, ca, cb, sa, sb)


def _mdot_fwd(a, b, ca, cb, sa, sb):
    return _mdot_impl(a, b, ca, cb, sa, sb), (a, b)


def _mdot_bwd(ca, cb, sa, sb, res, g):
    a, b = res
    ga, gb = (3 if sb == 1 else 2), (3 if sa == 1 else 2)
    if sa == 1:
        da = jnp.zeros_like(a)
    elif ca == 1:
        da = mdot(g, b, 1, 1 if cb == 0 else 0, ga, sb)
    else:
        da = mdot(b, g, 1 if cb == 0 else 0, 1, sb, ga)
    if sb == 1:
        db = jnp.zeros_like(b)
    elif cb == 0:
        db = mdot(a, g, 0, 0, sa, gb) if ca == 1 else mdot(a, g, 1, 0, sa, gb)
    else:
        db = mdot(g, a, 0, 0, gb, sa) if ca == 1 else mdot(g, a, 0, 1, gb, sa)
    return da, db


mdot.defvjp(_mdot_fwd, _mdot_bwd)


def mask_r(a, m, ca=1, cb=0):
    return mdot(a, m, ca, cb, 3, 1)


def mask_l(m, b, ca=1, cb=0):
    return mdot(m, b, ca, cb, 1, 3)


def ddot(a, b, ca=1, cb=0):
    return mdot(a, b, ca, cb, 2, 2)


def _head_mask(h):
    return ((_iota((1, GROUP), 1) >> 6) == h).astype(F32)


def _block_diag_mask():
    return ((_iota((GROUP, GROUP), 0) >> 6) == (_iota((GROUP, GROUP), 1) >> 6)).astype(F32)


def _expand_mat(offset):
    return ((_iota((128, GROUP), 0) - offset) == (_iota((128, GROUP), 1) >> 6)).astype(F32)


def _tril(n, strict=False):
    r, c = _iota((n, n), 0), _iota((n, n), 1)
    return (r > c) if strict else (r >= c)


def _row_pick(x, row):
    return jnp.sum(jnp.where(_iota(x.shape, 0) == row, x, 0.0), axis=0, keepdims=True)


def _shift_rows_impl(x, halo, j):
    n = x.shape[0]
    r = _iota(x.shape, 0)
    top = jnp.concatenate([pltpu.roll(halo, j, 0), jnp.zeros((n - HALO, x.shape[1]), x.dtype)], axis=0)
    return jnp.where(r >= j, pltpu.roll(x, j, 0), top)


def _mxu_round(a):
    return a.astype(MXU_DTYPE).astype(F32)


@functools.partial(jax.custom_vjp, nondiff_argnums=(3,))
def _causal_conv(x, halo, w, width):
    xb, hb, wb = _mxu_round(x), _mxu_round(halo), _mxu_round(w)
    out = xb * _row_pick(wb, width - 1)
    for j in range(1, width):
        out = out + _shift_rows_impl(xb, hb, j) * _row_pick(wb, width - 1 - j)
    return out


def _causal_conv_fwd(x, halo, w, width):
    return _causal_conv(x, halo, w, width), (x, halo, w)


def _causal_conv_bwd(width, res, g):
    x, halo, w = res
    xb, hb, wb, gb = _mxu_round(x), _mxu_round(halo), _mxu_round(w), _mxu_round(g)
    n = g.shape[0]
    rows, rows8 = _iota(g.shape, 0), _iota(halo.shape, 0)
    dx = gb * _row_pick(wb, width - 1)
    dh = jnp.zeros_like(halo)
    dw = jnp.where(rows8 == width - 1, jnp.sum(xb * gb, axis=0, keepdims=True), 0.0)
    for j in range(1, width):
        gj = gb * _row_pick(wb, width - 1 - j)
        dx = dx + jnp.where(rows < n - j, pltpu.roll(gj, n - j, 0), 0.0)
        dh = dh + jnp.where(rows8 >= HALO - j, pltpu.roll(gj[0:HALO], HALO - j, 0), 0.0)
        tap = jnp.sum(_shift_rows_impl(xb, hb, j) * gb, axis=0, keepdims=True)
        dw = dw + jnp.where(rows8 == width - 1 - j, tap, 0.0)
    return dx, dh, dw


_causal_conv.defvjp(_causal_conv_fwd, _causal_conv_bwd)


def _head_sum(x, bd):
    return mask_r(x, bd)


def _softplus(x):
    return jnp.maximum(x, 0.0) + jnp.log1p(jnp.exp(-jnp.abs(x)))


def _log_sigmoid(x):
    return -_softplus(-x)


def _silu(x):
    return x * jax.nn.sigmoid(x)


def _head_rmsnorm_gate(o, nw, z, bd):
    ms = _head_sum(o * o, bd) * (1.0 / HEAD_DIM)
    return o * lax.rsqrt(ms + EPS) * nw * _silu(z)


def _sgu_chunk(pu, pv, ln_w, ln_b, ws0, ws1, ws2, ws3, bs_t):
    u = jax.nn.gelu(pu)
    g = jax.nn.gelu(pv)
    mu = jnp.mean(g, axis=-1, keepdims=True)
    var = jnp.mean(jnp.square(g - mu), axis=-1, keepdims=True)
    v = (g - mu) * lax.rsqrt(var + EPS) * ln_w + ln_b
    keep = _tril(SGU_CHUNK)
    bias = mask_r(bs_t, _expand_mat(0))
    causal_ws = [jnp.where(keep, ws, 0.0) for ws in (ws0, ws1, ws2, ws3)]
    mixed = []
    for ci in range(pu.shape[0] // SGU_CHUNK):
        v_c = v[ci * SGU_CHUNK:(ci + 1) * SGU_CHUNK]
        m_c = bias
        for h in range(HEADS):
            m_c = m_c + _head_mask(h) * bdot(causal_ws[h], v_c, 1, 0)
        mixed.append(m_c)
    return u * jnp.concatenate(mixed, axis=0)


def _sc_chunk(pb, pc, ph, halo_c, halo_h, cw):
    return pb * _causal_conv(pc * ph, halo_c * halo_h, cw, 3)


def _neumann_inverses(lows):
    n = lows[0].shape[0]
    eye = (_iota((n, n), 0) == _iota((n, n), 1)).astype(F32)
    a = [-low for low in lows]
    t = [eye + x for x in a]
    for _ in range(5):
        a = [ddot(x, x) for x in a]
        t = [ti + ddot(ti, ai) for ti, ai in zip(t, a)]
    return t


@jax.custom_vjp
def _saved_inverse(low, inv):
    return inv


def _saved_inverse_fwd(low, inv):
    return inv, inv


def _saved_inverse_bwd(inv, g):
    return -ddot(ddot(inv, g, 0, 0), inv, 1, 1), jnp.zeros_like(inv)


_saved_inverse.defvjp(_saved_inverse_fwd, _saved_inverse_bwd)


def _chunk_tril(rows):
    r, c = _iota((rows, rows), 0), _iota((rows, rows), 1)
    return ((r >> 6) == (c >> 6)) & (r >= c)


def _dn_block(pq, pk, pv, hq, hk, hv, small, pz, cwq, cwk, cwv, a_log, dt_bias, nw, state, saved_inv=None):
    c = SCAN_CHUNK
    rows = pq.shape[0]
    bd = _block_diag_mask()
    q = _silu(_causal_conv(pq, hq, cwq, 4))
    k = _silu(_causal_conv(pk, hk, cwk, 4))
    v = _silu(_causal_conv(pv, hv, cwv, 4))
    q = q * lax.rsqrt(_head_sum(q * q, bd) + EPS) * (HEAD_DIM ** -0.5)
    k = k * lax.rsqrt(_head_sum(k * k, bd) + EPS)
    lane = _iota((1, 128), 1)
    g = jnp.where(lane < HEADS, -jnp.exp(a_log) * _softplus(small + dt_bias), 0.0)
    beta_b = mask_r(jax.nn.sigmoid(small), _expand_mat(HEADS))
    gc_all = mask_l(_chunk_tril(rows).astype(F32), g)
    gcb_all = mask_r(gc_all, _expand_mat(0))
    kb_all = k * beta_b
    vb_all = v * beta_b
    kbe_all = kb_all * jnp.exp(gcb_all)
    qg_all = q * jnp.exp(gcb_all)
    causal, strict = _tril(c), _tril(c, strict=True)
    nc = rows // c
    pairs = [(ci, h) for ci in range(nc) for h in range(HEADS)]
    sls = [slice(ci * c, (ci + 1) * c) for ci in range(nc)]
    decays, lows, attns = [], [], []
    for ci, h in pairs:
        gc = gc_all[sls[ci]]
        onehot = (_iota((c, 128), 1) == h).astype(F32)
        col = mask_l(onehot, gc, 1, 1)
        row = jnp.sum(gc * onehot, axis=1, keepdims=True)
        decays.append(jnp.exp(jnp.where(causal, row - col, -jnp.inf)))
    for j, (ci, h) in enumerate(pairs):
        mh = _head_mask(h)
        k_c = k[sls[ci]]
        lows.append(jnp.where(strict, bdot(kb_all[sls[ci]] * mh, k_c, 1, 1) * decays[j], 0.0))
        attns.append(bdot(q[sls[ci]] * mh, k_c, 1, 1) * decays[j])
    if saved_inv is None:
        invs = _neumann_inverses(lows)
    else:
        invs = [_saved_inverse(low, s) for low, s in zip(lows, saved_inv)]
    us, ws = [], []
    for ci in range(nc):
        u = jnp.zeros((c, GROUP), F32)
        w = jnp.zeros((c, GROUP), F32)
        for h in range(HEADS):
            mh = _head_mask(h)
            u = u + mh * ddot(invs[ci * HEADS + h], vb_all[sls[ci]])
            w = w + mh * ddot(invs[ci * HEADS + h], kbe_all[sls[ci]])
        us.append(u)
        ws.append(w)
    outs = []
    for ci in range(nc):
        gc_b = gcb_all[sls[ci]]
        gc_last_b = _row_pick(gc_b, c - 1)
        v_new = us[ci] - bdot(ws[ci], state, 1, 0)
        o = bdot(qg_all[sls[ci]], state, 1, 0)
        for h in range(HEADS):
            o = o + _head_mask(h) * bdot(attns[ci * HEADS + h], v_new, 1, 0)
        k_dec = k[sls[ci]] * jnp.exp(gc_last_b - gc_b)
        state = state * jnp.exp(gc_last_b) + bd * bdot(k_dec, v_new, 0, 0)
        outs.append(o)
    o = jnp.concatenate(outs, axis=0)
    return _head_rmsnorm_gate(o, nw, pz, bd), state, invs


def _gla_chunk(pq, pk, pv, small, pz, w2, gbias, nw, state_t):
    c = SCAN_CHUNK
    rows = pq.shape[0]
    nc = rows // c
    sls = [slice(ci * c, (ci + 1) * c) for ci in range(nc)]
    bd = _block_diag_mask()
    log_a = _log_sigmoid(bdot(small, w2, 1, 0) + gbias) * (1.0 / 16.0)
    gcum = mask_l(_chunk_tril(rows).astype(F32), log_a)
    r, s = _iota((rows, rows), 0), _iota((rows, rows), 1)
    base = (r >> 6) << 6
    g_mid = mask_l((s == base + c // 2).astype(F32), gcum)
    g_last = mask_l((s == base + c - 1).astype(F32), gcum)
    q = pq * (HEAD_DIM ** -0.5)
    qa = q * jnp.exp(gcum - g_mid)
    ka = pk * jnp.exp(g_mid - gcum)
    qg = q * jnp.exp(gcum)
    k_last = pk * jnp.exp(g_last - gcum)
    causal = _tril(c)
    attns = [jnp.where(causal, bdot(qa[sls[ci]] * _head_mask(h), ka[sls[ci]], 1, 1), 0.0)
             for ci in range(nc) for h in range(HEADS)]
    intra = []
    for ci in range(nc):
        o = jnp.zeros((c, GROUP), F32)
        for h in range(HEADS):
            o = o + _head_mask(h) * bdot(attns[ci * HEADS + h], pv[sls[ci]], 1, 0)
        intra.append(o)
    kvs = [bd * bdot(pv[sls[ci]], k_last[sls[ci]], 0, 0) for ci in range(nc)]
    states = []
    for ci in range(nc):
        states.append(state_t)
        state_t = state_t * jnp.exp(_row_pick(g_last[sls[ci]], 0)) + kvs[ci]
    outs = [intra[ci] + bdot(qg[sls[ci]], states[ci], 1, 1) for ci in range(nc)]
    o = jnp.concatenate(outs, axis=0)
    return _head_rmsnorm_gate(o, nw, pz, bd), state_t


def _col_spec(rows, group, rev_n=None):
    if rev_n is None:
        return pl.BlockSpec((rows, GROUP), lambda i: (i, group))
    return pl.BlockSpec((rows, GROUP), lambda i: (rev_n - 1 - i, group))


def _small_spec(rows, group128, rev_n=None):
    if rev_n is None:
        return pl.BlockSpec((rows, 128), lambda i: (i, group128))
    return pl.BlockSpec((rows, 128), lambda i: (rev_n - 1 - i, group128))


def _halo_spec(rows, group, rev_n=None):
    per = rows // HALO
    if rev_n is None:
        return pl.BlockSpec((HALO, GROUP), lambda i: (jnp.maximum(i * per - 1, 0), group))
    return pl.BlockSpec((HALO, GROUP), lambda i: (jnp.maximum((rev_n - 1 - i) * per - 1, 0), group))


def _full_spec(shape):
    nd = len(shape)
    return pl.BlockSpec(shape, lambda i: (0,) * nd)


def _out_rows_spec(rows, lanes, rev_n=None):
    if rev_n is None:
        return pl.BlockSpec((rows, lanes), lambda i: (i, 0))
    return pl.BlockSpec((rows, lanes), lambda i: (rev_n - 1 - i, 0))


SGU_ROWS = 4 * SGU_CHUNK


def _sgu_fwd(p, ln_w, ln_b, ws, bs_t):
    t = p.shape[0]
    n = t // SGU_ROWS

    def body(pu_ref, pv_ref, lw_ref, lb_ref, ws_ref, bs_ref, y_ref):
        y = _sgu_chunk(pu_ref[...], pv_ref[...], lw_ref[...], lb_ref[...],
                       ws_ref[0], ws_ref[1], ws_ref[2], ws_ref[3], bs_ref[...])
        y_ref[...] = y.astype(y_ref.dtype)

    return pl.pallas_call(
        body, name="sgu_fwd", grid=(n,),
        in_specs=[_col_spec(SGU_ROWS, COL_AU), _col_spec(SGU_ROWS, COL_AV), _full_spec((1, GROUP)),
                  _full_spec((1, GROUP)), _full_spec((HEADS, SGU_CHUNK, SGU_CHUNK)), _full_spec((SGU_CHUNK, 128))],
        out_specs=_out_rows_spec(SGU_ROWS, GROUP),
        out_shape=jax.ShapeDtypeStruct((t, GROUP), BF16),
        compiler_params=_cparams(("arbitrary",)),
    )(p, p, ln_w, ln_b, ws, bs_t)


def _sgu_bwd(p, dmix, ln_w, ln_b, ws, bs_t):
    t = p.shape[0]
    n = t // SGU_ROWS

    def body(pu_ref, pv_ref, dy_ref, lw_ref, lb_ref, ws_ref, bs_ref,
             dpu_ref, dpv_ref, dlw_ref, dlb_ref, dws_ref, dbs_ref):
        args = (pu_ref[...], pv_ref[...], lw_ref[...], lb_ref[...],
                ws_ref[0], ws_ref[1], ws_ref[2], ws_ref[3], bs_ref[...])
        _, vjp = jax.vjp(_sgu_chunk, *args)
        dpu, dpv, dlw, dlb, d0, d1, d2, d3, dbs = vjp(dy_ref[...])
        dpu_ref[...] = dpu.astype(dpu_ref.dtype)
        dpv_ref[...] = dpv.astype(dpv_ref.dtype)

        @pl.when(pl.program_id(0) == 0)
        def _():
            dlw_ref[...] = jnp.zeros_like(dlw_ref)
            dlb_ref[...] = jnp.zeros_like(dlb_ref)
            dws_ref[...] = jnp.zeros_like(dws_ref)
            dbs_ref[...] = jnp.zeros_like(dbs_ref)

        dlw_ref[...] += dlw
        dlb_ref[...] += dlb
        for h, d in enumerate((d0, d1, d2, d3)):
            dws_ref[h] += d
        dbs_ref[...] += dbs

    return pl.pallas_call(
        body, name="sgu_bwd", grid=(n,),
        in_specs=[_col_spec(SGU_ROWS, COL_AU), _col_spec(SGU_ROWS, COL_AV),
                  pl.BlockSpec((SGU_ROWS, GROUP), lambda i: (i, 0)),
                  _full_spec((1, GROUP)), _full_spec((1, GROUP)), _full_spec((HEADS, SGU_CHUNK, SGU_CHUNK)),
                  _full_spec((SGU_CHUNK, 128))],
        out_specs=[_out_rows_spec(SGU_ROWS, GROUP), _out_rows_spec(SGU_ROWS, GROUP), _full_spec((1, GROUP)),
                   _full_spec((1, GROUP)), _full_spec((HEADS, SGU_CHUNK, SGU_CHUNK)), _full_spec((SGU_CHUNK, 128))],
        out_shape=[jax.ShapeDtypeStruct((t, GROUP), BF16), jax.ShapeDtypeStruct((t, GROUP), BF16),
                   jax.ShapeDtypeStruct((1, GROUP), F32), jax.ShapeDtypeStruct((1, GROUP), F32),
                   jax.ShapeDtypeStruct((HEADS, SGU_CHUNK, SGU_CHUNK), F32),
                   jax.ShapeDtypeStruct((SGU_CHUNK, 128), F32)],
        compiler_params=_cparams(("arbitrary",)),
    )(p, p, dmix, ln_w, ln_b, ws, bs_t)


SC_ROWS = 512


def _first_block_zero(halo, first):
    return jnp.where(first, 0.0, halo)


def _sc_fwd(p, cw):
    t = p.shape[0]
    n = t // SC_ROWS

    def body(pb_ref, pc_ref, ph_ref, hc_ref, hh_ref, cw_ref, y_ref):
        first = pl.program_id(0) == 0
        y = _sc_chunk(pb_ref[...], pc_ref[...], ph_ref[...], _first_block_zero(hc_ref[...], first),
                      _first_block_zero(hh_ref[...], first), cw_ref[...])
        y_ref[...] = y.astype(y_ref.dtype)

    return pl.pallas_call(
        body, name="sc_fwd", grid=(n,),
        in_specs=[_col_spec(SC_ROWS, COL_BB), _col_spec(SC_ROWS, COL_BC), _col_spec(SC_ROWS, COL_BH),
                  _halo_spec(SC_ROWS, COL_BC), _halo_spec(SC_ROWS, COL_BH), _full_spec((HALO, GROUP))],
        out_specs=_out_rows_spec(SC_ROWS, GROUP),
        out_shape=jax.ShapeDtypeStruct((t, GROUP), BF16),
        compiler_params=_cparams(("arbitrary",)),
    )(p, p, p, p, p, cw)


def _add_halo_grad(d, carry):
    return d + jnp.concatenate([jnp.zeros((d.shape[0] - HALO, d.shape[1]), d.dtype), carry], axis=0)


def _sc_bwd(p, dmix, cw):
    t = p.shape[0]
    n = t // SC_ROWS

    def body(pb_ref, pc_ref, ph_ref, hc_ref, hh_ref, dy_ref, cw_ref,
             dpb_ref, dpc_ref, dph_ref, dcw_ref, carry_c, carry_h):
        i = pl.program_id(0)
        first = i == n - 1

        @pl.when(i == 0)
        def _():
            carry_c[...] = jnp.zeros_like(carry_c)
            carry_h[...] = jnp.zeros_like(carry_h)
            dcw_ref[...] = jnp.zeros_like(dcw_ref)

        args = (pb_ref[...], pc_ref[...], ph_ref[...], _first_block_zero(hc_ref[...], first),
                _first_block_zero(hh_ref[...], first), cw_ref[...])
        _, vjp = jax.vjp(_sc_chunk, *args)
        dpb, dpc, dph, dhc, dhh, dcw = vjp(dy_ref[...])
        dpb_ref[...] = dpb.astype(dpb_ref.dtype)
        dpc_ref[...] = _add_halo_grad(dpc, carry_c[...]).astype(dpc_ref.dtype)
        dph_ref[...] = _add_halo_grad(dph, carry_h[...]).astype(dph_ref.dtype)
        carry_c[...] = dhc
        carry_h[...] = dhh
        dcw_ref[...] += dcw

    return pl.pallas_call(
        body, name="sc_bwd", grid=(n,),
        in_specs=[_col_spec(SC_ROWS, COL_BB, n), _col_spec(SC_ROWS, COL_BC, n), _col_spec(SC_ROWS, COL_BH, n),
                  _halo_spec(SC_ROWS, COL_BC, n), _halo_spec(SC_ROWS, COL_BH, n),
                  pl.BlockSpec((SC_ROWS, GROUP), lambda i: (n - 1 - i, 1)), _full_spec((HALO, GROUP))],
        out_specs=[_out_rows_spec(SC_ROWS, GROUP, n)] * 3 + [_full_spec((HALO, GROUP))],
        out_shape=[jax.ShapeDtypeStruct((t, GROUP), BF16)] * 3 + [jax.ShapeDtypeStruct((HALO, GROUP), F32)],
        scratch_shapes=[pltpu.VMEM((HALO, GROUP), F32), pltpu.VMEM((HALO, GROUP), F32)],
        compiler_params=_cparams(("arbitrary",)),
    )(p, p, p, p, p, dmix, cw)


SCAN_STEP_CHUNKS = 4
SCAN_ROWS = SCAN_STEP_CHUNKS * SCAN_CHUNK


def _host_call(body, hosted, *, name, grid, in_specs, out_specs, out_shape, scratch_shapes, args):
    params = _cparams(("arbitrary",))
    if hosted is None:
        outs = pl.pallas_call(body, name=name, grid=grid, in_specs=in_specs, out_specs=out_specs,
                              out_shape=out_shape, scratch_shapes=scratch_shapes, compiler_params=params)(*args)
        return outs, None
    stage, arrays, modes, bufs = hosted
    arrays = list(arrays) if stage == 1 else []
    n_in, n_out, n_scr, n_src, na = len(in_specs), len(out_specs), len(scratch_shapes), len(arrays), len(bufs)
    last = grid[0] - 1

    def new_body(*refs):
        srcs = refs[n_in:n_in + n_src]
        o0 = n_in + n_src + na
        ex = refs[o0 + n_out:o0 + n_out + na]
        s0 = o0 + n_out + na
        sems = refs[s0 + n_scr:]
        i = pl.program_id(0)

        def run(what):
            if stage == 1:
                _exchange_stage1(srcs, ex, modes, sems[0], sems[1], what)
            else:
                _exchange_stage2(ex, sems[0], sems[1], what)

        @pl.when(i == 0)
        def _():
            run("start")

        body(*refs[:n_in], *refs[o0:o0 + n_out], *refs[s0:s0 + n_scr])

        @pl.when(i == last)
        def _():
            run("wait")

    any_spec = pl.BlockSpec(memory_space=pl.ANY)
    outs = pl.pallas_call(
        new_body, name=name, grid=grid,
        in_specs=list(in_specs) + [any_spec] * (n_src + na), out_specs=list(out_specs) + [any_spec] * na,
        out_shape=list(out_shape) + [jax.ShapeDtypeStruct(b.shape, b.dtype) for b in bufs],
        input_output_aliases={n_in + n_src + a: n_out + a for a in range(na)},
        scratch_shapes=list(scratch_shapes) + (_stage1_sems(na) if stage == 1 else _stage2_sems(na)),
        compiler_params=params,
    )(*args, *arrays, *bufs)
    return outs[:n_out], outs[n_out:]


def _dn_fwd(p, cw3, a_log, dt_bias, nw, hosted=None):
    t = p.shape[0]
    r = SCAN_ROWS
    n = t // r

    def body(pq_ref, pk_ref, pv_ref, hq_ref, hk_ref, hv_ref, sm_ref, pz_ref, cw_ref, al_ref, dt_ref, nw_ref,
             y_ref, ck_ref, inv_ref, state):
        first = pl.program_id(0) == 0

        @pl.when(first)
        def _():
            state[...] = jnp.zeros_like(state)

        s_in = state[...]
        ck_ref[0] = s_in
        y, s_out, invs = _dn_block(pq_ref[...], pk_ref[...], pv_ref[...], _first_block_zero(hq_ref[...], first),
                                   _first_block_zero(hk_ref[...], first), _first_block_zero(hv_ref[...], first),
                                   sm_ref[...], pz_ref[...], cw_ref[0], cw_ref[1], cw_ref[2],
                                   al_ref[...], dt_ref[...], nw_ref[...], s_in)
        y_ref[...] = y.astype(y_ref.dtype)
        state[...] = s_out
        for j, inv in enumerate(invs):
            inv_ref[j] = inv

    nh = SCAN_STEP_CHUNKS * HEADS
    return _host_call(
        body, hosted, name="dn_fwd", grid=(n,),
        in_specs=[_col_spec(r, COL_CQ), _col_spec(r, COL_CK), _col_spec(r, COL_CV),
                  _halo_spec(r, COL_CQ), _halo_spec(r, COL_CK), _halo_spec(r, COL_CV),
                  _small_spec(r, COL128_SMALL_C), _col_spec(r, COL_CZ), _full_spec((3, HALO, GROUP)),
                  _full_spec((1, 128)), _full_spec((1, 128)), _full_spec((1, GROUP))],
        out_specs=[_out_rows_spec(r, GROUP), pl.BlockSpec((1, GROUP, GROUP), lambda i: (i, 0, 0)),
                   pl.BlockSpec((nh, SCAN_CHUNK, SCAN_CHUNK), lambda i: (i, 0, 0))],
        out_shape=[jax.ShapeDtypeStruct((t, GROUP), BF16), jax.ShapeDtypeStruct((n, GROUP, GROUP), F32),
                   jax.ShapeDtypeStruct((n * nh, SCAN_CHUNK, SCAN_CHUNK), F32)],
        scratch_shapes=[pltpu.VMEM((GROUP, GROUP), F32)],
        args=(p, p, p, p, p, p, p, p, cw3, a_log, dt_bias, nw))


def _dn_bwd(p, dmix, states, invs, cw3, a_log, dt_bias, nw, hosted=None):
    t = p.shape[0]
    c = SCAN_ROWS
    n = t // c
    nh = SCAN_STEP_CHUNKS * HEADS

    def body(pq_ref, pk_ref, pv_ref, hq_ref, hk_ref, hv_ref, sm_ref, pz_ref, dy_ref, ck_ref, inv_ref,
             cw_ref, al_ref, dt_ref, nw_ref,
             dpq_ref, dpk_ref, dpv_ref, dsm_ref, dpz_ref, dcw_ref, dal_ref, ddt_ref, dnw_ref,
             dstate, carry):
        i = pl.program_id(0)
        first = i == n - 1

        @pl.when(i == 0)
        def _():
            dstate[...] = jnp.zeros_like(dstate)
            carry[...] = jnp.zeros_like(carry)
            dcw_ref[...] = jnp.zeros_like(dcw_ref)
            dal_ref[...] = jnp.zeros_like(dal_ref)
            ddt_ref[...] = jnp.zeros_like(ddt_ref)
            dnw_ref[...] = jnp.zeros_like(dnw_ref)

        args = (pq_ref[...], pk_ref[...], pv_ref[...], _first_block_zero(hq_ref[...], first),
                _first_block_zero(hk_ref[...], first), _first_block_zero(hv_ref[...], first),
                sm_ref[...], pz_ref[...], cw_ref[0], cw_ref[1], cw_ref[2],
                al_ref[...], dt_ref[...], nw_ref[...], ck_ref[0])
        saved = [inv_ref[j] for j in range(nh)]
        _, vjp = jax.vjp(lambda *a: _dn_block(*a, saved_inv=saved)[:2], *args)
        (dpq, dpk, dpv, dhq, dhk, dhv, dsm, dpz, dcq, dck, dcv, dal, ddt, dnw, dst) = vjp(
            (dy_ref[...], dstate[...]))
        dpq_ref[...] = _add_halo_grad(dpq, carry[0]).astype(dpq_ref.dtype)
        dpk_ref[...] = _add_halo_grad(dpk, carry[1]).astype(dpk_ref.dtype)
        dpv_ref[...] = _add_halo_grad(dpv, carry[2]).astype(dpv_ref.dtype)
        dsm_ref[...] = dsm.astype(dsm_ref.dtype)
        dpz_ref[...] = dpz.astype(dpz_ref.dtype)
        carry[0] = dhq
        carry[1] = dhk
        carry[2] = dhv
        dstate[...] = dst
        dcw_ref[0] += dcq
        dcw_ref[1] += dck
        dcw_ref[2] += dcv
        dal_ref[...] += dal
        ddt_ref[...] += ddt
        dnw_ref[...] += dnw

    return _host_call(
        body, hosted, name="dn_bwd", grid=(n,),
        in_specs=[_col_spec(c, COL_CQ, n), _col_spec(c, COL_CK, n), _col_spec(c, COL_CV, n),
                  _halo_spec(c, COL_CQ, n), _halo_spec(c, COL_CK, n), _halo_spec(c, COL_CV, n),
                  _small_spec(c, COL128_SMALL_C, n), _col_spec(c, COL_CZ, n),
                  pl.BlockSpec((c, GROUP), lambda i: (n - 1 - i, 2)),
                  pl.BlockSpec((1, GROUP, GROUP), lambda i: (n - 1 - i, 0, 0)),
                  pl.BlockSpec((nh, SCAN_CHUNK, SCAN_CHUNK), lambda i: (n - 1 - i, 0, 0)),
                  _full_spec((3, HALO, GROUP)), _full_spec((1, 128)), _full_spec((1, 128)), _full_spec((1, GROUP))],
        out_specs=[_out_rows_spec(c, GROUP, n)] * 3 + [_out_rows_spec(c, 128, n), _out_rows_spec(c, GROUP, n),
                   _full_spec((3, HALO, GROUP)), _full_spec((1, 128)), _full_spec((1, 128)), _full_spec((1, GROUP))],
        out_shape=[jax.ShapeDtypeStruct((t, GROUP), BF16)] * 3 + [
            jax.ShapeDtypeStruct((t, 128), BF16), jax.ShapeDtypeStruct((t, GROUP), BF16),
            jax.ShapeDtypeStruct((3, HALO, GROUP), F32), jax.ShapeDtypeStruct((1, 128), F32),
            jax.ShapeDtypeStruct((1, 128), F32), jax.ShapeDtypeStruct((1, GROUP), F32)],
        scratch_shapes=[pltpu.VMEM((GROUP, GROUP), F32), pltpu.VMEM((3, HALO, GROUP), F32)],
        args=(p, p, p, p, p, p, p, p, dmix, states, invs, cw3, a_log, dt_bias, nw))


def _gla_fwd(p, w2, gbias, nw, hosted=None):
    t = p.shape[0]
    c = SCAN_ROWS
    n = t // c

    def body(pq_ref, pk_ref, pv_ref, sm_ref, pz_ref, w2_ref, gb_ref, nw_ref, y_ref, ck_ref, state):
        @pl.when(pl.program_id(0) == 0)
        def _():
            state[...] = jnp.zeros_like(state)

        s_in = state[...]
        ck_ref[0] = s_in
        y, s_out = _gla_chunk(pq_ref[...], pk_ref[...], pv_ref[...], sm_ref[...], pz_ref[...],
                              w2_ref[...], gb_ref[...], nw_ref[...], s_in)
        y_ref[...] = y.astype(y_ref.dtype)
        state[...] = s_out

    return _host_call(
        body, hosted, name="gla_fwd", grid=(n,),
        in_specs=[_col_spec(c, COL_DQ), _col_spec(c, COL_DK), _col_spec(c, COL_DV),
                  _small_spec(c, COL128_SMALL_D), _col_spec(c, COL_DZ),
                  _full_spec((128, GROUP)), _full_spec((1, GROUP)), _full_spec((1, GROUP))],
        out_specs=[_out_rows_spec(c, GROUP), pl.BlockSpec((1, GROUP, GROUP), lambda i: (i, 0, 0))],
        out_shape=[jax.ShapeDtypeStruct((t, GROUP), BF16), jax.ShapeDtypeStruct((n, GROUP, GROUP), F32)],
        scratch_shapes=[pltpu.VMEM((GROUP, GROUP), F32)],
        args=(p, p, p, p, p, w2, gbias, nw))


def _gla_bwd(p, dmix, states, w2, gbias, nw, hosted=None):
    t = p.shape[0]
    c = SCAN_ROWS
    n = t // c

    def body(pq_ref, pk_ref, pv_ref, sm_ref, pz_ref, dy_ref, ck_ref, w2_ref, gb_ref, nw_ref,
             dpq_ref, dpk_ref, dpv_ref, dsm_ref, dpz_ref, dw2_ref, dgb_ref, dnw_ref, dstate):
        @pl.when(pl.program_id(0) == 0)
        def _():
            dstate[...] = jnp.zeros_like(dstate)
            dw2_ref[...] = jnp.zeros_like(dw2_ref)
            dgb_ref[...] = jnp.zeros_like(dgb_ref)
            dnw_ref[...] = jnp.zeros_like(dnw_ref)

        args = (pq_ref[...], pk_ref[...], pv_ref[...], sm_ref[...], pz_ref[...],
                w2_ref[...], gb_ref[...], nw_ref[...], ck_ref[0])
        _, vjp = jax.vjp(_gla_chunk, *args)
        dpq, dpk, dpv, dsm, dpz, dw2, dgb, dnw, dst = vjp((dy_ref[...], dstate[...]))
        dpq_ref[...] = dpq.astype(dpq_ref.dtype)
        dpk_ref[...] = dpk.astype(dpk_ref.dtype)
        dpv_ref[...] = dpv.astype(dpv_ref.dtype)
        dsm_ref[...] = dsm.astype(dsm_ref.dtype)
        dpz_ref[...] = dpz.astype(dpz_ref.dtype)
        dstate[...] = dst
        dw2_ref[...] += dw2
        dgb_ref[...] += dgb
        dnw_ref[...] += dnw

    return _host_call(
        body, hosted, name="gla_bwd", grid=(n,),
        in_specs=[_col_spec(c, COL_DQ, n), _col_spec(c, COL_DK, n), _col_spec(c, COL_DV, n),
                  _small_spec(c, COL128_SMALL_D, n), _col_spec(c, COL_DZ, n),
                  pl.BlockSpec((c, GROUP), lambda i: (n - 1 - i, 3)),
                  pl.BlockSpec((1, GROUP, GROUP), lambda i: (n - 1 - i, 0, 0)),
                  _full_spec((128, GROUP)), _full_spec((1, GROUP)), _full_spec((1, GROUP))],
        out_specs=[_out_rows_spec(c, GROUP, n)] * 3 + [_out_rows_spec(c, 128, n), _out_rows_spec(c, GROUP, n),
                   _full_spec((128, GROUP)), _full_spec((1, GROUP)), _full_spec((1, GROUP))],
        out_shape=[jax.ShapeDtypeStruct((t, GROUP), BF16)] * 3 + [
            jax.ShapeDtypeStruct((t, 128), BF16), jax.ShapeDtypeStruct((t, GROUP), BF16),
            jax.ShapeDtypeStruct((128, GROUP), F32), jax.ShapeDtypeStruct((1, GROUP), F32),
            jax.ShapeDtypeStruct((1, GROUP), F32)],
        scratch_shapes=[pltpu.VMEM((GROUP, GROUP), F32)],
        args=(p, p, p, p, p, dmix, states, w2, gbias, nw))


def _pick_tile(n, pref):
    for cand in pref:
        if n % cand == 0:
            return cand
    return n


MM_TILE_CAP = 1408


def _largest_tile(n, cap):
    best = None
    for mult in range(1, cap // 128 + 1):
        if n % (128 * mult) == 0:
            best = 128 * mult
    return best if best is not None else n


def _half_index(t, per_half, middle):
    half = jnp.where(t >= per_half, 1, 0)
    return half, middle, t - half * per_half


def _matmul(a, b, mode, out_dtype, name, res=None):
    if mode == "nn":
        (m, k), n = a.shape, b.shape[1]
    elif mode == "nt":
        (m, k), n = a.shape, b.shape[0]
    else:
        (k, m), n = a.shape, b.shape[-1] * (2 if b.ndim == 3 else 1)
    tm = _largest_tile(m, MM_TILE_CAP)
    tn = _largest_tile(b.shape[-1] if b.ndim == 3 else n, MM_TILE_CAP)
    tk = _largest_tile(k, MM_TILE_CAP)
    nk = k // tk
    if mode == "nn":
        a_spec = pl.BlockSpec((tm, tk), lambda i, j, kk: (i, kk))
        b_spec = pl.BlockSpec((tk, tn), lambda i, j, kk: (kk, j))
        dims = (1, 0)
    elif mode == "nt":
        a_spec = pl.BlockSpec((tm, tk), lambda i, j, kk: (i, kk))
        b_spec = pl.BlockSpec((tn, tk), lambda i, j, kk: (j, kk))
        dims = (1, 1)
    else:
        a_spec = pl.BlockSpec((tk, tm), lambda i, j, kk: (kk, i))
        if b.ndim == 3:
            njh = b.shape[-1] // tn
            b_spec = pl.BlockSpec((None, tk, tn), lambda i, j, kk: _half_index(j, njh, kk))
        else:
            b_spec = pl.BlockSpec((tk, tn), lambda i, j, kk: (kk, j))
        dims = (0, 0)
    o_spec = pl.BlockSpec((tm, tn), lambda i, j, kk: (i, j))
    has_res = res is not None

    def body(*refs):
        a_ref, b_ref = refs[:2]
        r_ref = refs[2] if has_res else None
        o_ref = refs[3] if has_res else refs[2]
        part = _dg(a_ref[...].astype(MXU_DTYPE), b_ref[...].astype(MXU_DTYPE), *dims)

        def finish(out):
            if has_res:
                out = out + r_ref[...]
            o_ref[...] = out.astype(o_ref.dtype)

        if nk == 1:
            finish(part)
            return
        acc = refs[-1]
        kk = pl.program_id(2)

        @pl.when(kk == 0)
        def _():
            acc[...] = part

        @pl.when(kk > 0)
        def _():
            acc[...] += part

        @pl.when(kk == nk - 1)
        def _():
            finish(acc[...])

    in_specs = [a_spec, b_spec] + ([o_spec] if has_res else [])
    args = (a, b) + ((res,) if has_res else ())
    return pl.pallas_call(
        body, name=name, grid=(m // tm, n // tn, nk), in_specs=in_specs, out_specs=o_spec,
        out_shape=jax.ShapeDtypeStruct((m, n), out_dtype),
        scratch_shapes=[pltpu.VMEM((tm, tn), F32)] if nk > 1 else [],
        compiler_params=_cparams(("parallel", "parallel", "arbitrary")),
    )(*args)


def _matmul_nt_norm_bwd(a, b, x, w, dres, name):
    n = b.shape[0]
    m = a.shape[-2]
    tm = _largest_tile(m, 1024)
    if a.ndim == 3:
        kh = a.shape[2]
        k = 2 * kh
        tk = _largest_tile(kh, MM_TILE_CAP)
        nkh = kh // tk
        a_spec = pl.BlockSpec((None, tm, tk), lambda i, kk: _half_index(kk, nkh, i))
    else:
        k = a.shape[1]
        tk = _largest_tile(k, MM_TILE_CAP)
        a_spec = pl.BlockSpec((tm, tk), lambda i, kk: (i, kk))
    nk = k // tk

    def body(a_ref, b_ref, x_ref, w_ref, r_ref, dx_ref, dw_ref, acc):
        i, kk = pl.program_id(0), pl.program_id(1)
        part = _dg(a_ref[...].astype(MXU_DTYPE), b_ref[...].astype(MXU_DTYPE), 1, 1)

        @pl.when(kk == 0)
        def _():
            acc[...] = part

        @pl.when(kk > 0)
        def _():
            acc[...] += part

        @pl.when((i == 0) & (kk == 0))
        def _():
            dw_ref[...] = jnp.zeros_like(dw_ref)

        @pl.when(kk == nk - 1)
        def _():
            g = acc[...]
            xv = x_ref[...]
            r = lax.rsqrt(jnp.mean(xv * xv, axis=-1, keepdims=True) + EPS)
            xhat = xv * r
            dw_ref[...] += jnp.sum(g * xhat, axis=0, keepdims=True)
            gx = g * w_ref[...]
            dx_ref[...] = r_ref[...] + r * (gx - xhat * jnp.mean(gx * xhat, axis=-1, keepdims=True))

    row_spec = pl.BlockSpec((tm, n), lambda i, kk: (i, 0))
    return pl.pallas_call(
        body, name=name, grid=(m // tm, nk),
        in_specs=[a_spec, pl.BlockSpec((n, tk), lambda i, kk: (0, kk)),
                  row_spec, pl.BlockSpec((1, n), lambda i, kk: (0, 0)), row_spec],
        out_specs=[row_spec, pl.BlockSpec((1, n), lambda i, kk: (0, 0))],
        out_shape=[jax.ShapeDtypeStruct((m, n), F32), jax.ShapeDtypeStruct((1, n), F32)],
        scratch_shapes=[pltpu.VMEM((tm, n), F32)],
        compiler_params=_cparams(("arbitrary", "arbitrary")),
    )(a, b, x, w, dres)


NORM_ROWS = 512


def _rmsnorm_fwd(x, w, name):
    t, d = x.shape

    def body(x_ref, w_ref, o_ref):
        xv = x_ref[...]
        r = lax.rsqrt(jnp.mean(xv * xv, axis=-1, keepdims=True) + EPS)
        o_ref[...] = (xv * r * w_ref[...]).astype(o_ref.dtype)

    return pl.pallas_call(
        body, name=name, grid=(t // NORM_ROWS,),
        in_specs=[pl.BlockSpec((NORM_ROWS, d), lambda i: (i, 0)), _full_spec((1, d))],
        out_specs=pl.BlockSpec((NORM_ROWS, d), lambda i: (i, 0)),
        out_shape=jax.ShapeDtypeStruct((t, d), BF16),
        compiler_params=_cparams(("parallel",)),
    )(x, w)


SWIGLU_ROWS = 128


def _ffn_up_swiglu(h, w_gate_up):
    m, k = h.shape
    tm = _largest_tile(m, 512)
    tn = _largest_tile(D_FF, MM_TILE_CAP)
    nj = D_FF // tn

    def body(a_ref, bg_ref, bu_ref, g_ref, u_ref, act_ref):
        a = a_ref[...].astype(MXU_DTYPE)
        gate = _dg(a, bg_ref[...].astype(MXU_DTYPE), 1, 0)
        up = _dg(a, bu_ref[...].astype(MXU_DTYPE), 1, 0)
        g_ref[...] = gate
        u_ref[...] = up
        act_ref[...] = (_silu(gate) * up).astype(act_ref.dtype)

    o_spec = pl.BlockSpec((tm, tn), lambda i, j: (i, j))
    return pl.pallas_call(
        body, name="ffn_up", grid=(m // tm, nj),
        in_specs=[pl.BlockSpec((tm, k), lambda i, j: (i, 0)), pl.BlockSpec((k, tn), lambda i, j: (0, j)),
                  pl.BlockSpec((k, tn), lambda i, j: (0, j + nj))],
        out_specs=[o_spec, o_spec, o_spec],
        out_shape=[jax.ShapeDtypeStruct((m, D_FF), F32), jax.ShapeDtypeStruct((m, D_FF), F32),
                   jax.ShapeDtypeStruct((m, D_FF), BF16)],
        compiler_params=_cparams(("parallel", "parallel")),
    )(h, w_gate_up, w_gate_up)


def _ffn_down_dx_swiglu(dx, w_down, gate, up):
    m, k = dx.shape
    tm = _largest_tile(m, 512)
    tn = _largest_tile(D_FF, MM_TILE_CAP)

    def body(a_ref, b_ref, g_ref, u_ref, o_ref):
        da = _dg(a_ref[...].astype(MXU_DTYPE), b_ref[...].astype(MXU_DTYPE), 1, 1)
        gate = g_ref[...]
        sg = jax.nn.sigmoid(gate)
        o_ref[0] = (da * u_ref[...] * (sg * (1.0 + gate * (1.0 - sg)))).astype(o_ref.dtype)
        o_ref[1] = (da * gate * sg).astype(o_ref.dtype)

    tile = pl.BlockSpec((tm, tn), lambda i, j: (i, j))
    return pl.pallas_call(
        body, name="ffn_down_dx", grid=(m // tm, D_FF // tn),
        in_specs=[pl.BlockSpec((tm, k), lambda i, j: (i, 0)), pl.BlockSpec((tn, k), lambda i, j: (j, 0)), tile, tile],
        out_specs=pl.BlockSpec((2, tm, tn), lambda i, j: (0, i, j)),
        out_shape=jax.ShapeDtypeStruct((2, m, D_FF), MXU_DTYPE),
        compiler_params=_cparams(("parallel", "parallel")),
    )(dx, w_down, gate, up)


def _loss_head(x, w, target):
    t, d = x.shape

    def fwd(xv, wv, tv):
        r = lax.rsqrt(jnp.mean(xv * xv, axis=-1, keepdims=True) + EPS)
        err = xv * r * wv - tv
        return 0.5 * jnp.sum(jnp.mean(err * err, axis=-1, keepdims=True), axis=0, keepdims=True)

    def body(x_ref, w_ref, t_ref, dx_ref, dw_ref, loss_ref):
        @pl.when(pl.program_id(0) == 0)
        def _():
            dw_ref[...] = jnp.zeros_like(dw_ref)
            loss_ref[...] = jnp.zeros_like(loss_ref)

        loss, vjp = jax.vjp(fwd, x_ref[...], w_ref[...], t_ref[...])
        dx, dw, _ = vjp(jnp.ones((1, 1), F32))
        dx_ref[...] = dx
        dw_ref[...] += dw
        loss_ref[...] += jnp.broadcast_to(loss, loss_ref.shape)

    return pl.pallas_call(
        body, name="loss_head", grid=(t // NORM_ROWS,),
        in_specs=[pl.BlockSpec((NORM_ROWS, d), lambda i: (i, 0)), _full_spec((1, d)),
                  pl.BlockSpec((NORM_ROWS, d), lambda i: (i, 0))],
        out_specs=[pl.BlockSpec((NORM_ROWS, d), lambda i: (i, 0)), _full_spec((1, d)), _full_spec((8, 128))],
        out_shape=[jax.ShapeDtypeStruct((t, d), F32), jax.ShapeDtypeStruct((1, d), F32),
                   jax.ShapeDtypeStruct((8, 128), F32)],
        compiler_params=_cparams(("arbitrary",)),
    )(x, w, target)


def _pad_w_in(w):
    z = lambda n: jnp.zeros((w.shape[0], n), w.dtype)
    return jnp.concatenate([w[:, 0:2048], w[:, 2056:2312], w[:, 2312:3080], w[:, 3096:3352],
                            w[:, 2048:2056], z(120), w[:, 3080:3096], z(112)], axis=1)


def _unpad_w_in(wp):
    return jnp.concatenate([wp[:, 0:2048], wp[:, 3328:3336], wp[:, 2048:2304], wp[:, 2304:3072],
                            wp[:, 3456:3472], wp[:, 3072:3328]], axis=1)


def _pad_rows(a, rows):
    return jnp.concatenate([a, jnp.zeros((rows - a.shape[0],) + a.shape[1:], a.dtype)], axis=0)


def _pad_lanes(a, lanes):
    return jnp.concatenate([a, jnp.zeros(a.shape[:-1] + (lanes - a.shape[-1],), a.dtype)], axis=-1)


def _layer_params(l, small):
    dn_cw = small["dn_conv_w"][l]
    return dict(
        ln_w=small["sgu_ln_w"][l][None], ln_b=small["sgu_ln_b"][l][None],
        ws=small["sgu_w_spatial"][l], bs_t=_pad_lanes(small["sgu_b_spatial"][l].T, 128),
        sc_cw=_pad_rows(small["sc_conv_w"][l], HALO),
        dn_cw=jnp.stack([_pad_rows(dn_cw[:, j * GROUP:(j + 1) * GROUP], HALO) for j in range(3)]),
        dn_al=_pad_lanes(small["dn_a_log"][l][None], 128), dn_dt=_pad_lanes(small["dn_dt_bias"][l][None], 128),
        dn_nw=jnp.tile(small["dn_norm_w"][l][None], (1, HEADS)),
        gla_w2=_pad_rows(small["gla_w_gate2"][l], 128), gla_gb=small["gla_gate_bias"][l][None],
        gla_nw=jnp.tile(small["gla_norm_w"][l][None], (1, HEADS)),
    )


def _exchange_piece(name, grad):
    if name == "w_in":
        grad = _unpad_w_in(grad)
    if name in ("w_in", "w_gate_up"):
        r, c4 = grad.shape
        return jnp.transpose(grad.reshape(2, r // 2, N_CHIPS, c4 // N_CHIPS), (0, 2, 1, 3))
    r4, c = grad.shape
    return jnp.transpose(grad.reshape(N_CHIPS, 2, r4 // (2 * N_CHIPS), c), (1, 0, 2, 3))


def _reduce_on_chip(pieces):
    return _pair_add(pieces, _sibling_swap(pieces))


def _local_step(x, target, big, small, late_weights=None, exchange=False, small_extra=None):
    saved = []
    h = x
    for l in range(DEPTH):
        lp = _layer_params(l, small)
        h1 = _rmsnorm_fwd(h, small["norm1_w"][l][None], "norm1_fwd")
        p = _matmul(h1, big["w_in"][l], "nn", F32, "proj_in")
        y_a = _sgu_fwd(p, lp["ln_w"], lp["ln_b"], lp["ws"], lp["bs_t"])
        y_b = _sc_fwd(p, lp["sc_cw"])
        host1 = host2 = None
        if late_weights is not None:
            shards, finish = late_weights[l]
            modes = ["layer"] * len(shards)
            host1 = (1, shards, modes, _exchange_buffers(shards, modes))
        (y_c, st_c, inv_c), ex = _dn_fwd(p, lp["dn_cw"], lp["dn_al"], lp["dn_dt"], lp["dn_nw"], hosted=host1)
        if host1 is not None:
            host2 = (2, None, None, ex)
        (y_d, st_d), ex = _gla_fwd(p, lp["gla_w2"], lp["gla_gb"], lp["gla_nw"], hosted=host2)
        if host2 is not None:
            big = finish(big, ex)
        mix = jnp.concatenate([y_a, y_b, y_c, y_d], axis=1)
        x1 = _matmul(mix, big["w_out"][l], "nn", F32, "proj_out", res=h)
        h2 = _rmsnorm_fwd(x1, small["norm2_w"][l][None], "norm2_fwd")
        gate, up, act = _ffn_up_swiglu(h2, big["w_gate_up"][l])
        x2 = _matmul(act, big["w_down"][l], "nn", F32, "ffn_down", res=x1)
        saved.append(dict(x0=h, h1=h1, p=p, st_c=st_c, inv_c=inv_c, st_d=st_d, mix=mix, x1=x1, h2=h2, gate=gate, up=up, act=act, lp=lp))
        h = x2

    dx, d_final, loss = _loss_head(h, small["final_norm_w"][None], target)
    gbig = {k: [None] * DEPTH for k in ("w_in", "w_out", "w_gate_up", "w_down")}
    gs = {k: [None] * DEPTH for k in ("norm1_w", "sgu_ln_w", "sgu_ln_b", "sgu_w_spatial", "sgu_b_spatial", "sc_conv_w",
                                     "dn_conv_w", "dn_a_log", "dn_dt_bias", "dn_norm_w", "gla_w_gate2",
                                     "gla_gate_bias", "gla_norm_w", "norm2_w")}
    carry = []
    contribs = {}
    for l in reversed(range(DEPTH)):
        s = saved[l]
        lp = s["lp"]
        gbig["w_down"][l] = _matmul(s["act"], dx, "tn", GRAD_WIRE_DTYPE, "ffn_down_dw")
        dgu = _ffn_down_dx_swiglu(dx, big["w_down"][l], s["gate"], s["up"])
        gbig["w_gate_up"][l] = _matmul(s["h2"], dgu, "tn", GRAD_WIRE_DTYPE, "ffn_up_dw")
        dx1, gs["norm2_w"][l] = _matmul_nt_norm_bwd(dgu, big["w_gate_up"][l], s["x1"], small["norm2_w"][l][None], dx,
                                                    "ffn_up_dx")
        gbig["w_out"][l] = _matmul(s["mix"], dx1, "tn", GRAD_WIRE_DTYPE, "proj_out_dw")
        dmix = _matmul(dx1, big["w_out"][l], "nt", F32, "proj_out_dx")
        p = s["p"]
        dpu, dpv, g_lw, g_lb, g_ws, g_bs = _sgu_bwd(p, dmix, lp["ln_w"], lp["ln_b"], lp["ws"], lp["bs_t"])
        dpb, dpc, dph, g_sc = _sc_bwd(p, dmix, lp["sc_cw"])
        host1 = host2 = None
        if exchange:
            unit = carry + [(n, l, _exchange_piece(n, gbig[n][l])) for n in ("w_out", "w_gate_up", "w_down")]
            carry = []
            summed = _reduce_on_chip([piece for _, _, piece in unit])
            modes = ["piece"] * len(summed)
            host1 = (1, summed, modes, _exchange_buffers(summed, modes))
        (dcq, dck, dcv, dcs, dcz, g_dcw, g_al, g_dt, g_dnw), ex = _dn_bwd(
            p, dmix, s["st_c"], s["inv_c"], lp["dn_cw"], lp["dn_al"], lp["dn_dt"], lp["dn_nw"], hosted=host1)
        if host1 is not None:
            host2 = (2, None, None, ex)
        (ddq, ddk, ddv, dds, ddz, g_w2, g_gb, g_gnw), ex = _gla_bwd(p, dmix, s["st_d"], lp["gla_w2"], lp["gla_gb"],
                                                                   lp["gla_nw"], hosted=host2)
        if host2 is not None:
            for (n, lay, _), got in zip(unit, ex):
                contribs[(n, lay)] = got
        dp = jnp.concatenate([dpu, dpv, dpb, dpc, dph, dcq, dck, dcv, dcz, ddq, ddk, ddv, ddz, dcs, dds], axis=1)
        gbig["w_in"][l] = _matmul(s["h1"], dp, "tn", GRAD_WIRE_DTYPE, "proj_in_dw")
        dx, gs["norm1_w"][l] = _matmul_nt_norm_bwd(dp, big["w_in"][l], s["x0"], small["norm1_w"][l][None], dx1,
                                                   "proj_in_dx")
        gs["sgu_ln_w"][l], gs["sgu_ln_b"][l] = g_lw[0], g_lb[0]
        gs["sgu_w_spatial"][l] = g_ws
        gs["sgu_b_spatial"][l] = g_bs[:, :HEADS].T
        gs["sc_conv_w"][l] = g_sc[:3]
        gs["dn_conv_w"][l] = jnp.concatenate([g_dcw[0, :4], g_dcw[1, :4], g_dcw[2, :4]], axis=1)
        gs["dn_a_log"][l], gs["dn_dt_bias"][l] = g_al[0, :HEADS], g_dt[0, :HEADS]
        gs["dn_norm_w"][l] = jnp.sum(g_dnw.reshape(HEADS, HEAD_DIM), axis=0)
        gs["gla_w_gate2"][l] = g_w2[:16]
        gs["gla_gate_bias"][l] = g_gb[0]
        gs["gla_norm_w"][l] = jnp.sum(g_gnw.reshape(HEADS, HEAD_DIM), axis=0)
        gs["norm1_w"][l] = gs["norm1_w"][l][0]
        gs["norm2_w"][l] = gs["norm2_w"][l][0]
        if exchange:
            carry = [("w_in", l, _exchange_piece("w_in", gbig["w_in"][l]))]
    gsmall = {k: jnp.stack(v) for k, v in gs.items()}
    gsmall["final_norm_w"] = d_final[0]
    if not exchange:
        return loss, dx, gbig, gsmall
    summed = _reduce_on_chip([piece for _, _, piece in carry])
    last = _chip_exchange(summed + [small_extra(gsmall, loss)], ["piece"] * len(summed) + ["whole"], "exchange_grads")
    for (n, lay, _), got in zip(carry, last):
        contribs[(n, lay)] = got
    return loss, dx, contribs, last[-1]


def _peer_chips(x, y):
    return [(1 - x, y, 2 * (1 - x) + y), (x, 1 - y, 2 * x + 1 - y), (1 - x, 1 - y, 2 * (1 - x) + 1 - y)]


def _chip_exchange(arrays, modes, name):
    na = len(arrays)
    bufs = _exchange_buffers(arrays, modes)

    def body(*refs):
        ins, outs = refs[:na], refs[2 * na:3 * na]
        send1, recv1, send2, recv2 = refs[3 * na:]
        _exchange_stage1(ins, outs, modes, send1, recv1, "start")
        _exchange_stage1(ins, outs, modes, send1, recv1, "wait")
        _exchange_stage2(outs, send2, recv2, "start")
        _exchange_stage2(outs, send2, recv2, "wait")

    any_spec = pl.BlockSpec(memory_space=pl.ANY)
    return pl.pallas_call(
        body, name=name,
        in_specs=[any_spec] * (2 * na), out_specs=[any_spec] * na,
        out_shape=[jax.ShapeDtypeStruct(b.shape, b.dtype) for b in bufs],
        input_output_aliases={na + a: a for a in range(na)},
        scratch_shapes=_stage1_sems(na) + _stage2_sems(na),
    )(*arrays, *bufs)


def _exchange_buffers(arrays, modes):
    c_idx = lax.axis_index("c")
    chip = 2 * lax.axis_index("x") + lax.axis_index("y")
    bufs = []
    for arr, md in zip(arrays, modes):
        if md == "layer":
            unit = lax.dynamic_index_in_dim(arr, c_idx, 0, keepdims=False)
        elif md == "piece":
            unit = lax.dynamic_index_in_dim(arr, chip, 0, keepdims=False)
        else:
            unit = arr
        buf = lax.empty((2, N_CHIPS) + unit.shape, unit.dtype)
        bufs.append(lax.dynamic_update_slice(buf, unit[None, None], (c_idx, chip) + (0,) * unit.ndim))
    return bufs


def _stage1_sems(na):
    return [pltpu.SemaphoreType.DMA((na, 3)), pltpu.SemaphoreType.DMA((na, 3))]


def _stage2_sems(na):
    return [pltpu.SemaphoreType.DMA((na,)), pltpu.SemaphoreType.DMA((na,))]


def _exchange_stage1(ins, outs, modes, send1, recv1, what):
    x, y, c = lax.axis_index("x"), lax.axis_index("y"), lax.axis_index("c")
    me = 2 * x + y
    for a in range(len(ins)):
        for k, (px, py, pidx) in enumerate(_peer_chips(x, y)):
            if modes[a] == "layer":
                src = ins[a].at[c]
            else:
                src = ins[a].at[pidx] if modes[a] == "piece" else ins[a]
            if what == "start":
                pltpu.make_async_remote_copy(
                    src_ref=src, dst_ref=outs[a].at[c, me], send_sem=send1.at[a, k], recv_sem=recv1.at[a, k],
                    device_id=(px, py, c), device_id_type=MESH).start()
            else:
                cp = pltpu.make_async_remote_copy(
                    src_ref=src, dst_ref=outs[a].at[c, pidx], send_sem=send1.at[a, k], recv_sem=recv1.at[a, k],
                    device_id=(px, py, c), device_id_type=MESH)
                cp.wait_send()
                cp.wait_recv()


def _exchange_stage2(outs, send2, recv2, what):
    x, y, c = lax.axis_index("x"), lax.axis_index("y"), lax.axis_index("c")
    sibling = (x, y, 1 - c)
    for a in range(len(outs)):
        if what == "start":
            pltpu.make_async_remote_copy(
                src_ref=outs[a].at[c], dst_ref=outs[a].at[c], send_sem=send2.at[a], recv_sem=recv2.at[a],
                device_id=sibling, device_id_type=MESH).start()
        else:
            cp = pltpu.make_async_remote_copy(
                src_ref=outs[a].at[c], dst_ref=outs[a].at[1 - c], send_sem=send2.at[a], recv_sem=recv2.at[a],
                device_id=sibling, device_id_type=MESH)
            cp.wait_send()
            cp.wait_recv()


def _sibling_swap(arrays):
    na = len(arrays)

    def body(*refs):
        ins, theirs = refs[:na], refs[na:2 * na]
        send_sems, recv_sems = refs[2 * na:]
        x, y, c = lax.axis_index("x"), lax.axis_index("y"), lax.axis_index("c")
        sibling = (x, y, 1 - c)
        for a in range(na):
            pltpu.make_async_remote_copy(
                src_ref=ins[a].at[1 - c], dst_ref=theirs[a], send_sem=send_sems.at[a], recv_sem=recv_sems.at[a],
                device_id=sibling, device_id_type=MESH).start()
        for a in range(na):
            cp = pltpu.make_async_remote_copy(
                src_ref=ins[a].at[1 - c], dst_ref=theirs[a], send_sem=send_sems.at[a], recv_sem=recv_sems.at[a],
                device_id=sibling, device_id_type=MESH)
            cp.wait_send()
            cp.wait_recv()

    any_spec = pl.BlockSpec(memory_space=pl.ANY)
    return pl.pallas_call(
        body, name="sibling_swap",
        in_specs=[any_spec] * na, out_specs=[any_spec] * na,
        out_shape=[jax.ShapeDtypeStruct(s.shape[1:], s.dtype) for s in arrays],
        scratch_shapes=[pltpu.SemaphoreType.DMA((na,)), pltpu.SemaphoreType.DMA((na,))],
    )(*arrays)


PAIR_ADD_STEPS = 8


def _pair_add(boths, theirs):
    na = len(boths)
    core = lax.axis_index("c").astype(jnp.int32).reshape(1)
    flat_b = [b.reshape(2, b.shape[1] * b.shape[2], b.shape[3]) for b in boths]
    flat_t = [t.reshape(t.shape[0] * t.shape[1], t.shape[2]) for t in theirs]
    rows = [t.shape[0] // PAIR_ADD_STEPS for t in flat_t]

    def body(core_ref, *refs):
        for a in range(na):
            refs[2 * na + a][...] = (refs[a][...].astype(F32) + refs[na + a][...].astype(F32)).astype(
                refs[2 * na + a].dtype)

    own = [pl.BlockSpec((None, r, t.shape[1]), lambda i, core_ref: (core_ref[0], i, 0)) for r, t in zip(rows, flat_t)]
    plain = [pl.BlockSpec((r, t.shape[1]), lambda i, core_ref: (i, 0)) for r, t in zip(rows, flat_t)]
    outs = pl.pallas_call(
        body, name="pair_add",
        grid_spec=pltpu.PrefetchScalarGridSpec(
            num_scalar_prefetch=1, grid=(PAIR_ADD_STEPS,), in_specs=own + plain, out_specs=plain),
        out_shape=[jax.ShapeDtypeStruct(t.shape, t.dtype) for t in flat_t],
        compiler_params=_cparams(("parallel",)),
    )(core, *flat_b, *flat_t)
    return [o.reshape(t.shape) for o, t in zip(outs, theirs)]


def _adamw_math(g, w, m, v):
    m2 = ADAM_B1 * m + (1.0 - ADAM_B1) * g
    v2 = ADAM_B2 * v + (1.0 - ADAM_B2) * (g * g)
    m_hat = m2 / (1.0 - ADAM_B1 ** ADAM_STEP)
    v_hat = v2 / (1.0 - ADAM_B2 ** ADAM_STEP)
    delta = -ADAM_LR * (m_hat / (jnp.sqrt(v_hat) + ADAM_EPS) + ADAM_WD * w)
    return delta, m2, v2


def _adamw_big(contrib, w, m, v, layer, name, prev=None):
    _, r, c = w.shape
    rh = r // 2
    tr = _pick_tile(rh, (256, 176, 128, 64, 8))
    nj = rh // tr
    blk = pl.BlockSpec((1, tr, c), lambda h, j: (layer, h * nj + j, 0))
    n_prev = 0 if prev is None else 4

    def body(*refs):
        g_ref, w_ref, m_ref, v_ref = refs[:4]
        go_ref, d_ref, mo_ref, vo_ref = refs[4 + n_prev:]
        g = g_ref[0, 0].astype(F32)
        for s in range(1, N_CHIPS):
            g = g + g_ref[0, s].astype(F32)
        delta, m2, v2 = _adamw_math(g, w_ref[0], m_ref[0], v_ref[0])
        go_ref[0] = g
        d_ref[0] = delta
        mo_ref[0] = m2
        vo_ref[0] = v2

    any_spec = pl.BlockSpec(memory_space=pl.ANY)
    return pl.pallas_call(
        body, name=name, grid=(2, nj),
        in_specs=[pl.BlockSpec((1, N_CHIPS, tr, c), lambda h, j: (h, 0, j, 0)), blk, blk, blk] + [any_spec] * n_prev,
        out_specs=[blk] * 4, out_shape=[jax.ShapeDtypeStruct(w.shape, F32)] * 4,
        input_output_aliases={4 + a: a for a in range(n_prev)},
        compiler_params=_cparams(("parallel", "parallel")),
    )(contrib, w, m, v, *([] if prev is None else prev))


def _sum_small(contrib):
    rows = contrib.shape[2]

    def body(g_ref, o_ref):
        total = g_ref[0, 0]
        for j in range(1, N_DEV):
            total = total + g_ref[j // N_CHIPS, j % N_CHIPS]
        o_ref[...] = total

    return pl.pallas_call(
        body, name="sum_small", out_shape=jax.ShapeDtypeStruct((rows, 128), F32),
        compiler_params=_cparams(),
    )(contrib)


def _adamw_small(gs, ws, ms, vs):
    n = len(gs)
    as2d = lambda a: a.reshape(1, -1) if a.ndim == 1 else a

    def body(*refs):
        g_refs, w_refs, m_refs, v_refs = refs[:n], refs[n:2 * n], refs[2 * n:3 * n], refs[3 * n:4 * n]
        d_refs, mo_refs, vo_refs = refs[4 * n:5 * n], refs[5 * n:6 * n], refs[6 * n:]
        for j in range(n):
            delta, m2, v2 = _adamw_math(g_refs[j][...], w_refs[j][...], m_refs[j][...], v_refs[j][...])
            d_refs[j][...] = delta
            mo_refs[j][...] = m2
            vo_refs[j][...] = v2

    ins = [as2d(a) for a in (*gs, *ws, *ms, *vs)]
    outs = pl.pallas_call(
        body, name="adamw_small", out_shape=[jax.ShapeDtypeStruct(a.shape, F32) for a in ins[:n]] * 3,
        compiler_params=_cparams(),
    )(*ins)
    back = lambda group: [o.reshape(g.shape) for o, g in zip(group, gs)]
    return back(outs[:n]), back(outs[n:2 * n]), back(outs[2 * n:])


PACK_ALIGN = 8 * 128


def _packed_rows(shape):
    n = 1
    for d in shape:
        n *= d
    return (n + PACK_ALIGN - 1) // PACK_ALIGN * 8


def _pack(arrays):
    parts = []
    for a in arrays:
        flat = a.reshape(-1)
        pad = _packed_rows(a.shape) * 128 - flat.shape[0]
        if pad:
            flat = jnp.concatenate([flat, jnp.zeros((pad,), F32)])
        parts.append(flat.reshape(-1, 128))
    return jnp.concatenate(parts, axis=0)


def _unpack(packed, shapes):
    out, row = [], 0
    for s in shapes:
        rows = _packed_rows(s)
        n = 1
        for d in s:
            n *= d
        out.append(packed[row:row + rows].reshape(-1)[:n].reshape(s))
        row += rows
    return out


SMALL_NAMES = ("norm1_w", "sgu_ln_w", "sgu_ln_b", "sgu_w_spatial", "sgu_b_spatial", "sc_conv_w", "dn_conv_w",
               "dn_a_log", "dn_dt_bias", "dn_norm_w", "gla_w_gate2", "gla_gate_bias", "gla_norm_w", "norm2_w",
               "final_norm_w")
SHARDED_SMALL = ("sc_conv_w", "dn_conv_w", "gla_w_gate2")
BIG_NAMES = ("w_in", "w_out", "w_gate_up", "w_down")
WEIGHT_ORDER = ("norm1_w", "w_in", "sgu_ln_w", "sgu_ln_b", "sgu_w_spatial", "sgu_b_spatial", "sc_conv_w", "dn_conv_w",
                "dn_a_log", "dn_dt_bias", "dn_norm_w", "gla_w_gate2", "gla_gate_bias", "gla_norm_w", "w_out",
                "norm2_w", "w_gate_up", "w_down", "final_norm_w")


def _cols_from_shards(g):
    l, n, r, c = g.shape
    return jnp.transpose(g, (0, 2, 1, 3)).reshape(l, r, n * c)


def kernel(x, norm1_w, w_in, sgu_ln_w, sgu_ln_b, sgu_w_spatial, sgu_b_spatial, sc_conv_w, dn_conv_w, dn_a_log, dn_dt_bias, dn_norm_w, gla_w_gate2, gla_gate_bias, gla_norm_w, w_out, norm2_w, w_gate_up, w_down, final_norm_w, loss_target, m_norm1_w, m_w_in, m_sgu_ln_w, m_sgu_ln_b, m_sgu_w_spatial, m_sgu_b_spatial, m_sc_conv_w, m_dn_conv_w, m_dn_a_log, m_dn_dt_bias, m_dn_norm_w, m_gla_w_gate2, m_gla_gate_bias, m_gla_norm_w, m_w_out, m_norm2_w, m_w_gate_up, m_w_down, m_final_norm_w, v_norm1_w, v_w_in, v_sgu_ln_w, v_sgu_ln_b, v_sgu_w_spatial, v_sgu_b_spatial, v_sc_conv_w, v_dn_conv_w, v_dn_a_log, v_dn_dt_bias, v_dn_norm_w, v_gla_w_gate2, v_gla_gate_bias, v_gla_norm_w, v_w_out, v_norm2_w, v_w_gate_up, v_w_down, v_final_norm_w):
    w = dict(norm1_w=norm1_w, w_in=w_in, sgu_ln_w=sgu_ln_w, sgu_ln_b=sgu_ln_b, sgu_w_spatial=sgu_w_spatial,
             sgu_b_spatial=sgu_b_spatial, sc_conv_w=sc_conv_w, dn_conv_w=dn_conv_w, dn_a_log=dn_a_log,
             dn_dt_bias=dn_dt_bias, dn_norm_w=dn_norm_w, gla_w_gate2=gla_w_gate2, gla_gate_bias=gla_gate_bias,
             gla_norm_w=gla_norm_w, w_out=w_out, norm2_w=norm2_w, w_gate_up=w_gate_up, w_down=w_down,
             final_norm_w=final_norm_w)
    m = dict(norm1_w=m_norm1_w, w_in=m_w_in, sgu_ln_w=m_sgu_ln_w, sgu_ln_b=m_sgu_ln_b, sgu_w_spatial=m_sgu_w_spatial,
             sgu_b_spatial=m_sgu_b_spatial, sc_conv_w=m_sc_conv_w, dn_conv_w=m_dn_conv_w, dn_a_log=m_dn_a_log,
             dn_dt_bias=m_dn_dt_bias, dn_norm_w=m_dn_norm_w, gla_w_gate2=m_gla_w_gate2,
             gla_gate_bias=m_gla_gate_bias, gla_norm_w=m_gla_norm_w, w_out=m_w_out, norm2_w=m_norm2_w,
             w_gate_up=m_w_gate_up, w_down=m_w_down, final_norm_w=m_final_norm_w)
    v = dict(norm1_w=v_norm1_w, w_in=v_w_in, sgu_ln_w=v_sgu_ln_w, sgu_ln_b=v_sgu_ln_b, sgu_w_spatial=v_sgu_w_spatial,
             sgu_b_spatial=v_sgu_b_spatial, sc_conv_w=v_sc_conv_w, dn_conv_w=v_dn_conv_w, dn_a_log=v_dn_a_log,
             dn_dt_bias=v_dn_dt_bias, dn_norm_w=v_dn_norm_w, gla_w_gate2=v_gla_w_gate2,
             gla_gate_bias=v_gla_gate_bias, gla_norm_w=v_gla_norm_w, w_out=v_w_out, norm2_w=v_norm2_w,
             w_gate_up=v_w_gate_up, w_down=v_w_down, final_norm_w=v_final_norm_w)
    chip = 2 * lax.axis_index("x") + lax.axis_index("y")

    w_in_wire = w["w_in"].astype(MXU_DTYPE)
    row_halves = lambda a: a.reshape(2, a.shape[0] // 2, a.shape[1])

    def full_w_in(g):
        return _pad_w_in(jnp.transpose(g, (0, 2, 1, 3)).reshape(D_MODEL, IN_COLS))

    first = [row_halves(w_in_wire[0])] + [w[n] for n in SHARDED_SMALL]
    gathered = _chip_exchange(first, ["layer"] * len(first), "gather_first")
    w_in_0 = full_w_in(gathered[0])
    big = dict(w_in=[w_in_0, None])
    small = {n: w[n] for n in SMALL_NAMES if n not in SHARDED_SMALL}
    for j, n in enumerate(SHARDED_SMALL):
        small[n] = _cols_from_shards(gathered[1 + j])
    wire = {n: w[n].astype(MXU_DTYPE) for n in ("w_out", "w_gate_up", "w_down")}

    def cols_full(g):
        _, n, rh, c = g.shape
        return jnp.transpose(g, (0, 2, 1, 3)).reshape(2 * rh, n * c)

    def rows_full(g):
        _, n, rh, c = g.shape
        return jnp.transpose(g, (1, 0, 2, 3)).reshape(n * 2 * rh, c)

    def finish_0(big, g):
        return dict(w_in=[w_in_0, full_w_in(g[0])], w_out=[rows_full(g[1]), None],
                    w_gate_up=[cols_full(g[2]), None], w_down=[rows_full(g[3]), None])

    def finish_1(big, g):
        return dict(big, w_out=[big["w_out"][0], rows_full(g[0])], w_gate_up=[big["w_gate_up"][0], cols_full(g[1])],
                    w_down=[big["w_down"][0], rows_full(g[2])])

    late = [([row_halves(w_in_wire[1])] + [row_halves(wire[n][0]) for n in ("w_out", "w_gate_up", "w_down")], finish_0),
            ([row_halves(wire[n][1]) for n in ("w_out", "w_gate_up", "w_down")], finish_1)]

    small_shapes = [(DEPTH,) + w[n].shape[1:-1] + (w[n].shape[-1] * (N_CHIPS if n in SHARDED_SMALL else 1),)
                    if n != "final_norm_w" else w[n].shape for n in SMALL_NAMES] + [(1,)]

    def pack_small(gsmall, loss_tile):
        return _pack([gsmall[n] for n in SMALL_NAMES] + [loss_tile[0:1, 0]])

    _, grad_x, contribs, small_contrib = _local_step(
        x[0], loss_target[0], big, small, late_weights=late, exchange=True, small_extra=pack_small)

    out_g, out_d, out_m, out_v = {}, {}, {}, {}
    for j, n in enumerate(BIG_NAMES):
        outs = _adamw_big(contribs[(n, 1)], w[n], m[n], v[n], 1, "adamw_" + n + "_1")
        out_g[n], out_d[n], out_m[n], out_v[n] = _adamw_big(contribs[(n, 0)], w[n], m[n], v[n], 0,
                                                            "adamw_" + n + "_0", prev=outs)
    summed = _unpack(_sum_small(small_contrib), small_shapes)
    loss = summed[-1][0]
    for n, g in zip(SMALL_NAMES, summed[:-1]):
        if n in SHARDED_SMALL:
            cols = g.shape[-1] // N_CHIPS
            g = lax.dynamic_slice_in_dim(g, chip * cols, cols, axis=g.ndim - 1)
        out_g[n] = g
    d_s, m_s, v_s = _adamw_small([out_g[n] for n in SMALL_NAMES], [w[n] for n in SMALL_NAMES],
                                 [m[n] for n in SMALL_NAMES], [v[n] for n in SMALL_NAMES])
    for n, d_, m_, v_ in zip(SMALL_NAMES, d_s, m_s, v_s):
        out_d[n], out_m[n], out_v[n] = d_, m_, v_

    return (loss, grad_x[None], *[out_g[n] for n in WEIGHT_ORDER], *[out_d[n] for n in WEIGHT_ORDER],
            *[out_m[n] for n in WEIGHT_ORDER], *[out_v[n] for n in WEIGHT_ORDER])
```

```python
import functools

import jax
import jax.numpy as jnp
from jax import lax
from jax.experimental import pallas as pl
from jax.experimental.pallas import tpu as pltpu

F32 = jnp.float32
BF16 = jnp.bfloat16
MXU_DTYPE = jnp.bfloat16
GRAD_WIRE_DTYPE = jnp.bfloat16
HI = lax.Precision.HIGHEST
MESH = pl.DeviceIdType.MESH

D_MODEL = 1024
DEPTH = 2
GROUP = 256
HEADS = 4
HEAD_DIM = 64
SGU_CHUNK = 128
SCAN_CHUNK = 64
D_FF = 2816
EPS = 1e-6
IN_COLS = 3352
P_COLS = 3584
HALO = 8
N_CHIPS = 4
N_DEV = 8
VMEM_LIMIT = 56 * 1024 * 1024

ADAM_LR = 0.001
ADAM_B1 = 0.9
ADAM_B2 = 0.999
ADAM_EPS = 1e-08
ADAM_WD = 0.01
ADAM_STEP = 10

(COL_AU, COL_AV, COL_BB, COL_BC, COL_BH, COL_CQ, COL_CK, COL_CV, COL_CZ,
 COL_DQ, COL_DK, COL_DV, COL_DZ) = range(13)
COL128_SMALL_C = 26
COL128_SMALL_D = 27


def _cparams(sem=None):
    return pltpu.CompilerParams(dimension_semantics=sem, vmem_limit_bytes=VMEM_LIMIT)


def _iota(shape, dim):
    return lax.broadcasted_iota(jnp.int32, shape, dim)


def _dg(a, b, ca, cb, prec=None):
    return lax.dot_general(a, b, (((ca,), (cb,)), ((), ())), preferred_element_type=F32, precision=prec)


@functools.partial(jax.custom_vjp, nondiff_argnums=(2, 3))
def bdot(a, b, ca, cb):
    return _dg(a.astype(MXU_DTYPE), b.astype(MXU_DTYPE), ca, cb)


def _bdot_fwd(a, b, ca, cb):
    return bdot(a, b, ca, cb), (a, b)


def _bdot_bwd(ca, cb, res, g):
    a, b = res
    if ca == 1:
        da = bdot(g, b, 1, 1 if cb == 0 else 0)
    else:
        da = bdot(b, g, 1 if cb == 0 else 0, 1)
    if cb == 0:
        db = bdot(a, g, 0, 0) if ca == 1 else bdot(a, g, 1, 0)
    else:
        db = bdot(g, a, 0, 0) if ca == 1 else bdot(g, a, 0, 1)
    return da, db


bdot.defvjp(_bdot_fwd, _bdot_bwd)


def _pieces(a, n):
    out, r = [], a
    for i in range(n):
        p = r.astype(MXU_DTYPE)
        out.append(p)
        if i + 1 < n:
            r = r - p.astype(F32)
    return out


def _mdot_impl(a, b, ca, cb, sa, sb):
    total = None
    for i, x in enumerate(_pieces(a, sa)):
        for j, y in enumerate(_pieces(b, sb)):
            if i + j < max(sa, sb):
                t = _dg(x, y, ca, cb)
                total = t if total is None else total + t
    return total


@functools.partial(jax.custom_vjp, nondiff_argnums=(2, 3, 4, 5))
def mdot(a, b, ca, cb, sa, sb):
    return _mdot_impl(a, b, ca, cb, sa, sb)


def _mdot_fwd(a, b, ca, cb, sa, sb):
    return _mdot_impl(a, b, ca, cb, sa, sb), (a, b)


def _mdot_bwd(ca, cb, sa, sb, res, g):
    a, b = res
    ga, gb = (3 if sb == 1 else 2), (3 if sa == 1 else 2)
    if sa == 1:
        da = jnp.zeros_like(a)
    elif ca == 1:
        da = mdot(g, b, 1, 1 if cb == 0 else 0, ga, sb)
    else:
        da = mdot(b, g, 1 if cb == 0 else 0, 1, sb, ga)
    if sb == 1:
        db = jnp.zeros_like(b)
    elif cb == 0:
        db = mdot(a, g, 0, 0, sa, gb) if ca == 1 else mdot(a, g, 1, 0, sa, gb)
    else:
        db = mdot(g, a, 0, 0, gb, sa) if ca == 1 else mdot(g, a, 0, 1, gb, sa)
    return da, db


mdot.defvjp(_mdot_fwd, _mdot_bwd)


def mask_r(a, m, ca=1, cb=0):
    return mdot(a, m, ca, cb, 3, 1)


def mask_l(m, b, ca=1, cb=0):
    return mdot(m, b, ca, cb, 1, 3)


def ddot(a, b, ca=1, cb=0):
    return mdot(a, b, ca, cb, 2, 2)


def _head_mask(h):
    return ((_iota((1, GROUP), 1) >> 6) == h).astype(F32)


def _block_diag_mask():
    return ((_iota((GROUP, GROUP), 0) >> 6) == (_iota((GROUP, GROUP), 1) >> 6)).astype(F32)


def _expand_mat(offset):
    return ((_iota((128, GROUP), 0) - offset) == (_iota((128, GROUP), 1) >> 6)).astype(F32)


def _tril(n, strict=False):
    r, c = _iota((n, n), 0), _iota((n, n), 1)
    return (r > c) if strict else (r >= c)


def _row_pick(x, row):
    return jnp.sum(jnp.where(_iota(x.shape, 0) == row, x, 0.0), axis=0, keepdims=True)


def _shift_rows_impl(x, halo, j):
    n = x.shape[0]
    r = _iota(x.shape, 0)
    top = jnp.concatenate([pltpu.roll(halo, j, 0), jnp.zeros((n - HALO, x.shape[1]), x.dtype)], axis=0)
    return jnp.where(r >= j, pltpu.roll(x, j, 0), top)


def _mxu_round(a):
    return a.astype(MXU_DTYPE).astype(F32)


@functools.partial(jax.custom_vjp, nondiff_argnums=(3,))
def _causal_conv(x, halo, w, width):
    xb, hb, wb = _mxu_round(x), _mxu_round(halo), _mxu_round(w)
    out = xb * _row_pick(wb, width - 1)
    for j in range(1, width):
        out = out + _shift_rows_impl(xb, hb, j) * _row_pick(wb, width - 1 - j)
    return out


def _causal_conv_fwd(x, halo, w, width):
    return _causal_conv(x, halo, w, width), (x, halo, w)


def _causal_conv_bwd(width, res, g):
    x, halo, w = res
    xb, hb, wb, gb = _mxu_round(x), _mxu_round(halo), _mxu_round(w), _mxu_round(g)
    n = g.shape[0]
    rows, rows8 = _iota(g.shape, 0), _iota(halo.shape, 0)
    dx = gb * _row_pick(wb, width - 1)
    dh = jnp.zeros_like(halo)
    dw = jnp.where(rows8 == width - 1, jnp.sum(xb * gb, axis=0, keepdims=True), 0.0)
    for j in range(1, width):
        gj = gb * _row_pick(wb, width - 1 - j)
        dx = dx + jnp.where(rows < n - j, pltpu.roll(gj, n - j, 0), 0.0)
        dh = dh + jnp.where(rows8 >= HALO - j, pltpu.roll(gj[0:HALO], HALO - j, 0), 0.0)
        tap = jnp.sum(_shift_rows_impl(xb, hb, j) * gb, axis=0, keepdims=True)
        dw = dw + jnp.where(rows8 == width - 1 - j, tap, 0.0)
    return dx, dh, dw


_causal_conv.defvjp(_causal_conv_fwd, _causal_conv_bwd)


def _head_sum(x, bd):
    return mask_r(x, bd)


def _softplus(x):
    return jnp.maximum(x, 0.0) + jnp.log1p(jnp.exp(-jnp.abs(x)))


def _log_sigmoid(x):
    return -_softplus(-x)


def _silu(x):
    return x * jax.nn.sigmoid(x)


def _head_rmsnorm_gate(o, nw, z, bd):
    ms = _head_sum(o * o, bd) * (1.0 / HEAD_DIM)
    return o * lax.rsqrt(ms + EPS) * nw * _silu(z)


def _sgu_chunk(pu, pv, ln_w, ln_b, ws0, ws1, ws2, ws3, bs_t):
    u = jax.nn.gelu(pu)
    g = jax.nn.gelu(pv)
    mu = jnp.mean(g, axis=-1, keepdims=True)
    var = jnp.mean(jnp.square(g - mu), axis=-1, keepdims=True)
    v = (g - mu) * lax.rsqrt(var + EPS) * ln_w + ln_b
    keep = _tril(SGU_CHUNK)
    bias = mask_r(bs_t, _expand_mat(0))
    causal_ws = [jnp.where(keep, ws, 0.0) for ws in (ws0, ws1, ws2, ws3)]
    mixed = []
    for ci in range(pu.shape[0] // SGU_CHUNK):
        v_c = v[ci * SGU_CHUNK:(ci + 1) * SGU_CHUNK]
        m_c = bias
        for h in range(HEADS):
            m_c = m_c + _head_mask(h) * bdot(causal_ws[h], v_c, 1, 0)
        mixed.append(m_c)
    return u * jnp.concatenate(mixed, axis=0)


def _sc_chunk(pb, pc, ph, halo_c, halo_h, cw):
    return pb * _causal_conv(pc * ph, halo_c * halo_h, cw, 3)


def _neumann_inverses(lows):
    n = lows[0].shape[0]
    eye = (_iota((n, n), 0) == _iota((n, n), 1)).astype(F32)
    a = [-low for low in lows]
    t = [eye + x for x in a]
    for _ in range(5):
        a = [ddot(x, x) for x in a]
        t = [ti + ddot(ti, ai) for ti, ai in zip(t, a)]
    return t


@jax.custom_vjp
def _saved_inverse(low, inv):
    return inv


def _saved_inverse_fwd(low, inv):
    return inv, inv


def _saved_inverse_bwd(inv, g):
    return -ddot(ddot(inv, g, 0, 0), inv, 1, 1), jnp.zeros_like(inv)


_saved_inverse.defvjp(_saved_inverse_fwd, _saved_inverse_bwd)


def _chunk_tril(rows):
    r, c = _iota((rows, rows), 0), _iota((rows, rows), 1)
    return ((r >> 6) == (c >> 6)) & (r >= c)


def _dn_block(pq, pk, pv, hq, hk, hv, small, pz, cwq, cwk, cwv, a_log, dt_bias, nw, state, saved_inv=None):
    c = SCAN_CHUNK
    rows = pq.shape[0]
    bd = _block_diag_mask()
    q = _silu(_causal_conv(pq, hq, cwq, 4))
    k = _silu(_causal_conv(pk, hk, cwk, 4))
    v = _silu(_causal_conv(pv, hv, cwv, 4))
    q = q * lax.rsqrt(_head_sum(q * q, bd) + EPS) * (HEAD_DIM ** -0.5)
    k = k * lax.rsqrt(_head_sum(k * k, bd) + EPS)
    lane = _iota((1, 128), 1)
    g = jnp.where(lane < HEADS, -jnp.exp(a_log) * _softplus(small + dt_bias), 0.0)
    beta_b = mask_r(jax.nn.sigmoid(small), _expand_mat(HEADS))
    gc_all = mask_l(_chunk_tril(rows).astype(F32), g)
    gcb_all = mask_r(gc_all, _expand_mat(0))
    kb_all = k * beta_b
    vb_all = v * beta_b
    kbe_all = kb_all * jnp.exp(gcb_all)
    qg_all = q * jnp.exp(gcb_all)
    causal, strict = _tril(c), _tril(c, strict=True)
    nc = rows // c
    pairs = [(ci, h) for ci in range(nc) for h in range(HEADS)]
    sls = [slice(ci * c, (ci + 1) * c) for ci in range(nc)]
    decays, lows, attns = [], [], []
    for ci, h in pairs:
        gc = gc_all[sls[ci]]
        onehot = (_iota((c, 128), 1) == h).astype(F32)
        col = mask_l(onehot, gc, 1, 1)
        row = jnp.sum(gc * onehot, axis=1, keepdims=True)
        decays.append(jnp.exp(jnp.where(causal, row - col, -jnp.inf)))
    for j, (ci, h) in enumerate(pairs):
        mh = _head_mask(h)
        k_c = k[sls[ci]]
        lows.append(jnp.where(strict, bdot(kb_all[sls[ci]] * mh, k_c, 1, 1) * decays[j], 0.0))
        attns.append(bdot(q[sls[ci]] * mh, k_c, 1, 1) * decays[j])
    if saved_inv is None:
        invs = _neumann_inverses(lows)
    else:
        invs = [_saved_inverse(low, s) for low, s in zip(lows, saved_inv)]
    us, ws = [], []
    for ci in range(nc):
        u = jnp.zeros((c, GROUP), F32)
        w = jnp.zeros((c, GROUP), F32)
        for h in range(HEADS):
            mh = _head_mask(h)
            u = u + mh * ddot(invs[ci * HEADS + h], vb_all[sls[ci]])
            w = w + mh * ddot(invs[ci * HEADS + h], kbe_all[sls[ci]])
        us.append(u)
        ws.append(w)
    outs = []
    for ci in range(nc):
        gc_b = gcb_all[sls[ci]]
        gc_last_b = _row_pick(gc_b, c - 1)
        v_new = us[ci] - bdot(ws[ci], state, 1, 0)
        o = bdot(qg_all[sls[ci]], state, 1, 0)
        for h in range(HEADS):
            o = o + _head_mask(h) * bdot(attns[ci * HEADS + h], v_new, 1, 0)
        k_dec = k[sls[ci]] * jnp.exp(gc_last_b - gc_b)
        state = state * jnp.exp(gc_last_b) + bd * bdot(k_dec, v_new, 0, 0)
        outs.append(o)
    o = jnp.concatenate(outs, axis=0)
    return _head_rmsnorm_gate(o, nw, pz, bd), state, invs


def _gla_chunk(pq, pk, pv, small, pz, w2, gbias, nw, state_t):
    c = SCAN_CHUNK
    rows = pq.shape[0]
    nc = rows // c
    sls = [slice(ci * c, (ci + 1) * c) for ci in range(nc)]
    bd = _block_diag_mask()
    log_a = _log_sigmoid(bdot(small, w2, 1, 0) + gbias) * (1.0 / 16.0)
    gcum = mask_l(_chunk_tril(rows).astype(F32), log_a)
    r, s = _iota((rows, rows), 0), _iota((rows, rows), 1)
    base = (r >> 6) << 6
    g_mid = mask_l((s == base + c // 2).astype(F32), gcum)
    g_last = mask_l((s == base + c - 1).astype(F32), gcum)
    q = pq * (HEAD_DIM ** -0.5)
    qa = q * jnp.exp(gcum - g_mid)
    ka = pk * jnp.exp(g_mid - gcum)
    qg = q * jnp.exp(gcum)
    k_last = pk * jnp.exp(g_last - gcum)
    causal = _tril(c)
    attns = [jnp.where(causal, bdot(qa[sls[ci]] * _head_mask(h), ka[sls[ci]], 1, 1), 0.0)
             for ci in range(nc) for h in range(HEADS)]
    intra = []
    for ci in range(nc):
        o = jnp.zeros((c, GROUP), F32)
        for h in range(HEADS):
            o = o + _head_mask(h) * bdot(attns[ci * HEADS + h], pv[sls[ci]], 1, 0)
        intra.append(o)
    kvs = [bd * bdot(pv[sls[ci]], k_last[sls[ci]], 0, 0) for ci in range(nc)]
    states = []
    for ci in range(nc):
        states.append(state_t)
        state_t = state_t * jnp.exp(_row_pick(g_last[sls[ci]], 0)) + kvs[ci]
    outs = [intra[ci] + bdot(qg[sls[ci]], states[ci], 1, 1) for ci in range(nc)]
    o = jnp.concatenate(outs, axis=0)
    return _head_rmsnorm_gate(o, nw, pz, bd), state_t


def _col_spec(rows, group, rev_n=None):
    if rev_n is None:
        return pl.BlockSpec((rows, GROUP), lambda i: (i, group))
    return pl.BlockSpec((rows, GROUP), lambda i: (rev_n - 1 - i, group))


def _small_spec(rows, group128, rev_n=None):
    if rev_n is None:
        return pl.BlockSpec((rows, 128), lambda i: (i, group128))
    return pl.BlockSpec((rows, 128), lambda i: (rev_n - 1 - i, group128))


def _halo_spec(rows, group, rev_n=None):
    per = rows // HALO
    if rev_n is None:
        return pl.BlockSpec((HALO, GROUP), lambda i: (jnp.maximum(i * per - 1, 0), group))
    return pl.BlockSpec((HALO, GROUP), lambda i: (jnp.maximum((rev_n - 1 - i) * per - 1, 0), group))


def _full_spec(shape):
    nd = len(shape)
    return pl.BlockSpec(shape, lambda i: (0,) * nd)


def _out_rows_spec(rows, lanes, rev_n=None):
    if rev_n is None:
        return pl.BlockSpec((rows, lanes), lambda i: (i, 0))
    return pl.BlockSpec((rows, lanes), lambda i: (rev_n - 1 - i, 0))


SGU_ROWS = 4 * SGU_CHUNK


def _sgu_fwd(p, ln_w, ln_b, ws, bs_t):
    t = p.shape[0]
    n = t // SGU_ROWS

    def body(pu_ref, pv_ref, lw_ref, lb_ref, ws_ref, bs_ref, y_ref):
        y = _sgu_chunk(pu_ref[...], pv_ref[...], lw_ref[...], lb_ref[...],
                       ws_ref[0], ws_ref[1], ws_ref[2], ws_ref[3], bs_ref[...])
        y_ref[...] = y.astype(y_ref.dtype)

    return pl.pallas_call(
        body, name="sgu_fwd", grid=(n,),
        in_specs=[_col_spec(SGU_ROWS, COL_AU), _col_spec(SGU_ROWS, COL_AV), _full_spec((1, GROUP)),
                  _full_spec((1, GROUP)), _full_spec((HEADS, SGU_CHUNK, SGU_CHUNK)), _full_spec((SGU_CHUNK, 128))],
        out_specs=_out_rows_spec(SGU_ROWS, GROUP),
        out_shape=jax.ShapeDtypeStruct((t, GROUP), BF16),
        compiler_params=_cparams(("arbitrary",)),
    )(p, p, ln_w, ln_b, ws, bs_t)


def _sgu_bwd(p, dmix, ln_w, ln_b, ws, bs_t):
    t = p.shape[0]
    n = t // SGU_ROWS

    def body(pu_ref, pv_ref, dy_ref, lw_ref, lb_ref, ws_ref, bs_ref,
             dpu_ref, dpv_ref, dlw_ref, dlb_ref, dws_ref, dbs_ref):
        args = (pu_ref[...], pv_ref[...], lw_ref[...], lb_ref[...],
                ws_ref[0], ws_ref[1], ws_ref[2], ws_ref[3], bs_ref[...])
        _, vjp = jax.vjp(_sgu_chunk, *args)
        dpu, dpv, dlw, dlb, d0, d1, d2, d3, dbs = vjp(dy_ref[...])
        dpu_ref[...] = dpu.astype(dpu_ref.dtype)
        dpv_ref[...] = dpv.astype(dpv_ref.dtype)

        @pl.when(pl.program_id(0) == 0)
        def _():
            dlw_ref[...] = jnp.zeros_like(dlw_ref)
            dlb_ref[...] = jnp.zeros_like(dlb_ref)
            dws_ref[...] = jnp.zeros_like(dws_ref)
            dbs_ref[...] = jnp.zeros_like(dbs_ref)

        dlw_ref[...] += dlw
        dlb_ref[...] += dlb
        for h, d in enumerate((d0, d1, d2, d3)):
            dws_ref[h] += d
        dbs_ref[...] += dbs

    return pl.pallas_call(
        body, name="sgu_bwd", grid=(n,),
        in_specs=[_col_spec(SGU_ROWS, COL_AU), _col_spec(SGU_ROWS, COL_AV),
                  pl.BlockSpec((SGU_ROWS, GROUP), lambda i: (i, 0)),
                  _full_spec((1, GROUP)), _full_spec((1, GROUP)), _full_spec((HEADS, SGU_CHUNK, SGU_CHUNK)),
                  _full_spec((SGU_CHUNK, 128))],
        out_specs=[_out_rows_spec(SGU_ROWS, GROUP), _out_rows_spec(SGU_ROWS, GROUP), _full_spec((1, GROUP)),
                   _full_spec((1, GROUP)), _full_spec((HEADS, SGU_CHUNK, SGU_CHUNK)), _full_spec((SGU_CHUNK, 128))],
        out_shape=[jax.ShapeDtypeStruct((t, GROUP), BF16), jax.ShapeDtypeStruct((t, GROUP), BF16),
                   jax.ShapeDtypeStruct((1, GROUP), F32), jax.ShapeDtypeStruct((1, GROUP), F32),
                   jax.ShapeDtypeStruct((HEADS, SGU_CHUNK, SGU_CHUNK), F32),
                   jax.ShapeDtypeStruct((SGU_CHUNK, 128), F32)],
        compiler_params=_cparams(("arbitrary",)),
    )(p, p, dmix, ln_w, ln_b, ws, bs_t)


SC_ROWS = 512


def _first_block_zero(halo, first):
    return jnp.where(first, 0.0, halo)


def _sc_fwd(p, cw):
    t = p.shape[0]
    n = t // SC_ROWS

    def body(pb_ref, pc_ref, ph_ref, hc_ref, hh_ref, cw_ref, y_ref):
        first = pl.program_id(0) == 0
        y = _sc_chunk(pb_ref[...], pc_ref[...], ph_ref[...], _first_block_zero(hc_ref[...], first),
                      _first_block_zero(hh_ref[...], first), cw_ref[...])
        y_ref[...] = y.astype(y_ref.dtype)

    return pl.pallas_call(
        body, name="sc_fwd", grid=(n,),
        in_specs=[_col_spec(SC_ROWS, COL_BB), _col_spec(SC_ROWS, COL_BC), _col_spec(SC_ROWS, COL_BH),
                  _halo_spec(SC_ROWS, COL_BC), _halo_spec(SC_ROWS, COL_BH), _full_spec((HALO, GROUP))],
        out_specs=_out_rows_spec(SC_ROWS, GROUP),
        out_shape=jax.ShapeDtypeStruct((t, GROUP), BF16),
        compiler_params=_cparams(("arbitrary",)),
    )(p, p, p, p, p, cw)


def _add_halo_grad(d, carry):
    return d + jnp.concatenate([jnp.zeros((d.shape[0] - HALO, d.shape[1]), d.dtype), carry], axis=0)


def _sc_bwd(p, dmix, cw):
    t = p.shape[0]
    n = t // SC_ROWS

    def body(pb_ref, pc_ref, ph_ref, hc_ref, hh_ref, dy_ref, cw_ref,
             dpb_ref, dpc_ref, dph_ref, dcw_ref, carry_c, carry_h):
        i = pl.program_id(0)
        first = i == n - 1

        @pl.when(i == 0)
        def _():
            carry_c[...] = jnp.zeros_like(carry_c)
            carry_h[...] = jnp.zeros_like(carry_h)
            dcw_ref[...] = jnp.zeros_like(dcw_ref)

        args = (pb_ref[...], pc_ref[...], ph_ref[...], _first_block_zero(hc_ref[...], first),
                _first_block_zero(hh_ref[...], first), cw_ref[...])
        _, vjp = jax.vjp(_sc_chunk, *args)
        dpb, dpc, dph, dhc, dhh, dcw = vjp(dy_ref[...])
        dpb_ref[...] = dpb.astype(dpb_ref.dtype)
        dpc_ref[...] = _add_halo_grad(dpc, carry_c[...]).astype(dpc_ref.dtype)
        dph_ref[...] = _add_halo_grad(dph, carry_h[...]).astype(dph_ref.dtype)
        carry_c[...] = dhc
        carry_h[...] = dhh
        dcw_ref[...] += dcw

    return pl.pallas_call(
        body, name="sc_bwd", grid=(n,),
        in_specs=[_col_spec(SC_ROWS, COL_BB, n), _col_spec(SC_ROWS, COL_BC, n), _col_spec(SC_ROWS, COL_BH, n),
                  _halo_spec(SC_ROWS, COL_BC, n), _halo_spec(SC_ROWS, COL_BH, n),
                  pl.BlockSpec((SC_ROWS, GROUP), lambda i: (n - 1 - i, 1)), _full_spec((HALO, GROUP))],
        out_specs=[_out_rows_spec(SC_ROWS, GROUP, n)] * 3 + [_full_spec((HALO, GROUP))],
        out_shape=[jax.ShapeDtypeStruct((t, GROUP), BF16)] * 3 + [jax.ShapeDtypeStruct((HALO, GROUP), F32)],
        scratch_shapes=[pltpu.VMEM((HALO, GROUP), F32), pltpu.VMEM((HALO, GROUP), F32)],
        compiler_params=_cparams(("arbitrary",)),
    )(p, p, p, p, p, dmix, cw)


SCAN_STEP_CHUNKS = 4
SCAN_ROWS = SCAN_STEP_CHUNKS * SCAN_CHUNK


def _host_call(body, hosted, *, name, grid, in_specs, out_specs, out_shape, scratch_shapes, args):
    params = _cparams(("arbitrary",))
    if hosted is None:
        outs = pl.pallas_call(body, name=name, grid=grid, in_specs=in_specs, out_specs=out_specs,
                              out_shape=out_shape, scratch_shapes=scratch_shapes, compiler_params=params)(*args)
        return outs, None
    stage, arrays, modes, bufs = hosted
    arrays = list(arrays) if stage == 1 else []
    n_in, n_out, n_scr, n_src, na = len(in_specs), len(out_specs), len(scratch_shapes), len(arrays), len(bufs)
    last = grid[0] - 1

    def new_body(*refs):
        srcs = refs[n_in:n_in + n_src]
        o0 = n_in + n_src + na
        ex = refs[o0 + n_out:o0 + n_out + na]
        s0 = o0 + n_out + na
        sems = refs[s0 + n_scr:]
        i = pl.program_id(0)

        def run(what):
            if stage == 1:
                _exchange_stage1(srcs, ex, modes, sems[0], sems[1], what)
            else:
                _exchange_stage2(ex, sems[0], sems[1], what)

        @pl.when(i == 0)
        def _():
            run("start")

        body(*refs[:n_in], *refs[o0:o0 + n_out], *refs[s0:s0 + n_scr])

        @pl.when(i == last)
        def _():
            run("wait")

    any_spec = pl.BlockSpec(memory_space=pl.ANY)
    outs = pl.pallas_call(
        new_body, name=name, grid=grid,
        in_specs=list(in_specs) + [any_spec] * (n_src + na), out_specs=list(out_specs) + [any_spec] * na,
        out_shape=list(out_shape) + [jax.ShapeDtypeStruct(b.shape, b.dtype) for b in bufs],
        input_output_aliases={n_in + n_src + a: n_out + a for a in range(na)},
        scratch_shapes=list(scratch_shapes) + (_stage1_sems(na) if stage == 1 else _stage2_sems(na)),
        compiler_params=params,
    )(*args, *arrays, *bufs)
    return outs[:n_out], outs[n_out:]


def _dn_fwd(p, cw3, a_log, dt_bias, nw, hosted=None):
    t = p.shape[0]
    r = SCAN_ROWS
    n = t // r

    def body(pq_ref, pk_ref, pv_ref, hq_ref, hk_ref, hv_ref, sm_ref, pz_ref, cw_ref, al_ref, dt_ref, nw_ref,
             y_ref, ck_ref, inv_ref, state):
        first = pl.program_id(0) == 0

        @pl.when(first)
        def _():
            state[...] = jnp.zeros_like(state)

        s_in = state[...]
        ck_ref[0] = s_in
        y, s_out, invs = _dn_block(pq_ref[...], pk_ref[...], pv_ref[...], _first_block_zero(hq_ref[...], first),
                                   _first_block_zero(hk_ref[...], first), _first_block_zero(hv_ref[...], first),
                                   sm_ref[...], pz_ref[...], cw_ref[0], cw_ref[1], cw_ref[2],
                                   al_ref[...], dt_ref[...], nw_ref[...], s_in)
        y_ref[...] = y.astype(y_ref.dtype)
        state[...] = s_out
        for j, inv in enumerate(invs):
            inv_ref[j] = inv

    nh = SCAN_STEP_CHUNKS * HEADS
    return _host_call(
        body, hosted, name="dn_fwd", grid=(n,),
        in_specs=[_col_spec(r, COL_CQ), _col_spec(r, COL_CK), _col_spec(r, COL_CV),
                  _halo_spec(r, COL_CQ), _halo_spec(r, COL_CK), _halo_spec(r, COL_CV),
                  _small_spec(r, COL128_SMALL_C), _col_spec(r, COL_CZ), _full_spec((3, HALO, GROUP)),
                  _full_spec((1, 128)), _full_spec((1, 128)), _full_spec((1, GROUP))],
        out_specs=[_out_rows_spec(r, GROUP), pl.BlockSpec((1, GROUP, GROUP), lambda i: (i, 0, 0)),
                   pl.BlockSpec((nh, SCAN_CHUNK, SCAN_CHUNK), lambda i: (i, 0, 0))],
        out_shape=[jax.ShapeDtypeStruct((t, GROUP), BF16), jax.ShapeDtypeStruct((n, GROUP, GROUP), F32),
                   jax.ShapeDtypeStruct((n * nh, SCAN_CHUNK, SCAN_CHUNK), F32)],
        scratch_shapes=[pltpu.VMEM((GROUP, GROUP), F32)],
        args=(p, p, p, p, p, p, p, p, cw3, a_log, dt_bias, nw))


def _dn_bwd(p, dmix, states, invs, cw3, a_log, dt_bias, nw, hosted=None):
    t = p.shape[0]
    c = SCAN_ROWS
    n = t // c
    nh = SCAN_STEP_CHUNKS * HEADS

    def body(pq_ref, pk_ref, pv_ref, hq_ref, hk_ref, hv_ref, sm_ref, pz_ref, dy_ref, ck_ref, inv_ref,
             cw_ref, al_ref, dt_ref, nw_ref,
             dpq_ref, dpk_ref, dpv_ref, dsm_ref, dpz_ref, dcw_ref, dal_ref, ddt_ref, dnw_ref,
             dstate, carry):
        i = pl.program_id(0)
        first = i == n - 1

        @pl.when(i == 0)
        def _():
            dstate[...] = jnp.zeros_like(dstate)
            carry[...] = jnp.zeros_like(carry)
            dcw_ref[...] = jnp.zeros_like(dcw_ref)
            dal_ref[...] = jnp.zeros_like(dal_ref)
            ddt_ref[...] = jnp.zeros_like(ddt_ref)
            dnw_ref[...] = jnp.zeros_like(dnw_ref)

        args = (pq_ref[...], pk_ref[...], pv_ref[...], _first_block_zero(hq_ref[...], first),
                _first_block_zero(hk_ref[...], first), _first_block_zero(hv_ref[...], first),
                sm_ref[...], pz_ref[...], cw_ref[0], cw_ref[1], cw_ref[2],
                al_ref[...], dt_ref[...], nw_ref[...], ck_ref[0])
        saved = [inv_ref[j] for j in range(nh)]
        _, vjp = jax.vjp(lambda *a: _dn_block(*a, saved_inv=saved)[:2], *args)
        (dpq, dpk, dpv, dhq, dhk, dhv, dsm, dpz, dcq, dck, dcv, dal, ddt, dnw, dst) = vjp(
            (dy_ref[...], dstate[...]))
        dpq_ref[...] = _add_halo_grad(dpq, carry[0]).astype(dpq_ref.dtype)
        dpk_ref[...] = _add_halo_grad(dpk, carry[1]).astype(dpk_ref.dtype)
        dpv_ref[...] = _add_halo_grad(dpv, carry[2]).astype(dpv_ref.dtype)
        dsm_ref[...] = dsm.astype(dsm_ref.dtype)
        dpz_ref[...] = dpz.astype(dpz_ref.dtype)
        carry[0] = dhq
        carry[1] = dhk
        carry[2] = dhv
        dstate[...] = dst
        dcw_ref[0] += dcq
        dcw_ref[1] += dck
        dcw_ref[2] += dcv
        dal_ref[...] += dal
        ddt_ref[...] += ddt
        dnw_ref[...] += dnw

    return _host_call(
        body, hosted, name="dn_bwd", grid=(n,),
        in_specs=[_col_spec(c, COL_CQ, n), _col_spec(c, COL_CK, n), _col_spec(c, COL_CV, n),
                  _halo_spec(c, COL_CQ, n), _halo_spec(c, COL_CK, n), _halo_spec(c, COL_CV, n),
                  _small_spec(c, COL128_SMALL_C, n), _col_spec(c, COL_CZ, n),
                  pl.BlockSpec((c, GROUP), lambda i: (n - 1 - i, 2)),
                  pl.BlockSpec((1, GROUP, GROUP), lambda i: (n - 1 - i, 0, 0)),
                  pl.BlockSpec((nh, SCAN_CHUNK, SCAN_CHUNK), lambda i: (n - 1 - i, 0, 0)),
                  _full_spec((3, HALO, GROUP)), _full_spec((1, 128)), _full_spec((1, 128)), _full_spec((1, GROUP))],
        out_specs=[_out_rows_spec(c, GROUP, n)] * 3 + [_out_rows_spec(c, 128, n), _out_rows_spec(c, GROUP, n),
                   _full_spec((3, HALO, GROUP)), _full_spec((1, 128)), _full_spec((1, 128)), _full_spec((1, GROUP))],
        out_shape=[jax.ShapeDtypeStruct((t, GROUP), BF16)] * 3 + [
            jax.ShapeDtypeStruct((t, 128), BF16), jax.ShapeDtypeStruct((t, GROUP), BF16),
            jax.ShapeDtypeStruct((3, HALO, GROUP), F32), jax.ShapeDtypeStruct((1, 128), F32),
            jax.ShapeDtypeStruct((1, 128), F32), jax.ShapeDtypeStruct((1, GROUP), F32)],
        scratch_shapes=[pltpu.VMEM((GROUP, GROUP), F32), pltpu.VMEM((3, HALO, GROUP), F32)],
        args=(p, p, p, p, p, p, p, p, dmix, states, invs, cw3, a_log, dt_bias, nw))


def _gla_fwd(p, w2, gbias, nw, hosted=None):
    t = p.shape[0]
    c = SCAN_ROWS
    n = t // c

    def body(pq_ref, pk_ref, pv_ref, sm_ref, pz_ref, w2_ref, gb_ref, nw_ref, y_ref, ck_ref, state):
        @pl.when(pl.program_id(0) == 0)
        def _():
            state[...] = jnp.zeros_like(state)

        s_in = state[...]
        ck_ref[0] = s_in
        y, s_out = _gla_chunk(pq_ref[...], pk_ref[...], pv_ref[...], sm_ref[...], pz_ref[...],
                              w2_ref[...], gb_ref[...], nw_ref[...], s_in)
        y_ref[...] = y.astype(y_ref.dtype)
        state[...] = s_out

    return _host_call(
        body, hosted, name="gla_fwd", grid=(n,),
        in_specs=[_col_spec(c, COL_DQ), _col_spec(c, COL_DK), _col_spec(c, COL_DV),
                  _small_spec(c, COL128_SMALL_D), _col_spec(c, COL_DZ),
                  _full_spec((128, GROUP)), _full_spec((1, GROUP)), _full_spec((1, GROUP))],
        out_specs=[_out_rows_spec(c, GROUP), pl.BlockSpec((1, GROUP, GROUP), lambda i: (i, 0, 0))],
        out_shape=[jax.ShapeDtypeStruct((t, GROUP), BF16), jax.ShapeDtypeStruct((n, GROUP, GROUP), F32)],
        scratch_shapes=[pltpu.VMEM((GROUP, GROUP), F32)],
        args=(p, p, p, p, p, w2, gbias, nw))


def _gla_bwd(p, dmix, states, w2, gbias, nw, hosted=None):
    t = p.shape[0]
    c = SCAN_ROWS
    n = t // c

    def body(pq_ref, pk_ref, pv_ref, sm_ref, pz_ref, dy_ref, ck_ref, w2_ref, gb_ref, nw_ref,
             dpq_ref, dpk_ref, dpv_ref, dsm_ref, dpz_ref, dw2_ref, dgb_ref, dnw_ref, dstate):
        @pl.when(pl.program_id(0) == 0)
        def _():
            dstate[...] = jnp.zeros_like(dstate)
            dw2_ref[...] = jnp.zeros_like(dw2_ref)
            dgb_ref[...] = jnp.zeros_like(dgb_ref)
            dnw_ref[...] = jnp.zeros_like(dnw_ref)

        args = (pq_ref[...], pk_ref[...], pv_ref[...], sm_ref[...], pz_ref[...],
                w2_ref[...], gb_ref[...], nw_ref[...], ck_ref[0])
        _, vjp = jax.vjp(_gla_chunk, *args)
        dpq, dpk, dpv, dsm, dpz, dw2, dgb, dnw, dst = vjp((dy_ref[...], dstate[...]))
        dpq_ref[...] = dpq.astype(dpq_ref.dtype)
        dpk_ref[...] = dpk.astype(dpk_ref.dtype)
        dpv_ref[...] = dpv.astype(dpv_ref.dtype)
        dsm_ref[...] = dsm.astype(dsm_ref.dtype)
        dpz_ref[...] = dpz.astype(dpz_ref.dtype)
        dstate[...] = dst
        dw2_ref[...] += dw2
        dgb_ref[...] += dgb
        dnw_ref[...] += dnw

    return _host_call(
        body, hosted, name="gla_bwd", grid=(n,),
        in_specs=[_col_spec(c, COL_DQ, n), _col_spec(c, COL_DK, n), _col_spec(c, COL_DV, n),
                  _small_spec(c, COL128_SMALL_D, n), _col_spec(c, COL_DZ, n),
                  pl.BlockSpec((c, GROUP), lambda i: (n - 1 - i, 3)),
                  pl.BlockSpec((1, GROUP, GROUP), lambda i: (n - 1 - i, 0, 0)),
                  _full_spec((128, GROUP)), _full_spec((1, GROUP)), _full_spec((1, GROUP))],
        out_specs=[_out_rows_spec(c, GROUP, n)] * 3 + [_out_rows_spec(c, 128, n), _out_rows_spec(c, GROUP, n),
                   _full_spec((128, GROUP)), _full_spec((1, GROUP)), _full_spec((1, GROUP))],
        out_shape=[jax.ShapeDtypeStruct((t, GROUP), BF16)] * 3 + [
            jax.ShapeDtypeStruct((t, 128), BF16), jax.ShapeDtypeStruct((t, GROUP), BF16),
            jax.ShapeDtypeStruct((128, GROUP), F32), jax.ShapeDtypeStruct((1, GROUP), F32),
            jax.ShapeDtypeStruct((1, GROUP), F32)],
        scratch_shapes=[pltpu.VMEM((GROUP, GROUP), F32)],
        args=(p, p, p, p, p, dmix, states, w2, gbias, nw))


def _pick_tile(n, pref):
    for cand in pref:
        if n % cand == 0:
            return cand
    return n


MM_TILE_CAP = 1408


def _largest_tile(n, cap):
    best = None
    for mult in range(1, cap // 128 + 1):
        if n % (128 * mult) == 0:
            best = 128 * mult
    return best if best is not None else n


def _half_index(t, per_half, middle):
    half = jnp.where(t >= per_half, 1, 0)
    return half, middle, t - half * per_half


def _matmul(a, b, mode, out_dtype, name, res=None, pieces=None):
    if mode == "nn":
        (m, k), n = a.shape, b.shape[1]
    elif mode == "nt":
        (m, k), n = a.shape, b.shape[0]
    else:
        (k, m), n = a.shape, b.shape[-1] * (2 if b.ndim == 3 else 1)
    tm = _largest_tile(m, MM_TILE_CAP)
    tn = _largest_tile(b.shape[-1] if b.ndim == 3 else n, MM_TILE_CAP)
    if pieces == "cols":
        tm, tn = m // 2, n // N_CHIPS
    tk = _largest_tile(k, MM_TILE_CAP)
    nk = k // tk
    if mode == "nn":
        a_spec = pl.BlockSpec((tm, tk), lambda i, j, kk: (i, kk))
        b_spec = pl.BlockSpec((tk, tn), lambda i, j, kk: (kk, j))
        dims = (1, 0)
    elif mode == "nt":
        a_spec = pl.BlockSpec((tm, tk), lambda i, j, kk: (i, kk))
        b_spec = pl.BlockSpec((tn, tk), lambda i, j, kk: (j, kk))
        dims = (1, 1)
    else:
        a_spec = pl.BlockSpec((tk, tm), lambda i, j, kk: (kk, i))
        if b.ndim == 3:
            njh = b.shape[-1] // tn
            b_spec = pl.BlockSpec((None, tk, tn), lambda i, j, kk: _half_index(j, njh, kk))
        else:
            b_spec = pl.BlockSpec((tk, tn), lambda i, j, kk: (kk, j))
        dims = (0, 0)
    o_spec = pl.BlockSpec((tm, tn), lambda i, j, kk: (i, j))
    o_shape = (m, n)
    if pieces == "cols":
        o_spec = pl.BlockSpec((None, None, tm, tn), lambda i, j, kk: (i, j, 0, 0))
        o_shape = (2, N_CHIPS, tm, tn)
    has_res = res is not None

    def body(*refs):
        a_ref, b_ref = refs[:2]
        r_ref = refs[2] if has_res else None
        o_ref = refs[3] if has_res else refs[2]
        part = _dg(a_ref[...].astype(MXU_DTYPE), b_ref[...].astype(MXU_DTYPE), *dims)

        def finish(out):
            if has_res:
                out = out + r_ref[...]
            o_ref[...] = out.astype(o_ref.dtype)

        if nk == 1:
            finish(part)
            return
        acc = refs[-1]
        kk = pl.program_id(2)

        @pl.when(kk == 0)
        def _():
            acc[...] = part

        @pl.when(kk > 0)
        def _():
            acc[...] += part

        @pl.when(kk == nk - 1)
        def _():
            finish(acc[...])

    in_specs = [a_spec, b_spec] + ([o_spec] if has_res else [])
    args = (a, b) + ((res,) if has_res else ())
    return pl.pallas_call(
        body, name=name, grid=(m // tm, n // tn, nk), in_specs=in_specs, out_specs=o_spec,
        out_shape=jax.ShapeDtypeStruct(o_shape, out_dtype),
        scratch_shapes=[pltpu.VMEM((tm, tn), F32)] if nk > 1 else [],
        compiler_params=_cparams(("parallel", "parallel", "arbitrary")),
    )(*args)


def _matmul_nt_norm_bwd(a, b, x, w, dres, name):
    n = b.shape[0]
    m = a.shape[-2]
    tm = _largest_tile(m, 1024)
    if a.ndim == 3:
        kh = a.shape[2]
        k = 2 * kh
        tk = _largest_tile(kh, MM_TILE_CAP)
        nkh = kh // tk
        a_spec = pl.BlockSpec((None, tm, tk), lambda i, kk: _half_index(kk, nkh, i))
    else:
        k = a.shape[1]
        tk = _largest_tile(k, MM_TILE_CAP)
        a_spec = pl.BlockSpec((tm, tk), lambda i, kk: (i, kk))
    nk = k // tk

    def body(a_ref, b_ref, x_ref, w_ref, r_ref, dx_ref, dw_ref, acc):
        i, kk = pl.program_id(0), pl.program_id(1)
        part = _dg(a_ref[...].astype(MXU_DTYPE), b_ref[...].astype(MXU_DTYPE), 1, 1)

        @pl.when(kk == 0)
        def _():
            acc[...] = part

        @pl.when(kk > 0)
        def _():
            acc[...] += part

        @pl.when((i == 0) & (kk == 0))
        def _():
            dw_ref[...] = jnp.zeros_like(dw_ref)

        @pl.when(kk == nk - 1)
        def _():
            g = acc[...]
            xv = x_ref[...]
            r = lax.rsqrt(jnp.mean(xv * xv, axis=-1, keepdims=True) + EPS)
            xhat = xv * r
            dw_ref[...] += jnp.sum(g * xhat, axis=0, keepdims=True)
            gx = g * w_ref[...]
            dx_ref[...] = r_ref[...] + r * (gx - xhat * jnp.mean(gx * xhat, axis=-1, keepdims=True))

    row_spec = pl.BlockSpec((tm, n), lambda i, kk: (i, 0))
    return pl.pallas_call(
        body, name=name, grid=(m // tm, nk),
        in_specs=[a_spec, pl.BlockSpec((n, tk), lambda i, kk: (0, kk)),
                  row_spec, pl.BlockSpec((1, n), lambda i, kk: (0, 0)), row_spec],
        out_specs=[row_spec, pl.BlockSpec((1, n), lambda i, kk: (0, 0))],
        out_shape=[jax.ShapeDtypeStruct((m, n), F32), jax.ShapeDtypeStruct((1, n), F32)],
        scratch_shapes=[pltpu.VMEM((tm, n), F32)],
        compiler_params=_cparams(("arbitrary", "arbitrary")),
    )(a, b, x, w, dres)


NORM_ROWS = 512


def _rmsnorm_fwd(x, w, name):
    t, d = x.shape

    def body(x_ref, w_ref, o_ref):
        xv = x_ref[...]
        r = lax.rsqrt(jnp.mean(xv * xv, axis=-1, keepdims=True) + EPS)
        o_ref[...] = (xv * r * w_ref[...]).astype(o_ref.dtype)

    return pl.pallas_call(
        body, name=name, grid=(t // NORM_ROWS,),
        in_specs=[pl.BlockSpec((NORM_ROWS, d), lambda i: (i, 0)), _full_spec((1, d))],
        out_specs=pl.BlockSpec((NORM_ROWS, d), lambda i: (i, 0)),
        out_shape=jax.ShapeDtypeStruct((t, d), BF16),
        compiler_params=_cparams(("parallel",)),
    )(x, w)


SWIGLU_ROWS = 128


def _ffn_up_swiglu(h, w_gate_up):
    m, k = h.shape
    tm = _largest_tile(m, 512)
    tn = _largest_tile(D_FF, MM_TILE_CAP)
    nj = D_FF // tn

    def body(a_ref, bg_ref, bu_ref, g_ref, u_ref, act_ref):
        a = a_ref[...].astype(MXU_DTYPE)
        gate = _dg(a, bg_ref[...].astype(MXU_DTYPE), 1, 0)
        up = _dg(a, bu_ref[...].astype(MXU_DTYPE), 1, 0)
        g_ref[...] = gate
        u_ref[...] = up
        act_ref[...] = (_silu(gate) * up).astype(act_ref.dtype)

    o_spec = pl.BlockSpec((tm, tn), lambda i, j: (i, j))
    return pl.pallas_call(
        body, name="ffn_up", grid=(m // tm, nj),
        in_specs=[pl.BlockSpec((tm, k), lambda i, j: (i, 0)), pl.BlockSpec((k, tn), lambda i, j: (0, j)),
                  pl.BlockSpec((k, tn), lambda i, j: (0, j + nj))],
        out_specs=[o_spec, o_spec, o_spec],
        out_shape=[jax.ShapeDtypeStruct((m, D_FF), F32), jax.ShapeDtypeStruct((m, D_FF), F32),
                   jax.ShapeDtypeStruct((m, D_FF), BF16)],
        compiler_params=_cparams(("parallel", "parallel")),
    )(h, w_gate_up, w_gate_up)


def _ffn_down_dx_swiglu(dx, w_down, gate, up):
    m, k = dx.shape
    tm = _largest_tile(m, 512)
    tn = _largest_tile(D_FF, MM_TILE_CAP)

    def body(a_ref, b_ref, g_ref, u_ref, o_ref):
        da = _dg(a_ref[...].astype(MXU_DTYPE), b_ref[...].astype(MXU_DTYPE), 1, 1)
        gate = g_ref[...]
        sg = jax.nn.sigmoid(gate)
        o_ref[0] = (da * u_ref[...] * (sg * (1.0 + gate * (1.0 - sg)))).astype(o_ref.dtype)
        o_ref[1] = (da * gate * sg).astype(o_ref.dtype)

    tile = pl.BlockSpec((tm, tn), lambda i, j: (i, j))
    return pl.pallas_call(
        body, name="ffn_down_dx", grid=(m // tm, D_FF // tn),
        in_specs=[pl.BlockSpec((tm, k), lambda i, j: (i, 0)), pl.BlockSpec((tn, k), lambda i, j: (j, 0)), tile, tile],
        out_specs=pl.BlockSpec((2, tm, tn), lambda i, j: (0, i, j)),
        out_shape=jax.ShapeDtypeStruct((2, m, D_FF), MXU_DTYPE),
        compiler_params=_cparams(("parallel", "parallel")),
    )(dx, w_down, gate, up)


def _loss_head(x, w, target):
    t, d = x.shape

    def fwd(xv, wv, tv):
        r = lax.rsqrt(jnp.mean(xv * xv, axis=-1, keepdims=True) + EPS)
        err = xv * r * wv - tv
        return 0.5 * jnp.sum(jnp.mean(err * err, axis=-1, keepdims=True), axis=0, keepdims=True)

    def body(x_ref, w_ref, t_ref, dx_ref, dw_ref, loss_ref):
        @pl.when(pl.program_id(0) == 0)
        def _():
            dw_ref[...] = jnp.zeros_like(dw_ref)
            loss_ref[...] = jnp.zeros_like(loss_ref)

        loss, vjp = jax.vjp(fwd, x_ref[...], w_ref[...], t_ref[...])
        dx, dw, _ = vjp(jnp.ones((1, 1), F32))
        dx_ref[...] = dx
        dw_ref[...] += dw
        loss_ref[...] += jnp.broadcast_to(loss, loss_ref.shape)

    return pl.pallas_call(
        body, name="loss_head", grid=(t // NORM_ROWS,),
        in_specs=[pl.BlockSpec((NORM_ROWS, d), lambda i: (i, 0)), _full_spec((1, d)),
                  pl.BlockSpec((NORM_ROWS, d), lambda i: (i, 0))],
        out_specs=[pl.BlockSpec((NORM_ROWS, d), lambda i: (i, 0)), _full_spec((1, d)), _full_spec((8, 128))],
        out_shape=[jax.ShapeDtypeStruct((t, d), F32), jax.ShapeDtypeStruct((1, d), F32),
                   jax.ShapeDtypeStruct((8, 128), F32)],
        compiler_params=_cparams(("arbitrary",)),
    )(x, w, target)


def _pad_w_in(w):
    z = lambda n: jnp.zeros((w.shape[0], n), w.dtype)
    return jnp.concatenate([w[:, 0:2048], w[:, 2056:2312], w[:, 2312:3080], w[:, 3096:3352],
                            w[:, 2048:2056], z(120), w[:, 3080:3096], z(112)], axis=1)


def _unpad_w_in(wp):
    return jnp.concatenate([wp[:, 0:2048], wp[:, 3328:3336], wp[:, 2048:2304], wp[:, 2304:3072],
                            wp[:, 3456:3472], wp[:, 3072:3328]], axis=1)


def _pad_rows(a, rows):
    return jnp.concatenate([a, jnp.zeros((rows - a.shape[0],) + a.shape[1:], a.dtype)], axis=0)


def _pad_lanes(a, lanes):
    return jnp.concatenate([a, jnp.zeros(a.shape[:-1] + (lanes - a.shape[-1],), a.dtype)], axis=-1)


def _layer_params(l, small):
    dn_cw = small["dn_conv_w"][l]
    return dict(
        ln_w=small["sgu_ln_w"][l][None], ln_b=small["sgu_ln_b"][l][None],
        ws=small["sgu_w_spatial"][l], bs_t=_pad_lanes(small["sgu_b_spatial"][l].T, 128),
        sc_cw=_pad_rows(small["sc_conv_w"][l], HALO),
        dn_cw=jnp.stack([_pad_rows(dn_cw[:, j * GROUP:(j + 1) * GROUP], HALO) for j in range(3)]),
        dn_al=_pad_lanes(small["dn_a_log"][l][None], 128), dn_dt=_pad_lanes(small["dn_dt_bias"][l][None], 128),
        dn_nw=jnp.tile(small["dn_norm_w"][l][None], (1, HEADS)),
        gla_w2=_pad_rows(small["gla_w_gate2"][l], 128), gla_gb=small["gla_gate_bias"][l][None],
        gla_nw=jnp.tile(small["gla_norm_w"][l][None], (1, HEADS)),
    )


def _exchange_piece(name, grad):
    if grad.ndim == 4:
        return grad
    if name == "w_in":
        grad = _unpad_w_in(grad)
    if name in ("w_in", "w_gate_up"):
        r, c4 = grad.shape
        return jnp.transpose(grad.reshape(2, r // 2, N_CHIPS, c4 // N_CHIPS), (0, 2, 1, 3))
    r4, c = grad.shape
    return jnp.transpose(grad.reshape(N_CHIPS, 2, r4 // (2 * N_CHIPS), c), (1, 0, 2, 3))


def _reduce_on_chip(pieces):
    return _pair_add(pieces, _sibling_swap(pieces))


def _local_step(x, target, big, small, late_weights=None, exchange=False, small_extra=None):
    saved = []
    h = x
    for l in range(DEPTH):
        lp = _layer_params(l, small)
        h1 = _rmsnorm_fwd(h, small["norm1_w"][l][None], "norm1_fwd")
        p = _matmul(h1, big["w_in"][l], "nn", F32, "proj_in")
        y_a = _sgu_fwd(p, lp["ln_w"], lp["ln_b"], lp["ws"], lp["bs_t"])
        y_b = _sc_fwd(p, lp["sc_cw"])
        host1 = host2 = None
        if late_weights is not None:
            shards, finish = late_weights[l]
            modes = ["layer"] * len(shards)
            host1 = (1, shards, modes, _exchange_buffers(shards, modes))
        (y_c, st_c, inv_c), ex = _dn_fwd(p, lp["dn_cw"], lp["dn_al"], lp["dn_dt"], lp["dn_nw"], hosted=host1)
        if host1 is not None:
            host2 = (2, None, None, ex)
        (y_d, st_d), ex = _gla_fwd(p, lp["gla_w2"], lp["gla_gb"], lp["gla_nw"], hosted=host2)
        if host2 is not None:
            big = finish(big, ex)
        mix = jnp.concatenate([y_a, y_b, y_c, y_d], axis=1)
        x1 = _matmul(mix, big["w_out"][l], "nn", F32, "proj_out", res=h)
        h2 = _rmsnorm_fwd(x1, small["norm2_w"][l][None], "norm2_fwd")
        gate, up, act = _ffn_up_swiglu(h2, big["w_gate_up"][l])
        x2 = _matmul(act, big["w_down"][l], "nn", F32, "ffn_down", res=x1)
        saved.append(dict(x0=h, h1=h1, p=p, st_c=st_c, inv_c=inv_c, st_d=st_d, mix=mix, x1=x1, h2=h2, gate=gate, up=up, act=act, lp=lp))
        h = x2

    dx, d_final, loss = _loss_head(h, small["final_norm_w"][None], target)
    gbig = {k: [None] * DEPTH for k in ("w_in", "w_out", "w_gate_up", "w_down")}
    gs = {k: [None] * DEPTH for k in ("norm1_w", "sgu_ln_w", "sgu_ln_b", "sgu_w_spatial", "sgu_b_spatial", "sc_conv_w",
                                     "dn_conv_w", "dn_a_log", "dn_dt_bias", "dn_norm_w", "gla_w_gate2",
                                     "gla_gate_bias", "gla_norm_w", "norm2_w")}
    carry = []
    contribs = {}
    for l in reversed(range(DEPTH)):
        s = saved[l]
        lp = s["lp"]
        gbig["w_down"][l] = _matmul(s["act"], dx, "tn", GRAD_WIRE_DTYPE, "ffn_down_dw")
        dgu = _ffn_down_dx_swiglu(dx, big["w_down"][l], s["gate"], s["up"])
        gbig["w_gate_up"][l] = _matmul(s["h2"], dgu, "tn", GRAD_WIRE_DTYPE, "ffn_up_dw",
                                       pieces="cols" if exchange else None)
        dx1, gs["norm2_w"][l] = _matmul_nt_norm_bwd(dgu, big["w_gate_up"][l], s["x1"], small["norm2_w"][l][None], dx,
                                                    "ffn_up_dx")
        gbig["w_out"][l] = _matmul(s["mix"], dx1, "tn", GRAD_WIRE_DTYPE, "proj_out_dw")
        dmix = _matmul(dx1, big["w_out"][l], "nt", F32, "proj_out_dx")
        p = s["p"]
        dpu, dpv, g_lw, g_lb, g_ws, g_bs = _sgu_bwd(p, dmix, lp["ln_w"], lp["ln_b"], lp["ws"], lp["bs_t"])
        dpb, dpc, dph, g_sc = _sc_bwd(p, dmix, lp["sc_cw"])
        host1 = host2 = None
        if exchange:
            unit = carry + [(n, l, _exchange_piece(n, gbig[n][l])) for n in ("w_out", "w_gate_up", "w_down")]
            carry = []
            summed = _reduce_on_chip([piece for _, _, piece in unit])
            modes = ["piece"] * len(summed)
            host1 = (1, summed, modes, _exchange_buffers(summed, modes))
        (dcq, dck, dcv, dcs, dcz, g_dcw, g_al, g_dt, g_dnw), ex = _dn_bwd(
            p, dmix, s["st_c"], s["inv_c"], lp["dn_cw"], lp["dn_al"], lp["dn_dt"], lp["dn_nw"], hosted=host1)
        if host1 is not None:
            host2 = (2, None, None, ex)
        (ddq, ddk, ddv, dds, ddz, g_w2, g_gb, g_gnw), ex = _gla_bwd(p, dmix, s["st_d"], lp["gla_w2"], lp["gla_gb"],
                                                                   lp["gla_nw"], hosted=host2)
        if host2 is not None:
            for (n, lay, _), got in zip(unit, ex):
                contribs[(n, lay)] = got
        dp = jnp.concatenate([dpu, dpv, dpb, dpc, dph, dcq, dck, dcv, dcz, ddq, ddk, ddv, ddz, dcs, dds], axis=1)
        gbig["w_in"][l] = _matmul(s["h1"], dp, "tn", GRAD_WIRE_DTYPE, "proj_in_dw")
        dx, gs["norm1_w"][l] = _matmul_nt_norm_bwd(dp, big["w_in"][l], s["x0"], small["norm1_w"][l][None], dx1,
                                                   "proj_in_dx")
        gs["sgu_ln_w"][l], gs["sgu_ln_b"][l] = g_lw[0], g_lb[0]
        gs["sgu_w_spatial"][l] = g_ws
        gs["sgu_b_spatial"][l] = g_bs[:, :HEADS].T
        gs["sc_conv_w"][l] = g_sc[:3]
        gs["dn_conv_w"][l] = jnp.concatenate([g_dcw[0, :4], g_dcw[1, :4], g_dcw[2, :4]], axis=1)
        gs["dn_a_log"][l], gs["dn_dt_bias"][l] = g_al[0, :HEADS], g_dt[0, :HEADS]
        gs["dn_norm_w"][l] = jnp.sum(g_dnw.reshape(HEADS, HEAD_DIM), axis=0)
        gs["gla_w_gate2"][l] = g_w2[:16]
        gs["gla_gate_bias"][l] = g_gb[0]
        gs["gla_norm_w"][l] = jnp.sum(g_gnw.reshape(HEADS, HEAD_DIM), axis=0)
        gs["norm1_w"][l] = gs["norm1_w"][l][0]
        gs["norm2_w"][l] = gs["norm2_w"][l][0]
        if exchange:
            carry = [("w_in", l, _exchange_piece("w_in", gbig["w_in"][l]))]
    gsmall = {k: jnp.stack(v) for k, v in gs.items()}
    gsmall["final_norm_w"] = d_final[0]
    if not exchange:
        return loss, dx, gbig, gsmall
    summed = _reduce_on_chip([piece for _, _, piece in carry])
    last = _chip_exchange(summed + [small_extra(gsmall, loss)], ["piece"] * len(summed) + ["whole"], "exchange_grads")
    for (n, lay, _), got in zip(carry, last):
        contribs[(n, lay)] = got
    return loss, dx, contribs, last[-1]


def _peer_chips(x, y):
    return [(1 - x, y, 2 * (1 - x) + y), (x, 1 - y, 2 * x + 1 - y), (1 - x, 1 - y, 2 * (1 - x) + 1 - y)]


def _chip_exchange(arrays, modes, name):
    na = len(arrays)
    bufs = _exchange_buffers(arrays, modes)

    def body(*refs):
        ins, outs = refs[:na], refs[2 * na:3 * na]
        send1, recv1, send2, recv2 = refs[3 * na:]
        _exchange_stage1(ins, outs, modes, send1, recv1, "start")
        _exchange_stage1(ins, outs, modes, send1, recv1, "wait")
        _exchange_stage2(outs, send2, recv2, "start")
        _exchange_stage2(outs, send2, recv2, "wait")

    any_spec = pl.BlockSpec(memory_space=pl.ANY)
    return pl.pallas_call(
        body, name=name,
        in_specs=[any_spec] * (2 * na), out_specs=[any_spec] * na,
        out_shape=[jax.ShapeDtypeStruct(b.shape, b.dtype) for b in bufs],
        input_output_aliases={na + a: a for a in range(na)},
        scratch_shapes=_stage1_sems(na) + _stage2_sems(na),
    )(*arrays, *bufs)


def _exchange_buffers(arrays, modes):
    c_idx = lax.axis_index("c")
    chip = 2 * lax.axis_index("x") + lax.axis_index("y")
    units = []
    for arr, md in zip(arrays, modes):
        if md == "layer":
            units.append(lax.dynamic_index_in_dim(arr, c_idx, 0, keepdims=False))
        elif md == "piece":
            units.append(lax.dynamic_index_in_dim(arr, chip, 0, keepdims=False))
        else:
            units.append(arr)
    any_spec = pl.BlockSpec(memory_space=pl.ANY)
    bufs = pl.pallas_call(
        lambda *refs: None, name="exchange_alloc", out_specs=[any_spec] * len(units),
        out_shape=[jax.ShapeDtypeStruct((2, N_CHIPS) + u.shape, u.dtype) for u in units],
    )()
    return [lax.dynamic_update_slice(buf, u[None, None], (c_idx, chip) + (0,) * u.ndim) for buf, u in zip(bufs, units)]


def _stage1_sems(na):
    return [pltpu.SemaphoreType.DMA((na, 3)), pltpu.SemaphoreType.DMA((na, 3))]


def _stage2_sems(na):
    return [pltpu.SemaphoreType.DMA((na,)), pltpu.SemaphoreType.DMA((na,))]


def _exchange_stage1(ins, outs, modes, send1, recv1, what):
    x, y, c = lax.axis_index("x"), lax.axis_index("y"), lax.axis_index("c")
    me = 2 * x + y
    for a in range(len(ins)):
        for k, (px, py, pidx) in enumerate(_peer_chips(x, y)):
            if modes[a] == "layer":
                src = ins[a].at[c]
            else:
                src = ins[a].at[pidx] if modes[a] == "piece" else ins[a]
            if what == "start":
                pltpu.make_async_remote_copy(
                    src_ref=src, dst_ref=outs[a].at[c, me], send_sem=send1.at[a, k], recv_sem=recv1.at[a, k],
                    device_id=(px, py, c), device_id_type=MESH).start()
            else:
                cp = pltpu.make_async_remote_copy(
                    src_ref=src, dst_ref=outs[a].at[c, pidx], send_sem=send1.at[a, k], recv_sem=recv1.at[a, k],
                    device_id=(px, py, c), device_id_type=MESH)
                cp.wait_send()
                cp.wait_recv()


def _exchange_stage2(outs, send2, recv2, what):
    x, y, c = lax.axis_index("x"), lax.axis_index("y"), lax.axis_index("c")
    sibling = (x, y, 1 - c)
    for a in range(len(outs)):
        if what == "start":
            pltpu.make_async_remote_copy(
                src_ref=outs[a].at[c], dst_ref=outs[a].at[c], send_sem=send2.at[a], recv_sem=recv2.at[a],
                device_id=sibling, device_id_type=MESH).start()
        else:
            cp = pltpu.make_async_remote_copy(
                src_ref=outs[a].at[c], dst_ref=outs[a].at[1 - c], send_sem=send2.at[a], recv_sem=recv2.at[a],
                device_id=sibling, device_id_type=MESH)
            cp.wait_send()
            cp.wait_recv()


def _sibling_swap(arrays):
    na = len(arrays)

    def body(*refs):
        ins, theirs = refs[:na], refs[na:2 * na]
        send_sems, recv_sems = refs[2 * na:]
        x, y, c = lax.axis_index("x"), lax.axis_index("y"), lax.axis_index("c")
        sibling = (x, y, 1 - c)
        for a in range(na):
            pltpu.make_async_remote_copy(
                src_ref=ins[a].at[1 - c], dst_ref=theirs[a], send_sem=send_sems.at[a], recv_sem=recv_sems.at[a],
                device_id=sibling, device_id_type=MESH).start()
        for a in range(na):
            cp = pltpu.make_async_remote_copy(
                src_ref=ins[a].at[1 - c], dst_ref=theirs[a], send_sem=send_sems.at[a], recv_sem=recv_sems.at[a],
                device_id=sibling, device_id_type=MESH)
            cp.wait_send()
            cp.wait_recv()

    any_spec = pl.BlockSpec(memory_space=pl.ANY)
    return pl.pallas_call(
        body, name="sibling_swap",
        in_specs=[any_spec] * na, out_specs=[any_spec] * na,
        out_shape=[jax.ShapeDtypeStruct(s.shape[1:], s.dtype) for s in arrays],
        scratch_shapes=[pltpu.SemaphoreType.DMA((na,)), pltpu.SemaphoreType.DMA((na,))],
    )(*arrays)


PAIR_ADD_STEPS = 8


def _pair_add(boths, theirs):
    na = len(boths)
    core = lax.axis_index("c").astype(jnp.int32).reshape(1)
    flat_b = [b.reshape(2, b.shape[1] * b.shape[2], b.shape[3]) for b in boths]
    flat_t = [t.reshape(t.shape[0] * t.shape[1], t.shape[2]) for t in theirs]
    rows = [t.shape[0] // PAIR_ADD_STEPS for t in flat_t]

    def body(core_ref, *refs):
        for a in range(na):
            refs[2 * na + a][...] = (refs[a][...].astype(F32) + refs[na + a][...].astype(F32)).astype(
                refs[2 * na + a].dtype)

    own = [pl.BlockSpec((None, r, t.shape[1]), lambda i, core_ref: (core_ref[0], i, 0)) for r, t in zip(rows, flat_t)]
    plain = [pl.BlockSpec((r, t.shape[1]), lambda i, core_ref: (i, 0)) for r, t in zip(rows, flat_t)]
    outs = pl.pallas_call(
        body, name="pair_add",
        grid_spec=pltpu.PrefetchScalarGridSpec(
            num_scalar_prefetch=1, grid=(PAIR_ADD_STEPS,), in_specs=own + plain, out_specs=plain),
        out_shape=[jax.ShapeDtypeStruct(t.shape, t.dtype) for t in flat_t],
        compiler_params=_cparams(("parallel",)),
    )(core, *flat_b, *flat_t)
    return [o.reshape(t.shape) for o, t in zip(outs, theirs)]


def _adamw_math(g, w, m, v):
    m2 = ADAM_B1 * m + (1.0 - ADAM_B1) * g
    v2 = ADAM_B2 * v + (1.0 - ADAM_B2) * (g * g)
    m_hat = m2 / (1.0 - ADAM_B1 ** ADAM_STEP)
    v_hat = v2 / (1.0 - ADAM_B2 ** ADAM_STEP)
    delta = -ADAM_LR * (m_hat / (jnp.sqrt(v_hat) + ADAM_EPS) + ADAM_WD * w)
    return delta, m2, v2


def _adamw_big(contrib, w, m, v, layer, name, prev=None):
    _, r, c = w.shape
    rh = r // 2
    tr = _pick_tile(rh, (256, 176, 128, 64, 8))
    nj = rh // tr
    blk = pl.BlockSpec((1, tr, c), lambda h, j: (layer, h * nj + j, 0))
    n_prev = 0 if prev is None else 4

    def body(*refs):
        g_ref, w_ref, m_ref, v_ref = refs[:4]
        go_ref, d_ref, mo_ref, vo_ref = refs[4 + n_prev:]
        g = g_ref[0, 0].astype(F32)
        for s in range(1, N_CHIPS):
            g = g + g_ref[0, s].astype(F32)
        delta, m2, v2 = _adamw_math(g, w_ref[0], m_ref[0], v_ref[0])
        go_ref[0] = g
        d_ref[0] = delta
        mo_ref[0] = m2
        vo_ref[0] = v2

    any_spec = pl.BlockSpec(memory_space=pl.ANY)
    return pl.pallas_call(
        body, name=name, grid=(2, nj),
        in_specs=[pl.BlockSpec((1, N_CHIPS, tr, c), lambda h, j: (h, 0, j, 0)), blk, blk, blk] + [any_spec] * n_prev,
        out_specs=[blk] * 4, out_shape=[jax.ShapeDtypeStruct(w.shape, F32)] * 4,
        input_output_aliases={4 + a: a for a in range(n_prev)},
        compiler_params=_cparams(("parallel", "parallel")),
    )(contrib, w, m, v, *([] if prev is None else prev))


def _sum_small(contrib):
    rows = contrib.shape[2]

    def body(g_ref, o_ref):
        total = g_ref[0, 0]
        for j in range(1, N_DEV):
            total = total + g_ref[j // N_CHIPS, j % N_CHIPS]
        o_ref[...] = total

    return pl.pallas_call(
        body, name="sum_small", out_shape=jax.ShapeDtypeStruct((rows, 128), F32),
        compiler_params=_cparams(),
    )(contrib)


def _adamw_small(gs, ws, ms, vs):
    n = len(gs)
    as2d = lambda a: a.reshape(1, -1) if a.ndim == 1 else a

    def body(*refs):
        g_refs, w_refs, m_refs, v_refs = refs[:n], refs[n:2 * n], refs[2 * n:3 * n], refs[3 * n:4 * n]
        d_refs, mo_refs, vo_refs = refs[4 * n:5 * n], refs[5 * n:6 * n], refs[6 * n:]
        for j in range(n):
            delta, m2, v2 = _adamw_math(g_refs[j][...], w_refs[j][...], m_refs[j][...], v_refs[j][...])
            d_refs[j][...] = delta
            mo_refs[j][...] = m2
            vo_refs[j][...] = v2

    ins = [as2d(a) for a in (*gs, *ws, *ms, *vs)]
    outs = pl.pallas_call(
        body, name="adamw_small", out_shape=[jax.ShapeDtypeStruct(a.shape, F32) for a in ins[:n]] * 3,
        compiler_params=_cparams(),
    )(*ins)
    back = lambda group: [o.reshape(g.shape) for o, g in zip(group, gs)]
    return back(outs[:n]), back(outs[n:2 * n]), back(outs[2 * n:])


PACK_ALIGN = 8 * 128


def _packed_rows(shape):
    n = 1
    for d in shape:
        n *= d
    return (n + PACK_ALIGN - 1) // PACK_ALIGN * 8


def _pack(arrays):
    parts = []
    for a in arrays:
        flat = a.reshape(-1)
        pad = _packed_rows(a.shape) * 128 - flat.shape[0]
        if pad:
            flat = jnp.concatenate([flat, jnp.zeros((pad,), F32)])
        parts.append(flat.reshape(-1, 128))
    return jnp.concatenate(parts, axis=0)


def _unpack(packed, shapes):
    out, row = [], 0
    for s in shapes:
        rows = _packed_rows(s)
        n = 1
        for d in s:
            n *= d
        out.append(packed[row:row + rows].reshape(-1)[:n].reshape(s))
        row += rows
    return out


SMALL_NAMES = ("norm1_w", "sgu_ln_w", "sgu_ln_b", "sgu_w_spatial", "sgu_b_spatial", "sc_conv_w", "dn_conv_w",
               "dn_a_log", "dn_dt_bias", "dn_norm_w", "gla_w_gate2", "gla_gate_bias", "gla_norm_w", "norm2_w",
               "final_norm_w")
SHARDED_SMALL = ("sc_conv_w", "dn_conv_w", "gla_w_gate2")
BIG_NAMES = ("w_in", "w_out", "w_gate_up", "w_down")
WEIGHT_ORDER = ("norm1_w", "w_in", "sgu_ln_w", "sgu_ln_b", "sgu_w_spatial", "sgu_b_spatial", "sc_conv_w", "dn_conv_w",
                "dn_a_log", "dn_dt_bias", "dn_norm_w", "gla_w_gate2", "gla_gate_bias", "gla_norm_w", "w_out",
                "norm2_w", "w_gate_up", "w_down", "final_norm_w")


def _cols_from_shards(g):
    l, n, r, c = g.shape
    return jnp.transpose(g, (0, 2, 1, 3)).reshape(l, r, n * c)


def kernel(x, norm1_w, w_in, sgu_ln_w, sgu_ln_b, sgu_w_spatial, sgu_b_spatial, sc_conv_w, dn_conv_w, dn_a_log, dn_dt_bias, dn_norm_w, gla_w_gate2, gla_gate_bias, gla_norm_w, w_out, norm2_w, w_gate_up, w_down, final_norm_w, loss_target, m_norm1_w, m_w_in, m_sgu_ln_w, m_sgu_ln_b, m_sgu_w_spatial, m_sgu_b_spatial, m_sc_conv_w, m_dn_conv_w, m_dn_a_log, m_dn_dt_bias, m_dn_norm_w, m_gla_w_gate2, m_gla_gate_bias, m_gla_norm_w, m_w_out, m_norm2_w, m_w_gate_up, m_w_down, m_final_norm_w, v_norm1_w, v_w_in, v_sgu_ln_w, v_sgu_ln_b, v_sgu_w_spatial, v_sgu_b_spatial, v_sc_conv_w, v_dn_conv_w, v_dn_a_log, v_dn_dt_bias, v_dn_norm_w, v_gla_w_gate2, v_gla_gate_bias, v_gla_norm_w, v_w_out, v_norm2_w, v_w_gate_up, v_w_down, v_final_norm_w):
    w = dict(norm1_w=norm1_w, w_in=w_in, sgu_ln_w=sgu_ln_w, sgu_ln_b=sgu_ln_b, sgu_w_spatial=sgu_w_spatial,
             sgu_b_spatial=sgu_b_spatial, sc_conv_w=sc_conv_w, dn_conv_w=dn_conv_w, dn_a_log=dn_a_log,
             dn_dt_bias=dn_dt_bias, dn_norm_w=dn_norm_w, gla_w_gate2=gla_w_gate2, gla_gate_bias=gla_gate_bias,
             gla_norm_w=gla_norm_w, w_out=w_out, norm2_w=norm2_w, w_gate_up=w_gate_up, w_down=w_down,
             final_norm_w=final_norm_w)
    m = dict(norm1_w=m_norm1_w, w_in=m_w_in, sgu_ln_w=m_sgu_ln_w, sgu_ln_b=m_sgu_ln_b, sgu_w_spatial=m_sgu_w_spatial,
             sgu_b_spatial=m_sgu_b_spatial, sc_conv_w=m_sc_conv_w, dn_conv_w=m_dn_conv_w, dn_a_log=m_dn_a_log,
             dn_dt_bias=m_dn_dt_bias, dn_norm_w=m_dn_norm_w, gla_w_gate2=m_gla_w_gate2,
             gla_gate_bias=m_gla_gate_bias, gla_norm_w=m_gla_norm_w, w_out=m_w_out, norm2_w=m_norm2_w,
             w_gate_up=m_w_gate_up, w_down=m_w_down, final_norm_w=m_final_norm_w)
    v = dict(norm1_w=v_norm1_w, w_in=v_w_in, sgu_ln_w=v_sgu_ln_w, sgu_ln_b=v_sgu_ln_b, sgu_w_spatial=v_sgu_w_spatial,
             sgu_b_spatial=v_sgu_b_spatial, sc_conv_w=v_sc_conv_w, dn_conv_w=v_dn_conv_w, dn_a_log=v_dn_a_log,
             dn_dt_bias=v_dn_dt_bias, dn_norm_w=v_dn_norm_w, gla_w_gate2=v_gla_w_gate2,
             gla_gate_bias=v_gla_gate_bias, gla_norm_w=v_gla_norm_w, w_out=v_w_out, norm2_w=v_norm2_w,
             w_gate_up=v_w_gate_up, w_down=v_w_down, final_norm_w=v_final_norm_w)
    chip = 2 * lax.axis_index("x") + lax.axis_index("y")

    w_in_wire = w["w_in"].astype(MXU_DTYPE)
    row_halves = lambda a: a.reshape(2, a.shape[0] // 2, a.shape[1])

    def full_w_in(g):
        return _pad_w_in(jnp.transpose(g, (0, 2, 1, 3)).reshape(D_MODEL, IN_COLS))

    first = [row_halves(w_in_wire[0])] + [w[n] for n in SHARDED_SMALL]
    gathered = _chip_exchange(first, ["layer"] * len(first), "gather_first")
    w_in_0 = full_w_in(gathered[0])
    big = dict(w_in=[w_in_0, None])
    small = {n: w[n] for n in SMALL_NAMES if n not in SHARDED_SMALL}
    for j, n in enumerate(SHARDED_SMALL):
        small[n] = _cols_from_shards(gathered[1 + j])
    wire = {n: w[n].astype(MXU_DTYPE) for n in ("w_out", "w_gate_up", "w_down")}

    def cols_full(g):
        _, n, rh, c = g.shape
        return jnp.transpose(g, (0, 2, 1, 3)).reshape(2 * rh, n * c)

    def rows_full(g):
        _, n, rh, c = g.shape
        return jnp.transpose(g, (1, 0, 2, 3)).reshape(n * 2 * rh, c)

    def finish_0(big, g):
        return dict(w_in=[w_in_0, full_w_in(g[0])], w_out=[rows_full(g[1]), None],
                    w_gate_up=[cols_full(g[2]), None], w_down=[rows_full(g[3]), None])

    def finish_1(big, g):
        return dict(big, w_out=[big["w_out"][0], rows_full(g[0])], w_gate_up=[big["w_gate_up"][0], cols_full(g[1])],
                    w_down=[big["w_down"][0], rows_full(g[2])])

    late = [([row_halves(w_in_wire[1])] + [row_halves(wire[n][0]) for n in ("w_out", "w_gate_up", "w_down")], finish_0),
            ([row_halves(wire[n][1]) for n in ("w_out", "w_gate_up", "w_down")], finish_1)]

    small_shapes = [(DEPTH,) + w[n].shape[1:-1] + (w[n].shape[-1] * (N_CHIPS if n in SHARDED_SMALL else 1),)
                    if n != "final_norm_w" else w[n].shape for n in SMALL_NAMES] + [(1,)]

    def pack_small(gsmall, loss_tile):
        return _pack([gsmall[n] for n in SMALL_NAMES] + [loss_tile[0:1, 0]])

    _, grad_x, contribs, small_contrib = _local_step(
        x[0], loss_target[0], big, small, late_weights=late, exchange=True, small_extra=pack_small)

    out_g, out_d, out_m, out_v = {}, {}, {}, {}
    for j, n in enumerate(BIG_NAMES):
        outs = _adamw_big(contribs[(n, 1)], w[n], m[n], v[n], 1, "adamw_" + n + "_1")
        out_g[n], out_d[n], out_m[n], out_v[n] = _adamw_big(contribs[(n, 0)], w[n], m[n], v[n], 0,
                                                            "adamw_" + n + "_0", prev=outs)
    summed = _unpack(_sum_small(small_contrib), small_shapes)
    loss = summed[-1][0]
    for n, g in zip(SMALL_NAMES, summed[:-1]):
        if n in SHARDED_SMALL:
            cols = g.shape[-1] // N_CHIPS
            g = lax.dynamic_slice_in_dim(g, chip * cols, cols, axis=g.ndim - 1)
        out_g[n] = g
    d_s, m_s, v_s = _adamw_small([out_g[n] for n in SMALL_NAMES], [w[n] for n in SMALL_NAMES],
                                 [m[n] for n in SMALL_NAMES], [v[n] for n in SMALL_NAMES])
    for n, d_, m_, v_ in zip(SMALL_NAMES, d_s, m_s, v_s):
        out_d[n], out_m[n], out_v[n] = d_, m_, v_

    return (loss, grad_x[None], *[out_g[n] for n in WEIGHT_ORDER], *[out_d[n] for n in WEIGHT_ORDER],
            *[out_m[n] for n in WEIGHT_ORDER], *[out_v[n] for n in WEIGHT_ORDER])
```

```python
import functools

import jax
import jax.numpy as jnp
from jax import lax
from jax.experimental import pallas as pl
from jax.experimental.pallas import tpu as pltpu

F32 = jnp.float32
BF16 = jnp.bfloat16
MXU_DTYPE = jnp.bfloat16
GRAD_WIRE_DTYPE = jnp.bfloat16
HI = lax.Precision.HIGHEST
MESH = pl.DeviceIdType.MESH

D_MODEL = 1024
DEPTH = 2
GROUP = 256
HEADS = 4
HEAD_DIM = 64
SGU_CHUNK = 128
SCAN_CHUNK = 64
D_FF = 2816
EPS = 1e-6
IN_COLS = 3352
P_COLS = 3584
HALO = 8
N_CHIPS = 4
N_DEV = 8
VMEM_LIMIT = 56 * 1024 * 1024

ADAM_LR = 0.001
ADAM_B1 = 0.9
ADAM_B2 = 0.999
ADAM_EPS = 1e-08
ADAM_WD = 0.01
ADAM_STEP = 10

(COL_AU, COL_AV, COL_BB, COL_BC, COL_BH, COL_CQ, COL_CK, COL_CV, COL_CZ,
 COL_DQ, COL_DK, COL_DV, COL_DZ) = range(13)
COL128_SMALL_C = 26
COL128_SMALL_D = 27


def _cparams(sem=None):
    return pltpu.CompilerParams(dimension_semantics=sem, vmem_limit_bytes=VMEM_LIMIT)


def _iota(shape, dim):
    return lax.broadcasted_iota(jnp.int32, shape, dim)


def _dg(a, b, ca, cb, prec=None):
    return lax.dot_general(a, b, (((ca,), (cb,)), ((), ())), preferred_element_type=F32, precision=prec)


@functools.partial(jax.custom_vjp, nondiff_argnums=(2, 3))
def bdot(a, b, ca, cb):
    return _dg(a.astype(MXU_DTYPE), b.astype(MXU_DTYPE), ca, cb)


def _bdot_fwd(a, b, ca, cb):
    return bdot(a, b, ca, cb), (a, b)


def _bdot_bwd(ca, cb, res, g):
    a, b = res
    if ca == 1:
        da = bdot(g, b, 1, 1 if cb == 0 else 0)
    else:
        da = bdot(b, g, 1 if cb == 0 else 0, 1)
    if cb == 0:
        db = bdot(a, g, 0, 0) if ca == 1 else bdot(a, g, 1, 0)
    else:
        db = bdot(g, a, 0, 0) if ca == 1 else bdot(g, a, 0, 1)
    return da, db


bdot.defvjp(_bdot_fwd, _bdot_bwd)


def _pieces(a, n):
    out, r = [], a
    for i in range(n):
        p = r.astype(MXU_DTYPE)
        out.append(p)
        if i + 1 < n:
            r = r - p.astype(F32)
    return out


def _mdot_impl(a, b, ca, cb, sa, sb):
    total = None
    for i, x in enumerate(_pieces(a, sa)):
        for j, y in enumerate(_pieces(b, sb)):
            if i + j < max(sa, sb):
                t = _dg(x, y, ca, cb)
                total = t if total is None else total + t
    return total


@functools.partial(jax.custom_vjp, nondiff_argnums=(2, 3, 4, 5))
def mdot(a, b, ca, cb, sa, sb):
    return _mdot_impl(a, b, ca, cb, sa, sb)


def _mdot_fwd(a, b, ca, cb, sa, sb):
    return _mdot_impl(a, b, ca, cb, sa, sb), (a, b)


def _mdot_bwd(ca, cb, sa, sb, res, g):
    a, b = res
    ga, gb = (3 if sb == 1 else 2), (3 if sa == 1 else 2)
    if sa == 1:
        da = jnp.zeros_like(a)
    elif ca == 1:
        da = mdot(g, b, 1, 1 if cb == 0 else 0, ga, sb)
    else:
        da = mdot(b, g, 1 if cb == 0 else 0, 1, sb, ga)
    if sb == 1:
        db = jnp.zeros_like(b)
    elif cb == 0:
        db = mdot(a, g, 0, 0, sa, gb) if ca == 1 else mdot(a, g, 1, 0, sa, gb)
    else:
        db = mdot(g, a, 0, 0, gb, sa) if ca == 1 else mdot(g, a, 0, 1, gb, sa)
    return da, db


mdot.defvjp(_mdot_fwd, _mdot_bwd)


def mask_r(a, m, ca=1, cb=0):
    return mdot(a, m, ca, cb, 3, 1)


def mask_l(m, b, ca=1, cb=0):
    return mdot(m, b, ca, cb, 1, 3)


def ddot(a, b, ca=1, cb=0):
    return mdot(a, b, ca, cb, 2, 2)


def _head_mask(h):
    return ((_iota((1, GROUP), 1) >> 6) == h).astype(F32)


def _block_diag_mask():
    return ((_iota((GROUP, GROUP), 0) >> 6) == (_iota((GROUP, GROUP), 1) >> 6)).astype(F32)


def _expand_mat(offset):
    return ((_iota((128, GROUP), 0) - offset) == (_iota((128, GROUP), 1) >> 6)).astype(F32)


def _tril(n, strict=False):
    r, c = _iota((n, n), 0), _iota((n, n), 1)
    return (r > c) if strict else (r >= c)


def _row_pick(x, row):
    return jnp.sum(jnp.where(_iota(x.shape, 0) == row, x, 0.0), axis=0, keepdims=True)


def _shift_rows_impl(x, halo, j):
    n = x.shape[0]
    r = _iota(x.shape, 0)
    top = jnp.concatenate([pltpu.roll(halo, j, 0), jnp.zeros((n - HALO, x.shape[1]), x.dtype)], axis=0)
    return jnp.where(r >= j, pltpu.roll(x, j, 0), top)


def _mxu_round(a):
    return a.astype(MXU_DTYPE).astype(F32)


@functools.partial(jax.custom_vjp, nondiff_argnums=(3,))
def _causal_conv(x, halo, w, width):
    xb, hb, wb = _mxu_round(x), _mxu_round(halo), _mxu_round(w)
    out = xb * _row_pick(wb, width - 1)
    for j in range(1, width):
        out = out + _shift_rows_impl(xb, hb, j) * _row_pick(wb, width - 1 - j)
    return out


def _causal_conv_fwd(x, halo, w, width):
    return _causal_conv(x, halo, w, width), (x, halo, w)


def _causal_conv_bwd(width, res, g):
    x, halo, w = res
    xb, hb, wb, gb = _mxu_round(x), _mxu_round(halo), _mxu_round(w), _mxu_round(g)
    n = g.shape[0]
    rows, rows8 = _iota(g.shape, 0), _iota(halo.shape, 0)
    dx = gb * _row_pick(wb, width - 1)
    dh = jnp.zeros_like(halo)
    dw = jnp.where(rows8 == width - 1, jnp.sum(xb * gb, axis=0, keepdims=True), 0.0)
    for j in range(1, width):
        gj = gb * _row_pick(wb, width - 1 - j)
        dx = dx + jnp.where(rows < n - j, pltpu.roll(gj, n - j, 0), 0.0)
        dh = dh + jnp.where(rows8 >= HALO - j, pltpu.roll(gj[0:HALO], HALO - j, 0), 0.0)
        tap = jnp.sum(_shift_rows_impl(xb, hb, j) * gb, axis=0, keepdims=True)
        dw = dw + jnp.where(rows8 == width - 1 - j, tap, 0.0)
    return dx, dh, dw


_causal_conv.defvjp(_causal_conv_fwd, _causal_conv_bwd)


def _head_sum(x, bd):
    return mask_r(x, bd)


def _softplus(x):
    return jnp.maximum(x, 0.0) + jnp.log1p(jnp.exp(-jnp.abs(x)))


def _log_sigmoid(x):
    return -_softplus(-x)


def _silu(x):
    return x * jax.nn.sigmoid(x)


def _head_rmsnorm_gate(o, nw, z, bd):
    ms = _head_sum(o * o, bd) * (1.0 / HEAD_DIM)
    return o * lax.rsqrt(ms + EPS) * nw * _silu(z)


def _sgu_chunk(pu, pv, ln_w, ln_b, ws0, ws1, ws2, ws3, bs_t):
    u = jax.nn.gelu(pu)
    g = jax.nn.gelu(pv)
    mu = jnp.mean(g, axis=-1, keepdims=True)
    var = jnp.mean(jnp.square(g - mu), axis=-1, keepdims=True)
    v = (g - mu) * lax.rsqrt(var + EPS) * ln_w + ln_b
    keep = _tril(SGU_CHUNK)
    bias = mask_r(bs_t, _expand_mat(0))
    causal_ws = [jnp.where(keep, ws, 0.0) for ws in (ws0, ws1, ws2, ws3)]
    mixed = []
    for ci in range(pu.shape[0] // SGU_CHUNK):
        v_c = v[ci * SGU_CHUNK:(ci + 1) * SGU_CHUNK]
        m_c = bias
        for h in range(HEADS):
            m_c = m_c + _head_mask(h) * bdot(causal_ws[h], v_c, 1, 0)
        mixed.append(m_c)
    return u * jnp.concatenate(mixed, axis=0)


def _sc_chunk(pb, pc, ph, halo_c, halo_h, cw):
    return pb * _causal_conv(pc * ph, halo_c * halo_h, cw, 3)


def _neumann_inverses(lows):
    n = lows[0].shape[0]
    eye = (_iota((n, n), 0) == _iota((n, n), 1)).astype(F32)
    a = [-low for low in lows]
    t = [eye + x for x in a]
    for _ in range(5):
        a = [ddot(x, x) for x in a]
        t = [ti + ddot(ti, ai) for ti, ai in zip(t, a)]
    return t


@jax.custom_vjp
def _saved_inverse(low, inv):
    return inv


def _saved_inverse_fwd(low, inv):
    return inv, inv


def _saved_inverse_bwd(inv, g):
    return -ddot(ddot(inv, g, 0, 0), inv, 1, 1), jnp.zeros_like(inv)


_saved_inverse.defvjp(_saved_inverse_fwd, _saved_inverse_bwd)


def _chunk_tril(rows):
    r, c = _iota((rows, rows), 0), _iota((rows, rows), 1)
    return ((r >> 6) == (c >> 6)) & (r >= c)


def _dn_block(pq, pk, pv, hq, hk, hv, small, pz, cwq, cwk, cwv, a_log, dt_bias, nw, state, saved_inv=None):
    c = SCAN_CHUNK
    rows = pq.shape[0]
    bd = _block_diag_mask()
    q = _silu(_causal_conv(pq, hq, cwq, 4))
    k = _silu(_causal_conv(pk, hk, cwk, 4))
    v = _silu(_causal_conv(pv, hv, cwv, 4))
    q = q * lax.rsqrt(_head_sum(q * q, bd) + EPS) * (HEAD_DIM ** -0.5)
    k = k * lax.rsqrt(_head_sum(k * k, bd) + EPS)
    lane = _iota((1, 128), 1)
    g = jnp.where(lane < HEADS, -jnp.exp(a_log) * _softplus(small + dt_bias), 0.0)
    beta_b = mask_r(jax.nn.sigmoid(small), _expand_mat(HEADS))
    gc_all = mask_l(_chunk_tril(rows).astype(F32), g)
    gcb_all = mask_r(gc_all, _expand_mat(0))
    kb_all = k * beta_b
    vb_all = v * beta_b
    kbe_all = kb_all * jnp.exp(gcb_all)
    qg_all = q * jnp.exp(gcb_all)
    causal, strict = _tril(c), _tril(c, strict=True)
    nc = rows // c
    pairs = [(ci, h) for ci in range(nc) for h in range(HEADS)]
    sls = [slice(ci * c, (ci + 1) * c) for ci in range(nc)]
    decays, lows, attns = [], [], []
    for ci, h in pairs:
        gc = gc_all[sls[ci]]
        onehot = (_iota((c, 128), 1) == h).astype(F32)
        col = mask_l(onehot, gc, 1, 1)
        row = jnp.sum(gc * onehot, axis=1, keepdims=True)
        decays.append(jnp.exp(jnp.where(causal, row - col, -jnp.inf)))
    for j, (ci, h) in enumerate(pairs):
        mh = _head_mask(h)
        k_c = k[sls[ci]]
        lows.append(jnp.where(strict, bdot(kb_all[sls[ci]] * mh, k_c, 1, 1) * decays[j], 0.0))
        attns.append(bdot(q[sls[ci]] * mh, k_c, 1, 1) * decays[j])
    if saved_inv is None:
        invs = _neumann_inverses(lows)
    else:
        invs = [_saved_inverse(low, s) for low, s in zip(lows, saved_inv)]
    us, ws = [], []
    for ci in range(nc):
        u = jnp.zeros((c, GROUP), F32)
        w = jnp.zeros((c, GROUP), F32)
        for h in range(HEADS):
            mh = _head_mask(h)
            u = u + mh * ddot(invs[ci * HEADS + h], vb_all[sls[ci]])
            w = w + mh * ddot(invs[ci * HEADS + h], kbe_all[sls[ci]])
        us.append(u)
        ws.append(w)
    outs = []
    for ci in range(nc):
        gc_b = gcb_all[sls[ci]]
        gc_last_b = _row_pick(gc_b, c - 1)
        v_new = us[ci] - bdot(ws[ci], state, 1, 0)
        o = bdot(qg_all[sls[ci]], state, 1, 0)
        for h in range(HEADS):
            o = o + _head_mask(h) * bdot(attns[ci * HEADS + h], v_new, 1, 0)
        k_dec = k[sls[ci]] * jnp.exp(gc_last_b - gc_b)
        state = state * jnp.exp(gc_last_b) + bd * bdot(k_dec, v_new, 0, 0)
        outs.append(o)
    o = jnp.concatenate(outs, axis=0)
    return _head_rmsnorm_gate(o, nw, pz, bd), state, invs


def _gla_chunk(pq, pk, pv, small, pz, w2, gbias, nw, state_t):
    c = SCAN_CHUNK
    rows = pq.shape[0]
    nc = rows // c
    sls = [slice(ci * c, (ci + 1) * c) for ci in range(nc)]
    bd = _block_diag_mask()
    log_a = _log_sigmoid(bdot(small, w2, 1, 0) + gbias) * (1.0 / 16.0)
    gcum = mask_l(_chunk_tril(rows).astype(F32), log_a)
    r, s = _iota((rows, rows), 0), _iota((rows, rows), 1)
    base = (r >> 6) << 6
    g_mid = mask_l((s == base + c // 2).astype(F32), gcum)
    g_last = mask_l((s == base + c - 1).astype(F32), gcum)
    q = pq * (HEAD_DIM ** -0.5)
    qa = q * jnp.exp(gcum - g_mid)
    ka = pk * jnp.exp(g_mid - gcum)
    qg = q * jnp.exp(gcum)
    k_last = pk * jnp.exp(g_last - gcum)
    causal = _tril(c)
    attns = [jnp.where(causal, bdot(qa[sls[ci]] * _head_mask(h), ka[sls[ci]], 1, 1), 0.0)
             for ci in range(nc) for h in range(HEADS)]
    intra = []
    for ci in range(nc):
        o = jnp.zeros((c, GROUP), F32)
        for h in range(HEADS):
            o = o + _head_mask(h) * bdot(attns[ci * HEADS + h], pv[sls[ci]], 1, 0)
        intra.append(o)
    kvs = [bd * bdot(pv[sls[ci]], k_last[sls[ci]], 0, 0) for ci in range(nc)]
    states = []
    for ci in range(nc):
        states.append(state_t)
        state_t = state_t * jnp.exp(_row_pick(g_last[sls[ci]], 0)) + kvs[ci]
    outs = [intra[ci] + bdot(qg[sls[ci]], states[ci], 1, 1) for ci in range(nc)]
    o = jnp.concatenate(outs, axis=0)
    return _head_rmsnorm_gate(o, nw, pz, bd), state_t


def _col_spec(rows, group, rev_n=None):
    if rev_n is None:
        return pl.BlockSpec((rows, GROUP), lambda i: (i, group))
    return pl.BlockSpec((rows, GROUP), lambda i: (rev_n - 1 - i, group))


def _small_spec(rows, group128, rev_n=None):
    if rev_n is None:
        return pl.BlockSpec((rows, 128), lambda i: (i, group128))
    return pl.BlockSpec((rows, 128), lambda i: (rev_n - 1 - i, group128))


def _halo_spec(rows, group, rev_n=None):
    per = rows // HALO
    if rev_n is None:
        return pl.BlockSpec((HALO, GROUP), lambda i: (jnp.maximum(i * per - 1, 0), group))
    return pl.BlockSpec((HALO, GROUP), lambda i: (jnp.maximum((rev_n - 1 - i) * per - 1, 0), group))


def _full_spec(shape):
    nd = len(shape)
    return pl.BlockSpec(shape, lambda i: (0,) * nd)


def _out_rows_spec(rows, lanes, rev_n=None):
    if rev_n is None:
        return pl.BlockSpec((rows, lanes), lambda i: (i, 0))
    return pl.BlockSpec((rows, lanes), lambda i: (rev_n - 1 - i, 0))


SGU_ROWS = 4 * SGU_CHUNK


def _sgu_fwd(p, ln_w, ln_b, ws, bs_t):
    t = p.shape[0]
    n = t // SGU_ROWS

    def body(pu_ref, pv_ref, lw_ref, lb_ref, ws_ref, bs_ref, y_ref):
        y = _sgu_chunk(pu_ref[...], pv_ref[...], lw_ref[...], lb_ref[...],
                       ws_ref[0], ws_ref[1], ws_ref[2], ws_ref[3], bs_ref[...])
        y_ref[...] = y.astype(y_ref.dtype)

    return pl.pallas_call(
        body, name="sgu_fwd", grid=(n,),
        in_specs=[_col_spec(SGU_ROWS, COL_AU), _col_spec(SGU_ROWS, COL_AV), _full_spec((1, GROUP)),
                  _full_spec((1, GROUP)), _full_spec((HEADS, SGU_CHUNK, SGU_CHUNK)), _full_spec((SGU_CHUNK, 128))],
        out_specs=_out_rows_spec(SGU_ROWS, GROUP),
        out_shape=jax.ShapeDtypeStruct((t, GROUP), BF16),
        compiler_params=_cparams(("arbitrary",)),
    )(p, p, ln_w, ln_b, ws, bs_t)


def _sgu_bwd(p, dmix, ln_w, ln_b, ws, bs_t):
    t = p.shape[0]
    n = t // SGU_ROWS

    def body(pu_ref, pv_ref, dy_ref, lw_ref, lb_ref, ws_ref, bs_ref,
             dpu_ref, dpv_ref, dlw_ref, dlb_ref, dws_ref, dbs_ref):
        args = (pu_ref[...], pv_ref[...], lw_ref[...], lb_ref[...],
                ws_ref[0], ws_ref[1], ws_ref[2], ws_ref[3], bs_ref[...])
        _, vjp = jax.vjp(_sgu_chunk, *args)
        dpu, dpv, dlw, dlb, d0, d1, d2, d3, dbs = vjp(dy_ref[...])
        dpu_ref[...] = dpu.astype(dpu_ref.dtype)
        dpv_ref[...] = dpv.astype(dpv_ref.dtype)

        @pl.when(pl.program_id(0) == 0)
        def _():
            dlw_ref[...] = jnp.zeros_like(dlw_ref)
            dlb_ref[...] = jnp.zeros_like(dlb_ref)
            dws_ref[...] = jnp.zeros_like(dws_ref)
            dbs_ref[...] = jnp.zeros_like(dbs_ref)

        dlw_ref[...] += dlw
        dlb_ref[...] += dlb
        for h, d in enumerate((d0, d1, d2, d3)):
            dws_ref[h] += d
        dbs_ref[...] += dbs

    return pl.pallas_call(
        body, name="sgu_bwd", grid=(n,),
        in_specs=[_col_spec(SGU_ROWS, COL_AU), _col_spec(SGU_ROWS, COL_AV),
                  pl.BlockSpec((SGU_ROWS, GROUP), lambda i: (i, 0)),
                  _full_spec((1, GROUP)), _full_spec((1, GROUP)), _full_spec((HEADS, SGU_CHUNK, SGU_CHUNK)),
                  _full_spec((SGU_CHUNK, 128))],
        out_specs=[_out_rows_spec(SGU_ROWS, GROUP), _out_rows_spec(SGU_ROWS, GROUP), _full_spec((1, GROUP)),
                   _full_spec((1, GROUP)), _full_spec((HEADS, SGU_CHUNK, SGU_CHUNK)), _full_spec((SGU_CHUNK, 128))],
        out_shape=[jax.ShapeDtypeStruct((t, GROUP), BF16), jax.ShapeDtypeStruct((t, GROUP), BF16),
                   jax.ShapeDtypeStruct((1, GROUP), F32), jax.ShapeDtypeStruct((1, GROUP), F32),
                   jax.ShapeDtypeStruct((HEADS, SGU_CHUNK, SGU_CHUNK), F32),
                   jax.ShapeDtypeStruct((SGU_CHUNK, 128), F32)],
        compiler_params=_cparams(("arbitrary",)),
    )(p, p, dmix, ln_w, ln_b, ws, bs_t)


SC_ROWS = 512


def _first_block_zero(halo, first):
    return jnp.where(first, 0.0, halo)


def _sc_fwd(p, cw):
    t = p.shape[0]
    n = t // SC_ROWS

    def body(pb_ref, pc_ref, ph_ref, hc_ref, hh_ref, cw_ref, y_ref):
        first = pl.program_id(0) == 0
        y = _sc_chunk(pb_ref[...], pc_ref[...], ph_ref[...], _first_block_zero(hc_ref[...], first),
                      _first_block_zero(hh_ref[...], first), cw_ref[...])
        y_ref[...] = y.astype(y_ref.dtype)

    return pl.pallas_call(
        body, name="sc_fwd", grid=(n,),
        in_specs=[_col_spec(SC_ROWS, COL_BB), _col_spec(SC_ROWS, COL_BC), _col_spec(SC_ROWS, COL_BH),
                  _halo_spec(SC_ROWS, COL_BC), _halo_spec(SC_ROWS, COL_BH), _full_spec((HALO, GROUP))],
        out_specs=_out_rows_spec(SC_ROWS, GROUP),
        out_shape=jax.ShapeDtypeStruct((t, GROUP), BF16),
        compiler_params=_cparams(("arbitrary",)),
    )(p, p, p, p, p, cw)


def _add_halo_grad(d, carry):
    return d + jnp.concatenate([jnp.zeros((d.shape[0] - HALO, d.shape[1]), d.dtype), carry], axis=0)


def _sc_bwd(p, dmix, cw):
    t = p.shape[0]
    n = t // SC_ROWS

    def body(pb_ref, pc_ref, ph_ref, hc_ref, hh_ref, dy_ref, cw_ref,
             dpb_ref, dpc_ref, dph_ref, dcw_ref, carry_c, carry_h):
        i = pl.program_id(0)
        first = i == n - 1

        @pl.when(i == 0)
        def _():
            carry_c[...] = jnp.zeros_like(carry_c)
            carry_h[...] = jnp.zeros_like(carry_h)
            dcw_ref[...] = jnp.zeros_like(dcw_ref)

        args = (pb_ref[...], pc_ref[...], ph_ref[...], _first_block_zero(hc_ref[...], first),
                _first_block_zero(hh_ref[...], first), cw_ref[...])
        _, vjp = jax.vjp(_sc_chunk, *args)
        dpb, dpc, dph, dhc, dhh, dcw = vjp(dy_ref[...])
        dpb_ref[...] = dpb.astype(dpb_ref.dtype)
        dpc_ref[...] = _add_halo_grad(dpc, carry_c[...]).astype(dpc_ref.dtype)
        dph_ref[...] = _add_halo_grad(dph, carry_h[...]).astype(dph_ref.dtype)
        carry_c[...] = dhc
        carry_h[...] = dhh
        dcw_ref[...] += dcw

    return pl.pallas_call(
        body, name="sc_bwd", grid=(n,),
        in_specs=[_col_spec(SC_ROWS, COL_BB, n), _col_spec(SC_ROWS, COL_BC, n), _col_spec(SC_ROWS, COL_BH, n),
                  _halo_spec(SC_ROWS, COL_BC, n), _halo_spec(SC_ROWS, COL_BH, n),
                  pl.BlockSpec((SC_ROWS, GROUP), lambda i: (n - 1 - i, 1)), _full_spec((HALO, GROUP))],
        out_specs=[_out_rows_spec(SC_ROWS, GROUP, n)] * 3 + [_full_spec((HALO, GROUP))],
        out_shape=[jax.ShapeDtypeStruct((t, GROUP), BF16)] * 3 + [jax.ShapeDtypeStruct((HALO, GROUP), F32)],
        scratch_shapes=[pltpu.VMEM((HALO, GROUP), F32), pltpu.VMEM((HALO, GROUP), F32)],
        compiler_params=_cparams(("arbitrary",)),
    )(p, p, p, p, p, dmix, cw)


SCAN_STEP_CHUNKS = 4
SCAN_ROWS = SCAN_STEP_CHUNKS * SCAN_CHUNK


def _host_call(body, hosted, *, name, grid, in_specs, out_specs, out_shape, scratch_shapes, args):
    params = _cparams(("arbitrary",))
    if hosted is None:
        outs = pl.pallas_call(body, name=name, grid=grid, in_specs=in_specs, out_specs=out_specs,
                              out_shape=out_shape, scratch_shapes=scratch_shapes, compiler_params=params)(*args)
        return outs, None
    stage, arrays, modes, bufs = hosted
    arrays = list(arrays) if stage == 1 else []
    n_in, n_out, n_scr, n_src, na = len(in_specs), len(out_specs), len(scratch_shapes), len(arrays), len(bufs)
    last = grid[0] - 1

    def new_body(*refs):
        srcs = refs[n_in:n_in + n_src]
        o0 = n_in + n_src + na
        ex = refs[o0 + n_out:o0 + n_out + na]
        s0 = o0 + n_out + na
        sems = refs[s0 + n_scr:]
        i = pl.program_id(0)

        def run(what):
            if stage == 1:
                _exchange_stage1(srcs, ex, modes, sems[0], sems[1], what)
            else:
                _exchange_stage2(ex, sems[0], sems[1], what)

        @pl.when(i == 0)
        def _():
            run("start")

        body(*refs[:n_in], *refs[o0:o0 + n_out], *refs[s0:s0 + n_scr])

        @pl.when(i == last)
        def _():
            run("wait")

    any_spec = pl.BlockSpec(memory_space=pl.ANY)
    outs = pl.pallas_call(
        new_body, name=name, grid=grid,
        in_specs=list(in_specs) + [any_spec] * (n_src + na), out_specs=list(out_specs) + [any_spec] * na,
        out_shape=list(out_shape) + [jax.ShapeDtypeStruct(b.shape, b.dtype) for b in bufs],
        input_output_aliases={n_in + n_src + a: n_out + a for a in range(na)},
        scratch_shapes=list(scratch_shapes) + (_stage1_sems(na) if stage == 1 else _stage2_sems(na)),
        compiler_params=params,
    )(*args, *arrays, *bufs)
    return outs[:n_out], outs[n_out:]


def _dn_fwd(p, cw3, a_log, dt_bias, nw, hosted=None):
    t = p.shape[0]
    r = SCAN_ROWS
    n = t // r

    def body(pq_ref, pk_ref, pv_ref, hq_ref, hk_ref, hv_ref, sm_ref, pz_ref, cw_ref, al_ref, dt_ref, nw_ref,
             y_ref, ck_ref, inv_ref, state):
        first = pl.program_id(0) == 0

        @pl.when(first)
        def _():
            state[...] = jnp.zeros_like(state)

        s_in = state[...]
        ck_ref[0] = s_in
        y, s_out, invs = _dn_block(pq_ref[...], pk_ref[...], pv_ref[...], _first_block_zero(hq_ref[...], first),
                                   _first_block_zero(hk_ref[...], first), _first_block_zero(hv_ref[...], first),
                                   sm_ref[...], pz_ref[...], cw_ref[0], cw_ref[1], cw_ref[2],
                                   al_ref[...], dt_ref[...], nw_ref[...], s_in)
        y_ref[...] = y.astype(y_ref.dtype)
        state[...] = s_out
        for j, inv in enumerate(invs):
            inv_ref[j] = inv

    nh = SCAN_STEP_CHUNKS * HEADS
    return _host_call(
        body, hosted, name="dn_fwd", grid=(n,),
        in_specs=[_col_spec(r, COL_CQ), _col_spec(r, COL_CK), _col_spec(r, COL_CV),
                  _halo_spec(r, COL_CQ), _halo_spec(r, COL_CK), _halo_spec(r, COL_CV),
                  _small_spec(r, COL128_SMALL_C), _col_spec(r, COL_CZ), _full_spec((3, HALO, GROUP)),
                  _full_spec((1, 128)), _full_spec((1, 128)), _full_spec((1, GROUP))],
        out_specs=[_out_rows_spec(r, GROUP), pl.BlockSpec((1, GROUP, GROUP), lambda i: (i, 0, 0)),
                   pl.BlockSpec((nh, SCAN_CHUNK, SCAN_CHUNK), lambda i: (i, 0, 0))],
        out_shape=[jax.ShapeDtypeStruct((t, GROUP), BF16), jax.ShapeDtypeStruct((n, GROUP, GROUP), F32),
                   jax.ShapeDtypeStruct((n * nh, SCAN_CHUNK, SCAN_CHUNK), F32)],
        scratch_shapes=[pltpu.VMEM((GROUP, GROUP), F32)],
        args=(p, p, p, p, p, p, p, p, cw3, a_log, dt_bias, nw))


def _dn_bwd(p, dmix, states, invs, cw3, a_log, dt_bias, nw, hosted=None):
    t = p.shape[0]
    c = SCAN_ROWS
    n = t // c
    nh = SCAN_STEP_CHUNKS * HEADS

    def body(pq_ref, pk_ref, pv_ref, hq_ref, hk_ref, hv_ref, sm_ref, pz_ref, dy_ref, ck_ref, inv_ref,
             cw_ref, al_ref, dt_ref, nw_ref,
             dpq_ref, dpk_ref, dpv_ref, dsm_ref, dpz_ref, dcw_ref, dal_ref, ddt_ref, dnw_ref,
             dstate, carry):
        i = pl.program_id(0)
        first = i == n - 1

        @pl.when(i == 0)
        def _():
            dstate[...] = jnp.zeros_like(dstate)
            carry[...] = jnp.zeros_like(carry)
            dcw_ref[...] = jnp.zeros_like(dcw_ref)
            dal_ref[...] = jnp.zeros_like(dal_ref)
            ddt_ref[...] = jnp.zeros_like(ddt_ref)
            dnw_ref[...] = jnp.zeros_like(dnw_ref)

        args = (pq_ref[...], pk_ref[...], pv_ref[...], _first_block_zero(hq_ref[...], first),
                _first_block_zero(hk_ref[...], first), _first_block_zero(hv_ref[...], first),
                sm_ref[...], pz_ref[...], cw_ref[0], cw_ref[1], cw_ref[2],
                al_ref[...], dt_ref[...], nw_ref[...], ck_ref[0])
        saved = [inv_ref[j] for j in range(nh)]
        _, vjp = jax.vjp(lambda *a: _dn_block(*a, saved_inv=saved)[:2], *args)
        (dpq, dpk, dpv, dhq, dhk, dhv, dsm, dpz, dcq, dck, dcv, dal, ddt, dnw, dst) = vjp(
            (dy_ref[...], dstate[...]))
        dpq_ref[...] = _add_halo_grad(dpq, carry[0]).astype(dpq_ref.dtype)
        dpk_ref[...] = _add_halo_grad(dpk, carry[1]).astype(dpk_ref.dtype)
        dpv_ref[...] = _add_halo_grad(dpv, carry[2]).astype(dpv_ref.dtype)
        dsm_ref[...] = dsm.astype(dsm_ref.dtype)
        dpz_ref[...] = dpz.astype(dpz_ref.dtype)
        carry[0] = dhq
        carry[1] = dhk
        carry[2] = dhv
        dstate[...] = dst
        dcw_ref[0] += dcq
        dcw_ref[1] += dck
        dcw_ref[2] += dcv
        dal_ref[...] += dal
        ddt_ref[...] += ddt
        dnw_ref[...] += dnw

    return _host_call(
        body, hosted, name="dn_bwd", grid=(n,),
        in_specs=[_col_spec(c, COL_CQ, n), _col_spec(c, COL_CK, n), _col_spec(c, COL_CV, n),
                  _halo_spec(c, COL_CQ, n), _halo_spec(c, COL_CK, n), _halo_spec(c, COL_CV, n),
                  _small_spec(c, COL128_SMALL_C, n), _col_spec(c, COL_CZ, n),
                  pl.BlockSpec((c, GROUP), lambda i: (n - 1 - i, 2)),
                  pl.BlockSpec((1, GROUP, GROUP), lambda i: (n - 1 - i, 0, 0)),
                  pl.BlockSpec((nh, SCAN_CHUNK, SCAN_CHUNK), lambda i: (n - 1 - i, 0, 0)),
                  _full_spec((3, HALO, GROUP)), _full_spec((1, 128)), _full_spec((1, 128)), _full_spec((1, GROUP))],
        out_specs=[_out_rows_spec(c, GROUP, n)] * 3 + [_out_rows_spec(c, 128, n), _out_rows_spec(c, GROUP, n),
                   _full_spec((3, HALO, GROUP)), _full_spec((1, 128)), _full_spec((1, 128)), _full_spec((1, GROUP))],
        out_shape=[jax.ShapeDtypeStruct((t, GROUP), BF16)] * 3 + [
            jax.ShapeDtypeStruct((t, 128), BF16), jax.ShapeDtypeStruct((t, GROUP), BF16),
            jax.ShapeDtypeStruct((3, HALO, GROUP), F32), jax.ShapeDtypeStruct((1, 128), F32),
            jax.ShapeDtypeStruct((1, 128), F32), jax.ShapeDtypeStruct((1, GROUP), F32)],
        scratch_shapes=[pltpu.VMEM((GROUP, GROUP), F32), pltpu.VMEM((3, HALO, GROUP), F32)],
        args=(p, p, p, p, p, p, p, p, dmix, states, invs, cw3, a_log, dt_bias, nw))


def _gla_fwd(p, w2, gbias, nw, hosted=None):
    t = p.shape[0]
    c = SCAN_ROWS
    n = t // c

    def body(pq_ref, pk_ref, pv_ref, sm_ref, pz_ref, w2_ref, gb_ref, nw_ref, y_ref, ck_ref, state):
        @pl.when(pl.program_id(0) == 0)
        def _():
            state[...] = jnp.zeros_like(state)

        s_in = state[...]
        ck_ref[0] = s_in
        y, s_out = _gla_chunk(pq_ref[...], pk_ref[...], pv_ref[...], sm_ref[...], pz_ref[...],
                              w2_ref[...], gb_ref[...], nw_ref[...], s_in)
        y_ref[...] = y.astype(y_ref.dtype)
        state[...] = s_out

    return _host_call(
        body, hosted, name="gla_fwd", grid=(n,),
        in_specs=[_col_spec(c, COL_DQ), _col_spec(c, COL_DK), _col_spec(c, COL_DV),
                  _small_spec(c, COL128_SMALL_D), _col_spec(c, COL_DZ),
                  _full_spec((128, GROUP)), _full_spec((1, GROUP)), _full_spec((1, GROUP))],
        out_specs=[_out_rows_spec(c, GROUP), pl.BlockSpec((1, GROUP, GROUP), lambda i: (i, 0, 0))],
        out_shape=[jax.ShapeDtypeStruct((t, GROUP), BF16), jax.ShapeDtypeStruct((n, GROUP, GROUP), F32)],
        scratch_shapes=[pltpu.VMEM((GROUP, GROUP), F32)],
        args=(p, p, p, p, p, w2, gbias, nw))


def _gla_bwd(p, dmix, states, w2, gbias, nw, hosted=None):
    t = p.shape[0]
    c = SCAN_ROWS
    n = t // c

    def body(pq_ref, pk_ref, pv_ref, sm_ref, pz_ref, dy_ref, ck_ref, w2_ref, gb_ref, nw_ref,
             dpq_ref, dpk_ref, dpv_ref, dsm_ref, dpz_ref, dw2_ref, dgb_ref, dnw_ref, dstate):
        @pl.when(pl.program_id(0) == 0)
        def _():
            dstate[...] = jnp.zeros_like(dstate)
            dw2_ref[...] = jnp.zeros_like(dw2_ref)
            dgb_ref[...] = jnp.zeros_like(dgb_ref)
            dnw_ref[...] = jnp.zeros_like(dnw_ref)

        args = (pq_ref[...], pk_ref[...], pv_ref[...], sm_ref[...], pz_ref[...],
                w2_ref[...], gb_ref[...], nw_ref[...], ck_ref[0])
        _, vjp = jax.vjp(_gla_chunk, *args)
        dpq, dpk, dpv, dsm, dpz, dw2, dgb, dnw, dst = vjp((dy_ref[...], dstate[...]))
        dpq_ref[...] = dpq.astype(dpq_ref.dtype)
        dpk_ref[...] = dpk.astype(dpk_ref.dtype)
        dpv_ref[...] = dpv.astype(dpv_ref.dtype)
        dsm_ref[...] = dsm.astype(dsm_ref.dtype)
        dpz_ref[...] = dpz.astype(dpz_ref.dtype)
        dstate[...] = dst
        dw2_ref[...] += dw2
        dgb_ref[...] += dgb
        dnw_ref[...] += dnw

    return _host_call(
        body, hosted, name="gla_bwd", grid=(n,),
        in_specs=[_col_spec(c, COL_DQ, n), _col_spec(c, COL_DK, n), _col_spec(c, COL_DV, n),
                  _small_spec(c, COL128_SMALL_D, n), _col_spec(c, COL_DZ, n),
                  pl.BlockSpec((c, GROUP), lambda i: (n - 1 - i, 3)),
                  pl.BlockSpec((1, GROUP, GROUP), lambda i: (n - 1 - i, 0, 0)),
                  _full_spec((128, GROUP)), _full_spec((1, GROUP)), _full_spec((1, GROUP))],
        out_specs=[_out_rows_spec(c, GROUP, n)] * 3 + [_out_rows_spec(c, 128, n), _out_rows_spec(c, GROUP, n),
                   _full_spec((128, GROUP)), _full_spec((1, GROUP)), _full_spec((1, GROUP))],
        out_shape=[jax.ShapeDtypeStruct((t, GROUP), BF16)] * 3 + [
            jax.ShapeDtypeStruct((t, 128), BF16), jax.ShapeDtypeStruct((t, GROUP), BF16),
            jax.ShapeDtypeStruct((128, GROUP), F32), jax.ShapeDtypeStruct((1, GROUP), F32),
            jax.ShapeDtypeStruct((1, GROUP), F32)],
        scratch_shapes=[pltpu.VMEM((GROUP, GROUP), F32)],
        args=(p, p, p, p, p, dmix, states, w2, gbias, nw))


def _pick_tile(n, pref):
    for cand in pref:
        if n % cand == 0:
            return cand
    return n


MM_TILE_CAP = 1408


def _largest_tile(n, cap):
    best = None
    for mult in range(1, cap // 128 + 1):
        if n % (128 * mult) == 0:
            best = 128 * mult
    return best if best is not None else n


def _half_index(t, per_half, middle):
    half = jnp.where(t >= per_half, 1, 0)
    return half, middle, t - half * per_half


def _matmul(a, b, mode, out_dtype, name, res=None, pieces=None, norm_w=None):
    a_list = list(a) if isinstance(a, (list, tuple)) else [a]
    a_rows, a_cols = a_list[0].shape[0], sum(x.shape[1] for x in a_list)
    if mode == "nn":
        (m, k), n = (a_rows, a_cols), b.shape[1]
    elif mode == "nt":
        (m, k), n = (a_rows, a_cols), b.shape[0]
    else:
        (k, m), n = (a_rows, a_cols), b.shape[-1] * (2 if b.ndim == 3 else 1)
    tm = _largest_tile(m, MM_TILE_CAP)
    tn = _largest_tile(b.shape[-1] if b.ndim == 3 else n, MM_TILE_CAP)
    if pieces == "cols":
        tm, tn = m // 2, n // N_CHIPS
    tk = _largest_tile(k, MM_TILE_CAP)
    if len(a_list) > 1:
        tk, tm = (k, tm) if mode == "nn" else (tk, m)
    nk = k // tk
    na = len(a_list)
    if mode == "nn":
        a_specs = [pl.BlockSpec((tm, tk if na == 1 else x.shape[1]), lambda i, j, kk: (i, kk)) for x in a_list]
        b_spec = pl.BlockSpec((tk, tn), lambda i, j, kk: (kk, j))
        dims = (1, 0)
    elif mode == "nt":
        a_specs = [pl.BlockSpec((tm, tk), lambda i, j, kk: (i, kk))]
        b_spec = pl.BlockSpec((tn, tk), lambda i, j, kk: (j, kk))
        dims = (1, 1)
    else:
        a_specs = [pl.BlockSpec((tk, tm if na == 1 else x.shape[1]), lambda i, j, kk: (kk, i)) for x in a_list]
        if b.ndim == 3:
            njh = b.shape[-1] // tn
            b_spec = pl.BlockSpec((None, tk, tn), lambda i, j, kk: _half_index(j, njh, kk))
        else:
            b_spec = pl.BlockSpec((tk, tn), lambda i, j, kk: (kk, j))
        dims = (0, 0)
    o_spec = pl.BlockSpec((tm, tn), lambda i, j, kk: (i, j))
    o_shape = (m, n)
    if pieces == "cols":
        o_spec = pl.BlockSpec((None, None, tm, tn), lambda i, j, kk: (i, j, 0, 0))
        o_shape = (2, N_CHIPS, tm, tn)
    has_res = res is not None
    has_norm = norm_w is not None

    def body(*refs):
        a_refs, b_ref = refs[:na], refs[na]
        pos = na + 1
        r_ref = refs[pos] if has_res else None
        pos += has_res
        nw_ref = refs[pos] if has_norm else None
        pos += has_norm
        o_ref = refs[pos]
        h_ref = refs[pos + 1] if has_norm else None
        if na == 1:
            a_val = a_refs[0][...].astype(MXU_DTYPE)
        else:
            a_val = jnp.concatenate([r[...].astype(MXU_DTYPE) for r in a_refs], axis=1)
        part = _dg(a_val, b_ref[...].astype(MXU_DTYPE), *dims)

        def finish(out):
            if has_res:
                out = out + r_ref[...]
            o_ref[...] = out.astype(o_ref.dtype)
            if has_norm:
                r = lax.rsqrt(jnp.mean(out * out, axis=-1, keepdims=True) + EPS)
                h_ref[...] = (out * r * nw_ref[...]).astype(h_ref.dtype)

        if nk == 1:
            finish(part)
            return
        acc = refs[-1]
        kk = pl.program_id(2)

        @pl.when(kk == 0)
        def _():
            acc[...] = part

        @pl.when(kk > 0)
        def _():
            acc[...] += part

        @pl.when(kk == nk - 1)
        def _():
            finish(acc[...])

    in_specs = a_specs + [b_spec] + ([o_spec] if has_res else [])
    args = (*a_list, b) + ((res,) if has_res else ())
    out_specs, out_shape = o_spec, jax.ShapeDtypeStruct(o_shape, out_dtype)
    if has_norm:
        assert mode == "nn" and tn == n
        in_specs.append(pl.BlockSpec((1, n), lambda i, j, kk: (0, 0)))
        args += (norm_w,)
        out_specs, out_shape = [o_spec, o_spec], [out_shape, jax.ShapeDtypeStruct(o_shape, BF16)]
    return pl.pallas_call(
        body, name=name, grid=(m // tm, n // tn, nk), in_specs=in_specs, out_specs=out_specs,
        out_shape=out_shape,
        scratch_shapes=[pltpu.VMEM((tm, tn), F32)] if nk > 1 else [],
        compiler_params=_cparams(("parallel", "parallel", "arbitrary")),
    )(*args)


def _matmul_nt_norm_bwd(a, b, x, w, dres, name):
    n = b.shape[0]
    m = a.shape[-2]
    tm = _largest_tile(m, 1024)
    if a.ndim == 3:
        kh = a.shape[2]
        k = 2 * kh
        tk = _largest_tile(kh, MM_TILE_CAP)
        nkh = kh // tk
        a_spec = pl.BlockSpec((None, tm, tk), lambda i, kk: _half_index(kk, nkh, i))
    else:
        k = a.shape[1]
        tk = _largest_tile(k, MM_TILE_CAP)
        a_spec = pl.BlockSpec((tm, tk), lambda i, kk: (i, kk))
    nk = k // tk

    def body(a_ref, b_ref, x_ref, w_ref, r_ref, dx_ref, dw_ref, acc):
        i, kk = pl.program_id(0), pl.program_id(1)
        part = _dg(a_ref[...].astype(MXU_DTYPE), b_ref[...].astype(MXU_DTYPE), 1, 1)

        @pl.when(kk == 0)
        def _():
            acc[...] = part

        @pl.when(kk > 0)
        def _():
            acc[...] += part

        @pl.when((i == 0) & (kk == 0))
        def _():
            dw_ref[...] = jnp.zeros_like(dw_ref)

        @pl.when(kk == nk - 1)
        def _():
            g = acc[...]
            xv = x_ref[...]
            r = lax.rsqrt(jnp.mean(xv * xv, axis=-1, keepdims=True) + EPS)
            xhat = xv * r
            dw_ref[...] += jnp.sum(g * xhat, axis=0, keepdims=True)
            gx = g * w_ref[...]
            dx_ref[...] = r_ref[...] + r * (gx - xhat * jnp.mean(gx * xhat, axis=-1, keepdims=True))

    row_spec = pl.BlockSpec((tm, n), lambda i, kk: (i, 0))
    return pl.pallas_call(
        body, name=name, grid=(m // tm, nk),
        in_specs=[a_spec, pl.BlockSpec((n, tk), lambda i, kk: (0, kk)),
                  row_spec, pl.BlockSpec((1, n), lambda i, kk: (0, 0)), row_spec],
        out_specs=[row_spec, pl.BlockSpec((1, n), lambda i, kk: (0, 0))],
        out_shape=[jax.ShapeDtypeStruct((m, n), F32), jax.ShapeDtypeStruct((1, n), F32)],
        scratch_shapes=[pltpu.VMEM((tm, n), F32)],
        compiler_params=_cparams(("arbitrary", "arbitrary")),
    )(a, b, x, w, dres)


NORM_ROWS = 512


def _rmsnorm_fwd(x, w, name):
    t, d = x.shape

    def body(x_ref, w_ref, o_ref):
        xv = x_ref[...]
        r = lax.rsqrt(jnp.mean(xv * xv, axis=-1, keepdims=True) + EPS)
        o_ref[...] = (xv * r * w_ref[...]).astype(o_ref.dtype)

    return pl.pallas_call(
        body, name=name, grid=(t // NORM_ROWS,),
        in_specs=[pl.BlockSpec((NORM_ROWS, d), lambda i: (i, 0)), _full_spec((1, d))],
        out_specs=pl.BlockSpec((NORM_ROWS, d), lambda i: (i, 0)),
        out_shape=jax.ShapeDtypeStruct((t, d), BF16),
        compiler_params=_cparams(("parallel",)),
    )(x, w)


SWIGLU_ROWS = 128


def _ffn_up_swiglu(h, w_gate_up):
    m, k = h.shape
    tm = _largest_tile(m, 512)
    tn = _largest_tile(D_FF, MM_TILE_CAP)
    nj = D_FF // tn

    def body(a_ref, bg_ref, bu_ref, g_ref, u_ref, act_ref):
        a = a_ref[...].astype(MXU_DTYPE)
        gate = _dg(a, bg_ref[...].astype(MXU_DTYPE), 1, 0)
        up = _dg(a, bu_ref[...].astype(MXU_DTYPE), 1, 0)
        g_ref[...] = gate
        u_ref[...] = up
        act_ref[...] = (_silu(gate) * up).astype(act_ref.dtype)

    o_spec = pl.BlockSpec((tm, tn), lambda i, j: (i, j))
    return pl.pallas_call(
        body, name="ffn_up", grid=(m // tm, nj),
        in_specs=[pl.BlockSpec((tm, k), lambda i, j: (i, 0)), pl.BlockSpec((k, tn), lambda i, j: (0, j)),
                  pl.BlockSpec((k, tn), lambda i, j: (0, j + nj))],
        out_specs=[o_spec, o_spec, o_spec],
        out_shape=[jax.ShapeDtypeStruct((m, D_FF), F32), jax.ShapeDtypeStruct((m, D_FF), F32),
                   jax.ShapeDtypeStruct((m, D_FF), BF16)],
        compiler_params=_cparams(("parallel", "parallel")),
    )(h, w_gate_up, w_gate_up)


def _ffn_down_dx_swiglu(dx, w_down, gate, up):
    m, k = dx.shape
    tm = _largest_tile(m, 512)
    tn = _largest_tile(D_FF, MM_TILE_CAP)

    def body(a_ref, b_ref, g_ref, u_ref, o_ref):
        da = _dg(a_ref[...].astype(MXU_DTYPE), b_ref[...].astype(MXU_DTYPE), 1, 1)
        gate = g_ref[...]
        sg = jax.nn.sigmoid(gate)
        o_ref[0] = (da * u_ref[...] * (sg * (1.0 + gate * (1.0 - sg)))).astype(o_ref.dtype)
        o_ref[1] = (da * gate * sg).astype(o_ref.dtype)

    tile = pl.BlockSpec((tm, tn), lambda i, j: (i, j))
    return pl.pallas_call(
        body, name="ffn_down_dx", grid=(m // tm, D_FF // tn),
        in_specs=[pl.BlockSpec((tm, k), lambda i, j: (i, 0)), pl.BlockSpec((tn, k), lambda i, j: (j, 0)), tile, tile],
        out_specs=pl.BlockSpec((2, tm, tn), lambda i, j: (0, i, j)),
        out_shape=jax.ShapeDtypeStruct((2, m, D_FF), MXU_DTYPE),
        compiler_params=_cparams(("parallel", "parallel")),
    )(dx, w_down, gate, up)


def _loss_head(x, w, target):
    t, d = x.shape

    def fwd(xv, wv, tv):
        r = lax.rsqrt(jnp.mean(xv * xv, axis=-1, keepdims=True) + EPS)
        err = xv * r * wv - tv
        return 0.5 * jnp.sum(jnp.mean(err * err, axis=-1, keepdims=True), axis=0, keepdims=True)

    def body(x_ref, w_ref, t_ref, dx_ref, dw_ref, loss_ref):
        @pl.when(pl.program_id(0) == 0)
        def _():
            dw_ref[...] = jnp.zeros_like(dw_ref)
            loss_ref[...] = jnp.zeros_like(loss_ref)

        loss, vjp = jax.vjp(fwd, x_ref[...], w_ref[...], t_ref[...])
        dx, dw, _ = vjp(jnp.ones((1, 1), F32))
        dx_ref[...] = dx
        dw_ref[...] += dw
        loss_ref[...] += jnp.broadcast_to(loss, loss_ref.shape)

    return pl.pallas_call(
        body, name="loss_head", grid=(t // NORM_ROWS,),
        in_specs=[pl.BlockSpec((NORM_ROWS, d), lambda i: (i, 0)), _full_spec((1, d)),
                  pl.BlockSpec((NORM_ROWS, d), lambda i: (i, 0))],
        out_specs=[pl.BlockSpec((NORM_ROWS, d), lambda i: (i, 0)), _full_spec((1, d)), _full_spec((8, 128))],
        out_shape=[jax.ShapeDtypeStruct((t, d), F32), jax.ShapeDtypeStruct((1, d), F32),
                   jax.ShapeDtypeStruct((8, 128), F32)],
        compiler_params=_cparams(("arbitrary",)),
    )(x, w, target)


def _pad_w_in(w):
    z = lambda n: jnp.zeros((w.shape[0], n), w.dtype)
    return jnp.concatenate([w[:, 0:2048], w[:, 2056:2312], w[:, 2312:3080], w[:, 3096:3352],
                            w[:, 2048:2056], z(120), w[:, 3080:3096], z(112)], axis=1)


def _unpad_w_in(wp):
    return jnp.concatenate([wp[:, 0:2048], wp[:, 3328:3336], wp[:, 2048:2304], wp[:, 2304:3072],
                            wp[:, 3456:3472], wp[:, 3072:3328]], axis=1)


def _pad_rows(a, rows):
    return jnp.concatenate([a, jnp.zeros((rows - a.shape[0],) + a.shape[1:], a.dtype)], axis=0)


def _pad_lanes(a, lanes):
    return jnp.concatenate([a, jnp.zeros(a.shape[:-1] + (lanes - a.shape[-1],), a.dtype)], axis=-1)


def _layer_params(l, small):
    dn_cw = small["dn_conv_w"][l]
    return dict(
        ln_w=small["sgu_ln_w"][l][None], ln_b=small["sgu_ln_b"][l][None],
        ws=small["sgu_w_spatial"][l], bs_t=_pad_lanes(small["sgu_b_spatial"][l].T, 128),
        sc_cw=_pad_rows(small["sc_conv_w"][l], HALO),
        dn_cw=jnp.stack([_pad_rows(dn_cw[:, j * GROUP:(j + 1) * GROUP], HALO) for j in range(3)]),
        dn_al=_pad_lanes(small["dn_a_log"][l][None], 128), dn_dt=_pad_lanes(small["dn_dt_bias"][l][None], 128),
        dn_nw=jnp.tile(small["dn_norm_w"][l][None], (1, HEADS)),
        gla_w2=_pad_rows(small["gla_w_gate2"][l], 128), gla_gb=small["gla_gate_bias"][l][None],
        gla_nw=jnp.tile(small["gla_norm_w"][l][None], (1, HEADS)),
    )


def _exchange_piece(name, grad):
    if grad.ndim == 4:
        return grad
    if name == "w_in":
        grad = _unpad_w_in(grad)
    if name in ("w_in", "w_gate_up"):
        r, c4 = grad.shape
        return jnp.transpose(grad.reshape(2, r // 2, N_CHIPS, c4 // N_CHIPS), (0, 2, 1, 3))
    r4, c = grad.shape
    return jnp.transpose(grad.reshape(N_CHIPS, 2, r4 // (2 * N_CHIPS), c), (1, 0, 2, 3))


def _reduce_on_chip(pieces):
    return _pair_add(pieces, _sibling_swap(pieces))


def _local_step(x, target, big, small, late_weights=None, exchange=False, small_extra=None):
    saved = []
    h = x
    h1 = _rmsnorm_fwd(h, small["norm1_w"][0][None], "norm1_fwd")
    for l in range(DEPTH):
        lp = _layer_params(l, small)
        p = _matmul(h1, big["w_in"][l], "nn", F32, "proj_in")
        y_a = _sgu_fwd(p, lp["ln_w"], lp["ln_b"], lp["ws"], lp["bs_t"])
        y_b = _sc_fwd(p, lp["sc_cw"])
        host1 = host2 = None
        if late_weights is not None:
            shards, finish = late_weights[l]
            modes = ["layer"] * len(shards)
            host1 = (1, shards, modes, _exchange_buffers(shards, modes))
        (y_c, st_c, inv_c), ex = _dn_fwd(p, lp["dn_cw"], lp["dn_al"], lp["dn_dt"], lp["dn_nw"], hosted=host1)
        if host1 is not None:
            host2 = (2, None, None, ex)
        (y_d, st_d), ex = _gla_fwd(p, lp["gla_w2"], lp["gla_gb"], lp["gla_nw"], hosted=host2)
        if host2 is not None:
            big = finish(big, ex)
        mix = [y_a, y_b, y_c, y_d]
        x1, h2 = _matmul(mix, big["w_out"][l], "nn", F32, "proj_out", res=h, norm_w=small["norm2_w"][l][None])
        gate, up, act = _ffn_up_swiglu(h2, big["w_gate_up"][l])
        if l + 1 < DEPTH:
            x2, h1_next = _matmul(act, big["w_down"][l], "nn", F32, "ffn_down", res=x1,
                                  norm_w=small["norm1_w"][l + 1][None])
        else:
            x2, h1_next = _matmul(act, big["w_down"][l], "nn", F32, "ffn_down", res=x1), None
        saved.append(dict(x0=h, h1=h1, p=p, st_c=st_c, inv_c=inv_c, st_d=st_d, mix=mix, x1=x1, h2=h2, gate=gate,
                          up=up, act=act, lp=lp))
        h, h1 = x2, h1_next

    dx, d_final, loss = _loss_head(h, small["final_norm_w"][None], target)
    gbig = {k: [None] * DEPTH for k in ("w_in", "w_out", "w_gate_up", "w_down")}
    gs = {k: [None] * DEPTH for k in ("norm1_w", "sgu_ln_w", "sgu_ln_b", "sgu_w_spatial", "sgu_b_spatial", "sc_conv_w",
                                     "dn_conv_w", "dn_a_log", "dn_dt_bias", "dn_norm_w", "gla_w_gate2",
                                     "gla_gate_bias", "gla_norm_w", "norm2_w")}
    carry = []
    contribs = {}
    for l in reversed(range(DEPTH)):
        s = saved[l]
        lp = s["lp"]
        gbig["w_down"][l] = _matmul(s["act"], dx, "tn", GRAD_WIRE_DTYPE, "ffn_down_dw")
        dgu = _ffn_down_dx_swiglu(dx, big["w_down"][l], s["gate"], s["up"])
        gbig["w_gate_up"][l] = _matmul(s["h2"], dgu, "tn", GRAD_WIRE_DTYPE, "ffn_up_dw",
                                       pieces="cols" if exchange else None)
        dx1, gs["norm2_w"][l] = _matmul_nt_norm_bwd(dgu, big["w_gate_up"][l], s["x1"], small["norm2_w"][l][None], dx,
                                                    "ffn_up_dx")
        gbig["w_out"][l] = _matmul(s["mix"], dx1, "tn", GRAD_WIRE_DTYPE, "proj_out_dw")
        dmix = _matmul(dx1, big["w_out"][l], "nt", F32, "proj_out_dx")
        p = s["p"]
        dpu, dpv, g_lw, g_lb, g_ws, g_bs = _sgu_bwd(p, dmix, lp["ln_w"], lp["ln_b"], lp["ws"], lp["bs_t"])
        dpb, dpc, dph, g_sc = _sc_bwd(p, dmix, lp["sc_cw"])
        host1 = host2 = None
        if exchange:
            unit = carry + [(n, l, _exchange_piece(n, gbig[n][l])) for n in ("w_out", "w_gate_up", "w_down")]
            carry = []
            summed = _reduce_on_chip([piece for _, _, piece in unit])
            modes = ["piece"] * len(summed)
            host1 = (1, summed, modes, _exchange_buffers(summed, modes))
        (dcq, dck, dcv, dcs, dcz, g_dcw, g_al, g_dt, g_dnw), ex = _dn_bwd(
            p, dmix, s["st_c"], s["inv_c"], lp["dn_cw"], lp["dn_al"], lp["dn_dt"], lp["dn_nw"], hosted=host1)
        if host1 is not None:
            host2 = (2, None, None, ex)
        (ddq, ddk, ddv, dds, ddz, g_w2, g_gb, g_gnw), ex = _gla_bwd(p, dmix, s["st_d"], lp["gla_w2"], lp["gla_gb"],
                                                                   lp["gla_nw"], hosted=host2)
        if host2 is not None:
            for (n, lay, _), got in zip(unit, ex):
                contribs[(n, lay)] = got
        dp = jnp.concatenate([dpu, dpv, dpb, dpc, dph, dcq, dck, dcv, dcz, ddq, ddk, ddv, ddz, dcs, dds], axis=1)
        gbig["w_in"][l] = _matmul(s["h1"], dp, "tn", GRAD_WIRE_DTYPE, "proj_in_dw")
        dx, gs["norm1_w"][l] = _matmul_nt_norm_bwd(dp, big["w_in"][l], s["x0"], small["norm1_w"][l][None], dx1,
                                                   "proj_in_dx")
        gs["sgu_ln_w"][l], gs["sgu_ln_b"][l] = g_lw[0], g_lb[0]
        gs["sgu_w_spatial"][l] = g_ws
        gs["sgu_b_spatial"][l] = g_bs[:, :HEADS].T
        gs["sc_conv_w"][l] = g_sc[:3]
        gs["dn_conv_w"][l] = jnp.concatenate([g_dcw[0, :4], g_dcw[1, :4], g_dcw[2, :4]], axis=1)
        gs["dn_a_log"][l], gs["dn_dt_bias"][l] = g_al[0, :HEADS], g_dt[0, :HEADS]
        gs["dn_norm_w"][l] = jnp.sum(g_dnw.reshape(HEADS, HEAD_DIM), axis=0)
        gs["gla_w_gate2"][l] = g_w2[:16]
        gs["gla_gate_bias"][l] = g_gb[0]
        gs["gla_norm_w"][l] = jnp.sum(g_gnw.reshape(HEADS, HEAD_DIM), axis=0)
        gs["norm1_w"][l] = gs["norm1_w"][l][0]
        gs["norm2_w"][l] = gs["norm2_w"][l][0]
        if exchange:
            carry = [("w_in", l, _exchange_piece("w_in", gbig["w_in"][l]))]
    gsmall = {k: jnp.stack(v) for k, v in gs.items()}
    gsmall["final_norm_w"] = d_final[0]
    if not exchange:
        return loss, dx, gbig, gsmall
    summed = _reduce_on_chip([piece for _, _, piece in carry])
    last = _chip_exchange(summed + [small_extra(gsmall, loss)], ["piece"] * len(summed) + ["whole"], "exchange_grads")
    for (n, lay, _), got in zip(carry, last):
        contribs[(n, lay)] = got
    return loss, dx, contribs, last[-1]


def _peer_chips(x, y):
    return [(1 - x, y, 2 * (1 - x) + y), (x, 1 - y, 2 * x + 1 - y), (1 - x, 1 - y, 2 * (1 - x) + 1 - y)]


def _chip_exchange(arrays, modes, name):
    na = len(arrays)
    bufs = _exchange_buffers(arrays, modes)

    def body(*refs):
        ins, outs = refs[:na], refs[2 * na:3 * na]
        send1, recv1, send2, recv2 = refs[3 * na:]
        _exchange_stage1(ins, outs, modes, send1, recv1, "start")
        _exchange_stage1(ins, outs, modes, send1, recv1, "wait")
        _exchange_stage2(outs, send2, recv2, "start")
        _exchange_stage2(outs, send2, recv2, "wait")

    any_spec = pl.BlockSpec(memory_space=pl.ANY)
    return pl.pallas_call(
        body, name=name,
        in_specs=[any_spec] * (2 * na), out_specs=[any_spec] * na,
        out_shape=[jax.ShapeDtypeStruct(b.shape, b.dtype) for b in bufs],
        input_output_aliases={na + a: a for a in range(na)},
        scratch_shapes=_stage1_sems(na) + _stage2_sems(na),
    )(*arrays, *bufs)


def _exchange_buffers(arrays, modes):
    c_idx = lax.axis_index("c")
    chip = 2 * lax.axis_index("x") + lax.axis_index("y")
    units = []
    for arr, md in zip(arrays, modes):
        if md == "layer":
            units.append(lax.dynamic_index_in_dim(arr, c_idx, 0, keepdims=False))
        elif md == "piece":
            units.append(lax.dynamic_index_in_dim(arr, chip, 0, keepdims=False))
        else:
            units.append(arr)
    any_spec = pl.BlockSpec(memory_space=pl.ANY)
    bufs = pl.pallas_call(
        lambda *refs: None, name="exchange_alloc", out_specs=[any_spec] * len(units),
        out_shape=[jax.ShapeDtypeStruct((2, N_CHIPS) + u.shape, u.dtype) for u in units],
    )()
    return [lax.dynamic_update_slice(buf, u[None, None], (c_idx, chip) + (0,) * u.ndim) for buf, u in zip(bufs, units)]


def _stage1_sems(na):
    return [pltpu.SemaphoreType.DMA((na, 3)), pltpu.SemaphoreType.DMA((na, 3))]


def _stage2_sems(na):
    return [pltpu.SemaphoreType.DMA((na,)), pltpu.SemaphoreType.DMA((na,))]


def _exchange_stage1(ins, outs, modes, send1, recv1, what):
    x, y, c = lax.axis_index("x"), lax.axis_index("y"), lax.axis_index("c")
    me = 2 * x + y
    for a in range(len(ins)):
        for k, (px, py, pidx) in enumerate(_peer_chips(x, y)):
            if modes[a] == "layer":
                src = ins[a].at[c]
            else:
                src = ins[a].at[pidx] if modes[a] == "piece" else ins[a]
            if what == "start":
                pltpu.make_async_remote_copy(
                    src_ref=src, dst_ref=outs[a].at[c, me], send_sem=send1.at[a, k], recv_sem=recv1.at[a, k],
                    device_id=(px, py, c), device_id_type=MESH).start()
            else:
                cp = pltpu.make_async_remote_copy(
                    src_ref=src, dst_ref=outs[a].at[c, pidx], send_sem=send1.at[a, k], recv_sem=recv1.at[a, k],
                    device_id=(px, py, c), device_id_type=MESH)
                cp.wait_send()
                cp.wait_recv()


def _exchange_stage2(outs, send2, recv2, what):
    x, y, c = lax.axis_index("x"), lax.axis_index("y"), lax.axis_index("c")
    sibling = (x, y, 1 - c)
    for a in range(len(outs)):
        if what == "start":
            pltpu.make_async_remote_copy(
                src_ref=outs[a].at[c], dst_ref=outs[a].at[c], send_sem=send2.at[a], recv_sem=recv2.at[a],
                device_id=sibling, device_id_type=MESH).start()
        else:
            cp = pltpu.make_async_remote_copy(
                src_ref=outs[a].at[c], dst_ref=outs[a].at[1 - c], send_sem=send2.at[a], recv_sem=recv2.at[a],
                device_id=sibling, device_id_type=MESH)
            cp.wait_send()
            cp.wait_recv()


def _sibling_swap(arrays):
    na = len(arrays)

    def body(*refs):
        ins, theirs = refs[:na], refs[na:2 * na]
        send_sems, recv_sems = refs[2 * na:]
        x, y, c = lax.axis_index("x"), lax.axis_index("y"), lax.axis_index("c")
        sibling = (x, y, 1 - c)
        for a in range(na):
            pltpu.make_async_remote_copy(
                src_ref=ins[a].at[1 - c], dst_ref=theirs[a], send_sem=send_sems.at[a], recv_sem=recv_sems.at[a],
                device_id=sibling, device_id_type=MESH).start()
        for a in range(na):
            cp = pltpu.make_async_remote_copy(
                src_ref=ins[a].at[1 - c], dst_ref=theirs[a], send_sem=send_sems.at[a], recv_sem=recv_sems.at[a],
                device_id=sibling, device_id_type=MESH)
            cp.wait_send()
            cp.wait_recv()

    any_spec = pl.BlockSpec(memory_space=pl.ANY)
    return pl.pallas_call(
        body, name="sibling_swap",
        in_specs=[any_spec] * na, out_specs=[any_spec] * na,
        out_shape=[jax.ShapeDtypeStruct(s.shape[1:], s.dtype) for s in arrays],
        scratch_shapes=[pltpu.SemaphoreType.DMA((na,)), pltpu.SemaphoreType.DMA((na,))],
    )(*arrays)


PAIR_ADD_STEPS = 8


def _pair_add(boths, theirs):
    na = len(boths)
    core = lax.axis_index("c").astype(jnp.int32).reshape(1)
    flat_b = [b.reshape(2, b.shape[1] * b.shape[2], b.shape[3]) for b in boths]
    flat_t = [t.reshape(t.shape[0] * t.shape[1], t.shape[2]) for t in theirs]
    rows = [t.shape[0] // PAIR_ADD_STEPS for t in flat_t]

    def body(core_ref, *refs):
        for a in range(na):
            refs[2 * na + a][...] = (refs[a][...].astype(F32) + refs[na + a][...].astype(F32)).astype(
                refs[2 * na + a].dtype)

    own = [pl.BlockSpec((None, r, t.shape[1]), lambda i, core_ref: (core_ref[0], i, 0)) for r, t in zip(rows, flat_t)]
    plain = [pl.BlockSpec((r, t.shape[1]), lambda i, core_ref: (i, 0)) for r, t in zip(rows, flat_t)]
    outs = pl.pallas_call(
        body, name="pair_add",
        grid_spec=pltpu.PrefetchScalarGridSpec(
            num_scalar_prefetch=1, grid=(PAIR_ADD_STEPS,), in_specs=own + plain, out_specs=plain),
        out_shape=[jax.ShapeDtypeStruct(t.shape, t.dtype) for t in flat_t],
        compiler_params=_cparams(("parallel",)),
    )(core, *flat_b, *flat_t)
    return [o.reshape(t.shape) for o, t in zip(outs, theirs)]


def _adamw_math(g, w, m, v):
    m2 = ADAM_B1 * m + (1.0 - ADAM_B1) * g
    v2 = ADAM_B2 * v + (1.0 - ADAM_B2) * (g * g)
    m_hat = m2 / (1.0 - ADAM_B1 ** ADAM_STEP)
    v_hat = v2 / (1.0 - ADAM_B2 ** ADAM_STEP)
    delta = -ADAM_LR * (m_hat / (jnp.sqrt(v_hat) + ADAM_EPS) + ADAM_WD * w)
    return delta, m2, v2


def _adamw_big(contrib, w, m, v, layer, name, prev=None):
    _, r, c = w.shape
    rh = r // 2
    tr = _pick_tile(rh, (256, 176, 128, 64, 8))
    nj = rh // tr
    blk = pl.BlockSpec((1, tr, c), lambda h, j: (layer, h * nj + j, 0))
    n_prev = 0 if prev is None else 4

    def body(*refs):
        g_ref, w_ref, m_ref, v_ref = refs[:4]
        go_ref, d_ref, mo_ref, vo_ref = refs[4 + n_prev:]
        g = g_ref[0, 0].astype(F32)
        for s in range(1, N_CHIPS):
            g = g + g_ref[0, s].astype(F32)
        delta, m2, v2 = _adamw_math(g, w_ref[0], m_ref[0], v_ref[0])
        go_ref[0] = g
        d_ref[0] = delta
        mo_ref[0] = m2
        vo_ref[0] = v2

    any_spec = pl.BlockSpec(memory_space=pl.ANY)
    return pl.pallas_call(
        body, name=name, grid=(2, nj),
        in_specs=[pl.BlockSpec((1, N_CHIPS, tr, c), lambda h, j: (h, 0, j, 0)), blk, blk, blk] + [any_spec] * n_prev,
        out_specs=[blk] * 4, out_shape=[jax.ShapeDtypeStruct(w.shape, F32)] * 4,
        input_output_aliases={4 + a: a for a in range(n_prev)},
        compiler_params=_cparams(("parallel", "parallel")),
    )(contrib, w, m, v, *([] if prev is None else prev))


def _sum_small(contrib):
    rows = contrib.shape[2]

    def body(g_ref, o_ref):
        total = g_ref[0, 0]
        for j in range(1, N_DEV):
            total = total + g_ref[j // N_CHIPS, j % N_CHIPS]
        o_ref[...] = total

    return pl.pallas_call(
        body, name="sum_small", out_shape=jax.ShapeDtypeStruct((rows, 128), F32),
        compiler_params=_cparams(),
    )(contrib)


def _adamw_small(gs, ws, ms, vs):
    n = len(gs)
    as2d = lambda a: a.reshape(1, -1) if a.ndim == 1 else a

    def body(*refs):
        g_refs, w_refs, m_refs, v_refs = refs[:n], refs[n:2 * n], refs[2 * n:3 * n], refs[3 * n:4 * n]
        d_refs, mo_refs, vo_refs = refs[4 * n:5 * n], refs[5 * n:6 * n], refs[6 * n:]
        for j in range(n):
            delta, m2, v2 = _adamw_math(g_refs[j][...], w_refs[j][...], m_refs[j][...], v_refs[j][...])
            d_refs[j][...] = delta
            mo_refs[j][...] = m2
            vo_refs[j][...] = v2

    ins = [as2d(a) for a in (*gs, *ws, *ms, *vs)]
    outs = pl.pallas_call(
        body, name="adamw_small", out_shape=[jax.ShapeDtypeStruct(a.shape, F32) for a in ins[:n]] * 3,
        compiler_params=_cparams(),
    )(*ins)
    back = lambda group: [o.reshape(g.shape) for o, g in zip(group, gs)]
    return back(outs[:n]), back(outs[n:2 * n]), back(outs[2 * n:])


PACK_ALIGN = 8 * 128


def _packed_rows(shape):
    n = 1
    for d in shape:
        n *= d
    return (n + PACK_ALIGN - 1) // PACK_ALIGN * 8


def _pack(arrays):
    parts = []
    for a in arrays:
        flat = a.reshape(-1)
        pad = _packed_rows(a.shape) * 128 - flat.shape[0]
        if pad:
            flat = jnp.concatenate([flat, jnp.zeros((pad,), F32)])
        parts.append(flat.reshape(-1, 128))
    return jnp.concatenate(parts, axis=0)


def _unpack(packed, shapes):
    out, row = [], 0
    for s in shapes:
        rows = _packed_rows(s)
        n = 1
        for d in s:
            n *= d
        out.append(packed[row:row + rows].reshape(-1)[:n].reshape(s))
        row += rows
    return out


SMALL_NAMES = ("norm1_w", "sgu_ln_w", "sgu_ln_b", "sgu_w_spatial", "sgu_b_spatial", "sc_conv_w", "dn_conv_w",
               "dn_a_log", "dn_dt_bias", "dn_norm_w", "gla_w_gate2", "gla_gate_bias", "gla_norm_w", "norm2_w",
               "final_norm_w")
SHARDED_SMALL = ("sc_conv_w", "dn_conv_w", "gla_w_gate2")
BIG_NAMES = ("w_in", "w_out", "w_gate_up", "w_down")
WEIGHT_ORDER = ("norm1_w", "w_in", "sgu_ln_w", "sgu_ln_b", "sgu_w_spatial", "sgu_b_spatial", "sc_conv_w", "dn_conv_w",
                "dn_a_log", "dn_dt_bias", "dn_norm_w", "gla_w_gate2", "gla_gate_bias", "gla_norm_w", "w_out",
                "norm2_w", "w_gate_up", "w_down", "final_norm_w")


def _cols_from_shards(g):
    l, n, r, c = g.shape
    return jnp.transpose(g, (0, 2, 1, 3)).reshape(l, r, n * c)


def kernel(x, norm1_w, w_in, sgu_ln_w, sgu_ln_b, sgu_w_spatial, sgu_b_spatial, sc_conv_w, dn_conv_w, dn_a_log, dn_dt_bias, dn_norm_w, gla_w_gate2, gla_gate_bias, gla_norm_w, w_out, norm2_w, w_gate_up, w_down, final_norm_w, loss_target, m_norm1_w, m_w_in, m_sgu_ln_w, m_sgu_ln_b, m_sgu_w_spatial, m_sgu_b_spatial, m_sc_conv_w, m_dn_conv_w, m_dn_a_log, m_dn_dt_bias, m_dn_norm_w, m_gla_w_gate2, m_gla_gate_bias, m_gla_norm_w, m_w_out, m_norm2_w, m_w_gate_up, m_w_down, m_final_norm_w, v_norm1_w, v_w_in, v_sgu_ln_w, v_sgu_ln_b, v_sgu_w_spatial, v_sgu_b_spatial, v_sc_conv_w, v_dn_conv_w, v_dn_a_log, v_dn_dt_bias, v_dn_norm_w, v_gla_w_gate2, v_gla_gate_bias, v_gla_norm_w, v_w_out, v_norm2_w, v_w_gate_up, v_w_down, v_final_norm_w):
    w = dict(norm1_w=norm1_w, w_in=w_in, sgu_ln_w=sgu_ln_w, sgu_ln_b=sgu_ln_b, sgu_w_spatial=sgu_w_spatial,
             sgu_b_spatial=sgu_b_spatial, sc_conv_w=sc_conv_w, dn_conv_w=dn_conv_w, dn_a_log=dn_a_log,
             dn_dt_bias=dn_dt_bias, dn_norm_w=dn_norm_w, gla_w_gate2=gla_w_gate2, gla_gate_bias=gla_gate_bias,
             gla_norm_w=gla_norm_w, w_out=w_out, norm2_w=norm2_w, w_gate_up=w_gate_up, w_down=w_down,
             final_norm_w=final_norm_w)
    m = dict(norm1_w=m_norm1_w, w_in=m_w_in, sgu_ln_w=m_sgu_ln_w, sgu_ln_b=m_sgu_ln_b, sgu_w_spatial=m_sgu_w_spatial,
             sgu_b_spatial=m_sgu_b_spatial, sc_conv_w=m_sc_conv_w, dn_conv_w=m_dn_conv_w, dn_a_log=m_dn_a_log,
             dn_dt_bias=m_dn_dt_bias, dn_norm_w=m_dn_norm_w, gla_w_gate2=m_gla_w_gate2,
             gla_gate_bias=m_gla_gate_bias, gla_norm_w=m_gla_norm_w, w_out=m_w_out, norm2_w=m_norm2_w,
             w_gate_up=m_w_gate_up, w_down=m_w_down, final_norm_w=m_final_norm_w)
    v = dict(norm1_w=v_norm1_w, w_in=v_w_in, sgu_ln_w=v_sgu_ln_w, sgu_ln_b=v_sgu_ln_b, sgu_w_spatial=v_sgu_w_spatial,
             sgu_b_spatial=v_sgu_b_spatial, sc_conv_w=v_sc_conv_w, dn_conv_w=v_dn_conv_w, dn_a_log=v_dn_a_log,
             dn_dt_bias=v_dn_dt_bias, dn_norm_w=v_dn_norm_w, gla_w_gate2=v_gla_w_gate2,
             gla_gate_bias=v_gla_gate_bias, gla_norm_w=v_gla_norm_w, w_out=v_w_out, norm2_w=v_norm2_w,
             w_gate_up=v_w_gate_up, w_down=v_w_down, final_norm_w=v_final_norm_w)
    chip = 2 * lax.axis_index("x") + lax.axis_index("y")

    w_in_wire = w["w_in"].astype(MXU_DTYPE)
    row_halves = lambda a: a.reshape(2, a.shape[0] // 2, a.shape[1])

    def full_w_in(g):
        return _pad_w_in(jnp.transpose(g, (0, 2, 1, 3)).reshape(D_MODEL, IN_COLS))

    first = [row_halves(w_in_wire[0])] + [w[n] for n in SHARDED_SMALL]
    gathered = _chip_exchange(first, ["layer"] * len(first), "gather_first")
    w_in_0 = full_w_in(gathered[0])
    big = dict(w_in=[w_in_0, None])
    small = {n: w[n] for n in SMALL_NAMES if n not in SHARDED_SMALL}
    for j, n in enumerate(SHARDED_SMALL):
        small[n] = _cols_from_shards(gathered[1 + j])
    wire = {n: w[n].astype(MXU_DTYPE) for n in ("w_out", "w_gate_up", "w_down")}

    def cols_full(g):
        _, n, rh, c = g.shape
        return jnp.transpose(g, (0, 2, 1, 3)).reshape(2 * rh, n * c)

    def rows_full(g):
        _, n, rh, c = g.shape
        return jnp.transpose(g, (1, 0, 2, 3)).reshape(n * 2 * rh, c)

    def finish_0(big, g):
        return dict(w_in=[w_in_0, full_w_in(g[0])], w_out=[rows_full(g[1]), None],
                    w_gate_up=[cols_full(g[2]), None], w_down=[rows_full(g[3]), None])

    def finish_1(big, g):
        return dict(big, w_out=[big["w_out"][0], rows_full(g[0])], w_gate_up=[big["w_gate_up"][0], cols_full(g[1])],
                    w_down=[big["w_down"][0], rows_full(g[2])])

    late = [([row_halves(w_in_wire[1])] + [row_halves(wire[n][0]) for n in ("w_out", "w_gate_up", "w_down")], finish_0),
            ([row_halves(wire[n][1]) for n in ("w_out", "w_gate_up", "w_down")], finish_1)]

    small_shapes = [(DEPTH,) + w[n].shape[1:-1] + (w[n].shape[-1] * (N_CHIPS if n in SHARDED_SMALL else 1),)
                    if n != "final_norm_w" else w[n].shape for n in SMALL_NAMES] + [(1,)]

    def pack_small(gsmall, loss_tile):
        return _pack([gsmall[n] for n in SMALL_NAMES] + [loss_tile[0:1, 0]])

    _, grad_x, contribs, small_contrib = _local_step(
        x[0], loss_target[0], big, small, late_weights=late, exchange=True, small_extra=pack_small)

    out_g, out_d, out_m, out_v = {}, {}, {}, {}
    for j, n in enumerate(BIG_NAMES):
        outs = _adamw_big(contribs[(n, 1)], w[n], m[n], v[n], 1, "adamw_" + n + "_1")
        out_g[n], out_d[n], out_m[n], out_v[n] = _adamw_big(contribs[(n, 0)], w[n], m[n], v[n], 0,
                                                            "adamw_" + n + "_0", prev=outs)
    summed = _unpack(_sum_small(small_contrib), small_shapes)
    loss = summed[-1][0]
    for n, g in zip(SMALL_NAMES, summed[:-1]):
        if n in SHARDED_SMALL:
            cols = g.shape[-1] // N_CHIPS
            g = lax.dynamic_slice_in_dim(g, chip * cols, cols, axis=g.ndim - 1)
        out_g[n] = g
    d_s, m_s, v_s = _adamw_small([out_g[n] for n in SMALL_NAMES], [w[n] for n in SMALL_NAMES],
                                 [m[n] for n in SMALL_NAMES], [v[n] for n in SMALL_NAMES])
    for n, d_, m_, v_ in zip(SMALL_NAMES, d_s, m_s, v_s):
        out_d[n], out_m[n], out_v[n] = d_, m_, v_

    return (loss, grad_x[None], *[out_g[n] for n in WEIGHT_ORDER], *[out_d[n] for n in WEIGHT_ORDER],
            *[out_m[n] for n in WEIGHT_ORDER], *[out_v[n] for n in WEIGHT_ORDER])
```

```python
import functools

import jax
import jax.numpy as jnp
from jax import lax
from jax.experimental import pallas as pl
from jax.experimental.pallas import tpu as pltpu

F32 = jnp.float32
BF16 = jnp.bfloat16
MXU_DTYPE = jnp.bfloat16
GRAD_WIRE_DTYPE = jnp.bfloat16
HI = lax.Precision.HIGHEST
MESH = pl.DeviceIdType.MESH

D_MODEL = 1024
DEPTH = 2
GROUP = 256
HEADS = 4
HEAD_DIM = 64
SGU_CHUNK = 128
SCAN_CHUNK = 64
D_FF = 2816
EPS = 1e-6
IN_COLS = 3352
P_COLS = 3584
HALO = 8
N_CHIPS = 4
N_DEV = 8
VMEM_LIMIT = 56 * 1024 * 1024

ADAM_LR = 0.001
ADAM_B1 = 0.9
ADAM_B2 = 0.999
ADAM_EPS = 1e-08
ADAM_WD = 0.01
ADAM_STEP = 10

(COL_AU, COL_AV, COL_BB, COL_BC, COL_BH, COL_CQ, COL_CK, COL_CV, COL_CZ,
 COL_DQ, COL_DK, COL_DV, COL_DZ) = range(13)
COL128_SMALL_C = 26
COL128_SMALL_D = 27


def _cparams(sem=None):
    return pltpu.CompilerParams(dimension_semantics=sem, vmem_limit_bytes=VMEM_LIMIT)


def _iota(shape, dim):
    return lax.broadcasted_iota(jnp.int32, shape, dim)


def _dg(a, b, ca, cb, prec=None):
    return lax.dot_general(a, b, (((ca,), (cb,)), ((), ())), preferred_element_type=F32, precision=prec)


@functools.partial(jax.custom_vjp, nondiff_argnums=(2, 3))
def bdot(a, b, ca, cb):
    return _dg(a.astype(MXU_DTYPE), b.astype(MXU_DTYPE), ca, cb)


def _bdot_fwd(a, b, ca, cb):
    return bdot(a, b, ca, cb), (a, b)


def _bdot_bwd(ca, cb, res, g):
    a, b = res
    if ca == 1:
        da = bdot(g, b, 1, 1 if cb == 0 else 0)
    else:
        da = bdot(b, g, 1 if cb == 0 else 0, 1)
    if cb == 0:
        db = bdot(a, g, 0, 0) if ca == 1 else bdot(a, g, 1, 0)
    else:
        db = bdot(g, a, 0, 0) if ca == 1 else bdot(g, a, 0, 1)
    return da, db


bdot.defvjp(_bdot_fwd, _bdot_bwd)


def _pieces(a, n):
    out, r = [], a
    for i in range(n):
        p = r.astype(MXU_DTYPE)
        out.append(p)
        if i + 1 < n:
            r = r - p.astype(F32)
    return out


def _mdot_impl(a, b, ca, cb, sa, sb):
    total = None
    for i, x in enumerate(_pieces(a, sa)):
        for j, y in enumerate(_pieces(b, sb)):
            if i + j < max(sa, sb):
                t = _dg(x, y, ca, cb)
                total = t if total is None else total + t
    return total


@functools.partial(jax.custom_vjp, nondiff_argnums=(2, 3, 4, 5))
def mdot(a, b, ca, cb, sa, sb):
    return _mdot_impl(a, b, ca, cb, sa, sb)


def _mdot_fwd(a, b, ca, cb, sa, sb):
    return _mdot_impl(a, b, ca, cb, sa, sb), (a, b)


def _mdot_bwd(ca, cb, sa, sb, res, g):
    a, b = res
    ga, gb = (3 if sb == 1 else 2), (3 if sa == 1 else 2)
    if sa == 1:
        da = jnp.zeros_like(a)
    elif ca == 1:
        da = mdot(g, b, 1, 1 if cb == 0 else 0, ga, sb)
    else:
        da = mdot(b, g, 1 if cb == 0 else 0, 1, sb, ga)
    if sb == 1:
        db = jnp.zeros_like(b)
    elif cb == 0:
        db = mdot(a, g, 0, 0, sa, gb) if ca == 1 else mdot(a, g, 1, 0, sa, gb)
    else:
        db = mdot(g, a, 0, 0, gb, sa) if ca == 1 else mdot(g, a, 0, 1, gb, sa)
    return da, db


mdot.defvjp(_mdot_fwd, _mdot_bwd)


def mask_r(a, m, ca=1, cb=0):
    return mdot(a, m, ca, cb, 3, 1)


def mask_l(m, b, ca=1, cb=0):
    return mdot(m, b, ca, cb, 1, 3)


def ddot(a, b, ca=1, cb=0):
    return mdot(a, b, ca, cb, 2, 2)


def _head_mask(h):
    return ((_iota((1, GROUP), 1) >> 6) == h).astype(F32)


def _block_diag_mask():
    return ((_iota((GROUP, GROUP), 0) >> 6) == (_iota((GROUP, GROUP), 1) >> 6)).astype(F32)


def _expand_mat(offset):
    return ((_iota((128, GROUP), 0) - offset) == (_iota((128, GROUP), 1) >> 6)).astype(F32)


def _tril(n, strict=False):
    r, c = _iota((n, n), 0), _iota((n, n), 1)
    return (r > c) if strict else (r >= c)


def _row_pick(x, row):
    return jnp.sum(jnp.where(_iota(x.shape, 0) == row, x, 0.0), axis=0, keepdims=True)


def _shift_rows_impl(x, halo, j):
    n = x.shape[0]
    r = _iota(x.shape, 0)
    top = jnp.concatenate([pltpu.roll(halo, j, 0), jnp.zeros((n - HALO, x.shape[1]), x.dtype)], axis=0)
    return jnp.where(r >= j, pltpu.roll(x, j, 0), top)


def _mxu_round(a):
    return a.astype(MXU_DTYPE).astype(F32)


@functools.partial(jax.custom_vjp, nondiff_argnums=(3,))
def _causal_conv(x, halo, w, width):
    xb, hb, wb = _mxu_round(x), _mxu_round(halo), _mxu_round(w)
    out = xb * _row_pick(wb, width - 1)
    for j in range(1, width):
        out = out + _shift_rows_impl(xb, hb, j) * _row_pick(wb, width - 1 - j)
    return out


def _causal_conv_fwd(x, halo, w, width):
    return _causal_conv(x, halo, w, width), (x, halo, w)


def _causal_conv_bwd(width, res, g):
    x, halo, w = res
    xb, hb, wb, gb = _mxu_round(x), _mxu_round(halo), _mxu_round(w), _mxu_round(g)
    n = g.shape[0]
    rows, rows8 = _iota(g.shape, 0), _iota(halo.shape, 0)
    dx = gb * _row_pick(wb, width - 1)
    dh = jnp.zeros_like(halo)
    dw = jnp.where(rows8 == width - 1, jnp.sum(xb * gb, axis=0, keepdims=True), 0.0)
    for j in range(1, width):
        gj = gb * _row_pick(wb, width - 1 - j)
        dx = dx + jnp.where(rows < n - j, pltpu.roll(gj, n - j, 0), 0.0)
        dh = dh + jnp.where(rows8 >= HALO - j, pltpu.roll(gj[0:HALO], HALO - j, 0), 0.0)
        tap = jnp.sum(_shift_rows_impl(xb, hb, j) * gb, axis=0, keepdims=True)
        dw = dw + jnp.where(rows8 == width - 1 - j, tap, 0.0)
    return dx, dh, dw


_causal_conv.defvjp(_causal_conv_fwd, _causal_conv_bwd)


def _head_sum(x, bd):
    return mask_r(x, bd)


def _softplus(x):
    return jnp.maximum(x, 0.0) + jnp.log1p(jnp.exp(-jnp.abs(x)))


def _log_sigmoid(x):
    return -_softplus(-x)


def _silu(x):
    return x * jax.nn.sigmoid(x)


def _head_rmsnorm_gate(o, nw, z, bd):
    ms = _head_sum(o * o, bd) * (1.0 / HEAD_DIM)
    return o * lax.rsqrt(ms + EPS) * nw * _silu(z)


def _sgu_chunk(pu, pv, ln_w, ln_b, ws0, ws1, ws2, ws3, bs_t):
    u = jax.nn.gelu(pu)
    g = jax.nn.gelu(pv)
    mu = jnp.mean(g, axis=-1, keepdims=True)
    var = jnp.mean(jnp.square(g - mu), axis=-1, keepdims=True)
    v = (g - mu) * lax.rsqrt(var + EPS) * ln_w + ln_b
    keep = _tril(SGU_CHUNK)
    bias = mask_r(bs_t, _expand_mat(0))
    causal_ws = [jnp.where(keep, ws, 0.0) for ws in (ws0, ws1, ws2, ws3)]
    mixed = []
    for ci in range(pu.shape[0] // SGU_CHUNK):
        v_c = v[ci * SGU_CHUNK:(ci + 1) * SGU_CHUNK]
        m_c = bias
        for h in range(HEADS):
            m_c = m_c + _head_mask(h) * bdot(causal_ws[h], v_c, 1, 0)
        mixed.append(m_c)
    return u * jnp.concatenate(mixed, axis=0)


def _sc_chunk(pb, pc, ph, halo_c, halo_h, cw):
    return pb * _causal_conv(pc * ph, halo_c * halo_h, cw, 3)


def _neumann_inverses(lows):
    n = lows[0].shape[0]
    eye = (_iota((n, n), 0) == _iota((n, n), 1)).astype(F32)
    a = [-low for low in lows]
    t = [eye + x for x in a]
    for _ in range(5):
        a = [ddot(x, x) for x in a]
        t = [ti + ddot(ti, ai) for ti, ai in zip(t, a)]
    return t


@jax.custom_vjp
def _saved_inverse(low, inv):
    return inv


def _saved_inverse_fwd(low, inv):
    return inv, inv


def _saved_inverse_bwd(inv, g):
    return -ddot(ddot(inv, g, 0, 0), inv, 1, 1), jnp.zeros_like(inv)


_saved_inverse.defvjp(_saved_inverse_fwd, _saved_inverse_bwd)


def _chunk_tril(rows):
    r, c = _iota((rows, rows), 0), _iota((rows, rows), 1)
    return ((r >> 6) == (c >> 6)) & (r >= c)


def _dn_block(pq, pk, pv, hq, hk, hv, small, pz, cwq, cwk, cwv, a_log, dt_bias, nw, state, saved_inv=None):
    c = SCAN_CHUNK
    rows = pq.shape[0]
    bd = _block_diag_mask()
    q = _silu(_causal_conv(pq, hq, cwq, 4))
    k = _silu(_causal_conv(pk, hk, cwk, 4))
    v = _silu(_causal_conv(pv, hv, cwv, 4))
    q = q * lax.rsqrt(_head_sum(q * q, bd) + EPS) * (HEAD_DIM ** -0.5)
    k = k * lax.rsqrt(_head_sum(k * k, bd) + EPS)
    lane = _iota((1, 128), 1)
    g = jnp.where(lane < HEADS, -jnp.exp(a_log) * _softplus(small + dt_bias), 0.0)
    beta_b = mask_r(jax.nn.sigmoid(small), _expand_mat(HEADS))
    gc_all = mask_l(_chunk_tril(rows).astype(F32), g)
    gcb_all = mask_r(gc_all, _expand_mat(0))
    kb_all = k * beta_b
    vb_all = v * beta_b
    kbe_all = kb_all * jnp.exp(gcb_all)
    qg_all = q * jnp.exp(gcb_all)
    causal, strict = _tril(c), _tril(c, strict=True)
    nc = rows // c
    pairs = [(ci, h) for ci in range(nc) for h in range(HEADS)]
    sls = [slice(ci * c, (ci + 1) * c) for ci in range(nc)]
    decays, lows, attns = [], [], []
    for ci, h in pairs:
        gc = gc_all[sls[ci]]
        onehot = (_iota((c, 128), 1) == h).astype(F32)
        col = mask_l(onehot, gc, 1, 1)
        row = jnp.sum(gc * onehot, axis=1, keepdims=True)
        decays.append(jnp.exp(jnp.where(causal, row - col, -jnp.inf)))
    for j, (ci, h) in enumerate(pairs):
        mh = _head_mask(h)
        k_c = k[sls[ci]]
        lows.append(jnp.where(strict, bdot(kb_all[sls[ci]] * mh, k_c, 1, 1) * decays[j], 0.0))
        attns.append(bdot(q[sls[ci]] * mh, k_c, 1, 1) * decays[j])
    if saved_inv is None:
        invs = _neumann_inverses(lows)
    else:
        invs = [_saved_inverse(low, s) for low, s in zip(lows, saved_inv)]
    us, ws = [], []
    for ci in range(nc):
        u = jnp.zeros((c, GROUP), F32)
        w = jnp.zeros((c, GROUP), F32)
        for h in range(HEADS):
            mh = _head_mask(h)
            u = u + mh * ddot(invs[ci * HEADS + h], vb_all[sls[ci]])
            w = w + mh * ddot(invs[ci * HEADS + h], kbe_all[sls[ci]])
        us.append(u)
        ws.append(w)
    outs = []
    for ci in range(nc):
        gc_b = gcb_all[sls[ci]]
        gc_last_b = _row_pick(gc_b, c - 1)
        v_new = us[ci] - bdot(ws[ci], state, 1, 0)
        o = bdot(qg_all[sls[ci]], state, 1, 0)
        for h in range(HEADS):
            o = o + _head_mask(h) * bdot(attns[ci * HEADS + h], v_new, 1, 0)
        k_dec = k[sls[ci]] * jnp.exp(gc_last_b - gc_b)
        state = state * jnp.exp(gc_last_b) + bd * bdot(k_dec, v_new, 0, 0)
        outs.append(o)
    o = jnp.concatenate(outs, axis=0)
    return _head_rmsnorm_gate(o, nw, pz, bd), state, invs


def _gla_chunk(pq, pk, pv, small, pz, w2, gbias, nw, state_t):
    c = SCAN_CHUNK
    rows = pq.shape[0]
    nc = rows // c
    sls = [slice(ci * c, (ci + 1) * c) for ci in range(nc)]
    bd = _block_diag_mask()
    log_a = _log_sigmoid(bdot(small, w2, 1, 0) + gbias) * (1.0 / 16.0)
    gcum = mask_l(_chunk_tril(rows).astype(F32), log_a)
    r, s = _iota((rows, rows), 0), _iota((rows, rows), 1)
    base = (r >> 6) << 6
    g_mid = mask_l((s == base + c // 2).astype(F32), gcum)
    g_last = mask_l((s == base + c - 1).astype(F32), gcum)
    q = pq * (HEAD_DIM ** -0.5)
    qa = q * jnp.exp(gcum - g_mid)
    ka = pk * jnp.exp(g_mid - gcum)
    qg = q * jnp.exp(gcum)
    k_last = pk * jnp.exp(g_last - gcum)
    causal = _tril(c)
    attns = [jnp.where(causal, bdot(qa[sls[ci]] * _head_mask(h), ka[sls[ci]], 1, 1), 0.0)
             for ci in range(nc) for h in range(HEADS)]
    intra = []
    for ci in range(nc):
        o = jnp.zeros((c, GROUP), F32)
        for h in range(HEADS):
            o = o + _head_mask(h) * bdot(attns[ci * HEADS + h], pv[sls[ci]], 1, 0)
        intra.append(o)
    kvs = [bd * bdot(pv[sls[ci]], k_last[sls[ci]], 0, 0) for ci in range(nc)]
    states = []
    for ci in range(nc):
        states.append(state_t)
        state_t = state_t * jnp.exp(_row_pick(g_last[sls[ci]], 0)) + kvs[ci]
    outs = [intra[ci] + bdot(qg[sls[ci]], states[ci], 1, 1) for ci in range(nc)]
    o = jnp.concatenate(outs, axis=0)
    return _head_rmsnorm_gate(o, nw, pz, bd), state_t


def _col_spec(rows, group, rev_n=None):
    if rev_n is None:
        return pl.BlockSpec((rows, GROUP), lambda i: (i, group))
    return pl.BlockSpec((rows, GROUP), lambda i: (rev_n - 1 - i, group))


def _small_spec(rows, group128, rev_n=None):
    if rev_n is None:
        return pl.BlockSpec((rows, 128), lambda i: (i, group128))
    return pl.BlockSpec((rows, 128), lambda i: (rev_n - 1 - i, group128))


def _halo_spec(rows, group, rev_n=None):
    per = rows // HALO
    if rev_n is None:
        return pl.BlockSpec((HALO, GROUP), lambda i: (jnp.maximum(i * per - 1, 0), group))
    return pl.BlockSpec((HALO, GROUP), lambda i: (jnp.maximum((rev_n - 1 - i) * per - 1, 0), group))


def _full_spec(shape):
    nd = len(shape)
    return pl.BlockSpec(shape, lambda i: (0,) * nd)


def _out_rows_spec(rows, lanes, rev_n=None):
    if rev_n is None:
        return pl.BlockSpec((rows, lanes), lambda i: (i, 0))
    return pl.BlockSpec((rows, lanes), lambda i: (rev_n - 1 - i, 0))


SGU_ROWS = 4 * SGU_CHUNK


def _sgu_fwd(p, ln_w, ln_b, ws, bs_t):
    t = p.shape[0]
    n = t // SGU_ROWS

    def body(pu_ref, pv_ref, lw_ref, lb_ref, ws_ref, bs_ref, y_ref):
        y = _sgu_chunk(pu_ref[...], pv_ref[...], lw_ref[...], lb_ref[...],
                       ws_ref[0], ws_ref[1], ws_ref[2], ws_ref[3], bs_ref[...])
        y_ref[...] = y.astype(y_ref.dtype)

    return pl.pallas_call(
        body, name="sgu_fwd", grid=(n,),
        in_specs=[_col_spec(SGU_ROWS, COL_AU), _col_spec(SGU_ROWS, COL_AV), _full_spec((1, GROUP)),
                  _full_spec((1, GROUP)), _full_spec((HEADS, SGU_CHUNK, SGU_CHUNK)), _full_spec((SGU_CHUNK, 128))],
        out_specs=_out_rows_spec(SGU_ROWS, GROUP),
        out_shape=jax.ShapeDtypeStruct((t, GROUP), BF16),
        compiler_params=_cparams(("arbitrary",)),
    )(p, p, ln_w, ln_b, ws, bs_t)


def _sgu_bwd(p, dmix, ln_w, ln_b, ws, bs_t):
    t = p.shape[0]
    n = t // SGU_ROWS

    def body(pu_ref, pv_ref, dy_ref, lw_ref, lb_ref, ws_ref, bs_ref,
             dpu_ref, dpv_ref, dlw_ref, dlb_ref, dws_ref, dbs_ref):
        args = (pu_ref[...], pv_ref[...], lw_ref[...], lb_ref[...],
                ws_ref[0], ws_ref[1], ws_ref[2], ws_ref[3], bs_ref[...])
        _, vjp = jax.vjp(_sgu_chunk, *args)
        dpu, dpv, dlw, dlb, d0, d1, d2, d3, dbs = vjp(dy_ref[...])
        dpu_ref[...] = dpu.astype(dpu_ref.dtype)
        dpv_ref[...] = dpv.astype(dpv_ref.dtype)

        @pl.when(pl.program_id(0) == 0)
        def _():
            dlw_ref[...] = jnp.zeros_like(dlw_ref)
            dlb_ref[...] = jnp.zeros_like(dlb_ref)
            dws_ref[...] = jnp.zeros_like(dws_ref)
            dbs_ref[...] = jnp.zeros_like(dbs_ref)

        dlw_ref[...] += dlw
        dlb_ref[...] += dlb
        for h, d in enumerate((d0, d1, d2, d3)):
            dws_ref[h] += d
        dbs_ref[...] += dbs

    return pl.pallas_call(
        body, name="sgu_bwd", grid=(n,),
        in_specs=[_col_spec(SGU_ROWS, COL_AU), _col_spec(SGU_ROWS, COL_AV),
                  pl.BlockSpec((SGU_ROWS, GROUP), lambda i: (i, 0)),
                  _full_spec((1, GROUP)), _full_spec((1, GROUP)), _full_spec((HEADS, SGU_CHUNK, SGU_CHUNK)),
                  _full_spec((SGU_CHUNK, 128))],
        out_specs=[_out_rows_spec(SGU_ROWS, GROUP), _out_rows_spec(SGU_ROWS, GROUP), _full_spec((1, GROUP)),
                   _full_spec((1, GROUP)), _full_spec((HEADS, SGU_CHUNK, SGU_CHUNK)), _full_spec((SGU_CHUNK, 128))],
        out_shape=[jax.ShapeDtypeStruct((t, GROUP), BF16), jax.ShapeDtypeStruct((t, GROUP), BF16),
                   jax.ShapeDtypeStruct((1, GROUP), F32), jax.ShapeDtypeStruct((1, GROUP), F32),
                   jax.ShapeDtypeStruct((HEADS, SGU_CHUNK, SGU_CHUNK), F32),
                   jax.ShapeDtypeStruct((SGU_CHUNK, 128), F32)],
        compiler_params=_cparams(("arbitrary",)),
    )(p, p, dmix, ln_w, ln_b, ws, bs_t)


SC_ROWS = 512


def _first_block_zero(halo, first):
    return jnp.where(first, 0.0, halo)


def _sc_fwd(p, cw):
    t = p.shape[0]
    n = t // SC_ROWS

    def body(pb_ref, pc_ref, ph_ref, hc_ref, hh_ref, cw_ref, y_ref):
        first = pl.program_id(0) == 0
        y = _sc_chunk(pb_ref[...], pc_ref[...], ph_ref[...], _first_block_zero(hc_ref[...], first),
                      _first_block_zero(hh_ref[...], first), cw_ref[...])
        y_ref[...] = y.astype(y_ref.dtype)

    return pl.pallas_call(
        body, name="sc_fwd", grid=(n,),
        in_specs=[_col_spec(SC_ROWS, COL_BB), _col_spec(SC_ROWS, COL_BC), _col_spec(SC_ROWS, COL_BH),
                  _halo_spec(SC_ROWS, COL_BC), _halo_spec(SC_ROWS, COL_BH), _full_spec((HALO, GROUP))],
        out_specs=_out_rows_spec(SC_ROWS, GROUP),
        out_shape=jax.ShapeDtypeStruct((t, GROUP), BF16),
        compiler_params=_cparams(("arbitrary",)),
    )(p, p, p, p, p, cw)


def _add_halo_grad(d, carry):
    return d + jnp.concatenate([jnp.zeros((d.shape[0] - HALO, d.shape[1]), d.dtype), carry], axis=0)


def _sc_bwd(p, dmix, cw):
    t = p.shape[0]
    n = t // SC_ROWS

    def body(pb_ref, pc_ref, ph_ref, hc_ref, hh_ref, dy_ref, cw_ref,
             dpb_ref, dpc_ref, dph_ref, dcw_ref, carry_c, carry_h):
        i = pl.program_id(0)
        first = i == n - 1

        @pl.when(i == 0)
        def _():
            carry_c[...] = jnp.zeros_like(carry_c)
            carry_h[...] = jnp.zeros_like(carry_h)
            dcw_ref[...] = jnp.zeros_like(dcw_ref)

        args = (pb_ref[...], pc_ref[...], ph_ref[...], _first_block_zero(hc_ref[...], first),
                _first_block_zero(hh_ref[...], first), cw_ref[...])
        _, vjp = jax.vjp(_sc_chunk, *args)
        dpb, dpc, dph, dhc, dhh, dcw = vjp(dy_ref[...])
        dpb_ref[...] = dpb.astype(dpb_ref.dtype)
        dpc_ref[...] = _add_halo_grad(dpc, carry_c[...]).astype(dpc_ref.dtype)
        dph_ref[...] = _add_halo_grad(dph, carry_h[...]).astype(dph_ref.dtype)
        carry_c[...] = dhc
        carry_h[...] = dhh
        dcw_ref[...] += dcw

    return pl.pallas_call(
        body, name="sc_bwd", grid=(n,),
        in_specs=[_col_spec(SC_ROWS, COL_BB, n), _col_spec(SC_ROWS, COL_BC, n), _col_spec(SC_ROWS, COL_BH, n),
                  _halo_spec(SC_ROWS, COL_BC, n), _halo_spec(SC_ROWS, COL_BH, n),
                  pl.BlockSpec((SC_ROWS, GROUP), lambda i: (n - 1 - i, 1)), _full_spec((HALO, GROUP))],
        out_specs=[_out_rows_spec(SC_ROWS, GROUP, n)] * 3 + [_full_spec((HALO, GROUP))],
        out_shape=[jax.ShapeDtypeStruct((t, GROUP), BF16)] * 3 + [jax.ShapeDtypeStruct((HALO, GROUP), F32)],
        scratch_shapes=[pltpu.VMEM((HALO, GROUP), F32), pltpu.VMEM((HALO, GROUP), F32)],
        compiler_params=_cparams(("arbitrary",)),
    )(p, p, p, p, p, dmix, cw)


SCAN_STEP_CHUNKS = 4
SCAN_ROWS = SCAN_STEP_CHUNKS * SCAN_CHUNK


def _host_call(body, hosted, *, name, grid, in_specs, out_specs, out_shape, scratch_shapes, args):
    params = _cparams(("arbitrary",))
    if hosted is None:
        outs = pl.pallas_call(body, name=name, grid=grid, in_specs=in_specs, out_specs=out_specs,
                              out_shape=out_shape, scratch_shapes=scratch_shapes, compiler_params=params)(*args)
        return outs, None
    stage, arrays, modes, bufs = hosted
    arrays = list(arrays) if stage == 1 else []
    n_in, n_out, n_scr, n_src, na = len(in_specs), len(out_specs), len(scratch_shapes), len(arrays), len(bufs)
    last = grid[0] - 1

    def new_body(*refs):
        srcs = refs[n_in:n_in + n_src]
        o0 = n_in + n_src + na
        ex = refs[o0 + n_out:o0 + n_out + na]
        s0 = o0 + n_out + na
        sems = refs[s0 + n_scr:]
        i = pl.program_id(0)

        def run(what):
            if stage == 1:
                _exchange_stage1(srcs, ex, modes, sems[0], sems[1], what)
            else:
                _exchange_stage2(ex, sems[0], sems[1], what)

        @pl.when(i == 0)
        def _():
            run("start")

        body(*refs[:n_in], *refs[o0:o0 + n_out], *refs[s0:s0 + n_scr])

        @pl.when(i == last)
        def _():
            run("wait")

    any_spec = pl.BlockSpec(memory_space=pl.ANY)
    outs = pl.pallas_call(
        new_body, name=name, grid=grid,
        in_specs=list(in_specs) + [any_spec] * (n_src + na), out_specs=list(out_specs) + [any_spec] * na,
        out_shape=list(out_shape) + [jax.ShapeDtypeStruct(b.shape, b.dtype) for b in bufs],
        input_output_aliases={n_in + n_src + a: n_out + a for a in range(na)},
        scratch_shapes=list(scratch_shapes) + (_stage1_sems(na) if stage == 1 else _stage2_sems(na)),
        compiler_params=params,
    )(*args, *arrays, *bufs)
    return outs[:n_out], outs[n_out:]


def _dn_fwd(p, cw3, a_log, dt_bias, nw, hosted=None):
    t = p.shape[0]
    r = SCAN_ROWS
    n = t // r

    def body(pq_ref, pk_ref, pv_ref, hq_ref, hk_ref, hv_ref, sm_ref, pz_ref, cw_ref, al_ref, dt_ref, nw_ref,
             y_ref, ck_ref, inv_ref, state):
        first = pl.program_id(0) == 0

        @pl.when(first)
        def _():
            state[...] = jnp.zeros_like(state)

        s_in = state[...]
        ck_ref[0] = s_in
        y, s_out, invs = _dn_block(pq_ref[...], pk_ref[...], pv_ref[...], _first_block_zero(hq_ref[...], first),
                                   _first_block_zero(hk_ref[...], first), _first_block_zero(hv_ref[...], first),
                                   sm_ref[...], pz_ref[...], cw_ref[0], cw_ref[1], cw_ref[2],
                                   al_ref[...], dt_ref[...], nw_ref[...], s_in)
        y_ref[...] = y.astype(y_ref.dtype)
        state[...] = s_out
        for j, inv in enumerate(invs):
            inv_ref[j] = inv

    nh = SCAN_STEP_CHUNKS * HEADS
    return _host_call(
        body, hosted, name="dn_fwd", grid=(n,),
        in_specs=[_col_spec(r, COL_CQ), _col_spec(r, COL_CK), _col_spec(r, COL_CV),
                  _halo_spec(r, COL_CQ), _halo_spec(r, COL_CK), _halo_spec(r, COL_CV),
                  _small_spec(r, COL128_SMALL_C), _col_spec(r, COL_CZ), _full_spec((3, HALO, GROUP)),
                  _full_spec((1, 128)), _full_spec((1, 128)), _full_spec((1, GROUP))],
        out_specs=[_out_rows_spec(r, GROUP), pl.BlockSpec((1, GROUP, GROUP), lambda i: (i, 0, 0)),
                   pl.BlockSpec((nh, SCAN_CHUNK, SCAN_CHUNK), lambda i: (i, 0, 0))],
        out_shape=[jax.ShapeDtypeStruct((t, GROUP), BF16), jax.ShapeDtypeStruct((n, GROUP, GROUP), F32),
                   jax.ShapeDtypeStruct((n * nh, SCAN_CHUNK, SCAN_CHUNK), F32)],
        scratch_shapes=[pltpu.VMEM((GROUP, GROUP), F32)],
        args=(p, p, p, p, p, p, p, p, cw3, a_log, dt_bias, nw))


def _dn_bwd(p, dmix, states, invs, cw3, a_log, dt_bias, nw, hosted=None):
    t = p.shape[0]
    c = SCAN_ROWS
    n = t // c
    nh = SCAN_STEP_CHUNKS * HEADS

    def body(pq_ref, pk_ref, pv_ref, hq_ref, hk_ref, hv_ref, sm_ref, pz_ref, dy_ref, ck_ref, inv_ref,
             cw_ref, al_ref, dt_ref, nw_ref,
             dpq_ref, dpk_ref, dpv_ref, dsm_ref, dpz_ref, dcw_ref, dal_ref, ddt_ref, dnw_ref,
             dstate, carry):
        i = pl.program_id(0)
        first = i == n - 1

        @pl.when(i == 0)
        def _():
            dstate[...] = jnp.zeros_like(dstate)
            carry[...] = jnp.zeros_like(carry)
            dcw_ref[...] = jnp.zeros_like(dcw_ref)
            dal_ref[...] = jnp.zeros_like(dal_ref)
            ddt_ref[...] = jnp.zeros_like(ddt_ref)
            dnw_ref[...] = jnp.zeros_like(dnw_ref)

        args = (pq_ref[...], pk_ref[...], pv_ref[...], _first_block_zero(hq_ref[...], first),
                _first_block_zero(hk_ref[...], first), _first_block_zero(hv_ref[...], first),
                sm_ref[...], pz_ref[...], cw_ref[0], cw_ref[1], cw_ref[2],
                al_ref[...], dt_ref[...], nw_ref[...], ck_ref[0])
        saved = [inv_ref[j] for j in range(nh)]
        _, vjp = jax.vjp(lambda *a: _dn_block(*a, saved_inv=saved)[:2], *args)
        (dpq, dpk, dpv, dhq, dhk, dhv, dsm, dpz, dcq, dck, dcv, dal, ddt, dnw, dst) = vjp(
            (dy_ref[...], dstate[...]))
        dpq_ref[...] = _add_halo_grad(dpq, carry[0]).astype(dpq_ref.dtype)
        dpk_ref[...] = _add_halo_grad(dpk, carry[1]).astype(dpk_ref.dtype)
        dpv_ref[...] = _add_halo_grad(dpv, carry[2]).astype(dpv_ref.dtype)
        dsm_ref[...] = dsm.astype(dsm_ref.dtype)
        dpz_ref[...] = dpz.astype(dpz_ref.dtype)
        carry[0] = dhq
        carry[1] = dhk
        carry[2] = dhv
        dstate[...] = dst
        dcw_ref[0] += dcq
        dcw_ref[1] += dck
        dcw_ref[2] += dcv
        dal_ref[...] += dal
        ddt_ref[...] += ddt
        dnw_ref[...] += dnw

    return _host_call(
        body, hosted, name="dn_bwd", grid=(n,),
        in_specs=[_col_spec(c, COL_CQ, n), _col_spec(c, COL_CK, n), _col_spec(c, COL_CV, n),
                  _halo_spec(c, COL_CQ, n), _halo_spec(c, COL_CK, n), _halo_spec(c, COL_CV, n),
                  _small_spec(c, COL128_SMALL_C, n), _col_spec(c, COL_CZ, n),
                  pl.BlockSpec((c, GROUP), lambda i: (n - 1 - i, 2)),
                  pl.BlockSpec((1, GROUP, GROUP), lambda i: (n - 1 - i, 0, 0)),
                  pl.BlockSpec((nh, SCAN_CHUNK, SCAN_CHUNK), lambda i: (n - 1 - i, 0, 0)),
                  _full_spec((3, HALO, GROUP)), _full_spec((1, 128)), _full_spec((1, 128)), _full_spec((1, GROUP))],
        out_specs=[_out_rows_spec(c, GROUP, n)] * 3 + [_out_rows_spec(c, 128, n), _out_rows_spec(c, GROUP, n),
                   _full_spec((3, HALO, GROUP)), _full_spec((1, 128)), _full_spec((1, 128)), _full_spec((1, GROUP))],
        out_shape=[jax.ShapeDtypeStruct((t, GROUP), BF16)] * 3 + [
            jax.ShapeDtypeStruct((t, 128), BF16), jax.ShapeDtypeStruct((t, GROUP), BF16),
            jax.ShapeDtypeStruct((3, HALO, GROUP), F32), jax.ShapeDtypeStruct((1, 128), F32),
            jax.ShapeDtypeStruct((1, 128), F32), jax.ShapeDtypeStruct((1, GROUP), F32)],
        scratch_shapes=[pltpu.VMEM((GROUP, GROUP), F32), pltpu.VMEM((3, HALO, GROUP), F32)],
        args=(p, p, p, p, p, p, p, p, dmix, states, invs, cw3, a_log, dt_bias, nw))


def _gla_fwd(p, w2, gbias, nw, hosted=None):
    t = p.shape[0]
    c = SCAN_ROWS
    n = t // c

    def body(pq_ref, pk_ref, pv_ref, sm_ref, pz_ref, w2_ref, gb_ref, nw_ref, y_ref, ck_ref, state):
        @pl.when(pl.program_id(0) == 0)
        def _():
            state[...] = jnp.zeros_like(state)

        s_in = state[...]
        ck_ref[0] = s_in
        y, s_out = _gla_chunk(pq_ref[...], pk_ref[...], pv_ref[...], sm_ref[...], pz_ref[...],
                              w2_ref[...], gb_ref[...], nw_ref[...], s_in)
        y_ref[...] = y.astype(y_ref.dtype)
        state[...] = s_out

    return _host_call(
        body, hosted, name="gla_fwd", grid=(n,),
        in_specs=[_col_spec(c, COL_DQ), _col_spec(c, COL_DK), _col_spec(c, COL_DV),
                  _small_spec(c, COL128_SMALL_D), _col_spec(c, COL_DZ),
                  _full_spec((128, GROUP)), _full_spec((1, GROUP)), _full_spec((1, GROUP))],
        out_specs=[_out_rows_spec(c, GROUP), pl.BlockSpec((1, GROUP, GROUP), lambda i: (i, 0, 0))],
        out_shape=[jax.ShapeDtypeStruct((t, GROUP), BF16), jax.ShapeDtypeStruct((n, GROUP, GROUP), F32)],
        scratch_shapes=[pltpu.VMEM((GROUP, GROUP), F32)],
        args=(p, p, p, p, p, w2, gbias, nw))


def _gla_bwd(p, dmix, states, w2, gbias, nw, hosted=None):
    t = p.shape[0]
    c = SCAN_ROWS
    n = t // c

    def body(pq_ref, pk_ref, pv_ref, sm_ref, pz_ref, dy_ref, ck_ref, w2_ref, gb_ref, nw_ref,
             dpq_ref, dpk_ref, dpv_ref, dsm_ref, dpz_ref, dw2_ref, dgb_ref, dnw_ref, dstate):
        @pl.when(pl.program_id(0) == 0)
        def _():
            dstate[...] = jnp.zeros_like(dstate)
            dw2_ref[...] = jnp.zeros_like(dw2_ref)
            dgb_ref[...] = jnp.zeros_like(dgb_ref)
            dnw_ref[...] = jnp.zeros_like(dnw_ref)

        args = (pq_ref[...], pk_ref[...], pv_ref[...], sm_ref[...], pz_ref[...],
                w2_ref[...], gb_ref[...], nw_ref[...], ck_ref[0])
        _, vjp = jax.vjp(_gla_chunk, *args)
        dpq, dpk, dpv, dsm, dpz, dw2, dgb, dnw, dst = vjp((dy_ref[...], dstate[...]))
        dpq_ref[...] = dpq.astype(dpq_ref.dtype)
        dpk_ref[...] = dpk.astype(dpk_ref.dtype)
        dpv_ref[...] = dpv.astype(dpv_ref.dtype)
        dsm_ref[...] = dsm.astype(dsm_ref.dtype)
        dpz_ref[...] = dpz.astype(dpz_ref.dtype)
        dstate[...] = dst
        dw2_ref[...] += dw2
        dgb_ref[...] += dgb
        dnw_ref[...] += dnw

    return _host_call(
        body, hosted, name="gla_bwd", grid=(n,),
        in_specs=[_col_spec(c, COL_DQ, n), _col_spec(c, COL_DK, n), _col_spec(c, COL_DV, n),
                  _small_spec(c, COL128_SMALL_D, n), _col_spec(c, COL_DZ, n),
                  pl.BlockSpec((c, GROUP), lambda i: (n - 1 - i, 3)),
                  pl.BlockSpec((1, GROUP, GROUP), lambda i: (n - 1 - i, 0, 0)),
                  _full_spec((128, GROUP)), _full_spec((1, GROUP)), _full_spec((1, GROUP))],
        out_specs=[_out_rows_spec(c, GROUP, n)] * 3 + [_out_rows_spec(c, 128, n), _out_rows_spec(c, GROUP, n),
                   _full_spec((128, GROUP)), _full_spec((1, GROUP)), _full_spec((1, GROUP))],
        out_shape=[jax.ShapeDtypeStruct((t, GROUP), BF16)] * 3 + [
            jax.ShapeDtypeStruct((t, 128), BF16), jax.ShapeDtypeStruct((t, GROUP), BF16),
            jax.ShapeDtypeStruct((128, GROUP), F32), jax.ShapeDtypeStruct((1, GROUP), F32),
            jax.ShapeDtypeStruct((1, GROUP), F32)],
        scratch_shapes=[pltpu.VMEM((GROUP, GROUP), F32)],
        args=(p, p, p, p, p, dmix, states, w2, gbias, nw))


def _pick_tile(n, pref):
    for cand in pref:
        if n % cand == 0:
            return cand
    return n


MM_TILE_CAP = 1408


def _largest_tile(n, cap):
    best = None
    for mult in range(1, cap // 128 + 1):
        if n % (128 * mult) == 0:
            best = 128 * mult
    return best if best is not None else n


def _half_index(t, per_half, middle):
    half = jnp.where(t >= per_half, 1, 0)
    return half, middle, t - half * per_half


def _matmul(a, b, mode, out_dtype, name, res=None, pieces=None, norm_w=None):
    a_list = list(a) if isinstance(a, (list, tuple)) else [a]
    a_rows, a_cols = a_list[0].shape[0], sum(x.shape[1] for x in a_list)
    if mode == "nn":
        (m, k), n = (a_rows, a_cols), b.shape[1]
    elif mode == "nt":
        (m, k), n = (a_rows, a_cols), b.shape[0]
    else:
        (k, m), n = (a_rows, a_cols), b.shape[-1] * (2 if b.ndim == 3 else 1)
    tm = _largest_tile(m, MM_TILE_CAP)
    tn = _largest_tile(b.shape[-1] if b.ndim == 3 else n, MM_TILE_CAP)
    if pieces == "cols":
        tm, tn = m, n // N_CHIPS
    tk = _largest_tile(k, MM_TILE_CAP)
    if len(a_list) > 1:
        tk, tm = (k, tm) if mode == "nn" else (tk, m)
    nk = k // tk
    na = len(a_list)
    if mode == "nn":
        a_specs = [pl.BlockSpec((tm, tk if na == 1 else x.shape[1]), lambda i, j, kk: (i, kk)) for x in a_list]
        b_spec = pl.BlockSpec((tk, tn), lambda i, j, kk: (kk, j))
        dims = (1, 0)
    elif mode == "nt":
        a_specs = [pl.BlockSpec((tm, tk), lambda i, j, kk: (i, kk))]
        b_spec = pl.BlockSpec((tn, tk), lambda i, j, kk: (j, kk))
        dims = (1, 1)
    else:
        a_specs = [pl.BlockSpec((tk, tm if na == 1 else x.shape[1]), lambda i, j, kk: (kk, i)) for x in a_list]
        if b.ndim == 3:
            njh = b.shape[-1] // tn
            b_spec = pl.BlockSpec((None, tk, tn), lambda i, j, kk: _half_index(j, njh, kk))
        else:
            b_spec = pl.BlockSpec((tk, tn), lambda i, j, kk: (kk, j))
        dims = (0, 0)
    o_spec = pl.BlockSpec((tm, tn), lambda i, j, kk: (i, j))
    o_shape = (m, n)
    if pieces == "cols":
        o_spec = pl.BlockSpec((2, None, tm // 2, tn), lambda i, j, kk: (0, j, 0, 0))
        o_shape = (2, N_CHIPS, tm // 2, tn)
    has_res = res is not None
    has_norm = norm_w is not None

    def body(*refs):
        a_refs, b_ref = refs[:na], refs[na]
        pos = na + 1
        r_ref = refs[pos] if has_res else None
        pos += has_res
        nw_ref = refs[pos] if has_norm else None
        pos += has_norm
        o_ref = refs[pos]
        h_ref = refs[pos + 1] if has_norm else None
        if na == 1:
            a_val = a_refs[0][...].astype(MXU_DTYPE)
        else:
            a_val = jnp.concatenate([r[...].astype(MXU_DTYPE) for r in a_refs], axis=1)
        part = _dg(a_val, b_ref[...].astype(MXU_DTYPE), *dims)

        def finish(out):
            if has_res:
                out = out + r_ref[...]
            if pieces == "cols":
                o_ref[0] = out[:tm // 2].astype(o_ref.dtype)
                o_ref[1] = out[tm // 2:].astype(o_ref.dtype)
            else:
                o_ref[...] = out.astype(o_ref.dtype)
            if has_norm:
                r = lax.rsqrt(jnp.mean(out * out, axis=-1, keepdims=True) + EPS)
                h_ref[...] = (out * r * nw_ref[...]).astype(h_ref.dtype)

        if nk == 1:
            finish(part)
            return
        acc = refs[-1]
        kk = pl.program_id(2)

        @pl.when(kk == 0)
        def _():
            acc[...] = part

        @pl.when(kk > 0)
        def _():
            acc[...] += part

        @pl.when(kk == nk - 1)
        def _():
            finish(acc[...])

    in_specs = a_specs + [b_spec] + ([o_spec] if has_res else [])
    args = (*a_list, b) + ((res,) if has_res else ())
    out_specs, out_shape = o_spec, jax.ShapeDtypeStruct(o_shape, out_dtype)
    if has_norm:
        assert mode == "nn" and tn == n
        in_specs.append(pl.BlockSpec((1, n), lambda i, j, kk: (0, 0)))
        args += (norm_w,)
        out_specs, out_shape = [o_spec, o_spec], [out_shape, jax.ShapeDtypeStruct(o_shape, BF16)]
    return pl.pallas_call(
        body, name=name, grid=(m // tm, n // tn, nk), in_specs=in_specs, out_specs=out_specs,
        out_shape=out_shape,
        scratch_shapes=[pltpu.VMEM((tm, tn), F32)] if nk > 1 else [],
        compiler_params=_cparams(("parallel", "parallel", "arbitrary")),
    )(*args)


def _matmul_nt_norm_bwd(a, b, x, w, dres, name):
    n = b.shape[0]
    m = a.shape[-2]
    tm = _largest_tile(m, 1024)
    if a.ndim == 3:
        kh = a.shape[2]
        k = 2 * kh
        tk = _largest_tile(kh, MM_TILE_CAP)
        nkh = kh // tk
        a_spec = pl.BlockSpec((None, tm, tk), lambda i, kk: _half_index(kk, nkh, i))
    else:
        k = a.shape[1]
        tk = _largest_tile(k, MM_TILE_CAP)
        a_spec = pl.BlockSpec((tm, tk), lambda i, kk: (i, kk))
    nk = k // tk

    def body(a_ref, b_ref, x_ref, w_ref, r_ref, dx_ref, dw_ref, acc):
        i, kk = pl.program_id(0), pl.program_id(1)
        part = _dg(a_ref[...].astype(MXU_DTYPE), b_ref[...].astype(MXU_DTYPE), 1, 1)

        @pl.when(kk == 0)
        def _():
            acc[...] = part

        @pl.when(kk > 0)
        def _():
            acc[...] += part

        @pl.when((i == 0) & (kk == 0))
        def _():
            dw_ref[...] = jnp.zeros_like(dw_ref)

        @pl.when(kk == nk - 1)
        def _():
            g = acc[...]
            xv = x_ref[...]
            r = lax.rsqrt(jnp.mean(xv * xv, axis=-1, keepdims=True) + EPS)
            xhat = xv * r
            dw_ref[...] += jnp.sum(g * xhat, axis=0, keepdims=True)
            gx = g * w_ref[...]
            dx_ref[...] = r_ref[...] + r * (gx - xhat * jnp.mean(gx * xhat, axis=-1, keepdims=True))

    row_spec = pl.BlockSpec((tm, n), lambda i, kk: (i, 0))
    return pl.pallas_call(
        body, name=name, grid=(m // tm, nk),
        in_specs=[a_spec, pl.BlockSpec((n, tk), lambda i, kk: (0, kk)),
                  row_spec, pl.BlockSpec((1, n), lambda i, kk: (0, 0)), row_spec],
        out_specs=[row_spec, pl.BlockSpec((1, n), lambda i, kk: (0, 0))],
        out_shape=[jax.ShapeDtypeStruct((m, n), F32), jax.ShapeDtypeStruct((1, n), F32)],
        scratch_shapes=[pltpu.VMEM((tm, n), F32)],
        compiler_params=_cparams(("arbitrary", "arbitrary")),
    )(a, b, x, w, dres)


NORM_ROWS = 512


def _rmsnorm_fwd(x, w, name):
    t, d = x.shape

    def body(x_ref, w_ref, o_ref):
        xv = x_ref[...]
        r = lax.rsqrt(jnp.mean(xv * xv, axis=-1, keepdims=True) + EPS)
        o_ref[...] = (xv * r * w_ref[...]).astype(o_ref.dtype)

    return pl.pallas_call(
        body, name=name, grid=(t // NORM_ROWS,),
        in_specs=[pl.BlockSpec((NORM_ROWS, d), lambda i: (i, 0)), _full_spec((1, d))],
        out_specs=pl.BlockSpec((NORM_ROWS, d), lambda i: (i, 0)),
        out_shape=jax.ShapeDtypeStruct((t, d), BF16),
        compiler_params=_cparams(("parallel",)),
    )(x, w)


SWIGLU_ROWS = 128


def _ffn_up_swiglu(h, w_gate_up):
    m, k = h.shape
    tm = _largest_tile(m, 512)
    tn = _largest_tile(D_FF, MM_TILE_CAP)
    nj = D_FF // tn

    def body(a_ref, bg_ref, bu_ref, g_ref, u_ref, act_ref):
        a = a_ref[...].astype(MXU_DTYPE)
        gate = _dg(a, bg_ref[...].astype(MXU_DTYPE), 1, 0)
        up = _dg(a, bu_ref[...].astype(MXU_DTYPE), 1, 0)
        g_ref[...] = gate
        u_ref[...] = up
        act_ref[...] = (_silu(gate) * up).astype(act_ref.dtype)

    o_spec = pl.BlockSpec((tm, tn), lambda i, j: (i, j))
    return pl.pallas_call(
        body, name="ffn_up", grid=(m // tm, nj),
        in_specs=[pl.BlockSpec((tm, k), lambda i, j: (i, 0)), pl.BlockSpec((k, tn), lambda i, j: (0, j)),
                  pl.BlockSpec((k, tn), lambda i, j: (0, j + nj))],
        out_specs=[o_spec, o_spec, o_spec],
        out_shape=[jax.ShapeDtypeStruct((m, D_FF), F32), jax.ShapeDtypeStruct((m, D_FF), F32),
                   jax.ShapeDtypeStruct((m, D_FF), BF16)],
        compiler_params=_cparams(("parallel", "parallel")),
    )(h, w_gate_up, w_gate_up)


def _ffn_down_dx_swiglu(dx, w_down, gate, up):
    m, k = dx.shape
    tm = _largest_tile(m, 512)
    tn = _largest_tile(D_FF, MM_TILE_CAP)

    def body(a_ref, b_ref, g_ref, u_ref, o_ref):
        da = _dg(a_ref[...].astype(MXU_DTYPE), b_ref[...].astype(MXU_DTYPE), 1, 1)
        gate = g_ref[...]
        sg = jax.nn.sigmoid(gate)
        o_ref[0] = (da * u_ref[...] * (sg * (1.0 + gate * (1.0 - sg)))).astype(o_ref.dtype)
        o_ref[1] = (da * gate * sg).astype(o_ref.dtype)

    tile = pl.BlockSpec((tm, tn), lambda i, j: (i, j))
    return pl.pallas_call(
        body, name="ffn_down_dx", grid=(m // tm, D_FF // tn),
        in_specs=[pl.BlockSpec((tm, k), lambda i, j: (i, 0)), pl.BlockSpec((tn, k), lambda i, j: (j, 0)), tile, tile],
        out_specs=pl.BlockSpec((2, tm, tn), lambda i, j: (0, i, j)),
        out_shape=jax.ShapeDtypeStruct((2, m, D_FF), MXU_DTYPE),
        compiler_params=_cparams(("parallel", "parallel")),
    )(dx, w_down, gate, up)


def _loss_head(x, w, target):
    t, d = x.shape

    def fwd(xv, wv, tv):
        r = lax.rsqrt(jnp.mean(xv * xv, axis=-1, keepdims=True) + EPS)
        err = xv * r * wv - tv
        return 0.5 * jnp.sum(jnp.mean(err * err, axis=-1, keepdims=True), axis=0, keepdims=True)

    def body(x_ref, w_ref, t_ref, dx_ref, dw_ref, loss_ref):
        @pl.when(pl.program_id(0) == 0)
        def _():
            dw_ref[...] = jnp.zeros_like(dw_ref)
            loss_ref[...] = jnp.zeros_like(loss_ref)

        loss, vjp = jax.vjp(fwd, x_ref[...], w_ref[...], t_ref[...])
        dx, dw, _ = vjp(jnp.ones((1, 1), F32))
        dx_ref[...] = dx
        dw_ref[...] += dw
        loss_ref[...] += jnp.broadcast_to(loss, loss_ref.shape)

    return pl.pallas_call(
        body, name="loss_head", grid=(t // NORM_ROWS,),
        in_specs=[pl.BlockSpec((NORM_ROWS, d), lambda i: (i, 0)), _full_spec((1, d)),
                  pl.BlockSpec((NORM_ROWS, d), lambda i: (i, 0))],
        out_specs=[pl.BlockSpec((NORM_ROWS, d), lambda i: (i, 0)), _full_spec((1, d)), _full_spec((8, 128))],
        out_shape=[jax.ShapeDtypeStruct((t, d), F32), jax.ShapeDtypeStruct((1, d), F32),
                   jax.ShapeDtypeStruct((8, 128), F32)],
        compiler_params=_cparams(("arbitrary",)),
    )(x, w, target)


def _pad_w_in(w):
    z = lambda n: jnp.zeros((w.shape[0], n), w.dtype)
    return jnp.concatenate([w[:, 0:2048], w[:, 2056:2312], w[:, 2312:3080], w[:, 3096:3352],
                            w[:, 2048:2056], z(120), w[:, 3080:3096], z(112)], axis=1)


def _unpad_w_in(wp):
    return jnp.concatenate([wp[:, 0:2048], wp[:, 3328:3336], wp[:, 2048:2304], wp[:, 2304:3072],
                            wp[:, 3456:3472], wp[:, 3072:3328]], axis=1)


def _pad_rows(a, rows):
    return jnp.concatenate([a, jnp.zeros((rows - a.shape[0],) + a.shape[1:], a.dtype)], axis=0)


def _pad_lanes(a, lanes):
    return jnp.concatenate([a, jnp.zeros(a.shape[:-1] + (lanes - a.shape[-1],), a.dtype)], axis=-1)


def _layer_params(l, small):
    dn_cw = small["dn_conv_w"][l]
    return dict(
        ln_w=small["sgu_ln_w"][l][None], ln_b=small["sgu_ln_b"][l][None],
        ws=small["sgu_w_spatial"][l], bs_t=_pad_lanes(small["sgu_b_spatial"][l].T, 128),
        sc_cw=_pad_rows(small["sc_conv_w"][l], HALO),
        dn_cw=jnp.stack([_pad_rows(dn_cw[:, j * GROUP:(j + 1) * GROUP], HALO) for j in range(3)]),
        dn_al=_pad_lanes(small["dn_a_log"][l][None], 128), dn_dt=_pad_lanes(small["dn_dt_bias"][l][None], 128),
        dn_nw=jnp.tile(small["dn_norm_w"][l][None], (1, HEADS)),
        gla_w2=_pad_rows(small["gla_w_gate2"][l], 128), gla_gb=small["gla_gate_bias"][l][None],
        gla_nw=jnp.tile(small["gla_norm_w"][l][None], (1, HEADS)),
    )


def _exchange_piece(name, grad):
    if grad.ndim == 4:
        return grad
    if name == "w_in":
        grad = _unpad_w_in(grad)
    if name in ("w_in", "w_gate_up"):
        r, c4 = grad.shape
        return jnp.transpose(grad.reshape(2, r // 2, N_CHIPS, c4 // N_CHIPS), (0, 2, 1, 3))
    r4, c = grad.shape
    return jnp.transpose(grad.reshape(N_CHIPS, 2, r4 // (2 * N_CHIPS), c), (1, 0, 2, 3))


def _reduce_on_chip(pieces):
    return _pair_add(pieces, _sibling_swap(pieces))


def _local_step(x, target, big, small, late_weights=None, exchange=False, small_extra=None):
    saved = []
    h = x
    h1 = _rmsnorm_fwd(h, small["norm1_w"][0][None], "norm1_fwd")
    for l in range(DEPTH):
        lp = _layer_params(l, small)
        p = _matmul(h1, big["w_in"][l], "nn", F32, "proj_in")
        y_a = _sgu_fwd(p, lp["ln_w"], lp["ln_b"], lp["ws"], lp["bs_t"])
        y_b = _sc_fwd(p, lp["sc_cw"])
        host1 = host2 = None
        if late_weights is not None:
            shards, finish = late_weights[l]
            modes = ["layer"] * len(shards)
            host1 = (1, shards, modes, _exchange_buffers(shards, modes))
        (y_c, st_c, inv_c), ex = _dn_fwd(p, lp["dn_cw"], lp["dn_al"], lp["dn_dt"], lp["dn_nw"], hosted=host1)
        if host1 is not None:
            host2 = (2, None, None, ex)
        (y_d, st_d), ex = _gla_fwd(p, lp["gla_w2"], lp["gla_gb"], lp["gla_nw"], hosted=host2)
        if host2 is not None:
            big = finish(big, ex)
        mix = [y_a, y_b, y_c, y_d]
        x1, h2 = _matmul(mix, big["w_out"][l], "nn", F32, "proj_out", res=h, norm_w=small["norm2_w"][l][None])
        gate, up, act = _ffn_up_swiglu(h2, big["w_gate_up"][l])
        if l + 1 < DEPTH:
            x2, h1_next = _matmul(act, big["w_down"][l], "nn", F32, "ffn_down", res=x1,
                                  norm_w=small["norm1_w"][l + 1][None])
        else:
            x2, h1_next = _matmul(act, big["w_down"][l], "nn", F32, "ffn_down", res=x1), None
        saved.append(dict(x0=h, h1=h1, p=p, st_c=st_c, inv_c=inv_c, st_d=st_d, mix=mix, x1=x1, h2=h2, gate=gate,
                          up=up, act=act, lp=lp))
        h, h1 = x2, h1_next

    dx, d_final, loss = _loss_head(h, small["final_norm_w"][None], target)
    gbig = {k: [None] * DEPTH for k in ("w_in", "w_out", "w_gate_up", "w_down")}
    gs = {k: [None] * DEPTH for k in ("norm1_w", "sgu_ln_w", "sgu_ln_b", "sgu_w_spatial", "sgu_b_spatial", "sc_conv_w",
                                     "dn_conv_w", "dn_a_log", "dn_dt_bias", "dn_norm_w", "gla_w_gate2",
                                     "gla_gate_bias", "gla_norm_w", "norm2_w")}
    carry = []
    contribs = {}
    for l in reversed(range(DEPTH)):
        s = saved[l]
        lp = s["lp"]
        gbig["w_down"][l] = _matmul(s["act"], dx, "tn", GRAD_WIRE_DTYPE, "ffn_down_dw")
        dgu = _ffn_down_dx_swiglu(dx, big["w_down"][l], s["gate"], s["up"])
        gbig["w_gate_up"][l] = _matmul(s["h2"], dgu, "tn", GRAD_WIRE_DTYPE, "ffn_up_dw",
                                       pieces="cols" if exchange else None)
        dx1, gs["norm2_w"][l] = _matmul_nt_norm_bwd(dgu, big["w_gate_up"][l], s["x1"], small["norm2_w"][l][None], dx,
                                                    "ffn_up_dx")
        gbig["w_out"][l] = _matmul(s["mix"], dx1, "tn", GRAD_WIRE_DTYPE, "proj_out_dw")
        dmix = _matmul(dx1, big["w_out"][l], "nt", F32, "proj_out_dx")
        p = s["p"]
        dpu, dpv, g_lw, g_lb, g_ws, g_bs = _sgu_bwd(p, dmix, lp["ln_w"], lp["ln_b"], lp["ws"], lp["bs_t"])
        dpb, dpc, dph, g_sc = _sc_bwd(p, dmix, lp["sc_cw"])
        host1 = host2 = None
        if exchange:
            unit = carry + [(n, l, _exchange_piece(n, gbig[n][l])) for n in ("w_out", "w_gate_up", "w_down")]
            carry = []
            summed = _reduce_on_chip([piece for _, _, piece in unit])
            modes = ["piece"] * len(summed)
            host1 = (1, summed, modes, _exchange_buffers(summed, modes))
        (dcq, dck, dcv, dcs, dcz, g_dcw, g_al, g_dt, g_dnw), ex = _dn_bwd(
            p, dmix, s["st_c"], s["inv_c"], lp["dn_cw"], lp["dn_al"], lp["dn_dt"], lp["dn_nw"], hosted=host1)
        if host1 is not None:
            host2 = (2, None, None, ex)
        (ddq, ddk, ddv, dds, ddz, g_w2, g_gb, g_gnw), ex = _gla_bwd(p, dmix, s["st_d"], lp["gla_w2"], lp["gla_gb"],
                                                                   lp["gla_nw"], hosted=host2)
        if host2 is not None:
            for (n, lay, _), got in zip(unit, ex):
                contribs[(n, lay)] = got
        dp = jnp.concatenate([dpu, dpv, dpb, dpc, dph, dcq, dck, dcv, dcz, ddq, ddk, ddv, ddz, dcs, dds], axis=1)
        gbig["w_in"][l] = _matmul(s["h1"], dp, "tn", GRAD_WIRE_DTYPE, "proj_in_dw")
        dx, gs["norm1_w"][l] = _matmul_nt_norm_bwd(dp, big["w_in"][l], s["x0"], small["norm1_w"][l][None], dx1,
                                                   "proj_in_dx")
        gs["sgu_ln_w"][l], gs["sgu_ln_b"][l] = g_lw[0], g_lb[0]
        gs["sgu_w_spatial"][l] = g_ws
        gs["sgu_b_spatial"][l] = g_bs[:, :HEADS].T
        gs["sc_conv_w"][l] = g_sc[:3]
        gs["dn_conv_w"][l] = jnp.concatenate([g_dcw[0, :4], g_dcw[1, :4], g_dcw[2, :4]], axis=1)
        gs["dn_a_log"][l], gs["dn_dt_bias"][l] = g_al[0, :HEADS], g_dt[0, :HEADS]
        gs["dn_norm_w"][l] = jnp.sum(g_dnw.reshape(HEADS, HEAD_DIM), axis=0)
        gs["gla_w_gate2"][l] = g_w2[:16]
        gs["gla_gate_bias"][l] = g_gb[0]
        gs["gla_norm_w"][l] = jnp.sum(g_gnw.reshape(HEADS, HEAD_DIM), axis=0)
        gs["norm1_w"][l] = gs["norm1_w"][l][0]
        gs["norm2_w"][l] = gs["norm2_w"][l][0]
        if exchange:
            carry = [("w_in", l, _exchange_piece("w_in", gbig["w_in"][l]))]
    gsmall = {k: jnp.stack(v) for k, v in gs.items()}
    gsmall["final_norm_w"] = d_final[0]
    if not exchange:
        return loss, dx, gbig, gsmall
    summed = _reduce_on_chip([piece for _, _, piece in carry])
    last = _chip_exchange(summed + [small_extra(gsmall, loss)], ["piece"] * len(summed) + ["whole"], "exchange_grads")
    for (n, lay, _), got in zip(carry, last):
        contribs[(n, lay)] = got
    return loss, dx, contribs, last[-1]


def _peer_chips(x, y):
    return [(1 - x, y, 2 * (1 - x) + y), (x, 1 - y, 2 * x + 1 - y), (1 - x, 1 - y, 2 * (1 - x) + 1 - y)]


def _chip_exchange(arrays, modes, name):
    na = len(arrays)
    bufs = _exchange_buffers(arrays, modes)

    def body(*refs):
        ins, outs = refs[:na], refs[2 * na:3 * na]
        send1, recv1, send2, recv2 = refs[3 * na:]
        _exchange_stage1(ins, outs, modes, send1, recv1, "start")
        _exchange_stage1(ins, outs, modes, send1, recv1, "wait")
        _exchange_stage2(outs, send2, recv2, "start")
        _exchange_stage2(outs, send2, recv2, "wait")

    any_spec = pl.BlockSpec(memory_space=pl.ANY)
    return pl.pallas_call(
        body, name=name,
        in_specs=[any_spec] * (2 * na), out_specs=[any_spec] * na,
        out_shape=[jax.ShapeDtypeStruct(b.shape, b.dtype) for b in bufs],
        input_output_aliases={na + a: a for a in range(na)},
        scratch_shapes=_stage1_sems(na) + _stage2_sems(na),
    )(*arrays, *bufs)


def _exchange_buffers(arrays, modes):
    c_idx = lax.axis_index("c")
    chip = 2 * lax.axis_index("x") + lax.axis_index("y")
    units = []
    for arr, md in zip(arrays, modes):
        if md == "layer":
            units.append(lax.dynamic_index_in_dim(arr, c_idx, 0, keepdims=False))
        elif md == "piece":
            units.append(lax.dynamic_index_in_dim(arr, chip, 0, keepdims=False))
        else:
            units.append(arr)
    any_spec = pl.BlockSpec(memory_space=pl.ANY)
    bufs = pl.pallas_call(
        lambda *refs: None, name="exchange_alloc", out_specs=[any_spec] * len(units),
        out_shape=[jax.ShapeDtypeStruct((2, N_CHIPS) + u.shape, u.dtype) for u in units],
    )()
    return [lax.dynamic_update_slice(buf, u[None, None], (c_idx, chip) + (0,) * u.ndim) for buf, u in zip(bufs, units)]


def _stage1_sems(na):
    return [pltpu.SemaphoreType.DMA((na, 3)), pltpu.SemaphoreType.DMA((na, 3))]


def _stage2_sems(na):
    return [pltpu.SemaphoreType.DMA((na,)), pltpu.SemaphoreType.DMA((na,))]


def _exchange_stage1(ins, outs, modes, send1, recv1, what):
    x, y, c = lax.axis_index("x"), lax.axis_index("y"), lax.axis_index("c")
    me = 2 * x + y
    for a in range(len(ins)):
        for k, (px, py, pidx) in enumerate(_peer_chips(x, y)):
            if modes[a] == "layer":
                src = ins[a].at[c]
            else:
                src = ins[a].at[pidx] if modes[a] == "piece" else ins[a]
            if what == "start":
                pltpu.make_async_remote_copy(
                    src_ref=src, dst_ref=outs[a].at[c, me], send_sem=send1.at[a, k], recv_sem=recv1.at[a, k],
                    device_id=(px, py, c), device_id_type=MESH).start()
            else:
                cp = pltpu.make_async_remote_copy(
                    src_ref=src, dst_ref=outs[a].at[c, pidx], send_sem=send1.at[a, k], recv_sem=recv1.at[a, k],
                    device_id=(px, py, c), device_id_type=MESH)
                cp.wait_send()
                cp.wait_recv()


def _exchange_stage2(outs, send2, recv2, what):
    x, y, c = lax.axis_index("x"), lax.axis_index("y"), lax.axis_index("c")
    sibling = (x, y, 1 - c)
    for a in range(len(outs)):
        if what == "start":
            pltpu.make_async_remote_copy(
                src_ref=outs[a].at[c], dst_ref=outs[a].at[c], send_sem=send2.at[a], recv_sem=recv2.at[a],
                device_id=sibling, device_id_type=MESH).start()
        else:
            cp = pltpu.make_async_remote_copy(
                src_ref=outs[a].at[c], dst_ref=outs[a].at[1 - c], send_sem=send2.at[a], recv_sem=recv2.at[a],
                device_id=sibling, device_id_type=MESH)
            cp.wait_send()
            cp.wait_recv()


def _sibling_swap(arrays):
    na = len(arrays)

    def body(*refs):
        ins, theirs = refs[:na], refs[na:2 * na]
        send_sems, recv_sems = refs[2 * na:]
        x, y, c = lax.axis_index("x"), lax.axis_index("y"), lax.axis_index("c")
        sibling = (x, y, 1 - c)
        for a in range(na):
            pltpu.make_async_remote_copy(
                src_ref=ins[a].at[1 - c], dst_ref=theirs[a], send_sem=send_sems.at[a], recv_sem=recv_sems.at[a],
                device_id=sibling, device_id_type=MESH).start()
        for a in range(na):
            cp = pltpu.make_async_remote_copy(
                src_ref=ins[a].at[1 - c], dst_ref=theirs[a], send_sem=send_sems.at[a], recv_sem=recv_sems.at[a],
                device_id=sibling, device_id_type=MESH)
            cp.wait_send()
            cp.wait_recv()

    any_spec = pl.BlockSpec(memory_space=pl.ANY)
    return pl.pallas_call(
        body, name="sibling_swap",
        in_specs=[any_spec] * na, out_specs=[any_spec] * na,
        out_shape=[jax.ShapeDtypeStruct(s.shape[1:], s.dtype) for s in arrays],
        scratch_shapes=[pltpu.SemaphoreType.DMA((na,)), pltpu.SemaphoreType.DMA((na,))],
    )(*arrays)


PAIR_ADD_STEPS = 8


def _pair_add(boths, theirs):
    na = len(boths)
    core = lax.axis_index("c").astype(jnp.int32).reshape(1)
    flat_b = [b.reshape(2, b.shape[1] * b.shape[2], b.shape[3]) for b in boths]
    flat_t = [t.reshape(t.shape[0] * t.shape[1], t.shape[2]) for t in theirs]
    rows = [t.shape[0] // PAIR_ADD_STEPS for t in flat_t]

    def body(core_ref, *refs):
        for a in range(na):
            refs[2 * na + a][...] = (refs[a][...].astype(F32) + refs[na + a][...].astype(F32)).astype(
                refs[2 * na + a].dtype)

    own = [pl.BlockSpec((None, r, t.shape[1]), lambda i, core_ref: (core_ref[0], i, 0)) for r, t in zip(rows, flat_t)]
    plain = [pl.BlockSpec((r, t.shape[1]), lambda i, core_ref: (i, 0)) for r, t in zip(rows, flat_t)]
    outs = pl.pallas_call(
        body, name="pair_add",
        grid_spec=pltpu.PrefetchScalarGridSpec(
            num_scalar_prefetch=1, grid=(PAIR_ADD_STEPS,), in_specs=own + plain, out_specs=plain),
        out_shape=[jax.ShapeDtypeStruct(t.shape, t.dtype) for t in flat_t],
        compiler_params=_cparams(("parallel",)),
    )(core, *flat_b, *flat_t)
    return [o.reshape(t.shape) for o, t in zip(outs, theirs)]


def _adamw_math(g, w, m, v):
    m2 = ADAM_B1 * m + (1.0 - ADAM_B1) * g
    v2 = ADAM_B2 * v + (1.0 - ADAM_B2) * (g * g)
    m_hat = m2 / (1.0 - ADAM_B1 ** ADAM_STEP)
    v_hat = v2 / (1.0 - ADAM_B2 ** ADAM_STEP)
    delta = -ADAM_LR * (m_hat / (jnp.sqrt(v_hat) + ADAM_EPS) + ADAM_WD * w)
    return delta, m2, v2


def _adamw_big(contrib, w, m, v, layer, name, prev=None):
    _, r, c = w.shape
    rh = r // 2
    tr = _pick_tile(rh, (256, 176, 128, 64, 8))
    nj = rh // tr
    blk = pl.BlockSpec((1, tr, c), lambda h, j: (layer, h * nj + j, 0))
    n_prev = 0 if prev is None else 4

    def body(*refs):
        g_ref, w_ref, m_ref, v_ref = refs[:4]
        go_ref, d_ref, mo_ref, vo_ref = refs[4 + n_prev:]
        g = g_ref[0, 0].astype(F32)
        for s in range(1, N_CHIPS):
            g = g + g_ref[0, s].astype(F32)
        delta, m2, v2 = _adamw_math(g, w_ref[0], m_ref[0], v_ref[0])
        go_ref[0] = g
        d_ref[0] = delta
        mo_ref[0] = m2
        vo_ref[0] = v2

    any_spec = pl.BlockSpec(memory_space=pl.ANY)
    return pl.pallas_call(
        body, name=name, grid=(2, nj),
        in_specs=[pl.BlockSpec((1, N_CHIPS, tr, c), lambda h, j: (h, 0, j, 0)), blk, blk, blk] + [any_spec] * n_prev,
        out_specs=[blk] * 4, out_shape=[jax.ShapeDtypeStruct(w.shape, F32)] * 4,
        input_output_aliases={4 + a: a for a in range(n_prev)},
        compiler_params=_cparams(("parallel", "parallel")),
    )(contrib, w, m, v, *([] if prev is None else prev))


def _sum_small(contrib):
    rows = contrib.shape[2]

    def body(g_ref, o_ref):
        total = g_ref[0, 0]
        for j in range(1, N_DEV):
            total = total + g_ref[j // N_CHIPS, j % N_CHIPS]
        o_ref[...] = total

    return pl.pallas_call(
        body, name="sum_small", out_shape=jax.ShapeDtypeStruct((rows, 128), F32),
        compiler_params=_cparams(),
    )(contrib)


def _adamw_small(gs, ws, ms, vs):
    n = len(gs)
    as2d = lambda a: a.reshape(1, -1) if a.ndim == 1 else a

    def body(*refs):
        g_refs, w_refs, m_refs, v_refs = refs[:n], refs[n:2 * n], refs[2 * n:3 * n], refs[3 * n:4 * n]
        d_refs, mo_refs, vo_refs = refs[4 * n:5 * n], refs[5 * n:6 * n], refs[6 * n:]
        for j in range(n):
            delta, m2, v2 = _adamw_math(g_refs[j][...], w_refs[j][...], m_refs[j][...], v_refs[j][...])
            d_refs[j][...] = delta
            mo_refs[j][...] = m2
            vo_refs[j][...] = v2

    ins = [as2d(a) for a in (*gs, *ws, *ms, *vs)]
    outs = pl.pallas_call(
        body, name="adamw_small", out_shape=[jax.ShapeDtypeStruct(a.shape, F32) for a in ins[:n]] * 3,
        compiler_params=_cparams(),
    )(*ins)
    back = lambda group: [o.reshape(g.shape) for o, g in zip(group, gs)]
    return back(outs[:n]), back(outs[n:2 * n]), back(outs[2 * n:])


PACK_ALIGN = 8 * 128


def _packed_rows(shape):
    n = 1
    for d in shape:
        n *= d
    return (n + PACK_ALIGN - 1) // PACK_ALIGN * 8


def _pack(arrays):
    parts = []
    for a in arrays:
        flat = a.reshape(-1)
        pad = _packed_rows(a.shape) * 128 - flat.shape[0]
        if pad:
            flat = jnp.concatenate([flat, jnp.zeros((pad,), F32)])
        parts.append(flat.reshape(-1, 128))
    return jnp.concatenate(parts, axis=0)


def _unpack(packed, shapes):
    out, row = [], 0
    for s in shapes:
        rows = _packed_rows(s)
        n = 1
        for d in s:
            n *= d
        out.append(packed[row:row + rows].reshape(-1)[:n].reshape(s))
        row += rows
    return out


SMALL_NAMES = ("norm1_w", "sgu_ln_w", "sgu_ln_b", "sgu_w_spatial", "sgu_b_spatial", "sc_conv_w", "dn_conv_w",
               "dn_a_log", "dn_dt_bias", "dn_norm_w", "gla_w_gate2", "gla_gate_bias", "gla_norm_w", "norm2_w",
               "final_norm_w")
SHARDED_SMALL = ("sc_conv_w", "dn_conv_w", "gla_w_gate2")
BIG_NAMES = ("w_in", "w_out", "w_gate_up", "w_down")
WEIGHT_ORDER = ("norm1_w", "w_in", "sgu_ln_w", "sgu_ln_b", "sgu_w_spatial", "sgu_b_spatial", "sc_conv_w", "dn_conv_w",
                "dn_a_log", "dn_dt_bias", "dn_norm_w", "gla_w_gate2", "gla_gate_bias", "gla_norm_w", "w_out",
                "norm2_w", "w_gate_up", "w_down", "final_norm_w")


def _cols_from_shards(g):
    l, n, r, c = g.shape
    return jnp.transpose(g, (0, 2, 1, 3)).reshape(l, r, n * c)


def kernel(x, norm1_w, w_in, sgu_ln_w, sgu_ln_b, sgu_w_spatial, sgu_b_spatial, sc_conv_w, dn_conv_w, dn_a_log, dn_dt_bias, dn_norm_w, gla_w_gate2, gla_gate_bias, gla_norm_w, w_out, norm2_w, w_gate_up, w_down, final_norm_w, loss_target, m_norm1_w, m_w_in, m_sgu_ln_w, m_sgu_ln_b, m_sgu_w_spatial, m_sgu_b_spatial, m_sc_conv_w, m_dn_conv_w, m_dn_a_log, m_dn_dt_bias, m_dn_norm_w, m_gla_w_gate2, m_gla_gate_bias, m_gla_norm_w, m_w_out, m_norm2_w, m_w_gate_up, m_w_down, m_final_norm_w, v_norm1_w, v_w_in, v_sgu_ln_w, v_sgu_ln_b, v_sgu_w_spatial, v_sgu_b_spatial, v_sc_conv_w, v_dn_conv_w, v_dn_a_log, v_dn_dt_bias, v_dn_norm_w, v_gla_w_gate2, v_gla_gate_bias, v_gla_norm_w, v_w_out, v_norm2_w, v_w_gate_up, v_w_down, v_final_norm_w):
    w = dict(norm1_w=norm1_w, w_in=w_in, sgu_ln_w=sgu_ln_w, sgu_ln_b=sgu_ln_b, sgu_w_spatial=sgu_w_spatial,
             sgu_b_spatial=sgu_b_spatial, sc_conv_w=sc_conv_w, dn_conv_w=dn_conv_w, dn_a_log=dn_a_log,
             dn_dt_bias=dn_dt_bias, dn_norm_w=dn_norm_w, gla_w_gate2=gla_w_gate2, gla_gate_bias=gla_gate_bias,
             gla_norm_w=gla_norm_w, w_out=w_out, norm2_w=norm2_w, w_gate_up=w_gate_up, w_down=w_down,
             final_norm_w=final_norm_w)
    m = dict(norm1_w=m_norm1_w, w_in=m_w_in, sgu_ln_w=m_sgu_ln_w, sgu_ln_b=m_sgu_ln_b, sgu_w_spatial=m_sgu_w_spatial,
             sgu_b_spatial=m_sgu_b_spatial, sc_conv_w=m_sc_conv_w, dn_conv_w=m_dn_conv_w, dn_a_log=m_dn_a_log,
             dn_dt_bias=m_dn_dt_bias, dn_norm_w=m_dn_norm_w, gla_w_gate2=m_gla_w_gate2,
             gla_gate_bias=m_gla_gate_bias, gla_norm_w=m_gla_norm_w, w_out=m_w_out, norm2_w=m_norm2_w,
             w_gate_up=m_w_gate_up, w_down=m_w_down, final_norm_w=m_final_norm_w)
    v = dict(norm1_w=v_norm1_w, w_in=v_w_in, sgu_ln_w=v_sgu_ln_w, sgu_ln_b=v_sgu_ln_b, sgu_w_spatial=v_sgu_w_spatial,
             sgu_b_spatial=v_sgu_b_spatial, sc_conv_w=v_sc_conv_w, dn_conv_w=v_dn_conv_w, dn_a_log=v_dn_a_log,
             dn_dt_bias=v_dn_dt_bias, dn_norm_w=v_dn_norm_w, gla_w_gate2=v_gla_w_gate2,
             gla_gate_bias=v_gla_gate_bias, gla_norm_w=v_gla_norm_w, w_out=v_w_out, norm2_w=v_norm2_w,
             w_gate_up=v_w_gate_up, w_down=v_w_down, final_norm_w=v_final_norm_w)
    chip = 2 * lax.axis_index("x") + lax.axis_index("y")

    w_in_wire = w["w_in"].astype(MXU_DTYPE)
    row_halves = lambda a: a.reshape(2, a.shape[0] // 2, a.shape[1])

    def full_w_in(g):
        return _pad_w_in(jnp.transpose(g, (0, 2, 1, 3)).reshape(D_MODEL, IN_COLS))

    first = [row_halves(w_in_wire[0])] + [w[n] for n in SHARDED_SMALL]
    gathered = _chip_exchange(first, ["layer"] * len(first), "gather_first")
    w_in_0 = full_w_in(gathered[0])
    big = dict(w_in=[w_in_0, None])
    small = {n: w[n] for n in SMALL_NAMES if n not in SHARDED_SMALL}
    for j, n in enumerate(SHARDED_SMALL):
        small[n] = _cols_from_shards(gathered[1 + j])
    wire = {n: w[n].astype(MXU_DTYPE) for n in ("w_out", "w_gate_up", "w_down")}

    def cols_full(g):
        _, n, rh, c = g.shape
        return jnp.transpose(g, (0, 2, 1, 3)).reshape(2 * rh, n * c)

    def rows_full(g):
        _, n, rh, c = g.shape
        return jnp.transpose(g, (1, 0, 2, 3)).reshape(n * 2 * rh, c)

    def finish_0(big, g):
        return dict(w_in=[w_in_0, full_w_in(g[0])], w_out=[rows_full(g[1]), None],
                    w_gate_up=[cols_full(g[2]), None], w_down=[rows_full(g[3]), None])

    def finish_1(big, g):
        return dict(big, w_out=[big["w_out"][0], rows_full(g[0])], w_gate_up=[big["w_gate_up"][0], cols_full(g[1])],
                    w_down=[big["w_down"][0], rows_full(g[2])])

    late = [([row_halves(w_in_wire[1])] + [row_halves(wire[n][0]) for n in ("w_out", "w_gate_up", "w_down")], finish_0),
            ([row_halves(wire[n][1]) for n in ("w_out", "w_gate_up", "w_down")], finish_1)]

    small_shapes = [(DEPTH,) + w[n].shape[1:-1] + (w[n].shape[-1] * (N_CHIPS if n in SHARDED_SMALL else 1),)
                    if n != "final_norm_w" else w[n].shape for n in SMALL_NAMES] + [(1,)]

    def pack_small(gsmall, loss_tile):
        return _pack([gsmall[n] for n in SMALL_NAMES] + [loss_tile[0:1, 0]])

    _, grad_x, contribs, small_contrib = _local_step(
        x[0], loss_target[0], big, small, late_weights=late, exchange=True, small_extra=pack_small)

    out_g, out_d, out_m, out_v = {}, {}, {}, {}
    for j, n in enumerate(BIG_NAMES):
        outs = _adamw_big(contribs[(n, 1)], w[n], m[n], v[n], 1, "adamw_" + n + "_1")
        out_g[n], out_d[n], out_m[n], out_v[n] = _adamw_big(contribs[(n, 0)], w[n], m[n], v[n], 0,
                                                            "adamw_" + n + "_0", prev=outs)
    summed = _unpack(_sum_small(small_contrib), small_shapes)
    loss = summed[-1][0]
    for n, g in zip(SMALL_NAMES, summed[:-1]):
        if n in SHARDED_SMALL:
            cols = g.shape[-1] // N_CHIPS
            g = lax.dynamic_slice_in_dim(g, chip * cols, cols, axis=g.ndim - 1)
        out_g[n] = g
    d_s, m_s, v_s = _adamw_small([out_g[n] for n in SMALL_NAMES], [w[n] for n in SMALL_NAMES],
                                 [m[n] for n in SMALL_NAMES], [v[n] for n in SMALL_NAMES])
    for n, d_, m_, v_ in zip(SMALL_NAMES, d_s, m_s, v_s):
        out_d[n], out_m[n], out_v[n] = d_, m_, v_

    return (loss, grad_x[None], *[out_g[n] for n in WEIGHT_ORDER], *[out_d[n] for n in WEIGHT_ORDER],
            *[out_m[n] for n in WEIGHT_ORDER], *[out_v[n] for n in WEIGHT_ORDER])
```

```python
import functools

import jax
import jax.numpy as jnp
from jax import lax
from jax.experimental import pallas as pl
from jax.experimental.pallas import tpu as pltpu

F32 = jnp.float32
BF16 = jnp.bfloat16
MXU_DTYPE = jnp.bfloat16
GRAD_WIRE_DTYPE = jnp.bfloat16
HI = lax.Precision.HIGHEST
MESH = pl.DeviceIdType.MESH

D_MODEL = 1024
DEPTH = 2
GROUP = 256
HEADS = 4
HEAD_DIM = 64
SGU_CHUNK = 128
SCAN_CHUNK = 64
D_FF = 2816
EPS = 1e-6
IN_COLS = 3352
P_COLS = 3584
HALO = 8
N_CHIPS = 4
N_DEV = 8
VMEM_LIMIT = 56 * 1024 * 1024

ADAM_LR = 0.001
ADAM_B1 = 0.9
ADAM_B2 = 0.999
ADAM_EPS = 1e-08
ADAM_WD = 0.01
ADAM_STEP = 10

(COL_AU, COL_AV, COL_BB, COL_BC, COL_BH, COL_CQ, COL_CK, COL_CV, COL_CZ,
 COL_DQ, COL_DK, COL_DV, COL_DZ) = range(13)
COL128_SMALL_C = 26
COL128_SMALL_D = 27


def _cparams(sem=None):
    return pltpu.CompilerParams(dimension_semantics=sem, vmem_limit_bytes=VMEM_LIMIT)


def _iota(shape, dim):
    return lax.broadcasted_iota(jnp.int32, shape, dim)


def _dg(a, b, ca, cb, prec=None):
    return lax.dot_general(a, b, (((ca,), (cb,)), ((), ())), preferred_element_type=F32, precision=prec)


@functools.partial(jax.custom_vjp, nondiff_argnums=(2, 3))
def bdot(a, b, ca, cb):
    return _dg(a.astype(MXU_DTYPE), b.astype(MXU_DTYPE), ca, cb)


def _bdot_fwd(a, b, ca, cb):
    return bdot(a, b, ca, cb), (a, b)


def _bdot_bwd(ca, cb, res, g):
    a, b = res
    if ca == 1:
        da = bdot(g, b, 1, 1 if cb == 0 else 0)
    else:
        da = bdot(b, g, 1 if cb == 0 else 0, 1)
    if cb == 0:
        db = bdot(a, g, 0, 0) if ca == 1 else bdot(a, g, 1, 0)
    else:
        db = bdot(g, a, 0, 0) if ca == 1 else bdot(g, a, 0, 1)
    return da, db


bdot.defvjp(_bdot_fwd, _bdot_bwd)


def _pieces(a, n):
    out, r = [], a
    for i in range(n):
        p = r.astype(MXU_DTYPE)
        out.append(p)
        if i + 1 < n:
            r = r - p.astype(F32)
    return out


def _mdot_impl(a, b, ca, cb, sa, sb):
    total = None
    for i, x in enumerate(_pieces(a, sa)):
        for j, y in enumerate(_pieces(b, sb)):
            if i + j < max(sa, sb):
                t = _dg(x, y, ca, cb)
                total = t if total is None else total + t
    return total


@functools.partial(jax.custom_vjp, nondiff_argnums=(2, 3, 4, 5))
def mdot(a, b, ca, cb, sa, sb):
    return _mdot_impl(a, b, ca, cb, sa, sb)


def _mdot_fwd(a, b, ca, cb, sa, sb):
    return _mdot_impl(a, b, ca, cb, sa, sb), (a, b)


def _mdot_bwd(ca, cb, sa, sb, res, g):
    a, b = res
    ga, gb = (3 if sb == 1 else 2), (3 if sa == 1 else 2)
    if sa == 1:
        da = jnp.zeros_like(a)
    elif ca == 1:
        da = mdot(g, b, 1, 1 if cb == 0 else 0, ga, sb)
    else:
        da = mdot(b, g, 1 if cb == 0 else 0, 1, sb, ga)
    if sb == 1:
        db = jnp.zeros_like(b)
    elif cb == 0:
        db = mdot(a, g, 0, 0, sa, gb) if ca == 1 else mdot(a, g, 1, 0, sa, gb)
    else:
        db = mdot(g, a, 0, 0, gb, sa) if ca == 1 else mdot(g, a, 0, 1, gb, sa)
    return da, db


mdot.defvjp(_mdot_fwd, _mdot_bwd)


def mask_r(a, m, ca=1, cb=0):
    return mdot(a, m, ca, cb, 3, 1)


def mask_l(m, b, ca=1, cb=0):
    return mdot(m, b, ca, cb, 1, 3)


def ddot(a, b, ca=1, cb=0):
    return mdot(a, b, ca, cb, 2, 2)


def _head_mask(h):
    return ((_iota((1, GROUP), 1) >> 6) == h).astype(F32)


def _block_diag_mask():
    return ((_iota((GROUP, GROUP), 0) >> 6) == (_iota((GROUP, GROUP), 1) >> 6)).astype(F32)


def _expand_mat(offset):
    return ((_iota((128, GROUP), 0) - offset) == (_iota((128, GROUP), 1) >> 6)).astype(F32)


def _tril(n, strict=False):
    r, c = _iota((n, n), 0), _iota((n, n), 1)
    return (r > c) if strict else (r >= c)


def _row_pick(x, row):
    return jnp.sum(jnp.where(_iota(x.shape, 0) == row, x, 0.0), axis=0, keepdims=True)


def _shift_rows_impl(x, halo, j):
    n = x.shape[0]
    r = _iota(x.shape, 0)
    top = jnp.concatenate([pltpu.roll(halo, j, 0), jnp.zeros((n - HALO, x.shape[1]), x.dtype)], axis=0)
    return jnp.where(r >= j, pltpu.roll(x, j, 0), top)


def _mxu_round(a):
    return a.astype(MXU_DTYPE).astype(F32)


@functools.partial(jax.custom_vjp, nondiff_argnums=(3,))
def _causal_conv(x, halo, w, width):
    xb, hb, wb = _mxu_round(x), _mxu_round(halo), _mxu_round(w)
    out = xb * _row_pick(wb, width - 1)
    for j in range(1, width):
        out = out + _shift_rows_impl(xb, hb, j) * _row_pick(wb, width - 1 - j)
    return out


def _causal_conv_fwd(x, halo, w, width):
    return _causal_conv(x, halo, w, width), (x, halo, w)


def _causal_conv_bwd(width, res, g):
    x, halo, w = res
    xb, hb, wb, gb = _mxu_round(x), _mxu_round(halo), _mxu_round(w), _mxu_round(g)
    n = g.shape[0]
    rows, rows8 = _iota(g.shape, 0), _iota(halo.shape, 0)
    dx = gb * _row_pick(wb, width - 1)
    dh = jnp.zeros_like(halo)
    dw = jnp.where(rows8 == width - 1, jnp.sum(xb * gb, axis=0, keepdims=True), 0.0)
    for j in range(1, width):
        gj = gb * _row_pick(wb, width - 1 - j)
        dx = dx + jnp.where(rows < n - j, pltpu.roll(gj, n - j, 0), 0.0)
        dh = dh + jnp.where(rows8 >= HALO - j, pltpu.roll(gj[0:HALO], HALO - j, 0), 0.0)
        tap = jnp.sum(_shift_rows_impl(xb, hb, j) * gb, axis=0, keepdims=True)
        dw = dw + jnp.where(rows8 == width - 1 - j, tap, 0.0)
    return dx, dh, dw


_causal_conv.defvjp(_causal_conv_fwd, _causal_conv_bwd)


def _head_sum(x, bd):
    return mask_r(x, bd)


def _softplus(x):
    return jnp.maximum(x, 0.0) + jnp.log1p(jnp.exp(-jnp.abs(x)))


def _log_sigmoid(x):
    return -_softplus(-x)


def _silu(x):
    return x * jax.nn.sigmoid(x)


def _head_rmsnorm_gate(o, nw, z, bd):
    ms = _head_sum(o * o, bd) * (1.0 / HEAD_DIM)
    return o * lax.rsqrt(ms + EPS) * nw * _silu(z)


def _sgu_chunk(pu, pv, ln_w, ln_b, ws0, ws1, ws2, ws3, bs_t):
    u = jax.nn.gelu(pu)
    g = jax.nn.gelu(pv)
    mu = jnp.mean(g, axis=-1, keepdims=True)
    var = jnp.mean(jnp.square(g - mu), axis=-1, keepdims=True)
    v = (g - mu) * lax.rsqrt(var + EPS) * ln_w + ln_b
    keep = _tril(SGU_CHUNK)
    bias = mask_r(bs_t, _expand_mat(0))
    causal_ws = [jnp.where(keep, ws, 0.0) for ws in (ws0, ws1, ws2, ws3)]
    mixed = []
    for ci in range(pu.shape[0] // SGU_CHUNK):
        v_c = v[ci * SGU_CHUNK:(ci + 1) * SGU_CHUNK]
        m_c = bias
        for h in range(HEADS):
            m_c = m_c + _head_mask(h) * bdot(causal_ws[h], v_c, 1, 0)
        mixed.append(m_c)
    return u * jnp.concatenate(mixed, axis=0)


def _sc_chunk(pb, pc, ph, halo_c, halo_h, cw):
    return pb * _causal_conv(pc * ph, halo_c * halo_h, cw, 3)


def _neumann_inverses(lows):
    n = lows[0].shape[0]
    eye = (_iota((n, n), 0) == _iota((n, n), 1)).astype(F32)
    a = [-low for low in lows]
    t = [eye + x for x in a]
    for _ in range(5):
        a = [ddot(x, x) for x in a]
        t = [ti + ddot(ti, ai) for ti, ai in zip(t, a)]
    return t


@jax.custom_vjp
def _saved_inverse(low, inv):
    return inv


def _saved_inverse_fwd(low, inv):
    return inv, inv


def _saved_inverse_bwd(inv, g):
    return -ddot(ddot(inv, g, 0, 0), inv, 1, 1), jnp.zeros_like(inv)


_saved_inverse.defvjp(_saved_inverse_fwd, _saved_inverse_bwd)


def _chunk_tril(rows):
    r, c = _iota((rows, rows), 0), _iota((rows, rows), 1)
    return ((r >> 6) == (c >> 6)) & (r >= c)


def _dn_block(pq, pk, pv, hq, hk, hv, small, pz, cwq, cwk, cwv, a_log, dt_bias, nw, state, saved_inv=None):
    c = SCAN_CHUNK
    rows = pq.shape[0]
    bd = _block_diag_mask()
    q = _silu(_causal_conv(pq, hq, cwq, 4))
    k = _silu(_causal_conv(pk, hk, cwk, 4))
    v = _silu(_causal_conv(pv, hv, cwv, 4))
    q = q * lax.rsqrt(_head_sum(q * q, bd) + EPS) * (HEAD_DIM ** -0.5)
    k = k * lax.rsqrt(_head_sum(k * k, bd) + EPS)
    lane = _iota((1, 128), 1)
    g = jnp.where(lane < HEADS, -jnp.exp(a_log) * _softplus(small + dt_bias), 0.0)
    beta_b = mask_r(jax.nn.sigmoid(small), _expand_mat(HEADS))
    gc_all = mask_l(_chunk_tril(rows).astype(F32), g)
    gcb_all = mask_r(gc_all, _expand_mat(0))
    kb_all = k * beta_b
    vb_all = v * beta_b
    kbe_all = kb_all * jnp.exp(gcb_all)
    qg_all = q * jnp.exp(gcb_all)
    causal, strict = _tril(c), _tril(c, strict=True)
    nc = rows // c
    pairs = [(ci, h) for ci in range(nc) for h in range(HEADS)]
    sls = [slice(ci * c, (ci + 1) * c) for ci in range(nc)]
    decays, lows, attns = [], [], []
    for ci, h in pairs:
        gc = gc_all[sls[ci]]
        onehot = (_iota((c, 128), 1) == h).astype(F32)
        col = mask_l(onehot, gc, 1, 1)
        row = jnp.sum(gc * onehot, axis=1, keepdims=True)
        decays.append(jnp.exp(jnp.where(causal, row - col, -jnp.inf)))
    for j, (ci, h) in enumerate(pairs):
        mh = _head_mask(h)
        k_c = k[sls[ci]]
        lows.append(jnp.where(strict, bdot(kb_all[sls[ci]] * mh, k_c, 1, 1) * decays[j], 0.0))
        attns.append(bdot(q[sls[ci]] * mh, k_c, 1, 1) * decays[j])
    if saved_inv is None:
        invs = _neumann_inverses(lows)
    else:
        invs = [_saved_inverse(low, s) for low, s in zip(lows, saved_inv)]
    us, ws = [], []
    for ci in range(nc):
        u = jnp.zeros((c, GROUP), F32)
        w = jnp.zeros((c, GROUP), F32)
        for h in range(HEADS):
            mh = _head_mask(h)
            u = u + mh * ddot(invs[ci * HEADS + h], vb_all[sls[ci]])
            w = w + mh * ddot(invs[ci * HEADS + h], kbe_all[sls[ci]])
        us.append(u)
        ws.append(w)
    outs = []
    for ci in range(nc):
        gc_b = gcb_all[sls[ci]]
        gc_last_b = _row_pick(gc_b, c - 1)
        v_new = us[ci] - bdot(ws[ci], state, 1, 0)
        o = bdot(qg_all[sls[ci]], state, 1, 0)
        for h in range(HEADS):
            o = o + _head_mask(h) * bdot(attns[ci * HEADS + h], v_new, 1, 0)
        k_dec = k[sls[ci]] * jnp.exp(gc_last_b - gc_b)
        state = state * jnp.exp(gc_last_b) + bd * bdot(k_dec, v_new, 0, 0)
        outs.append(o)
    o = jnp.concatenate(outs, axis=0)
    return _head_rmsnorm_gate(o, nw, pz, bd), state, invs


def _gla_chunk(pq, pk, pv, small, pz, w2, gbias, nw, state_t):
    c = SCAN_CHUNK
    rows = pq.shape[0]
    nc = rows // c
    sls = [slice(ci * c, (ci + 1) * c) for ci in range(nc)]
    bd = _block_diag_mask()
    log_a = _log_sigmoid(bdot(small, w2, 1, 0) + gbias) * (1.0 / 16.0)
    gcum = mask_l(_chunk_tril(rows).astype(F32), log_a)
    r, s = _iota((rows, rows), 0), _iota((rows, rows), 1)
    base = (r >> 6) << 6
    g_mid = mask_l((s == base + c // 2).astype(F32), gcum)
    g_last = mask_l((s == base + c - 1).astype(F32), gcum)
    q = pq * (HEAD_DIM ** -0.5)
    qa = q * jnp.exp(gcum - g_mid)
    ka = pk * jnp.exp(g_mid - gcum)
    qg = q * jnp.exp(gcum)
    k_last = pk * jnp.exp(g_last - gcum)
    causal = _tril(c)
    attns = [jnp.where(causal, bdot(qa[sls[ci]] * _head_mask(h), ka[sls[ci]], 1, 1), 0.0)
             for ci in range(nc) for h in range(HEADS)]
    intra = []
    for ci in range(nc):
        o = jnp.zeros((c, GROUP), F32)
        for h in range(HEADS):
            o = o + _head_mask(h) * bdot(attns[ci * HEADS + h], pv[sls[ci]], 1, 0)
        intra.append(o)
    kvs = [bd * bdot(pv[sls[ci]], k_last[sls[ci]], 0, 0) for ci in range(nc)]
    states = []
    for ci in range(nc):
        states.append(state_t)
        state_t = state_t * jnp.exp(_row_pick(g_last[sls[ci]], 0)) + kvs[ci]
    outs = [intra[ci] + bdot(qg[sls[ci]], states[ci], 1, 1) for ci in range(nc)]
    o = jnp.concatenate(outs, axis=0)
    return _head_rmsnorm_gate(o, nw, pz, bd), state_t


def _col_spec(rows, group, rev_n=None):
    if rev_n is None:
        return pl.BlockSpec((rows, GROUP), lambda i: (i, group))
    return pl.BlockSpec((rows, GROUP), lambda i: (rev_n - 1 - i, group))


def _small_spec(rows, group128, rev_n=None):
    if rev_n is None:
        return pl.BlockSpec((rows, 128), lambda i: (i, group128))
    return pl.BlockSpec((rows, 128), lambda i: (rev_n - 1 - i, group128))


def _halo_spec(rows, group, rev_n=None):
    per = rows // HALO
    if rev_n is None:
        return pl.BlockSpec((HALO, GROUP), lambda i: (jnp.maximum(i * per - 1, 0), group))
    return pl.BlockSpec((HALO, GROUP), lambda i: (jnp.maximum((rev_n - 1 - i) * per - 1, 0), group))


def _full_spec(shape):
    nd = len(shape)
    return pl.BlockSpec(shape, lambda i: (0,) * nd)


def _out_rows_spec(rows, lanes, rev_n=None):
    if rev_n is None:
        return pl.BlockSpec((rows, lanes), lambda i: (i, 0))
    return pl.BlockSpec((rows, lanes), lambda i: (rev_n - 1 - i, 0))


SGU_ROWS = 4 * SGU_CHUNK


def _sgu_fwd(p, ln_w, ln_b, ws, bs_t):
    t = p.shape[0]
    n = t // SGU_ROWS

    def body(pu_ref, pv_ref, lw_ref, lb_ref, ws_ref, bs_ref, y_ref):
        y = _sgu_chunk(pu_ref[...], pv_ref[...], lw_ref[...], lb_ref[...],
                       ws_ref[0], ws_ref[1], ws_ref[2], ws_ref[3], bs_ref[...])
        y_ref[...] = y.astype(y_ref.dtype)

    return pl.pallas_call(
        body, name="sgu_fwd", grid=(n,),
        in_specs=[_col_spec(SGU_ROWS, COL_AU), _col_spec(SGU_ROWS, COL_AV), _full_spec((1, GROUP)),
                  _full_spec((1, GROUP)), _full_spec((HEADS, SGU_CHUNK, SGU_CHUNK)), _full_spec((SGU_CHUNK, 128))],
        out_specs=_out_rows_spec(SGU_ROWS, GROUP),
        out_shape=jax.ShapeDtypeStruct((t, GROUP), BF16),
        compiler_params=_cparams(("arbitrary",)),
    )(p, p, ln_w, ln_b, ws, bs_t)


def _sgu_bwd(p, dmix, ln_w, ln_b, ws, bs_t):
    t = p.shape[0]
    n = t // SGU_ROWS

    def body(pu_ref, pv_ref, dy_ref, lw_ref, lb_ref, ws_ref, bs_ref,
             dpu_ref, dpv_ref, dlw_ref, dlb_ref, dws_ref, dbs_ref):
        args = (pu_ref[...], pv_ref[...], lw_ref[...], lb_ref[...],
                ws_ref[0], ws_ref[1], ws_ref[2], ws_ref[3], bs_ref[...])
        _, vjp = jax.vjp(_sgu_chunk, *args)
        dpu, dpv, dlw, dlb, d0, d1, d2, d3, dbs = vjp(dy_ref[...])
        dpu_ref[...] = dpu.astype(dpu_ref.dtype)
        dpv_ref[...] = dpv.astype(dpv_ref.dtype)

        @pl.when(pl.program_id(0) == 0)
        def _():
            dlw_ref[...] = jnp.zeros_like(dlw_ref)
            dlb_ref[...] = jnp.zeros_like(dlb_ref)
            dws_ref[...] = jnp.zeros_like(dws_ref)
            dbs_ref[...] = jnp.zeros_like(dbs_ref)

        dlw_ref[...] += dlw
        dlb_ref[...] += dlb
        for h, d in enumerate((d0, d1, d2, d3)):
            dws_ref[h] += d
        dbs_ref[...] += dbs

    return pl.pallas_call(
        body, name="sgu_bwd", grid=(n,),
        in_specs=[_col_spec(SGU_ROWS, COL_AU), _col_spec(SGU_ROWS, COL_AV),
                  pl.BlockSpec((SGU_ROWS, GROUP), lambda i: (i, 0)),
                  _full_spec((1, GROUP)), _full_spec((1, GROUP)), _full_spec((HEADS, SGU_CHUNK, SGU_CHUNK)),
                  _full_spec((SGU_CHUNK, 128))],
        out_specs=[_out_rows_spec(SGU_ROWS, GROUP), _out_rows_spec(SGU_ROWS, GROUP), _full_spec((1, GROUP)),
                   _full_spec((1, GROUP)), _full_spec((HEADS, SGU_CHUNK, SGU_CHUNK)), _full_spec((SGU_CHUNK, 128))],
        out_shape=[jax.ShapeDtypeStruct((t, GROUP), BF16), jax.ShapeDtypeStruct((t, GROUP), BF16),
                   jax.ShapeDtypeStruct((1, GROUP), F32), jax.ShapeDtypeStruct((1, GROUP), F32),
                   jax.ShapeDtypeStruct((HEADS, SGU_CHUNK, SGU_CHUNK), F32),
                   jax.ShapeDtypeStruct((SGU_CHUNK, 128), F32)],
        compiler_params=_cparams(("arbitrary",)),
    )(p, p, dmix, ln_w, ln_b, ws, bs_t)


SC_ROWS = 512


def _first_block_zero(halo, first):
    return jnp.where(first, 0.0, halo)


def _sc_fwd(p, cw):
    t = p.shape[0]
    n = t // SC_ROWS

    def body(pb_ref, pc_ref, ph_ref, hc_ref, hh_ref, cw_ref, y_ref):
        first = pl.program_id(0) == 0
        y = _sc_chunk(pb_ref[...], pc_ref[...], ph_ref[...], _first_block_zero(hc_ref[...], first),
                      _first_block_zero(hh_ref[...], first), cw_ref[...])
        y_ref[...] = y.astype(y_ref.dtype)

    return pl.pallas_call(
        body, name="sc_fwd", grid=(n,),
        in_specs=[_col_spec(SC_ROWS, COL_BB), _col_spec(SC_ROWS, COL_BC), _col_spec(SC_ROWS, COL_BH),
                  _halo_spec(SC_ROWS, COL_BC), _halo_spec(SC_ROWS, COL_BH), _full_spec((HALO, GROUP))],
        out_specs=_out_rows_spec(SC_ROWS, GROUP),
        out_shape=jax.ShapeDtypeStruct((t, GROUP), BF16),
        compiler_params=_cparams(("arbitrary",)),
    )(p, p, p, p, p, cw)


def _add_halo_grad(d, carry):
    return d + jnp.concatenate([jnp.zeros((d.shape[0] - HALO, d.shape[1]), d.dtype), carry], axis=0)


def _sc_bwd(p, dmix, cw):
    t = p.shape[0]
    n = t // SC_ROWS

    def body(pb_ref, pc_ref, ph_ref, hc_ref, hh_ref, dy_ref, cw_ref,
             dpb_ref, dpc_ref, dph_ref, dcw_ref, carry_c, carry_h):
        i = pl.program_id(0)
        first = i == n - 1

        @pl.when(i == 0)
        def _():
            carry_c[...] = jnp.zeros_like(carry_c)
            carry_h[...] = jnp.zeros_like(carry_h)
            dcw_ref[...] = jnp.zeros_like(dcw_ref)

        args = (pb_ref[...], pc_ref[...], ph_ref[...], _first_block_zero(hc_ref[...], first),
                _first_block_zero(hh_ref[...], first), cw_ref[...])
        _, vjp = jax.vjp(_sc_chunk, *args)
        dpb, dpc, dph, dhc, dhh, dcw = vjp(dy_ref[...])
        dpb_ref[...] = dpb.astype(dpb_ref.dtype)
        dpc_ref[...] = _add_halo_grad(dpc, carry_c[...]).astype(dpc_ref.dtype)
        dph_ref[...] = _add_halo_grad(dph, carry_h[...]).astype(dph_ref.dtype)
        carry_c[...] = dhc
        carry_h[...] = dhh
        dcw_ref[...] += dcw

    return pl.pallas_call(
        body, name="sc_bwd", grid=(n,),
        in_specs=[_col_spec(SC_ROWS, COL_BB, n), _col_spec(SC_ROWS, COL_BC, n), _col_spec(SC_ROWS, COL_BH, n),
                  _halo_spec(SC_ROWS, COL_BC, n), _halo_spec(SC_ROWS, COL_BH, n),
                  pl.BlockSpec((SC_ROWS, GROUP), lambda i: (n - 1 - i, 1)), _full_spec((HALO, GROUP))],
        out_specs=[_out_rows_spec(SC_ROWS, GROUP, n)] * 3 + [_full_spec((HALO, GROUP))],
        out_shape=[jax.ShapeDtypeStruct((t, GROUP), BF16)] * 3 + [jax.ShapeDtypeStruct((HALO, GROUP), F32)],
        scratch_shapes=[pltpu.VMEM((HALO, GROUP), F32), pltpu.VMEM((HALO, GROUP), F32)],
        compiler_params=_cparams(("arbitrary",)),
    )(p, p, p, p, p, dmix, cw)


SCAN_STEP_CHUNKS = 4
SCAN_ROWS = SCAN_STEP_CHUNKS * SCAN_CHUNK


def _host_call(body, hosted, *, name, grid, in_specs, out_specs, out_shape, scratch_shapes, args):
    params = _cparams(("arbitrary",))
    if hosted is None:
        outs = pl.pallas_call(body, name=name, grid=grid, in_specs=in_specs, out_specs=out_specs,
                              out_shape=out_shape, scratch_shapes=scratch_shapes, compiler_params=params)(*args)
        return outs, None
    stage, arrays, modes, bufs = hosted
    arrays = list(arrays) if stage == 1 else []
    n_in, n_out, n_scr, n_src, na = len(in_specs), len(out_specs), len(scratch_shapes), len(arrays), len(bufs)
    last = grid[0] - 1

    def new_body(*refs):
        srcs = refs[n_in:n_in + n_src]
        o0 = n_in + n_src + na
        ex = refs[o0 + n_out:o0 + n_out + na]
        s0 = o0 + n_out + na
        sems = refs[s0 + n_scr:]
        i = pl.program_id(0)

        def run(what):
            if stage == 1:
                _exchange_stage1(srcs, ex, modes, sems[0], sems[1], what)
            else:
                _exchange_stage2(ex, sems[0], sems[1], what)

        @pl.when(i == 0)
        def _():
            run("start")

        body(*refs[:n_in], *refs[o0:o0 + n_out], *refs[s0:s0 + n_scr])

        @pl.when(i == last)
        def _():
            run("wait")

    any_spec = pl.BlockSpec(memory_space=pl.ANY)
    outs = pl.pallas_call(
        new_body, name=name, grid=grid,
        in_specs=list(in_specs) + [any_spec] * (n_src + na), out_specs=list(out_specs) + [any_spec] * na,
        out_shape=list(out_shape) + [jax.ShapeDtypeStruct(b.shape, b.dtype) for b in bufs],
        input_output_aliases={n_in + n_src + a: n_out + a for a in range(na)},
        scratch_shapes=list(scratch_shapes) + (_stage1_sems(na) if stage == 1 else _stage2_sems(na)),
        compiler_params=params,
    )(*args, *arrays, *bufs)
    return outs[:n_out], outs[n_out:]


def _dn_fwd(p, cw3, a_log, dt_bias, nw, hosted=None):
    t = p.shape[0]
    r = SCAN_ROWS
    n = t // r

    def body(pq_ref, pk_ref, pv_ref, hq_ref, hk_ref, hv_ref, sm_ref, pz_ref, cw_ref, al_ref, dt_ref, nw_ref,
             y_ref, ck_ref, inv_ref, state):
        first = pl.program_id(0) == 0

        @pl.when(first)
        def _():
            state[...] = jnp.zeros_like(state)

        s_in = state[...]
        ck_ref[0] = s_in
        y, s_out, invs = _dn_block(pq_ref[...], pk_ref[...], pv_ref[...], _first_block_zero(hq_ref[...], first),
                                   _first_block_zero(hk_ref[...], first), _first_block_zero(hv_ref[...], first),
                                   sm_ref[...], pz_ref[...], cw_ref[0], cw_ref[1], cw_ref[2],
                                   al_ref[...], dt_ref[...], nw_ref[...], s_in)
        y_ref[...] = y.astype(y_ref.dtype)
        state[...] = s_out
        for j, inv in enumerate(invs):
            inv_ref[j] = inv

    nh = SCAN_STEP_CHUNKS * HEADS
    return _host_call(
        body, hosted, name="dn_fwd", grid=(n,),
        in_specs=[_col_spec(r, COL_CQ), _col_spec(r, COL_CK), _col_spec(r, COL_CV),
                  _halo_spec(r, COL_CQ), _halo_spec(r, COL_CK), _halo_spec(r, COL_CV),
                  _small_spec(r, COL128_SMALL_C), _col_spec(r, COL_CZ), _full_spec((3, HALO, GROUP)),
                  _full_spec((1, 128)), _full_spec((1, 128)), _full_spec((1, GROUP))],
        out_specs=[_out_rows_spec(r, GROUP), pl.BlockSpec((1, GROUP, GROUP), lambda i: (i, 0, 0)),
                   pl.BlockSpec((nh, SCAN_CHUNK, SCAN_CHUNK), lambda i: (i, 0, 0))],
        out_shape=[jax.ShapeDtypeStruct((t, GROUP), BF16), jax.ShapeDtypeStruct((n, GROUP, GROUP), F32),
                   jax.ShapeDtypeStruct((n * nh, SCAN_CHUNK, SCAN_CHUNK), F32)],
        scratch_shapes=[pltpu.VMEM((GROUP, GROUP), F32)],
        args=(p, p, p, p, p, p, p, p, cw3, a_log, dt_bias, nw))


def _dn_bwd(p, dmix, states, invs, cw3, a_log, dt_bias, nw, hosted=None):
    t = p.shape[0]
    c = SCAN_ROWS
    n = t // c
    nh = SCAN_STEP_CHUNKS * HEADS

    def body(pq_ref, pk_ref, pv_ref, hq_ref, hk_ref, hv_ref, sm_ref, pz_ref, dy_ref, ck_ref, inv_ref,
             cw_ref, al_ref, dt_ref, nw_ref,
             dpq_ref, dpk_ref, dpv_ref, dsm_ref, dpz_ref, dcw_ref, dal_ref, ddt_ref, dnw_ref,
             dstate, carry):
        i = pl.program_id(0)
        first = i == n - 1

        @pl.when(i == 0)
        def _():
            dstate[...] = jnp.zeros_like(dstate)
            carry[...] = jnp.zeros_like(carry)
            dcw_ref[...] = jnp.zeros_like(dcw_ref)
            dal_ref[...] = jnp.zeros_like(dal_ref)
            ddt_ref[...] = jnp.zeros_like(ddt_ref)
            dnw_ref[...] = jnp.zeros_like(dnw_ref)

        args = (pq_ref[...], pk_ref[...], pv_ref[...], _first_block_zero(hq_ref[...], first),
                _first_block_zero(hk_ref[...], first), _first_block_zero(hv_ref[...], first),
                sm_ref[...], pz_ref[...], cw_ref[0], cw_ref[1], cw_ref[2],
                al_ref[...], dt_ref[...], nw_ref[...], ck_ref[0])
        saved = [inv_ref[j] for j in range(nh)]
        _, vjp = jax.vjp(lambda *a: _dn_block(*a, saved_inv=saved)[:2], *args)
        (dpq, dpk, dpv, dhq, dhk, dhv, dsm, dpz, dcq, dck, dcv, dal, ddt, dnw, dst) = vjp(
            (dy_ref[...], dstate[...]))
        dpq_ref[...] = _add_halo_grad(dpq, carry[0]).astype(dpq_ref.dtype)
        dpk_ref[...] = _add_halo_grad(dpk, carry[1]).astype(dpk_ref.dtype)
        dpv_ref[...] = _add_halo_grad(dpv, carry[2]).astype(dpv_ref.dtype)
        dsm_ref[...] = dsm.astype(dsm_ref.dtype)
        dpz_ref[...] = dpz.astype(dpz_ref.dtype)
        carry[0] = dhq
        carry[1] = dhk
        carry[2] = dhv
        dstate[...] = dst
        dcw_ref[0] += dcq
        dcw_ref[1] += dck
        dcw_ref[2] += dcv
        dal_ref[...] += dal
        ddt_ref[...] += ddt
        dnw_ref[...] += dnw

    return _host_call(
        body, hosted, name="dn_bwd", grid=(n,),
        in_specs=[_col_spec(c, COL_CQ, n), _col_spec(c, COL_CK, n), _col_spec(c, COL_CV, n),
                  _halo_spec(c, COL_CQ, n), _halo_spec(c, COL_CK, n), _halo_spec(c, COL_CV, n),
                  _small_spec(c, COL128_SMALL_C, n), _col_spec(c, COL_CZ, n),
                  pl.BlockSpec((c, GROUP), lambda i: (n - 1 - i, 2)),
                  pl.BlockSpec((1, GROUP, GROUP), lambda i: (n - 1 - i, 0, 0)),
                  pl.BlockSpec((nh, SCAN_CHUNK, SCAN_CHUNK), lambda i: (n - 1 - i, 0, 0)),
                  _full_spec((3, HALO, GROUP)), _full_spec((1, 128)), _full_spec((1, 128)), _full_spec((1, GROUP))],
        out_specs=[_out_rows_spec(c, GROUP, n)] * 3 + [_out_rows_spec(c, 128, n), _out_rows_spec(c, GROUP, n),
                   _full_spec((3, HALO, GROUP)), _full_spec((1, 128)), _full_spec((1, 128)), _full_spec((1, GROUP))],
        out_shape=[jax.ShapeDtypeStruct((t, GROUP), BF16)] * 3 + [
            jax.ShapeDtypeStruct((t, 128), BF16), jax.ShapeDtypeStruct((t, GROUP), BF16),
            jax.ShapeDtypeStruct((3, HALO, GROUP), F32), jax.ShapeDtypeStruct((1, 128), F32),
            jax.ShapeDtypeStruct((1, 128), F32), jax.ShapeDtypeStruct((1, GROUP), F32)],
        scratch_shapes=[pltpu.VMEM((GROUP, GROUP), F32), pltpu.VMEM((3, HALO, GROUP), F32)],
        args=(p, p, p, p, p, p, p, p, dmix, states, invs, cw3, a_log, dt_bias, nw))


def _gla_fwd(p, w2, gbias, nw, hosted=None):
    t = p.shape[0]
    c = SCAN_ROWS
    n = t // c

    def body(pq_ref, pk_ref, pv_ref, sm_ref, pz_ref, w2_ref, gb_ref, nw_ref, y_ref, ck_ref, state):
        @pl.when(pl.program_id(0) == 0)
        def _():
            state[...] = jnp.zeros_like(state)

        s_in = state[...]
        ck_ref[0] = s_in
        y, s_out = _gla_chunk(pq_ref[...], pk_ref[...], pv_ref[...], sm_ref[...], pz_ref[...],
                              w2_ref[...], gb_ref[...], nw_ref[...], s_in)
        y_ref[...] = y.astype(y_ref.dtype)
        state[...] = s_out

    return _host_call(
        body, hosted, name="gla_fwd", grid=(n,),
        in_specs=[_col_spec(c, COL_DQ), _col_spec(c, COL_DK), _col_spec(c, COL_DV),
                  _small_spec(c, COL128_SMALL_D), _col_spec(c, COL_DZ),
                  _full_spec((128, GROUP)), _full_spec((1, GROUP)), _full_spec((1, GROUP))],
        out_specs=[_out_rows_spec(c, GROUP), pl.BlockSpec((1, GROUP, GROUP), lambda i: (i, 0, 0))],
        out_shape=[jax.ShapeDtypeStruct((t, GROUP), BF16), jax.ShapeDtypeStruct((n, GROUP, GROUP), F32)],
        scratch_shapes=[pltpu.VMEM((GROUP, GROUP), F32)],
        args=(p, p, p, p, p, w2, gbias, nw))


def _gla_bwd(p, dmix, states, w2, gbias, nw, hosted=None):
    t = p.shape[0]
    c = SCAN_ROWS
    n = t // c

    def body(pq_ref, pk_ref, pv_ref, sm_ref, pz_ref, dy_ref, ck_ref, w2_ref, gb_ref, nw_ref,
             dpq_ref, dpk_ref, dpv_ref, dsm_ref, dpz_ref, dw2_ref, dgb_ref, dnw_ref, dstate):
        @pl.when(pl.program_id(0) == 0)
        def _():
            dstate[...] = jnp.zeros_like(dstate)
            dw2_ref[...] = jnp.zeros_like(dw2_ref)
            dgb_ref[...] = jnp.zeros_like(dgb_ref)
            dnw_ref[...] = jnp.zeros_like(dnw_ref)

        args = (pq_ref[...], pk_ref[...], pv_ref[...], sm_ref[...], pz_ref[...],
                w2_ref[...], gb_ref[...], nw_ref[...], ck_ref[0])
        _, vjp = jax.vjp(_gla_chunk, *args)
        dpq, dpk, dpv, dsm, dpz, dw2, dgb, dnw, dst = vjp((dy_ref[...], dstate[...]))
        dpq_ref[...] = dpq.astype(dpq_ref.dtype)
        dpk_ref[...] = dpk.astype(dpk_ref.dtype)
        dpv_ref[...] = dpv.astype(dpv_ref.dtype)
        dsm_ref[...] = dsm.astype(dsm_ref.dtype)
        dpz_ref[...] = dpz.astype(dpz_ref.dtype)
        dstate[...] = dst
        dw2_ref[...] += dw2
        dgb_ref[...] += dgb
        dnw_ref[...] += dnw

    return _host_call(
        body, hosted, name="gla_bwd", grid=(n,),
        in_specs=[_col_spec(c, COL_DQ, n), _col_spec(c, COL_DK, n), _col_spec(c, COL_DV, n),
                  _small_spec(c, COL128_SMALL_D, n), _col_spec(c, COL_DZ, n),
                  pl.BlockSpec((c, GROUP), lambda i: (n - 1 - i, 3)),
                  pl.BlockSpec((1, GROUP, GROUP), lambda i: (n - 1 - i, 0, 0)),
                  _full_spec((128, GROUP)), _full_spec((1, GROUP)), _full_spec((1, GROUP))],
        out_specs=[_out_rows_spec(c, GROUP, n)] * 3 + [_out_rows_spec(c, 128, n), _out_rows_spec(c, GROUP, n),
                   _full_spec((128, GROUP)), _full_spec((1, GROUP)), _full_spec((1, GROUP))],
        out_shape=[jax.ShapeDtypeStruct((t, GROUP), BF16)] * 3 + [
            jax.ShapeDtypeStruct((t, 128), BF16), jax.ShapeDtypeStruct((t, GROUP), BF16),
            jax.ShapeDtypeStruct((128, GROUP), F32), jax.ShapeDtypeStruct((1, GROUP), F32),
            jax.ShapeDtypeStruct((1, GROUP), F32)],
        scratch_shapes=[pltpu.VMEM((GROUP, GROUP), F32)],
        args=(p, p, p, p, p, dmix, states, w2, gbias, nw))


def _pick_tile(n, pref):
    for cand in pref:
        if n % cand == 0:
            return cand
    return n


MM_TILE_CAP = 1408


def _largest_tile(n, cap):
    best = None
    for mult in range(1, cap // 128 + 1):
        if n % (128 * mult) == 0:
            best = 128 * mult
    return best if best is not None else n


def _half_index(t, per_half, middle):
    half = jnp.where(t >= per_half, 1, 0)
    return half, middle, t - half * per_half


def _matmul(a, b, mode, out_dtype, name, res=None, pieces=None, norm_w=None):
    a_list = list(a) if isinstance(a, (list, tuple)) else [a]
    a_rows, a_cols = a_list[0].shape[0], sum(x.shape[1] for x in a_list)
    if mode == "nn":
        (m, k), n = (a_rows, a_cols), b.shape[1]
    elif mode == "nt":
        (m, k), n = (a_rows, a_cols), b.shape[0]
    else:
        (k, m), n = (a_rows, a_cols), b.shape[-1] * (2 if b.ndim == 3 else 1)
    tm = _largest_tile(m, MM_TILE_CAP)
    tn = _largest_tile(b.shape[-1] if b.ndim == 3 else n, MM_TILE_CAP)
    if pieces == "cols":
        tm, tn = m, n // N_CHIPS
    tk = _largest_tile(k, MM_TILE_CAP)
    if len(a_list) > 1:
        tk, tm = (k, tm) if mode == "nn" else (tk, m)
    nk = k // tk
    na = len(a_list)
    if mode == "nn":
        a_specs = [pl.BlockSpec((tm, tk if na == 1 else x.shape[1]), lambda i, j, kk: (i, kk)) for x in a_list]
        b_spec = pl.BlockSpec((tk, tn), lambda i, j, kk: (kk, j))
        dims = (1, 0)
    elif mode == "nt":
        a_specs = [pl.BlockSpec((tm, tk), lambda i, j, kk: (i, kk))]
        b_spec = pl.BlockSpec((tn, tk), lambda i, j, kk: (j, kk))
        dims = (1, 1)
    else:
        a_specs = [pl.BlockSpec((tk, tm if na == 1 else x.shape[1]), lambda i, j, kk: (kk, i)) for x in a_list]
        if b.ndim == 3:
            njh = b.shape[-1] // tn
            b_spec = pl.BlockSpec((None, tk, tn), lambda i, j, kk: _half_index(j, njh, kk))
        else:
            b_spec = pl.BlockSpec((tk, tn), lambda i, j, kk: (kk, j))
        dims = (0, 0)
    o_spec = pl.BlockSpec((tm, tn), lambda i, j, kk: (i, j))
    o_shape = (m, n)
    if pieces == "cols":
        o_spec = pl.BlockSpec((2, None, tm // 2, tn), lambda i, j, kk: (0, j, 0, 0))
        o_shape = (2, N_CHIPS, tm // 2, tn)
    has_res = res is not None
    has_norm = norm_w is not None

    def body(*refs):
        a_refs, b_ref = refs[:na], refs[na]
        pos = na + 1
        r_ref = refs[pos] if has_res else None
        pos += has_res
        nw_ref = refs[pos] if has_norm else None
        pos += has_norm
        o_ref = refs[pos]
        h_ref = refs[pos + 1] if has_norm else None
        if na == 1:
            a_val = a_refs[0][...].astype(MXU_DTYPE)
        else:
            a_val = jnp.concatenate([r[...].astype(MXU_DTYPE) for r in a_refs], axis=1)
        part = _dg(a_val, b_ref[...].astype(MXU_DTYPE), *dims)

        def finish(out):
            if has_res:
                out = out + r_ref[...]
            if pieces == "cols":
                o_ref[0] = out[:tm // 2].astype(o_ref.dtype)
                o_ref[1] = out[tm // 2:].astype(o_ref.dtype)
            else:
                o_ref[...] = out.astype(o_ref.dtype)
            if has_norm:
                r = lax.rsqrt(jnp.mean(out * out, axis=-1, keepdims=True) + EPS)
                h_ref[...] = (out * r * nw_ref[...]).astype(h_ref.dtype)

        if nk == 1:
            finish(part)
            return
        acc = refs[-1]
        kk = pl.program_id(2)

        @pl.when(kk == 0)
        def _():
            acc[...] = part

        @pl.when(kk > 0)
        def _():
            acc[...] += part

        @pl.when(kk == nk - 1)
        def _():
            finish(acc[...])

    in_specs = a_specs + [b_spec] + ([o_spec] if has_res else [])
    args = (*a_list, b) + ((res,) if has_res else ())
    out_specs, out_shape = o_spec, jax.ShapeDtypeStruct(o_shape, out_dtype)
    if has_norm:
        assert mode == "nn" and tn == n
        in_specs.append(pl.BlockSpec((1, n), lambda i, j, kk: (0, 0)))
        args += (norm_w,)
        out_specs, out_shape = [o_spec, o_spec], [out_shape, jax.ShapeDtypeStruct(o_shape, BF16)]
    return pl.pallas_call(
        body, name=name, grid=(m // tm, n // tn, nk), in_specs=in_specs, out_specs=out_specs,
        out_shape=out_shape,
        scratch_shapes=[pltpu.VMEM((tm, tn), F32)] if nk > 1 else [],
        compiler_params=_cparams(("parallel", "parallel", "arbitrary")),
    )(*args)


def _matmul_nt_norm_bwd(a, b, x, w, dres, name):
    n = b.shape[0]
    m = a.shape[-2]
    tm = _largest_tile(m, 1024)
    if a.ndim == 3:
        kh = a.shape[2]
        k = 2 * kh
        tk = _largest_tile(kh, MM_TILE_CAP)
        nkh = kh // tk
        a_spec = pl.BlockSpec((None, tm, tk), lambda i, kk: _half_index(kk, nkh, i))
    else:
        k = a.shape[1]
        tk = _largest_tile(k, MM_TILE_CAP)
        a_spec = pl.BlockSpec((tm, tk), lambda i, kk: (i, kk))
    nk = k // tk

    def body(a_ref, b_ref, x_ref, w_ref, r_ref, dx_ref, dw_ref, acc):
        i, kk = pl.program_id(0), pl.program_id(1)
        part = _dg(a_ref[...].astype(MXU_DTYPE), b_ref[...].astype(MXU_DTYPE), 1, 1)

        @pl.when(kk == 0)
        def _():
            acc[...] = part

        @pl.when(kk > 0)
        def _():
            acc[...] += part

        @pl.when((i == 0) & (kk == 0))
        def _():
            dw_ref[...] = jnp.zeros_like(dw_ref)

        @pl.when(kk == nk - 1)
        def _():
            g = acc[...]
            xv = x_ref[...]
            r = lax.rsqrt(jnp.mean(xv * xv, axis=-1, keepdims=True) + EPS)
            xhat = xv * r
            dw_ref[...] += jnp.sum(g * xhat, axis=0, keepdims=True)
            gx = g * w_ref[...]
            dx_ref[...] = r_ref[...] + r * (gx - xhat * jnp.mean(gx * xhat, axis=-1, keepdims=True))

    row_spec = pl.BlockSpec((tm, n), lambda i, kk: (i, 0))
    return pl.pallas_call(
        body, name=name, grid=(m // tm, nk),
        in_specs=[a_spec, pl.BlockSpec((n, tk), lambda i, kk: (0, kk)),
                  row_spec, pl.BlockSpec((1, n), lambda i, kk: (0, 0)), row_spec],
        out_specs=[row_spec, pl.BlockSpec((1, n), lambda i, kk: (0, 0))],
        out_shape=[jax.ShapeDtypeStruct((m, n), F32), jax.ShapeDtypeStruct((1, n), F32)],
        scratch_shapes=[pltpu.VMEM((tm, n), F32)],
        compiler_params=_cparams(("arbitrary", "arbitrary")),
    )(a, b, x, w, dres)


NORM_ROWS = 512


def _rmsnorm_fwd(x, w, name):
    t, d = x.shape

    def body(x_ref, w_ref, o_ref):
        xv = x_ref[...]
        r = lax.rsqrt(jnp.mean(xv * xv, axis=-1, keepdims=True) + EPS)
        o_ref[...] = (xv * r * w_ref[...]).astype(o_ref.dtype)

    return pl.pallas_call(
        body, name=name, grid=(t // NORM_ROWS,),
        in_specs=[pl.BlockSpec((NORM_ROWS, d), lambda i: (i, 0)), _full_spec((1, d))],
        out_specs=pl.BlockSpec((NORM_ROWS, d), lambda i: (i, 0)),
        out_shape=jax.ShapeDtypeStruct((t, d), BF16),
        compiler_params=_cparams(("parallel",)),
    )(x, w)


SWIGLU_ROWS = 128


def _ffn_up_swiglu(h, w_gate_up):
    m, k = h.shape
    tm = _largest_tile(m, 512)
    tn = _largest_tile(D_FF, MM_TILE_CAP)
    nj = D_FF // tn

    def body(a_ref, bg_ref, bu_ref, g_ref, u_ref, act_ref):
        a = a_ref[...].astype(MXU_DTYPE)
        gate = _dg(a, bg_ref[...].astype(MXU_DTYPE), 1, 0)
        up = _dg(a, bu_ref[...].astype(MXU_DTYPE), 1, 0)
        g_ref[...] = gate
        u_ref[...] = up
        act_ref[...] = (_silu(gate) * up).astype(act_ref.dtype)

    o_spec = pl.BlockSpec((tm, tn), lambda i, j: (i, j))
    return pl.pallas_call(
        body, name="ffn_up", grid=(m // tm, nj),
        in_specs=[pl.BlockSpec((tm, k), lambda i, j: (i, 0)), pl.BlockSpec((k, tn), lambda i, j: (0, j)),
                  pl.BlockSpec((k, tn), lambda i, j: (0, j + nj))],
        out_specs=[o_spec, o_spec, o_spec],
        out_shape=[jax.ShapeDtypeStruct((m, D_FF), F32), jax.ShapeDtypeStruct((m, D_FF), F32),
                   jax.ShapeDtypeStruct((m, D_FF), BF16)],
        compiler_params=_cparams(("parallel", "parallel")),
    )(h, w_gate_up, w_gate_up)


def _ffn_down_dx_swiglu(dx, w_down, gate, up):
    m, k = dx.shape
    tm = _largest_tile(m, 512)
    tn = _largest_tile(D_FF, MM_TILE_CAP)

    def body(a_ref, b_ref, g_ref, u_ref, o_ref):
        da = _dg(a_ref[...].astype(MXU_DTYPE), b_ref[...].astype(MXU_DTYPE), 1, 1)
        gate = g_ref[...]
        sg = jax.nn.sigmoid(gate)
        o_ref[0] = (da * u_ref[...] * (sg * (1.0 + gate * (1.0 - sg)))).astype(o_ref.dtype)
        o_ref[1] = (da * gate * sg).astype(o_ref.dtype)

    tile = pl.BlockSpec((tm, tn), lambda i, j: (i, j))
    return pl.pallas_call(
        body, name="ffn_down_dx", grid=(m // tm, D_FF // tn),
        in_specs=[pl.BlockSpec((tm, k), lambda i, j: (i, 0)), pl.BlockSpec((tn, k), lambda i, j: (j, 0)), tile, tile],
        out_specs=pl.BlockSpec((2, tm, tn), lambda i, j: (0, i, j)),
        out_shape=jax.ShapeDtypeStruct((2, m, D_FF), MXU_DTYPE),
        compiler_params=_cparams(("parallel", "parallel")),
    )(dx, w_down, gate, up)


def _loss_head(x, w, target):
    t, d = x.shape

    def fwd(xv, wv, tv):
        r = lax.rsqrt(jnp.mean(xv * xv, axis=-1, keepdims=True) + EPS)
        err = xv * r * wv - tv
        return 0.5 * jnp.sum(jnp.mean(err * err, axis=-1, keepdims=True), axis=0, keepdims=True)

    def body(x_ref, w_ref, t_ref, dx_ref, dw_ref, loss_ref):
        @pl.when(pl.program_id(0) == 0)
        def _():
            dw_ref[...] = jnp.zeros_like(dw_ref)
            loss_ref[...] = jnp.zeros_like(loss_ref)

        loss, vjp = jax.vjp(fwd, x_ref[...], w_ref[...], t_ref[...])
        dx, dw, _ = vjp(jnp.ones((1, 1), F32))
        dx_ref[...] = dx
        dw_ref[...] += dw
        loss_ref[...] += jnp.broadcast_to(loss, loss_ref.shape)

    return pl.pallas_call(
        body, name="loss_head", grid=(t // NORM_ROWS,),
        in_specs=[pl.BlockSpec((NORM_ROWS, d), lambda i: (i, 0)), _full_spec((1, d)),
                  pl.BlockSpec((NORM_ROWS, d), lambda i: (i, 0))],
        out_specs=[pl.BlockSpec((NORM_ROWS, d), lambda i: (i, 0)), _full_spec((1, d)), _full_spec((8, 128))],
        out_shape=[jax.ShapeDtypeStruct((t, d), F32), jax.ShapeDtypeStruct((1, d), F32),
                   jax.ShapeDtypeStruct((8, 128), F32)],
        compiler_params=_cparams(("arbitrary",)),
    )(x, w, target)


def _pad_w_in(w):
    z = lambda n: jnp.zeros((w.shape[0], n), w.dtype)
    return jnp.concatenate([w[:, 0:2048], w[:, 2056:2312], w[:, 2312:3080], w[:, 3096:3352],
                            w[:, 2048:2056], z(120), w[:, 3080:3096], z(112)], axis=1)


def _unpad_w_in(wp):
    return jnp.concatenate([wp[:, 0:2048], wp[:, 3328:3336], wp[:, 2048:2304], wp[:, 2304:3072],
                            wp[:, 3456:3472], wp[:, 3072:3328]], axis=1)


def _pad_rows(a, rows):
    return jnp.concatenate([a, jnp.zeros((rows - a.shape[0],) + a.shape[1:], a.dtype)], axis=0)


def _pad_lanes(a, lanes):
    return jnp.concatenate([a, jnp.zeros(a.shape[:-1] + (lanes - a.shape[-1],), a.dtype)], axis=-1)


def _layer_params(l, small):
    dn_cw = small["dn_conv_w"][l]
    return dict(
        ln_w=small["sgu_ln_w"][l][None], ln_b=small["sgu_ln_b"][l][None],
        ws=small["sgu_w_spatial"][l], bs_t=_pad_lanes(small["sgu_b_spatial"][l].T, 128),
        sc_cw=_pad_rows(small["sc_conv_w"][l], HALO),
        dn_cw=jnp.stack([_pad_rows(dn_cw[:, j * GROUP:(j + 1) * GROUP], HALO) for j in range(3)]),
        dn_al=_pad_lanes(small["dn_a_log"][l][None], 128), dn_dt=_pad_lanes(small["dn_dt_bias"][l][None], 128),
        dn_nw=jnp.tile(small["dn_norm_w"][l][None], (1, HEADS)),
        gla_w2=_pad_rows(small["gla_w_gate2"][l], 128), gla_gb=small["gla_gate_bias"][l][None],
        gla_nw=jnp.tile(small["gla_norm_w"][l][None], (1, HEADS)),
    )


def _exchange_piece(name, grad):
    if grad.ndim == 4:
        return grad
    if name == "w_in":
        grad = _unpad_w_in(grad)
    if name in ("w_in", "w_gate_up"):
        r, c4 = grad.shape
        return jnp.transpose(grad.reshape(2, r // 2, N_CHIPS, c4 // N_CHIPS), (0, 2, 1, 3))
    r4, c = grad.shape
    return jnp.transpose(grad.reshape(N_CHIPS, 2, r4 // (2 * N_CHIPS), c), (1, 0, 2, 3))


def _reduce_on_chip(pieces):
    return _pair_add(pieces, _sibling_swap(pieces))


def _local_step(x, target, big, small, late_weights=None, exchange=False, small_extra=None):
    saved = []
    h = x
    h1 = _rmsnorm_fwd(h, small["norm1_w"][0][None], "norm1_fwd")
    for l in range(DEPTH):
        lp = _layer_params(l, small)
        p = _matmul(h1, big["w_in"][l], "nn", F32, "proj_in")
        y_a = _sgu_fwd(p, lp["ln_w"], lp["ln_b"], lp["ws"], lp["bs_t"])
        y_b = _sc_fwd(p, lp["sc_cw"])
        host1 = host2 = None
        if late_weights is not None:
            shards, finish = late_weights[l]
            modes = ["layer"] * len(shards)
            host1 = (1, shards, modes, _exchange_buffers(shards, modes))
        (y_c, st_c, inv_c), ex = _dn_fwd(p, lp["dn_cw"], lp["dn_al"], lp["dn_dt"], lp["dn_nw"], hosted=host1)
        if host1 is not None:
            host2 = (2, None, None, ex)
        (y_d, st_d), ex = _gla_fwd(p, lp["gla_w2"], lp["gla_gb"], lp["gla_nw"], hosted=host2)
        if host2 is not None:
            big = finish(big, ex)
        mix = [y_a, y_b, y_c, y_d]
        x1, h2 = _matmul(mix, big["w_out"][l], "nn", F32, "proj_out", res=h, norm_w=small["norm2_w"][l][None])
        gate, up, act = _ffn_up_swiglu(h2, big["w_gate_up"][l])
        if l + 1 < DEPTH:
            x2, h1_next = _matmul(act, big["w_down"][l], "nn", F32, "ffn_down", res=x1,
                                  norm_w=small["norm1_w"][l + 1][None])
        else:
            x2, h1_next = _matmul(act, big["w_down"][l], "nn", F32, "ffn_down", res=x1), None
        saved.append(dict(x0=h, h1=h1, p=p, st_c=st_c, inv_c=inv_c, st_d=st_d, mix=mix, x1=x1, h2=h2, gate=gate,
                          up=up, act=act, lp=lp))
        h, h1 = x2, h1_next

    dx, d_final, loss = _loss_head(h, small["final_norm_w"][None], target)
    gbig = {k: [None] * DEPTH for k in ("w_in", "w_out", "w_gate_up", "w_down")}
    gs = {k: [None] * DEPTH for k in ("norm1_w", "sgu_ln_w", "sgu_ln_b", "sgu_w_spatial", "sgu_b_spatial", "sc_conv_w",
                                     "dn_conv_w", "dn_a_log", "dn_dt_bias", "dn_norm_w", "gla_w_gate2",
                                     "gla_gate_bias", "gla_norm_w", "norm2_w")}
    carry = []
    contribs = {}
    for l in reversed(range(DEPTH)):
        s = saved[l]
        lp = s["lp"]
        gbig["w_down"][l] = _matmul(s["act"], dx, "tn", GRAD_WIRE_DTYPE, "ffn_down_dw")
        dgu = _ffn_down_dx_swiglu(dx, big["w_down"][l], s["gate"], s["up"])
        gbig["w_gate_up"][l] = _matmul(s["h2"], dgu, "tn", GRAD_WIRE_DTYPE, "ffn_up_dw",
                                       pieces="cols" if exchange else None)
        dx1, gs["norm2_w"][l] = _matmul_nt_norm_bwd(dgu, big["w_gate_up"][l], s["x1"], small["norm2_w"][l][None], dx,
                                                    "ffn_up_dx")
        gbig["w_out"][l] = _matmul(s["mix"], dx1, "tn", GRAD_WIRE_DTYPE, "proj_out_dw")
        dmix = _matmul(dx1, big["w_out"][l], "nt", F32, "proj_out_dx")
        p = s["p"]
        dpu, dpv, g_lw, g_lb, g_ws, g_bs = _sgu_bwd(p, dmix, lp["ln_w"], lp["ln_b"], lp["ws"], lp["bs_t"])
        dpb, dpc, dph, g_sc = _sc_bwd(p, dmix, lp["sc_cw"])
        host1 = host2 = None
        if exchange:
            unit = carry + [(n, l, _exchange_piece(n, gbig[n][l])) for n in ("w_out", "w_gate_up", "w_down")]
            carry = []
            summed = _reduce_on_chip([piece for _, _, piece in unit])
            modes = ["piece"] * len(summed)
            host1 = (1, summed, modes, _exchange_buffers(summed, modes))
        (dcq, dck, dcv, dcs, dcz, g_dcw, g_al, g_dt, g_dnw), ex = _dn_bwd(
            p, dmix, s["st_c"], s["inv_c"], lp["dn_cw"], lp["dn_al"], lp["dn_dt"], lp["dn_nw"], hosted=host1)
        if host1 is not None:
            host2 = (2, None, None, ex)
        (ddq, ddk, ddv, dds, ddz, g_w2, g_gb, g_gnw), ex = _gla_bwd(p, dmix, s["st_d"], lp["gla_w2"], lp["gla_gb"],
                                                                   lp["gla_nw"], hosted=host2)
        if host2 is not None:
            for (n, lay, _), got in zip(unit, ex):
                contribs[(n, lay)] = got
        dp = jnp.concatenate([dpu, dpv, dpb, dpc, dph, dcq, dck, dcv, dcz, ddq, ddk, ddv, ddz, dcs, dds], axis=1)
        gbig["w_in"][l] = _matmul(s["h1"], dp, "tn", GRAD_WIRE_DTYPE, "proj_in_dw")
        dx, gs["norm1_w"][l] = _matmul_nt_norm_bwd(dp, big["w_in"][l], s["x0"], small["norm1_w"][l][None], dx1,
                                                   "proj_in_dx")
        gs["sgu_ln_w"][l], gs["sgu_ln_b"][l] = g_lw[0], g_lb[0]
        gs["sgu_w_spatial"][l] = g_ws
        gs["sgu_b_spatial"][l] = g_bs[:, :HEADS].T
        gs["sc_conv_w"][l] = g_sc[:3]
        gs["dn_conv_w"][l] = jnp.concatenate([g_dcw[0, :4], g_dcw[1, :4], g_dcw[2, :4]], axis=1)
        gs["dn_a_log"][l], gs["dn_dt_bias"][l] = g_al[0, :HEADS], g_dt[0, :HEADS]
        gs["dn_norm_w"][l] = jnp.sum(g_dnw.reshape(HEADS, HEAD_DIM), axis=0)
        gs["gla_w_gate2"][l] = g_w2[:16]
        gs["gla_gate_bias"][l] = g_gb[0]
        gs["gla_norm_w"][l] = jnp.sum(g_gnw.reshape(HEADS, HEAD_DIM), axis=0)
        gs["norm1_w"][l] = gs["norm1_w"][l][0]
        gs["norm2_w"][l] = gs["norm2_w"][l][0]
        if exchange:
            carry = [("w_in", l, _exchange_piece("w_in", gbig["w_in"][l]))]
    gsmall = {k: jnp.stack(v) for k, v in gs.items()}
    gsmall["final_norm_w"] = d_final[0]
    if not exchange:
        return loss, dx, gbig, gsmall
    summed = _reduce_on_chip([piece for _, _, piece in carry])
    last = _chip_exchange(summed + [small_extra(gsmall, loss)], ["piece"] * len(summed) + ["whole"], "exchange_grads")
    for (n, lay, _), got in zip(carry, last):
        contribs[(n, lay)] = got
    return loss, dx, contribs, last[-1]


def _peer_chips(x, y):
    return [(1 - x, y, 2 * (1 - x) + y), (x, 1 - y, 2 * x + 1 - y), (1 - x, 1 - y, 2 * (1 - x) + 1 - y)]


def _chip_exchange(arrays, modes, name):
    na = len(arrays)
    bufs = _exchange_buffers(arrays, modes)

    def body(*refs):
        ins, outs = refs[:na], refs[2 * na:3 * na]
        send1, recv1, send2, recv2 = refs[3 * na:]
        _exchange_stage1(ins, outs, modes, send1, recv1, "start")
        _exchange_stage1(ins, outs, modes, send1, recv1, "wait")
        _exchange_stage2(outs, send2, recv2, "start")
        _exchange_stage2(outs, send2, recv2, "wait")

    any_spec = pl.BlockSpec(memory_space=pl.ANY)
    return pl.pallas_call(
        body, name=name,
        in_specs=[any_spec] * (2 * na), out_specs=[any_spec] * na,
        out_shape=[jax.ShapeDtypeStruct(b.shape, b.dtype) for b in bufs],
        input_output_aliases={na + a: a for a in range(na)},
        scratch_shapes=_stage1_sems(na) + _stage2_sems(na),
    )(*arrays, *bufs)


def _exchange_buffers(arrays, modes):
    c_idx = lax.axis_index("c")
    chip = 2 * lax.axis_index("x") + lax.axis_index("y")
    units = []
    for arr, md in zip(arrays, modes):
        if md == "layer":
            units.append(lax.dynamic_index_in_dim(arr, c_idx, 0, keepdims=False))
        elif md == "piece":
            units.append(lax.dynamic_index_in_dim(arr, chip, 0, keepdims=False))
        else:
            units.append(arr)
    any_spec = pl.BlockSpec(memory_space=pl.ANY)
    bufs = pl.pallas_call(
        lambda *refs: None, name="exchange_alloc", out_specs=[any_spec] * len(units),
        out_shape=[jax.ShapeDtypeStruct((2, N_CHIPS) + u.shape, u.dtype) for u in units],
    )()
    return [lax.dynamic_update_slice(buf, u[None, None], (c_idx, chip) + (0,) * u.ndim) for buf, u in zip(bufs, units)]


def _stage1_sems(na):
    return [pltpu.SemaphoreType.DMA((na, 3)), pltpu.SemaphoreType.DMA((na, 3))]


def _stage2_sems(na):
    return [pltpu.SemaphoreType.DMA((na,)), pltpu.SemaphoreType.DMA((na,))]


def _exchange_stage1(ins, outs, modes, send1, recv1, what):
    x, y, c = lax.axis_index("x"), lax.axis_index("y"), lax.axis_index("c")
    me = 2 * x + y
    for a in range(len(ins)):
        for k, (px, py, pidx) in enumerate(_peer_chips(x, y)):
            if modes[a] == "layer":
                src = ins[a].at[c]
            else:
                src = ins[a].at[pidx] if modes[a] == "piece" else ins[a]
            if what == "start":
                pltpu.make_async_remote_copy(
                    src_ref=src, dst_ref=outs[a].at[c, me], send_sem=send1.at[a, k], recv_sem=recv1.at[a, k],
                    device_id=(px, py, c), device_id_type=MESH).start()
            else:
                cp = pltpu.make_async_remote_copy(
                    src_ref=src, dst_ref=outs[a].at[c, pidx], send_sem=send1.at[a, k], recv_sem=recv1.at[a, k],
                    device_id=(px, py, c), device_id_type=MESH)
                cp.wait_send()
                cp.wait_recv()


def _exchange_stage2(outs, send2, recv2, what):
    x, y, c = lax.axis_index("x"), lax.axis_index("y"), lax.axis_index("c")
    sibling = (x, y, 1 - c)
    for a in range(len(outs)):
        if what == "start":
            pltpu.make_async_remote_copy(
                src_ref=outs[a].at[c], dst_ref=outs[a].at[c], send_sem=send2.at[a], recv_sem=recv2.at[a],
                device_id=sibling, device_id_type=MESH).start()
        else:
            cp = pltpu.make_async_remote_copy(
                src_ref=outs[a].at[c], dst_ref=outs[a].at[1 - c], send_sem=send2.at[a], recv_sem=recv2.at[a],
                device_id=sibling, device_id_type=MESH)
            cp.wait_send()
            cp.wait_recv()


def _sibling_swap(arrays):
    na = len(arrays)

    def body(*refs):
        ins, theirs = refs[:na], refs[na:2 * na]
        send_sems, recv_sems = refs[2 * na:]
        x, y, c = lax.axis_index("x"), lax.axis_index("y"), lax.axis_index("c")
        sibling = (x, y, 1 - c)
        for a in range(na):
            pltpu.make_async_remote_copy(
                src_ref=ins[a].at[1 - c], dst_ref=theirs[a], send_sem=send_sems.at[a], recv_sem=recv_sems.at[a],
                device_id=sibling, device_id_type=MESH).start()
        for a in range(na):
            cp = pltpu.make_async_remote_copy(
                src_ref=ins[a].at[1 - c], dst_ref=theirs[a], send_sem=send_sems.at[a], recv_sem=recv_sems.at[a],
                device_id=sibling, device_id_type=MESH)
            cp.wait_send()
            cp.wait_recv()

    any_spec = pl.BlockSpec(memory_space=pl.ANY)
    return pl.pallas_call(
        body, name="sibling_swap",
        in_specs=[any_spec] * na, out_specs=[any_spec] * na,
        out_shape=[jax.ShapeDtypeStruct(s.shape[1:], s.dtype) for s in arrays],
        scratch_shapes=[pltpu.SemaphoreType.DMA((na,)), pltpu.SemaphoreType.DMA((na,))],
    )(*arrays)


PAIR_ADD_STEPS = 8


def _pair_add(boths, theirs):
    na = len(boths)
    core = lax.axis_index("c").astype(jnp.int32).reshape(1)
    flat_b = [b.reshape(2, b.shape[1] * b.shape[2], b.shape[3]) for b in boths]
    flat_t = [t.reshape(t.shape[0] * t.shape[1], t.shape[2]) for t in theirs]
    rows = [t.shape[0] // PAIR_ADD_STEPS for t in flat_t]

    def body(core_ref, *refs):
        for a in range(na):
            refs[2 * na + a][...] = (refs[a][...].astype(F32) + refs[na + a][...].astype(F32)).astype(
                refs[2 * na + a].dtype)

    own = [pl.BlockSpec((None, r, t.shape[1]), lambda i, core_ref: (core_ref[0], i, 0)) for r, t in zip(rows, flat_t)]
    plain = [pl.BlockSpec((r, t.shape[1]), lambda i, core_ref: (i, 0)) for r, t in zip(rows, flat_t)]
    outs = pl.pallas_call(
        body, name="pair_add",
        grid_spec=pltpu.PrefetchScalarGridSpec(
            num_scalar_prefetch=1, grid=(PAIR_ADD_STEPS,), in_specs=own + plain, out_specs=plain),
        out_shape=[jax.ShapeDtypeStruct(t.shape, t.dtype) for t in flat_t],
        compiler_params=_cparams(("parallel",)),
    )(core, *flat_b, *flat_t)
    return [o.reshape(t.shape) for o, t in zip(outs, theirs)]


def _adamw_math(g, w, m, v):
    m2 = ADAM_B1 * m + (1.0 - ADAM_B1) * g
    v2 = ADAM_B2 * v + (1.0 - ADAM_B2) * (g * g)
    m_hat = m2 / (1.0 - ADAM_B1 ** ADAM_STEP)
    v_hat = v2 / (1.0 - ADAM_B2 ** ADAM_STEP)
    delta = -ADAM_LR * (m_hat / (jnp.sqrt(v_hat) + ADAM_EPS) + ADAM_WD * w)
    return delta, m2, v2


def _adamw_big(contrib0, contrib1, w, m, v, name):
    _, r, c = w.shape
    rh = r // 2
    tr = _pick_tile(rh, (256, 176, 128, 64, 8))
    nj = rh // tr
    blk = pl.BlockSpec((1, tr, c), lambda l, h, j: (l, h * nj + j, 0))

    def contrib_spec(layer, parked_h, parked_j):
        return pl.BlockSpec(
            (1, N_CHIPS, tr, c),
            lambda l, h, j: (jnp.where(l == layer, h, parked_h), 0, jnp.where(l == layer, j, parked_j), 0))

    def body(g0_ref, g1_ref, w_ref, m_ref, v_ref, go_ref, d_ref, mo_ref, vo_ref):
        def chip_sum(ref):
            g = ref[0, 0].astype(F32)
            for s in range(1, N_CHIPS):
                g = g + ref[0, s].astype(F32)
            return g

        g = jnp.where(pl.program_id(0) == 0, chip_sum(g0_ref), chip_sum(g1_ref))
        delta, m2, v2 = _adamw_math(g, w_ref[0], m_ref[0], v_ref[0])
        go_ref[0] = g
        d_ref[0] = delta
        mo_ref[0] = m2
        vo_ref[0] = v2

    return pl.pallas_call(
        body, name=name, grid=(2, 2, nj),
        in_specs=[contrib_spec(0, 1, nj - 1), contrib_spec(1, 0, 0), blk, blk, blk],
        out_specs=[blk] * 4, out_shape=[jax.ShapeDtypeStruct(w.shape, F32)] * 4,
        compiler_params=_cparams(("arbitrary", "arbitrary", "arbitrary")),
    )(contrib0, contrib1, w, m, v)


def _sum_small(contrib):
    rows = contrib.shape[2]

    def body(g_ref, o_ref):
        total = g_ref[0, 0]
        for j in range(1, N_DEV):
            total = total + g_ref[j // N_CHIPS, j % N_CHIPS]
        o_ref[...] = total

    return pl.pallas_call(
        body, name="sum_small", out_shape=jax.ShapeDtypeStruct((rows, 128), F32),
        compiler_params=_cparams(),
    )(contrib)


def _adamw_small(gs, ws, ms, vs):
    n = len(gs)
    as2d = lambda a: a.reshape(1, -1) if a.ndim == 1 else a

    def body(*refs):
        g_refs, w_refs, m_refs, v_refs = refs[:n], refs[n:2 * n], refs[2 * n:3 * n], refs[3 * n:4 * n]
        d_refs, mo_refs, vo_refs = refs[4 * n:5 * n], refs[5 * n:6 * n], refs[6 * n:]
        for j in range(n):
            delta, m2, v2 = _adamw_math(g_refs[j][...], w_refs[j][...], m_refs[j][...], v_refs[j][...])
            d_refs[j][...] = delta
            mo_refs[j][...] = m2
            vo_refs[j][...] = v2

    ins = [as2d(a) for a in (*gs, *ws, *ms, *vs)]
    outs = pl.pallas_call(
        body, name="adamw_small", out_shape=[jax.ShapeDtypeStruct(a.shape, F32) for a in ins[:n]] * 3,
        compiler_params=_cparams(),
    )(*ins)
    back = lambda group: [o.reshape(g.shape) for o, g in zip(group, gs)]
    return back(outs[:n]), back(outs[n:2 * n]), back(outs[2 * n:])


PACK_ALIGN = 8 * 128


def _packed_rows(shape):
    n = 1
    for d in shape:
        n *= d
    return (n + PACK_ALIGN - 1) // PACK_ALIGN * 8


def _pack(arrays):
    parts = []
    for a in arrays:
        flat = a.reshape(-1)
        pad = _packed_rows(a.shape) * 128 - flat.shape[0]
        if pad:
            flat = jnp.concatenate([flat, jnp.zeros((pad,), F32)])
        parts.append(flat.reshape(-1, 128))
    return jnp.concatenate(parts, axis=0)


def _unpack(packed, shapes):
    out, row = [], 0
    for s in shapes:
        rows = _packed_rows(s)
        n = 1
        for d in s:
            n *= d
        out.append(packed[row:row + rows].reshape(-1)[:n].reshape(s))
        row += rows
    return out


SMALL_NAMES = ("norm1_w", "sgu_ln_w", "sgu_ln_b", "sgu_w_spatial", "sgu_b_spatial", "sc_conv_w", "dn_conv_w",
               "dn_a_log", "dn_dt_bias", "dn_norm_w", "gla_w_gate2", "gla_gate_bias", "gla_norm_w", "norm2_w",
               "final_norm_w")
SHARDED_SMALL = ("sc_conv_w", "dn_conv_w", "gla_w_gate2")
BIG_NAMES = ("w_in", "w_out", "w_gate_up", "w_down")
WEIGHT_ORDER = ("norm1_w", "w_in", "sgu_ln_w", "sgu_ln_b", "sgu_w_spatial", "sgu_b_spatial", "sc_conv_w", "dn_conv_w",
                "dn_a_log", "dn_dt_bias", "dn_norm_w", "gla_w_gate2", "gla_gate_bias", "gla_norm_w", "w_out",
                "norm2_w", "w_gate_up", "w_down", "final_norm_w")


def _cols_from_shards(g):
    l, n, r, c = g.shape
    return jnp.transpose(g, (0, 2, 1, 3)).reshape(l, r, n * c)


def kernel(x, norm1_w, w_in, sgu_ln_w, sgu_ln_b, sgu_w_spatial, sgu_b_spatial, sc_conv_w, dn_conv_w, dn_a_log, dn_dt_bias, dn_norm_w, gla_w_gate2, gla_gate_bias, gla_norm_w, w_out, norm2_w, w_gate_up, w_down, final_norm_w, loss_target, m_norm1_w, m_w_in, m_sgu_ln_w, m_sgu_ln_b, m_sgu_w_spatial, m_sgu_b_spatial, m_sc_conv_w, m_dn_conv_w, m_dn_a_log, m_dn_dt_bias, m_dn_norm_w, m_gla_w_gate2, m_gla_gate_bias, m_gla_norm_w, m_w_out, m_norm2_w, m_w_gate_up, m_w_down, m_final_norm_w, v_norm1_w, v_w_in, v_sgu_ln_w, v_sgu_ln_b, v_sgu_w_spatial, v_sgu_b_spatial, v_sc_conv_w, v_dn_conv_w, v_dn_a_log, v_dn_dt_bias, v_dn_norm_w, v_gla_w_gate2, v_gla_gate_bias, v_gla_norm_w, v_w_out, v_norm2_w, v_w_gate_up, v_w_down, v_final_norm_w):
    w = dict(norm1_w=norm1_w, w_in=w_in, sgu_ln_w=sgu_ln_w, sgu_ln_b=sgu_ln_b, sgu_w_spatial=sgu_w_spatial,
             sgu_b_spatial=sgu_b_spatial, sc_conv_w=sc_conv_w, dn_conv_w=dn_conv_w, dn_a_log=dn_a_log,
             dn_dt_bias=dn_dt_bias, dn_norm_w=dn_norm_w, gla_w_gate2=gla_w_gate2, gla_gate_bias=gla_gate_bias,
             gla_norm_w=gla_norm_w, w_out=w_out, norm2_w=norm2_w, w_gate_up=w_gate_up, w_down=w_down,
             final_norm_w=final_norm_w)
    m = dict(norm1_w=m_norm1_w, w_in=m_w_in, sgu_ln_w=m_sgu_ln_w, sgu_ln_b=m_sgu_ln_b, sgu_w_spatial=m_sgu_w_spatial,
             sgu_b_spatial=m_sgu_b_spatial, sc_conv_w=m_sc_conv_w, dn_conv_w=m_dn_conv_w, dn_a_log=m_dn_a_log,
             dn_dt_bias=m_dn_dt_bias, dn_norm_w=m_dn_norm_w, gla_w_gate2=m_gla_w_gate2,
             gla_gate_bias=m_gla_gate_bias, gla_norm_w=m_gla_norm_w, w_out=m_w_out, norm2_w=m_norm2_w,
             w_gate_up=m_w_gate_up, w_down=m_w_down, final_norm_w=m_final_norm_w)
    v = dict(norm1_w=v_norm1_w, w_in=v_w_in, sgu_ln_w=v_sgu_ln_w, sgu_ln_b=v_sgu_ln_b, sgu_w_spatial=v_sgu_w_spatial,
             sgu_b_spatial=v_sgu_b_spatial, sc_conv_w=v_sc_conv_w, dn_conv_w=v_dn_conv_w, dn_a_log=v_dn_a_log,
             dn_dt_bias=v_dn_dt_bias, dn_norm_w=v_dn_norm_w, gla_w_gate2=v_gla_w_gate2,
             gla_gate_bias=v_gla_gate_bias, gla_norm_w=v_gla_norm_w, w_out=v_w_out, norm2_w=v_norm2_w,
             w_gate_up=v_w_gate_up, w_down=v_w_down, final_norm_w=v_final_norm_w)
    chip = 2 * lax.axis_index("x") + lax.axis_index("y")

    w_in_wire = w["w_in"].astype(MXU_DTYPE)
    row_halves = lambda a: a.reshape(2, a.shape[0] // 2, a.shape[1])

    def full_w_in(g):
        return _pad_w_in(jnp.transpose(g, (0, 2, 1, 3)).reshape(D_MODEL, IN_COLS))

    first = [row_halves(w_in_wire[0])] + [w[n] for n in SHARDED_SMALL]
    gathered = _chip_exchange(first, ["layer"] * len(first), "gather_first")
    w_in_0 = full_w_in(gathered[0])
    big = dict(w_in=[w_in_0, None])
    small = {n: w[n] for n in SMALL_NAMES if n not in SHARDED_SMALL}
    for j, n in enumerate(SHARDED_SMALL):
        small[n] = _cols_from_shards(gathered[1 + j])
    wire = {n: w[n].astype(MXU_DTYPE) for n in ("w_out", "w_gate_up", "w_down")}

    def cols_full(g):
        _, n, rh, c = g.shape
        return jnp.transpose(g, (0, 2, 1, 3)).reshape(2 * rh, n * c)

    def rows_full(g):
        _, n, rh, c = g.shape
        return jnp.transpose(g, (1, 0, 2, 3)).reshape(n * 2 * rh, c)

    def finish_0(big, g):
        return dict(w_in=[w_in_0, full_w_in(g[0])], w_out=[rows_full(g[1]), None],
                    w_gate_up=[cols_full(g[2]), None], w_down=[rows_full(g[3]), None])

    def finish_1(big, g):
        return dict(big, w_out=[big["w_out"][0], rows_full(g[0])], w_gate_up=[big["w_gate_up"][0], cols_full(g[1])],
                    w_down=[big["w_down"][0], rows_full(g[2])])

    late = [([row_halves(w_in_wire[1])] + [row_halves(wire[n][0]) for n in ("w_out", "w_gate_up", "w_down")], finish_0),
            ([row_halves(wire[n][1]) for n in ("w_out", "w_gate_up", "w_down")], finish_1)]

    small_shapes = [(DEPTH,) + w[n].shape[1:-1] + (w[n].shape[-1] * (N_CHIPS if n in SHARDED_SMALL else 1),)
                    if n != "final_norm_w" else w[n].shape for n in SMALL_NAMES] + [(1,)]

    def pack_small(gsmall, loss_tile):
        return _pack([gsmall[n] for n in SMALL_NAMES] + [loss_tile[0:1, 0]])

    _, grad_x, contribs, small_contrib = _local_step(
        x[0], loss_target[0], big, small, late_weights=late, exchange=True, small_extra=pack_small)

    out_g, out_d, out_m, out_v = {}, {}, {}, {}
    for j, n in enumerate(BIG_NAMES):
        out_g[n], out_d[n], out_m[n], out_v[n] = _adamw_big(contribs[(n, 0)], contribs[(n, 1)], w[n], m[n], v[n],
                                                            "adamw_" + n)
    summed = _unpack(_sum_small(small_contrib), small_shapes)
    loss = summed[-1][0]
    for n, g in zip(SMALL_NAMES, summed[:-1]):
        if n in SHARDED_SMALL:
            cols = g.shape[-1] // N_CHIPS
            g = lax.dynamic_slice_in_dim(g, chip * cols, cols, axis=g.ndim - 1)
        out_g[n] = g
    d_s, m_s, v_s = _adamw_small([out_g[n] for n in SMALL_NAMES], [w[n] for n in SMALL_NAMES],
                                 [m[n] for n in SMALL_NAMES], [v[n] for n in SMALL_NAMES])
    for n, d_, m_, v_ in zip(SMALL_NAMES, d_s, m_s, v_s):
        out_d[n], out_m[n], out_v[n] = d_, m_, v_

    return (loss, grad_x[None], *[out_g[n] for n in WEIGHT_ORDER], *[out_d[n] for n in WEIGHT_ORDER],
            *[out_m[n] for n in WEIGHT_ORDER], *[out_v[n] for n in WEIGHT_ORDER])
```

```python
import functools

import jax
import jax.numpy as jnp
from jax import lax
from jax.experimental import pallas as pl
from jax.experimental.pallas import tpu as pltpu

F32 = jnp.float32
BF16 = jnp.bfloat16
MXU_DTYPE = jnp.bfloat16
GRAD_WIRE_DTYPE = jnp.bfloat16
HI = lax.Precision.HIGHEST
MESH = pl.DeviceIdType.MESH

D_MODEL = 1024
DEPTH = 2
GROUP = 256
HEADS = 4
HEAD_DIM = 64
SGU_CHUNK = 128
SCAN_CHUNK = 64
D_FF = 2816
EPS = 1e-6
IN_COLS = 3352
P_COLS = 3584
HALO = 8
N_CHIPS = 4
N_DEV = 8
VMEM_LIMIT = 56 * 1024 * 1024

ADAM_LR = 0.001
ADAM_B1 = 0.9
ADAM_B2 = 0.999
ADAM_EPS = 1e-08
ADAM_WD = 0.01
ADAM_STEP = 10

(COL_AU, COL_AV, COL_BB, COL_BC, COL_BH, COL_CQ, COL_CK, COL_CV, COL_CZ,
 COL_DQ, COL_DK, COL_DV, COL_DZ) = range(13)
COL128_SMALL_C = 26
COL128_SMALL_D = 27


def _cparams(sem=None):
    return pltpu.CompilerParams(dimension_semantics=sem, vmem_limit_bytes=VMEM_LIMIT)


def _iota(shape, dim):
    return lax.broadcasted_iota(jnp.int32, shape, dim)


def _dg(a, b, ca, cb, prec=None):
    return lax.dot_general(a, b, (((ca,), (cb,)), ((), ())), preferred_element_type=F32, precision=prec)


@functools.partial(jax.custom_vjp, nondiff_argnums=(2, 3))
def bdot(a, b, ca, cb):
    return _dg(a.astype(MXU_DTYPE), b.astype(MXU_DTYPE), ca, cb)


def _bdot_fwd(a, b, ca, cb):
    return bdot(a, b, ca, cb), (a, b)


def _bdot_bwd(ca, cb, res, g):
    a, b = res
    if ca == 1:
        da = bdot(g, b, 1, 1 if cb == 0 else 0)
    else:
        da = bdot(b, g, 1 if cb == 0 else 0, 1)
    if cb == 0:
        db = bdot(a, g, 0, 0) if ca == 1 else bdot(a, g, 1, 0)
    else:
        db = bdot(g, a, 0, 0) if ca == 1 else bdot(g, a, 0, 1)
    return da, db


bdot.defvjp(_bdot_fwd, _bdot_bwd)


def _pieces(a, n):
    out, r = [], a
    for i in range(n):
        p = r.astype(MXU_DTYPE)
        out.append(p)
        if i + 1 < n:
            r = r - p.astype(F32)
    return out


def _mdot_impl(a, b, ca, cb, sa, sb):
    total = None
    for i, x in enumerate(_pieces(a, sa)):
        for j, y in enumerate(_pieces(b, sb)):
            if i + j < max(sa, sb):
                t = _dg(x, y, ca, cb)
                total = t if total is None else total + t
    return total


@functools.partial(jax.custom_vjp, nondiff_argnums=(2, 3, 4, 5))
def mdot(a, b, ca, cb, sa, sb):
    return _mdot_impl(a, b, ca, cb, sa, sb)


def _mdot_fwd(a, b, ca, cb, sa, sb):
    return _mdot_impl(a, b, ca, cb, sa, sb), (a, b)


def _mdot_bwd(ca, cb, sa, sb, res, g):
    a, b = res
    ga, gb = (3 if sb == 1 else 2), (3 if sa == 1 else 2)
    if sa == 1:
        da = jnp.zeros_like(a)
    elif ca == 1:
        da = mdot(g, b, 1, 1 if cb == 0 else 0, ga, sb)
    else:
        da = mdot(b, g, 1 if cb == 0 else 0, 1, sb, ga)
    if sb == 1:
        db = jnp.zeros_like(b)
    elif cb == 0:
        db = mdot(a, g, 0, 0, sa, gb) if ca == 1 else mdot(a, g, 1, 0, sa, gb)
    else:
        db = mdot(g, a, 0, 0, gb, sa) if ca == 1 else mdot(g, a, 0, 1, gb, sa)
    return da, db


mdot.defvjp(_mdot_fwd, _mdot_bwd)


def mask_r(a, m, ca=1, cb=0):
    return mdot(a, m, ca, cb, 3, 1)


def mask_l(m, b, ca=1, cb=0):
    return mdot(m, b, ca, cb, 1, 3)


def ddot(a, b, ca=1, cb=0):
    return mdot(a, b, ca, cb, 2, 2)


def _head_mask(h):
    return ((_iota((1, GROUP), 1) >> 6) == h).astype(F32)


def _block_diag_mask():
    return ((_iota((GROUP, GROUP), 0) >> 6) == (_iota((GROUP, GROUP), 1) >> 6)).astype(F32)


def _expand_mat(offset):
    return ((_iota((128, GROUP), 0) - offset) == (_iota((128, GROUP), 1) >> 6)).astype(F32)


def _tril(n, strict=False):
    r, c = _iota((n, n), 0), _iota((n, n), 1)
    return (r > c) if strict else (r >= c)


def _row_pick(x, row):
    return jnp.sum(jnp.where(_iota(x.shape, 0) == row, x, 0.0), axis=0, keepdims=True)


def _shift_rows_impl(x, halo, j):
    n = x.shape[0]
    r = _iota(x.shape, 0)
    top = jnp.concatenate([pltpu.roll(halo, j, 0), jnp.zeros((n - HALO, x.shape[1]), x.dtype)], axis=0)
    return jnp.where(r >= j, pltpu.roll(x, j, 0), top)


def _mxu_round(a):
    return a.astype(MXU_DTYPE).astype(F32)


@functools.partial(jax.custom_vjp, nondiff_argnums=(3,))
def _causal_conv(x, halo, w, width):
    xb, hb, wb = _mxu_round(x), _mxu_round(halo), _mxu_round(w)
    out = xb * _row_pick(wb, width - 1)
    for j in range(1, width):
        out = out + _shift_rows_impl(xb, hb, j) * _row_pick(wb, width - 1 - j)
    return out


def _causal_conv_fwd(x, halo, w, width):
    return _causal_conv(x, halo, w, width), (x, halo, w)


def _causal_conv_bwd(width, res, g):
    x, halo, w = res
    xb, hb, wb, gb = _mxu_round(x), _mxu_round(halo), _mxu_round(w), _mxu_round(g)
    n = g.shape[0]
    rows, rows8 = _iota(g.shape, 0), _iota(halo.shape, 0)
    dx = gb * _row_pick(wb, width - 1)
    dh = jnp.zeros_like(halo)
    dw = jnp.where(rows8 == width - 1, jnp.sum(xb * gb, axis=0, keepdims=True), 0.0)
    for j in range(1, width):
        gj = gb * _row_pick(wb, width - 1 - j)
        dx = dx + jnp.where(rows < n - j, pltpu.roll(gj, n - j, 0), 0.0)
        dh = dh + jnp.where(rows8 >= HALO - j, pltpu.roll(gj[0:HALO], HALO - j, 0), 0.0)
        tap = jnp.sum(_shift_rows_impl(xb, hb, j) * gb, axis=0, keepdims=True)
        dw = dw + jnp.where(rows8 == width - 1 - j, tap, 0.0)
    return dx, dh, dw


_causal_conv.defvjp(_causal_conv_fwd, _causal_conv_bwd)


def _head_sum(x, bd):
    return mask_r(x, bd)


def _softplus(x):
    return jnp.maximum(x, 0.0) + jnp.log1p(jnp.exp(-jnp.abs(x)))


def _log_sigmoid(x):
    return -_softplus(-x)


def _silu(x):
    return x * jax.nn.sigmoid(x)


def _head_rmsnorm_gate(o, nw, z, bd):
    ms = _head_sum(o * o, bd) * (1.0 / HEAD_DIM)
    return o * lax.rsqrt(ms + EPS) * nw * _silu(z)


def _sgu_chunk(pu, pv, ln_w, ln_b, ws0, ws1, ws2, ws3, bs_t):
    u = jax.nn.gelu(pu)
    g = jax.nn.gelu(pv)
    mu = jnp.mean(g, axis=-1, keepdims=True)
    var = jnp.mean(jnp.square(g - mu), axis=-1, keepdims=True)
    v = (g - mu) * lax.rsqrt(var + EPS) * ln_w + ln_b
    keep = _tril(SGU_CHUNK)
    bias = mask_r(bs_t, _expand_mat(0))
    causal_ws = [jnp.where(keep, ws, 0.0) for ws in (ws0, ws1, ws2, ws3)]
    mixed = []
    for ci in range(pu.shape[0] // SGU_CHUNK):
        v_c = v[ci * SGU_CHUNK:(ci + 1) * SGU_CHUNK]
        m_c = bias
        for h in range(HEADS):
            m_c = m_c + _head_mask(h) * bdot(causal_ws[h], v_c, 1, 0)
        mixed.append(m_c)
    return u * jnp.concatenate(mixed, axis=0)


def _sc_chunk(pb, pc, ph, halo_c, halo_h, cw):
    return pb * _causal_conv(pc * ph, halo_c * halo_h, cw, 3)


def _neumann_inverses(lows):
    n = lows[0].shape[0]
    eye = (_iota((n, n), 0) == _iota((n, n), 1)).astype(F32)
    a = [-low for low in lows]
    t = [eye + x for x in a]
    for _ in range(5):
        a = [ddot(x, x) for x in a]
        t = [ti + ddot(ti, ai) for ti, ai in zip(t, a)]
    return t


@jax.custom_vjp
def _saved_inverse(low, inv):
    return inv


def _saved_inverse_fwd(low, inv):
    return inv, inv


def _saved_inverse_bwd(inv, g):
    return -ddot(ddot(inv, g, 0, 0), inv, 1, 1), jnp.zeros_like(inv)


_saved_inverse.defvjp(_saved_inverse_fwd, _saved_inverse_bwd)


def _chunk_tril(rows):
    r, c = _iota((rows, rows), 0), _iota((rows, rows), 1)
    return ((r >> 6) == (c >> 6)) & (r >= c)


def _dn_block(pq, pk, pv, hq, hk, hv, small, pz, cwq, cwk, cwv, a_log, dt_bias, nw, state, saved_inv=None):
    c = SCAN_CHUNK
    rows = pq.shape[0]
    bd = _block_diag_mask()
    q = _silu(_causal_conv(pq, hq, cwq, 4))
    k = _silu(_causal_conv(pk, hk, cwk, 4))
    v = _silu(_causal_conv(pv, hv, cwv, 4))
    q = q * lax.rsqrt(_head_sum(q * q, bd) + EPS) * (HEAD_DIM ** -0.5)
    k = k * lax.rsqrt(_head_sum(k * k, bd) + EPS)
    lane = _iota((1, 128), 1)
    g = jnp.where(lane < HEADS, -jnp.exp(a_log) * _softplus(small + dt_bias), 0.0)
    beta_b = mask_r(jax.nn.sigmoid(small), _expand_mat(HEADS))
    gc_all = mask_l(_chunk_tril(rows).astype(F32), g)
    gcb_all = mask_r(gc_all, _expand_mat(0))
    kb_all = k * beta_b
    vb_all = v * beta_b
    kbe_all = kb_all * jnp.exp(gcb_all)
    qg_all = q * jnp.exp(gcb_all)
    causal, strict = _tril(c), _tril(c, strict=True)
    nc = rows // c
    pairs = [(ci, h) for ci in range(nc) for h in range(HEADS)]
    sls = [slice(ci * c, (ci + 1) * c) for ci in range(nc)]
    decays, lows, attns = [], [], []
    for ci, h in pairs:
        gc = gc_all[sls[ci]]
        onehot = (_iota((c, 128), 1) == h).astype(F32)
        col = mask_l(onehot, gc, 1, 1)
        row = jnp.sum(gc * onehot, axis=1, keepdims=True)
        decays.append(jnp.exp(jnp.where(causal, row - col, -jnp.inf)))
    for j, (ci, h) in enumerate(pairs):
        mh = _head_mask(h)
        k_c = k[sls[ci]]
        lows.append(jnp.where(strict, bdot(kb_all[sls[ci]] * mh, k_c, 1, 1) * decays[j], 0.0))
        attns.append(bdot(q[sls[ci]] * mh, k_c, 1, 1) * decays[j])
    if saved_inv is None:
        invs = _neumann_inverses(lows)
    else:
        invs = [_saved_inverse(low, s) for low, s in zip(lows, saved_inv)]
    us, ws = [], []
    for ci in range(nc):
        u = jnp.zeros((c, GROUP), F32)
        w = jnp.zeros((c, GROUP), F32)
        for h in range(HEADS):
            mh = _head_mask(h)
            u = u + mh * ddot(invs[ci * HEADS + h], vb_all[sls[ci]])
            w = w + mh * ddot(invs[ci * HEADS + h], kbe_all[sls[ci]])
        us.append(u)
        ws.append(w)
    outs = []
    for ci in range(nc):
        gc_b = gcb_all[sls[ci]]
        gc_last_b = _row_pick(gc_b, c - 1)
        v_new = us[ci] - bdot(ws[ci], state, 1, 0)
        o = bdot(qg_all[sls[ci]], state, 1, 0)
        for h in range(HEADS):
            o = o + _head_mask(h) * bdot(attns[ci * HEADS + h], v_new, 1, 0)
        k_dec = k[sls[ci]] * jnp.exp(gc_last_b - gc_b)
        state = state * jnp.exp(gc_last_b) + bd * bdot(k_dec, v_new, 0, 0)
        outs.append(o)
    o = jnp.concatenate(outs, axis=0)
    return _head_rmsnorm_gate(o, nw, pz, bd), state, invs


def _gla_chunk(pq, pk, pv, small, pz, w2, gbias, nw, state_t):
    c = SCAN_CHUNK
    rows = pq.shape[0]
    nc = rows // c
    sls = [slice(ci * c, (ci + 1) * c) for ci in range(nc)]
    bd = _block_diag_mask()
    log_a = _log_sigmoid(bdot(small, w2, 1, 0) + gbias) * (1.0 / 16.0)
    gcum = mask_l(_chunk_tril(rows).astype(F32), log_a)
    r, s = _iota((rows, rows), 0), _iota((rows, rows), 1)
    base = (r >> 6) << 6
    g_mid = mask_l((s == base + c // 2).astype(F32), gcum)
    g_last = mask_l((s == base + c - 1).astype(F32), gcum)
    q = pq * (HEAD_DIM ** -0.5)
    qa = q * jnp.exp(gcum - g_mid)
    ka = pk * jnp.exp(g_mid - gcum)
    qg = q * jnp.exp(gcum)
    k_last = pk * jnp.exp(g_last - gcum)
    causal = _tril(c)
    attns = [jnp.where(causal, bdot(qa[sls[ci]] * _head_mask(h), ka[sls[ci]], 1, 1), 0.0)
             for ci in range(nc) for h in range(HEADS)]
    intra = []
    for ci in range(nc):
        o = jnp.zeros((c, GROUP), F32)
        for h in range(HEADS):
            o = o + _head_mask(h) * bdot(attns[ci * HEADS + h], pv[sls[ci]], 1, 0)
        intra.append(o)
    kvs = [bd * bdot(pv[sls[ci]], k_last[sls[ci]], 0, 0) for ci in range(nc)]
    states = []
    for ci in range(nc):
        states.append(state_t)
        state_t = state_t * jnp.exp(_row_pick(g_last[sls[ci]], 0)) + kvs[ci]
    outs = [intra[ci] + bdot(qg[sls[ci]], states[ci], 1, 1) for ci in range(nc)]
    o = jnp.concatenate(outs, axis=0)
    return _head_rmsnorm_gate(o, nw, pz, bd), state_t


def _col_spec(rows, group, rev_n=None):
    if rev_n is None:
        return pl.BlockSpec((rows, GROUP), lambda i: (i, group))
    return pl.BlockSpec((rows, GROUP), lambda i: (rev_n - 1 - i, group))


def _small_spec(rows, group128, rev_n=None):
    if rev_n is None:
        return pl.BlockSpec((rows, 128), lambda i: (i, group128))
    return pl.BlockSpec((rows, 128), lambda i: (rev_n - 1 - i, group128))


def _halo_spec(rows, group, rev_n=None):
    per = rows // HALO
    if rev_n is None:
        return pl.BlockSpec((HALO, GROUP), lambda i: (jnp.maximum(i * per - 1, 0), group))
    return pl.BlockSpec((HALO, GROUP), lambda i: (jnp.maximum((rev_n - 1 - i) * per - 1, 0), group))


def _full_spec(shape):
    nd = len(shape)
    return pl.BlockSpec(shape, lambda i: (0,) * nd)


def _out_rows_spec(rows, lanes, rev_n=None):
    if rev_n is None:
        return pl.BlockSpec((rows, lanes), lambda i: (i, 0))
    return pl.BlockSpec((rows, lanes), lambda i: (rev_n - 1 - i, 0))


SGU_ROWS = 8 * SGU_CHUNK


def _sgu_fwd(p, ln_w, ln_b, ws, bs_t):
    t = p.shape[0]
    n = t // SGU_ROWS

    def body(pu_ref, pv_ref, lw_ref, lb_ref, ws_ref, bs_ref, y_ref):
        y = _sgu_chunk(pu_ref[...], pv_ref[...], lw_ref[...], lb_ref[...],
                       ws_ref[0], ws_ref[1], ws_ref[2], ws_ref[3], bs_ref[...])
        y_ref[...] = y.astype(y_ref.dtype)

    return pl.pallas_call(
        body, name="sgu_fwd", grid=(n,),
        in_specs=[_col_spec(SGU_ROWS, COL_AU), _col_spec(SGU_ROWS, COL_AV), _full_spec((1, GROUP)),
                  _full_spec((1, GROUP)), _full_spec((HEADS, SGU_CHUNK, SGU_CHUNK)), _full_spec((SGU_CHUNK, 128))],
        out_specs=_out_rows_spec(SGU_ROWS, GROUP),
        out_shape=jax.ShapeDtypeStruct((t, GROUP), BF16),
        compiler_params=_cparams(("arbitrary",)),
    )(p, p, ln_w, ln_b, ws, bs_t)


def _sgu_bwd(p, dmix, ln_w, ln_b, ws, bs_t):
    t = p.shape[0]
    n = t // SGU_ROWS

    def body(pu_ref, pv_ref, dy_ref, lw_ref, lb_ref, ws_ref, bs_ref,
             dpu_ref, dpv_ref, dlw_ref, dlb_ref, dws_ref, dbs_ref):
        args = (pu_ref[...], pv_ref[...], lw_ref[...], lb_ref[...],
                ws_ref[0], ws_ref[1], ws_ref[2], ws_ref[3], bs_ref[...])
        _, vjp = jax.vjp(_sgu_chunk, *args)
        dpu, dpv, dlw, dlb, d0, d1, d2, d3, dbs = vjp(dy_ref[...])
        dpu_ref[...] = dpu.astype(dpu_ref.dtype)
        dpv_ref[...] = dpv.astype(dpv_ref.dtype)

        @pl.when(pl.program_id(0) == 0)
        def _():
            dlw_ref[...] = jnp.zeros_like(dlw_ref)
            dlb_ref[...] = jnp.zeros_like(dlb_ref)
            dws_ref[...] = jnp.zeros_like(dws_ref)
            dbs_ref[...] = jnp.zeros_like(dbs_ref)

        dlw_ref[...] += dlw
        dlb_ref[...] += dlb
        for h, d in enumerate((d0, d1, d2, d3)):
            dws_ref[h] += d
        dbs_ref[...] += dbs

    return pl.pallas_call(
        body, name="sgu_bwd", grid=(n,),
        in_specs=[_col_spec(SGU_ROWS, COL_AU), _col_spec(SGU_ROWS, COL_AV),
                  pl.BlockSpec((SGU_ROWS, GROUP), lambda i: (i, 0)),
                  _full_spec((1, GROUP)), _full_spec((1, GROUP)), _full_spec((HEADS, SGU_CHUNK, SGU_CHUNK)),
                  _full_spec((SGU_CHUNK, 128))],
        out_specs=[_out_rows_spec(SGU_ROWS, GROUP), _out_rows_spec(SGU_ROWS, GROUP), _full_spec((1, GROUP)),
                   _full_spec((1, GROUP)), _full_spec((HEADS, SGU_CHUNK, SGU_CHUNK)), _full_spec((SGU_CHUNK, 128))],
        out_shape=[jax.ShapeDtypeStruct((t, GROUP), BF16), jax.ShapeDtypeStruct((t, GROUP), BF16),
                   jax.ShapeDtypeStruct((1, GROUP), F32), jax.ShapeDtypeStruct((1, GROUP), F32),
                   jax.ShapeDtypeStruct((HEADS, SGU_CHUNK, SGU_CHUNK), F32),
                   jax.ShapeDtypeStruct((SGU_CHUNK, 128), F32)],
        compiler_params=_cparams(("arbitrary",)),
    )(p, p, dmix, ln_w, ln_b, ws, bs_t)


SC_ROWS = 1024


def _first_block_zero(halo, first):
    return jnp.where(first, 0.0, halo)


def _sc_fwd(p, cw):
    t = p.shape[0]
    n = t // SC_ROWS

    def body(pb_ref, pc_ref, ph_ref, hc_ref, hh_ref, cw_ref, y_ref):
        first = pl.program_id(0) == 0
        y = _sc_chunk(pb_ref[...], pc_ref[...], ph_ref[...], _first_block_zero(hc_ref[...], first),
                      _first_block_zero(hh_ref[...], first), cw_ref[...])
        y_ref[...] = y.astype(y_ref.dtype)

    return pl.pallas_call(
        body, name="sc_fwd", grid=(n,),
        in_specs=[_col_spec(SC_ROWS, COL_BB), _col_spec(SC_ROWS, COL_BC), _col_spec(SC_ROWS, COL_BH),
                  _halo_spec(SC_ROWS, COL_BC), _halo_spec(SC_ROWS, COL_BH), _full_spec((HALO, GROUP))],
        out_specs=_out_rows_spec(SC_ROWS, GROUP),
        out_shape=jax.ShapeDtypeStruct((t, GROUP), BF16),
        compiler_params=_cparams(("arbitrary",)),
    )(p, p, p, p, p, cw)


def _add_halo_grad(d, carry):
    return d + jnp.concatenate([jnp.zeros((d.shape[0] - HALO, d.shape[1]), d.dtype), carry], axis=0)


def _sc_bwd(p, dmix, cw):
    t = p.shape[0]
    n = t // SC_ROWS

    def body(pb_ref, pc_ref, ph_ref, hc_ref, hh_ref, dy_ref, cw_ref,
             dpb_ref, dpc_ref, dph_ref, dcw_ref, carry_c, carry_h):
        i = pl.program_id(0)
        first = i == n - 1

        @pl.when(i == 0)
        def _():
            carry_c[...] = jnp.zeros_like(carry_c)
            carry_h[...] = jnp.zeros_like(carry_h)
            dcw_ref[...] = jnp.zeros_like(dcw_ref)

        args = (pb_ref[...], pc_ref[...], ph_ref[...], _first_block_zero(hc_ref[...], first),
                _first_block_zero(hh_ref[...], first), cw_ref[...])
        _, vjp = jax.vjp(_sc_chunk, *args)
        dpb, dpc, dph, dhc, dhh, dcw = vjp(dy_ref[...])
        dpb_ref[...] = dpb.astype(dpb_ref.dtype)
        dpc_ref[...] = _add_halo_grad(dpc, carry_c[...]).astype(dpc_ref.dtype)
        dph_ref[...] = _add_halo_grad(dph, carry_h[...]).astype(dph_ref.dtype)
        carry_c[...] = dhc
        carry_h[...] = dhh
        dcw_ref[...] += dcw

    return pl.pallas_call(
        body, name="sc_bwd", grid=(n,),
        in_specs=[_col_spec(SC_ROWS, COL_BB, n), _col_spec(SC_ROWS, COL_BC, n), _col_spec(SC_ROWS, COL_BH, n),
                  _halo_spec(SC_ROWS, COL_BC, n), _halo_spec(SC_ROWS, COL_BH, n),
                  pl.BlockSpec((SC_ROWS, GROUP), lambda i: (n - 1 - i, 1)), _full_spec((HALO, GROUP))],
        out_specs=[_out_rows_spec(SC_ROWS, GROUP, n)] * 3 + [_full_spec((HALO, GROUP))],
        out_shape=[jax.ShapeDtypeStruct((t, GROUP), BF16)] * 3 + [jax.ShapeDtypeStruct((HALO, GROUP), F32)],
        scratch_shapes=[pltpu.VMEM((HALO, GROUP), F32), pltpu.VMEM((HALO, GROUP), F32)],
        compiler_params=_cparams(("arbitrary",)),
    )(p, p, p, p, p, dmix, cw)


SCAN_STEP_CHUNKS = 4
SCAN_ROWS = SCAN_STEP_CHUNKS * SCAN_CHUNK


def _host_call(body, hosted, *, name, grid, in_specs, out_specs, out_shape, scratch_shapes, args):
    params = _cparams(("arbitrary",))
    if hosted is None:
        outs = pl.pallas_call(body, name=name, grid=grid, in_specs=in_specs, out_specs=out_specs,
                              out_shape=out_shape, scratch_shapes=scratch_shapes, compiler_params=params)(*args)
        return outs, None
    stage, arrays, modes, bufs = hosted
    arrays = list(arrays) if stage == 1 else []
    n_in, n_out, n_scr, n_src, na = len(in_specs), len(out_specs), len(scratch_shapes), len(arrays), len(bufs)
    last = grid[0] - 1

    def new_body(*refs):
        srcs = refs[n_in:n_in + n_src]
        o0 = n_in + n_src + na
        ex = refs[o0 + n_out:o0 + n_out + na]
        s0 = o0 + n_out + na
        sems = refs[s0 + n_scr:]
        i = pl.program_id(0)

        def run(what):
            if stage == 1:
                _exchange_stage1(srcs, ex, modes, sems[0], sems[1], what)
            else:
                _exchange_stage2(ex, sems[0], sems[1], what)

        @pl.when(i == 0)
        def _():
            run("start")

        body(*refs[:n_in], *refs[o0:o0 + n_out], *refs[s0:s0 + n_scr])

        @pl.when(i == last)
        def _():
            run("wait")

    any_spec = pl.BlockSpec(memory_space=pl.ANY)
    outs = pl.pallas_call(
        new_body, name=name, grid=grid,
        in_specs=list(in_specs) + [any_spec] * (n_src + na), out_specs=list(out_specs) + [any_spec] * na,
        out_shape=list(out_shape) + [jax.ShapeDtypeStruct(b.shape, b.dtype) for b in bufs],
        input_output_aliases={n_in + n_src + a: n_out + a for a in range(na)},
        scratch_shapes=list(scratch_shapes) + (_stage1_sems(na) if stage == 1 else _stage2_sems(na)),
        compiler_params=params,
    )(*args, *arrays, *bufs)
    return outs[:n_out], outs[n_out:]


def _dn_fwd(p, cw3, a_log, dt_bias, nw, hosted=None):
    t = p.shape[0]
    r = SCAN_ROWS
    n = t // r

    def body(pq_ref, pk_ref, pv_ref, hq_ref, hk_ref, hv_ref, sm_ref, pz_ref, cw_ref, al_ref, dt_ref, nw_ref,
             y_ref, ck_ref, inv_ref, state):
        first = pl.program_id(0) == 0

        @pl.when(first)
        def _():
            state[...] = jnp.zeros_like(state)

        s_in = state[...]
        ck_ref[0] = s_in
        y, s_out, invs = _dn_block(pq_ref[...], pk_ref[...], pv_ref[...], _first_block_zero(hq_ref[...], first),
                                   _first_block_zero(hk_ref[...], first), _first_block_zero(hv_ref[...], first),
                                   sm_ref[...], pz_ref[...], cw_ref[0], cw_ref[1], cw_ref[2],
                                   al_ref[...], dt_ref[...], nw_ref[...], s_in)
        y_ref[...] = y.astype(y_ref.dtype)
        state[...] = s_out
        for j, inv in enumerate(invs):
            inv_ref[j] = inv

    nh = SCAN_STEP_CHUNKS * HEADS
    return _host_call(
        body, hosted, name="dn_fwd", grid=(n,),
        in_specs=[_col_spec(r, COL_CQ), _col_spec(r, COL_CK), _col_spec(r, COL_CV),
                  _halo_spec(r, COL_CQ), _halo_spec(r, COL_CK), _halo_spec(r, COL_CV),
                  _small_spec(r, COL128_SMALL_C), _col_spec(r, COL_CZ), _full_spec((3, HALO, GROUP)),
                  _full_spec((1, 128)), _full_spec((1, 128)), _full_spec((1, GROUP))],
        out_specs=[_out_rows_spec(r, GROUP), pl.BlockSpec((1, GROUP, GROUP), lambda i: (i, 0, 0)),
                   pl.BlockSpec((nh, SCAN_CHUNK, SCAN_CHUNK), lambda i: (i, 0, 0))],
        out_shape=[jax.ShapeDtypeStruct((t, GROUP), BF16), jax.ShapeDtypeStruct((n, GROUP, GROUP), F32),
                   jax.ShapeDtypeStruct((n * nh, SCAN_CHUNK, SCAN_CHUNK), F32)],
        scratch_shapes=[pltpu.VMEM((GROUP, GROUP), F32)],
        args=(p, p, p, p, p, p, p, p, cw3, a_log, dt_bias, nw))


def _dn_bwd(p, dmix, states, invs, cw3, a_log, dt_bias, nw, hosted=None):
    t = p.shape[0]
    c = SCAN_ROWS
    n = t // c
    nh = SCAN_STEP_CHUNKS * HEADS

    def body(pq_ref, pk_ref, pv_ref, hq_ref, hk_ref, hv_ref, sm_ref, pz_ref, dy_ref, ck_ref, inv_ref,
             cw_ref, al_ref, dt_ref, nw_ref,
             dpq_ref, dpk_ref, dpv_ref, dsm_ref, dpz_ref, dcw_ref, dal_ref, ddt_ref, dnw_ref,
             dstate, carry):
        i = pl.program_id(0)
        first = i == n - 1

        @pl.when(i == 0)
        def _():
            dstate[...] = jnp.zeros_like(dstate)
            carry[...] = jnp.zeros_like(carry)
            dcw_ref[...] = jnp.zeros_like(dcw_ref)
            dal_ref[...] = jnp.zeros_like(dal_ref)
            ddt_ref[...] = jnp.zeros_like(ddt_ref)
            dnw_ref[...] = jnp.zeros_like(dnw_ref)

        args = (pq_ref[...], pk_ref[...], pv_ref[...], _first_block_zero(hq_ref[...], first),
                _first_block_zero(hk_ref[...], first), _first_block_zero(hv_ref[...], first),
                sm_ref[...], pz_ref[...], cw_ref[0], cw_ref[1], cw_ref[2],
                al_ref[...], dt_ref[...], nw_ref[...], ck_ref[0])
        saved = [inv_ref[j] for j in range(nh)]
        _, vjp = jax.vjp(lambda *a: _dn_block(*a, saved_inv=saved)[:2], *args)
        (dpq, dpk, dpv, dhq, dhk, dhv, dsm, dpz, dcq, dck, dcv, dal, ddt, dnw, dst) = vjp(
            (dy_ref[...], dstate[...]))
        dpq_ref[...] = _add_halo_grad(dpq, carry[0]).astype(dpq_ref.dtype)
        dpk_ref[...] = _add_halo_grad(dpk, carry[1]).astype(dpk_ref.dtype)
        dpv_ref[...] = _add_halo_grad(dpv, carry[2]).astype(dpv_ref.dtype)
        dsm_ref[...] = dsm.astype(dsm_ref.dtype)
        dpz_ref[...] = dpz.astype(dpz_ref.dtype)
        carry[0] = dhq
        carry[1] = dhk
        carry[2] = dhv
        dstate[...] = dst
        dcw_ref[0] += dcq
        dcw_ref[1] += dck
        dcw_ref[2] += dcv
        dal_ref[...] += dal
        ddt_ref[...] += ddt
        dnw_ref[...] += dnw

    return _host_call(
        body, hosted, name="dn_bwd", grid=(n,),
        in_specs=[_col_spec(c, COL_CQ, n), _col_spec(c, COL_CK, n), _col_spec(c, COL_CV, n),
                  _halo_spec(c, COL_CQ, n), _halo_spec(c, COL_CK, n), _halo_spec(c, COL_CV, n),
                  _small_spec(c, COL128_SMALL_C, n), _col_spec(c, COL_CZ, n),
                  pl.BlockSpec((c, GROUP), lambda i: (n - 1 - i, 2)),
                  pl.BlockSpec((1, GROUP, GROUP), lambda i: (n - 1 - i, 0, 0)),
                  pl.BlockSpec((nh, SCAN_CHUNK, SCAN_CHUNK), lambda i: (n - 1 - i, 0, 0)),
                  _full_spec((3, HALO, GROUP)), _full_spec((1, 128)), _full_spec((1, 128)), _full_spec((1, GROUP))],
        out_specs=[_out_rows_spec(c, GROUP, n)] * 3 + [_out_rows_spec(c, 128, n), _out_rows_spec(c, GROUP, n),
                   _full_spec((3, HALO, GROUP)), _full_spec((1, 128)), _full_spec((1, 128)), _full_spec((1, GROUP))],
        out_shape=[jax.ShapeDtypeStruct((t, GROUP), BF16)] * 3 + [
            jax.ShapeDtypeStruct((t, 128), BF16), jax.ShapeDtypeStruct((t, GROUP), BF16),
            jax.ShapeDtypeStruct((3, HALO, GROUP), F32), jax.ShapeDtypeStruct((1, 128), F32),
            jax.ShapeDtypeStruct((1, 128), F32), jax.ShapeDtypeStruct((1, GROUP), F32)],
        scratch_shapes=[pltpu.VMEM((GROUP, GROUP), F32), pltpu.VMEM((3, HALO, GROUP), F32)],
        args=(p, p, p, p, p, p, p, p, dmix, states, invs, cw3, a_log, dt_bias, nw))


def _gla_fwd(p, w2, gbias, nw, hosted=None):
    t = p.shape[0]
    c = SCAN_ROWS
    n = t // c

    def body(pq_ref, pk_ref, pv_ref, sm_ref, pz_ref, w2_ref, gb_ref, nw_ref, y_ref, ck_ref, state):
        @pl.when(pl.program_id(0) == 0)
        def _():
            state[...] = jnp.zeros_like(state)

        s_in = state[...]
        ck_ref[0] = s_in
        y, s_out = _gla_chunk(pq_ref[...], pk_ref[...], pv_ref[...], sm_ref[...], pz_ref[...],
                              w2_ref[...], gb_ref[...], nw_ref[...], s_in)
        y_ref[...] = y.astype(y_ref.dtype)
        state[...] = s_out

    return _host_call(
        body, hosted, name="gla_fwd", grid=(n,),
        in_specs=[_col_spec(c, COL_DQ), _col_spec(c, COL_DK), _col_spec(c, COL_DV),
                  _small_spec(c, COL128_SMALL_D), _col_spec(c, COL_DZ),
                  _full_spec((128, GROUP)), _full_spec((1, GROUP)), _full_spec((1, GROUP))],
        out_specs=[_out_rows_spec(c, GROUP), pl.BlockSpec((1, GROUP, GROUP), lambda i: (i, 0, 0))],
        out_shape=[jax.ShapeDtypeStruct((t, GROUP), BF16), jax.ShapeDtypeStruct((n, GROUP, GROUP), F32)],
        scratch_shapes=[pltpu.VMEM((GROUP, GROUP), F32)],
        args=(p, p, p, p, p, w2, gbias, nw))


def _gla_bwd(p, dmix, states, w2, gbias, nw, hosted=None):
    t = p.shape[0]
    c = SCAN_ROWS
    n = t // c

    def body(pq_ref, pk_ref, pv_ref, sm_ref, pz_ref, dy_ref, ck_ref, w2_ref, gb_ref, nw_ref,
             dpq_ref, dpk_ref, dpv_ref, dsm_ref, dpz_ref, dw2_ref, dgb_ref, dnw_ref, dstate):
        @pl.when(pl.program_id(0) == 0)
        def _():
            dstate[...] = jnp.zeros_like(dstate)
            dw2_ref[...] = jnp.zeros_like(dw2_ref)
            dgb_ref[...] = jnp.zeros_like(dgb_ref)
            dnw_ref[...] = jnp.zeros_like(dnw_ref)

        args = (pq_ref[...], pk_ref[...], pv_ref[...], sm_ref[...], pz_ref[...],
                w2_ref[...], gb_ref[...], nw_ref[...], ck_ref[0])
        _, vjp = jax.vjp(_gla_chunk, *args)
        dpq, dpk, dpv, dsm, dpz, dw2, dgb, dnw, dst = vjp((dy_ref[...], dstate[...]))
        dpq_ref[...] = dpq.astype(dpq_ref.dtype)
        dpk_ref[...] = dpk.astype(dpk_ref.dtype)
        dpv_ref[...] = dpv.astype(dpv_ref.dtype)
        dsm_ref[...] = dsm.astype(dsm_ref.dtype)
        dpz_ref[...] = dpz.astype(dpz_ref.dtype)
        dstate[...] = dst
        dw2_ref[...] += dw2
        dgb_ref[...] += dgb
        dnw_ref[...] += dnw

    return _host_call(
        body, hosted, name="gla_bwd", grid=(n,),
        in_specs=[_col_spec(c, COL_DQ, n), _col_spec(c, COL_DK, n), _col_spec(c, COL_DV, n),
                  _small_spec(c, COL128_SMALL_D, n), _col_spec(c, COL_DZ, n),
                  pl.BlockSpec((c, GROUP), lambda i: (n - 1 - i, 3)),
                  pl.BlockSpec((1, GROUP, GROUP), lambda i: (n - 1 - i, 0, 0)),
                  _full_spec((128, GROUP)), _full_spec((1, GROUP)), _full_spec((1, GROUP))],
        out_specs=[_out_rows_spec(c, GROUP, n)] * 3 + [_out_rows_spec(c, 128, n), _out_rows_spec(c, GROUP, n),
                   _full_spec((128, GROUP)), _full_spec((1, GROUP)), _full_spec((1, GROUP))],
        out_shape=[jax.ShapeDtypeStruct((t, GROUP), BF16)] * 3 + [
            jax.ShapeDtypeStruct((t, 128), BF16), jax.ShapeDtypeStruct((t, GROUP), BF16),
            jax.ShapeDtypeStruct((128, GROUP), F32), jax.ShapeDtypeStruct((1, GROUP), F32),
            jax.ShapeDtypeStruct((1, GROUP), F32)],
        scratch_shapes=[pltpu.VMEM((GROUP, GROUP), F32)],
        args=(p, p, p, p, p, dmix, states, w2, gbias, nw))


def _pick_tile(n, pref):
    for cand in pref:
        if n % cand == 0:
            return cand
    return n


MM_TILE_CAP = 1408


def _largest_tile(n, cap):
    best = None
    for mult in range(1, cap // 128 + 1):
        if n % (128 * mult) == 0:
            best = 128 * mult
    return best if best is not None else n


def _half_index(t, per_half, middle):
    half = jnp.where(t >= per_half, 1, 0)
    return half, middle, t - half * per_half


def _matmul(a, b, mode, out_dtype, name, res=None, pieces=None, norm_w=None):
    a_list = list(a) if isinstance(a, (list, tuple)) else [a]
    a_rows, a_cols = a_list[0].shape[0], sum(x.shape[1] for x in a_list)
    if mode == "nn":
        (m, k), n = (a_rows, a_cols), b.shape[1]
    elif mode == "nt":
        (m, k), n = (a_rows, a_cols), b.shape[0]
    else:
        (k, m), n = (a_rows, a_cols), b.shape[-1] * (2 if b.ndim == 3 else 1)
    tm = _largest_tile(m, MM_TILE_CAP)
    tn = _largest_tile(b.shape[-1] if b.ndim == 3 else n, MM_TILE_CAP)
    if pieces == "cols":
        tm, tn = m, n // N_CHIPS
    tk = _largest_tile(k, MM_TILE_CAP)
    if len(a_list) > 1:
        tk, tm = (k, tm) if mode == "nn" else (tk, m)
    nk = k // tk
    na = len(a_list)
    if mode == "nn":
        a_specs = [pl.BlockSpec((tm, tk if na == 1 else x.shape[1]), lambda i, j, kk: (i, kk)) for x in a_list]
        b_spec = pl.BlockSpec((tk, tn), lambda i, j, kk: (kk, j))
        dims = (1, 0)
    elif mode == "nt":
        a_specs = [pl.BlockSpec((tm, tk), lambda i, j, kk: (i, kk))]
        b_spec = pl.BlockSpec((tn, tk), lambda i, j, kk: (j, kk))
        dims = (1, 1)
    else:
        a_specs = [pl.BlockSpec((tk, tm if na == 1 else x.shape[1]), lambda i, j, kk: (kk, i)) for x in a_list]
        if b.ndim == 3:
            njh = b.shape[-1] // tn
            b_spec = pl.BlockSpec((None, tk, tn), lambda i, j, kk: _half_index(j, njh, kk))
        else:
            b_spec = pl.BlockSpec((tk, tn), lambda i, j, kk: (kk, j))
        dims = (0, 0)
    o_spec = pl.BlockSpec((tm, tn), lambda i, j, kk: (i, j))
    o_shape = (m, n)
    if pieces == "cols":
        o_spec = pl.BlockSpec((2, None, tm // 2, tn), lambda i, j, kk: (0, j, 0, 0))
        o_shape = (2, N_CHIPS, tm // 2, tn)
    has_res = res is not None
    has_norm = norm_w is not None

    def body(*refs):
        a_refs, b_ref = refs[:na], refs[na]
        pos = na + 1
        r_ref = refs[pos] if has_res else None
        pos += has_res
        nw_ref = refs[pos] if has_norm else None
        pos += has_norm
        o_ref = refs[pos]
        h_ref = refs[pos + 1] if has_norm else None
        if na == 1:
            a_val = a_refs[0][...].astype(MXU_DTYPE)
        else:
            a_val = jnp.concatenate([r[...].astype(MXU_DTYPE) for r in a_refs], axis=1)
        part = _dg(a_val, b_ref[...].astype(MXU_DTYPE), *dims)

        def finish(out):
            if has_res:
                out = out + r_ref[...]
            if pieces == "cols":
                o_ref[0] = out[:tm // 2].astype(o_ref.dtype)
                o_ref[1] = out[tm // 2:].astype(o_ref.dtype)
            else:
                o_ref[...] = out.astype(o_ref.dtype)
            if has_norm:
                r = lax.rsqrt(jnp.mean(out * out, axis=-1, keepdims=True) + EPS)
                h_ref[...] = (out * r * nw_ref[...]).astype(h_ref.dtype)

        if nk == 1:
            finish(part)
            return
        acc = refs[-1]
        kk = pl.program_id(2)

        @pl.when(kk == 0)
        def _():
            acc[...] = part

        @pl.when(kk > 0)
        def _():
            acc[...] += part

        @pl.when(kk == nk - 1)
        def _():
            finish(acc[...])

    in_specs = a_specs + [b_spec] + ([o_spec] if has_res else [])
    args = (*a_list, b) + ((res,) if has_res else ())
    out_specs, out_shape = o_spec, jax.ShapeDtypeStruct(o_shape, out_dtype)
    if has_norm:
        assert mode == "nn" and tn == n
        in_specs.append(pl.BlockSpec((1, n), lambda i, j, kk: (0, 0)))
        args += (norm_w,)
        out_specs, out_shape = [o_spec, o_spec], [out_shape, jax.ShapeDtypeStruct(o_shape, BF16)]
    return pl.pallas_call(
        body, name=name, grid=(m // tm, n // tn, nk), in_specs=in_specs, out_specs=out_specs,
        out_shape=out_shape,
        scratch_shapes=[pltpu.VMEM((tm, tn), F32)] if nk > 1 else [],
        compiler_params=_cparams(("parallel", "parallel", "arbitrary")),
    )(*args)


def _matmul_nt_norm_bwd(a, b, x, w, dres, name):
    n = b.shape[0]
    m = a.shape[-2]
    tm = _largest_tile(m, 1024)
    if a.ndim == 3:
        kh = a.shape[2]
        k = 2 * kh
        tk = _largest_tile(kh, MM_TILE_CAP)
        nkh = kh // tk
        a_spec = pl.BlockSpec((None, tm, tk), lambda i, kk: _half_index(kk, nkh, i))
    else:
        k = a.shape[1]
        tk = _largest_tile(k, MM_TILE_CAP)
        a_spec = pl.BlockSpec((tm, tk), lambda i, kk: (i, kk))
    nk = k // tk

    def body(a_ref, b_ref, x_ref, w_ref, r_ref, dx_ref, dw_ref, acc):
        i, kk = pl.program_id(0), pl.program_id(1)
        part = _dg(a_ref[...].astype(MXU_DTYPE), b_ref[...].astype(MXU_DTYPE), 1, 1)

        @pl.when(kk == 0)
        def _():
            acc[...] = part

        @pl.when(kk > 0)
        def _():
            acc[...] += part

        @pl.when((i == 0) & (kk == 0))
        def _():
            dw_ref[...] = jnp.zeros_like(dw_ref)

        @pl.when(kk == nk - 1)
        def _():
            g = acc[...]
            xv = x_ref[...]
            r = lax.rsqrt(jnp.mean(xv * xv, axis=-1, keepdims=True) + EPS)
            xhat = xv * r
            dw_ref[...] += jnp.sum(g * xhat, axis=0, keepdims=True)
            gx = g * w_ref[...]
            dx_ref[...] = r_ref[...] + r * (gx - xhat * jnp.mean(gx * xhat, axis=-1, keepdims=True))

    row_spec = pl.BlockSpec((tm, n), lambda i, kk: (i, 0))
    return pl.pallas_call(
        body, name=name, grid=(m // tm, nk),
        in_specs=[a_spec, pl.BlockSpec((n, tk), lambda i, kk: (0, kk)),
                  row_spec, pl.BlockSpec((1, n), lambda i, kk: (0, 0)), row_spec],
        out_specs=[row_spec, pl.BlockSpec((1, n), lambda i, kk: (0, 0))],
        out_shape=[jax.ShapeDtypeStruct((m, n), F32), jax.ShapeDtypeStruct((1, n), F32)],
        scratch_shapes=[pltpu.VMEM((tm, n), F32)],
        compiler_params=_cparams(("arbitrary", "arbitrary")),
    )(a, b, x, w, dres)


NORM_ROWS = 512


def _rmsnorm_fwd(x, w, name):
    t, d = x.shape

    def body(x_ref, w_ref, o_ref):
        xv = x_ref[...]
        r = lax.rsqrt(jnp.mean(xv * xv, axis=-1, keepdims=True) + EPS)
        o_ref[...] = (xv * r * w_ref[...]).astype(o_ref.dtype)

    return pl.pallas_call(
        body, name=name, grid=(t // NORM_ROWS,),
        in_specs=[pl.BlockSpec((NORM_ROWS, d), lambda i: (i, 0)), _full_spec((1, d))],
        out_specs=pl.BlockSpec((NORM_ROWS, d), lambda i: (i, 0)),
        out_shape=jax.ShapeDtypeStruct((t, d), BF16),
        compiler_params=_cparams(("parallel",)),
    )(x, w)


SWIGLU_ROWS = 128


def _ffn_up_swiglu(h, w_gate_up):
    m, k = h.shape
    tm = _largest_tile(m, 512)
    tn = _largest_tile(D_FF, MM_TILE_CAP)
    nj = D_FF // tn

    def body(a_ref, bg_ref, bu_ref, g_ref, u_ref, act_ref):
        a = a_ref[...].astype(MXU_DTYPE)
        gate = _dg(a, bg_ref[...].astype(MXU_DTYPE), 1, 0)
        up = _dg(a, bu_ref[...].astype(MXU_DTYPE), 1, 0)
        g_ref[...] = gate
        u_ref[...] = up
        act_ref[...] = (_silu(gate) * up).astype(act_ref.dtype)

    o_spec = pl.BlockSpec((tm, tn), lambda i, j: (i, j))
    return pl.pallas_call(
        body, name="ffn_up", grid=(m // tm, nj),
        in_specs=[pl.BlockSpec((tm, k), lambda i, j: (i, 0)), pl.BlockSpec((k, tn), lambda i, j: (0, j)),
                  pl.BlockSpec((k, tn), lambda i, j: (0, j + nj))],
        out_specs=[o_spec, o_spec, o_spec],
        out_shape=[jax.ShapeDtypeStruct((m, D_FF), F32), jax.ShapeDtypeStruct((m, D_FF), F32),
                   jax.ShapeDtypeStruct((m, D_FF), BF16)],
        compiler_params=_cparams(("parallel", "parallel")),
    )(h, w_gate_up, w_gate_up)


def _ffn_down_dx_swiglu(dx, w_down, gate, up):
    m, k = dx.shape
    tm = _largest_tile(m, 512)
    tn = _largest_tile(D_FF, MM_TILE_CAP)

    def body(a_ref, b_ref, g_ref, u_ref, o_ref):
        da = _dg(a_ref[...].astype(MXU_DTYPE), b_ref[...].astype(MXU_DTYPE), 1, 1)
        gate = g_ref[...]
        sg = jax.nn.sigmoid(gate)
        o_ref[0] = (da * u_ref[...] * (sg * (1.0 + gate * (1.0 - sg)))).astype(o_ref.dtype)
        o_ref[1] = (da * gate * sg).astype(o_ref.dtype)

    tile = pl.BlockSpec((tm, tn), lambda i, j: (i, j))
    return pl.pallas_call(
        body, name="ffn_down_dx", grid=(m // tm, D_FF // tn),
        in_specs=[pl.BlockSpec((tm, k), lambda i, j: (i, 0)), pl.BlockSpec((tn, k), lambda i, j: (j, 0)), tile, tile],
        out_specs=pl.BlockSpec((2, tm, tn), lambda i, j: (0, i, j)),
        out_shape=jax.ShapeDtypeStruct((2, m, D_FF), MXU_DTYPE),
        compiler_params=_cparams(("parallel", "parallel")),
    )(dx, w_down, gate, up)


def _loss_head(x, w, target):
    t, d = x.shape

    def fwd(xv, wv, tv):
        r = lax.rsqrt(jnp.mean(xv * xv, axis=-1, keepdims=True) + EPS)
        err = xv * r * wv - tv
        return 0.5 * jnp.sum(jnp.mean(err * err, axis=-1, keepdims=True), axis=0, keepdims=True)

    def body(x_ref, w_ref, t_ref, dx_ref, dw_ref, loss_ref):
        @pl.when(pl.program_id(0) == 0)
        def _():
            dw_ref[...] = jnp.zeros_like(dw_ref)
            loss_ref[...] = jnp.zeros_like(loss_ref)

        loss, vjp = jax.vjp(fwd, x_ref[...], w_ref[...], t_ref[...])
        dx, dw, _ = vjp(jnp.ones((1, 1), F32))
        dx_ref[...] = dx
        dw_ref[...] += dw
        loss_ref[...] += jnp.broadcast_to(loss, loss_ref.shape)

    return pl.pallas_call(
        body, name="loss_head", grid=(t // NORM_ROWS,),
        in_specs=[pl.BlockSpec((NORM_ROWS, d), lambda i: (i, 0)), _full_spec((1, d)),
                  pl.BlockSpec((NORM_ROWS, d), lambda i: (i, 0))],
        out_specs=[pl.BlockSpec((NORM_ROWS, d), lambda i: (i, 0)), _full_spec((1, d)), _full_spec((8, 128))],
        out_shape=[jax.ShapeDtypeStruct((t, d), F32), jax.ShapeDtypeStruct((1, d), F32),
                   jax.ShapeDtypeStruct((8, 128), F32)],
        compiler_params=_cparams(("arbitrary",)),
    )(x, w, target)


def _pad_w_in(w):
    z = lambda n: jnp.zeros((w.shape[0], n), w.dtype)
    return jnp.concatenate([w[:, 0:2048], w[:, 2056:2312], w[:, 2312:3080], w[:, 3096:3352],
                            w[:, 2048:2056], z(120), w[:, 3080:3096], z(112)], axis=1)


def _unpad_w_in(wp):
    return jnp.concatenate([wp[:, 0:2048], wp[:, 3328:3336], wp[:, 2048:2304], wp[:, 2304:3072],
                            wp[:, 3456:3472], wp[:, 3072:3328]], axis=1)


def _pad_rows(a, rows):
    return jnp.concatenate([a, jnp.zeros((rows - a.shape[0],) + a.shape[1:], a.dtype)], axis=0)


def _pad_lanes(a, lanes):
    return jnp.concatenate([a, jnp.zeros(a.shape[:-1] + (lanes - a.shape[-1],), a.dtype)], axis=-1)


def _layer_params(l, small):
    dn_cw = small["dn_conv_w"][l]
    return dict(
        ln_w=small["sgu_ln_w"][l][None], ln_b=small["sgu_ln_b"][l][None],
        ws=small["sgu_w_spatial"][l], bs_t=_pad_lanes(small["sgu_b_spatial"][l].T, 128),
        sc_cw=_pad_rows(small["sc_conv_w"][l], HALO),
        dn_cw=jnp.stack([_pad_rows(dn_cw[:, j * GROUP:(j + 1) * GROUP], HALO) for j in range(3)]),
        dn_al=_pad_lanes(small["dn_a_log"][l][None], 128), dn_dt=_pad_lanes(small["dn_dt_bias"][l][None], 128),
        dn_nw=jnp.tile(small["dn_norm_w"][l][None], (1, HEADS)),
        gla_w2=_pad_rows(small["gla_w_gate2"][l], 128), gla_gb=small["gla_gate_bias"][l][None],
        gla_nw=jnp.tile(small["gla_norm_w"][l][None], (1, HEADS)),
    )


def _exchange_piece(name, grad):
    if grad.ndim == 4:
        return grad
    if name == "w_in":
        grad = _unpad_w_in(grad)
    if name in ("w_in", "w_gate_up"):
        r, c4 = grad.shape
        return jnp.transpose(grad.reshape(2, r // 2, N_CHIPS, c4 // N_CHIPS), (0, 2, 1, 3))
    r4, c = grad.shape
    return jnp.transpose(grad.reshape(N_CHIPS, 2, r4 // (2 * N_CHIPS), c), (1, 0, 2, 3))


def _reduce_on_chip(pieces):
    return _pair_add(pieces, _sibling_swap(pieces))


def _local_step(x, target, big, small, late_weights=None, exchange=False, small_extra=None):
    saved = []
    h = x
    h1 = _rmsnorm_fwd(h, small["norm1_w"][0][None], "norm1_fwd")
    for l in range(DEPTH):
        lp = _layer_params(l, small)
        p = _matmul(h1, big["w_in"][l], "nn", F32, "proj_in")
        y_a = _sgu_fwd(p, lp["ln_w"], lp["ln_b"], lp["ws"], lp["bs_t"])
        y_b = _sc_fwd(p, lp["sc_cw"])
        host1 = host2 = None
        if late_weights is not None:
            shards, finish = late_weights[l]
            modes = ["layer"] * len(shards)
            host1 = (1, shards, modes, _exchange_buffers(shards, modes))
        (y_c, st_c, inv_c), ex = _dn_fwd(p, lp["dn_cw"], lp["dn_al"], lp["dn_dt"], lp["dn_nw"], hosted=host1)
        if host1 is not None:
            host2 = (2, None, None, ex)
        (y_d, st_d), ex = _gla_fwd(p, lp["gla_w2"], lp["gla_gb"], lp["gla_nw"], hosted=host2)
        if host2 is not None:
            big = finish(big, ex)
        mix = [y_a, y_b, y_c, y_d]
        x1, h2 = _matmul(mix, big["w_out"][l], "nn", F32, "proj_out", res=h, norm_w=small["norm2_w"][l][None])
        gate, up, act = _ffn_up_swiglu(h2, big["w_gate_up"][l])
        if l + 1 < DEPTH:
            x2, h1_next = _matmul(act, big["w_down"][l], "nn", F32, "ffn_down", res=x1,
                                  norm_w=small["norm1_w"][l + 1][None])
        else:
            x2, h1_next = _matmul(act, big["w_down"][l], "nn", F32, "ffn_down", res=x1), None
        saved.append(dict(x0=h, h1=h1, p=p, st_c=st_c, inv_c=inv_c, st_d=st_d, mix=mix, x1=x1, h2=h2, gate=gate,
                          up=up, act=act, lp=lp))
        h, h1 = x2, h1_next

    dx, d_final, loss = _loss_head(h, small["final_norm_w"][None], target)
    gbig = {k: [None] * DEPTH for k in ("w_in", "w_out", "w_gate_up", "w_down")}
    gs = {k: [None] * DEPTH for k in ("norm1_w", "sgu_ln_w", "sgu_ln_b", "sgu_w_spatial", "sgu_b_spatial", "sc_conv_w",
                                     "dn_conv_w", "dn_a_log", "dn_dt_bias", "dn_norm_w", "gla_w_gate2",
                                     "gla_gate_bias", "gla_norm_w", "norm2_w")}
    carry = []
    contribs = {}
    for l in reversed(range(DEPTH)):
        s = saved[l]
        lp = s["lp"]
        gbig["w_down"][l] = _matmul(s["act"], dx, "tn", GRAD_WIRE_DTYPE, "ffn_down_dw")
        dgu = _ffn_down_dx_swiglu(dx, big["w_down"][l], s["gate"], s["up"])
        gbig["w_gate_up"][l] = _matmul(s["h2"], dgu, "tn", GRAD_WIRE_DTYPE, "ffn_up_dw",
                                       pieces="cols" if exchange else None)
        dx1, gs["norm2_w"][l] = _matmul_nt_norm_bwd(dgu, big["w_gate_up"][l], s["x1"], small["norm2_w"][l][None], dx,
                                                    "ffn_up_dx")
        gbig["w_out"][l] = _matmul(s["mix"], dx1, "tn", GRAD_WIRE_DTYPE, "proj_out_dw")
        dmix = _matmul(dx1, big["w_out"][l], "nt", F32, "proj_out_dx")
        p = s["p"]
        dpu, dpv, g_lw, g_lb, g_ws, g_bs = _sgu_bwd(p, dmix, lp["ln_w"], lp["ln_b"], lp["ws"], lp["bs_t"])
        dpb, dpc, dph, g_sc = _sc_bwd(p, dmix, lp["sc_cw"])
        host1 = host2 = None
        if exchange:
            unit = carry + [(n, l, _exchange_piece(n, gbig[n][l])) for n in ("w_out", "w_gate_up", "w_down")]
            carry = []
            summed = _reduce_on_chip([piece for _, _, piece in unit])
            modes = ["piece"] * len(summed)
            host1 = (1, summed, modes, _exchange_buffers(summed, modes))
        (dcq, dck, dcv, dcs, dcz, g_dcw, g_al, g_dt, g_dnw), ex = _dn_bwd(
            p, dmix, s["st_c"], s["inv_c"], lp["dn_cw"], lp["dn_al"], lp["dn_dt"], lp["dn_nw"], hosted=host1)
        if host1 is not None:
            host2 = (2, None, None, ex)
        (ddq, ddk, ddv, dds, ddz, g_w2, g_gb, g_gnw), ex = _gla_bwd(p, dmix, s["st_d"], lp["gla_w2"], lp["gla_gb"],
                                                                   lp["gla_nw"], hosted=host2)
        if host2 is not None:
            for (n, lay, _), got in zip(unit, ex):
                contribs[(n, lay)] = got
        dp = jnp.concatenate([dpu, dpv, dpb, dpc, dph, dcq, dck, dcv, dcz, ddq, ddk, ddv, ddz, dcs, dds], axis=1)
        gbig["w_in"][l] = _matmul(s["h1"], dp, "tn", GRAD_WIRE_DTYPE, "proj_in_dw")
        dx, gs["norm1_w"][l] = _matmul_nt_norm_bwd(dp, big["w_in"][l], s["x0"], small["norm1_w"][l][None], dx1,
                                                   "proj_in_dx")
        gs["sgu_ln_w"][l], gs["sgu_ln_b"][l] = g_lw[0], g_lb[0]
        gs["sgu_w_spatial"][l] = g_ws
        gs["sgu_b_spatial"][l] = g_bs[:, :HEADS].T
        gs["sc_conv_w"][l] = g_sc[:3]
        gs["dn_conv_w"][l] = jnp.concatenate([g_dcw[0, :4], g_dcw[1, :4], g_dcw[2, :4]], axis=1)
        gs["dn_a_log"][l], gs["dn_dt_bias"][l] = g_al[0, :HEADS], g_dt[0, :HEADS]
        gs["dn_norm_w"][l] = jnp.sum(g_dnw.reshape(HEADS, HEAD_DIM), axis=0)
        gs["gla_w_gate2"][l] = g_w2[:16]
        gs["gla_gate_bias"][l] = g_gb[0]
        gs["gla_norm_w"][l] = jnp.sum(g_gnw.reshape(HEADS, HEAD_DIM), axis=0)
        gs["norm1_w"][l] = gs["norm1_w"][l][0]
        gs["norm2_w"][l] = gs["norm2_w"][l][0]
        if exchange:
            carry = [("w_in", l, _exchange_piece("w_in", gbig["w_in"][l]))]
    gsmall = {k: jnp.stack(v) for k, v in gs.items()}
    gsmall["final_norm_w"] = d_final[0]
    if not exchange:
        return loss, dx, gbig, gsmall
    summed = _reduce_on_chip([piece for _, _, piece in carry])
    last = _chip_exchange(summed + [small_extra(gsmall, loss)], ["piece"] * len(summed) + ["whole"], "exchange_grads")
    for (n, lay, _), got in zip(carry, last):
        contribs[(n, lay)] = got
    return loss, dx, contribs, last[-1]


def _peer_chips(x, y):
    return [(1 - x, y, 2 * (1 - x) + y), (x, 1 - y, 2 * x + 1 - y), (1 - x, 1 - y, 2 * (1 - x) + 1 - y)]


def _chip_exchange(arrays, modes, name):
    na = len(arrays)
    bufs = _exchange_buffers(arrays, modes)

    def body(*refs):
        ins, outs = refs[:na], refs[2 * na:3 * na]
        send1, recv1, send2, recv2 = refs[3 * na:]
        _exchange_stage1(ins, outs, modes, send1, recv1, "start")
        _exchange_stage1(ins, outs, modes, send1, recv1, "wait")
        _exchange_stage2(outs, send2, recv2, "start")
        _exchange_stage2(outs, send2, recv2, "wait")

    any_spec = pl.BlockSpec(memory_space=pl.ANY)
    return pl.pallas_call(
        body, name=name,
        in_specs=[any_spec] * (2 * na), out_specs=[any_spec] * na,
        out_shape=[jax.ShapeDtypeStruct(b.shape, b.dtype) for b in bufs],
        input_output_aliases={na + a: a for a in range(na)},
        scratch_shapes=_stage1_sems(na) + _stage2_sems(na),
    )(*arrays, *bufs)


def _exchange_buffers(arrays, modes):
    c_idx = lax.axis_index("c")
    chip = 2 * lax.axis_index("x") + lax.axis_index("y")
    units = []
    for arr, md in zip(arrays, modes):
        if md == "layer":
            units.append(lax.dynamic_index_in_dim(arr, c_idx, 0, keepdims=False))
        elif md == "piece":
            units.append(lax.dynamic_index_in_dim(arr, chip, 0, keepdims=False))
        else:
            units.append(arr)
    any_spec = pl.BlockSpec(memory_space=pl.ANY)
    bufs = pl.pallas_call(
        lambda *refs: None, name="exchange_alloc", out_specs=[any_spec] * len(units),
        out_shape=[jax.ShapeDtypeStruct((2, N_CHIPS) + u.shape, u.dtype) for u in units],
    )()
    return [lax.dynamic_update_slice(buf, u[None, None], (c_idx, chip) + (0,) * u.ndim) for buf, u in zip(bufs, units)]


def _stage1_sems(na):
    return [pltpu.SemaphoreType.DMA((na, 3)), pltpu.SemaphoreType.DMA((na, 3))]


def _stage2_sems(na):
    return [pltpu.SemaphoreType.DMA((na,)), pltpu.SemaphoreType.DMA((na,))]


def _exchange_stage1(ins, outs, modes, send1, recv1, what):
    x, y, c = lax.axis_index("x"), lax.axis_index("y"), lax.axis_index("c")
    me = 2 * x + y
    for a in range(len(ins)):
        for k, (px, py, pidx) in enumerate(_peer_chips(x, y)):
            if modes[a] == "layer":
                src = ins[a].at[c]
            else:
                src = ins[a].at[pidx] if modes[a] == "piece" else ins[a]
            if what == "start":
                pltpu.make_async_remote_copy(
                    src_ref=src, dst_ref=outs[a].at[c, me], send_sem=send1.at[a, k], recv_sem=recv1.at[a, k],
                    device_id=(px, py, c), device_id_type=MESH).start()
            else:
                cp = pltpu.make_async_remote_copy(
                    src_ref=src, dst_ref=outs[a].at[c, pidx], send_sem=send1.at[a, k], recv_sem=recv1.at[a, k],
                    device_id=(px, py, c), device_id_type=MESH)
                cp.wait_send()
                cp.wait_recv()


def _exchange_stage2(outs, send2, recv2, what):
    x, y, c = lax.axis_index("x"), lax.axis_index("y"), lax.axis_index("c")
    sibling = (x, y, 1 - c)
    for a in range(len(outs)):
        if what == "start":
            pltpu.make_async_remote_copy(
                src_ref=outs[a].at[c], dst_ref=outs[a].at[c], send_sem=send2.at[a], recv_sem=recv2.at[a],
                device_id=sibling, device_id_type=MESH).start()
        else:
            cp = pltpu.make_async_remote_copy(
                src_ref=outs[a].at[c], dst_ref=outs[a].at[1 - c], send_sem=send2.at[a], recv_sem=recv2.at[a],
                device_id=sibling, device_id_type=MESH)
            cp.wait_send()
            cp.wait_recv()


def _sibling_swap(arrays):
    na = len(arrays)

    def body(*refs):
        ins, theirs = refs[:na], refs[na:2 * na]
        send_sems, recv_sems = refs[2 * na:]
        x, y, c = lax.axis_index("x"), lax.axis_index("y"), lax.axis_index("c")
        sibling = (x, y, 1 - c)
        for a in range(na):
            pltpu.make_async_remote_copy(
                src_ref=ins[a].at[1 - c], dst_ref=theirs[a], send_sem=send_sems.at[a], recv_sem=recv_sems.at[a],
                device_id=sibling, device_id_type=MESH).start()
        for a in range(na):
            cp = pltpu.make_async_remote_copy(
                src_ref=ins[a].at[1 - c], dst_ref=theirs[a], send_sem=send_sems.at[a], recv_sem=recv_sems.at[a],
                device_id=sibling, device_id_type=MESH)
            cp.wait_send()
            cp.wait_recv()

    any_spec = pl.BlockSpec(memory_space=pl.ANY)
    return pl.pallas_call(
        body, name="sibling_swap",
        in_specs=[any_spec] * na, out_specs=[any_spec] * na,
        out_shape=[jax.ShapeDtypeStruct(s.shape[1:], s.dtype) for s in arrays],
        scratch_shapes=[pltpu.SemaphoreType.DMA((na,)), pltpu.SemaphoreType.DMA((na,))],
    )(*arrays)


PAIR_ADD_STEPS = 8


def _pair_add(boths, theirs):
    na = len(boths)
    core = lax.axis_index("c").astype(jnp.int32).reshape(1)
    flat_b = [b.reshape(2, b.shape[1] * b.shape[2], b.shape[3]) for b in boths]
    flat_t = [t.reshape(t.shape[0] * t.shape[1], t.shape[2]) for t in theirs]
    rows = [t.shape[0] // PAIR_ADD_STEPS for t in flat_t]

    def body(core_ref, *refs):
        for a in range(na):
            refs[2 * na + a][...] = (refs[a][...].astype(F32) + refs[na + a][...].astype(F32)).astype(
                refs[2 * na + a].dtype)

    own = [pl.BlockSpec((None, r, t.shape[1]), lambda i, core_ref: (core_ref[0], i, 0)) for r, t in zip(rows, flat_t)]
    plain = [pl.BlockSpec((r, t.shape[1]), lambda i, core_ref: (i, 0)) for r, t in zip(rows, flat_t)]
    outs = pl.pallas_call(
        body, name="pair_add",
        grid_spec=pltpu.PrefetchScalarGridSpec(
            num_scalar_prefetch=1, grid=(PAIR_ADD_STEPS,), in_specs=own + plain, out_specs=plain),
        out_shape=[jax.ShapeDtypeStruct(t.shape, t.dtype) for t in flat_t],
        compiler_params=_cparams(("parallel",)),
    )(core, *flat_b, *flat_t)
    return [o.reshape(t.shape) for o, t in zip(outs, theirs)]


def _adamw_math(g, w, m, v):
    m2 = ADAM_B1 * m + (1.0 - ADAM_B1) * g
    v2 = ADAM_B2 * v + (1.0 - ADAM_B2) * (g * g)
    m_hat = m2 / (1.0 - ADAM_B1 ** ADAM_STEP)
    v_hat = v2 / (1.0 - ADAM_B2 ** ADAM_STEP)
    delta = -ADAM_LR * (m_hat / (jnp.sqrt(v_hat) + ADAM_EPS) + ADAM_WD * w)
    return delta, m2, v2


def _adamw_big(contrib0, contrib1, w, m, v, name):
    _, r, c = w.shape
    rh = r // 2
    tr = _pick_tile(rh, (256, 176, 128, 64, 8))
    nj = rh // tr
    blk = pl.BlockSpec((1, tr, c), lambda l, h, j: (l, h * nj + j, 0))

    def contrib_spec(layer, parked_h, parked_j):
        return pl.BlockSpec(
            (1, N_CHIPS, tr, c),
            lambda l, h, j: (jnp.where(l == layer, h, parked_h), 0, jnp.where(l == layer, j, parked_j), 0))

    def body(g0_ref, g1_ref, w_ref, m_ref, v_ref, go_ref, d_ref, mo_ref, vo_ref):
        def chip_sum(ref):
            g = ref[0, 0].astype(F32)
            for s in range(1, N_CHIPS):
                g = g + ref[0, s].astype(F32)
            return g

        g = jnp.where(pl.program_id(0) == 0, chip_sum(g0_ref), chip_sum(g1_ref))
        delta, m2, v2 = _adamw_math(g, w_ref[0], m_ref[0], v_ref[0])
        go_ref[0] = g
        d_ref[0] = delta
        mo_ref[0] = m2
        vo_ref[0] = v2

    return pl.pallas_call(
        body, name=name, grid=(2, 2, nj),
        in_specs=[contrib_spec(0, 1, nj - 1), contrib_spec(1, 0, 0), blk, blk, blk],
        out_specs=[blk] * 4, out_shape=[jax.ShapeDtypeStruct(w.shape, F32)] * 4,
        compiler_params=_cparams(("arbitrary", "arbitrary", "arbitrary")),
    )(contrib0, contrib1, w, m, v)


def _sum_small(contrib):
    rows = contrib.shape[2]

    def body(g_ref, o_ref):
        total = g_ref[0, 0]
        for j in range(1, N_DEV):
            total = total + g_ref[j // N_CHIPS, j % N_CHIPS]
        o_ref[...] = total

    return pl.pallas_call(
        body, name="sum_small", out_shape=jax.ShapeDtypeStruct((rows, 128), F32),
        compiler_params=_cparams(),
    )(contrib)


def _adamw_small(gs, ws, ms, vs):
    n = len(gs)
    as2d = lambda a: a.reshape(1, -1) if a.ndim == 1 else a

    def body(*refs):
        g_refs, w_refs, m_refs, v_refs = refs[:n], refs[n:2 * n], refs[2 * n:3 * n], refs[3 * n:4 * n]
        d_refs, mo_refs, vo_refs = refs[4 * n:5 * n], refs[5 * n:6 * n], refs[6 * n:]
        for j in range(n):
            delta, m2, v2 = _adamw_math(g_refs[j][...], w_refs[j][...], m_refs[j][...], v_refs[j][...])
            d_refs[j][...] = delta
            mo_refs[j][...] = m2
            vo_refs[j][...] = v2

    ins = [as2d(a) for a in (*gs, *ws, *ms, *vs)]
    outs = pl.pallas_call(
        body, name="adamw_small", out_shape=[jax.ShapeDtypeStruct(a.shape, F32) for a in ins[:n]] * 3,
        compiler_params=_cparams(),
    )(*ins)
    back = lambda group: [o.reshape(g.shape) for o, g in zip(group, gs)]
    return back(outs[:n]), back(outs[n:2 * n]), back(outs[2 * n:])


PACK_ALIGN = 8 * 128


def _packed_rows(shape):
    n = 1
    for d in shape:
        n *= d
    return (n + PACK_ALIGN - 1) // PACK_ALIGN * 8


def _pack(arrays):
    parts = []
    for a in arrays:
        flat = a.reshape(-1)
        pad = _packed_rows(a.shape) * 128 - flat.shape[0]
        if pad:
            flat = jnp.concatenate([flat, jnp.zeros((pad,), F32)])
        parts.append(flat.reshape(-1, 128))
    return jnp.concatenate(parts, axis=0)


def _unpack(packed, shapes):
    out, row = [], 0
    for s in shapes:
        rows = _packed_rows(s)
        n = 1
        for d in s:
            n *= d
        out.append(packed[row:row + rows].reshape(-1)[:n].reshape(s))
        row += rows
    return out


SMALL_NAMES = ("norm1_w", "sgu_ln_w", "sgu_ln_b", "sgu_w_spatial", "sgu_b_spatial", "sc_conv_w", "dn_conv_w",
               "dn_a_log", "dn_dt_bias", "dn_norm_w", "gla_w_gate2", "gla_gate_bias", "gla_norm_w", "norm2_w",
               "final_norm_w")
SHARDED_SMALL = ("sc_conv_w", "dn_conv_w", "gla_w_gate2")
BIG_NAMES = ("w_in", "w_out", "w_gate_up", "w_down")
WEIGHT_ORDER = ("norm1_w", "w_in", "sgu_ln_w", "sgu_ln_b", "sgu_w_spatial", "sgu_b_spatial", "sc_conv_w", "dn_conv_w",
                "dn_a_log", "dn_dt_bias", "dn_norm_w", "gla_w_gate2", "gla_gate_bias", "gla_norm_w", "w_out",
                "norm2_w", "w_gate_up", "w_down", "final_norm_w")


def _cols_from_shards(g):
    l, n, r, c = g.shape
    return jnp.transpose(g, (0, 2, 1, 3)).reshape(l, r, n * c)


def kernel(x, norm1_w, w_in, sgu_ln_w, sgu_ln_b, sgu_w_spatial, sgu_b_spatial, sc_conv_w, dn_conv_w, dn_a_log, dn_dt_bias, dn_norm_w, gla_w_gate2, gla_gate_bias, gla_norm_w, w_out, norm2_w, w_gate_up, w_down, final_norm_w, loss_target, m_norm1_w, m_w_in, m_sgu_ln_w, m_sgu_ln_b, m_sgu_w_spatial, m_sgu_b_spatial, m_sc_conv_w, m_dn_conv_w, m_dn_a_log, m_dn_dt_bias, m_dn_norm_w, m_gla_w_gate2, m_gla_gate_bias, m_gla_norm_w, m_w_out, m_norm2_w, m_w_gate_up, m_w_down, m_final_norm_w, v_norm1_w, v_w_in, v_sgu_ln_w, v_sgu_ln_b, v_sgu_w_spatial, v_sgu_b_spatial, v_sc_conv_w, v_dn_conv_w, v_dn_a_log, v_dn_dt_bias, v_dn_norm_w, v_gla_w_gate2, v_gla_gate_bias, v_gla_norm_w, v_w_out, v_norm2_w, v_w_gate_up, v_w_down, v_final_norm_w):
    w = dict(norm1_w=norm1_w, w_in=w_in, sgu_ln_w=sgu_ln_w, sgu_ln_b=sgu_ln_b, sgu_w_spatial=sgu_w_spatial,
             sgu_b_spatial=sgu_b_spatial, sc_conv_w=sc_conv_w, dn_conv_w=dn_conv_w, dn_a_log=dn_a_log,
             dn_dt_bias=dn_dt_bias, dn_norm_w=dn_norm_w, gla_w_gate2=gla_w_gate2, gla_gate_bias=gla_gate_bias,
             gla_norm_w=gla_norm_w, w_out=w_out, norm2_w=norm2_w, w_gate_up=w_gate_up, w_down=w_down,
             final_norm_w=final_norm_w)
    m = dict(norm1_w=m_norm1_w, w_in=m_w_in, sgu_ln_w=m_sgu_ln_w, sgu_ln_b=m_sgu_ln_b, sgu_w_spatial=m_sgu_w_spatial,
             sgu_b_spatial=m_sgu_b_spatial, sc_conv_w=m_sc_conv_w, dn_conv_w=m_dn_conv_w, dn_a_log=m_dn_a_log,
             dn_dt_bias=m_dn_dt_bias, dn_norm_w=m_dn_norm_w, gla_w_gate2=m_gla_w_gate2,
             gla_gate_bias=m_gla_gate_bias, gla_norm_w=m_gla_norm_w, w_out=m_w_out, norm2_w=m_norm2_w,
             w_gate_up=m_w_gate_up, w_down=m_w_down, final_norm_w=m_final_norm_w)
    v = dict(norm1_w=v_norm1_w, w_in=v_w_in, sgu_ln_w=v_sgu_ln_w, sgu_ln_b=v_sgu_ln_b, sgu_w_spatial=v_sgu_w_spatial,
             sgu_b_spatial=v_sgu_b_spatial, sc_conv_w=v_sc_conv_w, dn_conv_w=v_dn_conv_w, dn_a_log=v_dn_a_log,
             dn_dt_bias=v_dn_dt_bias, dn_norm_w=v_dn_norm_w, gla_w_gate2=v_gla_w_gate2,
             gla_gate_bias=v_gla_gate_bias, gla_norm_w=v_gla_norm_w, w_out=v_w_out, norm2_w=v_norm2_w,
             w_gate_up=v_w_gate_up, w_down=v_w_down, final_norm_w=v_final_norm_w)
    chip = 2 * lax.axis_index("x") + lax.axis_index("y")

    w_in_wire = w["w_in"].astype(MXU_DTYPE)
    row_halves = lambda a: a.reshape(2, a.shape[0] // 2, a.shape[1])

    def full_w_in(g):
        return _pad_w_in(jnp.transpose(g, (0, 2, 1, 3)).reshape(D_MODEL, IN_COLS))

    first = [row_halves(w_in_wire[0])] + [w[n] for n in SHARDED_SMALL]
    gathered = _chip_exchange(first, ["layer"] * len(first), "gather_first")
    w_in_0 = full_w_in(gathered[0])
    big = dict(w_in=[w_in_0, None])
    small = {n: w[n] for n in SMALL_NAMES if n not in SHARDED_SMALL}
    for j, n in enumerate(SHARDED_SMALL):
        small[n] = _cols_from_shards(gathered[1 + j])
    wire = {n: w[n].astype(MXU_DTYPE) for n in ("w_out", "w_gate_up", "w_down")}

    def cols_full(g):
        _, n, rh, c = g.shape
        return jnp.transpose(g, (0, 2, 1, 3)).reshape(2 * rh, n * c)

    def rows_full(g):
        _, n, rh, c = g.shape
        return jnp.transpose(g, (1, 0, 2, 3)).reshape(n * 2 * rh, c)

    def finish_0(big, g):
        return dict(w_in=[w_in_0, full_w_in(g[0])], w_out=[rows_full(g[1]), None],
                    w_gate_up=[cols_full(g[2]), None], w_down=[rows_full(g[3]), None])

    def finish_1(big, g):
        return dict(big, w_out=[big["w_out"][0], rows_full(g[0])], w_gate_up=[big["w_gate_up"][0], cols_full(g[1])],
                    w_down=[big["w_down"][0], rows_full(g[2])])

    late = [([row_halves(w_in_wire[1])] + [row_halves(wire[n][0]) for n in ("w_out", "w_gate_up", "w_down")], finish_0),
            ([row_halves(wire[n][1]) for n in ("w_out", "w_gate_up", "w_down")], finish_1)]

    small_shapes = [(DEPTH,) + w[n].shape[1:-1] + (w[n].shape[-1] * (N_CHIPS if n in SHARDED_SMALL else 1),)
                    if n != "final_norm_w" else w[n].shape for n in SMALL_NAMES] + [(1,)]

    def pack_small(gsmall, loss_tile):
        return _pack([gsmall[n] for n in SMALL_NAMES] + [loss_tile[0:1, 0]])

    _, grad_x, contribs, small_contrib = _local_step(
        x[0], loss_target[0], big, small, late_weights=late, exchange=True, small_extra=pack_small)

    out_g, out_d, out_m, out_v = {}, {}, {}, {}
    for j, n in enumerate(BIG_NAMES):
        out_g[n], out_d[n], out_m[n], out_v[n] = _adamw_big(contribs[(n, 0)], contribs[(n, 1)], w[n], m[n], v[n],
                                                            "adamw_" + n)
    summed = _unpack(_sum_small(small_contrib), small_shapes)
    loss = summed[-1][0]
    for n, g in zip(SMALL_NAMES, summed[:-1]):
        if n in SHARDED_SMALL:
            cols = g.shape[-1] // N_CHIPS
            g = lax.dynamic_slice_in_dim(g, chip * cols, cols, axis=g.ndim - 1)
        out_g[n] = g
    d_s, m_s, v_s = _adamw_small([out_g[n] for n in SMALL_NAMES], [w[n] for n in SMALL_NAMES],
                                 [m[n] for n in SMALL_NAMES], [v[n] for n in SMALL_NAMES])
    for n, d_, m_, v_ in zip(SMALL_NAMES, d_s, m_s, v_s):
        out_d[n], out_m[n], out_v[n] = d_, m_, v_

    return (loss, grad_x[None], *[out_g[n] for n in WEIGHT_ORDER], *[out_d[n] for n in WEIGHT_ORDER],
            *[out_m[n] for n in WEIGHT_ORDER], *[out_v[n] for n in WEIGHT_ORDER])
```

```python
import functools

import jax
import jax.numpy as jnp
from jax import lax
from jax.experimental import pallas as pl
from jax.experimental.pallas import tpu as pltpu

F32 = jnp.float32
BF16 = jnp.bfloat16
MXU_DTYPE = jnp.bfloat16
GRAD_WIRE_DTYPE = jnp.bfloat16
HI = lax.Precision.HIGHEST
MESH = pl.DeviceIdType.MESH

D_MODEL = 1024
DEPTH = 2
GROUP = 256
HEADS = 4
HEAD_DIM = 64
SGU_CHUNK = 128
SCAN_CHUNK = 64
D_FF = 2816
EPS = 1e-6
IN_COLS = 3352
P_COLS = 3584
HALO = 8
N_CHIPS = 4
N_DEV = 8
VMEM_LIMIT = 56 * 1024 * 1024

ADAM_LR = 0.001
ADAM_B1 = 0.9
ADAM_B2 = 0.999
ADAM_EPS = 1e-08
ADAM_WD = 0.01
ADAM_STEP = 10

(COL_AU, COL_AV, COL_BB, COL_BC, COL_BH, COL_CQ, COL_CK, COL_CV, COL_CZ,
 COL_DQ, COL_DK, COL_DV, COL_DZ) = range(13)
COL128_SMALL_C = 26
COL128_SMALL_D = 27


def _cparams(sem=None):
    return pltpu.CompilerParams(dimension_semantics=sem, vmem_limit_bytes=VMEM_LIMIT)


def _iota(shape, dim):
    return lax.broadcasted_iota(jnp.int32, shape, dim)


def _dg(a, b, ca, cb, prec=None):
    return lax.dot_general(a, b, (((ca,), (cb,)), ((), ())), preferred_element_type=F32, precision=prec)


@functools.partial(jax.custom_vjp, nondiff_argnums=(2, 3))
def bdot(a, b, ca, cb):
    return _dg(a.astype(MXU_DTYPE), b.astype(MXU_DTYPE), ca, cb)


def _bdot_fwd(a, b, ca, cb):
    return bdot(a, b, ca, cb), (a, b)


def _bdot_bwd(ca, cb, res, g):
    a, b = res
    if ca == 1:
        da = bdot(g, b, 1, 1 if cb == 0 else 0)
    else:
        da = bdot(b, g, 1 if cb == 0 else 0, 1)
    if cb == 0:
        db = bdot(a, g, 0, 0) if ca == 1 else bdot(a, g, 1, 0)
    else:
        db = bdot(g, a, 0, 0) if ca == 1 else bdot(g, a, 0, 1)
    return da, db


bdot.defvjp(_bdot_fwd, _bdot_bwd)


def _pieces(a, n):
    out, r = [], a
    for i in range(n):
        p = r.astype(MXU_DTYPE)
        out.append(p)
        if i + 1 < n:
            r = r - p.astype(F32)
    return out


def _mdot_impl(a, b, ca, cb, sa, sb):
    total = None
    for i, x in enumerate(_pieces(a, sa)):
        for j, y in enumerate(_pieces(b, sb)):
            if i + j < max(sa, sb):
                t = _dg(x, y, ca, cb)
                total = t if total is None else total + t
    return total


@functools.partial(jax.custom_vjp, nondiff_argnums=(2, 3, 4, 5))
def mdot(a, b, ca, cb, sa, sb):
    return _mdot_impl(a, b, ca, cb, sa, sb)


def _mdot_fwd(a, b, ca, cb, sa, sb):
    return _mdot_impl(a, b, ca, cb, sa, sb), (a, b)


def _mdot_bwd(ca, cb, sa, sb, res, g):
    a, b = res
    ga, gb = (3 if sb == 1 else 2), (3 if sa == 1 else 2)
    if sa == 1:
        da = jnp.zeros_like(a)
    elif ca == 1:
        da = mdot(g, b, 1, 1 if cb == 0 else 0, ga, sb)
    else:
        da = mdot(b, g, 1 if cb == 0 else 0, 1, sb, ga)
    if sb == 1:
        db = jnp.zeros_like(b)
    elif cb == 0:
        db = mdot(a, g, 0, 0, sa, gb) if ca == 1 else mdot(a, g, 1, 0, sa, gb)
    else:
        db = mdot(g, a, 0, 0, gb, sa) if ca == 1 else mdot(g, a, 0, 1, gb, sa)
    return da, db


mdot.defvjp(_mdot_fwd, _mdot_bwd)


def mask_r(a, m, ca=1, cb=0):
    return mdot(a, m, ca, cb, 3, 1)


def mask_l(m, b, ca=1, cb=0):
    return mdot(m, b, ca, cb, 1, 3)


def ddot(a, b, ca=1, cb=0):
    return mdot(a, b, ca, cb, 2, 2)


def _head_mask(h):
    return ((_iota((1, GROUP), 1) >> 6) == h).astype(F32)


def _block_diag_mask():
    return ((_iota((GROUP, GROUP), 0) >> 6) == (_iota((GROUP, GROUP), 1) >> 6)).astype(F32)


def _expand_mat(offset):
    return ((_iota((128, GROUP), 0) - offset) == (_iota((128, GROUP), 1) >> 6)).astype(F32)


def _tril(n, strict=False):
    r, c = _iota((n, n), 0), _iota((n, n), 1)
    return (r > c) if strict else (r >= c)


def _row_pick(x, row):
    return jnp.sum(jnp.where(_iota(x.shape, 0) == row, x, 0.0), axis=0, keepdims=True)


def _shift_rows_impl(x, halo, j):
    n = x.shape[0]
    r = _iota(x.shape, 0)
    top = jnp.concatenate([pltpu.roll(halo, j, 0), jnp.zeros((n - HALO, x.shape[1]), x.dtype)], axis=0)
    return jnp.where(r >= j, pltpu.roll(x, j, 0), top)


def _mxu_round(a):
    return a.astype(MXU_DTYPE).astype(F32)


@functools.partial(jax.custom_vjp, nondiff_argnums=(3,))
def _causal_conv(x, halo, w, width):
    xb, hb, wb = _mxu_round(x), _mxu_round(halo), _mxu_round(w)
    out = xb * _row_pick(wb, width - 1)
    for j in range(1, width):
        out = out + _shift_rows_impl(xb, hb, j) * _row_pick(wb, width - 1 - j)
    return out


def _causal_conv_fwd(x, halo, w, width):
    return _causal_conv(x, halo, w, width), (x, halo, w)


def _causal_conv_bwd(width, res, g):
    x, halo, w = res
    xb, hb, wb, gb = _mxu_round(x), _mxu_round(halo), _mxu_round(w), _mxu_round(g)
    n = g.shape[0]
    rows, rows8 = _iota(g.shape, 0), _iota(halo.shape, 0)
    dx = gb * _row_pick(wb, width - 1)
    dh = jnp.zeros_like(halo)
    dw = jnp.where(rows8 == width - 1, jnp.sum(xb * gb, axis=0, keepdims=True), 0.0)
    for j in range(1, width):
        gj = gb * _row_pick(wb, width - 1 - j)
        dx = dx + jnp.where(rows < n - j, pltpu.roll(gj, n - j, 0), 0.0)
        dh = dh + jnp.where(rows8 >= HALO - j, pltpu.roll(gj[0:HALO], HALO - j, 0), 0.0)
        tap = jnp.sum(_shift_rows_impl(xb, hb, j) * gb, axis=0, keepdims=True)
        dw = dw + jnp.where(rows8 == width - 1 - j, tap, 0.0)
    return dx, dh, dw


_causal_conv.defvjp(_causal_conv_fwd, _causal_conv_bwd)


def _head_sum(x, bd):
    return mask_r(x, bd)


def _softplus(x):
    return jnp.maximum(x, 0.0) + jnp.log1p(jnp.exp(-jnp.abs(x)))


def _log_sigmoid(x):
    return -_softplus(-x)


def _silu(x):
    return x * jax.nn.sigmoid(x)


def _head_rmsnorm_gate(o, nw, z, bd):
    ms = _head_sum(o * o, bd) * (1.0 / HEAD_DIM)
    return o * lax.rsqrt(ms + EPS) * nw * _silu(z)


def _sgu_chunk(pu, pv, ln_w, ln_b, ws0, ws1, ws2, ws3, bs_t):
    u = jax.nn.gelu(pu)
    g = jax.nn.gelu(pv)
    mu = jnp.mean(g, axis=-1, keepdims=True)
    var = jnp.mean(jnp.square(g - mu), axis=-1, keepdims=True)
    v = (g - mu) * lax.rsqrt(var + EPS) * ln_w + ln_b
    keep = _tril(SGU_CHUNK)
    bias = mask_r(bs_t, _expand_mat(0))
    causal_ws = [jnp.where(keep, ws, 0.0) for ws in (ws0, ws1, ws2, ws3)]
    mixed = []
    for ci in range(pu.shape[0] // SGU_CHUNK):
        v_c = v[ci * SGU_CHUNK:(ci + 1) * SGU_CHUNK]
        m_c = bias
        for h in range(HEADS):
            m_c = m_c + _head_mask(h) * bdot(causal_ws[h], v_c, 1, 0)
        mixed.append(m_c)
    return u * jnp.concatenate(mixed, axis=0)


def _sc_chunk(pb, pc, ph, halo_c, halo_h, cw):
    return pb * _causal_conv(pc * ph, halo_c * halo_h, cw, 3)


def _neumann_inverses(lows):
    n = lows[0].shape[0]
    eye = (_iota((n, n), 0) == _iota((n, n), 1)).astype(F32)
    a = [-low for low in lows]
    t = [eye + x for x in a]
    for _ in range(5):
        a = [ddot(x, x) for x in a]
        t = [ti + ddot(ti, ai) for ti, ai in zip(t, a)]
    return t


@jax.custom_vjp
def _saved_inverse(low, inv):
    return inv


def _saved_inverse_fwd(low, inv):
    return inv, inv


def _saved_inverse_bwd(inv, g):
    return -ddot(ddot(inv, g, 0, 0), inv, 1, 1), jnp.zeros_like(inv)


_saved_inverse.defvjp(_saved_inverse_fwd, _saved_inverse_bwd)


def _chunk_tril(rows):
    r, c = _iota((rows, rows), 0), _iota((rows, rows), 1)
    return ((r >> 6) == (c >> 6)) & (r >= c)


def _dn_block(pq, pk, pv, hq, hk, hv, small, pz, cwq, cwk, cwv, a_log, dt_bias, nw, state, saved_inv=None):
    c = SCAN_CHUNK
    rows = pq.shape[0]
    bd = _block_diag_mask()
    q = _silu(_causal_conv(pq, hq, cwq, 4))
    k = _silu(_causal_conv(pk, hk, cwk, 4))
    v = _silu(_causal_conv(pv, hv, cwv, 4))
    q = q * lax.rsqrt(_head_sum(q * q, bd) + EPS) * (HEAD_DIM ** -0.5)
    k = k * lax.rsqrt(_head_sum(k * k, bd) + EPS)
    lane = _iota((1, 128), 1)
    g = jnp.where(lane < HEADS, -jnp.exp(a_log) * _softplus(small + dt_bias), 0.0)
    beta_b = mask_r(jax.nn.sigmoid(small), _expand_mat(HEADS))
    gc_all = mask_l(_chunk_tril(rows).astype(F32), g)
    gcb_all = mask_r(gc_all, _expand_mat(0))
    kb_all = k * beta_b
    vb_all = v * beta_b
    kbe_all = kb_all * jnp.exp(gcb_all)
    qg_all = q * jnp.exp(gcb_all)
    causal, strict = _tril(c), _tril(c, strict=True)
    nc = rows // c
    pairs = [(ci, h) for ci in range(nc) for h in range(HEADS)]
    sls = [slice(ci * c, (ci + 1) * c) for ci in range(nc)]
    decays, lows, attns = [], [], []
    for ci, h in pairs:
        gc = gc_all[sls[ci]]
        onehot = (_iota((c, 128), 1) == h).astype(F32)
        col = mask_l(onehot, gc, 1, 1)
        row = jnp.sum(gc * onehot, axis=1, keepdims=True)
        decays.append(jnp.exp(jnp.where(causal, row - col, -jnp.inf)))
    heads = lambda parts: jnp.concatenate(parts, axis=0)
    head_rows = lambda x, h: x[h * c:(h + 1) * c]
    for ci in range(nc):
        k_c = k[sls[ci]]
        stacked = heads([kb_all[sls[ci]] * _head_mask(h) for h in range(HEADS)]
                        + [q[sls[ci]] * _head_mask(h) for h in range(HEADS)])
        scores = bdot(stacked, k_c, 1, 1)
        for h in range(HEADS):
            lows.append(jnp.where(strict, head_rows(scores, h) * decays[ci * HEADS + h], 0.0))
            attns.append(head_rows(scores, HEADS + h) * decays[ci * HEADS + h])
    if saved_inv is None:
        invs = _neumann_inverses(lows)
    else:
        invs = [_saved_inverse(low, s) for low, s in zip(lows, saved_inv)]
    us, ws = [], []
    for ci in range(nc):
        rhs = jnp.concatenate([vb_all[sls[ci]], kbe_all[sls[ci]]], axis=1)
        sol = ddot(heads(invs[ci * HEADS:(ci + 1) * HEADS]), rhs)
        u = jnp.zeros((c, GROUP), F32)
        w = jnp.zeros((c, GROUP), F32)
        for h in range(HEADS):
            mh = _head_mask(h)
            u = u + mh * head_rows(sol, h)[:, :GROUP]
            w = w + mh * head_rows(sol, h)[:, GROUP:]
        us.append(u)
        ws.append(w)
    outs = []
    for ci in range(nc):
        gc_b = gcb_all[sls[ci]]
        gc_last_b = _row_pick(gc_b, c - 1)
        v_new = us[ci] - bdot(ws[ci], state, 1, 0)
        o = bdot(qg_all[sls[ci]], state, 1, 0)
        mixed = bdot(heads(attns[ci * HEADS:(ci + 1) * HEADS]), v_new, 1, 0)
        for h in range(HEADS):
            o = o + _head_mask(h) * head_rows(mixed, h)
        k_dec = k[sls[ci]] * jnp.exp(gc_last_b - gc_b)
        state = state * jnp.exp(gc_last_b) + bd * bdot(k_dec, v_new, 0, 0)
        outs.append(o)
    o = jnp.concatenate(outs, axis=0)
    return _head_rmsnorm_gate(o, nw, pz, bd), state, invs


def _gla_chunk(pq, pk, pv, small, pz, w2, gbias, nw, state_t):
    c = SCAN_CHUNK
    rows = pq.shape[0]
    nc = rows // c
    sls = [slice(ci * c, (ci + 1) * c) for ci in range(nc)]
    bd = _block_diag_mask()
    log_a = _log_sigmoid(bdot(small, w2, 1, 0) + gbias) * (1.0 / 16.0)
    gcum = mask_l(_chunk_tril(rows).astype(F32), log_a)
    r, s = _iota((rows, rows), 0), _iota((rows, rows), 1)
    base = (r >> 6) << 6
    g_mid = mask_l((s == base + c // 2).astype(F32), gcum)
    g_last = mask_l((s == base + c - 1).astype(F32), gcum)
    q = pq * (HEAD_DIM ** -0.5)
    qa = q * jnp.exp(gcum - g_mid)
    ka = pk * jnp.exp(g_mid - gcum)
    qg = q * jnp.exp(gcum)
    k_last = pk * jnp.exp(g_last - gcum)
    causal = _tril(c)
    attns = [jnp.where(causal, bdot(qa[sls[ci]] * _head_mask(h), ka[sls[ci]], 1, 1), 0.0)
             for ci in range(nc) for h in range(HEADS)]
    intra = []
    for ci in range(nc):
        o = jnp.zeros((c, GROUP), F32)
        for h in range(HEADS):
            o = o + _head_mask(h) * bdot(attns[ci * HEADS + h], pv[sls[ci]], 1, 0)
        intra.append(o)
    kvs = [bd * bdot(pv[sls[ci]], k_last[sls[ci]], 0, 0) for ci in range(nc)]
    states = []
    for ci in range(nc):
        states.append(state_t)
        state_t = state_t * jnp.exp(_row_pick(g_last[sls[ci]], 0)) + kvs[ci]
    outs = [intra[ci] + bdot(qg[sls[ci]], states[ci], 1, 1) for ci in range(nc)]
    o = jnp.concatenate(outs, axis=0)
    return _head_rmsnorm_gate(o, nw, pz, bd), state_t


def _col_spec(rows, group, rev_n=None):
    if rev_n is None:
        return pl.BlockSpec((rows, GROUP), lambda i: (i, group))
    return pl.BlockSpec((rows, GROUP), lambda i: (rev_n - 1 - i, group))


def _small_spec(rows, group128, rev_n=None):
    if rev_n is None:
        return pl.BlockSpec((rows, 128), lambda i: (i, group128))
    return pl.BlockSpec((rows, 128), lambda i: (rev_n - 1 - i, group128))


def _halo_spec(rows, group, rev_n=None):
    per = rows // HALO
    if rev_n is None:
        return pl.BlockSpec((HALO, GROUP), lambda i: (jnp.maximum(i * per - 1, 0), group))
    return pl.BlockSpec((HALO, GROUP), lambda i: (jnp.maximum((rev_n - 1 - i) * per - 1, 0), group))


def _full_spec(shape):
    nd = len(shape)
    return pl.BlockSpec(shape, lambda i: (0,) * nd)


def _out_rows_spec(rows, lanes, rev_n=None):
    if rev_n is None:
        return pl.BlockSpec((rows, lanes), lambda i: (i, 0))
    return pl.BlockSpec((rows, lanes), lambda i: (rev_n - 1 - i, 0))


SGU_ROWS = 4 * SGU_CHUNK


def _sgu_fwd(p, ln_w, ln_b, ws, bs_t):
    t = p.shape[0]
    n = t // SGU_ROWS

    def body(pu_ref, pv_ref, lw_ref, lb_ref, ws_ref, bs_ref, y_ref):
        y = _sgu_chunk(pu_ref[...], pv_ref[...], lw_ref[...], lb_ref[...],
                       ws_ref[0], ws_ref[1], ws_ref[2], ws_ref[3], bs_ref[...])
        y_ref[...] = y.astype(y_ref.dtype)

    return pl.pallas_call(
        body, name="sgu_fwd", grid=(n,),
        in_specs=[_col_spec(SGU_ROWS, COL_AU), _col_spec(SGU_ROWS, COL_AV), _full_spec((1, GROUP)),
                  _full_spec((1, GROUP)), _full_spec((HEADS, SGU_CHUNK, SGU_CHUNK)), _full_spec((SGU_CHUNK, 128))],
        out_specs=_out_rows_spec(SGU_ROWS, GROUP),
        out_shape=jax.ShapeDtypeStruct((t, GROUP), BF16),
        compiler_params=_cparams(("arbitrary",)),
    )(p, p, ln_w, ln_b, ws, bs_t)


def _sgu_bwd(p, dmix, ln_w, ln_b, ws, bs_t):
    t = p.shape[0]
    n = t // SGU_ROWS

    def body(pu_ref, pv_ref, dy_ref, lw_ref, lb_ref, ws_ref, bs_ref,
             dpu_ref, dpv_ref, dlw_ref, dlb_ref, dws_ref, dbs_ref):
        args = (pu_ref[...], pv_ref[...], lw_ref[...], lb_ref[...],
                ws_ref[0], ws_ref[1], ws_ref[2], ws_ref[3], bs_ref[...])
        _, vjp = jax.vjp(_sgu_chunk, *args)
        dpu, dpv, dlw, dlb, d0, d1, d2, d3, dbs = vjp(dy_ref[...])
        dpu_ref[...] = dpu.astype(dpu_ref.dtype)
        dpv_ref[...] = dpv.astype(dpv_ref.dtype)

        @pl.when(pl.program_id(0) == 0)
        def _():
            dlw_ref[...] = jnp.zeros_like(dlw_ref)
            dlb_ref[...] = jnp.zeros_like(dlb_ref)
            dws_ref[...] = jnp.zeros_like(dws_ref)
            dbs_ref[...] = jnp.zeros_like(dbs_ref)

        dlw_ref[...] += dlw
        dlb_ref[...] += dlb
        for h, d in enumerate((d0, d1, d2, d3)):
            dws_ref[h] += d
        dbs_ref[...] += dbs

    return pl.pallas_call(
        body, name="sgu_bwd", grid=(n,),
        in_specs=[_col_spec(SGU_ROWS, COL_AU), _col_spec(SGU_ROWS, COL_AV),
                  pl.BlockSpec((SGU_ROWS, GROUP), lambda i: (i, 0)),
                  _full_spec((1, GROUP)), _full_spec((1, GROUP)), _full_spec((HEADS, SGU_CHUNK, SGU_CHUNK)),
                  _full_spec((SGU_CHUNK, 128))],
        out_specs=[_out_rows_spec(SGU_ROWS, GROUP), _out_rows_spec(SGU_ROWS, GROUP), _full_spec((1, GROUP)),
                   _full_spec((1, GROUP)), _full_spec((HEADS, SGU_CHUNK, SGU_CHUNK)), _full_spec((SGU_CHUNK, 128))],
        out_shape=[jax.ShapeDtypeStruct((t, GROUP), BF16), jax.ShapeDtypeStruct((t, GROUP), BF16),
                   jax.ShapeDtypeStruct((1, GROUP), F32), jax.ShapeDtypeStruct((1, GROUP), F32),
                   jax.ShapeDtypeStruct((HEADS, SGU_CHUNK, SGU_CHUNK), F32),
                   jax.ShapeDtypeStruct((SGU_CHUNK, 128), F32)],
        compiler_params=_cparams(("arbitrary",)),
    )(p, p, dmix, ln_w, ln_b, ws, bs_t)


SC_ROWS = 512


def _first_block_zero(halo, first):
    return jnp.where(first, 0.0, halo)


def _sc_fwd(p, cw):
    t = p.shape[0]
    n = t // SC_ROWS

    def body(pb_ref, pc_ref, ph_ref, hc_ref, hh_ref, cw_ref, y_ref):
        first = pl.program_id(0) == 0
        y = _sc_chunk(pb_ref[...], pc_ref[...], ph_ref[...], _first_block_zero(hc_ref[...], first),
                      _first_block_zero(hh_ref[...], first), cw_ref[...])
        y_ref[...] = y.astype(y_ref.dtype)

    return pl.pallas_call(
        body, name="sc_fwd", grid=(n,),
        in_specs=[_col_spec(SC_ROWS, COL_BB), _col_spec(SC_ROWS, COL_BC), _col_spec(SC_ROWS, COL_BH),
                  _halo_spec(SC_ROWS, COL_BC), _halo_spec(SC_ROWS, COL_BH), _full_spec((HALO, GROUP))],
        out_specs=_out_rows_spec(SC_ROWS, GROUP),
        out_shape=jax.ShapeDtypeStruct((t, GROUP), BF16),
        compiler_params=_cparams(("arbitrary",)),
    )(p, p, p, p, p, cw)


def _add_halo_grad(d, carry):
    return d + jnp.concatenate([jnp.zeros((d.shape[0] - HALO, d.shape[1]), d.dtype), carry], axis=0)


def _sc_bwd(p, dmix, cw):
    t = p.shape[0]
    n = t // SC_ROWS

    def body(pb_ref, pc_ref, ph_ref, hc_ref, hh_ref, dy_ref, cw_ref,
             dpb_ref, dpc_ref, dph_ref, dcw_ref, carry_c, carry_h):
        i = pl.program_id(0)
        first = i == n - 1

        @pl.when(i == 0)
        def _():
            carry_c[...] = jnp.zeros_like(carry_c)
            carry_h[...] = jnp.zeros_like(carry_h)
            dcw_ref[...] = jnp.zeros_like(dcw_ref)

        args = (pb_ref[...], pc_ref[...], ph_ref[...], _first_block_zero(hc_ref[...], first),
                _first_block_zero(hh_ref[...], first), cw_ref[...])
        _, vjp = jax.vjp(_sc_chunk, *args)
        dpb, dpc, dph, dhc, dhh, dcw = vjp(dy_ref[...])
        dpb_ref[...] = dpb.astype(dpb_ref.dtype)
        dpc_ref[...] = _add_halo_grad(dpc, carry_c[...]).astype(dpc_ref.dtype)
        dph_ref[...] = _add_halo_grad(dph, carry_h[...]).astype(dph_ref.dtype)
        carry_c[...] = dhc
        carry_h[...] = dhh
        dcw_ref[...] += dcw

    return pl.pallas_call(
        body, name="sc_bwd", grid=(n,),
        in_specs=[_col_spec(SC_ROWS, COL_BB, n), _col_spec(SC_ROWS, COL_BC, n), _col_spec(SC_ROWS, COL_BH, n),
                  _halo_spec(SC_ROWS, COL_BC, n), _halo_spec(SC_ROWS, COL_BH, n),
                  pl.BlockSpec((SC_ROWS, GROUP), lambda i: (n - 1 - i, 1)), _full_spec((HALO, GROUP))],
        out_specs=[_out_rows_spec(SC_ROWS, GROUP, n)] * 3 + [_full_spec((HALO, GROUP))],
        out_shape=[jax.ShapeDtypeStruct((t, GROUP), BF16)] * 3 + [jax.ShapeDtypeStruct((HALO, GROUP), F32)],
        scratch_shapes=[pltpu.VMEM((HALO, GROUP), F32), pltpu.VMEM((HALO, GROUP), F32)],
        compiler_params=_cparams(("arbitrary",)),
    )(p, p, p, p, p, dmix, cw)


SCAN_STEP_CHUNKS = 4
SCAN_ROWS = SCAN_STEP_CHUNKS * SCAN_CHUNK


def _host_call(body, hosted, *, name, grid, in_specs, out_specs, out_shape, scratch_shapes, args):
    params = _cparams(("arbitrary",))
    if hosted is None:
        outs = pl.pallas_call(body, name=name, grid=grid, in_specs=in_specs, out_specs=out_specs,
                              out_shape=out_shape, scratch_shapes=scratch_shapes, compiler_params=params)(*args)
        return outs, None
    stage, arrays, modes, bufs = hosted
    arrays = list(arrays) if stage == 1 else []
    n_in, n_out, n_scr, n_src, na = len(in_specs), len(out_specs), len(scratch_shapes), len(arrays), len(bufs)
    last = grid[0] - 1

    def new_body(*refs):
        srcs = refs[n_in:n_in + n_src]
        o0 = n_in + n_src + na
        ex = refs[o0 + n_out:o0 + n_out + na]
        s0 = o0 + n_out + na
        sems = refs[s0 + n_scr:]
        i = pl.program_id(0)

        def run(what):
            if stage == 1:
                _exchange_stage1(srcs, ex, modes, sems[0], sems[1], what)
            else:
                _exchange_stage2(ex, sems[0], sems[1], what)

        @pl.when(i == 0)
        def _():
            run("start")

        body(*refs[:n_in], *refs[o0:o0 + n_out], *refs[s0:s0 + n_scr])

        @pl.when(i == last)
        def _():
            run("wait")

    any_spec = pl.BlockSpec(memory_space=pl.ANY)
    outs = pl.pallas_call(
        new_body, name=name, grid=grid,
        in_specs=list(in_specs) + [any_spec] * (n_src + na), out_specs=list(out_specs) + [any_spec] * na,
        out_shape=list(out_shape) + [jax.ShapeDtypeStruct(b.shape, b.dtype) for b in bufs],
        input_output_aliases={n_in + n_src + a: n_out + a for a in range(na)},
        scratch_shapes=list(scratch_shapes) + (_stage1_sems(na) if stage == 1 else _stage2_sems(na)),
        compiler_params=params,
    )(*args, *arrays, *bufs)
    return outs[:n_out], outs[n_out:]


def _dn_fwd(p, cw3, a_log, dt_bias, nw, hosted=None):
    t = p.shape[0]
    r = SCAN_ROWS
    n = t // r

    def body(pq_ref, pk_ref, pv_ref, hq_ref, hk_ref, hv_ref, sm_ref, pz_ref, cw_ref, al_ref, dt_ref, nw_ref,
             y_ref, ck_ref, inv_ref, state):
        first = pl.program_id(0) == 0

        @pl.when(first)
        def _():
            state[...] = jnp.zeros_like(state)

        s_in = state[...]
        ck_ref[0] = s_in
        y, s_out, invs = _dn_block(pq_ref[...], pk_ref[...], pv_ref[...], _first_block_zero(hq_ref[...], first),
                                   _first_block_zero(hk_ref[...], first), _first_block_zero(hv_ref[...], first),
                                   sm_ref[...], pz_ref[...], cw_ref[0], cw_ref[1], cw_ref[2],
                                   al_ref[...], dt_ref[...], nw_ref[...], s_in)
        y_ref[...] = y.astype(y_ref.dtype)
        state[...] = s_out
        for j, inv in enumerate(invs):
            inv_ref[j] = inv

    nh = SCAN_STEP_CHUNKS * HEADS
    return _host_call(
        body, hosted, name="dn_fwd", grid=(n,),
        in_specs=[_col_spec(r, COL_CQ), _col_spec(r, COL_CK), _col_spec(r, COL_CV),
                  _halo_spec(r, COL_CQ), _halo_spec(r, COL_CK), _halo_spec(r, COL_CV),
                  _small_spec(r, COL128_SMALL_C), _col_spec(r, COL_CZ), _full_spec((3, HALO, GROUP)),
                  _full_spec((1, 128)), _full_spec((1, 128)), _full_spec((1, GROUP))],
        out_specs=[_out_rows_spec(r, GROUP), pl.BlockSpec((1, GROUP, GROUP), lambda i: (i, 0, 0)),
                   pl.BlockSpec((nh, SCAN_CHUNK, SCAN_CHUNK), lambda i: (i, 0, 0))],
        out_shape=[jax.ShapeDtypeStruct((t, GROUP), BF16), jax.ShapeDtypeStruct((n, GROUP, GROUP), F32),
                   jax.ShapeDtypeStruct((n * nh, SCAN_CHUNK, SCAN_CHUNK), F32)],
        scratch_shapes=[pltpu.VMEM((GROUP, GROUP), F32)],
        args=(p, p, p, p, p, p, p, p, cw3, a_log, dt_bias, nw))


def _dn_bwd(p, dmix, states, invs, cw3, a_log, dt_bias, nw, hosted=None):
    t = p.shape[0]
    c = SCAN_ROWS
    n = t // c
    nh = SCAN_STEP_CHUNKS * HEADS

    def body(pq_ref, pk_ref, pv_ref, hq_ref, hk_ref, hv_ref, sm_ref, pz_ref, dy_ref, ck_ref, inv_ref,
             cw_ref, al_ref, dt_ref, nw_ref,
             dpq_ref, dpk_ref, dpv_ref, dsm_ref, dpz_ref, dcw_ref, dal_ref, ddt_ref, dnw_ref,
             dstate, carry):
        i = pl.program_id(0)
        first = i == n - 1

        @pl.when(i == 0)
        def _():
            dstate[...] = jnp.zeros_like(dstate)
            carry[...] = jnp.zeros_like(carry)
            dcw_ref[...] = jnp.zeros_like(dcw_ref)
            dal_ref[...] = jnp.zeros_like(dal_ref)
            ddt_ref[...] = jnp.zeros_like(ddt_ref)
            dnw_ref[...] = jnp.zeros_like(dnw_ref)

        args = (pq_ref[...], pk_ref[...], pv_ref[...], _first_block_zero(hq_ref[...], first),
                _first_block_zero(hk_ref[...], first), _first_block_zero(hv_ref[...], first),
                sm_ref[...], pz_ref[...], cw_ref[0], cw_ref[1], cw_ref[2],
                al_ref[...], dt_ref[...], nw_ref[...], ck_ref[0])
        saved = [inv_ref[j] for j in range(nh)]
        _, vjp = jax.vjp(lambda *a: _dn_block(*a, saved_inv=saved)[:2], *args)
        (dpq, dpk, dpv, dhq, dhk, dhv, dsm, dpz, dcq, dck, dcv, dal, ddt, dnw, dst) = vjp(
            (dy_ref[...], dstate[...]))
        dpq_ref[...] = _add_halo_grad(dpq, carry[0]).astype(dpq_ref.dtype)
        dpk_ref[...] = _add_halo_grad(dpk, carry[1]).astype(dpk_ref.dtype)
        dpv_ref[...] = _add_halo_grad(dpv, carry[2]).astype(dpv_ref.dtype)
        dsm_ref[...] = dsm.astype(dsm_ref.dtype)
        dpz_ref[...] = dpz.astype(dpz_ref.dtype)
        carry[0] = dhq
        carry[1] = dhk
        carry[2] = dhv
        dstate[...] = dst
        dcw_ref[0] += dcq
        dcw_ref[1] += dck
        dcw_ref[2] += dcv
        dal_ref[...] += dal
        ddt_ref[...] += ddt
        dnw_ref[...] += dnw

    return _host_call(
        body, hosted, name="dn_bwd", grid=(n,),
        in_specs=[_col_spec(c, COL_CQ, n), _col_spec(c, COL_CK, n), _col_spec(c, COL_CV, n),
                  _halo_spec(c, COL_CQ, n), _halo_spec(c, COL_CK, n), _halo_spec(c, COL_CV, n),
                  _small_spec(c, COL128_SMALL_C, n), _col_spec(c, COL_CZ, n),
                  pl.BlockSpec((c, GROUP), lambda i: (n - 1 - i, 2)),
                  pl.BlockSpec((1, GROUP, GROUP), lambda i: (n - 1 - i, 0, 0)),
                  pl.BlockSpec((nh, SCAN_CHUNK, SCAN_CHUNK), lambda i: (n - 1 - i, 0, 0)),
                  _full_spec((3, HALO, GROUP)), _full_spec((1, 128)), _full_spec((1, 128)), _full_spec((1, GROUP))],
        out_specs=[_out_rows_spec(c, GROUP, n)] * 3 + [_out_rows_spec(c, 128, n), _out_rows_spec(c, GROUP, n),
                   _full_spec((3, HALO, GROUP)), _full_spec((1, 128)), _full_spec((1, 128)), _full_spec((1, GROUP))],
        out_shape=[jax.ShapeDtypeStruct((t, GROUP), BF16)] * 3 + [
            jax.ShapeDtypeStruct((t, 128), BF16), jax.ShapeDtypeStruct((t, GROUP), BF16),
            jax.ShapeDtypeStruct((3, HALO, GROUP), F32), jax.ShapeDtypeStruct((1, 128), F32),
            jax.ShapeDtypeStruct((1, 128), F32), jax.ShapeDtypeStruct((1, GROUP), F32)],
        scratch_shapes=[pltpu.VMEM((GROUP, GROUP), F32), pltpu.VMEM((3, HALO, GROUP), F32)],
        args=(p, p, p, p, p, p, p, p, dmix, states, invs, cw3, a_log, dt_bias, nw))


def _gla_fwd(p, w2, gbias, nw, hosted=None):
    t = p.shape[0]
    c = SCAN_ROWS
    n = t // c

    def body(pq_ref, pk_ref, pv_ref, sm_ref, pz_ref, w2_ref, gb_ref, nw_ref, y_ref, ck_ref, state):
        @pl.when(pl.program_id(0) == 0)
        def _():
            state[...] = jnp.zeros_like(state)

        s_in = state[...]
        ck_ref[0] = s_in
        y, s_out = _gla_chunk(pq_ref[...], pk_ref[...], pv_ref[...], sm_ref[...], pz_ref[...],
                              w2_ref[...], gb_ref[...], nw_ref[...], s_in)
        y_ref[...] = y.astype(y_ref.dtype)
        state[...] = s_out

    return _host_call(
        body, hosted, name="gla_fwd", grid=(n,),
        in_specs=[_col_spec(c, COL_DQ), _col_spec(c, COL_DK), _col_spec(c, COL_DV),
                  _small_spec(c, COL128_SMALL_D), _col_spec(c, COL_DZ),
                  _full_spec((128, GROUP)), _full_spec((1, GROUP)), _full_spec((1, GROUP))],
        out_specs=[_out_rows_spec(c, GROUP), pl.BlockSpec((1, GROUP, GROUP), lambda i: (i, 0, 0))],
        out_shape=[jax.ShapeDtypeStruct((t, GROUP), BF16), jax.ShapeDtypeStruct((n, GROUP, GROUP), F32)],
        scratch_shapes=[pltpu.VMEM((GROUP, GROUP), F32)],
        args=(p, p, p, p, p, w2, gbias, nw))


def _gla_bwd(p, dmix, states, w2, gbias, nw, hosted=None):
    t = p.shape[0]
    c = SCAN_ROWS
    n = t // c

    def body(pq_ref, pk_ref, pv_ref, sm_ref, pz_ref, dy_ref, ck_ref, w2_ref, gb_ref, nw_ref,
             dpq_ref, dpk_ref, dpv_ref, dsm_ref, dpz_ref, dw2_ref, dgb_ref, dnw_ref, dstate):
        @pl.when(pl.program_id(0) == 0)
        def _():
            dstate[...] = jnp.zeros_like(dstate)
            dw2_ref[...] = jnp.zeros_like(dw2_ref)
            dgb_ref[...] = jnp.zeros_like(dgb_ref)
            dnw_ref[...] = jnp.zeros_like(dnw_ref)

        args = (pq_ref[...], pk_ref[...], pv_ref[...], sm_ref[...], pz_ref[...],
                w2_ref[...], gb_ref[...], nw_ref[...], ck_ref[0])
        _, vjp = jax.vjp(_gla_chunk, *args)
        dpq, dpk, dpv, dsm, dpz, dw2, dgb, dnw, dst = vjp((dy_ref[...], dstate[...]))
        dpq_ref[...] = dpq.astype(dpq_ref.dtype)
        dpk_ref[...] = dpk.astype(dpk_ref.dtype)
        dpv_ref[...] = dpv.astype(dpv_ref.dtype)
        dsm_ref[...] = dsm.astype(dsm_ref.dtype)
        dpz_ref[...] = dpz.astype(dpz_ref.dtype)
        dstate[...] = dst
        dw2_ref[...] += dw2
        dgb_ref[...] += dgb
        dnw_ref[...] += dnw

    return _host_call(
        body, hosted, name="gla_bwd", grid=(n,),
        in_specs=[_col_spec(c, COL_DQ, n), _col_spec(c, COL_DK, n), _col_spec(c, COL_DV, n),
                  _small_spec(c, COL128_SMALL_D, n), _col_spec(c, COL_DZ, n),
                  pl.BlockSpec((c, GROUP), lambda i: (n - 1 - i, 3)),
                  pl.BlockSpec((1, GROUP, GROUP), lambda i: (n - 1 - i, 0, 0)),
                  _full_spec((128, GROUP)), _full_spec((1, GROUP)), _full_spec((1, GROUP))],
        out_specs=[_out_rows_spec(c, GROUP, n)] * 3 + [_out_rows_spec(c, 128, n), _out_rows_spec(c, GROUP, n),
                   _full_spec((128, GROUP)), _full_spec((1, GROUP)), _full_spec((1, GROUP))],
        out_shape=[jax.ShapeDtypeStruct((t, GROUP), BF16)] * 3 + [
            jax.ShapeDtypeStruct((t, 128), BF16), jax.ShapeDtypeStruct((t, GROUP), BF16),
            jax.ShapeDtypeStruct((128, GROUP), F32), jax.ShapeDtypeStruct((1, GROUP), F32),
            jax.ShapeDtypeStruct((1, GROUP), F32)],
        scratch_shapes=[pltpu.VMEM((GROUP, GROUP), F32)],
        args=(p, p, p, p, p, dmix, states, w2, gbias, nw))


def _pick_tile(n, pref):
    for cand in pref:
        if n % cand == 0:
            return cand
    return n


MM_TILE_CAP = 1408


def _largest_tile(n, cap):
    best = None
    for mult in range(1, cap // 128 + 1):
        if n % (128 * mult) == 0:
            best = 128 * mult
    return best if best is not None else n


def _half_index(t, per_half, middle):
    half = jnp.where(t >= per_half, 1, 0)
    return half, middle, t - half * per_half


def _matmul(a, b, mode, out_dtype, name, res=None, pieces=None, norm_w=None):
    a_list = list(a) if isinstance(a, (list, tuple)) else [a]
    a_rows, a_cols = a_list[0].shape[0], sum(x.shape[1] for x in a_list)
    if mode == "nn":
        (m, k), n = (a_rows, a_cols), b.shape[1]
    elif mode == "nt":
        (m, k), n = (a_rows, a_cols), b.shape[0]
    else:
        (k, m), n = (a_rows, a_cols), b.shape[-1] * (2 if b.ndim == 3 else 1)
    tm = _largest_tile(m, MM_TILE_CAP)
    tn = _largest_tile(b.shape[-1] if b.ndim == 3 else n, MM_TILE_CAP)
    if pieces == "cols":
        tm, tn = m, n // N_CHIPS
    tk = _largest_tile(k, MM_TILE_CAP)
    if len(a_list) > 1:
        tk, tm = (k, tm) if mode == "nn" else (tk, m)
    nk = k // tk
    na = len(a_list)
    if mode == "nn":
        a_specs = [pl.BlockSpec((tm, tk if na == 1 else x.shape[1]), lambda i, j, kk: (i, kk)) for x in a_list]
        b_spec = pl.BlockSpec((tk, tn), lambda i, j, kk: (kk, j))
        dims = (1, 0)
    elif mode == "nt":
        a_specs = [pl.BlockSpec((tm, tk), lambda i, j, kk: (i, kk))]
        b_spec = pl.BlockSpec((tn, tk), lambda i, j, kk: (j, kk))
        dims = (1, 1)
    else:
        a_specs = [pl.BlockSpec((tk, tm if na == 1 else x.shape[1]), lambda i, j, kk: (kk, i)) for x in a_list]
        if b.ndim == 3:
            njh = b.shape[-1] // tn
            b_spec = pl.BlockSpec((None, tk, tn), lambda i, j, kk: _half_index(j, njh, kk))
        else:
            b_spec = pl.BlockSpec((tk, tn), lambda i, j, kk: (kk, j))
        dims = (0, 0)
    o_spec = pl.BlockSpec((tm, tn), lambda i, j, kk: (i, j))
    o_shape = (m, n)
    if pieces == "cols":
        o_spec = pl.BlockSpec((2, None, tm // 2, tn), lambda i, j, kk: (0, j, 0, 0))
        o_shape = (2, N_CHIPS, tm // 2, tn)
    has_res = res is not None
    has_norm = norm_w is not None

    def body(*refs):
        a_refs, b_ref = refs[:na], refs[na]
        pos = na + 1
        r_ref = refs[pos] if has_res else None
        pos += has_res
        nw_ref = refs[pos] if has_norm else None
        pos += has_norm
        o_ref = refs[pos]
        h_ref = refs[pos + 1] if has_norm else None
        if na == 1:
            a_val = a_refs[0][...].astype(MXU_DTYPE)
        else:
            a_val = jnp.concatenate([r[...].astype(MXU_DTYPE) for r in a_refs], axis=1)
        part = _dg(a_val, b_ref[...].astype(MXU_DTYPE), *dims)

        def finish(out):
            if has_res:
                out = out + r_ref[...]
            if pieces == "cols":
                o_ref[0] = out[:tm // 2].astype(o_ref.dtype)
                o_ref[1] = out[tm // 2:].astype(o_ref.dtype)
            else:
                o_ref[...] = out.astype(o_ref.dtype)
            if has_norm:
                r = lax.rsqrt(jnp.mean(out * out, axis=-1, keepdims=True) + EPS)
                h_ref[...] = (out * r * nw_ref[...]).astype(h_ref.dtype)

        if nk == 1:
            finish(part)
            return
        acc = refs[-1]
        kk = pl.program_id(2)

        @pl.when(kk == 0)
        def _():
            acc[...] = part

        @pl.when(kk > 0)
        def _():
            acc[...] += part

        @pl.when(kk == nk - 1)
        def _():
            finish(acc[...])

    in_specs = a_specs + [b_spec] + ([o_spec] if has_res else [])
    args = (*a_list, b) + ((res,) if has_res else ())
    out_specs, out_shape = o_spec, jax.ShapeDtypeStruct(o_shape, out_dtype)
    if has_norm:
        assert mode == "nn" and tn == n
        in_specs.append(pl.BlockSpec((1, n), lambda i, j, kk: (0, 0)))
        args += (norm_w,)
        out_specs, out_shape = [o_spec, o_spec], [out_shape, jax.ShapeDtypeStruct(o_shape, BF16)]
    return pl.pallas_call(
        body, name=name, grid=(m // tm, n // tn, nk), in_specs=in_specs, out_specs=out_specs,
        out_shape=out_shape,
        scratch_shapes=[pltpu.VMEM((tm, tn), F32)] if nk > 1 else [],
        compiler_params=_cparams(("parallel", "parallel", "arbitrary")),
    )(*args)


def _matmul_nt_norm_bwd(a, b, x, w, dres, name):
    n = b.shape[0]
    m = a.shape[-2]
    tm = _largest_tile(m, 1024)
    if a.ndim == 3:
        kh = a.shape[2]
        k = 2 * kh
        tk = _largest_tile(kh, MM_TILE_CAP)
        nkh = kh // tk
        a_spec = pl.BlockSpec((None, tm, tk), lambda i, kk: _half_index(kk, nkh, i))
    else:
        k = a.shape[1]
        tk = _largest_tile(k, MM_TILE_CAP)
        a_spec = pl.BlockSpec((tm, tk), lambda i, kk: (i, kk))
    nk = k // tk

    def body(a_ref, b_ref, x_ref, w_ref, r_ref, dx_ref, dw_ref, acc):
        i, kk = pl.program_id(0), pl.program_id(1)
        part = _dg(a_ref[...].astype(MXU_DTYPE), b_ref[...].astype(MXU_DTYPE), 1, 1)

        @pl.when(kk == 0)
        def _():
            acc[...] = part

        @pl.when(kk > 0)
        def _():
            acc[...] += part

        @pl.when((i == 0) & (kk == 0))
        def _():
            dw_ref[...] = jnp.zeros_like(dw_ref)

        @pl.when(kk == nk - 1)
        def _():
            g = acc[...]
            xv = x_ref[...]
            r = lax.rsqrt(jnp.mean(xv * xv, axis=-1, keepdims=True) + EPS)
            xhat = xv * r
            dw_ref[...] += jnp.sum(g * xhat, axis=0, keepdims=True)
            gx = g * w_ref[...]
            dx_ref[...] = r_ref[...] + r * (gx - xhat * jnp.mean(gx * xhat, axis=-1, keepdims=True))

    row_spec = pl.BlockSpec((tm, n), lambda i, kk: (i, 0))
    return pl.pallas_call(
        body, name=name, grid=(m // tm, nk),
        in_specs=[a_spec, pl.BlockSpec((n, tk), lambda i, kk: (0, kk)),
                  row_spec, pl.BlockSpec((1, n), lambda i, kk: (0, 0)), row_spec],
        out_specs=[row_spec, pl.BlockSpec((1, n), lambda i, kk: (0, 0))],
        out_shape=[jax.ShapeDtypeStruct((m, n), F32), jax.ShapeDtypeStruct((1, n), F32)],
        scratch_shapes=[pltpu.VMEM((tm, n), F32)],
        compiler_params=_cparams(("arbitrary", "arbitrary")),
    )(a, b, x, w, dres)


NORM_ROWS = 512


def _rmsnorm_fwd(x, w, name):
    t, d = x.shape

    def body(x_ref, w_ref, o_ref):
        xv = x_ref[...]
        r = lax.rsqrt(jnp.mean(xv * xv, axis=-1, keepdims=True) + EPS)
        o_ref[...] = (xv * r * w_ref[...]).astype(o_ref.dtype)

    return pl.pallas_call(
        body, name=name, grid=(t // NORM_ROWS,),
        in_specs=[pl.BlockSpec((NORM_ROWS, d), lambda i: (i, 0)), _full_spec((1, d))],
        out_specs=pl.BlockSpec((NORM_ROWS, d), lambda i: (i, 0)),
        out_shape=jax.ShapeDtypeStruct((t, d), BF16),
        compiler_params=_cparams(("parallel",)),
    )(x, w)


SWIGLU_ROWS = 128


def _ffn_up_swiglu(h, w_gate_up):
    m, k = h.shape
    tm = _largest_tile(m, 512)
    tn = _largest_tile(D_FF, MM_TILE_CAP)
    nj = D_FF // tn

    def body(a_ref, bg_ref, bu_ref, g_ref, u_ref, act_ref):
        a = a_ref[...].astype(MXU_DTYPE)
        gate = _dg(a, bg_ref[...].astype(MXU_DTYPE), 1, 0)
        up = _dg(a, bu_ref[...].astype(MXU_DTYPE), 1, 0)
        g_ref[...] = gate
        u_ref[...] = up
        act_ref[...] = (_silu(gate) * up).astype(act_ref.dtype)

    o_spec = pl.BlockSpec((tm, tn), lambda i, j: (i, j))
    return pl.pallas_call(
        body, name="ffn_up", grid=(m // tm, nj),
        in_specs=[pl.BlockSpec((tm, k), lambda i, j: (i, 0)), pl.BlockSpec((k, tn), lambda i, j: (0, j)),
                  pl.BlockSpec((k, tn), lambda i, j: (0, j + nj))],
        out_specs=[o_spec, o_spec, o_spec],
        out_shape=[jax.ShapeDtypeStruct((m, D_FF), F32), jax.ShapeDtypeStruct((m, D_FF), F32),
                   jax.ShapeDtypeStruct((m, D_FF), BF16)],
        compiler_params=_cparams(("parallel", "parallel")),
    )(h, w_gate_up, w_gate_up)


def _ffn_down_dx_swiglu(dx, w_down, gate, up):
    m, k = dx.shape
    tm = _largest_tile(m, 512)
    tn = _largest_tile(D_FF, MM_TILE_CAP)

    def body(a_ref, b_ref, g_ref, u_ref, o_ref):
        da = _dg(a_ref[...].astype(MXU_DTYPE), b_ref[...].astype(MXU_DTYPE), 1, 1)
        gate = g_ref[...]
        sg = jax.nn.sigmoid(gate)
        o_ref[0] = (da * u_ref[...] * (sg * (1.0 + gate * (1.0 - sg)))).astype(o_ref.dtype)
        o_ref[1] = (da * gate * sg).astype(o_ref.dtype)

    tile = pl.BlockSpec((tm, tn), lambda i, j: (i, j))
    return pl.pallas_call(
        body, name="ffn_down_dx", grid=(m // tm, D_FF // tn),
        in_specs=[pl.BlockSpec((tm, k), lambda i, j: (i, 0)), pl.BlockSpec((tn, k), lambda i, j: (j, 0)), tile, tile],
        out_specs=pl.BlockSpec((2, tm, tn), lambda i, j: (0, i, j)),
        out_shape=jax.ShapeDtypeStruct((2, m, D_FF), MXU_DTYPE),
        compiler_params=_cparams(("parallel", "parallel")),
    )(dx, w_down, gate, up)


def _loss_head(x, w, target):
    t, d = x.shape

    def fwd(xv, wv, tv):
        r = lax.rsqrt(jnp.mean(xv * xv, axis=-1, keepdims=True) + EPS)
        err = xv * r * wv - tv
        return 0.5 * jnp.sum(jnp.mean(err * err, axis=-1, keepdims=True), axis=0, keepdims=True)

    def body(x_ref, w_ref, t_ref, dx_ref, dw_ref, loss_ref):
        @pl.when(pl.program_id(0) == 0)
        def _():
            dw_ref[...] = jnp.zeros_like(dw_ref)
            loss_ref[...] = jnp.zeros_like(loss_ref)

        loss, vjp = jax.vjp(fwd, x_ref[...], w_ref[...], t_ref[...])
        dx, dw, _ = vjp(jnp.ones((1, 1), F32))
        dx_ref[...] = dx
        dw_ref[...] += dw
        loss_ref[...] += jnp.broadcast_to(loss, loss_ref.shape)

    return pl.pallas_call(
        body, name="loss_head", grid=(t // NORM_ROWS,),
        in_specs=[pl.BlockSpec((NORM_ROWS, d), lambda i: (i, 0)), _full_spec((1, d)),
                  pl.BlockSpec((NORM_ROWS, d), lambda i: (i, 0))],
        out_specs=[pl.BlockSpec((NORM_ROWS, d), lambda i: (i, 0)), _full_spec((1, d)), _full_spec((8, 128))],
        out_shape=[jax.ShapeDtypeStruct((t, d), F32), jax.ShapeDtypeStruct((1, d), F32),
                   jax.ShapeDtypeStruct((8, 128), F32)],
        compiler_params=_cparams(("arbitrary",)),
    )(x, w, target)


def _pad_w_in(w):
    z = lambda n: jnp.zeros((w.shape[0], n), w.dtype)
    return jnp.concatenate([w[:, 0:2048], w[:, 2056:2312], w[:, 2312:3080], w[:, 3096:3352],
                            w[:, 2048:2056], z(120), w[:, 3080:3096], z(112)], axis=1)


def _unpad_w_in(wp):
    return jnp.concatenate([wp[:, 0:2048], wp[:, 3328:3336], wp[:, 2048:2304], wp[:, 2304:3072],
                            wp[:, 3456:3472], wp[:, 3072:3328]], axis=1)


def _pad_rows(a, rows):
    return jnp.concatenate([a, jnp.zeros((rows - a.shape[0],) + a.shape[1:], a.dtype)], axis=0)


def _pad_lanes(a, lanes):
    return jnp.concatenate([a, jnp.zeros(a.shape[:-1] + (lanes - a.shape[-1],), a.dtype)], axis=-1)


def _layer_params(l, small):
    dn_cw = small["dn_conv_w"][l]
    return dict(
        ln_w=small["sgu_ln_w"][l][None], ln_b=small["sgu_ln_b"][l][None],
        ws=small["sgu_w_spatial"][l], bs_t=_pad_lanes(small["sgu_b_spatial"][l].T, 128),
        sc_cw=_pad_rows(small["sc_conv_w"][l], HALO),
        dn_cw=jnp.stack([_pad_rows(dn_cw[:, j * GROUP:(j + 1) * GROUP], HALO) for j in range(3)]),
        dn_al=_pad_lanes(small["dn_a_log"][l][None], 128), dn_dt=_pad_lanes(small["dn_dt_bias"][l][None], 128),
        dn_nw=jnp.tile(small["dn_norm_w"][l][None], (1, HEADS)),
        gla_w2=_pad_rows(small["gla_w_gate2"][l], 128), gla_gb=small["gla_gate_bias"][l][None],
        gla_nw=jnp.tile(small["gla_norm_w"][l][None], (1, HEADS)),
    )


def _exchange_piece(name, grad):
    if grad.ndim == 4:
        return grad
    if name == "w_in":
        grad = _unpad_w_in(grad)
    if name in ("w_in", "w_gate_up"):
        r, c4 = grad.shape
        return jnp.transpose(grad.reshape(2, r // 2, N_CHIPS, c4 // N_CHIPS), (0, 2, 1, 3))
    r4, c = grad.shape
    return jnp.transpose(grad.reshape(N_CHIPS, 2, r4 // (2 * N_CHIPS), c), (1, 0, 2, 3))


def _reduce_on_chip(pieces):
    return _pair_add(pieces, _sibling_swap(pieces))


def _local_step(x, target, big, small, late_weights=None, exchange=False, small_extra=None):
    saved = []
    h = x
    h1 = _rmsnorm_fwd(h, small["norm1_w"][0][None], "norm1_fwd")
    for l in range(DEPTH):
        lp = _layer_params(l, small)
        p = _matmul(h1, big["w_in"][l], "nn", F32, "proj_in")
        y_a = _sgu_fwd(p, lp["ln_w"], lp["ln_b"], lp["ws"], lp["bs_t"])
        y_b = _sc_fwd(p, lp["sc_cw"])
        host1 = host2 = None
        if late_weights is not None:
            shards, finish = late_weights[l]
            modes = ["layer"] * len(shards)
            host1 = (1, shards, modes, _exchange_buffers(shards, modes))
        (y_c, st_c, inv_c), ex = _dn_fwd(p, lp["dn_cw"], lp["dn_al"], lp["dn_dt"], lp["dn_nw"], hosted=host1)
        if host1 is not None:
            host2 = (2, None, None, ex)
        (y_d, st_d), ex = _gla_fwd(p, lp["gla_w2"], lp["gla_gb"], lp["gla_nw"], hosted=host2)
        if host2 is not None:
            big = finish(big, ex)
        mix = [y_a, y_b, y_c, y_d]
        x1, h2 = _matmul(mix, big["w_out"][l], "nn", F32, "proj_out", res=h, norm_w=small["norm2_w"][l][None])
        gate, up, act = _ffn_up_swiglu(h2, big["w_gate_up"][l])
        if l + 1 < DEPTH:
            x2, h1_next = _matmul(act, big["w_down"][l], "nn", F32, "ffn_down", res=x1,
                                  norm_w=small["norm1_w"][l + 1][None])
        else:
            x2, h1_next = _matmul(act, big["w_down"][l], "nn", F32, "ffn_down", res=x1), None
        saved.append(dict(x0=h, h1=h1, p=p, st_c=st_c, inv_c=inv_c, st_d=st_d, mix=mix, x1=x1, h2=h2, gate=gate,
                          up=up, act=act, lp=lp))
        h, h1 = x2, h1_next

    dx, d_final, loss = _loss_head(h, small["final_norm_w"][None], target)
    gbig = {k: [None] * DEPTH for k in ("w_in", "w_out", "w_gate_up", "w_down")}
    gs = {k: [None] * DEPTH for k in ("norm1_w", "sgu_ln_w", "sgu_ln_b", "sgu_w_spatial", "sgu_b_spatial", "sc_conv_w",
                                     "dn_conv_w", "dn_a_log", "dn_dt_bias", "dn_norm_w", "gla_w_gate2",
                                     "gla_gate_bias", "gla_norm_w", "norm2_w")}
    carry = []
    contribs = {}
    for l in reversed(range(DEPTH)):
        s = saved[l]
        lp = s["lp"]
        gbig["w_down"][l] = _matmul(s["act"], dx, "tn", GRAD_WIRE_DTYPE, "ffn_down_dw")
        dgu = _ffn_down_dx_swiglu(dx, big["w_down"][l], s["gate"], s["up"])
        gbig["w_gate_up"][l] = _matmul(s["h2"], dgu, "tn", GRAD_WIRE_DTYPE, "ffn_up_dw",
                                       pieces="cols" if exchange else None)
        dx1, gs["norm2_w"][l] = _matmul_nt_norm_bwd(dgu, big["w_gate_up"][l], s["x1"], small["norm2_w"][l][None], dx,
                                                    "ffn_up_dx")
        gbig["w_out"][l] = _matmul(s["mix"], dx1, "tn", GRAD_WIRE_DTYPE, "proj_out_dw")
        dmix = _matmul(dx1, big["w_out"][l], "nt", F32, "proj_out_dx")
        p = s["p"]
        dpu, dpv, g_lw, g_lb, g_ws, g_bs = _sgu_bwd(p, dmix, lp["ln_w"], lp["ln_b"], lp["ws"], lp["bs_t"])
        dpb, dpc, dph, g_sc = _sc_bwd(p, dmix, lp["sc_cw"])
        host1 = host2 = None
        if exchange:
            unit = carry + [(n, l, _exchange_piece(n, gbig[n][l])) for n in ("w_out", "w_gate_up", "w_down")]
            carry = []
            summed = _reduce_on_chip([piece for _, _, piece in unit])
            modes = ["piece"] * len(summed)
            host1 = (1, summed, modes, _exchange_buffers(summed, modes))
        (dcq, dck, dcv, dcs, dcz, g_dcw, g_al, g_dt, g_dnw), ex = _dn_bwd(
            p, dmix, s["st_c"], s["inv_c"], lp["dn_cw"], lp["dn_al"], lp["dn_dt"], lp["dn_nw"], hosted=host1)
        if host1 is not None:
            host2 = (2, None, None, ex)
        (ddq, ddk, ddv, dds, ddz, g_w2, g_gb, g_gnw), ex = _gla_bwd(p, dmix, s["st_d"], lp["gla_w2"], lp["gla_gb"],
                                                                   lp["gla_nw"], hosted=host2)
        if host2 is not None:
            for (n, lay, _), got in zip(unit, ex):
                contribs[(n, lay)] = got
        dp = jnp.concatenate([dpu, dpv, dpb, dpc, dph, dcq, dck, dcv, dcz, ddq, ddk, ddv, ddz, dcs, dds], axis=1)
        gbig["w_in"][l] = _matmul(s["h1"], dp, "tn", GRAD_WIRE_DTYPE, "proj_in_dw")
        dx, gs["norm1_w"][l] = _matmul_nt_norm_bwd(dp, big["w_in"][l], s["x0"], small["norm1_w"][l][None], dx1,
                                                   "proj_in_dx")
        gs["sgu_ln_w"][l], gs["sgu_ln_b"][l] = g_lw[0], g_lb[0]
        gs["sgu_w_spatial"][l] = g_ws
        gs["sgu_b_spatial"][l] = g_bs[:, :HEADS].T
        gs["sc_conv_w"][l] = g_sc[:3]
        gs["dn_conv_w"][l] = jnp.concatenate([g_dcw[0, :4], g_dcw[1, :4], g_dcw[2, :4]], axis=1)
        gs["dn_a_log"][l], gs["dn_dt_bias"][l] = g_al[0, :HEADS], g_dt[0, :HEADS]
        gs["dn_norm_w"][l] = jnp.sum(g_dnw.reshape(HEADS, HEAD_DIM), axis=0)
        gs["gla_w_gate2"][l] = g_w2[:16]
        gs["gla_gate_bias"][l] = g_gb[0]
        gs["gla_norm_w"][l] = jnp.sum(g_gnw.reshape(HEADS, HEAD_DIM), axis=0)
        gs["norm1_w"][l] = gs["norm1_w"][l][0]
        gs["norm2_w"][l] = gs["norm2_w"][l][0]
        if exchange:
            carry = [("w_in", l, _exchange_piece("w_in", gbig["w_in"][l]))]
    gsmall = {k: jnp.stack(v) for k, v in gs.items()}
    gsmall["final_norm_w"] = d_final[0]
    if not exchange:
        return loss, dx, gbig, gsmall
    summed = _reduce_on_chip([piece for _, _, piece in carry])
    last = _chip_exchange(summed + [small_extra(gsmall, loss)], ["piece"] * len(summed) + ["whole"], "exchange_grads")
    for (n, lay, _), got in zip(carry, last):
        contribs[(n, lay)] = got
    return loss, dx, contribs, last[-1]


def _peer_chips(x, y):
    return [(1 - x, y, 2 * (1 - x) + y), (x, 1 - y, 2 * x + 1 - y), (1 - x, 1 - y, 2 * (1 - x) + 1 - y)]


def _chip_exchange(arrays, modes, name):
    na = len(arrays)
    bufs = _exchange_buffers(arrays, modes)

    def body(*refs):
        ins, outs = refs[:na], refs[2 * na:3 * na]
        send1, recv1, send2, recv2 = refs[3 * na:]
        _exchange_stage1(ins, outs, modes, send1, recv1, "start")
        _exchange_stage1(ins, outs, modes, send1, recv1, "wait")
        _exchange_stage2(outs, send2, recv2, "start")
        _exchange_stage2(outs, send2, recv2, "wait")

    any_spec = pl.BlockSpec(memory_space=pl.ANY)
    return pl.pallas_call(
        body, name=name,
        in_specs=[any_spec] * (2 * na), out_specs=[any_spec] * na,
        out_shape=[jax.ShapeDtypeStruct(b.shape, b.dtype) for b in bufs],
        input_output_aliases={na + a: a for a in range(na)},
        scratch_shapes=_stage1_sems(na) + _stage2_sems(na),
    )(*arrays, *bufs)


def _exchange_buffers(arrays, modes):
    c_idx = lax.axis_index("c")
    chip = 2 * lax.axis_index("x") + lax.axis_index("y")
    units = []
    for arr, md in zip(arrays, modes):
        if md == "layer":
            units.append(lax.dynamic_index_in_dim(arr, c_idx, 0, keepdims=False))
        elif md == "piece":
            units.append(lax.dynamic_index_in_dim(arr, chip, 0, keepdims=False))
        else:
            units.append(arr)
    any_spec = pl.BlockSpec(memory_space=pl.ANY)
    bufs = pl.pallas_call(
        lambda *refs: None, name="exchange_alloc", out_specs=[any_spec] * len(units),
        out_shape=[jax.ShapeDtypeStruct((2, N_CHIPS) + u.shape, u.dtype) for u in units],
    )()
    return [lax.dynamic_update_slice(buf, u[None, None], (c_idx, chip) + (0,) * u.ndim) for buf, u in zip(bufs, units)]


def _stage1_sems(na):
    return [pltpu.SemaphoreType.DMA((na, 3)), pltpu.SemaphoreType.DMA((na, 3))]


def _stage2_sems(na):
    return [pltpu.SemaphoreType.DMA((na,)), pltpu.SemaphoreType.DMA((na,))]


def _exchange_stage1(ins, outs, modes, send1, recv1, what):
    x, y, c = lax.axis_index("x"), lax.axis_index("y"), lax.axis_index("c")
    me = 2 * x + y
    for a in range(len(ins)):
        for k, (px, py, pidx) in enumerate(_peer_chips(x, y)):
            if modes[a] == "layer":
                src = ins[a].at[c]
            else:
                src = ins[a].at[pidx] if modes[a] == "piece" else ins[a]
            if what == "start":
                pltpu.make_async_remote_copy(
                    src_ref=src, dst_ref=outs[a].at[c, me], send_sem=send1.at[a, k], recv_sem=recv1.at[a, k],
                    device_id=(px, py, c), device_id_type=MESH).start()
            else:
                cp = pltpu.make_async_remote_copy(
                    src_ref=src, dst_ref=outs[a].at[c, pidx], send_sem=send1.at[a, k], recv_sem=recv1.at[a, k],
                    device_id=(px, py, c), device_id_type=MESH)
                cp.wait_send()
                cp.wait_recv()


def _exchange_stage2(outs, send2, recv2, what):
    x, y, c = lax.axis_index("x"), lax.axis_index("y"), lax.axis_index("c")
    sibling = (x, y, 1 - c)
    for a in range(len(outs)):
        if what == "start":
            pltpu.make_async_remote_copy(
                src_ref=outs[a].at[c], dst_ref=outs[a].at[c], send_sem=send2.at[a], recv_sem=recv2.at[a],
                device_id=sibling, device_id_type=MESH).start()
        else:
            cp = pltpu.make_async_remote_copy(
                src_ref=outs[a].at[c], dst_ref=outs[a].at[1 - c], send_sem=send2.at[a], recv_sem=recv2.at[a],
                device_id=sibling, device_id_type=MESH)
            cp.wait_send()
            cp.wait_recv()


def _sibling_swap(arrays):
    na = len(arrays)

    def body(*refs):
        ins, theirs = refs[:na], refs[na:2 * na]
        send_sems, recv_sems = refs[2 * na:]
        x, y, c = lax.axis_index("x"), lax.axis_index("y"), lax.axis_index("c")
        sibling = (x, y, 1 - c)
        for a in range(na):
            pltpu.make_async_remote_copy(
                src_ref=ins[a].at[1 - c], dst_ref=theirs[a], send_sem=send_sems.at[a], recv_sem=recv_sems.at[a],
                device_id=sibling, device_id_type=MESH).start()
        for a in range(na):
            cp = pltpu.make_async_remote_copy(
                src_ref=ins[a].at[1 - c], dst_ref=theirs[a], send_sem=send_sems.at[a], recv_sem=recv_sems.at[a],
                device_id=sibling, device_id_type=MESH)
            cp.wait_send()
            cp.wait_recv()

    any_spec = pl.BlockSpec(memory_space=pl.ANY)
    return pl.pallas_call(
        body, name="sibling_swap",
        in_specs=[any_spec] * na, out_specs=[any_spec] * na,
        out_shape=[jax.ShapeDtypeStruct(s.shape[1:], s.dtype) for s in arrays],
        scratch_shapes=[pltpu.SemaphoreType.DMA((na,)), pltpu.SemaphoreType.DMA((na,))],
    )(*arrays)


PAIR_ADD_STEPS = 8


def _pair_add(boths, theirs):
    na = len(boths)
    core = lax.axis_index("c").astype(jnp.int32).reshape(1)
    flat_b = [b.reshape(2, b.shape[1] * b.shape[2], b.shape[3]) for b in boths]
    flat_t = [t.reshape(t.shape[0] * t.shape[1], t.shape[2]) for t in theirs]
    rows = [t.shape[0] // PAIR_ADD_STEPS for t in flat_t]

    def body(core_ref, *refs):
        for a in range(na):
            refs[2 * na + a][...] = (refs[a][...].astype(F32) + refs[na + a][...].astype(F32)).astype(
                refs[2 * na + a].dtype)

    own = [pl.BlockSpec((None, r, t.shape[1]), lambda i, core_ref: (core_ref[0], i, 0)) for r, t in zip(rows, flat_t)]
    plain = [pl.BlockSpec((r, t.shape[1]), lambda i, core_ref: (i, 0)) for r, t in zip(rows, flat_t)]
    outs = pl.pallas_call(
        body, name="pair_add",
        grid_spec=pltpu.PrefetchScalarGridSpec(
            num_scalar_prefetch=1, grid=(PAIR_ADD_STEPS,), in_specs=own + plain, out_specs=plain),
        out_shape=[jax.ShapeDtypeStruct(t.shape, t.dtype) for t in flat_t],
        compiler_params=_cparams(("parallel",)),
    )(core, *flat_b, *flat_t)
    return [o.reshape(t.shape) for o, t in zip(outs, theirs)]


def _adamw_math(g, w, m, v):
    m2 = ADAM_B1 * m + (1.0 - ADAM_B1) * g
    v2 = ADAM_B2 * v + (1.0 - ADAM_B2) * (g * g)
    m_hat = m2 / (1.0 - ADAM_B1 ** ADAM_STEP)
    v_hat = v2 / (1.0 - ADAM_B2 ** ADAM_STEP)
    delta = -ADAM_LR * (m_hat / (jnp.sqrt(v_hat) + ADAM_EPS) + ADAM_WD * w)
    return delta, m2, v2


def _adamw_big(contrib0, contrib1, w, m, v, name):
    _, r, c = w.shape
    rh = r // 2
    tr = _pick_tile(rh, (256, 176, 128, 64, 8))
    nj = rh // tr
    blk = pl.BlockSpec((1, tr, c), lambda l, h, j: (l, h * nj + j, 0))

    def contrib_spec(layer, parked_h, parked_j):
        return pl.BlockSpec(
            (1, N_CHIPS, tr, c),
            lambda l, h, j: (jnp.where(l == layer, h, parked_h), 0, jnp.where(l == layer, j, parked_j), 0))

    def body(g0_ref, g1_ref, w_ref, m_ref, v_ref, go_ref, d_ref, mo_ref, vo_ref):
        def chip_sum(ref):
            g = ref[0, 0].astype(F32)
            for s in range(1, N_CHIPS):
                g = g + ref[0, s].astype(F32)
            return g

        g = jnp.where(pl.program_id(0) == 0, chip_sum(g0_ref), chip_sum(g1_ref))
        delta, m2, v2 = _adamw_math(g, w_ref[0], m_ref[0], v_ref[0])
        go_ref[0] = g
        d_ref[0] = delta
        mo_ref[0] = m2
        vo_ref[0] = v2

    return pl.pallas_call(
        body, name=name, grid=(2, 2, nj),
        in_specs=[contrib_spec(0, 1, nj - 1), contrib_spec(1, 0, 0), blk, blk, blk],
        out_specs=[blk] * 4, out_shape=[jax.ShapeDtypeStruct(w.shape, F32)] * 4,
        compiler_params=_cparams(("arbitrary", "arbitrary", "arbitrary")),
    )(contrib0, contrib1, w, m, v)


def _sum_small(contrib):
    rows = contrib.shape[2]

    def body(g_ref, o_ref):
        total = g_ref[0, 0]
        for j in range(1, N_DEV):
            total = total + g_ref[j // N_CHIPS, j % N_CHIPS]
        o_ref[...] = total

    return pl.pallas_call(
        body, name="sum_small", out_shape=jax.ShapeDtypeStruct((rows, 128), F32),
        compiler_params=_cparams(),
    )(contrib)


def _adamw_small(gs, ws, ms, vs):
    n = len(gs)
    as2d = lambda a: a.reshape(1, -1) if a.ndim == 1 else a

    def body(*refs):
        g_refs, w_refs, m_refs, v_refs = refs[:n], refs[n:2 * n], refs[2 * n:3 * n], refs[3 * n:4 * n]
        d_refs, mo_refs, vo_refs = refs[4 * n:5 * n], refs[5 * n:6 * n], refs[6 * n:]
        for j in range(n):
            delta, m2, v2 = _adamw_math(g_refs[j][...], w_refs[j][...], m_refs[j][...], v_refs[j][...])
            d_refs[j][...] = delta
            mo_refs[j][...] = m2
            vo_refs[j][...] = v2

    ins = [as2d(a) for a in (*gs, *ws, *ms, *vs)]
    outs = pl.pallas_call(
        body, name="adamw_small", out_shape=[jax.ShapeDtypeStruct(a.shape, F32) for a in ins[:n]] * 3,
        compiler_params=_cparams(),
    )(*ins)
    back = lambda group: [o.reshape(g.shape) for o, g in zip(group, gs)]
    return back(outs[:n]), back(outs[n:2 * n]), back(outs[2 * n:])


PACK_ALIGN = 8 * 128


def _packed_rows(shape):
    n = 1
    for d in shape:
        n *= d
    return (n + PACK_ALIGN - 1) // PACK_ALIGN * 8


def _pack(arrays):
    parts = []
    for a in arrays:
        flat = a.reshape(-1)
        pad = _packed_rows(a.shape) * 128 - flat.shape[0]
        if pad:
            flat = jnp.concatenate([flat, jnp.zeros((pad,), F32)])
        parts.append(flat.reshape(-1, 128))
    return jnp.concatenate(parts, axis=0)


def _unpack(packed, shapes):
    out, row = [], 0
    for s in shapes:
        rows = _packed_rows(s)
        n = 1
        for d in s:
            n *= d
        out.append(packed[row:row + rows].reshape(-1)[:n].reshape(s))
        row += rows
    return out


SMALL_NAMES = ("norm1_w", "sgu_ln_w", "sgu_ln_b", "sgu_w_spatial", "sgu_b_spatial", "sc_conv_w", "dn_conv_w",
               "dn_a_log", "dn_dt_bias", "dn_norm_w", "gla_w_gate2", "gla_gate_bias", "gla_norm_w", "norm2_w",
               "final_norm_w")
SHARDED_SMALL = ("sc_conv_w", "dn_conv_w", "gla_w_gate2")
BIG_NAMES = ("w_in", "w_out", "w_gate_up", "w_down")
WEIGHT_ORDER = ("norm1_w", "w_in", "sgu_ln_w", "sgu_ln_b", "sgu_w_spatial", "sgu_b_spatial", "sc_conv_w", "dn_conv_w",
                "dn_a_log", "dn_dt_bias", "dn_norm_w", "gla_w_gate2", "gla_gate_bias", "gla_norm_w", "w_out",
                "norm2_w", "w_gate_up", "w_down", "final_norm_w")


def _cols_from_shards(g):
    l, n, r, c = g.shape
    return jnp.transpose(g, (0, 2, 1, 3)).reshape(l, r, n * c)


def kernel(x, norm1_w, w_in, sgu_ln_w, sgu_ln_b, sgu_w_spatial, sgu_b_spatial, sc_conv_w, dn_conv_w, dn_a_log, dn_dt_bias, dn_norm_w, gla_w_gate2, gla_gate_bias, gla_norm_w, w_out, norm2_w, w_gate_up, w_down, final_norm_w, loss_target, m_norm1_w, m_w_in, m_sgu_ln_w, m_sgu_ln_b, m_sgu_w_spatial, m_sgu_b_spatial, m_sc_conv_w, m_dn_conv_w, m_dn_a_log, m_dn_dt_bias, m_dn_norm_w, m_gla_w_gate2, m_gla_gate_bias, m_gla_norm_w, m_w_out, m_norm2_w, m_w_gate_up, m_w_down, m_final_norm_w, v_norm1_w, v_w_in, v_sgu_ln_w, v_sgu_ln_b, v_sgu_w_spatial, v_sgu_b_spatial, v_sc_conv_w, v_dn_conv_w, v_dn_a_log, v_dn_dt_bias, v_dn_norm_w, v_gla_w_gate2, v_gla_gate_bias, v_gla_norm_w, v_w_out, v_norm2_w, v_w_gate_up, v_w_down, v_final_norm_w):
    w = dict(norm1_w=norm1_w, w_in=w_in, sgu_ln_w=sgu_ln_w, sgu_ln_b=sgu_ln_b, sgu_w_spatial=sgu_w_spatial,
             sgu_b_spatial=sgu_b_spatial, sc_conv_w=sc_conv_w, dn_conv_w=dn_conv_w, dn_a_log=dn_a_log,
             dn_dt_bias=dn_dt_bias, dn_norm_w=dn_norm_w, gla_w_gate2=gla_w_gate2, gla_gate_bias=gla_gate_bias,
             gla_norm_w=gla_norm_w, w_out=w_out, norm2_w=norm2_w, w_gate_up=w_gate_up, w_down=w_down,
             final_norm_w=final_norm_w)
    m = dict(norm1_w=m_norm1_w, w_in=m_w_in, sgu_ln_w=m_sgu_ln_w, sgu_ln_b=m_sgu_ln_b, sgu_w_spatial=m_sgu_w_spatial,
             sgu_b_spatial=m_sgu_b_spatial, sc_conv_w=m_sc_conv_w, dn_conv_w=m_dn_conv_w, dn_a_log=m_dn_a_log,
             dn_dt_bias=m_dn_dt_bias, dn_norm_w=m_dn_norm_w, gla_w_gate2=m_gla_w_gate2,
             gla_gate_bias=m_gla_gate_bias, gla_norm_w=m_gla_norm_w, w_out=m_w_out, norm2_w=m_norm2_w,
             w_gate_up=m_w_gate_up, w_down=m_w_down, final_norm_w=m_final_norm_w)
    v = dict(norm1_w=v_norm1_w, w_in=v_w_in, sgu_ln_w=v_sgu_ln_w, sgu_ln_b=v_sgu_ln_b, sgu_w_spatial=v_sgu_w_spatial,
             sgu_b_spatial=v_sgu_b_spatial, sc_conv_w=v_sc_conv_w, dn_conv_w=v_dn_conv_w, dn_a_log=v_dn_a_log,
             dn_dt_bias=v_dn_dt_bias, dn_norm_w=v_dn_norm_w, gla_w_gate2=v_gla_w_gate2,
             gla_gate_bias=v_gla_gate_bias, gla_norm_w=v_gla_norm_w, w_out=v_w_out, norm2_w=v_norm2_w,
             w_gate_up=v_w_gate_up, w_down=v_w_down, final_norm_w=v_final_norm_w)
    chip = 2 * lax.axis_index("x") + lax.axis_index("y")

    w_in_wire = w["w_in"].astype(MXU_DTYPE)
    row_halves = lambda a: a.reshape(2, a.shape[0] // 2, a.shape[1])

    def full_w_in(g):
        return _pad_w_in(jnp.transpose(g, (0, 2, 1, 3)).reshape(D_MODEL, IN_COLS))

    first = [row_halves(w_in_wire[0])] + [w[n] for n in SHARDED_SMALL]
    gathered = _chip_exchange(first, ["layer"] * len(first), "gather_first")
    w_in_0 = full_w_in(gathered[0])
    big = dict(w_in=[w_in_0, None])
    small = {n: w[n] for n in SMALL_NAMES if n not in SHARDED_SMALL}
    for j, n in enumerate(SHARDED_SMALL):
        small[n] = _cols_from_shards(gathered[1 + j])
    wire = {n: w[n].astype(MXU_DTYPE) for n in ("w_out", "w_gate_up", "w_down")}

    def cols_full(g):
        _, n, rh, c = g.shape
        return jnp.transpose(g, (0, 2, 1, 3)).reshape(2 * rh, n * c)

    def rows_full(g):
        _, n, rh, c = g.shape
        return jnp.transpose(g, (1, 0, 2, 3)).reshape(n * 2 * rh, c)

    def finish_0(big, g):
        return dict(w_in=[w_in_0, full_w_in(g[0])], w_out=[rows_full(g[1]), None],
                    w_gate_up=[cols_full(g[2]), None], w_down=[rows_full(g[3]), None])

    def finish_1(big, g):
        return dict(big, w_out=[big["w_out"][0], rows_full(g[0])], w_gate_up=[big["w_gate_up"][0], cols_full(g[1])],
                    w_down=[big["w_down"][0], rows_full(g[2])])

    late = [([row_halves(w_in_wire[1])] + [row_halves(wire[n][0]) for n in ("w_out", "w_gate_up", "w_down")], finish_0),
            ([row_halves(wire[n][1]) for n in ("w_out", "w_gate_up", "w_down")], finish_1)]

    small_shapes = [(DEPTH,) + w[n].shape[1:-1] + (w[n].shape[-1] * (N_CHIPS if n in SHARDED_SMALL else 1),)
                    if n != "final_norm_w" else w[n].shape for n in SMALL_NAMES] + [(1,)]

    def pack_small(gsmall, loss_tile):
        return _pack([gsmall[n] for n in SMALL_NAMES] + [loss_tile[0:1, 0]])

    _, grad_x, contribs, small_contrib = _local_step(
        x[0], loss_target[0], big, small, late_weights=late, exchange=True, small_extra=pack_small)

    out_g, out_d, out_m, out_v = {}, {}, {}, {}
    for j, n in enumerate(BIG_NAMES):
        out_g[n], out_d[n], out_m[n], out_v[n] = _adamw_big(contribs[(n, 0)], contribs[(n, 1)], w[n], m[n], v[n],
                                                            "adamw_" + n)
    summed = _unpack(_sum_small(small_contrib), small_shapes)
    loss = summed[-1][0]
    for n, g in zip(SMALL_NAMES, summed[:-1]):
        if n in SHARDED_SMALL:
            cols = g.shape[-1] // N_CHIPS
            g = lax.dynamic_slice_in_dim(g, chip * cols, cols, axis=g.ndim - 1)
        out_g[n] = g
    d_s, m_s, v_s = _adamw_small([out_g[n] for n in SMALL_NAMES], [w[n] for n in SMALL_NAMES],
                                 [m[n] for n in SMALL_NAMES], [v[n] for n in SMALL_NAMES])
    for n, d_, m_, v_ in zip(SMALL_NAMES, d_s, m_s, v_s):
        out_d[n], out_m[n], out_v[n] = d_, m_, v_

    return (loss, grad_x[None], *[out_g[n] for n in WEIGHT_ORDER], *[out_d[n] for n in WEIGHT_ORDER],
            *[out_m[n] for n in WEIGHT_ORDER], *[out_v[n] for n in WEIGHT_ORDER])
```
